```python
import jax, jax.numpy as jnp
from jax import lax
import numpy as np

D_MODEL = 1024
BATCH = 32
SEQ = 2048
DEPTH = 2

GRID_W = 64
CTX_LEN = 256
HEAD_DIM = 64
ATTN_W = D_MODEL // 2
CONV_W = D_MODEL // 4
POOL_W = D_MODEL // 4
MIX_W = ATTN_W + CONV_W + POOL_W
ATTN_HEADS = ATTN_W // HEAD_DIM
KV_HEADS = ATTN_HEADS // 4
KV_W = KV_HEADS * HEAD_DIM
IN_W = ATTN_W + 2 * KV_W + 2 * CONV_W + POOL_W
WINDOW = 128
Q_BLOCK = 128
SPAN = Q_BLOCK + 2 * WINDOW
CONV_KERNEL = 31
POOL_WINDOWS = (2, 4, 8, 16)
POOL_GROUP = POOL_W // len(POOL_WINDOWS)
ROPE_BASE = 10000.0
D_FF = -(-8 * D_MODEL // (3 * 256)) * 256
EPS = 1e-6
NEG = -1e30

kernel_name = "hybrid_parallel_groups_dit_block"


def rms_norm(x, g):
    xf = x.astype(jnp.float32)
    y = xf * lax.rsqrt(jnp.mean(xf * xf, axis=-1, keepdims=True) + EPS)
    return (y * g.astype(jnp.float32)).astype(x.dtype)


def axial_rope_tables(n, dtype):
    rows = n // GRID_W
    row = jnp.repeat(jnp.arange(rows), GRID_W).astype(jnp.float32)
    col = jnp.tile(jnp.arange(GRID_W), rows).astype(jnp.float32)
    half = HEAD_DIM // 2
    inv = ROPE_BASE ** (-jnp.arange(0, half, 2, dtype=jnp.float32) / half)
    ar = row[:, None] * inv
    ac = col[:, None] * inv
    ang = jnp.concatenate([ar, ar, ac, ac], axis=-1)
    return jnp.cos(ang).astype(dtype), jnp.sin(ang).astype(dtype)


def apply_rope(x, cos, sin):
    xr = x.reshape(*x.shape[:-1], 2, 2, HEAD_DIM // 4)
    rot = jnp.stack([-xr[..., 1, :], xr[..., 0, :]], axis=-2).reshape(x.shape)
    return x * cos[:, None, :] + rot * sin[:, None, :]


def split_in(u):
    b, n, _ = u.shape
    q, k, v, cu, pu = jnp.split(
        u, [ATTN_W, ATTN_W + KV_W, ATTN_W + 2 * KV_W, ATTN_W + 2 * KV_W + 2 * CONV_W], axis=-1)
    return (q.reshape(b, n, ATTN_HEADS, HEAD_DIM), k.reshape(b, n, KV_HEADS, HEAD_DIM),
            v.reshape(b, n, KV_HEADS, HEAD_DIM), cu, pu)


def window_attention(q, k, v, k_ctx, v_ctx, sink):
    b, n, h, hd = q.shape
    kvh = k.shape[2]
    grp = h // kvh
    n_ctx = k_ctx.shape[1]
    nb = n // Q_BLOCK
    scale = HEAD_DIM ** -0.5
    pad = ((0, 0), (WINDOW, WINDOW), (0, 0), (0, 0))
    k_pad = jnp.pad(k, pad)
    v_pad = jnp.pad(v, pad)
    sink_b = jnp.broadcast_to(sink.astype(jnp.float32).reshape(1, kvh, grp, 1, 1), (b, kvh, grp, Q_BLOCK, 1))

    def one_block(i):
        start = i * Q_BLOCK
        qb = lax.dynamic_slice_in_dim(q, start, Q_BLOCK, axis=1).reshape(b, Q_BLOCK, kvh, grp, hd)
        kb = lax.dynamic_slice_in_dim(k_pad, start, SPAN, axis=1)
        vb = lax.dynamic_slice_in_dim(v_pad, start, SPAN, axis=1)
        qpos = start + jnp.arange(Q_BLOCK)
        kpos = start - WINDOW + jnp.arange(SPAN)
        valid = ((jnp.abs(qpos[:, None] - kpos[None, :]) <= WINDOW)
                 & (kpos >= 0)[None, :] & (kpos < n)[None, :])
        s_loc = jnp.einsum('bqkgd,bjkd->bkgqj', qb, kb).astype(jnp.float32) * scale
        s_loc = jnp.where(valid, s_loc, NEG)
        s_ctx = jnp.einsum('bqkgd,bckd->bkgqc', qb, k_ctx).astype(jnp.float32) * scale
        p = jax.nn.softmax(jnp.concatenate([sink_b, s_ctx, s_loc], axis=-1), axis=-1).astype(v.dtype)
        o = (jnp.einsum('bkgqc,bckd->bqkgd', p[..., 1:1 + n_ctx], v_ctx)
             + jnp.einsum('bkgqj,bjkd->bqkgd', p[..., 1 + n_ctx:], vb))
        return o.reshape(b, Q_BLOCK, h * hd)

    o = lax.map(one_block, jnp.arange(nb))
    return jnp.moveaxis(o, 0, 1).reshape(b, n, h * hd)


def context_attention(qc, kc, vc, sink):
    b, n_ctx, h, hd = qc.shape
    kvh = kc.shape[2]
    grp = h // kvh
    qg = qc.reshape(b, n_ctx, kvh, grp, hd)
    s = jnp.einsum('bqkgd,bckd->bkgqc', qg, kc).astype(jnp.float32) * (HEAD_DIM ** -0.5)
    sink_b = jnp.broadcast_to(sink.astype(jnp.float32).reshape(1, kvh, grp, 1, 1), (b, kvh, grp, n_ctx, 1))
    p = jax.nn.softmax(jnp.concatenate([sink_b, s], axis=-1), axis=-1).astype(vc.dtype)
    o = jnp.einsum('bkgqc,bckd->bqkgd', p[..., 1:], vc)
    return o.reshape(b, n_ctx, h * hd)


def conv_module(u, dw, dw_b, ln_g, ln_b):
    a, g = jnp.split(u, 2, axis=-1)
    h = a * jax.nn.sigmoid(g)
    h = lax.conv_general_dilated(
        h, dw[:, None, :], window_strides=(1,),
        padding=[(CONV_KERNEL // 2, CONV_KERNEL // 2)],
        dimension_numbers=('NWC', 'WIO', 'NWC'), feature_group_count=CONV_W) + dw_b
    hf = h.astype(jnp.float32)
    mu = jnp.mean(hf, axis=-1, keepdims=True)
    var = jnp.mean(jnp.square(hf - mu), axis=-1, keepdims=True)
    hn = (hf - mu) * lax.rsqrt(var + EPS) * ln_g.astype(jnp.float32) + ln_b.astype(jnp.float32)
    return jax.nn.silu(hn).astype(u.dtype)


def pool_mixer(p, w, scale):
    b, n, ch = p.shape
    t = jnp.arange(n)
    pf = p.astype(jnp.float32).reshape(b, n, len(POOL_WINDOWS), POOL_GROUP)
    cs = jnp.pad(jnp.cumsum(pf, axis=1), ((0, 0), (1, 0), (0, 0), (0, 0)))
    outs = []
    for gi, win in enumerate(POOL_WINDOWS):
        lo = jnp.maximum(t - win // 2, 0)
        hi = jnp.minimum(t + win - 1 - win // 2, n - 1)
        cg = cs[:, :, gi]
        mean = (cg[:, hi + 1] - cg[:, lo]) / (hi - lo + 1).astype(jnp.float32)[None, :, None]
        outs.append(mean - pf[:, :, gi])
    y = jnp.stack(outs, axis=2).astype(p.dtype)
    y = jnp.einsum('bsgc,gcd->bsgd', y, w).reshape(b, n, ch)
    return y * scale


def mixer_output(attn, cu, pu, w_out, dw, dw_b, ln_g, ln_b, pw, ps):
    conv = conv_module(cu, dw, dw_b, ln_g, ln_b)
    pool = pool_mixer(pu, pw, ps)
    return jnp.concatenate([attn, conv, pool], axis=-1) @ w_out


def swiglu(h, w_in, w_out):
    g, u = jnp.split(h @ w_in, 2, axis=-1)
    return (jax.nn.silu(g) * u) @ w_out


def _fwd_setup_inputs(seed: int = 0) -> dict:
    key = jax.random.key(seed)
    ks = jax.random.split(key, 24)
    f32 = jnp.float32
    nrm = lambda k, shape, s: jax.random.normal(k, shape, f32) * s
    return {
        "x": nrm(ks[0], (BATCH, SEQ, D_MODEL), 1.0),
        "c": nrm(ks[1], (BATCH, D_MODEL), 1.0),
        "ctx": nrm(ks[2], (BATCH, CTX_LEN, D_MODEL), 1.0),
        "c_ctx": nrm(ks[3], (D_MODEL,), 1.0),
        "w_mod": nrm(ks[4], (DEPTH, D_MODEL, 6 * D_MODEL), 0.5 * D_MODEL ** -0.5),
        "b_mod": nrm(ks[5], (DEPTH, 6 * D_MODEL), 0.01),
        "norm1_g": 1.0 + nrm(ks[6], (DEPTH, D_MODEL), 0.05),
        "norm2_g": 1.0 + nrm(ks[7], (DEPTH, D_MODEL), 0.05),
        "w_in": nrm(ks[8], (DEPTH, D_MODEL, IN_W), D_MODEL ** -0.5),
        "conv_dw": nrm(ks[9], (DEPTH, CONV_KERNEL, CONV_W), CONV_KERNEL ** -0.5),
        "conv_dw_b": nrm(ks[10], (DEPTH, CONV_W), 0.01),
        "conv_ln_g": 1.0 + nrm(ks[11], (DEPTH, CONV_W), 0.05),
        "conv_ln_b": nrm(ks[12], (DEPTH, CONV_W), 0.01),
        "attn_sink": nrm(ks[13], (DEPTH, ATTN_HEADS), 0.5),
        "pool_w": nrm(ks[14], (DEPTH, len(POOL_WINDOWS), POOL_GROUP, POOL_GROUP), POOL_GROUP ** -0.5),
        "pool_scale": 1.0 + nrm(ks[15], (DEPTH, POOL_W), 0.05),
        "w_out": nrm(ks[16], (DEPTH, MIX_W, D_MODEL), MIX_W ** -0.5),
        "w_ffn_in": nrm(ks[17], (DEPTH, D_MODEL, 2 * D_FF), D_MODEL ** -0.5),
        "w_ffn_out": nrm(ks[18], (DEPTH, D_FF, D_MODEL), D_FF ** -0.5),
        "final_g": 1.0 + nrm(ks[19], (D_MODEL,), 0.05),
    }


def _fwd_reference(x, c, ctx, c_ctx, w_mod, b_mod, norm1_g, norm2_g, w_in, conv_dw, conv_dw_b,
              conv_ln_g, conv_ln_b, attn_sink, pool_w, pool_scale, w_out, w_ffn_in, w_ffn_out, final_g):
    b, n, _ = x.shape
    n_ctx = ctx.shape[1]
    cos, sin = axial_rope_tables(n, x.dtype)
    cx = ctx
    for l in range(DEPTH):
        last = l == DEPTH - 1
        m = (jax.nn.silu(c) @ w_mod[l] + b_mod[l])[:, None, :]
        sh1, sc1, g1, sh2, sc2, g2 = jnp.split(m, 6, axis=-1)
        mc = jax.nn.silu(c_ctx) @ w_mod[l] + b_mod[l]
        csh1, csc1, cg1, csh2, csc2, cg2 = jnp.split(mc, 6)

        hl = rms_norm(x, norm1_g[l]) * (1.0 + sc1) + sh1
        hc = rms_norm(cx, norm1_g[l]) * (1.0 + csc1) + csh1
        q, k, v, cu, pu = split_in(hl @ w_in[l])
        if last:
            kvc = hc @ w_in[l][:, ATTN_W:ATTN_W + 2 * KV_W]
            kc, vc = [t.reshape(b, n_ctx, KV_HEADS, HEAD_DIM) for t in jnp.split(kvc, 2, axis=-1)]
        else:
            qc, kc, vc, cuc, puc = split_in(hc @ w_in[l])
        q = apply_rope(q, cos, sin)
        k = apply_rope(k, cos, sin)
        attn = window_attention(q, k, v, kc, vc, attn_sink[l])
        x = x + g1 * mixer_output(attn, cu, pu, w_out[l], conv_dw[l], conv_dw_b[l],
                                  conv_ln_g[l], conv_ln_b[l], pool_w[l], pool_scale[l])
        if not last:
            attn_c = context_attention(qc, kc, vc, attn_sink[l])
            cx = cx + cg1 * mixer_output(attn_c, cuc, puc, w_out[l], conv_dw[l], conv_dw_b[l],
                                         conv_ln_g[l], conv_ln_b[l], pool_w[l], pool_scale[l])

        x = x + g2 * swiglu(rms_norm(x, norm2_g[l]) * (1.0 + sc2) + sh2, w_ffn_in[l], w_ffn_out[l])
        if not last:
            cx = cx + cg2 * swiglu(rms_norm(cx, norm2_g[l]) * (1.0 + csc2) + csh2, w_ffn_in[l], w_ffn_out[l])
    return rms_norm(x, final_g)


import jax as _jax
import jax.numpy as _jnp

TWIN_FORMAT = 'train_step'
FWD_PARAMS = ['x', 'c', 'ctx', 'c_ctx', 'w_mod', 'b_mod', 'norm1_g', 'norm2_g', 'w_in', 'conv_dw', 'conv_dw_b', 'conv_ln_g', 'conv_ln_b', 'attn_sink', 'pool_w', 'pool_scale', 'w_out', 'w_ffn_in', 'w_ffn_out', 'final_g']
TWIN_WEIGHTS = ['c_ctx', 'w_mod', 'b_mod', 'norm1_g', 'norm2_g', 'w_in', 'conv_dw', 'conv_dw_b', 'conv_ln_g', 'conv_ln_b', 'attn_sink', 'pool_w', 'pool_scale', 'w_out', 'w_ffn_in', 'w_ffn_out', 'final_g']
TWIN_DIFF_INPUT = 'x'
TWIN_INPUTS = ['x', 'c', 'ctx', 'c_ctx', 'w_mod', 'b_mod', 'norm1_g', 'norm2_g', 'w_in', 'conv_dw', 'conv_dw_b', 'conv_ln_g', 'conv_ln_b', 'attn_sink', 'pool_w', 'pool_scale', 'w_out', 'w_ffn_in', 'w_ffn_out', 'final_g', 'loss_target', 'm_c_ctx', 'm_w_mod', 'm_b_mod', 'm_norm1_g', 'm_norm2_g', 'm_w_in', 'm_conv_dw', 'm_conv_dw_b', 'm_conv_ln_g', 'm_conv_ln_b', 'm_attn_sink', 'm_pool_w', 'm_pool_scale', 'm_w_out', 'm_w_ffn_in', 'm_w_ffn_out', 'm_final_g', 'v_c_ctx', 'v_w_mod', 'v_b_mod', 'v_norm1_g', 'v_norm2_g', 'v_w_in', 'v_conv_dw', 'v_conv_dw_b', 'v_conv_ln_g', 'v_conv_ln_b', 'v_attn_sink', 'v_pool_w', 'v_pool_scale', 'v_w_out', 'v_w_ffn_in', 'v_w_ffn_out', 'v_final_g']
TWIN_OUTPUTS = ['loss', 'grad_x', 'grad_c_ctx', 'grad_w_mod', 'grad_b_mod', 'grad_norm1_g', 'grad_norm2_g', 'grad_w_in', 'grad_conv_dw', 'grad_conv_dw_b', 'grad_conv_ln_g', 'grad_conv_ln_b', 'grad_attn_sink', 'grad_pool_w', 'grad_pool_scale', 'grad_w_out', 'grad_w_ffn_in', 'grad_w_ffn_out', 'grad_final_g', 'delta_c_ctx', 'delta_w_mod', 'delta_b_mod', 'delta_norm1_g', 'delta_norm2_g', 'delta_w_in', 'delta_conv_dw', 'delta_conv_dw_b', 'delta_conv_ln_g', 'delta_conv_ln_b', 'delta_attn_sink', 'delta_pool_w', 'delta_pool_scale', 'delta_w_out', 'delta_w_ffn_in', 'delta_w_ffn_out', 'delta_final_g', 'new_m_c_ctx', 'new_m_w_mod', 'new_m_b_mod', 'new_m_norm1_g', 'new_m_norm2_g', 'new_m_w_in', 'new_m_conv_dw', 'new_m_conv_dw_b', 'new_m_conv_ln_g', 'new_m_conv_ln_b', 'new_m_attn_sink', 'new_m_pool_w', 'new_m_pool_scale', 'new_m_w_out', 'new_m_w_ffn_in', 'new_m_w_ffn_out', 'new_m_final_g', 'new_v_c_ctx', 'new_v_w_mod', 'new_v_b_mod', 'new_v_norm1_g', 'new_v_norm2_g', 'new_v_w_in', 'new_v_conv_dw', 'new_v_conv_dw_b', 'new_v_conv_ln_g', 'new_v_conv_ln_b', 'new_v_attn_sink', 'new_v_pool_w', 'new_v_pool_scale', 'new_v_w_out', 'new_v_w_ffn_in', 'new_v_w_ffn_out', 'new_v_final_g']
TWIN_LEAF_KINDS = {'loss': 'loss', 'grad_x': 'grad_x', 'grad_c_ctx': 'grad_w', 'grad_w_mod': 'grad_w', 'grad_b_mod': 'grad_w', 'grad_norm1_g': 'grad_w', 'grad_norm2_g': 'grad_w', 'grad_w_in': 'grad_w', 'grad_conv_dw': 'grad_w', 'grad_conv_dw_b': 'grad_w', 'grad_conv_ln_g': 'grad_w', 'grad_conv_ln_b': 'grad_w', 'grad_attn_sink': 'grad_w', 'grad_pool_w': 'grad_w', 'grad_pool_scale': 'grad_w', 'grad_w_out': 'grad_w', 'grad_w_ffn_in': 'grad_w', 'grad_w_ffn_out': 'grad_w', 'grad_final_g': 'grad_w', 'delta_c_ctx': 'delta_w', 'delta_w_mod': 'delta_w', 'delta_b_mod': 'delta_w', 'delta_norm1_g': 'delta_w', 'delta_norm2_g': 'delta_w', 'delta_w_in': 'delta_w', 'delta_conv_dw': 'delta_w', 'delta_conv_dw_b': 'delta_w', 'delta_conv_ln_g': 'delta_w', 'delta_conv_ln_b': 'delta_w', 'delta_attn_sink': 'delta_w', 'delta_pool_w': 'delta_w', 'delta_pool_scale': 'delta_w', 'delta_w_out': 'delta_w', 'delta_w_ffn_in': 'delta_w', 'delta_w_ffn_out': 'delta_w', 'delta_final_g': 'delta_w', 'new_m_c_ctx': 'new_m', 'new_m_w_mod': 'new_m', 'new_m_b_mod': 'new_m', 'new_m_norm1_g': 'new_m', 'new_m_norm2_g': 'new_m', 'new_m_w_in': 'new_m', 'new_m_conv_dw': 'new_m', 'new_m_conv_dw_b': 'new_m', 'new_m_conv_ln_g': 'new_m', 'new_m_conv_ln_b': 'new_m', 'new_m_attn_sink': 'new_m', 'new_m_pool_w': 'new_m', 'new_m_pool_scale': 'new_m', 'new_m_w_out': 'new_m', 'new_m_w_ffn_in': 'new_m', 'new_m_w_ffn_out': 'new_m', 'new_m_final_g': 'new_m', 'new_v_c_ctx': 'new_v', 'new_v_w_mod': 'new_v', 'new_v_b_mod': 'new_v', 'new_v_norm1_g': 'new_v', 'new_v_norm2_g': 'new_v', 'new_v_w_in': 'new_v', 'new_v_conv_dw': 'new_v', 'new_v_conv_dw_b': 'new_v', 'new_v_conv_ln_g': 'new_v', 'new_v_conv_ln_b': 'new_v', 'new_v_attn_sink': 'new_v', 'new_v_pool_w': 'new_v', 'new_v_pool_scale': 'new_v', 'new_v_w_out': 'new_v', 'new_v_w_ffn_in': 'new_v', 'new_v_w_ffn_out': 'new_v', 'new_v_final_g': 'new_v'}


def _forward(args):
    return _fwd_reference(*[args[k] for k in FWD_PARAMS])


def _output_shape():
    out = _jax.eval_shape(lambda: _forward(_fwd_setup_inputs(0)))
    return out.shape, out.dtype

N_MICROBATCH = 1
ADAM_LR = 0.001
ADAM_B1 = 0.9
ADAM_B2 = 0.999
ADAM_EPS = 1e-08
ADAM_WD = 0.01
ADAM_STEP = 10
PER_EXAMPLE_BATCH_AXIS = {'x': 0, 'c': 0, 'ctx': 0, 'loss_target': 0}
SHARED_INPUTS = []
_WEIGHT_DTYPES = {'c_ctx': _jnp.float32, 'w_mod': _jnp.float32, 'b_mod': _jnp.float32, 'norm1_g': _jnp.float32, 'norm2_g': _jnp.float32, 'w_in': _jnp.float32, 'conv_dw': _jnp.float32, 'conv_dw_b': _jnp.float32, 'conv_ln_g': _jnp.float32, 'conv_ln_b': _jnp.float32, 'attn_sink': _jnp.float32, 'pool_w': _jnp.float32, 'pool_scale': _jnp.float32, 'w_out': _jnp.float32, 'w_ffn_in': _jnp.float32, 'w_ffn_out': _jnp.float32, 'final_g': _jnp.float32}
MOMENT_SCALE = {'c_ctx': 1.587519e-02, 'w_mod': 8.701546e-02, 'b_mod': 1.533405e-01, 'norm1_g': 4.437856e-02, 'norm2_g': 7.577422e-02, 'w_in': 3.865448e-02, 'conv_dw': 4.890557e-02, 'conv_dw_b': 9.631899e-02, 'conv_ln_g': 6.018342e-02, 'conv_ln_b': 5.940947e-02, 'attn_sink': 2.705150e-04, 'pool_w': 7.191907e-02, 'pool_scale': 7.228647e-02, 'w_out': 4.662048e-02, 'w_ffn_in': 3.339373e-02, 'w_ffn_out': 5.436585e-02, 'final_g': 6.397252e+01}


def _to_microbatches(a, axis):
    t = _jnp.moveaxis(a, axis, 0)
    t = t.reshape((N_MICROBATCH, t.shape[0] // N_MICROBATCH) + t.shape[1:])
    return _jnp.moveaxis(t, 1, axis + 1)


def setup_inputs(seed: int = 0) -> dict:
    inp = _fwd_setup_inputs(seed)
    key = _jax.random.fold_in(_jax.random.key(seed), 7919)
    shape, _ = _output_shape()
    out = dict(inp)
    out["loss_target"] = _jax.random.normal(_jax.random.fold_in(key, 0), shape, _jnp.float32)
    for i, name in enumerate(TWIN_WEIGHTS):
        w = inp[name].astype(_jnp.float32)
        if MOMENT_SCALE is None:
            s = _jnp.sqrt(_jnp.mean(_jnp.square(w)) + 1e-30)
        else:
            s = MOMENT_SCALE[name]
        km, kv = _jax.random.split(_jax.random.fold_in(key, i + 1))
        out[name] = w
        out["m_" + name] = s * _jax.random.normal(km, w.shape, _jnp.float32)
        out["v_" + name] = (s * s) * _jax.random.uniform(kv, w.shape, _jnp.float32, 0.5, 1.5)
    if N_MICROBATCH > 1:
        for name, axis in PER_EXAMPLE_BATCH_AXIS.items():
            out[name] = _to_microbatches(out[name], axis)
    return {'x': out['x'], 'c': out['c'], 'ctx': out['ctx'], 'c_ctx': out['c_ctx'], 'w_mod': out['w_mod'], 'b_mod': out['b_mod'], 'norm1_g': out['norm1_g'], 'norm2_g': out['norm2_g'], 'w_in': out['w_in'], 'conv_dw': out['conv_dw'], 'conv_dw_b': out['conv_dw_b'], 'conv_ln_g': out['conv_ln_g'], 'conv_ln_b': out['conv_ln_b'], 'attn_sink': out['attn_sink'], 'pool_w': out['pool_w'], 'pool_scale': out['pool_scale'], 'w_out': out['w_out'], 'w_ffn_in': out['w_ffn_in'], 'w_ffn_out': out['w_ffn_out'], 'final_g': out['final_g'], 'loss_target': out['loss_target'], 'm_c_ctx': out['m_c_ctx'], 'm_w_mod': out['m_w_mod'], 'm_b_mod': out['m_b_mod'], 'm_norm1_g': out['m_norm1_g'], 'm_norm2_g': out['m_norm2_g'], 'm_w_in': out['m_w_in'], 'm_conv_dw': out['m_conv_dw'], 'm_conv_dw_b': out['m_conv_dw_b'], 'm_conv_ln_g': out['m_conv_ln_g'], 'm_conv_ln_b': out['m_conv_ln_b'], 'm_attn_sink': out['m_attn_sink'], 'm_pool_w': out['m_pool_w'], 'm_pool_scale': out['m_pool_scale'], 'm_w_out': out['m_w_out'], 'm_w_ffn_in': out['m_w_ffn_in'], 'm_w_ffn_out': out['m_w_ffn_out'], 'm_final_g': out['m_final_g'], 'v_c_ctx': out['v_c_ctx'], 'v_w_mod': out['v_w_mod'], 'v_b_mod': out['v_b_mod'], 'v_norm1_g': out['v_norm1_g'], 'v_norm2_g': out['v_norm2_g'], 'v_w_in': out['v_w_in'], 'v_conv_dw': out['v_conv_dw'], 'v_conv_dw_b': out['v_conv_dw_b'], 'v_conv_ln_g': out['v_conv_ln_g'], 'v_conv_ln_b': out['v_conv_ln_b'], 'v_attn_sink': out['v_attn_sink'], 'v_pool_w': out['v_pool_w'], 'v_pool_scale': out['v_pool_scale'], 'v_w_out': out['v_w_out'], 'v_w_ffn_in': out['v_w_ffn_in'], 'v_w_ffn_out': out['v_w_ffn_out'], 'v_final_g': out['v_final_g']}


def _loss(weights, diff, rest, loss_target):
    with _jax.named_scope("forward"):
        args = {**rest, TWIN_DIFF_INPUT: diff, **{k: w.astype(_WEIGHT_DTYPES[k]) for k, w in weights.items()}}
        y = _forward(args)
    with _jax.named_scope("loss_head"):
        err = _jnp.square(y.astype(_jnp.float32) - loss_target)
        return 0.5 * _jnp.sum(_jnp.mean(err, axis=-1)) if err.ndim else 0.5 * err


def _adamw(w, g, m, v):
    m = ADAM_B1 * m + (1.0 - ADAM_B1) * g
    v = ADAM_B2 * v + (1.0 - ADAM_B2) * _jnp.square(g)
    m_hat = m / (1.0 - ADAM_B1 ** ADAM_STEP)
    v_hat = v / (1.0 - ADAM_B2 ** ADAM_STEP)
    delta = -ADAM_LR * (m_hat / (_jnp.sqrt(v_hat) + ADAM_EPS) + ADAM_WD * w)
    return delta, m, v


def reference(x, c, ctx, c_ctx, w_mod, b_mod, norm1_g, norm2_g, w_in, conv_dw, conv_dw_b, conv_ln_g, conv_ln_b, attn_sink, pool_w, pool_scale, w_out, w_ffn_in, w_ffn_out, final_g, loss_target, m_c_ctx, m_w_mod, m_b_mod, m_norm1_g, m_norm2_g, m_w_in, m_conv_dw, m_conv_dw_b, m_conv_ln_g, m_conv_ln_b, m_attn_sink, m_pool_w, m_pool_scale, m_w_out, m_w_ffn_in, m_w_ffn_out, m_final_g, v_c_ctx, v_w_mod, v_b_mod, v_norm1_g, v_norm2_g, v_w_in, v_conv_dw, v_conv_dw_b, v_conv_ln_g, v_conv_ln_b, v_attn_sink, v_pool_w, v_pool_scale, v_w_out, v_w_ffn_in, v_w_ffn_out, v_final_g):
    given = dict(x=x, c=c, ctx=ctx, c_ctx=c_ctx, w_mod=w_mod, b_mod=b_mod, norm1_g=norm1_g, norm2_g=norm2_g, w_in=w_in, conv_dw=conv_dw, conv_dw_b=conv_dw_b, conv_ln_g=conv_ln_g, conv_ln_b=conv_ln_b, attn_sink=attn_sink, pool_w=pool_w, pool_scale=pool_scale, w_out=w_out, w_ffn_in=w_ffn_in, w_ffn_out=w_ffn_out, final_g=final_g, loss_target=loss_target, m_c_ctx=m_c_ctx, m_w_mod=m_w_mod, m_b_mod=m_b_mod, m_norm1_g=m_norm1_g, m_norm2_g=m_norm2_g, m_w_in=m_w_in, m_conv_dw=m_conv_dw, m_conv_dw_b=m_conv_dw_b, m_conv_ln_g=m_conv_ln_g, m_conv_ln_b=m_conv_ln_b, m_attn_sink=m_attn_sink, m_pool_w=m_pool_w, m_pool_scale=m_pool_scale, m_w_out=m_w_out, m_w_ffn_in=m_w_ffn_in, m_w_ffn_out=m_w_ffn_out, m_final_g=m_final_g, v_c_ctx=v_c_ctx, v_w_mod=v_w_mod, v_b_mod=v_b_mod, v_norm1_g=v_norm1_g, v_norm2_g=v_norm2_g, v_w_in=v_w_in, v_conv_dw=v_conv_dw, v_conv_dw_b=v_conv_dw_b, v_conv_ln_g=v_conv_ln_g, v_conv_ln_b=v_conv_ln_b, v_attn_sink=v_attn_sink, v_pool_w=v_pool_w, v_pool_scale=v_pool_scale, v_w_out=v_w_out, v_w_ffn_in=v_w_ffn_in, v_w_ffn_out=v_w_ffn_out, v_final_g=v_final_g)
    weights = {n: given[n] for n in TWIN_WEIGHTS}
    shared = {n: given[n] for n in SHARED_INPUTS}
    per_example = {n: given[n] for n in ['x', 'c', 'ctx']}
    grad_fn = _jax.value_and_grad(_loss, argnums=(0, 1))

    def one_microbatch(ex, loss_target):
        ex = dict(ex)
        diff = ex.pop(TWIN_DIFF_INPUT)
        return grad_fn(weights, diff, {**shared, **ex}, loss_target)

    if N_MICROBATCH == 1:
        loss, (grad_w, grad_x) = one_microbatch(per_example, given["loss_target"])
    else:
        def body(carry, xs):
            loss_sum, grad_sum = carry
            l_k, (gw_k, gx_k) = one_microbatch(xs[0], xs[1])
            with _jax.named_scope("update"):
                return (loss_sum + l_k, _jax.tree.map(_jnp.add, grad_sum, gw_k)), gx_k

        init = (_jnp.zeros((), _jnp.float32), _jax.tree.map(_jnp.zeros_like, weights))
        (loss, grad_w), grad_x = _jax.lax.scan(body, init, (per_example, given["loss_target"]))
    with _jax.named_scope("update"):
        delta_w, new_m, new_v = {}, {}, {}
        for n in TWIN_WEIGHTS:
            delta_w[n], new_m[n], new_v[n] = _adamw(weights[n], grad_w[n], given["m_" + n], given["v_" + n])
    return (loss, grad_x, *[grad_w[n] for n in TWIN_WEIGHTS], *[delta_w[n] for n in TWIN_WEIGHTS],
            *[new_m[n] for n in TWIN_WEIGHTS], *[new_v[n] for n in TWIN_WEIGHTS])
```

```python
import functools
from typing import NamedTuple

import jax
import jax.numpy as jnp
import numpy as np
from jax import lax
from jax.experimental import pallas as pl
from jax.experimental.pallas import tpu as pltpu

F32 = jnp.float32
BF16 = jnp.bfloat16

D = 1024
GRID_W = 64
HEAD_DIM = 64
N_HEADS = 8
ATTN_W = 512
KV_W = 128
CONV_W = 256
POOL_W = 256
IN_W = 1536
D_FF = 2816
CONV_K = 31
WINDOW = 128
QB = 128
ROPE_BASE = 10000.0
EPS = 1e-6
NEG = -1e30
N_SHARD = 4
HALF_FF = D_FF // 2
MOD_ROWS = 16
PACK_LANES = 128

ADAM_LR = 0.001
ADAM_B1 = 0.9
ADAM_B2 = 0.999
ADAM_EPS = 1e-08
ADAM_WD = 0.01
ADAM_STEP = 10

VMEM_LIMIT_V7X = 56 * 1024 * 1024
TM = 512
CHUNK = 256

_MESH = pl.DeviceIdType.MESH
_ANY = pl.BlockSpec(memory_space=pl.ANY)
_DIMS = {"nn": (((1,), (0,)), ((), ())), "nt": (((1,), (1,)), ((), ())), "tn": (((0,), (0,)), ((), ()))}


class _Cfg(NamedTuple):
    b: int
    seq: int
    ctx: int

    @property
    def nl(self):
        return self.b * self.seq

    @property
    def nc(self):
        return self.b * self.ctx

    @property
    def r(self):
        return self.nl + self.nc

    def mod_row(self, i):
        return jnp.where(i < self.nl // TM, i // (self.seq // TM), self.b)

    def first_of_row(self, i):
        nlb = self.nl // TM
        return jnp.logical_or(jnp.logical_and(i < nlb, i % (self.seq // TM) == 0), i == nlb)


def _params(n_grid=0):
    sem = ("arbitrary",) * n_grid if n_grid else None
    return pltpu.CompilerParams(dimension_semantics=sem, vmem_limit_bytes=VMEM_LIMIT_V7X)


def _dot(a, b, mode="nn"):
    return lax.dot_general(a.astype(BF16), b.astype(BF16), _DIMS[mode], preferred_element_type=F32)


def _sigmoid(x):
    return 1.0 / (1.0 + jnp.exp(-x))


def _colsum(v):
    return jnp.sum(v, axis=0, keepdims=True)


def _epi_store(acc, ex, outs):
    for o in outs:
        o[...] = acc.astype(o.dtype)


def _mm(name, mode, grid, a, b, a_spec, b_spec, out_shape, out_specs, acc_shape=None, extras=(),
        extra_specs=(), a_fn=None, epi=_epi_store, aliases=None):
    nk = grid[2]
    n_ex, n_out = len(extras), len(out_shape)

    def body(*refs):
        a_ref, b_ref = refs[:2]
        ex = refs[2:2 + n_ex]
        outs = refs[2 + n_ex:2 + n_ex + n_out]
        av = a_ref[...]
        if a_fn is not None:
            av = a_fn(av)
        part = _dot(av, b_ref[...], mode)
        if nk == 1:
            epi(part, ex, outs)
        else:
            acc = refs[-1]
            k = pl.program_id(2)

            @pl.when(k == 0)
            def _():
                acc[...] = part

            @pl.when(k > 0)
            def _():
                acc[...] += part

            @pl.when(k == nk - 1)
            def _():
                epi(acc[...], ex, outs)

    scratch = [] if nk == 1 else [pltpu.VMEM(acc_shape, F32)]
    return pl.pallas_call(
        body, name=name, grid=grid, in_specs=[a_spec, b_spec, *extra_specs], out_specs=out_specs,
        out_shape=out_shape, scratch_shapes=scratch, input_output_aliases=aliases or {},
        compiler_params=_params(3))(a, b, *extras)


def _dw_pair(shape):
    return [jax.ShapeDtypeStruct(shape, F32), jax.ShapeDtypeStruct(shape, BF16)]


def _mm_dw(name, a, b, a_spec, b_spec, out_shape5, out_spec, n_out_blocks, acc_shape, prev, tr, a_fn=None,
           extras=(), extra_specs=(), epi=_epi_store, extra_out=(), extra_out_specs=()):
    rows = a.shape[0]
    grid = (1, n_out_blocks, rows // tr)
    outs = _dw_pair(out_shape5) + list(extra_out)
    specs = [out_spec, out_spec, *extra_out_specs]
    ex, ex_specs, aliases = list(extras), list(extra_specs), {}
    if prev is not None:
        aliases = {2 + len(ex): 0, 3 + len(ex): 1}
        ex += list(prev)
        ex_specs += [_ANY, _ANY]
    n_real = len(extras)

    def epi2(acc, exr, outr):
        epi(acc, exr[:n_real], outr)

    return _mm(name, "tn", grid, a, b, a_spec, b_spec, outs, specs, acc_shape, ex, ex_specs, a_fn, epi2, aliases)


def _mod_spec(cfg):
    return pl.BlockSpec((None, 6, D), lambda i: (cfg.mod_row(i), 0, 0))


def _norm_fwd(name, x, gvec, mod3, ish, isc, cfg):
    def body(x_ref, g_ref, m_ref, o_ref):
        xv = x_ref[...]
        r = lax.rsqrt(jnp.mean(xv * xv, axis=-1, keepdims=True) + EPS)
        o_ref[...] = (xv * r * g_ref[...] * (1.0 + m_ref[isc:isc + 1, :]) + m_ref[ish:ish + 1, :]).astype(BF16)

    row = pl.BlockSpec((TM, D), lambda i: (i, 0))
    return pl.pallas_call(
        body, name=name, grid=(cfg.r // TM,),
        in_specs=[row, pl.BlockSpec((1, D), lambda i: (0, 0)), _mod_spec(cfg)], out_specs=row,
        out_shape=jax.ShapeDtypeStruct((cfg.r, D), BF16), compiler_params=_params(1))(x, gvec, mod3)


def _accumulate_rows(first, ref, val):
    @pl.when(first)
    def _():
        ref[...] = val

    @pl.when(jnp.logical_not(first))
    def _():
        ref[...] += val


def _norm_bwd(name, x, dh, dres, gvec, mod3, isc, cfg):
    nb = cfg.r // TM

    def body(x_ref, dh_ref, dres_ref, g_ref, m_ref, dx_ref, dsh_ref, dsc_ref, dg_ref):
        i = pl.program_id(0)
        xv = x_ref[...]
        r = lax.rsqrt(jnp.mean(xv * xv, axis=-1, keepdims=True) + EPS)
        xh = xv * r
        g = g_ref[...]
        sc1 = 1.0 + m_ref[isc:isc + 1, :]
        dhv = dh_ref[...]
        t = dhv * xh
        first = cfg.first_of_row(i)
        _accumulate_rows(first, dsh_ref, _colsum(dhv))
        _accumulate_rows(first, dsc_ref, _colsum(t * g))
        _accumulate_rows(i == 0, dg_ref, _colsum(t * sc1))
        dxh = dhv * (g * sc1)
        dx_ref[...] = dres_ref[...] + r * (dxh - xh * jnp.mean(dxh * xh, axis=-1, keepdims=True))

    row = pl.BlockSpec((TM, D), lambda i: (i, 0))
    vec = pl.BlockSpec((1, D), lambda i: (0, 0))
    part = pl.BlockSpec((None, 1, D), lambda i: (cfg.mod_row(i), 0, 0))
    part_shape = jax.ShapeDtypeStruct((MOD_ROWS, 1, D), F32)
    return pl.pallas_call(
        body, name=name, grid=(nb,), in_specs=[row, row, row, vec, _mod_spec(cfg)],
        out_specs=[row, part, part, vec],
        out_shape=[jax.ShapeDtypeStruct((cfg.r, D), F32), part_shape, part_shape, jax.ShapeDtypeStruct((1, D), F32)],
        compiler_params=_params(1))(x, dh, dres, gvec, mod3)


def _gate_bwd(name, dx, y, mod3, ig, cfg):
    def body(dx_ref, y_ref, m_ref, dy_ref, dg_ref):
        i = pl.program_id(0)
        dxv = dx_ref[...]
        dy_ref[...] = (dxv * m_ref[ig:ig + 1, :]).astype(BF16)
        _accumulate_rows(cfg.first_of_row(i), dg_ref, _colsum(dxv * y_ref[...].astype(F32)))

    row = pl.BlockSpec((TM, D), lambda i: (i, 0))
    part = pl.BlockSpec((None, 1, D), lambda i: (cfg.mod_row(i), 0, 0))
    return pl.pallas_call(
        body, name=name, grid=(cfg.r // TM,), in_specs=[row, row, _mod_spec(cfg)], out_specs=[row, part],
        out_shape=[jax.ShapeDtypeStruct((cfg.r, D), BF16), jax.ShapeDtypeStruct((MOD_ROWS, 1, D), F32)],
        compiler_params=_params(1))(dx, y, mod3)


def _loss_head(x, target, gvec, cfg):
    nlb = cfg.nl // TM

    def body(x_ref, t_ref, g_ref, dx_ref, loss_ref, dg_ref):
        i = pl.program_id(0)

        @pl.when(i == 0)
        def _():
            loss_ref[...] = jnp.zeros_like(loss_ref)
            dg_ref[...] = jnp.zeros_like(dg_ref)

        @pl.when(i < nlb)
        def _():
            xv = x_ref[...]
            g = g_ref[...]
            r = lax.rsqrt(jnp.mean(xv * xv, axis=-1, keepdims=True) + EPS)
            xh = xv * r
            err = xh * g - t_ref[...]
            loss_ref[...] += (0.5 / D) * _colsum(jnp.sum(err * err, axis=-1, keepdims=True))
            dy = err * (1.0 / D)
            dg_ref[...] += _colsum(dy * xh)
            dxh = dy * g
            dx_ref[...] = r * (dxh - xh * jnp.mean(dxh * xh, axis=-1, keepdims=True))

        @pl.when(i >= nlb)
        def _():
            dx_ref[...] = jnp.zeros_like(dx_ref)

    row = pl.BlockSpec((TM, D), lambda i: (i, 0))
    vec = pl.BlockSpec((1, D), lambda i: (0, 0))
    return pl.pallas_call(
        body, name="loss_head", grid=(cfg.r // TM,),
        in_specs=[row, pl.BlockSpec((TM, D), lambda i: (jnp.minimum(i, nlb - 1), 0)), vec],
        out_specs=[row, pl.BlockSpec((1, 1), lambda i: (0, 0)), vec],
        out_shape=[jax.ShapeDtypeStruct((cfg.r, D), F32), jax.ShapeDtypeStruct((1, 1), F32),
                   jax.ShapeDtypeStruct((1, D), F32)],
        compiler_params=_params(1))(x, target, gvec)


def _rope_tables(seq):
    rows = seq // GRID_W
    row = jnp.repeat(jnp.arange(rows), GRID_W).astype(F32)
    col = jnp.tile(jnp.arange(GRID_W), rows).astype(F32)
    half = HEAD_DIM // 2
    inv = ROPE_BASE ** (-jnp.arange(0, half, 2, dtype=F32) / half)
    ar, ac = row[:, None] * inv, col[:, None] * inv
    ang = jnp.concatenate([ar, ar, ac, ac], axis=-1)
    sign = jnp.tile(jnp.concatenate([-jnp.ones((16,), F32), jnp.ones((16,), F32)]), 2)
    cos = jnp.tile(jnp.cos(ang), (1, 2))
    sin = jnp.tile(jnp.sin(ang) * sign, (1, 2))
    cos = jnp.concatenate([cos, jnp.ones((TM, 2 * HEAD_DIM), F32)], axis=0)
    sin = jnp.concatenate([sin, jnp.zeros((TM, 2 * HEAD_DIM), F32)], axis=0)
    return cos, sin


def _rope(x, cos, sin_signed, sign):
    lane = lax.broadcasted_iota(jnp.int32, x.shape, 1)
    low = (lane % 32) < 16
    rot = jnp.where(low, pltpu.roll(x, 112, 1), pltpu.roll(x, 16, 1))
    return x * cos + sign * (rot * sin_signed)


def _in_proj(name, h, w_in, layer, cos_t, sin_t, cfg):
    nlb, bps = cfg.nl // TM, cfg.seq // TM

    def body(h_ref, w_ref, cos_ref, sin_ref, qkv_ref, cp_ref):
        hv = h_ref[...]
        u = jnp.concatenate([_dot(hv, w_ref[j]) for j in range(N_SHARD)], axis=1)
        cos, sin = cos_ref[...], sin_ref[...]
        tiles = []
        for t in range(5):
            y = _rope(u[:, 128 * t:128 * (t + 1)], cos, sin, 1.0)
            tiles.append(y * (HEAD_DIM ** -0.5) if t < 4 else y)
        tiles.append(u[:, 640:768])
        qkv_ref[...] = jnp.concatenate(tiles, axis=1).astype(BF16)
        cp_ref[...] = u[:, 768:IN_W].astype(BF16)

    tab = pl.BlockSpec((TM, 128), lambda i: (jnp.where(i < nlb, i % bps, bps), 0))
    half = pl.BlockSpec((TM, 768), lambda i: (i, 0))
    return pl.pallas_call(
        body, name=name, grid=(cfg.r // TM,),
        in_specs=[pl.BlockSpec((TM, D), lambda i: (i, 0)),
                  pl.BlockSpec((None, N_SHARD, D, IN_W // N_SHARD), lambda i: (layer, 0, 0, 0)), tab, tab],
        out_specs=[half, half],
        out_shape=[jax.ShapeDtypeStruct((cfg.r, 768), BF16), jax.ShapeDtypeStruct((cfg.r, 768), BF16)],
        compiler_params=_params(1))(h, w_in, cos_t, sin_t)


def _att_specs(cfg):
    nlb, ncb = cfg.seq // QB, cfg.ctx // QB

    def qblk(s, qb):
        return jnp.where(qb < nlb, s * nlb + qb, cfg.nl // QB + s * ncb + qb - nlb)

    def near(off, col):
        return pl.BlockSpec((QB, 128), lambda s, qb: (s * nlb + jnp.clip(qb + off, 0, nlb - 1), col))

    def ctxs(col):
        return pl.BlockSpec((cfg.ctx, 128), lambda s, qb: (cfg.nl // cfg.ctx + s, col))

    qspec = pl.BlockSpec((QB, ATTN_W), lambda s, qb: (qblk(s, qb), 0))
    kv = [ctxs(4), ctxs(5), near(-1, 4), near(0, 4), near(1, 4), near(-1, 5), near(0, 5), near(1, 5)]
    return qblk, qspec, kv


def _att_scores(qb, nlb, sink_ref, q_ref, k_refs, kh):
    is_lat = qb < nlb
    ii = lax.broadcasted_iota(jnp.int32, (4 * QB, QB), 0) % QB
    jj = lax.broadcasted_iota(jnp.int32, (4 * QB, QB), 1)
    off_p = jnp.where(jnp.logical_and(is_lat, qb >= 1), 0.0, NEG)
    off_c = jnp.where(is_lat, 0.0, NEG)
    off_n = jnp.where(jnp.logical_and(is_lat, qb <= nlb - 2), 0.0, NEG)
    q4 = jnp.concatenate([q_ref[:, (4 * kh + g) * HEAD_DIM:(4 * kh + g + 1) * HEAD_DIM] for g in range(4)], axis=0)
    rg = lax.broadcasted_iota(jnp.int32, (4 * QB, 1), 0) // QB
    snk = jnp.where(rg == 0, sink_ref[4 * kh],
                    jnp.where(rg == 1, sink_ref[4 * kh + 1], jnp.where(rg == 2, sink_ref[4 * kh + 2], sink_ref[4 * kh + 3])))
    lanes = slice(kh * HEAD_DIM, (kh + 1) * HEAD_DIM)
    kx, kp, kc, kn = [r[:, lanes] for r in k_refs]
    sx = _dot(q4, kx, "nt")
    sp = jnp.where(jj >= ii, _dot(q4, kp, "nt"), NEG) + off_p
    sc = _dot(q4, kc, "nt") + off_c
    sn = jnp.where(jj <= ii, _dot(q4, kn, "nt"), NEG) + off_n
    return q4, snk, (kx, kp, kc, kn), (sx, sp, sc, sn)


def _att_fwd(name, qkv, sink, cfg):
    nlb, ncb = cfg.seq // QB, cfg.ctx // QB
    qblk, qspec, kvspecs = _att_specs(cfg)

    def body(sink_ref, q_ref, kx_ref, vx_ref, kp_ref, kc_ref, kn_ref, vp_ref, vc_ref, vn_ref, o_ref, lse_ref):
        qb = pl.program_id(1)
        for kh in range(2):
            q4, snk, _, ss = _att_scores(qb, nlb, sink_ref, q_ref, (kx_ref, kp_ref, kc_ref, kn_ref), kh)
            lanes = slice(kh * HEAD_DIM, (kh + 1) * HEAD_DIM)
            vs = [r[:, lanes] for r in (vx_ref, vp_ref, vc_ref, vn_ref)]
            m = snk
            for s_ in ss:
                m = jnp.maximum(m, jnp.max(s_, axis=-1, keepdims=True))
            den = jnp.exp(snk - m)
            o4 = jnp.zeros((4 * QB, HEAD_DIM), F32)
            for s_, v_ in zip(ss, vs):
                p = jnp.exp(s_ - m)
                den = den + jnp.sum(p, axis=-1, keepdims=True)
                o4 = o4 + _dot(p, v_)
            o4 = o4 / den
            lse = m + jnp.log(den)
            for g in range(4):
                h = 4 * kh + g
                o_ref[:, h * HEAD_DIM:(h + 1) * HEAD_DIM] = o4[g * QB:(g + 1) * QB].astype(BF16)
                lse_ref[:, h:h + 1] = lse[g * QB:(g + 1) * QB]

    return pl.pallas_call(
        body, name=name, grid=(cfg.b, nlb + ncb),
        in_specs=[pl.BlockSpec(memory_space=pltpu.SMEM), qspec, *kvspecs],
        out_specs=[pl.BlockSpec((QB, ATTN_W), lambda s, qb: (qblk(s, qb), 0)),
                   pl.BlockSpec((QB, N_HEADS), lambda s, qb: (qblk(s, qb), 0))],
        out_shape=[jax.ShapeDtypeStruct((cfg.r, D), BF16), jax.ShapeDtypeStruct((cfg.r, N_HEADS), F32)],
        compiler_params=_params(2))(sink, *([qkv] * 9))


def _att_bwd(name, qkv, mix, dmix, lse, sink, cos_t, sin_t, cfg):
    nlb, ncb = cfg.seq // QB, cfg.ctx // QB
    nqb = nlb + ncb
    qblk, qspec, kvspecs = _att_specs(cfg)

    def body(sink_ref, q_ref, kx_ref, vx_ref, kp_ref, kc_ref, kn_ref, vp_ref, vc_ref, vn_ref, o_ref, do_ref,
             lse_ref, cosq_ref, sinq_ref, cosk_ref, sink_tab_ref, dq_ref, dkvl_ref, dkvc_ref, dsink_ref,
             accl, accc, dqs):
        s_id, qb = pl.program_id(0), pl.program_id(1)

        @pl.when(qb == 0)
        def _():
            accl[...] = jnp.zeros_like(accl)
            accc[...] = jnp.zeros_like(accc)

        @pl.when(jnp.logical_and(s_id == 0, qb == 0))
        def _():
            dsink_ref[...] = jnp.zeros_like(dsink_ref)

        starts = [pl.multiple_of(jnp.clip(qb + off, 0, nlb - 1) * QB, QB) for off in (-1, 0, 1)]
        for kh in range(2):
            q4, snk, ks, ss = _att_scores(qb, nlb, sink_ref, q_ref, (kx_ref, kp_ref, kc_ref, kn_ref), kh)
            lanes = slice(kh * HEAD_DIM, (kh + 1) * HEAD_DIM)
            vs = [r[:, lanes] for r in (vx_ref, vp_ref, vc_ref, vn_ref)]
            heads = [slice((4 * kh + g) * HEAD_DIM, (4 * kh + g + 1) * HEAD_DIM) for g in range(4)]
            do4 = jnp.concatenate([do_ref[:, hs] for hs in heads], axis=0)
            o4 = jnp.concatenate([o_ref[:, hs] for hs in heads], axis=0).astype(F32)
            lse4 = jnp.concatenate([lse_ref[:, 4 * kh + g:4 * kh + g + 1] for g in range(4)], axis=0)
            delta = jnp.sum(do4 * o4, axis=-1, keepdims=True)
            dq4 = jnp.zeros((4 * QB, HEAD_DIM), F32)
            dks, dvs = [], []
            for s_, k_, v_ in zip(ss, ks, vs):
                p = jnp.exp(s_ - lse4)
                ds = p * (_dot(do4, v_, "nt") - delta)
                dq4 = dq4 + _dot(ds, k_)
                dks.append(_dot(ds, q4, "tn"))
                dvs.append(_dot(p, do4, "tn"))
            accc[:, lanes] += dks[0]
            accc[:, 128 + kh * HEAD_DIM:128 + (kh + 1) * HEAD_DIM] += dvs[0]
            for st, dk_, dv_ in zip(starts, dks[1:], dvs[1:]):
                accl[pl.ds(st, QB), lanes] += dk_
                accl[pl.ds(st, QB), 128 + kh * HEAD_DIM:128 + (kh + 1) * HEAD_DIM] += dv_
            dsk = -jnp.exp(snk - lse4) * delta
            for g in range(4):
                h = 4 * kh + g
                dsink_ref[h:h + 1, :] += jnp.broadcast_to(_colsum(dsk[g * QB:(g + 1) * QB]), (1, 128))
                dqs[:, heads[g]] = dq4[g * QB:(g + 1) * QB]
        cos, sin = cosq_ref[...], sinq_ref[...]
        dq_ref[...] = jnp.concatenate(
            [_rope(dqs[:, 128 * t:128 * (t + 1)], cos, sin, -1.0) * (HEAD_DIM ** -0.5) for t in range(4)],
            axis=1).astype(BF16)

        @pl.when(qb == nqb - 1)
        def _():
            dk = _rope(accl[:, 0:128], cosk_ref[...], sink_tab_ref[...], -1.0)
            dkvl_ref[...] = jnp.concatenate([dk, accl[:, 128:256]], axis=1).astype(BF16)
            dkvc_ref[...] = accc[...].astype(BF16)

    rowq = lambda w: pl.BlockSpec((QB, w), lambda s, qb: (qblk(s, qb), 0))
    tabq = pl.BlockSpec((QB, 128), lambda s, qb: (jnp.where(qb < nlb, qb, cfg.seq // QB), 0))
    tabk = pl.BlockSpec((cfg.seq, 128), lambda s, qb: (0, 0))
    return pl.pallas_call(
        body, name=name, grid=(cfg.b, nqb),
        in_specs=[pl.BlockSpec(memory_space=pltpu.SMEM), qspec, *kvspecs, rowq(ATTN_W), rowq(ATTN_W), rowq(N_HEADS),
                  tabq, tabq, tabk, tabk],
        out_specs=[rowq(ATTN_W), pl.BlockSpec((cfg.seq, 256), lambda s, qb: (s, 0)),
                   pl.BlockSpec((cfg.ctx, 256), lambda s, qb: (s, 0)), pl.BlockSpec((N_HEADS, 128), lambda s, qb: (0, 0))],
        out_shape=[jax.ShapeDtypeStruct((cfg.r, ATTN_W), BF16), jax.ShapeDtypeStruct((cfg.nl, 256), BF16),
                   jax.ShapeDtypeStruct((cfg.nc, 256), BF16), jax.ShapeDtypeStruct((N_HEADS, 128), F32)],
        scratch_shapes=[pltpu.VMEM((cfg.seq, 256), F32), pltpu.VMEM((cfg.ctx, 256), F32), pltpu.VMEM((QB, ATTN_W), F32)],
        compiler_params=_params(2))(sink, *([qkv] * 9), mix, dmix, lse, cos_t, sin_t, cos_t, sin_t)


PAD = 16


def _pool_geometry(n, c):
    lane = lax.broadcasted_iota(jnp.int32, (1, POOL_W), 1) // HEAD_DIM
    wl = jnp.where(lane == 0, 1, jnp.where(lane == 1, 2, jnp.where(lane == 2, 4, 8)))
    wr = wl - 1
    t = c * CHUNK + lax.broadcasted_iota(jnp.int32, (CHUNK, POOL_W), 0)
    cnt = (jnp.minimum(t + wr, n - 1) - jnp.maximum(t - wl, 0) + 1).astype(F32)
    return wl, wr, cnt


def _conv_chunk(hp, dw_ref, dwb_ref, c):
    acc = jnp.zeros((CHUNK, CONV_W), F32) + dwb_ref[...]
    for j in range(CONV_K):
        acc = acc + dw_ref[j:j + 1, :] * hp[c * CHUNK + j + 1:c * CHUNK + j + 1 + CHUNK, :]
    return acc


def _fill_glu(cp_ref, hp, n):
    hp[0:PAD, :] = jnp.zeros((PAD, CONV_W), F32)
    hp[PAD + n:2 * PAD + n, :] = jnp.zeros((PAD, CONV_W), F32)
    for c in range(n // CHUNK):
        rows = slice(c * CHUNK, (c + 1) * CHUNK)
        a = cp_ref[rows, 0:CONV_W].astype(F32)
        g = cp_ref[rows, CONV_W:2 * CONV_W].astype(F32)
        hp[PAD + c * CHUNK:PAD + (c + 1) * CHUNK, :] = a * _sigmoid(g)


def _fill_pool(cp_ref, pp, n):
    pp[0:PAD, :] = jnp.zeros((PAD, POOL_W), F32)
    pp[PAD + n:2 * PAD + n, :] = jnp.zeros((PAD, POOL_W), F32)
    for c in range(n // CHUNK):
        pp[PAD + c * CHUNK:PAD + (c + 1) * CHUNK, :] = cp_ref[c * CHUNK:(c + 1) * CHUNK, 2 * CONV_W:768].astype(F32)


def _pool_chunk(pp, n, c):
    wl, wr, cnt = _pool_geometry(n, c)
    acc = jnp.zeros((CHUNK, POOL_W), F32)
    for o in range(-8, 8):
        acc = acc + jnp.where(jnp.logical_and(o >= -wl, o <= wr), pp[PAD + o + c * CHUNK:PAD + o + (c + 1) * CHUNK, :], 0.0)
    return acc / cnt - pp[PAD + c * CHUNK:PAD + (c + 1) * CHUNK, :], cnt


def _seq_specs(n, blk_off, width, col=0):
    return pl.BlockSpec((n, width), lambda s: (blk_off + s, col))


def _full(shape):
    return pl.BlockSpec(shape, lambda s: (0,) * len(shape))


def _convpool_fwd(name, cpin, mix, prm, n, blk_off, cfg):
    dw, dwb, lng, lnb, wbd, ps = prm

    def body(cp_ref, mix_in, dw_ref, dwb_ref, lng_ref, lnb_ref, wbd_ref, ps_ref, out_ref, hp, pp):
        _fill_glu(cp_ref, hp, n)
        _fill_pool(cp_ref, pp, n)
        for c in range(n // CHUNK):
            rows = slice(c * CHUNK, (c + 1) * CHUNK)
            y = _conv_chunk(hp, dw_ref, dwb_ref, c)
            d = y - jnp.mean(y, axis=-1, keepdims=True)
            hn = d * lax.rsqrt(jnp.mean(d * d, axis=-1, keepdims=True) + EPS) * lng_ref[...] + lnb_ref[...]
            out_ref[rows, 0:CONV_W] = (hn * _sigmoid(hn)).astype(BF16)
            yp, _ = _pool_chunk(pp, n, c)
            out_ref[rows, CONV_W:2 * CONV_W] = (_dot(yp, wbd_ref[...]) * ps_ref[...]).astype(BF16)

    return pl.pallas_call(
        body, name=name, grid=(cfg.b,),
        in_specs=[_seq_specs(n, blk_off, 768), _ANY, _full((32, CONV_W)), _full((1, CONV_W)), _full((1, CONV_W)),
                  _full((1, CONV_W)), _full((POOL_W, POOL_W)), _full((1, POOL_W))],
        out_specs=_seq_specs(n, blk_off, 512, 1), out_shape=jax.ShapeDtypeStruct((cfg.r, D), BF16),
        scratch_shapes=[pltpu.VMEM((n + 2 * PAD, CONV_W), F32), pltpu.VMEM((n + 2 * PAD, POOL_W), F32)],
        input_output_aliases={1: 0}, compiler_params=_params(1))(cpin, mix, dw, dwb, lng, lnb, wbd, ps)


_SMALL_SHAPES = [(32, CONV_W), (1, CONV_W), (1, CONV_W), (1, CONV_W), (POOL_W, POOL_W), (1, POOL_W)]


def _convpool_bwd(name, cpin, dmix, prm, acc_in, n, blk_off, cfg):
    dw, dwb, lng, lnb, wbd, ps = prm
    nch = n // CHUNK

    def body(cp_ref, dm_ref, dw_ref, dwb_ref, lng_ref, lnb_ref, wbd_ref, ps_ref, dcp_in,
             a_dw, a_dwb, a_lng, a_lnb, a_wbd, a_ps,
             dcp_ref, o_dw, o_dwb, o_lng, o_lnb, o_wbd, o_ps, hp, dyp, pp, wp, dyv, dwacc):
        s = pl.program_id(0)

        @pl.when(s == 0)
        def _():
            for o_, a_ in ((o_dw, a_dw), (o_dwb, a_dwb), (o_lng, a_lng), (o_lnb, a_lnb), (o_wbd, a_wbd), (o_ps, a_ps)):
                o_[...] = a_[...]
            dwacc[...] = jnp.zeros_like(dwacc)

        _fill_glu(cp_ref, hp, n)
        _fill_pool(cp_ref, pp, n)
        for ref in (dyp, wp):
            ref[0:PAD, :] = jnp.zeros((PAD, CONV_W), F32)
            ref[PAD + n:2 * PAD + n, :] = jnp.zeros((PAD, CONV_W), F32)
        for c in range(nch):
            rows = slice(c * CHUNK, (c + 1) * CHUNK)
            y = _conv_chunk(hp, dw_ref, dwb_ref, c)
            d = y - jnp.mean(y, axis=-1, keepdims=True)
            rstd = lax.rsqrt(jnp.mean(d * d, axis=-1, keepdims=True) + EPS)
            xh = d * rstd
            hn = xh * lng_ref[...] + lnb_ref[...]
            sg = _sigmoid(hn)
            dhn = dm_ref[rows, 0:CONV_W] * (sg * (1.0 + hn * (1.0 - sg)))
            o_lnb[...] += _colsum(dhn)
            o_lng[...] += _colsum(dhn * xh)
            dxh = dhn * lng_ref[...]
            dy = rstd * (dxh - jnp.mean(dxh, axis=-1, keepdims=True) - xh * jnp.mean(dxh * xh, axis=-1, keepdims=True))
            o_dwb[...] += _colsum(dy)
            dyp[PAD + c * CHUNK:PAD + (c + 1) * CHUNK, :] = dy
            for j in range(CONV_K):
                prod = dy * hp[c * CHUNK + j + 1:c * CHUNK + j + 1 + CHUNK, :]
                dwacc[8 * j:8 * j + 8, :] += jnp.sum(prod.reshape(CHUNK // 8, 8, CONV_W), axis=0)
            yp, cnt = _pool_chunk(pp, n, c)
            dz = dm_ref[rows, CONV_W:2 * CONV_W]
            o_ps[...] += _colsum(dz * _dot(yp, wbd_ref[...]))
            dzs = dz * ps_ref[...]
            o_wbd[...] += _dot(yp, dzs, "tn")
            dv = _dot(dzs, wbd_ref[...], "nt")
            dyv[rows, :] = dv
            wp[PAD + c * CHUNK:PAD + (c + 1) * CHUNK, :] = dv / cnt
        for c in range(nch):
            rows = slice(c * CHUNK, (c + 1) * CHUNK)
            dh = jnp.zeros((CHUNK, CONV_W), F32)
            for j in range(CONV_K):
                dh = dh + dw_ref[j:j + 1, :] * dyp[c * CHUNK + 31 - j:c * CHUNK + 31 - j + CHUNK, :]
            a = cp_ref[rows, 0:CONV_W].astype(F32)
            sg = _sigmoid(cp_ref[rows, CONV_W:2 * CONV_W].astype(F32))
            dcp_ref[rows, 0:CONV_W] = (dh * sg).astype(BF16)
            dcp_ref[rows, CONV_W:2 * CONV_W] = (dh * a * sg * (1.0 - sg)).astype(BF16)
            wl, wr, _ = _pool_geometry(n, c)
            dp = -dyv[rows, :]
            for o in range(-8, 8):
                dp = dp + jnp.where(jnp.logical_and(o >= -wl, o <= wr),
                                    wp[PAD - o + c * CHUNK:PAD - o + (c + 1) * CHUNK, :], 0.0)
            dcp_ref[rows, 2 * CONV_W:768] = dp.astype(BF16)

        @pl.when(s == cfg.b - 1)
        def _():
            for j in range(CONV_K):
                o_dw[j:j + 1, :] += _colsum(dwacc[8 * j:8 * j + 8, :])

    small_specs = [_full(sh) for sh in _SMALL_SHAPES]
    return pl.pallas_call(
        body, name=name, grid=(cfg.b,),
        in_specs=[_seq_specs(n, blk_off, 768), _seq_specs(n, blk_off, 512, 1), *small_specs, _ANY, *small_specs],
        out_specs=[_seq_specs(n, blk_off, 768), *small_specs],
        out_shape=[jax.ShapeDtypeStruct((cfg.r, 768), BF16)] + [jax.ShapeDtypeStruct(sh, F32) for sh in _SMALL_SHAPES],
        scratch_shapes=[pltpu.VMEM((n + 2 * PAD, CONV_W), F32), pltpu.VMEM((n + 2 * PAD, CONV_W), F32),
                        pltpu.VMEM((n + 2 * PAD, POOL_W), F32), pltpu.VMEM((n + 2 * PAD, POOL_W), F32),
                        pltpu.VMEM((n, POOL_W), F32), pltpu.VMEM((8 * 32, CONV_W), F32)],
        input_output_aliases={8: 0}, compiler_params=_params(1))(cpin, dmix, dw, dwb, lng, lnb, wbd, ps, *acc_in)


def _ffn_in(name, h, w_ffn_in, layer, cfg):
    def body(h_ref, wg_ref, wu_ref, gu_ref, act_ref):
        hv = h_ref[...]
        g = _dot(hv, wg_ref[...])
        u = _dot(hv, wu_ref[...])
        gu_ref[0] = g.astype(BF16)
        gu_ref[1] = u.astype(BF16)
        act_ref[...] = (g * _sigmoid(g) * u).astype(BF16)

    wspec = lambda base: pl.BlockSpec((None, None, D, HALF_FF), lambda j, i: (layer, base + j, 0, 0))
    return pl.pallas_call(
        body, name=name, grid=(2, cfg.r // TM),
        in_specs=[pl.BlockSpec((TM, D), lambda j, i: (i, 0)), wspec(0), wspec(2)],
        out_specs=[pl.BlockSpec((2, TM, HALF_FF), lambda j, i: (0, i, j)), pl.BlockSpec((TM, HALF_FF), lambda j, i: (i, j))],
        out_shape=[jax.ShapeDtypeStruct((2, cfg.r, D_FF), BF16), jax.ShapeDtypeStruct((cfg.r, D_FF), BF16)],
        compiler_params=_params(2))(h, w_ffn_in, w_ffn_in)


def _row_block(rows, cols, max_bytes=1 << 20):
    best = 16
    for t in range(16, rows + 1, 16):
        if rows % t == 0 and t * cols * 4 <= max_bytes:
            best = t
    assert rows % best == 0
    return best


def _pair_add(name, own32, recv, c_idx):
    _, _, s0, s1 = own32.shape
    tr = _row_block(s0, s1)

    def body(c_ref, a_ref, b_ref, o_ref):
        o_ref[...] = (a_ref[...] + b_ref[...].astype(F32)).astype(BF16)

    grid_spec = pltpu.PrefetchScalarGridSpec(
        num_scalar_prefetch=1, grid=(N_SHARD * s0 // tr,),
        in_specs=[pl.BlockSpec((None, tr, s1), lambda i, c: (c[0], i, 0)), pl.BlockSpec((tr, s1), lambda i, c: (i, 0))],
        out_specs=pl.BlockSpec((tr, s1), lambda i, c: (i, 0)))
    out = pl.pallas_call(body, name=name, grid_spec=grid_spec, out_shape=jax.ShapeDtypeStruct((N_SHARD * s0, s1), BF16),
                         compiler_params=_params(1))(c_idx, own32.reshape(2, N_SHARD * s0, s1), recv.reshape(N_SHARD * s0, s1))
    return out.reshape(N_SHARD, s0, s1)


def _shard_sum(name, pair_sum, recv, j_idx):
    _, s0, s1 = pair_sum.shape
    tr = _row_block(s0, s1)

    def body(j_ref, a_ref, b_ref, o_ref):
        o_ref[...] = ((a_ref[...].astype(F32) + b_ref[0].astype(F32)) + b_ref[1].astype(F32)) + b_ref[2].astype(F32)

    grid_spec = pltpu.PrefetchScalarGridSpec(
        num_scalar_prefetch=1, grid=(s0 // tr,),
        in_specs=[pl.BlockSpec((None, tr, s1), lambda i, j: (j[0], i, 0)), pl.BlockSpec((3, tr, s1), lambda i, j: (0, i, 0))],
        out_specs=pl.BlockSpec((tr, s1), lambda i, j: (i, 0)))
    return pl.pallas_call(body, name=name, grid_spec=grid_spec, out_shape=jax.ShapeDtypeStruct((s0, s1), F32),
                          compiler_params=_params(1))(j_idx, pair_sum, recv)


def _adamw_math(w, g, m, v):
    m = ADAM_B1 * m + (1.0 - ADAM_B1) * g
    v = ADAM_B2 * v + (1.0 - ADAM_B2) * (g * g)
    m_hat = m / (1.0 - ADAM_B1 ** ADAM_STEP)
    v_hat = v / (1.0 - ADAM_B2 ** ADAM_STEP)
    delta = -ADAM_LR * (m_hat / (jnp.sqrt(v_hat) + ADAM_EPS) + ADAM_WD * w)
    return delta, m, v


def _adamw(name, w, g, m, v):
    rows, cols = w.shape
    tr = rows if rows % 16 else _row_block(rows, cols, 1 << 19)

    def body(w_ref, g_ref, m_ref, v_ref, d_ref, mo_ref, vo_ref):
        d, mn, vn = _adamw_math(w_ref[...], g_ref[...], m_ref[...], v_ref[...])
        d_ref[...] = d
        mo_ref[...] = mn
        vo_ref[...] = vn

    spec = pl.BlockSpec((tr, cols), lambda i: (i, 0))
    shape = jax.ShapeDtypeStruct((rows, cols), F32)
    return pl.pallas_call(body, name=name, grid=(rows // tr,), in_specs=[spec] * 4, out_specs=[spec] * 3,
                          out_shape=[shape] * 3, compiler_params=_params(1))(w, g, m, v)


def _position():
    return lax.axis_index("x"), lax.axis_index("y"), lax.axis_index("c")


def _comm(name, ins, out_shape, n_remote, n_local, plan):
    n_in, n_out = len(ins), len(out_shape)

    def body(*refs):
        plan(refs[:n_in], refs[n_in:n_in + n_out], *refs[n_in + n_out:])

    return pl.pallas_call(
        body, name=name, in_specs=[_ANY] * n_in, out_specs=[_ANY] * n_out, out_shape=out_shape,
        scratch_shapes=[pltpu.SemaphoreType.DMA((n_remote,)), pltpu.SemaphoreType.DMA((n_remote,)),
                        pltpu.SemaphoreType.DMA((max(n_local, 1),))])(*ins)


def _other_chips(x, y):
    return [(1 - x, y), (x, 1 - y), (1 - x, 1 - y)]


def _gather_layer(shards):
    n = len(shards)

    def plan(ins, outs, ssem, rsem, lsem):
        x, y, c = _position()
        me = 2 * x + y
        local, remote = [], []
        for a, (src, dst) in enumerate(zip(ins, outs)):
            loc = pltpu.make_async_copy(src.at[c], dst.at[me], lsem.at[a])
            loc.start()
            local.append(loc)
            for k, (px, py) in enumerate(_other_chips(x, y)):
                cp = pltpu.make_async_remote_copy(src.at[c], dst.at[me], ssem.at[3 * a + k], rsem.at[3 * a + k],
                                                  device_id=(px, py, c), device_id_type=_MESH)
                cp.start()
                remote.append(cp)
        for a, (src, dst) in enumerate(zip(ins, outs)):
            for k, (px, py) in enumerate(_other_chips(x, y)):
                pltpu.make_async_remote_copy(src.at[c], dst.at[2 * px + py], ssem.at[3 * a + k], rsem.at[3 * a + k],
                                             device_id=(px, py, c), device_id_type=_MESH).wait_recv()
        for cp in remote:
            cp.wait_send()
        for loc in local:
            loc.wait()

    shapes = [jax.ShapeDtypeStruct((N_SHARD,) + s.shape[1:], s.dtype) for s in shards]
    return _comm("gather_layer", shards, shapes, 3 * n, n, plan)


def _swap_layers(name, mine):
    n = len(mine)

    def plan(ins, outs, ssem, rsem, lsem):
        x, y, c = _position()
        started = []
        for a, (src, dst) in enumerate(zip(ins, outs)):
            loc = pltpu.make_async_copy(src, dst.at[c], lsem.at[a])
            cp = pltpu.make_async_remote_copy(src, dst.at[c], ssem.at[a], rsem.at[a], device_id=(x, y, 1 - c),
                                              device_id_type=_MESH)
            loc.start()
            cp.start()
            started.append((loc, cp))
        for a, (src, dst) in enumerate(zip(ins, outs)):
            pltpu.make_async_remote_copy(src, dst.at[1 - c], ssem.at[a], rsem.at[a], device_id=(x, y, 1 - c),
                                         device_id_type=_MESH).wait_recv()
        for loc, cp in started:
            loc.wait()
            cp.wait_send()

    shapes = [jax.ShapeDtypeStruct((2,) + s.shape, s.dtype) for s in mine]
    return _comm(name, mine, shapes, n, n, plan)


def _send_other_layer(grads_bf):
    n = len(grads_bf)

    def plan(ins, outs, ssem, rsem, lsem):
        x, y, c = _position()
        started = []
        for a, (src, dst) in enumerate(zip(ins, outs)):
            cp = pltpu.make_async_remote_copy(src.at[1 - c], dst, ssem.at[a], rsem.at[a], device_id=(x, y, 1 - c),
                                              device_id_type=_MESH)
            cp.start()
            started.append(cp)
        for cp in started:
            cp.wait_recv()
        for cp in started:
            cp.wait_send()

    shapes = [jax.ShapeDtypeStruct(s.shape[1:], s.dtype) for s in grads_bf]
    return _comm("send_other_layer", grads_bf, shapes, n, 0, plan)


def _exchange_shards(pair_sums):
    n = len(pair_sums)

    def plan(ins, outs, ssem, rsem, lsem):
        x, y, c = _position()
        started = []
        for a, (src, dst) in enumerate(zip(ins, outs)):
            for k, (px, py) in enumerate(_other_chips(x, y)):
                cp = pltpu.make_async_remote_copy(src.at[2 * px + py], dst.at[k], ssem.at[3 * a + k], rsem.at[3 * a + k],
                                                  device_id=(px, py, c), device_id_type=_MESH)
                cp.start()
                started.append(cp)
        for cp in started:
            cp.wait_recv()
        for cp in started:
            cp.wait_send()

    shapes = [jax.ShapeDtypeStruct((3,) + s.shape[1:], s.dtype) for s in pair_sums]
    return _comm("exchange_shards", pair_sums, shapes, 3 * n, 0, plan)


def _sum_small(pack):
    p = pack.shape[0]
    flips = [(dx, dy, dc) for dx in (0, 1) for dy in (0, 1) for dc in (0, 1) if dx + dy + dc]

    def body(in_ref, out_ref, buf, ssem, rsem):
        x, y, c = _position()
        me = 4 * x + 2 * y + c
        buf[me] = in_ref[...]
        started = []
        for k, (dx, dy, dc) in enumerate(flips):
            peer = ((x + dx) % 2, (y + dy) % 2, (c + dc) % 2)
            cp = pltpu.make_async_remote_copy(in_ref, buf.at[me], ssem.at[k], rsem.at[k], device_id=peer,
                                              device_id_type=_MESH)
            cp.start()
            started.append(cp)
        for k, (dx, dy, dc) in enumerate(flips):
            peer = ((x + dx) % 2, (y + dy) % 2, (c + dc) % 2)
            pltpu.make_async_remote_copy(in_ref, buf.at[4 * peer[0] + 2 * peer[1] + peer[2]], ssem.at[k], rsem.at[k],
                                         device_id=peer, device_id_type=_MESH).wait_recv()
        for cp in started:
            cp.wait_send()
        acc = buf[0]
        for d in range(1, 8):
            acc = acc + buf[d]
        out_ref[...] = acc

    vm = pl.BlockSpec(memory_space=pltpu.VMEM)
    return pl.pallas_call(
        body, name="sum_small", in_specs=[vm], out_specs=vm, out_shape=jax.ShapeDtypeStruct((p, PACK_LANES), F32),
        scratch_shapes=[pltpu.VMEM((8, p, PACK_LANES), F32), pltpu.SemaphoreType.DMA((7,)), pltpu.SemaphoreType.DMA((7,))],
        compiler_params=pltpu.CompilerParams(vmem_limit_bytes=VMEM_LIMIT_V7X))(pack)


def _pack(arrays):
    flat = jnp.concatenate([a.reshape(-1).astype(F32) for a in arrays])
    total = flat.shape[0]
    rows = -(-total // (8 * PACK_LANES)) * 8
    return jnp.pad(flat, (0, rows * PACK_LANES - total)).reshape(rows, PACK_LANES)


def _unpack(pack, shapes):
    flat, out, pos = pack.reshape(-1), [], 0
    for sh in shapes:
        size = int(np.prod(sh)) if len(sh) else 1
        out.append(flat[pos:pos + size].reshape(sh))
        pos += size
    return out


def _block_diag(pw):
    out = jnp.zeros((POOL_W, POOL_W), pw.dtype)
    for g in range(4):
        out = out.at[g * 64:(g + 1) * 64, g * 64:(g + 1) * 64].set(pw[g])
    return out


def _dsilu(x):
    s = _sigmoid(x)
    return s * (1.0 + x * (1.0 - s))


def _silu(x):
    return x * _sigmoid(x)


def _local_step(x, c, ctx, c_ctx, small, wf, loss_target):
    cfg = _Cfg(x.shape[0], x.shape[1], ctx.shape[1])
    nrb = cfg.r // TM
    assert cfg.seq % TM == 0 and cfg.nc % TM == 0 and cfg.seq % cfg.ctx == 0 and cfg.ctx % CHUNK == 0
    cos_t, sin_t = _rope_tables(cfg.seq)
    xs = jnp.concatenate([x.reshape(cfg.nl, D), ctx.reshape(cfg.nc, D)], axis=0)
    cc = jnp.concatenate([c, c_ctx[None, :], jnp.zeros((MOD_ROWS - cfg.b - 1, D), F32)], axis=0)
    w_out_v = wf["w_out"].reshape(2, D, D)
    w_ffo_v = wf["w_ffn_out"].reshape(2, D_FF, D)
    row = lambda w: pl.BlockSpec((TM, w), lambda i, j, k: (i, 0))
    mod3_spec = pl.BlockSpec((None, 6, D), lambda i, j, k: (cfg.mod_row(i), 0, 0))

    def conv_params(l):
        dw = jnp.pad(small["conv_dw"][l], ((0, 1), (0, 0)))
        return (dw, small["conv_dw_b"][l][None], small["conv_ln_g"][l][None], small["conv_ln_b"][l][None],
                _block_diag(small["pool_w"][l]).astype(BF16), small["pool_scale"][l][None])

    def residual_epi(ig):
        def epi(acc, ex, outs):
            x_ref, m_ref = ex
            outs[0][...] = x_ref[...] + m_ref[ig:ig + 1, :] * acc
            outs[1][...] = acc.astype(BF16)
        return epi

    saved = []
    for l in range(2):
        mvec = _mm(f"mod_fwd{l}", "nn", (1, N_SHARD, 1), cc, wf["w_mod"],
                   pl.BlockSpec((MOD_ROWS, D), lambda i, j, k: (0, 0)),
                   pl.BlockSpec((None, None, D, 6 * D // N_SHARD), lambda i, j, k: (l, j, 0, 0)),
                   [jax.ShapeDtypeStruct((MOD_ROWS, 6 * D), F32)],
                   [pl.BlockSpec((MOD_ROWS, 6 * D // N_SHARD), lambda i, j, k: (0, j))],
                   extras=[small["b_mod"][l][None]], extra_specs=[pl.BlockSpec((1, 6 * D // N_SHARD), lambda i, j, k: (0, j))],
                   a_fn=_silu, epi=lambda acc, ex, outs: outs[0].__setitem__(Ellipsis, acc + ex[0][...]))[0]
        mod3 = mvec.reshape(MOD_ROWS, 6, D)
        h1 = _norm_fwd(f"norm1_fwd{l}", xs, small["norm1_g"][l][None], mod3, 0, 1, cfg)
        qkv, cpin = _in_proj(f"in_proj{l}", h1, wf["w_in"], l, cos_t, sin_t, cfg)
        mix, lse = _att_fwd(f"att_fwd{l}", qkv, small["attn_sink"][l], cfg)
        prm = conv_params(l)
        mix = _convpool_fwd(f"convpool_fwd_lat{l}", cpin, mix, prm, cfg.seq, 0, cfg)
        mix = _convpool_fwd(f"convpool_fwd_ctx{l}", cpin, mix, prm, cfg.ctx, cfg.nl // cfg.ctx, cfg)
        x1, y1 = _mm(f"out_proj{l}", "nn", (nrb, 1, 1), mix, w_out_v, row(D),
                     pl.BlockSpec((None, D, D), lambda i, j, k: (l, 0, 0)),
                     [jax.ShapeDtypeStruct((cfg.r, D), F32), jax.ShapeDtypeStruct((cfg.r, D), BF16)], [row(D), row(D)],
                     extras=[xs, mod3], extra_specs=[row(D), mod3_spec], epi=residual_epi(2))
        h2 = _norm_fwd(f"norm2_fwd{l}", x1, small["norm2_g"][l][None], mod3, 3, 4, cfg)
        gu, act = _ffn_in(f"ffn_in{l}", h2, wf["w_ffn_in"], l, cfg)
        x2, y2 = _mm(f"ffn_out{l}", "nn", (nrb, 1, 1), act, w_ffo_v, row(D_FF),
                     pl.BlockSpec((None, D_FF, D), lambda i, j, k: (l, 0, 0)),
                     [jax.ShapeDtypeStruct((cfg.r, D), F32), jax.ShapeDtypeStruct((cfg.r, D), BF16)], [row(D), row(D)],
                     extras=[x1, mod3], extra_specs=[row(D), mod3_spec], epi=residual_epi(5))
        saved.append(dict(mod3=mod3, x0=xs, h1=h1, qkv=qkv, cpin=cpin, mix=mix, lse=lse, y1=y1, x1=x1, h2=h2,
                          gu=gu, act=act, y2=y2, prm=prm))
        xs = x2

    dx, loss, d_final_g = _loss_head(xs, loss_target.reshape(cfg.nl, D), small["final_g"][None], cfg)

    big = {k: None for k in ("w_mod", "w_in", "w_out", "w_ffn_in", "w_ffn_out")}
    sg = {k: [None, None] for k in ("b_mod", "norm1_g", "norm2_g", "conv_dw", "conv_dw_b", "conv_ln_g", "conv_ln_b",
                                    "attn_sink", "pool_w", "pool_scale")}
    d_c_ctx = jnp.zeros((D,), F32)

    def swiglu_bwd_epi(acc, ex, outs):
        g = ex[0][0].astype(F32)
        u = ex[0][1].astype(F32)
        s = _sigmoid(g)
        outs[0][0] = (acc * u * (s * (1.0 + g * (1.0 - s)))).astype(BF16)
        outs[0][1] = (acc * (g * s)).astype(BF16)

    for l in (1, 0):
        sv = saved[l]
        mod3 = sv["mod3"]
        dy2, dg2 = _gate_bwd(f"gate2_bwd{l}", dx, sv["y2"], mod3, 5, cfg)
        gu_spec = pl.BlockSpec((2, TM, HALF_FF), lambda i, j, k: (0, i, j))
        df = _mm(f"ffn_out_bwd{l}", "nt", (nrb, 2, 1), dy2, w_ffo_v, row(D),
                 pl.BlockSpec((None, HALF_FF, D), lambda i, j, k: (l, j, 0)),
                 [jax.ShapeDtypeStruct((2, cfg.r, D_FF), BF16)], [gu_spec], extras=[sv["gu"]], extra_specs=[gu_spec],
                 epi=swiglu_bwd_epi)[0]
        big["w_ffn_out"] = _mm_dw(f"dw_ffn_out{l}", sv["act"], dy2,
                                  pl.BlockSpec((TM, HALF_FF), lambda i, j, k: (k, j)), pl.BlockSpec((TM, D), lambda i, j, k: (k, 0)),
                                  (2, D_FF, D), pl.BlockSpec((None, HALF_FF, D), lambda i, j, k: (l, j, 0)), 2,
                                  (HALF_FF, D), big["w_ffn_out"], TM)[:2]
        big["w_ffn_in"] = _mm_dw(f"dw_ffn_in{l}", sv["h2"], df,
                                 pl.BlockSpec((TM, D), lambda i, j, k: (k, 0)),
                                 pl.BlockSpec((None, TM, HALF_FF), lambda i, j, k: (j // 2, k, j % 2)),
                                 (2, N_SHARD, D, HALF_FF), pl.BlockSpec((None, None, D, HALF_FF), lambda i, j, k: (l, j, 0, 0)),
                                 N_SHARD, (D, HALF_FF), big["w_ffn_in"], TM)[:2]
        dh2 = _mm(f"ffn_in_bwd{l}", "nt", (nrb, 1, N_SHARD), df, wf["w_ffn_in"],
                  pl.BlockSpec((None, TM, HALF_FF), lambda i, j, k: (k // 2, i, k % 2)),
                  pl.BlockSpec((None, None, D, HALF_FF), lambda i, j, k: (l, k, 0, 0)),
                  [jax.ShapeDtypeStruct((cfg.r, D), F32)], [row(D)], acc_shape=(TM, D))[0]
        dx1, dsh2, dsc2, dn2 = _norm_bwd(f"norm2_bwd{l}", sv["x1"], dh2, dx, small["norm2_g"][l][None], mod3, 4, cfg)
        dy1, dg1 = _gate_bwd(f"gate1_bwd{l}", dx1, sv["y1"], mod3, 2, cfg)
        dmix = _mm(f"out_proj_bwd{l}", "nt", (nrb, 1, 1), dy1, w_out_v, row(D),
                   pl.BlockSpec((None, D, D), lambda i, j, k: (l, 0, 0)),
                   [jax.ShapeDtypeStruct((cfg.r, D), F32)], [row(D)])[0]
        big["w_out"] = _mm_dw(f"dw_out{l}", sv["mix"], dy1, pl.BlockSpec((TM, D), lambda i, j, k: (k, 0)),
                              pl.BlockSpec((TM, D), lambda i, j, k: (k, 0)), (2, D, D),
                              pl.BlockSpec((None, D, D), lambda i, j, k: (l, 0, 0)), 1, (D, D), big["w_out"], TM)[:2]
        dq, dkvl, dkvc, dsink = _att_bwd(f"att_bwd{l}", sv["qkv"], sv["mix"], dmix, sv["lse"], small["attn_sink"][l],
                                         cos_t, sin_t, cfg)
        acc0 = [jnp.zeros((cfg.r, 768), BF16)] + [jnp.zeros(sh, F32) for sh in _SMALL_SHAPES]
        acc1 = _convpool_bwd(f"convpool_bwd_lat{l}", sv["cpin"], dmix, sv["prm"], acc0, cfg.seq, 0, cfg)
        dcp, g_dw, g_dwb, g_lng, g_lnb, g_wbd, g_ps = _convpool_bwd(
            f"convpool_bwd_ctx{l}", sv["cpin"], dmix, sv["prm"], acc1, cfg.ctx, cfg.nl // cfg.ctx, cfg)
        sg["attn_sink"][l] = dsink[:, 0]
        sg["conv_dw"][l], sg["conv_dw_b"][l], sg["conv_ln_g"][l], sg["conv_ln_b"][l] = g_dw[:CONV_K], g_dwb[0], g_lng[0], g_lnb[0]
        sg["pool_w"][l] = jnp.stack([g_wbd[g * 64:(g + 1) * 64, g * 64:(g + 1) * 64] for g in range(4)])
        sg["pool_scale"][l] = g_ps[0]
        du = jnp.concatenate([dq, jnp.concatenate([dkvl, dkvc], axis=0), dcp], axis=1)
        big["w_in"] = _mm_dw(f"dw_in{l}", sv["h1"], du, pl.BlockSpec((TM, D), lambda i, j, k: (k, 0)),
                             pl.BlockSpec((TM, IN_W // N_SHARD), lambda i, j, k: (k, j)), (2, N_SHARD, D, IN_W // N_SHARD),
                             pl.BlockSpec((None, None, D, IN_W // N_SHARD), lambda i, j, k: (l, j, 0, 0)), N_SHARD,
                             (D, IN_W // N_SHARD), big["w_in"], TM)[:2]
        dh1 = _mm(f"in_proj_bwd{l}", "nt", (nrb, 1, N_SHARD), du, wf["w_in"],
                  pl.BlockSpec((TM, IN_W // N_SHARD), lambda i, j, k: (i, k)),
                  pl.BlockSpec((None, None, D, IN_W // N_SHARD), lambda i, j, k: (l, k, 0, 0)),
                  [jax.ShapeDtypeStruct((cfg.r, D), F32)], [row(D)], acc_shape=(TM, D))[0]
        dx, dsh1, dsc1, dn1 = _norm_bwd(f"norm1_bwd{l}", sv["x0"], dh1, dx1, small["norm1_g"][l][None], mod3, 1, cfg)
        sg["norm1_g"][l], sg["norm2_g"][l] = dn1[0], dn2[0]
        dm = jnp.concatenate([t[:, 0, :] for t in (dsh1, dsc1, dg1, dsh2, dsc2, dg2)], axis=1)
        dm = jnp.concatenate([dm[:cfg.b + 1], jnp.zeros((MOD_ROWS - cfg.b - 1, 6 * D), F32)], axis=0)
        ncol = 6 * D // N_SHARD
        dm_spec = pl.BlockSpec((MOD_ROWS, ncol), lambda i, j, k: (0, j))
        res = _mm_dw(f"dw_mod{l}", cc, dm, pl.BlockSpec((MOD_ROWS, D), lambda i, j, k: (0, 0)), dm_spec,
                     (2, N_SHARD, D, ncol), pl.BlockSpec((None, None, D, ncol), lambda i, j, k: (l, j, 0, 0)), N_SHARD,
                     None, big["w_mod"], MOD_ROWS, a_fn=_silu, extras=[dm], extra_specs=[dm_spec],
                     epi=lambda acc, ex, outs: (_epi_store(acc, ex, outs[:2]), outs[2].__setitem__(Ellipsis, _colsum(ex[0][...]))),
                     extra_out=[jax.ShapeDtypeStruct((1, 6 * D), F32)], extra_out_specs=[pl.BlockSpec((1, ncol), lambda i, j, k: (0, j))])
        big["w_mod"], sg["b_mod"][l] = res[:2], res[2][0]
        dcc = _mm(f"mod_bwd{l}", "nt", (1, 1, N_SHARD), dm, wf["w_mod"], pl.BlockSpec((MOD_ROWS, ncol), lambda i, j, k: (0, k)),
                  pl.BlockSpec((None, None, D, ncol), lambda i, j, k: (l, k, 0, 0)),
                  [jax.ShapeDtypeStruct((MOD_ROWS, D), F32)], [pl.BlockSpec((MOD_ROWS, D), lambda i, j, k: (0, 0))],
                  acc_shape=(MOD_ROWS, D), extras=[cc], extra_specs=[pl.BlockSpec((MOD_ROWS, D), lambda i, j, k: (0, 0))],
                  epi=lambda acc, ex, outs: outs[0].__setitem__(Ellipsis, acc * _dsilu(ex[0][...])))[0]
        d_c_ctx = d_c_ctx + dcc[cfg.b]

    grad_x = dx[:cfg.nl].reshape(x.shape)
    small_grads = {k: jnp.stack(v) for k, v in sg.items()}
    small_grads["c_ctx"] = d_c_ctx
    small_grads["final_g"] = d_final_g[0]
    return loss, grad_x, small_grads, big


_BIG = ("w_mod", "w_in", "w_out", "w_ffn_in", "w_ffn_out")
_SMALL = ("c_ctx", "b_mod", "norm1_g", "norm2_g", "conv_dw", "conv_dw_b", "conv_ln_g", "conv_ln_b", "attn_sink",
          "pool_w", "pool_scale", "final_g")
_ORDER = ("c_ctx", "w_mod", "b_mod", "norm1_g", "norm2_g", "w_in", "conv_dw", "conv_dw_b", "conv_ln_g", "conv_ln_b",
          "attn_sink", "pool_w", "pool_scale", "w_out", "w_ffn_in", "w_ffn_out", "final_g")


def kernel(x, c, ctx, c_ctx, w_mod, b_mod, norm1_g, norm2_g, w_in, conv_dw, conv_dw_b, conv_ln_g, conv_ln_b, attn_sink, pool_w, pool_scale, w_out, w_ffn_in, w_ffn_out, final_g, loss_target, m_c_ctx, m_w_mod, m_b_mod, m_norm1_g, m_norm2_g, m_w_in, m_conv_dw, m_conv_dw_b, m_conv_ln_g, m_conv_ln_b, m_attn_sink, m_pool_w, m_pool_scale, m_w_out, m_w_ffn_in, m_w_ffn_out, m_final_g, v_c_ctx, v_w_mod, v_b_mod, v_norm1_g, v_norm2_g, v_w_in, v_conv_dw, v_conv_dw_b, v_conv_ln_g, v_conv_ln_b, v_attn_sink, v_pool_w, v_pool_scale, v_w_out, v_w_ffn_in, v_w_ffn_out, v_final_g):
    w = dict(c_ctx=c_ctx, w_mod=w_mod, b_mod=b_mod, norm1_g=norm1_g, norm2_g=norm2_g, w_in=w_in, conv_dw=conv_dw,
             conv_dw_b=conv_dw_b, conv_ln_g=conv_ln_g, conv_ln_b=conv_ln_b, attn_sink=attn_sink, pool_w=pool_w,
             pool_scale=pool_scale, w_out=w_out, w_ffn_in=w_ffn_in, w_ffn_out=w_ffn_out, final_g=final_g)
    m = dict(c_ctx=m_c_ctx, w_mod=m_w_mod, b_mod=m_b_mod, norm1_g=m_norm1_g, norm2_g=m_norm2_g, w_in=m_w_in,
             conv_dw=m_conv_dw, conv_dw_b=m_conv_dw_b, conv_ln_g=m_conv_ln_g, conv_ln_b=m_conv_ln_b,
             attn_sink=m_attn_sink, pool_w=m_pool_w, pool_scale=m_pool_scale, w_out=m_w_out, w_ffn_in=m_w_ffn_in,
             w_ffn_out=m_w_ffn_out, final_g=m_final_g)
    v = dict(c_ctx=v_c_ctx, w_mod=v_w_mod, b_mod=v_b_mod, norm1_g=v_norm1_g, norm2_g=v_norm2_g, w_in=v_w_in,
             conv_dw=v_conv_dw, conv_dw_b=v_conv_dw_b, conv_ln_g=v_conv_ln_g, conv_ln_b=v_conv_ln_b,
             attn_sink=v_attn_sink, pool_w=v_pool_w, pool_scale=v_pool_scale, w_out=v_w_out, w_ffn_in=v_w_ffn_in,
             w_ffn_out=v_w_ffn_out, final_g=v_final_g)
    xi, yi, ci = _position()
    c_idx = jnp.reshape(ci, (1,)).astype(jnp.int32)
    j_idx = jnp.reshape(2 * xi + yi, (1,)).astype(jnp.int32)

    dw_pad = jnp.pad(conv_dw, ((0, 0), (0, 1), (0, 64)))
    layer_c = _gather_layer([w[k].astype(BF16) for k in _BIG] + [dw_pad])
    both = _swap_layers("swap_weights", layer_c)
    wf = dict(zip(_BIG, both[:5]))
    dw_full = jnp.transpose(both[5][:, :, :CONV_K, :64], (0, 2, 1, 3)).reshape(2, CONV_K, CONV_W)
    small = {k: w[k] for k in _SMALL}
    small["conv_dw"] = dw_full

    loss, grad_x, sgrads, big = _local_step(x, c, ctx, c_ctx, small, wf, loss_target)

    shapes = {k: w[k].shape[1:] for k in _BIG}
    g32 = [big[k][0].reshape((2, N_SHARD) + shapes[k]) for k in _BIG]
    gbf = [big[k][1].reshape((2, N_SHARD) + shapes[k]) for k in _BIG]
    other = _send_other_layer(gbf)
    pair = [_pair_add(f"pair_add_{k}", a, b, c_idx) for k, a, b in zip(_BIG, g32, other)]
    recv = _exchange_shards(pair)
    mine = [_shard_sum(f"shard_sum_{k}", a, b, j_idx) for k, a, b in zip(_BIG, pair, recv)]
    gfull = dict(zip(_BIG, _swap_layers("swap_grads", mine)))

    names = list(_SMALL)
    total = _sum_small(_pack([loss] + [sgrads[k] for k in names]))
    parts = _unpack(total, [()] + [sgrads[k].shape for k in names])
    loss_out = parts[0]
    gsmall = dict(zip(names, parts[1:]))
    gsmall["conv_dw"] = lax.dynamic_slice_in_dim(gsmall["conv_dw"], (2 * xi + yi) * 64, 64, axis=2)

    grads, delta, new_m, new_v = dict(gfull), {}, {}, {}
    grads.update(gsmall)
    for k in _BIG:
        s0, s1 = shapes[k]
        flat = lambda a: a.reshape(2 * s0, s1)
        d_, m_, v_ = _adamw(f"adamw_{k}", flat(w[k]), flat(gfull[k]), flat(m[k]), flat(v[k]))
        delta[k], new_m[k], new_v[k] = d_.reshape(w[k].shape), m_.reshape(w[k].shape), v_.reshape(w[k].shape)
    d_, m_, v_ = _adamw("adamw_small", _pack([w[k] for k in names]), _pack([gsmall[k] for k in names]),
                        _pack([m[k] for k in names]), _pack([v[k] for k in names]))
    sshapes = [w[k].shape for k in names]
    for k, a, b, e in zip(names, _unpack(d_, sshapes), _unpack(m_, sshapes), _unpack(v_, sshapes)):
        delta[k], new_m[k], new_v[k] = a, b, e
    return (loss_out, grad_x, *[grads[k] for k in _ORDER], *[delta[k] for k in _ORDER],
            *[new_m[k] for k in _ORDER], *[new_v[k] for k in _ORDER])
```

```python
from typing import NamedTuple

import jax
import jax.numpy as jnp
import numpy as np
from jax import lax
from jax.experimental import pallas as pl
from jax.experimental.pallas import tpu as pltpu

F32 = jnp.float32
BF16 = jnp.bfloat16

D = 1024
GRID_W = 64
HEAD_DIM = 64
N_HEADS = 8
ATTN_W = 512
CONV_W = 256
POOL_W = 256
IN_W = 1536
D_FF = 2816
CONV_K = 31
QB = 128
ROPE_BASE = 10000.0
EPS = 1e-6
NEG = -1e30
N_SHARD = 4
IN_SHARD = IN_W // N_SHARD
HALF_FF = D_FF // 2
MOD_W = 6 * D // N_SHARD
MOD_ROWS = 16
PACK_LANES = 128

ADAM_LR = 0.001
ADAM_B1 = 0.9
ADAM_B2 = 0.999
ADAM_EPS = 1e-08
ADAM_WD = 0.01
ADAM_STEP = 10

VMEM_LIMIT_V7X = 56 * 1024 * 1024
TM = 512
TR_MAX = 1024
CHUNK = 256
PAD = 16

_MESH = pl.DeviceIdType.MESH
_ANY = pl.BlockSpec(memory_space=pl.ANY)
_DIMS = {"nn": (((1,), (0,)), ((), ())), "nt": (((1,), (1,)), ((), ())), "tn": (((0,), (0,)), ((), ()))}


class _Cfg(NamedTuple):
    b: int
    seq: int
    ctx: int

    @property
    def nl(self):
        return self.b * self.seq

    @property
    def nc(self):
        return self.b * self.ctx

    @property
    def r(self):
        return self.nl + self.nc

    def mod_row(self, i):
        return jnp.where(i < self.nl // TM, i // (self.seq // TM), self.b)

    def first_of_row(self, i):
        nlb = self.nl // TM
        return jnp.logical_or(jnp.logical_and(i < nlb, i % (self.seq // TM) == 0), i == nlb)


def _params(n_grid=0):
    sem = ("arbitrary",) * n_grid if n_grid else None
    return pltpu.CompilerParams(dimension_semantics=sem, vmem_limit_bytes=VMEM_LIMIT_V7X)


def _dot(a, b, mode="nn"):
    return lax.dot_general(a.astype(BF16), b.astype(BF16), _DIMS[mode], preferred_element_type=F32)


def _sigmoid(x):
    return 1.0 / (1.0 + jnp.exp(-x))


def _silu(x):
    return x * _sigmoid(x)


def _dsilu(x):
    s = _sigmoid(x)
    return s * (1.0 + x * (1.0 - s))


def _colsum(v):
    return jnp.sum(v, axis=0, keepdims=True)


def _dw_rows(rows):
    return TR_MAX if rows % TR_MAX == 0 else TM


def _epi_store(acc, ex, outs):
    for o in outs:
        o[...] = acc.astype(o.dtype)


def _mm(name, mode, grid, a, b, a_spec, b_spec, out_shape, out_specs, acc_shape=None, extras=(),
        extra_specs=(), a_fn=None, epi=_epi_store, aliases=None):
    nk = grid[2]
    n_ex, n_out = len(extras), len(out_shape)

    def body(*refs):
        a_ref, b_ref = refs[:2]
        ex = refs[2:2 + n_ex]
        outs = refs[2 + n_ex:2 + n_ex + n_out]
        av = a_ref[...]
        if a_fn is not None:
            av = a_fn(av)
        part = _dot(av, b_ref[...], mode)
        if nk == 1:
            epi(part, ex, outs)
        else:
            acc = refs[-1]
            k = pl.program_id(2)

            @pl.when(k == 0)
            def _():
                acc[...] = part

            @pl.when(k > 0)
            def _():
                acc[...] += part

            @pl.when(k == nk - 1)
            def _():
                epi(acc[...], ex, outs)

    scratch = [] if nk == 1 else [pltpu.VMEM(acc_shape, F32)]
    return pl.pallas_call(
        body, name=name, grid=grid, in_specs=[a_spec, b_spec, *extra_specs], out_specs=out_specs,
        out_shape=out_shape, scratch_shapes=scratch, input_output_aliases=aliases or {},
        compiler_params=_params(3))(a, b, *extras)


def _mm_dw(name, a, b, a_spec, b_spec, out_shape5, out_spec, n_out_blocks, acc_shape, prev, n_steps, a_fn=None,
           extras=(), extra_specs=(), epi=_epi_store, extra_out=(), extra_out_specs=()):
    grid = (1, n_out_blocks, n_steps)
    outs = [jax.ShapeDtypeStruct(out_shape5, F32), jax.ShapeDtypeStruct(out_shape5, BF16)] + list(extra_out)
    specs = [out_spec, out_spec, *extra_out_specs]
    ex, ex_specs, aliases = list(extras), list(extra_specs), {}
    if prev is not None:
        aliases = {2 + len(ex): 0, 3 + len(ex): 1}
        ex += list(prev)
        ex_specs += [_ANY, _ANY]
    n_real = len(extras)

    def epi2(acc, exr, outr):
        epi(acc, exr[:n_real], outr)

    return _mm(name, "tn", grid, a, b, a_spec, b_spec, outs, specs, acc_shape, ex, ex_specs, a_fn, epi2, aliases)


def _proj_bwd(name, a, a_spec, pick, w, layer, nblk, cfg):
    ns = w.shape[-1]

    def body(a_ref, w_ref, o_ref):
        acc = _dot(pick(a_ref, 0), w_ref[0], "nt")
        for j in range(1, N_SHARD):
            acc = acc + _dot(pick(a_ref, j), w_ref[j], "nt")
        o_ref[...] = acc

    return pl.pallas_call(
        body, name=name, grid=(nblk,),
        in_specs=[a_spec, pl.BlockSpec((None, N_SHARD, D, ns), lambda i: (layer, 0, 0, 0))],
        out_specs=pl.BlockSpec((TM, D), lambda i: (i, 0)), out_shape=jax.ShapeDtypeStruct((cfg.r, D), F32),
        compiler_params=_params(1))(a, w)


def _mod_spec(cfg):
    return pl.BlockSpec((None, 6, D), lambda i: (cfg.mod_row(i), 0, 0))


def _norm_fwd(name, x, gvec, mod3, ish, isc, nblk, cfg):
    def body(x_ref, g_ref, m_ref, o_ref):
        xv = x_ref[...]
        r = lax.rsqrt(jnp.mean(xv * xv, axis=-1, keepdims=True) + EPS)
        o_ref[...] = (xv * r * g_ref[...] * (1.0 + m_ref[isc:isc + 1, :]) + m_ref[ish:ish + 1, :]).astype(BF16)

    row = pl.BlockSpec((TM, D), lambda i: (i, 0))
    return pl.pallas_call(
        body, name=name, grid=(nblk,),
        in_specs=[row, pl.BlockSpec((1, D), lambda i: (0, 0)), _mod_spec(cfg)], out_specs=row,
        out_shape=jax.ShapeDtypeStruct((cfg.r, D), BF16), compiler_params=_params(1))(x, gvec, mod3)


def _accumulate_rows(first, ref, val):
    @pl.when(first)
    def _():
        ref[...] = val

    @pl.when(jnp.logical_not(first))
    def _():
        ref[...] += val


def _norm_bwd(name, x, dh, dres, gvec, mod3, isc, nblk, res_latent_only, cfg):
    nlb = cfg.nl // TM

    def body(x_ref, dh_ref, dres_ref, g_ref, m_ref, dx_ref, dsh_ref, dsc_ref, dg_ref):
        i = pl.program_id(0)
        xv = x_ref[...]
        r = lax.rsqrt(jnp.mean(xv * xv, axis=-1, keepdims=True) + EPS)
        xh = xv * r
        g = g_ref[...]
        sc1 = 1.0 + m_ref[isc:isc + 1, :]
        dhv = dh_ref[...]
        t = dhv * xh
        first = cfg.first_of_row(i)
        _accumulate_rows(first, dsh_ref, _colsum(dhv))
        _accumulate_rows(first, dsc_ref, _colsum(t * g))
        _accumulate_rows(i == 0, dg_ref, _colsum(t * sc1))
        dxh = dhv * (g * sc1)
        dxn = r * (dxh - xh * jnp.mean(dxh * xh, axis=-1, keepdims=True))
        if res_latent_only:
            dx_ref[...] = jnp.where(i < nlb, dres_ref[...], 0.0) + dxn
        else:
            dx_ref[...] = dres_ref[...] + dxn

    row = pl.BlockSpec((TM, D), lambda i: (i, 0))
    vec = pl.BlockSpec((1, D), lambda i: (0, 0))
    part = pl.BlockSpec((None, 1, D), lambda i: (cfg.mod_row(i), 0, 0))
    part_shape = jax.ShapeDtypeStruct((MOD_ROWS, 1, D), F32)
    return pl.pallas_call(
        body, name=name, grid=(nblk,), in_specs=[row, row, row, vec, _mod_spec(cfg)],
        out_specs=[row, part, part, vec],
        out_shape=[jax.ShapeDtypeStruct((cfg.r, D), F32), part_shape, part_shape, jax.ShapeDtypeStruct((1, D), F32)],
        compiler_params=_params(1))(x, dh, dres, gvec, mod3)


def _gate_bwd(name, dx, y, mod3, ig, nblk, cfg):
    def body(dx_ref, y_ref, m_ref, dy_ref, dg_ref):
        i = pl.program_id(0)
        dxv = dx_ref[...]
        dy_ref[...] = (dxv * m_ref[ig:ig + 1, :]).astype(BF16)
        _accumulate_rows(cfg.first_of_row(i), dg_ref, _colsum(dxv * y_ref[...].astype(F32)))

    row = pl.BlockSpec((TM, D), lambda i: (i, 0))
    part = pl.BlockSpec((None, 1, D), lambda i: (cfg.mod_row(i), 0, 0))
    return pl.pallas_call(
        body, name=name, grid=(nblk,), in_specs=[row, row, _mod_spec(cfg)], out_specs=[row, part],
        out_shape=[jax.ShapeDtypeStruct((cfg.r, D), BF16), jax.ShapeDtypeStruct((MOD_ROWS, 1, D), F32)],
        compiler_params=_params(1))(dx, y, mod3)


def _loss_head(x, target, gvec, cfg):
    def body(x_ref, t_ref, g_ref, dx_ref, loss_ref, dg_ref):
        i = pl.program_id(0)

        @pl.when(i == 0)
        def _():
            loss_ref[...] = jnp.zeros_like(loss_ref)
            dg_ref[...] = jnp.zeros_like(dg_ref)

        xv = x_ref[...]
        g = g_ref[...]
        r = lax.rsqrt(jnp.mean(xv * xv, axis=-1, keepdims=True) + EPS)
        xh = xv * r
        err = xh * g - t_ref[...]
        loss_ref[...] += (0.5 / D) * _colsum(jnp.sum(err * err, axis=-1, keepdims=True))
        dy = err * (1.0 / D)
        dg_ref[...] += _colsum(dy * xh)
        dxh = dy * g
        dx_ref[...] = r * (dxh - xh * jnp.mean(dxh * xh, axis=-1, keepdims=True))

    row = pl.BlockSpec((TM, D), lambda i: (i, 0))
    vec = pl.BlockSpec((1, D), lambda i: (0, 0))
    return pl.pallas_call(
        body, name="loss_head", grid=(cfg.nl // TM,), in_specs=[row, row, vec],
        out_specs=[row, pl.BlockSpec((1, 1), lambda i: (0, 0)), vec],
        out_shape=[jax.ShapeDtypeStruct((cfg.r, D), F32), jax.ShapeDtypeStruct((1, 1), F32),
                   jax.ShapeDtypeStruct((1, D), F32)],
        compiler_params=_params(1))(x, target, gvec)


def _rope_tables(seq):
    rows = seq // GRID_W
    row = jnp.repeat(jnp.arange(rows), GRID_W).astype(F32)
    col = jnp.tile(jnp.arange(GRID_W), rows).astype(F32)
    half = HEAD_DIM // 2
    inv = ROPE_BASE ** (-jnp.arange(0, half, 2, dtype=F32) / half)
    ar, ac = row[:, None] * inv, col[:, None] * inv
    ang = jnp.concatenate([ar, ar, ac, ac], axis=-1)
    sign = jnp.tile(jnp.concatenate([-jnp.ones((16,), F32), jnp.ones((16,), F32)]), 2)
    cos = jnp.tile(jnp.cos(ang), (1, 2))
    sin = jnp.tile(jnp.sin(ang) * sign, (1, 2))
    cos = jnp.concatenate([cos, jnp.ones((TM, 2 * HEAD_DIM), F32)], axis=0)
    sin = jnp.concatenate([sin, jnp.zeros((TM, 2 * HEAD_DIM), F32)], axis=0)
    return cos, sin


def _rope(x, cos, sin_signed, sign):
    lane = lax.broadcasted_iota(jnp.int32, x.shape, 1)
    low = (lane % 32) < 16
    rot = jnp.where(low, pltpu.roll(x, 112, 1), pltpu.roll(x, 16, 1))
    return x * cos + sign * (rot * sin_signed)


def _in_proj(name, h, w_in, layer, cos_t, sin_t, cfg):
    nlb, bps = cfg.nl // TM, cfg.seq // TM

    def body(h_ref, w_ref, cos_ref, sin_ref, qkv_ref, cp_ref):
        hv = h_ref[...]
        u = jnp.concatenate([_dot(hv, w_ref[j]) for j in range(N_SHARD)], axis=1)
        cos, sin = cos_ref[...], sin_ref[...]
        tiles = []
        for t in range(5):
            y = _rope(u[:, 128 * t:128 * (t + 1)], cos, sin, 1.0)
            tiles.append(y * (HEAD_DIM ** -0.5) if t < 4 else y)
        tiles.append(u[:, 640:768])
        qkv_ref[...] = jnp.concatenate(tiles, axis=1).astype(BF16)
        cp_ref[...] = u[:, 768:IN_W].astype(BF16)

    tab = pl.BlockSpec((TM, 128), lambda i: (jnp.where(i < nlb, i % bps, bps), 0))
    half = pl.BlockSpec((TM, 768), lambda i: (i, 0))
    return pl.pallas_call(
        body, name=name, grid=(cfg.r // TM,),
        in_specs=[pl.BlockSpec((TM, D), lambda i: (i, 0)),
                  pl.BlockSpec((None, N_SHARD, D, IN_SHARD), lambda i: (layer, 0, 0, 0)), tab, tab],
        out_specs=[half, half],
        out_shape=[jax.ShapeDtypeStruct((cfg.r, 768), BF16), jax.ShapeDtypeStruct((cfg.r, 768), BF16)],
        compiler_params=_params(1))(h, w_in, cos_t, sin_t)


def _att_specs(cfg):
    nlb, ncb = cfg.seq // QB, cfg.ctx // QB

    def qblk(s, qb):
        return jnp.where(qb < nlb, s * nlb + qb, cfg.nl // QB + s * ncb + qb - nlb)

    def near(off, col):
        return pl.BlockSpec((QB, 128), lambda s, qb: (s * nlb + jnp.clip(qb + off, 0, nlb - 1), col))

    def ctxs(col):
        return pl.BlockSpec((cfg.ctx, 128), lambda s, qb: (cfg.nl // cfg.ctx + s, col))

    qspec = pl.BlockSpec((QB, ATTN_W), lambda s, qb: (qblk(s, qb), 0))
    kv = [ctxs(4), ctxs(5), near(-1, 4), near(0, 4), near(1, 4), near(-1, 5), near(0, 5), near(1, 5)]
    return qblk, qspec, kv


def _att_scores(qb, nlb, sink_ref, q_ref, k_refs, kh):
    is_lat = qb < nlb
    ii = lax.broadcasted_iota(jnp.int32, (4 * QB, QB), 0) % QB
    jj = lax.broadcasted_iota(jnp.int32, (4 * QB, QB), 1)
    off_p = jnp.where(jnp.logical_and(is_lat, qb >= 1), 0.0, NEG)
    off_c = jnp.where(is_lat, 0.0, NEG)
    off_n = jnp.where(jnp.logical_and(is_lat, qb <= nlb - 2), 0.0, NEG)
    q4 = jnp.concatenate([q_ref[:, (4 * kh + g) * HEAD_DIM:(4 * kh + g + 1) * HEAD_DIM] for g in range(4)], axis=0)
    rg = lax.broadcasted_iota(jnp.int32, (4 * QB, 1), 0) // QB
    snk = jnp.where(rg == 0, sink_ref[4 * kh],
                    jnp.where(rg == 1, sink_ref[4 * kh + 1], jnp.where(rg == 2, sink_ref[4 * kh + 2], sink_ref[4 * kh + 3])))
    lanes = slice(kh * HEAD_DIM, (kh + 1) * HEAD_DIM)
    kx, kp, kc, kn = [r[:, lanes] for r in k_refs]
    sx = _dot(q4, kx, "nt")
    sp = jnp.where(jj >= ii, _dot(q4, kp, "nt"), NEG) + off_p
    sc = _dot(q4, kc, "nt") + off_c
    sn = jnp.where(jj <= ii, _dot(q4, kn, "nt"), NEG) + off_n
    return q4, snk, (kx, kp, kc, kn), (sx, sp, sc, sn)


def _att_fwd(name, qkv, sink, ctx_queries, cfg):
    nlb, ncb = cfg.seq // QB, cfg.ctx // QB
    qblk, qspec, kvspecs = _att_specs(cfg)

    def body(sink_ref, q_ref, kx_ref, vx_ref, kp_ref, kc_ref, kn_ref, vp_ref, vc_ref, vn_ref, o_ref, lse_ref):
        qb = pl.program_id(1)
        for kh in range(2):
            q4, snk, _, ss = _att_scores(qb, nlb, sink_ref, q_ref, (kx_ref, kp_ref, kc_ref, kn_ref), kh)
            lanes = slice(kh * HEAD_DIM, (kh + 1) * HEAD_DIM)
            vs = [r[:, lanes] for r in (vx_ref, vp_ref, vc_ref, vn_ref)]
            m = snk
            for s_ in ss:
                m = jnp.maximum(m, jnp.max(s_, axis=-1, keepdims=True))
            den = jnp.exp(snk - m)
            o4 = jnp.zeros((4 * QB, HEAD_DIM), F32)
            for s_, v_ in zip(ss, vs):
                p = jnp.exp(s_ - m)
                den = den + jnp.sum(p, axis=-1, keepdims=True)
                o4 = o4 + _dot(p, v_)
            o4 = o4 / den
            lse = m + jnp.log(den)
            for g in range(4):
                h = 4 * kh + g
                o_ref[:, h * HEAD_DIM:(h + 1) * HEAD_DIM] = o4[g * QB:(g + 1) * QB].astype(BF16)
                lse_ref[:, h:h + 1] = lse[g * QB:(g + 1) * QB]

    return pl.pallas_call(
        body, name=name, grid=(cfg.b, nlb + (ncb if ctx_queries else 0)),
        in_specs=[pl.BlockSpec(memory_space=pltpu.SMEM), qspec, *kvspecs],
        out_specs=[pl.BlockSpec((QB, ATTN_W), lambda s, qb: (qblk(s, qb), 0)),
                   pl.BlockSpec((QB, N_HEADS), lambda s, qb: (qblk(s, qb), 0))],
        out_shape=[jax.ShapeDtypeStruct((cfg.r, D), BF16), jax.ShapeDtypeStruct((cfg.r, N_HEADS), F32)],
        compiler_params=_params(2))(sink, *([qkv] * 9))


def _att_bwd(name, qkv, mix, dmix, lse, sink, cos_t, sin_t, ctx_queries, cfg):
    nlb, ncb = cfg.seq // QB, cfg.ctx // QB
    nqb = nlb + (ncb if ctx_queries else 0)
    qblk, qspec, kvspecs = _att_specs(cfg)

    def body(sink_ref, q_ref, kx_ref, vx_ref, kp_ref, kc_ref, kn_ref, vp_ref, vc_ref, vn_ref, o_ref, do_ref,
             lse_ref, cosq_ref, sinq_ref, cosk_ref, sink_tab_ref, dq_ref, dkvl_ref, dkvc_ref, dsink_ref,
             accl, accc, dqs):
        s_id, qb = pl.program_id(0), pl.program_id(1)

        @pl.when(qb == 0)
        def _():
            accl[...] = jnp.zeros_like(accl)
            accc[...] = jnp.zeros_like(accc)

        @pl.when(jnp.logical_and(s_id == 0, qb == 0))
        def _():
            dsink_ref[...] = jnp.zeros_like(dsink_ref)

        starts = [pl.multiple_of(jnp.clip(qb + off, 0, nlb - 1) * QB, QB) for off in (-1, 0, 1)]
        for kh in range(2):
            q4, snk, ks, ss = _att_scores(qb, nlb, sink_ref, q_ref, (kx_ref, kp_ref, kc_ref, kn_ref), kh)
            lanes = slice(kh * HEAD_DIM, (kh + 1) * HEAD_DIM)
            vs = [r[:, lanes] for r in (vx_ref, vp_ref, vc_ref, vn_ref)]
            heads = [slice((4 * kh + g) * HEAD_DIM, (4 * kh + g + 1) * HEAD_DIM) for g in range(4)]
            do4 = jnp.concatenate([do_ref[:, hs] for hs in heads], axis=0)
            o4 = jnp.concatenate([o_ref[:, hs] for hs in heads], axis=0).astype(F32)
            lse4 = jnp.concatenate([lse_ref[:, 4 * kh + g:4 * kh + g + 1] for g in range(4)], axis=0)
            delta = jnp.sum(do4 * o4, axis=-1, keepdims=True)
            dq4 = jnp.zeros((4 * QB, HEAD_DIM), F32)
            dks, dvs = [], []
            for s_, k_, v_ in zip(ss, ks, vs):
                p = jnp.exp(s_ - lse4)
                ds = p * (_dot(do4, v_, "nt") - delta)
                dq4 = dq4 + _dot(ds, k_)
                dks.append(_dot(ds, q4, "tn"))
                dvs.append(_dot(p, do4, "tn"))
            accc[:, lanes] += dks[0]
            accc[:, 128 + kh * HEAD_DIM:128 + (kh + 1) * HEAD_DIM] += dvs[0]
            for st, dk_, dv_ in zip(starts, dks[1:], dvs[1:]):
                accl[pl.ds(st, QB), lanes] += dk_
                accl[pl.ds(st, QB), 128 + kh * HEAD_DIM:128 + (kh + 1) * HEAD_DIM] += dv_
            dsk = -jnp.exp(snk - lse4) * delta
            for g in range(4):
                h = 4 * kh + g
                dsink_ref[h:h + 1, :] += jnp.broadcast_to(_colsum(dsk[g * QB:(g + 1) * QB]), (1, 128))
                dqs[:, heads[g]] = dq4[g * QB:(g + 1) * QB]
        cos, sin = cosq_ref[...], sinq_ref[...]
        dq_ref[...] = jnp.concatenate(
            [_rope(dqs[:, 128 * t:128 * (t + 1)], cos, sin, -1.0) * (HEAD_DIM ** -0.5) for t in range(4)],
            axis=1).astype(BF16)

        @pl.when(qb == nqb - 1)
        def _():
            dk = _rope(accl[:, 0:128], cosk_ref[...], sink_tab_ref[...], -1.0)
            dkvl_ref[...] = jnp.concatenate([dk, accl[:, 128:256]], axis=1).astype(BF16)
            dkvc_ref[...] = accc[...].astype(BF16)

    rowq = lambda w: pl.BlockSpec((QB, w), lambda s, qb: (qblk(s, qb), 0))
    tabq = pl.BlockSpec((QB, 128), lambda s, qb: (jnp.where(qb < nlb, qb, cfg.seq // QB), 0))
    tabk = pl.BlockSpec((cfg.seq, 128), lambda s, qb: (0, 0))
    return pl.pallas_call(
        body, name=name, grid=(cfg.b, nqb),
        in_specs=[pl.BlockSpec(memory_space=pltpu.SMEM), qspec, *kvspecs, rowq(ATTN_W), rowq(ATTN_W), rowq(N_HEADS),
                  tabq, tabq, tabk, tabk],
        out_specs=[rowq(ATTN_W), pl.BlockSpec((cfg.seq, 256), lambda s, qb: (s, 0)),
                   pl.BlockSpec((cfg.ctx, 256), lambda s, qb: (s, 0)), pl.BlockSpec((N_HEADS, 128), lambda s, qb: (0, 0))],
        out_shape=[jax.ShapeDtypeStruct((cfg.r, ATTN_W), BF16), jax.ShapeDtypeStruct((cfg.nl, 256), BF16),
                   jax.ShapeDtypeStruct((cfg.nc, 256), BF16), jax.ShapeDtypeStruct((N_HEADS, 128), F32)],
        scratch_shapes=[pltpu.VMEM((cfg.seq, 256), F32), pltpu.VMEM((cfg.ctx, 256), F32), pltpu.VMEM((QB, ATTN_W), F32)],
        compiler_params=_params(2))(sink, *([qkv] * 9), mix, dmix, lse, cos_t, sin_t, cos_t, sin_t)


def _pool_geometry(n, c):
    lane = lax.broadcasted_iota(jnp.int32, (1, POOL_W), 1) // HEAD_DIM
    wl = jnp.where(lane == 0, 1, jnp.where(lane == 1, 2, jnp.where(lane == 2, 4, 8)))
    wr = wl - 1
    t = c * CHUNK + lax.broadcasted_iota(jnp.int32, (CHUNK, POOL_W), 0)
    cnt = (jnp.minimum(t + wr, n - 1) - jnp.maximum(t - wl, 0) + 1).astype(F32)
    return wl, wr, cnt


def _build_phases(src, ph, c):
    for s in range(1, 8):
        ph[s - 1] = src[c * CHUNK + s:c * CHUNK + s + CHUNK + 24, :]


def _window(src, ph, c, off):
    a, s = divmod(off, 8)
    if s == 0:
        return src[c * CHUNK + 8 * a:c * CHUNK + 8 * a + CHUNK, :]
    return ph[s - 1, 8 * a:8 * a + CHUNK, :]


def _conv_chunk(hp, ph, dw_ref, dwb_ref, c):
    _build_phases(hp, ph, c)
    acc = jnp.zeros((CHUNK, CONV_W), F32) + dwb_ref[...]
    for j in range(CONV_K):
        acc = acc + dw_ref[j:j + 1, :] * _window(hp, ph, c, j + 1)
    return acc


def _fill_glu(cp_ref, hp, n):
    hp[0:PAD, :] = jnp.zeros((PAD, CONV_W), F32)
    hp[PAD + n:2 * PAD + n, :] = jnp.zeros((PAD, CONV_W), F32)
    for c in range(n // CHUNK):
        rows = slice(c * CHUNK, (c + 1) * CHUNK)
        a = cp_ref[rows, 0:CONV_W].astype(F32)
        g = cp_ref[rows, CONV_W:2 * CONV_W].astype(F32)
        hp[PAD + c * CHUNK:PAD + (c + 1) * CHUNK, :] = a * _sigmoid(g)


def _fill_pool(cp_ref, pp, n):
    pp[0:PAD, :] = jnp.zeros((PAD, POOL_W), F32)
    pp[PAD + n:2 * PAD + n, :] = jnp.zeros((PAD, POOL_W), F32)
    for c in range(n // CHUNK):
        pp[PAD + c * CHUNK:PAD + (c + 1) * CHUNK, :] = cp_ref[c * CHUNK:(c + 1) * CHUNK, 2 * CONV_W:768].astype(F32)


def _pool_chunk(pp, ph, n, c):
    wl, wr, cnt = _pool_geometry(n, c)
    _build_phases(pp, ph, c)
    acc = jnp.zeros((CHUNK, POOL_W), F32)
    for o in range(-8, 8):
        acc = acc + jnp.where(jnp.logical_and(o >= -wl, o <= wr), _window(pp, ph, c, PAD + o), 0.0)
    return acc / cnt - pp[PAD + c * CHUNK:PAD + (c + 1) * CHUNK, :], cnt


def _seq_specs(n, blk_off, width, col=0):
    return pl.BlockSpec((n, width), lambda s: (blk_off + s, col))


def _full(shape):
    return pl.BlockSpec(shape, lambda s: (0,) * len(shape))


_PHASES = pltpu.VMEM((7, CHUNK + 24, CONV_W), F32)


def _convpool_fwd(name, cpin, mix, yconv, prm, n, blk_off, cfg):
    dw, dwb, lng, lnb, wbd, ps = prm
    n_alias = 1 if yconv is None else 2

    def body(*refs):
        cp_ref, dw_ref, dwb_ref, lng_ref, lnb_ref, wbd_ref, ps_ref = refs[:7]
        out_ref, y_ref, hp, pp, ph = refs[7 + n_alias:]
        _fill_glu(cp_ref, hp, n)
        _fill_pool(cp_ref, pp, n)
        for c in range(n // CHUNK):
            rows = slice(c * CHUNK, (c + 1) * CHUNK)
            y = _conv_chunk(hp, ph, dw_ref, dwb_ref, c)
            y_ref[rows, :] = y
            d = y - jnp.mean(y, axis=-1, keepdims=True)
            hn = d * lax.rsqrt(jnp.mean(d * d, axis=-1, keepdims=True) + EPS) * lng_ref[...] + lnb_ref[...]
            out_ref[rows, 0:CONV_W] = (hn * _sigmoid(hn)).astype(BF16)
            yp, _ = _pool_chunk(pp, ph, n, c)
            out_ref[rows, CONV_W:2 * CONV_W] = (_dot(yp, wbd_ref[...]) * ps_ref[...]).astype(BF16)

    through = [mix] if yconv is None else [mix, yconv]
    return pl.pallas_call(
        body, name=name, grid=(cfg.b,),
        in_specs=[_seq_specs(n, blk_off, 768), _full((32, CONV_W)), _full((1, CONV_W)), _full((1, CONV_W)),
                  _full((1, CONV_W)), _full((POOL_W, POOL_W)), _full((1, POOL_W))] + [_ANY] * n_alias,
        out_specs=[_seq_specs(n, blk_off, 512, 1), _seq_specs(n, blk_off, CONV_W)],
        out_shape=[jax.ShapeDtypeStruct((cfg.r, D), BF16), jax.ShapeDtypeStruct((cfg.r, CONV_W), F32)],
        scratch_shapes=[pltpu.VMEM((n + 2 * PAD, CONV_W), F32), pltpu.VMEM((n + 2 * PAD, POOL_W), F32), _PHASES],
        input_output_aliases={7 + i: i for i in range(n_alias)},
        compiler_params=_params(1))(cpin, dw, dwb, lng, lnb, wbd, ps, *through)


_SMALL_SHAPES = [(32, CONV_W), (1, CONV_W), (1, CONV_W), (1, CONV_W), (POOL_W, POOL_W), (1, POOL_W)]


def _convpool_bwd(name, cpin, yconv, dmix, prm, acc_in, n, blk_off, cfg):
    dw, dwb, lng, lnb, wbd, ps = prm
    nch = n // CHUNK

    def body(cp_ref, y_ref, dm_ref, dw_ref, dwb_ref, lng_ref, lnb_ref, wbd_ref, ps_ref, dcp_in,
             a_dw, a_dwb, a_lng, a_lnb, a_wbd, a_ps,
             dcp_ref, o_dw, o_dwb, o_lng, o_lnb, o_wbd, o_ps, hp, dyp, pp, wp, dyv, dwacc, ph):
        s = pl.program_id(0)

        @pl.when(s == 0)
        def _():
            for o_, a_ in ((o_dw, a_dw), (o_dwb, a_dwb), (o_lng, a_lng), (o_lnb, a_lnb), (o_wbd, a_wbd), (o_ps, a_ps)):
                o_[...] = a_[...]
            dwacc[...] = jnp.zeros_like(dwacc)

        _fill_glu(cp_ref, hp, n)
        _fill_pool(cp_ref, pp, n)
        for ref in (dyp, wp):
            ref[0:PAD, :] = jnp.zeros((PAD, CONV_W), F32)
            ref[PAD + n:2 * PAD + n, :] = jnp.zeros((PAD, CONV_W), F32)
        for c in range(nch):
            rows = slice(c * CHUNK, (c + 1) * CHUNK)
            y = y_ref[rows, :]
            d = y - jnp.mean(y, axis=-1, keepdims=True)
            rstd = lax.rsqrt(jnp.mean(d * d, axis=-1, keepdims=True) + EPS)
            xh = d * rstd
            hn = xh * lng_ref[...] + lnb_ref[...]
            sg = _sigmoid(hn)
            dhn = dm_ref[rows, 0:CONV_W] * (sg * (1.0 + hn * (1.0 - sg)))
            o_lnb[...] += _colsum(dhn)
            o_lng[...] += _colsum(dhn * xh)
            dxh = dhn * lng_ref[...]
            dy = rstd * (dxh - jnp.mean(dxh, axis=-1, keepdims=True) - xh * jnp.mean(dxh * xh, axis=-1, keepdims=True))
            o_dwb[...] += _colsum(dy)
            dyp[PAD + c * CHUNK:PAD + (c + 1) * CHUNK, :] = dy
            _build_phases(hp, ph, c)
            for j in range(CONV_K):
                prod = dy * _window(hp, ph, c, j + 1)
                dwacc[8 * j:8 * j + 8, :] += jnp.sum(prod.reshape(CHUNK // 8, 8, CONV_W), axis=0)
            yp, cnt = _pool_chunk(pp, ph, n, c)
            dz = dm_ref[rows, CONV_W:2 * CONV_W]
            o_ps[...] += _colsum(dz * _dot(yp, wbd_ref[...]))
            dzs = dz * ps_ref[...]
            o_wbd[...] += _dot(yp, dzs, "tn")
            dv = _dot(dzs, wbd_ref[...], "nt")
            dyv[rows, :] = dv
            wp[PAD + c * CHUNK:PAD + (c + 1) * CHUNK, :] = dv / cnt
        for c in range(nch):
            rows = slice(c * CHUNK, (c + 1) * CHUNK)
            _build_phases(dyp, ph, c)
            dh = jnp.zeros((CHUNK, CONV_W), F32)
            for j in range(CONV_K):
                dh = dh + dw_ref[j:j + 1, :] * _window(dyp, ph, c, 31 - j)
            a = cp_ref[rows, 0:CONV_W].astype(F32)
            sg = _sigmoid(cp_ref[rows, CONV_W:2 * CONV_W].astype(F32))
            dcp_ref[rows, 0:CONV_W] = (dh * sg).astype(BF16)
            dcp_ref[rows, CONV_W:2 * CONV_W] = (dh * a * sg * (1.0 - sg)).astype(BF16)
            wl, wr, _ = _pool_geometry(n, c)
            _build_phases(wp, ph, c)
            dp = -dyv[rows, :]
            for o in range(-8, 8):
                dp = dp + jnp.where(jnp.logical_and(o >= -wl, o <= wr), _window(wp, ph, c, PAD - o), 0.0)
            dcp_ref[rows, 2 * CONV_W:768] = dp.astype(BF16)

        @pl.when(s == cfg.b - 1)
        def _():
            for j in range(CONV_K):
                o_dw[j:j + 1, :] += _colsum(dwacc[8 * j:8 * j + 8, :])

    small_specs = [_full(sh) for sh in _SMALL_SHAPES]
    return pl.pallas_call(
        body, name=name, grid=(cfg.b,),
        in_specs=[_seq_specs(n, blk_off, 768), _seq_specs(n, blk_off, CONV_W), _seq_specs(n, blk_off, 512, 1),
                  *small_specs, _ANY, *small_specs],
        out_specs=[_seq_specs(n, blk_off, 768), *small_specs],
        out_shape=[jax.ShapeDtypeStruct((cfg.r, 768), BF16)] + [jax.ShapeDtypeStruct(sh, F32) for sh in _SMALL_SHAPES],
        scratch_shapes=[pltpu.VMEM((n + 2 * PAD, CONV_W), F32), pltpu.VMEM((n + 2 * PAD, CONV_W), F32),
                        pltpu.VMEM((n + 2 * PAD, POOL_W), F32), pltpu.VMEM((n + 2 * PAD, POOL_W), F32),
                        pltpu.VMEM((n, POOL_W), F32), pltpu.VMEM((8 * 32, CONV_W), F32), _PHASES],
        input_output_aliases={9: 0}, compiler_params=_params(1))(cpin, yconv, dmix, dw, dwb, lng, lnb, wbd, ps, *acc_in)


def _ffn_in(name, h, w_ffn_in, layer, nblk, cfg):
    def body(h_ref, wg_ref, wu_ref, gu_ref, act_ref):
        hv = h_ref[...]
        g = _dot(hv, wg_ref[...])
        u = _dot(hv, wu_ref[...])
        gu_ref[0] = g.astype(BF16)
        gu_ref[1] = u.astype(BF16)
        act_ref[...] = (g * _sigmoid(g) * u).astype(BF16)

    wspec = lambda base: pl.BlockSpec((None, None, D, HALF_FF), lambda j, i: (layer, base + j, 0, 0))
    return pl.pallas_call(
        body, name=name, grid=(2, nblk),
        in_specs=[pl.BlockSpec((TM, D), lambda j, i: (i, 0)), wspec(0), wspec(2)],
        out_specs=[pl.BlockSpec((2, TM, HALF_FF), lambda j, i: (0, i, j)), pl.BlockSpec((TM, HALF_FF), lambda j, i: (i, j))],
        out_shape=[jax.ShapeDtypeStruct((2, cfg.r, D_FF), BF16), jax.ShapeDtypeStruct((cfg.r, D_FF), BF16)],
        compiler_params=_params(2))(h, w_ffn_in, w_ffn_in)


def _row_block(rows, cols, max_bytes=1 << 20):
    best = 16
    for t in range(16, rows + 1, 16):
        if rows % t == 0 and t * cols * 4 <= max_bytes:
            best = t
    assert rows % best == 0
    return best


def _pair_add(name, own32, recv, c_idx):
    _, _, s0, s1 = own32.shape
    tr = _row_block(s0, s1)

    def body(c_ref, a_ref, b_ref, o_ref):
        o_ref[...] = (a_ref[...] + b_ref[...].astype(F32)).astype(BF16)

    grid_spec = pltpu.PrefetchScalarGridSpec(
        num_scalar_prefetch=1, grid=(N_SHARD * s0 // tr,),
        in_specs=[pl.BlockSpec((None, tr, s1), lambda i, c: (c[0], i, 0)), pl.BlockSpec((tr, s1), lambda i, c: (i, 0))],
        out_specs=pl.BlockSpec((tr, s1), lambda i, c: (i, 0)))
    out = pl.pallas_call(body, name=name, grid_spec=grid_spec, out_shape=jax.ShapeDtypeStruct((N_SHARD * s0, s1), BF16),
                         compiler_params=_params(1))(c_idx, own32.reshape(2, N_SHARD * s0, s1), recv.reshape(N_SHARD * s0, s1))
    return out.reshape(N_SHARD, s0, s1)


def _shard_sum(name, pair_sum, recv, jc_idx):
    _, s0, s1 = pair_sum.shape
    tr = _row_block(s0, s1)

    def body(jc_ref, a_ref, b_ref, o_ref):
        o_ref[...] = ((a_ref[...].astype(F32) + b_ref[0].astype(F32)) + b_ref[1].astype(F32)) + b_ref[2].astype(F32)

    grid_spec = pltpu.PrefetchScalarGridSpec(
        num_scalar_prefetch=1, grid=(s0 // tr,),
        in_specs=[pl.BlockSpec((None, tr, s1), lambda i, jc: (jc[0], i, 0)), pl.BlockSpec((3, tr, s1), lambda i, jc: (0, i, 0))],
        out_specs=pl.BlockSpec((None, tr, s1), lambda i, jc: (jc[1], i, 0)))
    return pl.pallas_call(body, name=name, grid_spec=grid_spec, out_shape=jax.ShapeDtypeStruct((2, s0, s1), F32),
                          compiler_params=_params(1))(jc_idx, pair_sum, recv)


def _adamw_math(w, g, m, v):
    m = ADAM_B1 * m + (1.0 - ADAM_B1) * g
    v = ADAM_B2 * v + (1.0 - ADAM_B2) * (g * g)
    m_hat = m / (1.0 - ADAM_B1 ** ADAM_STEP)
    v_hat = v / (1.0 - ADAM_B2 ** ADAM_STEP)
    delta = -ADAM_LR * (m_hat / (jnp.sqrt(v_hat) + ADAM_EPS) + ADAM_WD * w)
    return delta, m, v


def _adamw(name, w, g, m, v):
    rows, cols = w.shape
    tr = rows if rows % 16 else _row_block(rows, cols, 1 << 19)

    def body(w_ref, g_ref, m_ref, v_ref, d_ref, mo_ref, vo_ref):
        d, mn, vn = _adamw_math(w_ref[...], g_ref[...], m_ref[...], v_ref[...])
        d_ref[...] = d
        mo_ref[...] = mn
        vo_ref[...] = vn

    spec = pl.BlockSpec((tr, cols), lambda i: (i, 0))
    shape = jax.ShapeDtypeStruct((rows, cols), F32)
    return pl.pallas_call(body, name=name, grid=(rows // tr,), in_specs=[spec] * 4, out_specs=[spec] * 3,
                          out_shape=[shape] * 3, compiler_params=_params(1))(w, g, m, v)


def _position():
    return lax.axis_index("x"), lax.axis_index("y"), lax.axis_index("c")


def _other_chips(x, y):
    return [(1 - x, y), (x, 1 - y), (1 - x, 1 - y)]


def _gather_weights(shards):
    n = len(shards)

    def body(*refs):
        ins, outs, bufs = refs[:n], refs[n:2 * n], refs[2 * n:3 * n]
        ssem, rsem, lsem = refs[3 * n:]
        x, y, c = _position()
        me = 2 * x + y
        sibling = (x, y, 1 - c)
        chips = _other_chips(x, y)
        sent = []
        for a, (src, dst) in enumerate(zip(ins, outs)):
            for k, (px, py) in enumerate(chips):
                cp = pltpu.make_async_remote_copy(src.at[c], dst.at[c, me], ssem.at[3 * a + k], rsem.at[3 * a + k],
                                                  device_id=(px, py, c), device_id_type=_MESH)
                cp.start()
                sent.append(cp)
        loads = []
        for a, (src, buf) in enumerate(zip(ins, bufs)):
            ld = pltpu.make_async_copy(src, buf, lsem.at[2 * a])
            ld.start()
            loads.append(ld)
        for a, (ld, buf, dst) in enumerate(zip(loads, bufs, outs)):
            ld.wait()
            for l in range(2):
                st = pltpu.make_async_copy(buf.at[l], dst.at[l, me], lsem.at[2 * a + 1])
                st.start()
                st.wait()
        for a, (src, dst) in enumerate(zip(ins, outs)):
            for k, (px, py) in enumerate(chips):
                j = 2 * px + py
                pltpu.make_async_remote_copy(src.at[c], dst.at[c, j], ssem.at[3 * a + k], rsem.at[3 * a + k],
                                             device_id=(px, py, c), device_id_type=_MESH).wait_recv()
                fw = pltpu.make_async_remote_copy(dst.at[c, j], dst.at[c, j], ssem.at[3 * n + 3 * a + k],
                                                  rsem.at[3 * n + 3 * a + k], device_id=sibling, device_id_type=_MESH)
                fw.start()
                sent.append(fw)
        for a, dst in enumerate(outs):
            for k, (px, py) in enumerate(chips):
                j = 2 * px + py
                pltpu.make_async_remote_copy(dst.at[1 - c, j], dst.at[1 - c, j], ssem.at[3 * n + 3 * a + k],
                                             rsem.at[3 * n + 3 * a + k], device_id=sibling, device_id_type=_MESH).wait_recv()
        for cp in sent:
            cp.wait_send()

    return pl.pallas_call(
        body, name="gather_weights", in_specs=[_ANY] * n, out_specs=[_ANY] * n,
        out_shape=[jax.ShapeDtypeStruct((2, N_SHARD) + s.shape[1:], s.dtype) for s in shards],
        scratch_shapes=[pltpu.VMEM(s.shape, s.dtype) for s in shards]
        + [pltpu.SemaphoreType.DMA((6 * n,)), pltpu.SemaphoreType.DMA((6 * n,)), pltpu.SemaphoreType.DMA((2 * n,))],
        compiler_params=pltpu.CompilerParams(vmem_limit_bytes=VMEM_LIMIT_V7X))(*shards)


def _comm(name, ins, out_shape, n_remote, plan):
    n_in, n_out = len(ins), len(out_shape)

    def body(*refs):
        plan(refs[:n_in], refs[n_in:n_in + n_out], *refs[n_in + n_out:])

    return pl.pallas_call(
        body, name=name, in_specs=[_ANY] * n_in, out_specs=[_ANY] * n_out, out_shape=out_shape,
        scratch_shapes=[pltpu.SemaphoreType.DMA((n_remote,)), pltpu.SemaphoreType.DMA((n_remote,))])(*ins)


def _send_other_layer(grads_bf):
    n = len(grads_bf)

    def plan(ins, outs, ssem, rsem):
        x, y, c = _position()
        started = []
        for a, (src, dst) in enumerate(zip(ins, outs)):
            cp = pltpu.make_async_remote_copy(src.at[1 - c], dst, ssem.at[a], rsem.at[a], device_id=(x, y, 1 - c),
                                              device_id_type=_MESH)
            cp.start()
            started.append(cp)
        for cp in started:
            cp.wait_recv()
        for cp in started:
            cp.wait_send()

    shapes = [jax.ShapeDtypeStruct(s.shape[1:], s.dtype) for s in grads_bf]
    return _comm("send_other_layer", grads_bf, shapes, n, plan)


def _exchange_shards(pair_sums):
    n = len(pair_sums)

    def plan(ins, outs, ssem, rsem):
        x, y, c = _position()
        started = []
        for a, (src, dst) in enumerate(zip(ins, outs)):
            for k, (px, py) in enumerate(_other_chips(x, y)):
                cp = pltpu.make_async_remote_copy(src.at[2 * px + py], dst.at[k], ssem.at[3 * a + k], rsem.at[3 * a + k],
                                                  device_id=(px, py, c), device_id_type=_MESH)
                cp.start()
                started.append(cp)
        for cp in started:
            cp.wait_recv()
        for cp in started:
            cp.wait_send()

    shapes = [jax.ShapeDtypeStruct((3,) + s.shape[1:], s.dtype) for s in pair_sums]
    return _comm("exchange_shards", pair_sums, shapes, 3 * n, plan)


def _swap_reduced(grads):
    n = len(grads)

    def body(*refs):
        ins, outs, ssem, rsem = refs[:n], refs[n:2 * n], refs[2 * n], refs[2 * n + 1]
        x, y, c = _position()
        sent = []
        for a, (src, dst) in enumerate(zip(ins, outs)):
            cp = pltpu.make_async_remote_copy(src.at[c], dst.at[c], ssem.at[a], rsem.at[a], device_id=(x, y, 1 - c),
                                              device_id_type=_MESH)
            cp.start()
            sent.append(cp)
        for a, (src, dst) in enumerate(zip(ins, outs)):
            pltpu.make_async_remote_copy(src.at[1 - c], dst.at[1 - c], ssem.at[a], rsem.at[a], device_id=(x, y, 1 - c),
                                         device_id_type=_MESH).wait_recv()
        for cp in sent:
            cp.wait_send()

    return pl.pallas_call(
        body, name="swap_reduced", in_specs=[_ANY] * n, out_specs=[_ANY] * n,
        out_shape=[jax.ShapeDtypeStruct(g.shape, g.dtype) for g in grads],
        scratch_shapes=[pltpu.SemaphoreType.DMA((n,)), pltpu.SemaphoreType.DMA((n,))],
        input_output_aliases={a: a for a in range(n)})(*grads)


def _sum_small(pack):
    p = pack.shape[0]
    flips = [(dx, dy, dc) for dx in (0, 1) for dy in (0, 1) for dc in (0, 1) if dx + dy + dc]

    def body(in_ref, out_ref, buf, ssem, rsem):
        x, y, c = _position()
        me = 4 * x + 2 * y + c
        buf[me] = in_ref[...]
        started = []
        for k, (dx, dy, dc) in enumerate(flips):
            peer = ((x + dx) % 2, (y + dy) % 2, (c + dc) % 2)
            cp = pltpu.make_async_remote_copy(in_ref, buf.at[me], ssem.at[k], rsem.at[k], device_id=peer,
                                              device_id_type=_MESH)
            cp.start()
            started.append(cp)
        for k, (dx, dy, dc) in enumerate(flips):
            peer = ((x + dx) % 2, (y + dy) % 2, (c + dc) % 2)
            pltpu.make_async_remote_copy(in_ref, buf.at[4 * peer[0] + 2 * peer[1] + peer[2]], ssem.at[k], rsem.at[k],
                                         device_id=peer, device_id_type=_MESH).wait_recv()
        for cp in started:
            cp.wait_send()
        acc = buf[0]
        for d in range(1, 8):
            acc = acc + buf[d]
        out_ref[...] = acc

    vm = pl.BlockSpec(memory_space=pltpu.VMEM)
    return pl.pallas_call(
        body, name="sum_small", in_specs=[vm], out_specs=vm, out_shape=jax.ShapeDtypeStruct((p, PACK_LANES), F32),
        scratch_shapes=[pltpu.VMEM((8, p, PACK_LANES), F32), pltpu.SemaphoreType.DMA((7,)), pltpu.SemaphoreType.DMA((7,))],
        compiler_params=pltpu.CompilerParams(vmem_limit_bytes=VMEM_LIMIT_V7X))(pack)


def _pack(arrays):
    flat = jnp.concatenate([a.reshape(-1).astype(F32) for a in arrays])
    total = flat.shape[0]
    rows = -(-total // (8 * PACK_LANES)) * 8
    return jnp.pad(flat, (0, rows * PACK_LANES - total)).reshape(rows, PACK_LANES)


def _unpack(pack, shapes):
    flat, out, pos = pack.reshape(-1), [], 0
    for sh in shapes:
        size = int(np.prod(sh)) if len(sh) else 1
        out.append(flat[pos:pos + size].reshape(sh))
        pos += size
    return out


def _block_diag(pw):
    out = jnp.zeros((POOL_W, POOL_W), pw.dtype)
    for g in range(4):
        out = out.at[g * 64:(g + 1) * 64, g * 64:(g + 1) * 64].set(pw[g])
    return out


def _local_step(x, c, ctx, c_ctx, small, wf, loss_target):
    cfg = _Cfg(x.shape[0], x.shape[1], ctx.shape[1])
    assert cfg.seq % TM == 0 and cfg.nc % TM == 0 and cfg.seq % cfg.ctx == 0 and cfg.ctx % CHUNK == 0
    nb_all, nb_lat = cfg.r // TM, cfg.nl // TM
    last = 1
    cos_t, sin_t = _rope_tables(cfg.seq)
    xs = jnp.concatenate([x.reshape(cfg.nl, D), ctx.reshape(cfg.nc, D)], axis=0)
    cc = jnp.concatenate([c, c_ctx[None, :], jnp.zeros((MOD_ROWS - cfg.b - 1, D), F32)], axis=0)
    w_out_v = wf["w_out"].reshape(2, D, D)
    w_ffo_v = wf["w_ffn_out"].reshape(2, D_FF, D)
    row = lambda w: pl.BlockSpec((TM, w), lambda i, j, k: (i, 0))
    mod3_spec = pl.BlockSpec((None, 6, D), lambda i, j, k: (cfg.mod_row(i), 0, 0))
    mod_rows_spec = pl.BlockSpec((MOD_ROWS, D), lambda i, j, k: (0, 0))
    mod_w_spec = lambda l, shard: pl.BlockSpec((None, None, D, MOD_W), lambda i, j, k: (l, shard(j, k), 0, 0))

    def conv_params(l):
        dw = jnp.pad(small["conv_dw"][l], ((0, 1), (0, 0)))
        return (dw, small["conv_dw_b"][l][None], small["conv_ln_g"][l][None], small["conv_ln_b"][l][None],
                _block_diag(small["pool_w"][l]).astype(BF16), small["pool_scale"][l][None])

    def residual_epi(ig):
        def epi(acc, ex, outs):
            x_ref, m_ref = ex
            outs[0][...] = x_ref[...] + m_ref[ig:ig + 1, :] * acc
            outs[1][...] = acc.astype(BF16)
        return epi

    def bias_epi(acc, ex, outs):
        outs[0][...] = acc + ex[0][...]

    saved = []
    for l in range(2):
        nb = nb_lat if l == last else nb_all
        mvec = _mm(f"mod_fwd{l}", "nn", (1, N_SHARD, 1), cc, wf["w_mod"], mod_rows_spec, mod_w_spec(l, lambda j, k: j),
                   [jax.ShapeDtypeStruct((MOD_ROWS, 6 * D), F32)], [pl.BlockSpec((MOD_ROWS, MOD_W), lambda i, j, k: (0, j))],
                   extras=[small["b_mod"][l][None]], extra_specs=[pl.BlockSpec((1, MOD_W), lambda i, j, k: (0, j))],
                   a_fn=_silu, epi=bias_epi)[0]
        mod3 = mvec.reshape(MOD_ROWS, 6, D)
        h1 = _norm_fwd(f"norm1_fwd{l}", xs, small["norm1_g"][l][None], mod3, 0, 1, nb_all, cfg)
        qkv, cpin = _in_proj(f"in_proj{l}", h1, wf["w_in"], l, cos_t, sin_t, cfg)
        mix, lse = _att_fwd(f"att_fwd{l}", qkv, small["attn_sink"][l], l != last, cfg)
        prm = conv_params(l)
        mix, yconv = _convpool_fwd(f"convpool_fwd_lat{l}", cpin, mix, None, prm, cfg.seq, 0, cfg)
        if l != last:
            mix, yconv = _convpool_fwd(f"convpool_fwd_ctx{l}", cpin, mix, yconv, prm, cfg.ctx, cfg.nl // cfg.ctx, cfg)
        x1, y1 = _mm(f"out_proj{l}", "nn", (nb, 1, 1), mix, w_out_v, row(D),
                     pl.BlockSpec((None, D, D), lambda i, j, k: (l, 0, 0)),
                     [jax.ShapeDtypeStruct((cfg.r, D), F32), jax.ShapeDtypeStruct((cfg.r, D), BF16)], [row(D), row(D)],
                     extras=[xs, mod3], extra_specs=[row(D), mod3_spec], epi=residual_epi(2))
        h2 = _norm_fwd(f"norm2_fwd{l}", x1, small["norm2_g"][l][None], mod3, 3, 4, nb, cfg)
        gu, act = _ffn_in(f"ffn_in{l}", h2, wf["w_ffn_in"], l, nb, cfg)
        x2, y2 = _mm(f"ffn_out{l}", "nn", (nb, 1, 1), act, w_ffo_v, row(D_FF),
                     pl.BlockSpec((None, D_FF, D), lambda i, j, k: (l, 0, 0)),
                     [jax.ShapeDtypeStruct((cfg.r, D), F32), jax.ShapeDtypeStruct((cfg.r, D), BF16)], [row(D), row(D)],
                     extras=[x1, mod3], extra_specs=[row(D), mod3_spec], epi=residual_epi(5))
        saved.append(dict(mod3=mod3, x0=xs, h1=h1, qkv=qkv, cpin=cpin, mix=mix, yconv=yconv, lse=lse, y1=y1, x1=x1,
                          h2=h2, gu=gu, act=act, y2=y2, prm=prm))
        xs = x2

    dx, loss, d_final_g = _loss_head(xs, loss_target.reshape(cfg.nl, D), small["final_g"][None], cfg)

    big = {k: None for k in ("w_mod", "w_in", "w_out", "w_ffn_in", "w_ffn_out")}
    sg = {k: [None, None] for k in ("b_mod", "norm1_g", "norm2_g", "conv_dw", "conv_dw_b", "conv_ln_g", "conv_ln_b",
                                    "attn_sink", "pool_w", "pool_scale")}
    d_c_ctx = jnp.zeros((D,), F32)

    def swiglu_bwd_epi(acc, ex, outs):
        g = ex[0][0].astype(F32)
        u = ex[0][1].astype(F32)
        s = _sigmoid(g)
        outs[0][0] = (acc * u * (s * (1.0 + g * (1.0 - s)))).astype(BF16)
        outs[0][1] = (acc * (g * s)).astype(BF16)

    def split_shards_epi(acc, ex, outs):
        for o in outs[:2]:
            for j in range(N_SHARD):
                o[j] = acc[:, j * IN_SHARD:(j + 1) * IN_SHARD].astype(o.dtype)

    def mod_dw_epi(acc, ex, outs):
        _epi_store(acc, ex, outs[:2])
        outs[2][...] = _colsum(ex[0][...])

    def dsilu_epi(acc, ex, outs):
        outs[0][...] = acc * _dsilu(ex[0][...])

    for l in (1, 0):
        sv = saved[l]
        mod3 = sv["mod3"]
        nb = nb_lat if l == last else nb_all
        tr = _dw_rows(nb * TM)
        steps = nb * TM // tr
        dy2, dg2 = _gate_bwd(f"gate2_bwd{l}", dx, sv["y2"], mod3, 5, nb, cfg)
        gu_spec = pl.BlockSpec((2, TM, HALF_FF), lambda i, j, k: (0, i, j))
        df = _mm(f"ffn_out_bwd{l}", "nt", (nb, 2, 1), dy2, w_ffo_v, row(D),
                 pl.BlockSpec((None, HALF_FF, D), lambda i, j, k: (l, j, 0)),
                 [jax.ShapeDtypeStruct((2, cfg.r, D_FF), BF16)], [gu_spec], extras=[sv["gu"]], extra_specs=[gu_spec],
                 epi=swiglu_bwd_epi)[0]
        big["w_ffn_out"] = _mm_dw(f"dw_ffn_out{l}", sv["act"], dy2,
                                  pl.BlockSpec((tr, HALF_FF), lambda i, j, k: (k, j)), pl.BlockSpec((tr, D), lambda i, j, k: (k, 0)),
                                  (2, D_FF, D), pl.BlockSpec((None, HALF_FF, D), lambda i, j, k: (l, j, 0)), 2,
                                  (HALF_FF, D), big["w_ffn_out"], steps)[:2]
        big["w_ffn_in"] = _mm_dw(f"dw_ffn_in{l}", sv["h2"], df,
                                 pl.BlockSpec((tr, D), lambda i, j, k: (k, 0)),
                                 pl.BlockSpec((None, tr, HALF_FF), lambda i, j, k: (j // 2, k, j % 2)),
                                 (2, N_SHARD, D, HALF_FF), pl.BlockSpec((None, None, D, HALF_FF), lambda i, j, k: (l, j, 0, 0)),
                                 N_SHARD, (D, HALF_FF), big["w_ffn_in"], steps)[:2]
        dh2 = _proj_bwd(f"ffn_in_bwd{l}", df, pl.BlockSpec((2, TM, D_FF), lambda i: (0, i, 0)),
                        lambda a_ref, j: a_ref[j // 2, :, (j % 2) * HALF_FF:(j % 2 + 1) * HALF_FF], wf["w_ffn_in"], l, nb, cfg)
        dx1, dsh2, dsc2, dn2 = _norm_bwd(f"norm2_bwd{l}", sv["x1"], dh2, dx, small["norm2_g"][l][None], mod3, 4, nb, False, cfg)
        dy1, dg1 = _gate_bwd(f"gate1_bwd{l}", dx1, sv["y1"], mod3, 2, nb, cfg)
        dmix = _mm(f"out_proj_bwd{l}", "nt", (nb, 1, 1), dy1, w_out_v, row(D),
                   pl.BlockSpec((None, D, D), lambda i, j, k: (l, 0, 0)),
                   [jax.ShapeDtypeStruct((cfg.r, D), F32)], [row(D)])[0]
        big["w_out"] = _mm_dw(f"dw_out{l}", sv["mix"], dy1, pl.BlockSpec((tr, D), lambda i, j, k: (k, 0)),
                              pl.BlockSpec((tr, D), lambda i, j, k: (k, 0)), (2, D, D),
                              pl.BlockSpec((None, D, D), lambda i, j, k: (l, 0, 0)), 1, (D, D), big["w_out"], steps)[:2]
        dq, dkvl, dkvc, dsink = _att_bwd(f"att_bwd{l}", sv["qkv"], sv["mix"], dmix, sv["lse"], small["attn_sink"][l],
                                         cos_t, sin_t, l != last, cfg)
        acc = [jnp.zeros((cfg.r, 768), BF16)] + [jnp.zeros(sh, F32) for sh in _SMALL_SHAPES]
        acc = _convpool_bwd(f"convpool_bwd_lat{l}", sv["cpin"], sv["yconv"], dmix, sv["prm"], acc, cfg.seq, 0, cfg)
        if l != last:
            acc = _convpool_bwd(f"convpool_bwd_ctx{l}", sv["cpin"], sv["yconv"], dmix, sv["prm"], acc, cfg.ctx,
                                cfg.nl // cfg.ctx, cfg)
        dcp, g_dw, g_dwb, g_lng, g_lnb, g_wbd, g_ps = acc
        sg["attn_sink"][l] = dsink[:, 0]
        sg["conv_dw"][l], sg["conv_dw_b"][l], sg["conv_ln_g"][l], sg["conv_ln_b"][l] = g_dw[:CONV_K], g_dwb[0], g_lng[0], g_lnb[0]
        sg["pool_w"][l] = jnp.stack([g_wbd[g * 64:(g + 1) * 64, g * 64:(g + 1) * 64] for g in range(4)])
        sg["pool_scale"][l] = g_ps[0]
        if l == last:
            dq = jnp.concatenate([dq[:cfg.nl], jnp.zeros((cfg.nc, ATTN_W), BF16)], axis=0)
        du = jnp.concatenate([dq, jnp.concatenate([dkvl, dkvc], axis=0), dcp], axis=1)
        tr_all = _dw_rows(cfg.r)
        big["w_in"] = _mm_dw(f"dw_in{l}", sv["h1"], du, pl.BlockSpec((tr_all, D), lambda i, j, k: (k, 0)),
                             pl.BlockSpec((tr_all, IN_W), lambda i, j, k: (k, 0)), (2, N_SHARD, D, IN_SHARD),
                             pl.BlockSpec((None, N_SHARD, D, IN_SHARD), lambda i, j, k: (l, 0, 0, 0)), 1,
                             (D, IN_W), big["w_in"], cfg.r // tr_all, epi=split_shards_epi)[:2]
        dh1 = _proj_bwd(f"in_proj_bwd{l}", du, pl.BlockSpec((TM, IN_W), lambda i: (i, 0)),
                        lambda a_ref, j: a_ref[:, j * IN_SHARD:(j + 1) * IN_SHARD], wf["w_in"], l, nb_all, cfg)
        dx, dsh1, dsc1, dn1 = _norm_bwd(f"norm1_bwd{l}", sv["x0"], dh1, dx1, small["norm1_g"][l][None], mod3, 1, nb_all,
                                        l == last, cfg)
        sg["norm1_g"][l], sg["norm2_g"][l] = dn1[0], dn2[0]
        parts = [dsh1, dsc1, dg1, dsh2, dsc2, dg2]
        dm = jnp.concatenate([t[:cfg.b, 0, :] for t in parts], axis=1)
        live = (0, 1) if l == last else range(6)
        dm_ctx = jnp.concatenate([t[cfg.b, 0, :] if i in live else jnp.zeros((D,), F32) for i, t in enumerate(parts)])
        dm = jnp.concatenate([dm, dm_ctx[None, :], jnp.zeros((MOD_ROWS - cfg.b - 1, 6 * D), F32)], axis=0)
        dm_spec = pl.BlockSpec((MOD_ROWS, MOD_W), lambda i, j, k: (0, j))
        res = _mm_dw(f"dw_mod{l}", cc, dm, mod_rows_spec, dm_spec, (2, N_SHARD, D, MOD_W), mod_w_spec(l, lambda j, k: j),
                     N_SHARD, None, big["w_mod"], 1, a_fn=_silu, extras=[dm], extra_specs=[dm_spec], epi=mod_dw_epi,
                     extra_out=[jax.ShapeDtypeStruct((1, 6 * D), F32)],
                     extra_out_specs=[pl.BlockSpec((1, MOD_W), lambda i, j, k: (0, j))])
        big["w_mod"], sg["b_mod"][l] = res[:2], res[2][0]
        dcc = _mm(f"mod_bwd{l}", "nt", (1, 1, N_SHARD), dm, wf["w_mod"], pl.BlockSpec((MOD_ROWS, MOD_W), lambda i, j, k: (0, k)),
                  mod_w_spec(l, lambda j, k: k), [jax.ShapeDtypeStruct((MOD_ROWS, D), F32)], [mod_rows_spec],
                  acc_shape=(MOD_ROWS, D), extras=[cc], extra_specs=[mod_rows_spec], epi=dsilu_epi)[0]
        d_c_ctx = d_c_ctx + dcc[cfg.b]

    grad_x = dx[:cfg.nl].reshape(x.shape)
    small_grads = {k: jnp.stack(v) for k, v in sg.items()}
    small_grads["c_ctx"] = d_c_ctx
    small_grads["final_g"] = d_final_g[0]
    return loss, grad_x, small_grads, big


_BIG = ("w_mod", "w_in", "w_out", "w_ffn_in", "w_ffn_out")
_SMALL = ("c_ctx", "b_mod", "norm1_g", "norm2_g", "conv_dw", "conv_dw_b", "conv_ln_g", "conv_ln_b", "attn_sink",
          "pool_w", "pool_scale", "final_g")
_ORDER = ("c_ctx", "w_mod", "b_mod", "norm1_g", "norm2_g", "w_in", "conv_dw", "conv_dw_b", "conv_ln_g", "conv_ln_b",
          "attn_sink", "pool_w", "pool_scale", "w_out", "w_ffn_in", "w_ffn_out", "final_g")


def kernel(x, c, ctx, c_ctx, w_mod, b_mod, norm1_g, norm2_g, w_in, conv_dw, conv_dw_b, conv_ln_g, conv_ln_b, attn_sink, pool_w, pool_scale, w_out, w_ffn_in, w_ffn_out, final_g, loss_target, m_c_ctx, m_w_mod, m_b_mod, m_norm1_g, m_norm2_g, m_w_in, m_conv_dw, m_conv_dw_b, m_conv_ln_g, m_conv_ln_b, m_attn_sink, m_pool_w, m_pool_scale, m_w_out, m_w_ffn_in, m_w_ffn_out, m_final_g, v_c_ctx, v_w_mod, v_b_mod, v_norm1_g, v_norm2_g, v_w_in, v_conv_dw, v_conv_dw_b, v_conv_ln_g, v_conv_ln_b, v_attn_sink, v_pool_w, v_pool_scale, v_w_out, v_w_ffn_in, v_w_ffn_out, v_final_g):
    w = dict(c_ctx=c_ctx, w_mod=w_mod, b_mod=b_mod, norm1_g=norm1_g, norm2_g=norm2_g, w_in=w_in, conv_dw=conv_dw,
             conv_dw_b=conv_dw_b, conv_ln_g=conv_ln_g, conv_ln_b=conv_ln_b, attn_sink=attn_sink, pool_w=pool_w,
             pool_scale=pool_scale, w_out=w_out, w_ffn_in=w_ffn_in, w_ffn_out=w_ffn_out, final_g=final_g)
    m = dict(c_ctx=m_c_ctx, w_mod=m_w_mod, b_mod=m_b_mod, norm1_g=m_norm1_g, norm2_g=m_norm2_g, w_in=m_w_in,
             conv_dw=m_conv_dw, conv_dw_b=m_conv_dw_b, conv_ln_g=m_conv_ln_g, conv_ln_b=m_conv_ln_b,
             attn_sink=m_attn_sink, pool_w=m_pool_w, pool_scale=m_pool_scale, w_out=m_w_out, w_ffn_in=m_w_ffn_in,
             w_ffn_out=m_w_ffn_out, final_g=m_final_g)
    v = dict(c_ctx=v_c_ctx, w_mod=v_w_mod, b_mod=v_b_mod, norm1_g=v_norm1_g, norm2_g=v_norm2_g, w_in=v_w_in,
             conv_dw=v_conv_dw, conv_dw_b=v_conv_dw_b, conv_ln_g=v_conv_ln_g, conv_ln_b=v_conv_ln_b,
             attn_sink=v_attn_sink, pool_w=v_pool_w, pool_scale=v_pool_scale, w_out=v_w_out, w_ffn_in=v_w_ffn_in,
             w_ffn_out=v_w_ffn_out, final_g=v_final_g)
    xi, yi, ci = _position()
    c_idx = jnp.reshape(ci, (1,)).astype(jnp.int32)
    jc_idx = jnp.stack([2 * xi + yi, ci]).astype(jnp.int32)

    dw_pad = jnp.pad(conv_dw, ((0, 0), (0, 1), (0, 64)))
    both = _gather_weights([w[k].astype(BF16) for k in _BIG] + [dw_pad])
    wf = dict(zip(_BIG, both[:5]))
    dw_full = jnp.transpose(both[5][:, :, :CONV_K, :64], (0, 2, 1, 3)).reshape(2, CONV_K, CONV_W)
    small = {k: w[k] for k in _SMALL}
    small["conv_dw"] = dw_full

    loss, grad_x, sgrads, big = _local_step(x, c, ctx, c_ctx, small, wf, loss_target)

    shapes = {k: w[k].shape[1:] for k in _BIG}
    g32 = [big[k][0].reshape((2, N_SHARD) + shapes[k]) for k in _BIG]
    gbf = [big[k][1].reshape((2, N_SHARD) + shapes[k]) for k in _BIG]
    other = _send_other_layer(gbf)
    pair = [_pair_add(f"pair_add_{k}", a, b, c_idx) for k, a, b in zip(_BIG, g32, other)]
    recv = _exchange_shards(pair)
    mine = [_shard_sum(f"shard_sum_{k}", a, b, jc_idx) for k, a, b in zip(_BIG, pair, recv)]
    gfull = dict(zip(_BIG, _swap_reduced(mine)))

    names = list(_SMALL)
    total = _sum_small(_pack([loss] + [sgrads[k] for k in names]))
    parts = _unpack(total, [()] + [sgrads[k].shape for k in names])
    loss_out = parts[0]
    gsmall = dict(zip(names, parts[1:]))
    gsmall["conv_dw"] = lax.dynamic_slice_in_dim(gsmall["conv_dw"], (2 * xi + yi) * 64, 64, axis=2)

    grads, delta, new_m, new_v = dict(gfull), {}, {}, {}
    grads.update(gsmall)
    for k in _BIG:
        s0, s1 = shapes[k]
        flat = lambda a: a.reshape(2 * s0, s1)
        d_, m_, v_ = _adamw(f"adamw_{k}", flat(w[k]), flat(gfull[k]), flat(m[k]), flat(v[k]))
        delta[k], new_m[k], new_v[k] = d_.reshape(w[k].shape), m_.reshape(w[k].shape), v_.reshape(w[k].shape)
    d_, m_, v_ = _adamw("adamw_small", _pack([w[k] for k in names]), _pack([gsmall[k] for k in names]),
                        _pack([m[k] for k in names]), _pack([v[k] for k in names]))
    sshapes = [w[k].shape for k in names]
    for k, a, b, e in zip(names, _unpack(d_, sshapes), _unpack(m_, sshapes), _unpack(v_, sshapes)):
        delta[k], new_m[k], new_v[k] = a, b, e
    return (loss_out, grad_x, *[grads[k] for k in _ORDER], *[delta[k] for k in _ORDER],
            *[new_m[k] for k in _ORDER], *[new_v[k] for k in _ORDER])
```

```python
from typing import NamedTuple

import jax
import jax.numpy as jnp
import numpy as np
from jax import lax
from jax.experimental import pallas as pl
from jax.experimental.pallas import tpu as pltpu

F32 = jnp.float32
BF16 = jnp.bfloat16

D = 1024
GRID_W = 64
HEAD_DIM = 64
N_HEADS = 8
ATTN_W = 512
CONV_W = 256
POOL_W = 256
IN_W = 1536
D_FF = 2816
CONV_K = 31
QB = 128
ROPE_BASE = 10000.0
EPS = 1e-6
NEG = -1e30
N_SHARD = 4
IN_SHARD = IN_W // N_SHARD
HALF_FF = D_FF // 2
MOD_W = 6 * D // N_SHARD
MOD_ROWS = 16
PACK_LANES = 128

ADAM_LR = 0.001
ADAM_B1 = 0.9
ADAM_B2 = 0.999
ADAM_EPS = 1e-08
ADAM_WD = 0.01
ADAM_STEP = 10

VMEM_LIMIT_V7X = 56 * 1024 * 1024
TM = 512
TR_MAX = 1024
CHUNK = 256
PAD = 16

_MESH = pl.DeviceIdType.MESH
_ANY = pl.BlockSpec(memory_space=pl.ANY)
_DIMS = {"nn": (((1,), (0,)), ((), ())), "nt": (((1,), (1,)), ((), ())), "tn": (((0,), (0,)), ((), ()))}


class _Cfg(NamedTuple):
    b: int
    seq: int
    ctx: int

    @property
    def nl(self):
        return self.b * self.seq

    @property
    def nc(self):
        return self.b * self.ctx

    @property
    def r(self):
        return self.nl + self.nc

    def mod_row(self, i):
        return jnp.where(i < self.nl // TM, i // (self.seq // TM), self.b)

    def first_of_row(self, i):
        nlb = self.nl // TM
        return jnp.logical_or(jnp.logical_and(i < nlb, i % (self.seq // TM) == 0), i == nlb)


def _params(n_grid=0):
    sem = ("arbitrary",) * n_grid if n_grid else None
    return pltpu.CompilerParams(dimension_semantics=sem, vmem_limit_bytes=VMEM_LIMIT_V7X)


def _dot(a, b, mode="nn"):
    return lax.dot_general(a.astype(BF16), b.astype(BF16), _DIMS[mode], preferred_element_type=F32)


def _sigmoid(x):
    return 1.0 / (1.0 + jnp.exp(-x))


def _silu(x):
    return x * _sigmoid(x)


def _dsilu(x):
    s = _sigmoid(x)
    return s * (1.0 + x * (1.0 - s))


def _colsum(v):
    return jnp.sum(v, axis=0, keepdims=True)


def _dw_rows(rows):
    return TR_MAX if rows % TR_MAX == 0 else TM


def _epi_store(acc, ex, outs):
    for o in outs:
        o[...] = acc.astype(o.dtype)


class _Ride(NamedTuple):
    ins: list
    out_shape: list
    scratch: list
    start: object
    finish: object


class _Hosted(NamedTuple):
    ride: _Ride
    n_in: int
    n_out: int
    grid: tuple

    def split(self, refs):
        n_ri, n_ro, n_rs = len(self.ride.ins), len(self.ride.out_shape), len(self.ride.scratch)
        r_in = refs[self.n_in:self.n_in + n_ri]
        r_out = refs[self.n_in + n_ri + self.n_out:self.n_in + n_ri + self.n_out + n_ro]
        own = refs[:self.n_in] + refs[self.n_in + n_ri:self.n_in + n_ri + self.n_out] + \
            refs[self.n_in + n_ri + self.n_out + n_ro:len(refs) - n_rs]
        return own, (r_in, r_out, refs[len(refs) - n_rs:])

    def start(self, parts):
        ids = [pl.program_id(d) for d in range(len(self.grid))]
        first = ids[0] == 0
        for i in ids[1:]:
            first = jnp.logical_and(first, i == 0)
        pl.when(first)(lambda: self.ride.start(*parts))

    def finish(self, parts):
        ids = [pl.program_id(d) for d in range(len(self.grid))]
        last = ids[0] == self.grid[0] - 1
        for i, g in zip(ids[1:], self.grid[1:]):
            last = jnp.logical_and(last, i == g - 1)
        pl.when(last)(lambda: self.ride.finish(*parts))


def _hosted_call(body, ride, name, grid, ins, in_specs, out_shape, out_specs, scratch, params):
    if ride is None:
        res = pl.pallas_call(body, name=name, grid=grid, in_specs=in_specs, out_specs=out_specs, out_shape=out_shape,
                             scratch_shapes=scratch, compiler_params=params)(*ins)
        return list(res), []
    host = _Hosted(ride, len(ins), len(out_shape), tuple(grid))

    def hosted(*refs):
        own, parts = host.split(refs)
        host.start(parts)
        body(*own)
        host.finish(parts)

    res = pl.pallas_call(
        hosted, name=name, grid=grid, in_specs=list(in_specs) + [_ANY] * len(ride.ins),
        out_specs=list(out_specs) + [_ANY] * len(ride.out_shape), out_shape=list(out_shape) + list(ride.out_shape),
        scratch_shapes=list(scratch) + list(ride.scratch), compiler_params=params)(*ins, *ride.ins)
    return list(res[:len(out_shape)]), list(res[len(out_shape):])


def _mm(name, mode, grid, a, b, a_spec, b_spec, out_shape, out_specs, acc_shape=None, extras=(),
        extra_specs=(), a_fn=None, epi=_epi_store, ride=None):
    nk = grid[2]
    n_ex, n_out = len(extras), len(out_shape)

    def body(*refs):
        a_ref, b_ref = refs[:2]
        ex = refs[2:2 + n_ex]
        outs = refs[2 + n_ex:2 + n_ex + n_out]
        av = a_ref[...]
        if a_fn is not None:
            av = a_fn(av)
        part = _dot(av, b_ref[...], mode)
        if nk == 1:
            epi(part, ex, outs)
        else:
            acc = refs[-1]
            k = pl.program_id(2)

            @pl.when(k == 0)
            def _():
                acc[...] = part

            @pl.when(k > 0)
            def _():
                acc[...] += part

            @pl.when(k == nk - 1)
            def _():
                epi(acc[...], ex, outs)

    scratch = [] if nk == 1 else [pltpu.VMEM(acc_shape, F32)]
    outs, ride_outs = _hosted_call(body, ride, name, grid, [a, b, *extras], [a_spec, b_spec, *extra_specs],
                                   list(out_shape), list(out_specs), scratch, _params(3))
    return outs if ride is None else (outs, ride_outs)


def _mm_dw(name, a, b, a_spec, b_spec, out_shape, out_spec, n_out_blocks, acc_shape, n_steps, epi, a_fn=None,
           extras=(), extra_specs=(), extra_out=(), extra_out_specs=(), ride=None):
    grid = (1, n_out_blocks, n_steps)
    outs = [jax.ShapeDtypeStruct(out_shape, BF16)] + list(extra_out)
    return _mm(name, "tn", grid, a, b, a_spec, b_spec, outs, [out_spec, *extra_out_specs], acc_shape, extras,
               extra_specs, a_fn, epi, ride)


def _proj_bwd(name, a, a_spec, pick, w, layer, nblk, cfg):
    ns = w.shape[-1]

    def body(a_ref, w_ref, o_ref):
        acc = _dot(pick(a_ref, 0), w_ref[0], "nt")
        for j in range(1, N_SHARD):
            acc = acc + _dot(pick(a_ref, j), w_ref[j], "nt")
        o_ref[...] = acc

    return pl.pallas_call(
        body, name=name, grid=(nblk,),
        in_specs=[a_spec, pl.BlockSpec((None, N_SHARD, D, ns), lambda i: (layer, 0, 0, 0))],
        out_specs=pl.BlockSpec((TM, D), lambda i: (i, 0)), out_shape=jax.ShapeDtypeStruct((cfg.r, D), F32),
        compiler_params=_params(1))(a, w)


def _mod_spec(cfg):
    return pl.BlockSpec((None, 6, D), lambda i: (cfg.mod_row(i), 0, 0))


def _norm_fwd(name, x, gvec, mod3, ish, isc, nblk, cfg):
    def body(x_ref, g_ref, m_ref, o_ref):
        xv = x_ref[...]
        r = lax.rsqrt(jnp.mean(xv * xv, axis=-1, keepdims=True) + EPS)
        o_ref[...] = (xv * r * g_ref[...] * (1.0 + m_ref[isc:isc + 1, :]) + m_ref[ish:ish + 1, :]).astype(BF16)

    row = pl.BlockSpec((TM, D), lambda i: (i, 0))
    return pl.pallas_call(
        body, name=name, grid=(nblk,),
        in_specs=[row, pl.BlockSpec((1, D), lambda i: (0, 0)), _mod_spec(cfg)], out_specs=row,
        out_shape=jax.ShapeDtypeStruct((cfg.r, D), BF16), compiler_params=_params(1))(x, gvec, mod3)


def _accumulate_rows(first, ref, val):
    @pl.when(first)
    def _():
        ref[...] = val

    @pl.when(jnp.logical_not(first))
    def _():
        ref[...] += val


def _norm_bwd(name, x, dh, dres, gvec, mod3, isc, nblk, res_latent_only, cfg):
    nlb = cfg.nl // TM

    def body(x_ref, dh_ref, dres_ref, g_ref, m_ref, dx_ref, dsh_ref, dsc_ref, dg_ref):
        i = pl.program_id(0)
        xv = x_ref[...]
        r = lax.rsqrt(jnp.mean(xv * xv, axis=-1, keepdims=True) + EPS)
        xh = xv * r
        g = g_ref[...]
        sc1 = 1.0 + m_ref[isc:isc + 1, :]
        dhv = dh_ref[...]
        t = dhv * xh
        first = cfg.first_of_row(i)
        _accumulate_rows(first, dsh_ref, _colsum(dhv))
        _accumulate_rows(first, dsc_ref, _colsum(t * g))
        _accumulate_rows(i == 0, dg_ref, _colsum(t * sc1))
        dxh = dhv * (g * sc1)
        dxn = r * (dxh - xh * jnp.mean(dxh * xh, axis=-1, keepdims=True))
        if res_latent_only:
            dx_ref[...] = jnp.where(i < nlb, dres_ref[...], 0.0) + dxn
        else:
            dx_ref[...] = dres_ref[...] + dxn

    row = pl.BlockSpec((TM, D), lambda i: (i, 0))
    vec = pl.BlockSpec((1, D), lambda i: (0, 0))
    part = pl.BlockSpec((None, 1, D), lambda i: (cfg.mod_row(i), 0, 0))
    part_shape = jax.ShapeDtypeStruct((MOD_ROWS, 1, D), F32)
    return pl.pallas_call(
        body, name=name, grid=(nblk,), in_specs=[row, row, row, vec, _mod_spec(cfg)],
        out_specs=[row, part, part, vec],
        out_shape=[jax.ShapeDtypeStruct((cfg.r, D), F32), part_shape, part_shape, jax.ShapeDtypeStruct((1, D), F32)],
        compiler_params=_params(1))(x, dh, dres, gvec, mod3)


def _gate_bwd(name, dx, y, mod3, ig, nblk, cfg):
    def body(dx_ref, y_ref, m_ref, dy_ref, dg_ref):
        i = pl.program_id(0)
        dxv = dx_ref[...]
        dy_ref[...] = (dxv * m_ref[ig:ig + 1, :]).astype(BF16)
        _accumulate_rows(cfg.first_of_row(i), dg_ref, _colsum(dxv * y_ref[...].astype(F32)))

    row = pl.BlockSpec((TM, D), lambda i: (i, 0))
    part = pl.BlockSpec((None, 1, D), lambda i: (cfg.mod_row(i), 0, 0))
    return pl.pallas_call(
        body, name=name, grid=(nblk,), in_specs=[row, row, _mod_spec(cfg)], out_specs=[row, part],
        out_shape=[jax.ShapeDtypeStruct((cfg.r, D), BF16), jax.ShapeDtypeStruct((MOD_ROWS, 1, D), F32)],
        compiler_params=_params(1))(dx, y, mod3)


def _loss_head(x, target, gvec, cfg):
    def body(x_ref, t_ref, g_ref, dx_ref, loss_ref, dg_ref):
        i = pl.program_id(0)

        @pl.when(i == 0)
        def _():
            loss_ref[...] = jnp.zeros_like(loss_ref)
            dg_ref[...] = jnp.zeros_like(dg_ref)

        xv = x_ref[...]
        g = g_ref[...]
        r = lax.rsqrt(jnp.mean(xv * xv, axis=-1, keepdims=True) + EPS)
        xh = xv * r
        err = xh * g - t_ref[...]
        loss_ref[...] += (0.5 / D) * _colsum(jnp.sum(err * err, axis=-1, keepdims=True))
        dy = err * (1.0 / D)
        dg_ref[...] += _colsum(dy * xh)
        dxh = dy * g
        dx_ref[...] = r * (dxh - xh * jnp.mean(dxh * xh, axis=-1, keepdims=True))

    row = pl.BlockSpec((TM, D), lambda i: (i, 0))
    vec = pl.BlockSpec((1, D), lambda i: (0, 0))
    return pl.pallas_call(
        body, name="loss_head", grid=(cfg.nl // TM,), in_specs=[row, row, vec],
        out_specs=[row, pl.BlockSpec((1, 1), lambda i: (0, 0)), vec],
        out_shape=[jax.ShapeDtypeStruct((cfg.r, D), F32), jax.ShapeDtypeStruct((1, 1), F32),
                   jax.ShapeDtypeStruct((1, D), F32)],
        compiler_params=_params(1))(x, target, gvec)


def _rope_tables(seq):
    rows = seq // GRID_W
    row = jnp.repeat(jnp.arange(rows), GRID_W).astype(F32)
    col = jnp.tile(jnp.arange(GRID_W), rows).astype(F32)
    half = HEAD_DIM // 2
    inv = ROPE_BASE ** (-jnp.arange(0, half, 2, dtype=F32) / half)
    ar, ac = row[:, None] * inv, col[:, None] * inv
    ang = jnp.concatenate([ar, ar, ac, ac], axis=-1)
    sign = jnp.tile(jnp.concatenate([-jnp.ones((16,), F32), jnp.ones((16,), F32)]), 2)
    cos = jnp.tile(jnp.cos(ang), (1, 2))
    sin = jnp.tile(jnp.sin(ang) * sign, (1, 2))
    cos = jnp.concatenate([cos, jnp.ones((TM, 2 * HEAD_DIM), F32)], axis=0)
    sin = jnp.concatenate([sin, jnp.zeros((TM, 2 * HEAD_DIM), F32)], axis=0)
    return cos, sin


def _rope(x, cos, sin_signed, sign):
    lane = lax.broadcasted_iota(jnp.int32, x.shape, 1)
    low = (lane % 32) < 16
    rot = jnp.where(low, pltpu.roll(x, 112, 1), pltpu.roll(x, 16, 1))
    return x * cos + sign * (rot * sin_signed)


def _in_proj(name, h, w_in, layer, cos_t, sin_t, cfg):
    nlb, bps = cfg.nl // TM, cfg.seq // TM

    def body(h_ref, w_ref, cos_ref, sin_ref, qkv_ref, cp_ref):
        hv = h_ref[...]
        u = jnp.concatenate([_dot(hv, w_ref[j]) for j in range(N_SHARD)], axis=1)
        cos, sin = cos_ref[...], sin_ref[...]
        tiles = []
        for t in range(5):
            y = _rope(u[:, 128 * t:128 * (t + 1)], cos, sin, 1.0)
            tiles.append(y * (HEAD_DIM ** -0.5) if t < 4 else y)
        tiles.append(u[:, 640:768])
        qkv_ref[...] = jnp.concatenate(tiles, axis=1).astype(BF16)
        cp_ref[...] = u[:, 768:IN_W].astype(BF16)

    tab = pl.BlockSpec((TM, 128), lambda i: (jnp.where(i < nlb, i % bps, bps), 0))
    half = pl.BlockSpec((TM, 768), lambda i: (i, 0))
    return pl.pallas_call(
        body, name=name, grid=(cfg.r // TM,),
        in_specs=[pl.BlockSpec((TM, D), lambda i: (i, 0)),
                  pl.BlockSpec((None, N_SHARD, D, IN_SHARD), lambda i: (layer, 0, 0, 0)), tab, tab],
        out_specs=[half, half],
        out_shape=[jax.ShapeDtypeStruct((cfg.r, 768), BF16), jax.ShapeDtypeStruct((cfg.r, 768), BF16)],
        compiler_params=_params(1))(h, w_in, cos_t, sin_t)


def _att_specs(cfg):
    nlb, ncb = cfg.seq // QB, cfg.ctx // QB

    def qblk(s, qb):
        return jnp.where(qb < nlb, s * nlb + qb, cfg.nl // QB + s * ncb + qb - nlb)

    def near(off, col):
        return pl.BlockSpec((QB, 128), lambda s, qb: (s * nlb + jnp.clip(qb + off, 0, nlb - 1), col))

    def ctxs(col):
        return pl.BlockSpec((cfg.ctx, 128), lambda s, qb: (cfg.nl // cfg.ctx + s, col))

    qspec = pl.BlockSpec((QB, ATTN_W), lambda s, qb: (qblk(s, qb), 0))
    kv = [ctxs(4), ctxs(5), near(-1, 4), near(0, 4), near(1, 4), near(-1, 5), near(0, 5), near(1, 5)]
    return qblk, qspec, kv


def _att_scores(qb, nlb, sink_ref, q_ref, k_refs, kh):
    is_lat = qb < nlb
    ii = lax.broadcasted_iota(jnp.int32, (4 * QB, QB), 0) % QB
    jj = lax.broadcasted_iota(jnp.int32, (4 * QB, QB), 1)
    off_p = jnp.where(jnp.logical_and(is_lat, qb >= 1), 0.0, NEG)
    off_c = jnp.where(is_lat, 0.0, NEG)
    off_n = jnp.where(jnp.logical_and(is_lat, qb <= nlb - 2), 0.0, NEG)
    q4 = jnp.concatenate([q_ref[:, (4 * kh + g) * HEAD_DIM:(4 * kh + g + 1) * HEAD_DIM] for g in range(4)], axis=0)
    rg = lax.broadcasted_iota(jnp.int32, (4 * QB, 1), 0) // QB
    snk = jnp.where(rg == 0, sink_ref[4 * kh],
                    jnp.where(rg == 1, sink_ref[4 * kh + 1], jnp.where(rg == 2, sink_ref[4 * kh + 2], sink_ref[4 * kh + 3])))
    lanes = slice(kh * HEAD_DIM, (kh + 1) * HEAD_DIM)
    kx, kp, kc, kn = [r[:, lanes] for r in k_refs]
    sx = _dot(q4, kx, "nt")
    sp = jnp.where(jj >= ii, _dot(q4, kp, "nt"), NEG) + off_p
    sc = _dot(q4, kc, "nt") + off_c
    sn = jnp.where(jj <= ii, _dot(q4, kn, "nt"), NEG) + off_n
    return q4, snk, (kx, kp, kc, kn), (sx, sp, sc, sn)


def _att_fwd(name, qkv, sink, ctx_queries, cfg, ride=None):
    nlb, ncb = cfg.seq // QB, cfg.ctx // QB
    qblk, qspec, kvspecs = _att_specs(cfg)

    def body(sink_ref, q_ref, kx_ref, vx_ref, kp_ref, kc_ref, kn_ref, vp_ref, vc_ref, vn_ref, o_ref, lse_ref):
        qb = pl.program_id(1)
        for kh in range(2):
            q4, snk, _, ss = _att_scores(qb, nlb, sink_ref, q_ref, (kx_ref, kp_ref, kc_ref, kn_ref), kh)
            lanes = slice(kh * HEAD_DIM, (kh + 1) * HEAD_DIM)
            vs = [r[:, lanes] for r in (vx_ref, vp_ref, vc_ref, vn_ref)]
            m = snk
            for s_ in ss:
                m = jnp.maximum(m, jnp.max(s_, axis=-1, keepdims=True))
            den = jnp.exp(snk - m)
            o4 = jnp.zeros((4 * QB, HEAD_DIM), F32)
            for s_, v_ in zip(ss, vs):
                p = jnp.exp(s_ - m)
                den = den + jnp.sum(p, axis=-1, keepdims=True)
                o4 = o4 + _dot(p, v_)
            o4 = o4 / den
            lse = m + jnp.log(den)
            for g in range(4):
                h = 4 * kh + g
                o_ref[:, h * HEAD_DIM:(h + 1) * HEAD_DIM] = o4[g * QB:(g + 1) * QB].astype(BF16)
                lse_ref[:, h:h + 1] = lse[g * QB:(g + 1) * QB]

    return _hosted_call(
        body, ride, name, (cfg.b, nlb + (ncb if ctx_queries else 0)), [sink] + [qkv] * 9,
        [pl.BlockSpec(memory_space=pltpu.SMEM), qspec, *kvspecs],
        [jax.ShapeDtypeStruct((cfg.r, D), BF16), jax.ShapeDtypeStruct((cfg.r, N_HEADS), F32)],
        [pl.BlockSpec((QB, ATTN_W), lambda s, qb: (qblk(s, qb), 0)),
         pl.BlockSpec((QB, N_HEADS), lambda s, qb: (qblk(s, qb), 0))], [], _params(2))


def _att_bwd(name, qkv, mix, dmix, lse, sink, cos_t, sin_t, ctx_queries, cfg):
    nlb, ncb = cfg.seq // QB, cfg.ctx // QB
    nqb = nlb + (ncb if ctx_queries else 0)
    qblk, qspec, kvspecs = _att_specs(cfg)

    def body(sink_ref, q_ref, kx_ref, vx_ref, kp_ref, kc_ref, kn_ref, vp_ref, vc_ref, vn_ref, o_ref, do_ref,
             lse_ref, cosq_ref, sinq_ref, cosk_ref, sink_tab_ref, dq_ref, dkvl_ref, dkvc_ref, dsink_ref,
             accl, accc, dqs):
        s_id, qb = pl.program_id(0), pl.program_id(1)

        @pl.when(qb == 0)
        def _():
            accl[...] = jnp.zeros_like(accl)
            accc[...] = jnp.zeros_like(accc)

        @pl.when(jnp.logical_and(s_id == 0, qb == 0))
        def _():
            dsink_ref[...] = jnp.zeros_like(dsink_ref)

        starts = [pl.multiple_of(jnp.clip(qb + off, 0, nlb - 1) * QB, QB) for off in (-1, 0, 1)]
        for kh in range(2):
            q4, snk, ks, ss = _att_scores(qb, nlb, sink_ref, q_ref, (kx_ref, kp_ref, kc_ref, kn_ref), kh)
            lanes = slice(kh * HEAD_DIM, (kh + 1) * HEAD_DIM)
            vs = [r[:, lanes] for r in (vx_ref, vp_ref, vc_ref, vn_ref)]
            heads = [slice((4 * kh + g) * HEAD_DIM, (4 * kh + g + 1) * HEAD_DIM) for g in range(4)]
            do4 = jnp.concatenate([do_ref[:, hs] for hs in heads], axis=0)
            o4 = jnp.concatenate([o_ref[:, hs] for hs in heads], axis=0).astype(F32)
            lse4 = jnp.concatenate([lse_ref[:, 4 * kh + g:4 * kh + g + 1] for g in range(4)], axis=0)
            delta = jnp.sum(do4 * o4, axis=-1, keepdims=True)
            dq4 = jnp.zeros((4 * QB, HEAD_DIM), F32)
            dks, dvs = [], []
            for s_, k_, v_ in zip(ss, ks, vs):
                p = jnp.exp(s_ - lse4)
                ds = p * (_dot(do4, v_, "nt") - delta)
                dq4 = dq4 + _dot(ds, k_)
                dks.append(_dot(ds, q4, "tn"))
                dvs.append(_dot(p, do4, "tn"))
            accc[:, lanes] += dks[0]
            accc[:, 128 + kh * HEAD_DIM:128 + (kh + 1) * HEAD_DIM] += dvs[0]
            for st, dk_, dv_ in zip(starts, dks[1:], dvs[1:]):
                accl[pl.ds(st, QB), lanes] += dk_
                accl[pl.ds(st, QB), 128 + kh * HEAD_DIM:128 + (kh + 1) * HEAD_DIM] += dv_
            dsk = -jnp.exp(snk - lse4) * delta
            for g in range(4):
                h = 4 * kh + g
                dsink_ref[h:h + 1, :] += jnp.broadcast_to(_colsum(dsk[g * QB:(g + 1) * QB]), (1, 128))
                dqs[:, heads[g]] = dq4[g * QB:(g + 1) * QB]
        cos, sin = cosq_ref[...], sinq_ref[...]
        dq_ref[...] = jnp.concatenate(
            [_rope(dqs[:, 128 * t:128 * (t + 1)], cos, sin, -1.0) * (HEAD_DIM ** -0.5) for t in range(4)],
            axis=1).astype(BF16)

        @pl.when(qb == nqb - 1)
        def _():
            dk = _rope(accl[:, 0:128], cosk_ref[...], sink_tab_ref[...], -1.0)
            dkvl_ref[...] = jnp.concatenate([dk, accl[:, 128:256]], axis=1).astype(BF16)
            dkvc_ref[...] = accc[...].astype(BF16)

    rowq = lambda w: pl.BlockSpec((QB, w), lambda s, qb: (qblk(s, qb), 0))
    tabq = pl.BlockSpec((QB, 128), lambda s, qb: (jnp.where(qb < nlb, qb, cfg.seq // QB), 0))
    tabk = pl.BlockSpec((cfg.seq, 128), lambda s, qb: (0, 0))
    return pl.pallas_call(
        body, name=name, grid=(cfg.b, nqb),
        in_specs=[pl.BlockSpec(memory_space=pltpu.SMEM), qspec, *kvspecs, rowq(ATTN_W), rowq(ATTN_W), rowq(N_HEADS),
                  tabq, tabq, tabk, tabk],
        out_specs=[rowq(ATTN_W), pl.BlockSpec((cfg.seq, 256), lambda s, qb: (s, 0)),
                   pl.BlockSpec((cfg.ctx, 256), lambda s, qb: (s, 0)), pl.BlockSpec((N_HEADS, 128), lambda s, qb: (0, 0))],
        out_shape=[jax.ShapeDtypeStruct((cfg.r, IN_W), BF16), jax.ShapeDtypeStruct((cfg.nl, 256), BF16),
                   jax.ShapeDtypeStruct((cfg.nc, 256), BF16), jax.ShapeDtypeStruct((N_HEADS, 128), F32)],
        scratch_shapes=[pltpu.VMEM((cfg.seq, 256), F32), pltpu.VMEM((cfg.ctx, 256), F32), pltpu.VMEM((QB, ATTN_W), F32)],
        compiler_params=_params(2))(sink, *([qkv] * 9), mix, dmix, lse, cos_t, sin_t, cos_t, sin_t)


def _pool_geometry(n, c):
    lane = lax.broadcasted_iota(jnp.int32, (1, POOL_W), 1) // HEAD_DIM
    wl = jnp.where(lane == 0, 1, jnp.where(lane == 1, 2, jnp.where(lane == 2, 4, 8)))
    wr = wl - 1
    t = c * CHUNK + lax.broadcasted_iota(jnp.int32, (CHUNK, POOL_W), 0)
    cnt = (jnp.minimum(t + wr, n - 1) - jnp.maximum(t - wl, 0) + 1).astype(F32)
    return wl, wr, cnt


def _build_phases(src, ph, c):
    for s in range(1, 8):
        ph[s - 1] = src[c * CHUNK + s:c * CHUNK + s + CHUNK + 24, :]


def _window(src, ph, c, off):
    a, s = divmod(off, 8)
    if s == 0:
        return src[c * CHUNK + 8 * a:c * CHUNK + 8 * a + CHUNK, :]
    return ph[s - 1, 8 * a:8 * a + CHUNK, :]


def _conv_chunk(hp, ph, dw_ref, dwb_ref, c):
    _build_phases(hp, ph, c)
    acc = jnp.zeros((CHUNK, CONV_W), F32) + dwb_ref[...]
    for j in range(CONV_K):
        acc = acc + dw_ref[j:j + 1, :] * _window(hp, ph, c, j + 1)
    return acc


def _fill_glu(cp_ref, hp, n):
    hp[0:PAD, :] = jnp.zeros((PAD, CONV_W), F32)
    hp[PAD + n:2 * PAD + n, :] = jnp.zeros((PAD, CONV_W), F32)
    for c in range(n // CHUNK):
        rows = slice(c * CHUNK, (c + 1) * CHUNK)
        a = cp_ref[rows, 0:CONV_W].astype(F32)
        g = cp_ref[rows, CONV_W:2 * CONV_W].astype(F32)
        hp[PAD + c * CHUNK:PAD + (c + 1) * CHUNK, :] = a * _sigmoid(g)


def _fill_pool(cp_ref, pp, n):
    pp[0:PAD, :] = jnp.zeros((PAD, POOL_W), F32)
    pp[PAD + n:2 * PAD + n, :] = jnp.zeros((PAD, POOL_W), F32)
    for c in range(n // CHUNK):
        pp[PAD + c * CHUNK:PAD + (c + 1) * CHUNK, :] = cp_ref[c * CHUNK:(c + 1) * CHUNK, 2 * CONV_W:768].astype(F32)


def _pool_chunk(pp, ph, n, c):
    wl, wr, cnt = _pool_geometry(n, c)
    _build_phases(pp, ph, c)
    acc = jnp.zeros((CHUNK, POOL_W), F32)
    for o in range(-8, 8):
        acc = acc + jnp.where(jnp.logical_and(o >= -wl, o <= wr), _window(pp, ph, c, PAD + o), 0.0)
    return acc / cnt - pp[PAD + c * CHUNK:PAD + (c + 1) * CHUNK, :], cnt


def _seq_specs(n, blk_off, width, col=0):
    return pl.BlockSpec((n, width), lambda s: (blk_off + s, col))


def _full(shape):
    return pl.BlockSpec(shape, lambda s: (0,) * len(shape))


_PHASES = pltpu.VMEM((7, CHUNK + 24, CONV_W), F32)


def _convpool_fwd(name, cpin, mix, yconv, prm, n, blk_off, cfg):
    dw, dwb, lng, lnb, wbd, ps = prm
    n_alias = 1 if yconv is None else 2

    def body(*refs):
        cp_ref, dw_ref, dwb_ref, lng_ref, lnb_ref, wbd_ref, ps_ref = refs[:7]
        out_ref, y_ref, hp, pp, ph = refs[7 + n_alias:]
        _fill_glu(cp_ref, hp, n)
        _fill_pool(cp_ref, pp, n)
        for c in range(n // CHUNK):
            rows = slice(c * CHUNK, (c + 1) * CHUNK)
            y = _conv_chunk(hp, ph, dw_ref, dwb_ref, c)
            y_ref[rows, :] = y
            d = y - jnp.mean(y, axis=-1, keepdims=True)
            hn = d * lax.rsqrt(jnp.mean(d * d, axis=-1, keepdims=True) + EPS) * lng_ref[...] + lnb_ref[...]
            out_ref[rows, 0:CONV_W] = (hn * _sigmoid(hn)).astype(BF16)
            yp, _ = _pool_chunk(pp, ph, n, c)
            out_ref[rows, CONV_W:2 * CONV_W] = (_dot(yp, wbd_ref[...]) * ps_ref[...]).astype(BF16)

    through = [mix] if yconv is None else [mix, yconv]
    return pl.pallas_call(
        body, name=name, grid=(cfg.b,),
        in_specs=[_seq_specs(n, blk_off, 768), _full((32, CONV_W)), _full((1, CONV_W)), _full((1, CONV_W)),
                  _full((1, CONV_W)), _full((POOL_W, POOL_W)), _full((1, POOL_W))] + [_ANY] * n_alias,
        out_specs=[_seq_specs(n, blk_off, 512, 1), _seq_specs(n, blk_off, CONV_W)],
        out_shape=[jax.ShapeDtypeStruct((cfg.r, D), BF16), jax.ShapeDtypeStruct((cfg.r, CONV_W), F32)],
        scratch_shapes=[pltpu.VMEM((n + 2 * PAD, CONV_W), F32), pltpu.VMEM((n + 2 * PAD, POOL_W), F32), _PHASES],
        input_output_aliases={7 + i: i for i in range(n_alias)},
        compiler_params=_params(1))(cpin, dw, dwb, lng, lnb, wbd, ps, *through)


_SMALL_SHAPES = [(32, CONV_W), (1, CONV_W), (1, CONV_W), (1, CONV_W), (POOL_W, POOL_W), (1, POOL_W)]


def _convpool_bwd(name, cpin, yconv, dmix, prm, acc_in, n, blk_off, cfg):
    dw, dwb, lng, lnb, wbd, ps = prm
    nch = n // CHUNK

    def body(cp_ref, y_ref, dm_ref, dw_ref, dwb_ref, lng_ref, lnb_ref, wbd_ref, ps_ref, dcp_in,
             a_dw, a_dwb, a_lng, a_lnb, a_wbd, a_ps,
             dcp_ref, o_dw, o_dwb, o_lng, o_lnb, o_wbd, o_ps, hp, dyp, pp, wp, dyv, dwacc, ph):
        s = pl.program_id(0)

        @pl.when(s == 0)
        def _():
            for o_, a_ in ((o_dw, a_dw), (o_dwb, a_dwb), (o_lng, a_lng), (o_lnb, a_lnb), (o_wbd, a_wbd), (o_ps, a_ps)):
                o_[...] = a_[...]
            dwacc[...] = jnp.zeros_like(dwacc)

        _fill_glu(cp_ref, hp, n)
        _fill_pool(cp_ref, pp, n)
        for ref in (dyp, wp):
            ref[0:PAD, :] = jnp.zeros((PAD, CONV_W), F32)
            ref[PAD + n:2 * PAD + n, :] = jnp.zeros((PAD, CONV_W), F32)
        for c in range(nch):
            rows = slice(c * CHUNK, (c + 1) * CHUNK)
            y = y_ref[rows, :]
            d = y - jnp.mean(y, axis=-1, keepdims=True)
            rstd = lax.rsqrt(jnp.mean(d * d, axis=-1, keepdims=True) + EPS)
            xh = d * rstd
            hn = xh * lng_ref[...] + lnb_ref[...]
            sg = _sigmoid(hn)
            dhn = dm_ref[rows, 0:CONV_W] * (sg * (1.0 + hn * (1.0 - sg)))
            o_lnb[...] += _colsum(dhn)
            o_lng[...] += _colsum(dhn * xh)
            dxh = dhn * lng_ref[...]
            dy = rstd * (dxh - jnp.mean(dxh, axis=-1, keepdims=True) - xh * jnp.mean(dxh * xh, axis=-1, keepdims=True))
            o_dwb[...] += _colsum(dy)
            dyp[PAD + c * CHUNK:PAD + (c + 1) * CHUNK, :] = dy
            _build_phases(hp, ph, c)
            for j in range(CONV_K):
                prod = dy * _window(hp, ph, c, j + 1)
                dwacc[8 * j:8 * j + 8, :] += jnp.sum(prod.reshape(CHUNK // 8, 8, CONV_W), axis=0)
            yp, cnt = _pool_chunk(pp, ph, n, c)
            dz = dm_ref[rows, CONV_W:2 * CONV_W]
            o_ps[...] += _colsum(dz * _dot(yp, wbd_ref[...]))
            dzs = dz * ps_ref[...]
            o_wbd[...] += _dot(yp, dzs, "tn")
            dv = _dot(dzs, wbd_ref[...], "nt")
            dyv[rows, :] = dv
            wp[PAD + c * CHUNK:PAD + (c + 1) * CHUNK, :] = dv / cnt
        for c in range(nch):
            rows = slice(c * CHUNK, (c + 1) * CHUNK)
            _build_phases(dyp, ph, c)
            dh = jnp.zeros((CHUNK, CONV_W), F32)
            for j in range(CONV_K):
                dh = dh + dw_ref[j:j + 1, :] * _window(dyp, ph, c, 31 - j)
            a = cp_ref[rows, 0:CONV_W].astype(F32)
            sg = _sigmoid(cp_ref[rows, CONV_W:2 * CONV_W].astype(F32))
            dcp_ref[rows, 0:CONV_W] = (dh * sg).astype(BF16)
            dcp_ref[rows, CONV_W:2 * CONV_W] = (dh * a * sg * (1.0 - sg)).astype(BF16)
            wl, wr, _ = _pool_geometry(n, c)
            _build_phases(wp, ph, c)
            dp = -dyv[rows, :]
            for o in range(-8, 8):
                dp = dp + jnp.where(jnp.logical_and(o >= -wl, o <= wr), _window(wp, ph, c, PAD - o), 0.0)
            dcp_ref[rows, 2 * CONV_W:768] = dp.astype(BF16)

        @pl.when(s == cfg.b - 1)
        def _():
            for j in range(CONV_K):
                o_dw[j:j + 1, :] += _colsum(dwacc[8 * j:8 * j + 8, :])

    small_specs = [_full(sh) for sh in _SMALL_SHAPES]
    return pl.pallas_call(
        body, name=name, grid=(cfg.b,),
        in_specs=[_seq_specs(n, blk_off, 768), _seq_specs(n, blk_off, CONV_W), _seq_specs(n, blk_off, 512, 1),
                  *small_specs, _ANY, *small_specs],
        out_specs=[_seq_specs(n, blk_off, 768, 1), *small_specs],
        out_shape=[jax.ShapeDtypeStruct((cfg.r, IN_W), BF16)] + [jax.ShapeDtypeStruct(sh, F32) for sh in _SMALL_SHAPES],
        scratch_shapes=[pltpu.VMEM((n + 2 * PAD, CONV_W), F32), pltpu.VMEM((n + 2 * PAD, CONV_W), F32),
                        pltpu.VMEM((n + 2 * PAD, POOL_W), F32), pltpu.VMEM((n + 2 * PAD, POOL_W), F32),
                        pltpu.VMEM((n, POOL_W), F32), pltpu.VMEM((8 * 32, CONV_W), F32), _PHASES],
        input_output_aliases={9: 0}, compiler_params=_params(1))(cpin, yconv, dmix, dw, dwb, lng, lnb, wbd, ps, *acc_in)


def _place_kv(name, du, dkvl, dkvc, with_ctx, cfg):
    nlb = cfg.nl // TM

    def body(l_ref, c_ref, du_in, o_ref):
        i = pl.program_id(0)
        o_ref[...] = jnp.where(i < nlb, l_ref[...], c_ref[...])

    return pl.pallas_call(
        body, name=name, grid=(cfg.r // TM if with_ctx else nlb,),
        in_specs=[pl.BlockSpec((TM, 256), lambda i: (jnp.minimum(i, nlb - 1), 0)),
                  pl.BlockSpec((TM, 256), lambda i: (jnp.maximum(i - nlb, 0), 0)), _ANY],
        out_specs=pl.BlockSpec((TM, 256), lambda i: (i, 2)), out_shape=jax.ShapeDtypeStruct((cfg.r, IN_W), BF16),
        input_output_aliases={2: 0}, compiler_params=_params(1))(dkvl, dkvc, du)


def _place_ctx_kv_only(name, du, dkvc, cfg):
    nlb = cfg.nl // TM

    def body(c_ref, du_in, o_ref):
        o_ref[...] = jnp.zeros_like(o_ref)
        o_ref[:, ATTN_W:ATTN_W + 256] = c_ref[...]

    return pl.pallas_call(
        body, name=name, grid=(cfg.nc // TM,), in_specs=[pl.BlockSpec((TM, 256), lambda i: (i, 0)), _ANY],
        out_specs=pl.BlockSpec((TM, IN_W), lambda i: (nlb + i, 0)), out_shape=jax.ShapeDtypeStruct((cfg.r, IN_W), BF16),
        input_output_aliases={1: 0}, compiler_params=_params(1))(dkvc, du)


def _ffn_in(name, h, w_ffn_in, layer, nblk, cfg):
    def body(h_ref, wg_ref, wu_ref, gu_ref, act_ref):
        hv = h_ref[...]
        g = _dot(hv, wg_ref[...])
        u = _dot(hv, wu_ref[...])
        gu_ref[0] = g.astype(BF16)
        gu_ref[1] = u.astype(BF16)
        act_ref[...] = (g * _sigmoid(g) * u).astype(BF16)

    wspec = lambda base: pl.BlockSpec((None, None, D, HALF_FF), lambda j, i: (layer, base + j, 0, 0))
    return pl.pallas_call(
        body, name=name, grid=(2, nblk),
        in_specs=[pl.BlockSpec((TM, D), lambda j, i: (i, 0)), wspec(0), wspec(2)],
        out_specs=[pl.BlockSpec((2, TM, HALF_FF), lambda j, i: (0, i, j)), pl.BlockSpec((TM, HALF_FF), lambda j, i: (i, j))],
        out_shape=[jax.ShapeDtypeStruct((2, cfg.r, D_FF), BF16), jax.ShapeDtypeStruct((cfg.r, D_FF), BF16)],
        compiler_params=_params(2))(h, w_ffn_in, w_ffn_in)


def _row_block(rows, cols, max_bytes=1 << 20):
    best = 16
    for t in range(16, rows + 1, 16):
        if rows % t == 0 and t * cols * 4 <= max_bytes:
            best = t
    assert rows % best == 0
    return best


def _pair_add(name, own32, recv, c_idx):
    _, _, s0, s1 = own32.shape
    tr = _row_block(s0, s1)

    def body(c_ref, a_ref, b_ref, o_ref):
        o_ref[...] = (a_ref[...].astype(F32) + b_ref[...].astype(F32)).astype(BF16)

    grid_spec = pltpu.PrefetchScalarGridSpec(
        num_scalar_prefetch=1, grid=(N_SHARD * s0 // tr,),
        in_specs=[pl.BlockSpec((None, tr, s1), lambda i, c: (c[0], i, 0)), pl.BlockSpec((tr, s1), lambda i, c: (i, 0))],
        out_specs=pl.BlockSpec((tr, s1), lambda i, c: (i, 0)))
    out = pl.pallas_call(body, name=name, grid_spec=grid_spec, out_shape=jax.ShapeDtypeStruct((N_SHARD * s0, s1), BF16),
                         compiler_params=_params(1))(c_idx, own32.reshape(2, N_SHARD * s0, s1), recv.reshape(N_SHARD * s0, s1))
    return out.reshape(N_SHARD, s0, s1)


def _shard_sum(name, pair_sum, recv, jc_idx):
    _, s0, s1 = pair_sum.shape
    tr = _row_block(s0, s1)

    def body(jc_ref, a_ref, b_ref, o_ref):
        o_ref[...] = ((a_ref[...].astype(F32) + b_ref[0].astype(F32)) + b_ref[1].astype(F32)) + b_ref[2].astype(F32)

    grid_spec = pltpu.PrefetchScalarGridSpec(
        num_scalar_prefetch=1, grid=(s0 // tr,),
        in_specs=[pl.BlockSpec((None, tr, s1), lambda i, jc: (jc[0], i, 0)), pl.BlockSpec((3, tr, s1), lambda i, jc: (0, i, 0))],
        out_specs=pl.BlockSpec((None, tr, s1), lambda i, jc: (jc[1], i, 0)))
    return pl.pallas_call(body, name=name, grid_spec=grid_spec, out_shape=jax.ShapeDtypeStruct((2, s0, s1), F32),
                          compiler_params=_params(1))(jc_idx, pair_sum, recv)


def _adamw_math(w, g, m, v):
    m = ADAM_B1 * m + (1.0 - ADAM_B1) * g
    v = ADAM_B2 * v + (1.0 - ADAM_B2) * (g * g)
    m_hat = m / (1.0 - ADAM_B1 ** ADAM_STEP)
    v_hat = v / (1.0 - ADAM_B2 ** ADAM_STEP)
    delta = -ADAM_LR * (m_hat / (jnp.sqrt(v_hat) + ADAM_EPS) + ADAM_WD * w)
    return delta, m, v


def _adamw(name, w, g, m, v):
    rows, cols = w.shape
    tr = rows if rows % 16 else _row_block(rows, cols, 1 << 19)

    def body(w_ref, g_ref, m_ref, v_ref, d_ref, mo_ref, vo_ref):
        d, mn, vn = _adamw_math(w_ref[...], g_ref[...], m_ref[...], v_ref[...])
        d_ref[...] = d
        mo_ref[...] = mn
        vo_ref[...] = vn

    spec = pl.BlockSpec((tr, cols), lambda i: (i, 0))
    shape = jax.ShapeDtypeStruct((rows, cols), F32)
    return pl.pallas_call(body, name=name, grid=(rows // tr,), in_specs=[spec] * 4, out_specs=[spec] * 3,
                          out_shape=[shape] * 3, compiler_params=_params(1))(w, g, m, v)


def _adamw_layers(name, w, g_layers, m, v):
    rows, cols = w.shape
    s0 = rows // 2
    tr = _row_block(s0, cols, 1 << 19)
    nb = s0 // tr

    def body(w_ref, g0_ref, g1_ref, m_ref, v_ref, g_ref, d_ref, mo_ref, vo_ref):
        g = jnp.where(pl.program_id(0) < nb, g0_ref[...], g1_ref[...])
        d, mn, vn = _adamw_math(w_ref[...], g, m_ref[...], v_ref[...])
        g_ref[...] = g
        d_ref[...] = d
        mo_ref[...] = mn
        vo_ref[...] = vn

    spec = pl.BlockSpec((tr, cols), lambda i: (i, 0))
    shape = jax.ShapeDtypeStruct((rows, cols), F32)
    return pl.pallas_call(
        body, name=name, grid=(2 * nb,),
        in_specs=[spec, pl.BlockSpec((tr, cols), lambda i: (jnp.minimum(i, nb - 1), 0)),
                  pl.BlockSpec((tr, cols), lambda i: (jnp.maximum(i - nb, 0), 0)), spec, spec],
        out_specs=[spec] * 4, out_shape=[shape] * 4, compiler_params=_params(1))(w, g_layers[0], g_layers[1], m, v)


def _position():
    return lax.axis_index("x"), lax.axis_index("y"), lax.axis_index("c")


def _other_chips(x, y):
    return [(1 - x, y), (x, 1 - y), (1 - x, 1 - y)]


def _run_ride(name, ride):
    n_in, n_out = len(ride.ins), len(ride.out_shape)

    def body(*refs):
        parts = (refs[:n_in], refs[n_in:n_in + n_out], refs[n_in + n_out:])
        ride.start(*parts)
        ride.finish(*parts)

    return pl.pallas_call(
        body, name=name, in_specs=[_ANY] * n_in, out_specs=[_ANY] * n_out, out_shape=ride.out_shape,
        scratch_shapes=ride.scratch, compiler_params=pltpu.CompilerParams(vmem_limit_bytes=VMEM_LIMIT_V7X))(*ride.ins)


def _gather_ride(shards):
    n = len(shards)

    def copies(ins, outs, scr):
        ssem, rsem = scr[n], scr[n + 1]
        x, y, c = _position()
        me, sibling = 2 * x + y, (x, y, 1 - c)

        def remote(src, dst, i, dev):
            return pltpu.make_async_remote_copy(src, dst, ssem.at[i], rsem.at[i], device_id=dev, device_id_type=_MESH)

        fetch_out, fetch_in, pass_out, pass_in = [], [], [], []
        for a, (src, dst) in enumerate(zip(ins, outs)):
            for k, (px, py) in enumerate(_other_chips(x, y)):
                j, i1, i2 = 2 * px + py, 3 * a + k, 3 * n + 3 * a + k
                fetch_out.append(remote(src.at[c], dst.at[me, c], i1, (px, py, c)))
                fetch_in.append(remote(src.at[c], dst.at[j, c], i1, (px, py, c)))
                pass_out.append(remote(dst.at[j, c], dst.at[j, c], i2, sibling))
                pass_in.append(remote(dst.at[j, 1 - c], dst.at[j, 1 - c], i2, sibling))
        return me, fetch_out, fetch_in, pass_out, pass_in

    def start(ins, outs, scr):
        bufs, lsem = scr[:n], scr[n + 2]
        me, fetch_out, _, _, _ = copies(ins, outs, scr)
        for cp in fetch_out:
            cp.start()
        loads = []
        for a, (src, buf) in enumerate(zip(ins, bufs)):
            ld = pltpu.make_async_copy(src, buf, lsem.at[2 * a])
            ld.start()
            loads.append(ld)
        for a, (ld, buf, dst) in enumerate(zip(loads, bufs, outs)):
            ld.wait()
            st = pltpu.make_async_copy(buf, dst.at[me], lsem.at[2 * a + 1])
            st.start()
            st.wait()

    def finish(ins, outs, scr):
        _, fetch_out, fetch_in, pass_out, pass_in = copies(ins, outs, scr)
        for arrived, onward in zip(fetch_in, pass_out):
            arrived.wait_recv()
            onward.start()
        for cp in pass_in:
            cp.wait_recv()
        for cp in fetch_out + pass_out:
            cp.wait_send()

    return _Ride(list(shards), [jax.ShapeDtypeStruct((N_SHARD,) + s.shape, s.dtype) for s in shards],
                 [pltpu.VMEM(s.shape, s.dtype) for s in shards]
                 + [pltpu.SemaphoreType.DMA((6 * n,)), pltpu.SemaphoreType.DMA((6 * n,)), pltpu.SemaphoreType.DMA((2 * n,))],
                 start, finish)


def _comm(name, ins, out_shape, n_remote, plan):
    n_in, n_out = len(ins), len(out_shape)

    def body(*refs):
        plan(refs[:n_in], refs[n_in:n_in + n_out], *refs[n_in + n_out:])

    return pl.pallas_call(
        body, name=name, in_specs=[_ANY] * n_in, out_specs=[_ANY] * n_out, out_shape=out_shape,
        scratch_shapes=[pltpu.SemaphoreType.DMA((n_remote,)), pltpu.SemaphoreType.DMA((n_remote,))])(*ins)


def _send_other_half(name, grads_bf):
    n = len(grads_bf)

    def plan(ins, outs, ssem, rsem):
        x, y, c = _position()
        started = []
        for a, (src, dst) in enumerate(zip(ins, outs)):
            cp = pltpu.make_async_remote_copy(src.at[1 - c], dst, ssem.at[a], rsem.at[a], device_id=(x, y, 1 - c),
                                              device_id_type=_MESH)
            cp.start()
            started.append(cp)
        for cp in started:
            cp.wait_recv()
        for cp in started:
            cp.wait_send()

    shapes = [jax.ShapeDtypeStruct(s.shape[1:], s.dtype) for s in grads_bf]
    return _comm(name, grads_bf, shapes, n, plan)


def _exchange_ride(pair_sums):
    n = len(pair_sums)

    def copies(ins, outs, scr):
        ssem, rsem = scr
        x, y, c = _position()
        return [pltpu.make_async_remote_copy(src.at[2 * px + py], dst.at[k], ssem.at[3 * a + k], rsem.at[3 * a + k],
                                             device_id=(px, py, c), device_id_type=_MESH)
                for a, (src, dst) in enumerate(zip(ins, outs)) for k, (px, py) in enumerate(_other_chips(x, y))]

    def start(ins, outs, scr):
        for cp in copies(ins, outs, scr):
            cp.start()

    def finish(ins, outs, scr):
        for cp in copies(ins, outs, scr):
            cp.wait_recv()
        for cp in copies(ins, outs, scr):
            cp.wait_send()

    return _Ride(list(pair_sums), [jax.ShapeDtypeStruct((3,) + s.shape[1:], s.dtype) for s in pair_sums],
                 [pltpu.SemaphoreType.DMA((3 * n,)), pltpu.SemaphoreType.DMA((3 * n,))], start, finish)


def _swap_reduced(name, grads):
    n = len(grads)

    def body(*refs):
        ins, outs, ssem, rsem = refs[:n], refs[n:2 * n], refs[2 * n], refs[2 * n + 1]
        x, y, c = _position()
        sent = []
        for a, (src, dst) in enumerate(zip(ins, outs)):
            cp = pltpu.make_async_remote_copy(src.at[c], dst.at[c], ssem.at[a], rsem.at[a], device_id=(x, y, 1 - c),
                                              device_id_type=_MESH)
            cp.start()
            sent.append(cp)
        for a, (src, dst) in enumerate(zip(ins, outs)):
            pltpu.make_async_remote_copy(src.at[1 - c], dst.at[1 - c], ssem.at[a], rsem.at[a], device_id=(x, y, 1 - c),
                                         device_id_type=_MESH).wait_recv()
        for cp in sent:
            cp.wait_send()

    return pl.pallas_call(
        body, name=name, in_specs=[_ANY] * n, out_specs=[_ANY] * n,
        out_shape=[jax.ShapeDtypeStruct(g.shape, g.dtype) for g in grads],
        scratch_shapes=[pltpu.SemaphoreType.DMA((n,)), pltpu.SemaphoreType.DMA((n,))],
        input_output_aliases={a: a for a in range(n)})(*grads)


def _sum_small(pack):
    p = pack.shape[0]
    flips = [(dx, dy, dc) for dx in (0, 1) for dy in (0, 1) for dc in (0, 1) if dx + dy + dc]

    def body(in_ref, out_ref, buf, ssem, rsem):
        x, y, c = _position()
        me = 4 * x + 2 * y + c
        buf[me] = in_ref[...]
        started = []
        for k, (dx, dy, dc) in enumerate(flips):
            peer = ((x + dx) % 2, (y + dy) % 2, (c + dc) % 2)
            cp = pltpu.make_async_remote_copy(in_ref, buf.at[me], ssem.at[k], rsem.at[k], device_id=peer,
                                              device_id_type=_MESH)
            cp.start()
            started.append(cp)
        for k, (dx, dy, dc) in enumerate(flips):
            peer = ((x + dx) % 2, (y + dy) % 2, (c + dc) % 2)
            pltpu.make_async_remote_copy(in_ref, buf.at[4 * peer[0] + 2 * peer[1] + peer[2]], ssem.at[k], rsem.at[k],
                                         device_id=peer, device_id_type=_MESH).wait_recv()
        for cp in started:
            cp.wait_send()
        acc = buf[0]
        for d in range(1, 8):
            acc = acc + buf[d]
        out_ref[...] = acc

    vm = pl.BlockSpec(memory_space=pltpu.VMEM)
    return pl.pallas_call(
        body, name="sum_small", in_specs=[vm], out_specs=vm, out_shape=jax.ShapeDtypeStruct((p, PACK_LANES), F32),
        scratch_shapes=[pltpu.VMEM((8, p, PACK_LANES), F32), pltpu.SemaphoreType.DMA((7,)), pltpu.SemaphoreType.DMA((7,))],
        compiler_params=pltpu.CompilerParams(vmem_limit_bytes=VMEM_LIMIT_V7X))(pack)


def _pack(arrays):
    flat = jnp.concatenate([a.reshape(-1).astype(F32) for a in arrays])
    total = flat.shape[0]
    rows = -(-total // (8 * PACK_LANES)) * 8
    return jnp.pad(flat, (0, rows * PACK_LANES - total)).reshape(rows, PACK_LANES)


def _unpack(pack, shapes):
    flat, out, pos = pack.reshape(-1), [], 0
    for sh in shapes:
        size = int(np.prod(sh)) if len(sh) else 1
        out.append(flat[pos:pos + size].reshape(sh))
        pos += size
    return out


def _block_diag(pw):
    out = jnp.zeros((POOL_W, POOL_W), pw.dtype)
    for g in range(4):
        out = out.at[g * 64:(g + 1) * 64, g * 64:(g + 1) * 64].set(pw[g])
    return out


class _NoComm:
    gather_ride = None

    def begin_reduce(self, grads, layer):
        return None


def _local_step(x, c, ctx, c_ctx, small, wf, loss_target, comm):
    cfg = _Cfg(x.shape[0], x.shape[1], ctx.shape[1])
    assert cfg.seq % TM == 0 and cfg.nc % TM == 0 and cfg.seq % cfg.ctx == 0 and cfg.ctx % CHUNK == 0
    nb_all, nb_lat = cfg.r // TM, cfg.nl // TM
    last = 1
    wf = list(wf)
    cos_t, sin_t = _rope_tables(cfg.seq)
    xs = jnp.concatenate([x.reshape(cfg.nl, D), ctx.reshape(cfg.nc, D)], axis=0)
    cc = jnp.concatenate([c, c_ctx[None, :], jnp.zeros((MOD_ROWS - cfg.b - 1, D), F32)], axis=0)
    row = lambda w: pl.BlockSpec((TM, w), lambda i, j, k: (i, 0))
    mod3_spec = pl.BlockSpec((None, 6, D), lambda i, j, k: (cfg.mod_row(i), 0, 0))
    mod_rows_spec = pl.BlockSpec((MOD_ROWS, D), lambda i, j, k: (0, 0))
    mod_w_spec = lambda shard: pl.BlockSpec((None, None, D, MOD_W), lambda i, j, k: (0, shard(j, k), 0, 0))
    whole = lambda rows: pl.BlockSpec((None, rows, D), lambda i, j, k: (0, 0, 0))

    def conv_params(l):
        dw = jnp.pad(wf[l]["conv_dw"], ((0, 1), (0, 0)))
        return (dw, small["conv_dw_b"][l][None], small["conv_ln_g"][l][None], small["conv_ln_b"][l][None],
                _block_diag(small["pool_w"][l]).astype(BF16), small["pool_scale"][l][None])

    def residual_epi(ig):
        def epi(acc, ex, outs):
            x_ref, m_ref = ex
            outs[0][...] = x_ref[...] + m_ref[ig:ig + 1, :] * acc
            outs[1][...] = acc.astype(BF16)
        return epi

    def bias_epi(acc, ex, outs):
        outs[0][...] = acc + ex[0][...]

    saved = []
    for l in range(2):
        nb = nb_lat if l == last else nb_all
        wl = wf[l]
        mvec = _mm(f"mod_fwd{l}", "nn", (1, N_SHARD, 1), cc, wl["w_mod"], mod_rows_spec, mod_w_spec(lambda j, k: j),
                   [jax.ShapeDtypeStruct((MOD_ROWS, 6 * D), F32)], [pl.BlockSpec((MOD_ROWS, MOD_W), lambda i, j, k: (0, j))],
                   extras=[small["b_mod"][l][None]], extra_specs=[pl.BlockSpec((1, MOD_W), lambda i, j, k: (0, j))],
                   a_fn=_silu, epi=bias_epi)[0]
        mod3 = mvec.reshape(MOD_ROWS, 6, D)
        h1 = _norm_fwd(f"norm1_fwd{l}", xs, small["norm1_g"][l][None], mod3, 0, 1, nb_all, cfg)
        qkv, cpin = _in_proj(f"in_proj{l}", h1, wl["w_in"], 0, cos_t, sin_t, cfg)
        ride = comm.gather_ride if l != last and wf[last] is None else None
        (mix, lse), fetched = _att_fwd(f"att_fwd{l}", qkv, small["attn_sink"][l], l != last, cfg, ride)
        if ride is not None:
            wf[last] = comm.weights(fetched)
        prm = conv_params(l)
        mix, yconv = _convpool_fwd(f"convpool_fwd_lat{l}", cpin, mix, None, prm, cfg.seq, 0, cfg)
        if l != last:
            mix, yconv = _convpool_fwd(f"convpool_fwd_ctx{l}", cpin, mix, yconv, prm, cfg.ctx, cfg.nl // cfg.ctx, cfg)
        x1, y1 = _mm(f"out_proj{l}", "nn", (nb, 1, 1), mix, wl["w_out"].reshape(1, D, D), row(D), whole(D),
                     [jax.ShapeDtypeStruct((cfg.r, D), F32), jax.ShapeDtypeStruct((cfg.r, D), BF16)], [row(D), row(D)],
                     extras=[xs, mod3], extra_specs=[row(D), mod3_spec], epi=residual_epi(2))
        h2 = _norm_fwd(f"norm2_fwd{l}", x1, small["norm2_g"][l][None], mod3, 3, 4, nb, cfg)
        gu, act = _ffn_in(f"ffn_in{l}", h2, wl["w_ffn_in"], 0, nb, cfg)
        x2, y2 = _mm(f"ffn_out{l}", "nn", (nb, 1, 1), act, wl["w_ffn_out"].reshape(1, D_FF, D), row(D_FF), whole(D_FF),
                     [jax.ShapeDtypeStruct((cfg.r, D), F32), jax.ShapeDtypeStruct((cfg.r, D), BF16)], [row(D), row(D)],
                     extras=[x1, mod3], extra_specs=[row(D), mod3_spec], epi=residual_epi(5))
        saved.append(dict(mod3=mod3, x0=xs, h1=h1, qkv=qkv, cpin=cpin, mix=mix, yconv=yconv, lse=lse, y1=y1, x1=x1,
                          h2=h2, gu=gu, act=act, y2=y2, prm=prm))
        xs = x2

    dx, loss, d_final_g = _loss_head(xs, loss_target.reshape(cfg.nl, D), small["final_g"][None], cfg)

    big = [dict(), dict()]
    sg = {k: [None, None] for k in ("b_mod", "norm1_g", "norm2_g", "conv_dw", "conv_dw_b", "conv_ln_g", "conv_ln_b",
                                    "attn_sink", "pool_w", "pool_scale")}
    d_c_ctx = jnp.zeros((D,), F32)

    def swiglu_bwd_epi(acc, ex, outs):
        g = ex[0][0].astype(F32)
        u = ex[0][1].astype(F32)
        s = _sigmoid(g)
        outs[0][0] = (acc * u * (s * (1.0 + g * (1.0 - s)))).astype(BF16)
        outs[0][1] = (acc * (g * s)).astype(BF16)

    def halves_epi(acc, ex, outs):
        h = acc.shape[0] // 2
        outs[0][0] = acc[:h].astype(BF16)
        outs[0][1] = acc[h:].astype(BF16)

    def row_shards_epi(n):
        def epi(acc, ex, outs):
            s0 = acc.shape[0] // n
            h = s0 // 2
            for t in range(n):
                for half in range(2):
                    outs[0][half, t] = acc[t * s0 + half * h:t * s0 + (half + 1) * h].astype(BF16)
        return epi

    def col_shards_epi(acc, ex, outs):
        h = acc.shape[0] // 2
        for j in range(N_SHARD):
            for half in range(2):
                outs[0][half, j] = acc[half * h:(half + 1) * h, j * IN_SHARD:(j + 1) * IN_SHARD].astype(BF16)

    def mod_dw_epi(acc, ex, outs):
        halves_epi(acc, ex, outs)
        outs[1][...] = _colsum(ex[0][...])

    def dsilu_epi(acc, ex, outs):
        outs[0][...] = acc * _dsilu(ex[0][...])

    for l in (1, 0):
        sv = saved[l]
        mod3 = sv["mod3"]
        nb = nb_lat if l == last else nb_all
        tr = _dw_rows(nb * TM)
        steps = nb * TM // tr
        wl = wf[l]
        ride = comm.begin_reduce(big[last], last) if l != last else None
        dy2, dg2 = _gate_bwd(f"gate2_bwd{l}", dx, sv["y2"], mod3, 5, nb, cfg)
        gu_spec = pl.BlockSpec((2, TM, HALF_FF), lambda i, j, k: (0, i, j))
        df = _mm(f"ffn_out_bwd{l}", "nt", (nb, 2, 1), dy2, wl["w_ffn_out"].reshape(1, D_FF, D), row(D),
                 pl.BlockSpec((None, HALF_FF, D), lambda i, j, k: (0, j, 0)),
                 [jax.ShapeDtypeStruct((2, cfg.r, D_FF), BF16)], [gu_spec], extras=[sv["gu"]], extra_specs=[gu_spec],
                 epi=swiglu_bwd_epi)[0]
        big[l]["w_ffn_out"] = _mm_dw(
            f"dw_ffn_out{l}", sv["act"], dy2, pl.BlockSpec((tr, HALF_FF), lambda i, j, k: (k, j)),
            pl.BlockSpec((tr, D), lambda i, j, k: (k, 0)), (2, N_SHARD, D_FF // 8, D),
            pl.BlockSpec((2, 2, D_FF // 8, D), lambda i, j, k: (0, j, 0, 0)), 2, (HALF_FF, D), steps, row_shards_epi(2))[0]
        res = _mm_dw(f"dw_ffn_in{l}", sv["h2"], df, pl.BlockSpec((tr, D), lambda i, j, k: (k, 0)),
                     pl.BlockSpec((None, tr, HALF_FF), lambda i, j, k: (j // 2, k, j % 2)), (2, N_SHARD, D // 2, HALF_FF),
                     pl.BlockSpec((2, None, D // 2, HALF_FF), lambda i, j, k: (0, j, 0, 0)), N_SHARD, (D, HALF_FF), steps,
                     halves_epi, ride=ride)
        if ride is not None:
            res, exchanged = res
            comm.end_reduce(exchanged, last)
        big[l]["w_ffn_in"] = res[0]
        dh2 = _proj_bwd(f"ffn_in_bwd{l}", df, pl.BlockSpec((2, TM, D_FF), lambda i: (0, i, 0)),
                        lambda a_ref, j: a_ref[j // 2, :, (j % 2) * HALF_FF:(j % 2 + 1) * HALF_FF], wl["w_ffn_in"], 0, nb, cfg)
        dx1, dsh2, dsc2, dn2 = _norm_bwd(f"norm2_bwd{l}", sv["x1"], dh2, dx, small["norm2_g"][l][None], mod3, 4, nb, False, cfg)
        dy1, dg1 = _gate_bwd(f"gate1_bwd{l}", dx1, sv["y1"], mod3, 2, nb, cfg)
        dmix = _mm(f"out_proj_bwd{l}", "nt", (nb, 1, 1), dy1, wl["w_out"].reshape(1, D, D), row(D), whole(D),
                   [jax.ShapeDtypeStruct((cfg.r, D), F32)], [row(D)])[0]
        big[l]["w_out"] = _mm_dw(
            f"dw_out{l}", sv["mix"], dy1, pl.BlockSpec((tr, D), lambda i, j, k: (k, 0)),
            pl.BlockSpec((tr, D), lambda i, j, k: (k, 0)), (2, N_SHARD, D // 8, D),
            pl.BlockSpec((2, N_SHARD, D // 8, D), lambda i, j, k: (0, 0, 0, 0)), 1, (D, D), steps, row_shards_epi(N_SHARD))[0]
        du, dkvl, dkvc, dsink = _att_bwd(f"att_bwd{l}", sv["qkv"], sv["mix"], dmix, sv["lse"], small["attn_sink"][l],
                                         cos_t, sin_t, l != last, cfg)
        acc = [du] + [jnp.zeros(sh, F32) for sh in _SMALL_SHAPES]
        acc = _convpool_bwd(f"convpool_bwd_lat{l}", sv["cpin"], sv["yconv"], dmix, sv["prm"], acc, cfg.seq, 0, cfg)
        if l != last:
            acc = _convpool_bwd(f"convpool_bwd_ctx{l}", sv["cpin"], sv["yconv"], dmix, sv["prm"], acc, cfg.ctx,
                                cfg.nl // cfg.ctx, cfg)
        du, g_dw, g_dwb, g_lng, g_lnb, g_wbd, g_ps = acc
        du = _place_kv(f"place_kv{l}", du, dkvl, dkvc, l != last, cfg)
        if l == last:
            du = _place_ctx_kv_only(f"place_ctx_kv{l}", du, dkvc, cfg)
        sg["attn_sink"][l] = dsink[:, 0]
        sg["conv_dw"][l], sg["conv_dw_b"][l], sg["conv_ln_g"][l], sg["conv_ln_b"][l] = g_dw[:CONV_K], g_dwb[0], g_lng[0], g_lnb[0]
        sg["pool_w"][l] = jnp.stack([g_wbd[g * 64:(g + 1) * 64, g * 64:(g + 1) * 64] for g in range(4)])
        sg["pool_scale"][l] = g_ps[0]
        tr_all = _dw_rows(cfg.r)
        big[l]["w_in"] = _mm_dw(
            f"dw_in{l}", sv["h1"], du, pl.BlockSpec((tr_all, D), lambda i, j, k: (k, 0)),
            pl.BlockSpec((tr_all, IN_W), lambda i, j, k: (k, 0)), (2, N_SHARD, D // 2, IN_SHARD),
            pl.BlockSpec((2, N_SHARD, D // 2, IN_SHARD), lambda i, j, k: (0, 0, 0, 0)), 1, (D, IN_W), cfg.r // tr_all,
            col_shards_epi)[0]
        dh1 = _proj_bwd(f"in_proj_bwd{l}", du, pl.BlockSpec((TM, IN_W), lambda i: (i, 0)),
                        lambda a_ref, j: a_ref[:, j * IN_SHARD:(j + 1) * IN_SHARD], wl["w_in"], 0, nb_all, cfg)
        dx, dsh1, dsc1, dn1 = _norm_bwd(f"norm1_bwd{l}", sv["x0"], dh1, dx1, small["norm1_g"][l][None], mod3, 1, nb_all,
                                        l == last, cfg)
        sg["norm1_g"][l], sg["norm2_g"][l] = dn1[0], dn2[0]
        parts = [dsh1, dsc1, dg1, dsh2, dsc2, dg2]
        dm = jnp.concatenate([t[:cfg.b, 0, :] for t in parts], axis=1)
        live = (0, 1) if l == last else range(6)
        dm_ctx = jnp.concatenate([t[cfg.b, 0, :] if i in live else jnp.zeros((D,), F32) for i, t in enumerate(parts)])
        dm = jnp.concatenate([dm, dm_ctx[None, :], jnp.zeros((MOD_ROWS - cfg.b - 1, 6 * D), F32)], axis=0)
        dm_spec = pl.BlockSpec((MOD_ROWS, MOD_W), lambda i, j, k: (0, j))
        res = _mm_dw(f"dw_mod{l}", cc, dm, mod_rows_spec, dm_spec, (2, N_SHARD, D // 2, MOD_W),
                     pl.BlockSpec((2, None, D // 2, MOD_W), lambda i, j, k: (0, j, 0, 0)), N_SHARD, None, 1, mod_dw_epi,
                     a_fn=_silu, extras=[dm], extra_specs=[dm_spec], extra_out=[jax.ShapeDtypeStruct((1, 6 * D), F32)],
                     extra_out_specs=[pl.BlockSpec((1, MOD_W), lambda i, j, k: (0, j))])
        big[l]["w_mod"], sg["b_mod"][l] = res[0], res[1][0]
        dcc = _mm(f"mod_bwd{l}", "nt", (1, 1, N_SHARD), dm, wl["w_mod"], pl.BlockSpec((MOD_ROWS, MOD_W), lambda i, j, k: (0, k)),
                  mod_w_spec(lambda j, k: k), [jax.ShapeDtypeStruct((MOD_ROWS, D), F32)], [mod_rows_spec],
                  acc_shape=(MOD_ROWS, D), extras=[cc], extra_specs=[mod_rows_spec], epi=dsilu_epi)[0]
        d_c_ctx = d_c_ctx + dcc[cfg.b]

    grad_x = dx[:cfg.nl].reshape(x.shape)
    small_grads = {k: jnp.stack(v) for k, v in sg.items()}
    small_grads["c_ctx"] = d_c_ctx
    small_grads["final_g"] = d_final_g[0]
    return loss, grad_x, small_grads, big


_BIG = ("w_mod", "w_in", "w_out", "w_ffn_in", "w_ffn_out")
_SMALL = ("c_ctx", "b_mod", "norm1_g", "norm2_g", "conv_dw", "conv_dw_b", "conv_ln_g", "conv_ln_b", "attn_sink",
          "pool_w", "pool_scale", "final_g")
_ORDER = ("c_ctx", "w_mod", "b_mod", "norm1_g", "norm2_g", "w_in", "conv_dw", "conv_dw_b", "conv_ln_g", "conv_ln_b",
          "attn_sink", "pool_w", "pool_scale", "w_out", "w_ffn_in", "w_ffn_out", "final_g")


class _Comm:
    def __init__(self, w, c_idx, jc_idx):
        self.shapes = {k: w[k].shape[1:] for k in _BIG}
        self.c_idx, self.jc_idx = c_idx, jc_idx
        halves = lambda a: a.reshape(2, a.shape[0] // 2, a.shape[1])
        taps = jnp.pad(w["conv_dw"], ((0, 0), (0, 1), (0, 64)))
        cast = {k: w[k].astype(BF16) for k in _BIG}
        self.shards = [[halves(cast[k][l]) for k in _BIG] + [halves(taps[l])] for l in range(2)]
        self.gather_ride = _gather_ride(self.shards[1])
        self.reduced = [None, None]
        self._pair = None

    def weights(self, fetched):
        out = {k: f.reshape((1, N_SHARD) + self.shapes[k]) for k, f in zip(_BIG, fetched)}
        taps = fetched[len(_BIG)].reshape(N_SHARD, 32, 128)[:, :CONV_K, :64]
        out["conv_dw"] = jnp.transpose(taps, (1, 0, 2)).reshape(CONV_K, CONV_W)
        return out

    def begin_reduce(self, grads, layer):
        mine = [grads[k] for k in _BIG]
        other = _send_other_half(f"send_other_half{layer}", mine)
        self._pair = [_pair_add(f"pair_add{layer}_{k}", a, b, self.c_idx) for k, a, b in zip(_BIG, mine, other)]
        return _exchange_ride(self._pair)

    def end_reduce(self, exchanged, layer):
        mine = [_shard_sum(f"shard_sum{layer}_{k}", a, b, self.jc_idx) for k, a, b in zip(_BIG, self._pair, exchanged)]
        swapped = _swap_reduced(f"swap_reduced{layer}", mine)
        self.reduced[layer] = [g.reshape(self.shapes[k]) for k, g in zip(_BIG, swapped)]


def kernel(x, c, ctx, c_ctx, w_mod, b_mod, norm1_g, norm2_g, w_in, conv_dw, conv_dw_b, conv_ln_g, conv_ln_b, attn_sink, pool_w, pool_scale, w_out, w_ffn_in, w_ffn_out, final_g, loss_target, m_c_ctx, m_w_mod, m_b_mod, m_norm1_g, m_norm2_g, m_w_in, m_conv_dw, m_conv_dw_b, m_conv_ln_g, m_conv_ln_b, m_attn_sink, m_pool_w, m_pool_scale, m_w_out, m_w_ffn_in, m_w_ffn_out, m_final_g, v_c_ctx, v_w_mod, v_b_mod, v_norm1_g, v_norm2_g, v_w_in, v_conv_dw, v_conv_dw_b, v_conv_ln_g, v_conv_ln_b, v_attn_sink, v_pool_w, v_pool_scale, v_w_out, v_w_ffn_in, v_w_ffn_out, v_final_g):
    w = dict(c_ctx=c_ctx, w_mod=w_mod, b_mod=b_mod, norm1_g=norm1_g, norm2_g=norm2_g, w_in=w_in, conv_dw=conv_dw,
             conv_dw_b=conv_dw_b, conv_ln_g=conv_ln_g, conv_ln_b=conv_ln_b, attn_sink=attn_sink, pool_w=pool_w,
             pool_scale=pool_scale, w_out=w_out, w_ffn_in=w_ffn_in, w_ffn_out=w_ffn_out, final_g=final_g)
    m = dict(c_ctx=m_c_ctx, w_mod=m_w_mod, b_mod=m_b_mod, norm1_g=m_norm1_g, norm2_g=m_norm2_g, w_in=m_w_in,
             conv_dw=m_conv_dw, conv_dw_b=m_conv_dw_b, conv_ln_g=m_conv_ln_g, conv_ln_b=m_conv_ln_b,
             attn_sink=m_attn_sink, pool_w=m_pool_w, pool_scale=m_pool_scale, w_out=m_w_out, w_ffn_in=m_w_ffn_in,
             w_ffn_out=m_w_ffn_out, final_g=m_final_g)
    v = dict(c_ctx=v_c_ctx, w_mod=v_w_mod, b_mod=v_b_mod, norm1_g=v_norm1_g, norm2_g=v_norm2_g, w_in=v_w_in,
             conv_dw=v_conv_dw, conv_dw_b=v_conv_dw_b, conv_ln_g=v_conv_ln_g, conv_ln_b=v_conv_ln_b,
             attn_sink=v_attn_sink, pool_w=v_pool_w, pool_scale=v_pool_scale, w_out=v_w_out, w_ffn_in=v_w_ffn_in,
             w_ffn_out=v_w_ffn_out, final_g=v_final_g)
    xi, yi, ci = _position()
    comm = _Comm(w, jnp.reshape(ci, (1,)).astype(jnp.int32), jnp.stack([2 * xi + yi, ci]).astype(jnp.int32))
    small = {k: w[k] for k in _SMALL if k != "conv_dw"}

    wf0 = comm.weights(_run_ride("gather_weights0", _gather_ride(comm.shards[0])))
    loss, grad_x, sgrads, big = _local_step(x, c, ctx, c_ctx, small, [wf0, None], loss_target, comm)
    comm.end_reduce(_run_ride("exchange_shards0", comm.begin_reduce(big[0], 0)), 0)

    names = list(_SMALL)
    total = _sum_small(_pack([loss] + [sgrads[k] for k in names]))
    parts = _unpack(total, [()] + [sgrads[k].shape for k in names])
    loss_out = parts[0]
    gsmall = dict(zip(names, parts[1:]))
    gsmall["conv_dw"] = lax.dynamic_slice_in_dim(gsmall["conv_dw"], (2 * xi + yi) * 64, 64, axis=2)

    grads, delta, new_m, new_v = dict(gsmall), {}, {}, {}
    for i, k in enumerate(_BIG):
        s0, s1 = comm.shapes[k]
        flat = lambda a: a.reshape(2 * s0, s1)
        g_, d_, m_, v_ = _adamw_layers(f"adamw_{k}", flat(w[k]), [comm.reduced[l][i] for l in range(2)], flat(m[k]), flat(v[k]))
        grads[k], delta[k], new_m[k], new_v[k] = [a.reshape(w[k].shape) for a in (g_, d_, m_, v_)]
    d_, m_, v_ = _adamw("adamw_small", _pack([w[k] for k in names]), _pack([gsmall[k] for k in names]),
                        _pack([m[k] for k in names]), _pack([v[k] for k in names]))
    sshapes = [w[k].shape for k in names]
    for k, a, b, e in zip(names, _unpack(d_, sshapes), _unpack(m_, sshapes), _unpack(v_, sshapes)):
        delta[k], new_m[k], new_v[k] = a, b, e
    return (loss_out, grad_x, *[grads[k] for k in _ORDER], *[delta[k] for k in _ORDER],
            *[new_m[k] for k in _ORDER], *[new_v[k] for k in _ORDER])
```

```python
from typing import NamedTuple

import jax
import jax.numpy as jnp
import numpy as np
from jax import lax
from jax.experimental import pallas as pl
from jax.experimental.pallas import tpu as pltpu

F32 = jnp.float32
BF16 = jnp.bfloat16

D = 1024
GRID_W = 64
HEAD_DIM = 64
N_HEADS = 8
ATTN_W = 512
CONV_W = 256
POOL_W = 256
IN_W = 1536
D_FF = 2816
CONV_K = 31
QB = 128
ROPE_BASE = 10000.0
EPS = 1e-6
NEG = -1e30
N_SHARD = 4
IN_SHARD = IN_W // N_SHARD
HALF_FF = D_FF // 2
MOD_W = 6 * D // N_SHARD
MOD_ROWS = 16
PACK_LANES = 128

ADAM_LR = 0.001
ADAM_B1 = 0.9
ADAM_B2 = 0.999
ADAM_EPS = 1e-08
ADAM_WD = 0.01
ADAM_STEP = 10

VMEM_LIMIT_V7X = 56 * 1024 * 1024
TM = 512
TR_MAX = 1024
CHUNK = 256
PAD = 16

_MESH = pl.DeviceIdType.MESH
_ANY = pl.BlockSpec(memory_space=pl.ANY)
_DIMS = {"nn": (((1,), (0,)), ((), ())), "nt": (((1,), (1,)), ((), ())), "tn": (((0,), (0,)), ((), ()))}


class _Cfg(NamedTuple):
    b: int
    seq: int
    ctx: int

    @property
    def nl(self):
        return self.b * self.seq

    @property
    def nc(self):
        return self.b * self.ctx

    @property
    def r(self):
        return self.nl + self.nc

    def mod_row(self, i):
        return jnp.where(i < self.nl // TM, i // (self.seq // TM), self.b)

    def first_of_row(self, i):
        nlb = self.nl // TM
        return jnp.logical_or(jnp.logical_and(i < nlb, i % (self.seq // TM) == 0), i == nlb)


def _params(n_grid=0):
    sem = ("arbitrary",) * n_grid if n_grid else None
    return pltpu.CompilerParams(dimension_semantics=sem, vmem_limit_bytes=VMEM_LIMIT_V7X)


def _dot(a, b, mode="nn"):
    return lax.dot_general(a.astype(BF16), b.astype(BF16), _DIMS[mode], preferred_element_type=F32)


def _sigmoid(x):
    return 1.0 / (1.0 + jnp.exp(-x))


def _silu(x):
    return x * _sigmoid(x)


def _dsilu(x):
    s = _sigmoid(x)
    return s * (1.0 + x * (1.0 - s))


def _colsum(v):
    return jnp.sum(v, axis=0, keepdims=True)


def _dw_rows(rows):
    return TR_MAX if rows % TR_MAX == 0 else TM


def _epi_store(acc, ex, outs):
    for o in outs:
        o[...] = acc.astype(o.dtype)


class _Ride(NamedTuple):
    ins: list
    out_shape: list
    scratch: list
    start: object
    finish: object


class _Hosted(NamedTuple):
    ride: _Ride
    n_in: int
    n_out: int
    grid: tuple

    def split(self, refs):
        n_ri, n_ro, n_rs = len(self.ride.ins), len(self.ride.out_shape), len(self.ride.scratch)
        r_in = refs[self.n_in:self.n_in + n_ri]
        r_out = refs[self.n_in + n_ri + self.n_out:self.n_in + n_ri + self.n_out + n_ro]
        own = refs[:self.n_in] + refs[self.n_in + n_ri:self.n_in + n_ri + self.n_out] + \
            refs[self.n_in + n_ri + self.n_out + n_ro:len(refs) - n_rs]
        return own, (r_in, r_out, refs[len(refs) - n_rs:])

    def start(self, parts):
        ids = [pl.program_id(d) for d in range(len(self.grid))]
        first = ids[0] == 0
        for i in ids[1:]:
            first = jnp.logical_and(first, i == 0)
        pl.when(first)(lambda: self.ride.start(*parts))

    def finish(self, parts):
        ids = [pl.program_id(d) for d in range(len(self.grid))]
        last = ids[0] == self.grid[0] - 1
        for i, g in zip(ids[1:], self.grid[1:]):
            last = jnp.logical_and(last, i == g - 1)
        pl.when(last)(lambda: self.ride.finish(*parts))


def _hosted_call(body, ride, name, grid, ins, in_specs, out_shape, out_specs, scratch, params):
    if ride is None:
        res = pl.pallas_call(body, name=name, grid=grid, in_specs=in_specs, out_specs=out_specs, out_shape=out_shape,
                             scratch_shapes=scratch, compiler_params=params)(*ins)
        return list(res), []
    host = _Hosted(ride, len(ins), len(out_shape), tuple(grid))

    def hosted(*refs):
        own, parts = host.split(refs)
        host.start(parts)
        body(*own)
        host.finish(parts)

    res = pl.pallas_call(
        hosted, name=name, grid=grid, in_specs=list(in_specs) + [_ANY] * len(ride.ins),
        out_specs=list(out_specs) + [_ANY] * len(ride.out_shape), out_shape=list(out_shape) + list(ride.out_shape),
        scratch_shapes=list(scratch) + list(ride.scratch), compiler_params=params)(*ins, *ride.ins)
    return list(res[:len(out_shape)]), list(res[len(out_shape):])


def _mm(name, mode, grid, a, b, a_spec, b_spec, out_shape, out_specs, acc_shape=None, extras=(),
        extra_specs=(), a_fn=None, epi=_epi_store, ride=None):
    nk = grid[2]
    n_ex, n_out = len(extras), len(out_shape)

    def body(*refs):
        a_ref, b_ref = refs[:2]
        ex = refs[2:2 + n_ex]
        outs = refs[2 + n_ex:2 + n_ex + n_out]
        av = a_ref[...]
        if a_fn is not None:
            av = a_fn(av)
        part = _dot(av, b_ref[...], mode)
        if nk == 1:
            epi(part, ex, outs)
        else:
            acc = refs[-1]
            k = pl.program_id(2)

            @pl.when(k == 0)
            def _():
                acc[...] = part

            @pl.when(k > 0)
            def _():
                acc[...] += part

            @pl.when(k == nk - 1)
            def _():
                epi(acc[...], ex, outs)

    scratch = [] if nk == 1 else [pltpu.VMEM(acc_shape, F32)]
    outs, ride_outs = _hosted_call(body, ride, name, grid, [a, b, *extras], [a_spec, b_spec, *extra_specs],
                                   list(out_shape), list(out_specs), scratch, _params(3))
    return outs if ride is None else (outs, ride_outs)


def _mm_dw(name, a, b, a_spec, b_spec, out_shape, out_spec, n_out_blocks, acc_shape, n_steps, epi, a_fn=None,
           extras=(), extra_specs=(), extra_out=(), extra_out_specs=(), ride=None):
    grid = (1, n_out_blocks, n_steps)
    outs = [jax.ShapeDtypeStruct(out_shape, BF16)] + list(extra_out)
    return _mm(name, "tn", grid, a, b, a_spec, b_spec, outs, [out_spec, *extra_out_specs], acc_shape, extras,
               extra_specs, a_fn, epi, ride)


def _proj_bwd(name, a, a_spec, pick, w, layer, nblk, cfg):
    ns = w.shape[-1]

    def body(a_ref, w_ref, o_ref):
        acc = _dot(pick(a_ref, 0), w_ref[0], "nt")
        for j in range(1, N_SHARD):
            acc = acc + _dot(pick(a_ref, j), w_ref[j], "nt")
        o_ref[...] = acc

    return pl.pallas_call(
        body, name=name, grid=(nblk,),
        in_specs=[a_spec, pl.BlockSpec((None, N_SHARD, D, ns), lambda i: (layer, 0, 0, 0))],
        out_specs=pl.BlockSpec((TM, D), lambda i: (i, 0)), out_shape=jax.ShapeDtypeStruct((cfg.r, D), F32),
        compiler_params=_params(1))(a, w)


def _mod_spec(cfg):
    return pl.BlockSpec((None, 6, D), lambda i: (cfg.mod_row(i), 0, 0))


def _norm_fwd(name, x, gvec, mod3, ish, isc, nblk, cfg):
    def body(x_ref, g_ref, m_ref, o_ref):
        xv = x_ref[...]
        r = lax.rsqrt(jnp.mean(xv * xv, axis=-1, keepdims=True) + EPS)
        o_ref[...] = (xv * r * g_ref[...] * (1.0 + m_ref[isc:isc + 1, :]) + m_ref[ish:ish + 1, :]).astype(BF16)

    row = pl.BlockSpec((TM, D), lambda i: (i, 0))
    return pl.pallas_call(
        body, name=name, grid=(nblk,),
        in_specs=[row, pl.BlockSpec((1, D), lambda i: (0, 0)), _mod_spec(cfg)], out_specs=row,
        out_shape=jax.ShapeDtypeStruct((cfg.r, D), BF16), compiler_params=_params(1))(x, gvec, mod3)


def _accumulate_rows(first, ref, val):
    @pl.when(first)
    def _():
        ref[...] = val

    @pl.when(jnp.logical_not(first))
    def _():
        ref[...] += val


def _norm_bwd(name, x, dh, dres, gvec, mod3, isc, nblk, res_latent_only, cfg):
    nlb = cfg.nl // TM

    def body(x_ref, dh_ref, dres_ref, g_ref, m_ref, dx_ref, dsh_ref, dsc_ref, dg_ref):
        i = pl.program_id(0)
        xv = x_ref[...]
        r = lax.rsqrt(jnp.mean(xv * xv, axis=-1, keepdims=True) + EPS)
        xh = xv * r
        g = g_ref[...]
        sc1 = 1.0 + m_ref[isc:isc + 1, :]
        dhv = dh_ref[...]
        t = dhv * xh
        first = cfg.first_of_row(i)
        _accumulate_rows(first, dsh_ref, _colsum(dhv))
        _accumulate_rows(first, dsc_ref, _colsum(t * g))
        _accumulate_rows(i == 0, dg_ref, _colsum(t * sc1))
        dxh = dhv * (g * sc1)
        dxn = r * (dxh - xh * jnp.mean(dxh * xh, axis=-1, keepdims=True))
        if res_latent_only:
            dx_ref[...] = jnp.where(i < nlb, dres_ref[...], 0.0) + dxn
        else:
            dx_ref[...] = dres_ref[...] + dxn

    row = pl.BlockSpec((TM, D), lambda i: (i, 0))
    vec = pl.BlockSpec((1, D), lambda i: (0, 0))
    part = pl.BlockSpec((None, 1, D), lambda i: (cfg.mod_row(i), 0, 0))
    part_shape = jax.ShapeDtypeStruct((MOD_ROWS, 1, D), F32)
    return pl.pallas_call(
        body, name=name, grid=(nblk,), in_specs=[row, row, row, vec, _mod_spec(cfg)],
        out_specs=[row, part, part, vec],
        out_shape=[jax.ShapeDtypeStruct((cfg.r, D), F32), part_shape, part_shape, jax.ShapeDtypeStruct((1, D), F32)],
        compiler_params=_params(1))(x, dh, dres, gvec, mod3)


def _gate_bwd(name, dx, y, mod3, ig, nblk, cfg):
    def body(dx_ref, y_ref, m_ref, dy_ref, dg_ref):
        i = pl.program_id(0)
        dxv = dx_ref[...]
        dy_ref[...] = (dxv * m_ref[ig:ig + 1, :]).astype(BF16)
        _accumulate_rows(cfg.first_of_row(i), dg_ref, _colsum(dxv * y_ref[...].astype(F32)))

    row = pl.BlockSpec((TM, D), lambda i: (i, 0))
    part = pl.BlockSpec((None, 1, D), lambda i: (cfg.mod_row(i), 0, 0))
    return pl.pallas_call(
        body, name=name, grid=(nblk,), in_specs=[row, row, _mod_spec(cfg)], out_specs=[row, part],
        out_shape=[jax.ShapeDtypeStruct((cfg.r, D), BF16), jax.ShapeDtypeStruct((MOD_ROWS, 1, D), F32)],
        compiler_params=_params(1))(dx, y, mod3)


def _loss_head(x, target, gvec, cfg):
    def body(x_ref, t_ref, g_ref, dx_ref, loss_ref, dg_ref):
        i = pl.program_id(0)

        @pl.when(i == 0)
        def _():
            loss_ref[...] = jnp.zeros_like(loss_ref)
            dg_ref[...] = jnp.zeros_like(dg_ref)

        xv = x_ref[...]
        g = g_ref[...]
        r = lax.rsqrt(jnp.mean(xv * xv, axis=-1, keepdims=True) + EPS)
        xh = xv * r
        err = xh * g - t_ref[...]
        loss_ref[...] += (0.5 / D) * _colsum(jnp.sum(err * err, axis=-1, keepdims=True))
        dy = err * (1.0 / D)
        dg_ref[...] += _colsum(dy * xh)
        dxh = dy * g
        dx_ref[...] = r * (dxh - xh * jnp.mean(dxh * xh, axis=-1, keepdims=True))

    row = pl.BlockSpec((TM, D), lambda i: (i, 0))
    vec = pl.BlockSpec((1, D), lambda i: (0, 0))
    return pl.pallas_call(
        body, name="loss_head", grid=(cfg.nl // TM,), in_specs=[row, row, vec],
        out_specs=[row, pl.BlockSpec((1, 1), lambda i: (0, 0)), vec],
        out_shape=[jax.ShapeDtypeStruct((cfg.r, D), F32), jax.ShapeDtypeStruct((1, 1), F32),
                   jax.ShapeDtypeStruct((1, D), F32)],
        compiler_params=_params(1))(x, target, gvec)


def _rope_tables(seq):
    rows = seq // GRID_W
    row = jnp.repeat(jnp.arange(rows), GRID_W).astype(F32)
    col = jnp.tile(jnp.arange(GRID_W), rows).astype(F32)
    half = HEAD_DIM // 2
    inv = ROPE_BASE ** (-jnp.arange(0, half, 2, dtype=F32) / half)
    ar, ac = row[:, None] * inv, col[:, None] * inv
    ang = jnp.concatenate([ar, ar, ac, ac], axis=-1)
    sign = jnp.tile(jnp.concatenate([-jnp.ones((16,), F32), jnp.ones((16,), F32)]), 2)
    cos = jnp.tile(jnp.cos(ang), (1, 2))
    sin = jnp.tile(jnp.sin(ang) * sign, (1, 2))
    cos = jnp.concatenate([cos, jnp.ones((TM, 2 * HEAD_DIM), F32)], axis=0)
    sin = jnp.concatenate([sin, jnp.zeros((TM, 2 * HEAD_DIM), F32)], axis=0)
    return cos, sin


def _rope(x, cos, sin_signed, sign):
    lane = lax.broadcasted_iota(jnp.int32, x.shape, 1)
    low = (lane % 32) < 16
    rot = jnp.where(low, pltpu.roll(x, 112, 1), pltpu.roll(x, 16, 1))
    return x * cos + sign * (rot * sin_signed)


def _in_proj(name, h, w_in, layer, cos_t, sin_t, cfg):
    nlb, bps = cfg.nl // TM, cfg.seq // TM

    def body(h_ref, w_ref, cos_ref, sin_ref, qkv_ref, cp_ref):
        hv = h_ref[...]
        u = jnp.concatenate([_dot(hv, w_ref[j]) for j in range(N_SHARD)], axis=1)
        cos, sin = cos_ref[...], sin_ref[...]
        tiles = []
        for t in range(5):
            y = _rope(u[:, 128 * t:128 * (t + 1)], cos, sin, 1.0)
            tiles.append(y * (HEAD_DIM ** -0.5) if t < 4 else y)
        tiles.append(u[:, 640:768])
        qkv_ref[...] = jnp.concatenate(tiles, axis=1).astype(BF16)
        cp_ref[...] = u[:, 768:IN_W].astype(BF16)

    tab = pl.BlockSpec((TM, 128), lambda i: (jnp.where(i < nlb, i % bps, bps), 0))
    half = pl.BlockSpec((TM, 768), lambda i: (i, 0))
    return pl.pallas_call(
        body, name=name, grid=(cfg.r // TM,),
        in_specs=[pl.BlockSpec((TM, D), lambda i: (i, 0)),
                  pl.BlockSpec((None, N_SHARD, D, IN_SHARD), lambda i: (layer, 0, 0, 0)), tab, tab],
        out_specs=[half, half],
        out_shape=[jax.ShapeDtypeStruct((cfg.r, 768), BF16), jax.ShapeDtypeStruct((cfg.r, 768), BF16)],
        compiler_params=_params(1))(h, w_in, cos_t, sin_t)


def _att_specs(cfg):
    nlb, ncb = cfg.seq // QB, cfg.ctx // QB

    def qblk(s, qb):
        return jnp.where(qb < nlb, s * nlb + qb, cfg.nl // QB + s * ncb + qb - nlb)

    def near(off, col):
        return pl.BlockSpec((QB, 128), lambda s, qb: (s * nlb + jnp.clip(qb + off, 0, nlb - 1), col))

    def ctxs(col):
        return pl.BlockSpec((cfg.ctx, 128), lambda s, qb: (cfg.nl // cfg.ctx + s, col))

    qspec = pl.BlockSpec((QB, ATTN_W), lambda s, qb: (qblk(s, qb), 0))
    kv = [ctxs(4), ctxs(5), near(-1, 4), near(0, 4), near(1, 4), near(-1, 5), near(0, 5), near(1, 5)]
    return qblk, qspec, kv


def _att_scores(qb, nlb, sink_ref, q_ref, k_refs, kh):
    is_lat = qb < nlb
    ii = lax.broadcasted_iota(jnp.int32, (4 * QB, QB), 0) % QB
    jj = lax.broadcasted_iota(jnp.int32, (4 * QB, QB), 1)
    off_p = jnp.where(jnp.logical_and(is_lat, qb >= 1), 0.0, NEG)
    off_c = jnp.where(is_lat, 0.0, NEG)
    off_n = jnp.where(jnp.logical_and(is_lat, qb <= nlb - 2), 0.0, NEG)
    q4 = jnp.concatenate([q_ref[:, (4 * kh + g) * HEAD_DIM:(4 * kh + g + 1) * HEAD_DIM] for g in range(4)], axis=0)
    rg = lax.broadcasted_iota(jnp.int32, (4 * QB, 1), 0) // QB
    snk = jnp.where(rg == 0, sink_ref[4 * kh],
                    jnp.where(rg == 1, sink_ref[4 * kh + 1], jnp.where(rg == 2, sink_ref[4 * kh + 2], sink_ref[4 * kh + 3])))
    lanes = slice(kh * HEAD_DIM, (kh + 1) * HEAD_DIM)
    kx, kp, kc, kn = [r[:, lanes] for r in k_refs]
    sx = _dot(q4, kx, "nt")
    sp = jnp.where(jj >= ii, _dot(q4, kp, "nt"), NEG) + off_p
    sc = _dot(q4, kc, "nt") + off_c
    sn = jnp.where(jj <= ii, _dot(q4, kn, "nt"), NEG) + off_n
    return q4, snk, (kx, kp, kc, kn), (sx, sp, sc, sn)


def _att_fwd(name, qkv, sink, ctx_queries, cfg, ride=None):
    nlb, ncb = cfg.seq // QB, cfg.ctx // QB
    qblk, qspec, kvspecs = _att_specs(cfg)

    def body(sink_ref, q_ref, kx_ref, vx_ref, kp_ref, kc_ref, kn_ref, vp_ref, vc_ref, vn_ref, o_ref, lse_ref):
        qb = pl.program_id(1)
        for kh in range(2):
            q4, snk, _, ss = _att_scores(qb, nlb, sink_ref, q_ref, (kx_ref, kp_ref, kc_ref, kn_ref), kh)
            lanes = slice(kh * HEAD_DIM, (kh + 1) * HEAD_DIM)
            vs = [r[:, lanes] for r in (vx_ref, vp_ref, vc_ref, vn_ref)]
            m = snk
            for s_ in ss:
                m = jnp.maximum(m, jnp.max(s_, axis=-1, keepdims=True))
            den = jnp.exp(snk - m)
            o4 = jnp.zeros((4 * QB, HEAD_DIM), F32)
            for s_, v_ in zip(ss, vs):
                p = jnp.exp(s_ - m)
                den = den + jnp.sum(p, axis=-1, keepdims=True)
                o4 = o4 + _dot(p, v_)
            o4 = o4 / den
            lse = m + jnp.log(den)
            for g in range(4):
                h = 4 * kh + g
                o_ref[:, h * HEAD_DIM:(h + 1) * HEAD_DIM] = o4[g * QB:(g + 1) * QB].astype(BF16)
                lse_ref[:, h:h + 1] = lse[g * QB:(g + 1) * QB]

    return _hosted_call(
        body, ride, name, (cfg.b, nlb + (ncb if ctx_queries else 0)), [sink] + [qkv] * 9,
        [pl.BlockSpec(memory_space=pltpu.SMEM), qspec, *kvspecs],
        [jax.ShapeDtypeStruct((cfg.r, D), BF16), jax.ShapeDtypeStruct((cfg.r, N_HEADS), F32)],
        [pl.BlockSpec((QB, ATTN_W), lambda s, qb: (qblk(s, qb), 0)),
         pl.BlockSpec((QB, N_HEADS), lambda s, qb: (qblk(s, qb), 0))], [], _params(2))


def _att_bwd(name, qkv, mix, dmix, lse, sink, cos_t, sin_t, ctx_queries, cfg, ride=None):
    nlb, ncb = cfg.seq // QB, cfg.ctx // QB
    nqb = nlb + (ncb if ctx_queries else 0)
    qblk, qspec, kvspecs = _att_specs(cfg)

    def body(sink_ref, q_ref, kx_ref, vx_ref, kp_ref, kc_ref, kn_ref, vp_ref, vc_ref, vn_ref, o_ref, do_ref,
             lse_ref, cosq_ref, sinq_ref, cosk_ref, sink_tab_ref, dq_ref, dkvl_ref, dkvc_ref, dsink_ref,
             accl, accc, dqs):
        s_id, qb = pl.program_id(0), pl.program_id(1)

        @pl.when(qb == 0)
        def _():
            accl[...] = jnp.zeros_like(accl)
            accc[...] = jnp.zeros_like(accc)

        @pl.when(jnp.logical_and(s_id == 0, qb == 0))
        def _():
            dsink_ref[...] = jnp.zeros_like(dsink_ref)

        starts = [pl.multiple_of(jnp.clip(qb + off, 0, nlb - 1) * QB, QB) for off in (-1, 0, 1)]
        for kh in range(2):
            q4, snk, ks, ss = _att_scores(qb, nlb, sink_ref, q_ref, (kx_ref, kp_ref, kc_ref, kn_ref), kh)
            lanes = slice(kh * HEAD_DIM, (kh + 1) * HEAD_DIM)
            vs = [r[:, lanes] for r in (vx_ref, vp_ref, vc_ref, vn_ref)]
            heads = [slice((4 * kh + g) * HEAD_DIM, (4 * kh + g + 1) * HEAD_DIM) for g in range(4)]
            do4 = jnp.concatenate([do_ref[:, hs] for hs in heads], axis=0)
            o4 = jnp.concatenate([o_ref[:, hs] for hs in heads], axis=0).astype(F32)
            lse4 = jnp.concatenate([lse_ref[:, 4 * kh + g:4 * kh + g + 1] for g in range(4)], axis=0)
            delta = jnp.sum(do4 * o4, axis=-1, keepdims=True)
            dq4 = jnp.zeros((4 * QB, HEAD_DIM), F32)
            dks, dvs = [], []
            for s_, k_, v_ in zip(ss, ks, vs):
                p = jnp.exp(s_ - lse4)
                ds = p * (_dot(do4, v_, "nt") - delta)
                dq4 = dq4 + _dot(ds, k_)
                dks.append(_dot(ds, q4, "tn"))
                dvs.append(_dot(p, do4, "tn"))
            accc[:, lanes] += dks[0]
            accc[:, 128 + kh * HEAD_DIM:128 + (kh + 1) * HEAD_DIM] += dvs[0]
            for st, dk_, dv_ in zip(starts, dks[1:], dvs[1:]):
                accl[pl.ds(st, QB), lanes] += dk_
                accl[pl.ds(st, QB), 128 + kh * HEAD_DIM:128 + (kh + 1) * HEAD_DIM] += dv_
            dsk = -jnp.exp(snk - lse4) * delta
            for g in range(4):
                h = 4 * kh + g
                dsink_ref[h:h + 1, :] += jnp.broadcast_to(_colsum(dsk[g * QB:(g + 1) * QB]), (1, 128))
                dqs[:, heads[g]] = dq4[g * QB:(g + 1) * QB]
        cos, sin = cosq_ref[...], sinq_ref[...]
        dq_ref[...] = jnp.concatenate(
            [_rope(dqs[:, 128 * t:128 * (t + 1)], cos, sin, -1.0) * (HEAD_DIM ** -0.5) for t in range(4)],
            axis=1).astype(BF16)

        @pl.when(qb == nqb - 1)
        def _():
            dk = _rope(accl[:, 0:128], cosk_ref[...], sink_tab_ref[...], -1.0)
            dkvl_ref[...] = jnp.concatenate([dk, accl[:, 128:256]], axis=1).astype(BF16)
            dkvc_ref[...] = accc[...].astype(BF16)

    rowq = lambda w: pl.BlockSpec((QB, w), lambda s, qb: (qblk(s, qb), 0))
    tabq = pl.BlockSpec((QB, 128), lambda s, qb: (jnp.where(qb < nlb, qb, cfg.seq // QB), 0))
    tabk = pl.BlockSpec((cfg.seq, 128), lambda s, qb: (0, 0))
    return _hosted_call(
        body, ride, name, (cfg.b, nqb), [sink] + [qkv] * 9 + [mix, dmix, lse, cos_t, sin_t, cos_t, sin_t],
        [pl.BlockSpec(memory_space=pltpu.SMEM), qspec, *kvspecs, rowq(ATTN_W), rowq(ATTN_W), rowq(N_HEADS),
         tabq, tabq, tabk, tabk],
        [jax.ShapeDtypeStruct((cfg.r, IN_W), BF16), jax.ShapeDtypeStruct((cfg.nl, 256), BF16),
         jax.ShapeDtypeStruct((cfg.nc, 256), BF16), jax.ShapeDtypeStruct((N_HEADS, 128), F32)],
        [rowq(ATTN_W), pl.BlockSpec((cfg.seq, 256), lambda s, qb: (s, 0)),
         pl.BlockSpec((cfg.ctx, 256), lambda s, qb: (s, 0)), pl.BlockSpec((N_HEADS, 128), lambda s, qb: (0, 0))],
        [pltpu.VMEM((cfg.seq, 256), F32), pltpu.VMEM((cfg.ctx, 256), F32), pltpu.VMEM((QB, ATTN_W), F32)], _params(2))


def _pool_geometry(n, c):
    lane = lax.broadcasted_iota(jnp.int32, (1, POOL_W), 1) // HEAD_DIM
    wl = jnp.where(lane == 0, 1, jnp.where(lane == 1, 2, jnp.where(lane == 2, 4, 8)))
    wr = wl - 1
    t = c * CHUNK + lax.broadcasted_iota(jnp.int32, (CHUNK, POOL_W), 0)
    cnt = (jnp.minimum(t + wr, n - 1) - jnp.maximum(t - wl, 0) + 1).astype(F32)
    return wl, wr, cnt


def _build_phases(src, ph, c):
    for s in range(1, 8):
        ph[s - 1] = src[c * CHUNK + s:c * CHUNK + s + CHUNK + 24, :]


def _window(src, ph, c, off):
    a, s = divmod(off, 8)
    if s == 0:
        return src[c * CHUNK + 8 * a:c * CHUNK + 8 * a + CHUNK, :]
    return ph[s - 1, 8 * a:8 * a + CHUNK, :]


def _conv_chunk(hp, ph, dw_ref, dwb_ref, c):
    _build_phases(hp, ph, c)
    acc = jnp.zeros((CHUNK, CONV_W), F32) + dwb_ref[...]
    for j in range(CONV_K):
        acc = acc + dw_ref[j:j + 1, :] * _window(hp, ph, c, j + 1)
    return acc


def _fill_glu(cp_ref, hp, n):
    hp[0:PAD, :] = jnp.zeros((PAD, CONV_W), F32)
    hp[PAD + n:2 * PAD + n, :] = jnp.zeros((PAD, CONV_W), F32)
    for c in range(n // CHUNK):
        rows = slice(c * CHUNK, (c + 1) * CHUNK)
        a = cp_ref[rows, 0:CONV_W].astype(F32)
        g = cp_ref[rows, CONV_W:2 * CONV_W].astype(F32)
        hp[PAD + c * CHUNK:PAD + (c + 1) * CHUNK, :] = a * _sigmoid(g)


def _fill_pool(cp_ref, pp, n):
    pp[0:PAD, :] = jnp.zeros((PAD, POOL_W), F32)
    pp[PAD + n:2 * PAD + n, :] = jnp.zeros((PAD, POOL_W), F32)
    for c in range(n // CHUNK):
        pp[PAD + c * CHUNK:PAD + (c + 1) * CHUNK, :] = cp_ref[c * CHUNK:(c + 1) * CHUNK, 2 * CONV_W:768].astype(F32)


def _pool_chunk(pp, ph, n, c):
    wl, wr, cnt = _pool_geometry(n, c)
    _build_phases(pp, ph, c)
    acc = jnp.zeros((CHUNK, POOL_W), F32)
    for o in range(-8, 8):
        acc = acc + jnp.where(jnp.logical_and(o >= -wl, o <= wr), _window(pp, ph, c, PAD + o), 0.0)
    return acc / cnt - pp[PAD + c * CHUNK:PAD + (c + 1) * CHUNK, :], cnt


def _seq_specs(n, blk_off, width, col=0):
    return pl.BlockSpec((n, width), lambda s: (blk_off + s, col))


def _full(shape):
    return pl.BlockSpec(shape, lambda s: (0,) * len(shape))


_PHASES = pltpu.VMEM((7, CHUNK + 24, CONV_W), F32)


def _convpool_fwd(name, cpin, mix, yconv, prm, n, blk_off, cfg):
    dw, dwb, lng, lnb, wbd, ps = prm
    n_alias = 1 if yconv is None else 2

    def body(*refs):
        cp_ref, dw_ref, dwb_ref, lng_ref, lnb_ref, wbd_ref, ps_ref = refs[:7]
        out_ref, y_ref, hp, pp, ph = refs[7 + n_alias:]
        _fill_glu(cp_ref, hp, n)
        _fill_pool(cp_ref, pp, n)
        for c in range(n // CHUNK):
            rows = slice(c * CHUNK, (c + 1) * CHUNK)
            y = _conv_chunk(hp, ph, dw_ref, dwb_ref, c)
            y_ref[rows, :] = y
            d = y - jnp.mean(y, axis=-1, keepdims=True)
            hn = d * lax.rsqrt(jnp.mean(d * d, axis=-1, keepdims=True) + EPS) * lng_ref[...] + lnb_ref[...]
            out_ref[rows, 0:CONV_W] = (hn * _sigmoid(hn)).astype(BF16)
            yp, _ = _pool_chunk(pp, ph, n, c)
            out_ref[rows, CONV_W:2 * CONV_W] = (_dot(yp, wbd_ref[...]) * ps_ref[...]).astype(BF16)

    through = [mix] if yconv is None else [mix, yconv]
    return pl.pallas_call(
        body, name=name, grid=(cfg.b,),
        in_specs=[_seq_specs(n, blk_off, 768), _full((32, CONV_W)), _full((1, CONV_W)), _full((1, CONV_W)),
                  _full((1, CONV_W)), _full((POOL_W, POOL_W)), _full((1, POOL_W))] + [_ANY] * n_alias,
        out_specs=[_seq_specs(n, blk_off, 512, 1), _seq_specs(n, blk_off, CONV_W)],
        out_shape=[jax.ShapeDtypeStruct((cfg.r, D), BF16), jax.ShapeDtypeStruct((cfg.r, CONV_W), F32)],
        scratch_shapes=[pltpu.VMEM((n + 2 * PAD, CONV_W), F32), pltpu.VMEM((n + 2 * PAD, POOL_W), F32), _PHASES],
        input_output_aliases={7 + i: i for i in range(n_alias)},
        compiler_params=_params(1))(cpin, dw, dwb, lng, lnb, wbd, ps, *through)


_SMALL_SHAPES = [(32, CONV_W), (1, CONV_W), (1, CONV_W), (1, CONV_W), (POOL_W, POOL_W), (1, POOL_W)]


def _convpool_bwd(name, cpin, yconv, dmix, prm, acc_in, n, blk_off, cfg):
    dw, dwb, lng, lnb, wbd, ps = prm
    nch = n // CHUNK

    def body(cp_ref, y_ref, dm_ref, dw_ref, dwb_ref, lng_ref, lnb_ref, wbd_ref, ps_ref, dcp_in,
             a_dw, a_dwb, a_lng, a_lnb, a_wbd, a_ps,
             dcp_ref, o_dw, o_dwb, o_lng, o_lnb, o_wbd, o_ps, hp, dyp, pp, wp, dyv, dwacc, ph):
        s = pl.program_id(0)

        @pl.when(s == 0)
        def _():
            for o_, a_ in ((o_dw, a_dw), (o_dwb, a_dwb), (o_lng, a_lng), (o_lnb, a_lnb), (o_wbd, a_wbd), (o_ps, a_ps)):
                o_[...] = a_[...]
            dwacc[...] = jnp.zeros_like(dwacc)

        _fill_glu(cp_ref, hp, n)
        _fill_pool(cp_ref, pp, n)
        for ref in (dyp, wp):
            ref[0:PAD, :] = jnp.zeros((PAD, CONV_W), F32)
            ref[PAD + n:2 * PAD + n, :] = jnp.zeros((PAD, CONV_W), F32)
        for c in range(nch):
            rows = slice(c * CHUNK, (c + 1) * CHUNK)
            y = y_ref[rows, :]
            d = y - jnp.mean(y, axis=-1, keepdims=True)
            rstd = lax.rsqrt(jnp.mean(d * d, axis=-1, keepdims=True) + EPS)
            xh = d * rstd
            hn = xh * lng_ref[...] + lnb_ref[...]
            sg = _sigmoid(hn)
            dhn = dm_ref[rows, 0:CONV_W] * (sg * (1.0 + hn * (1.0 - sg)))
            o_lnb[...] += _colsum(dhn)
            o_lng[...] += _colsum(dhn * xh)
            dxh = dhn * lng_ref[...]
            dy = rstd * (dxh - jnp.mean(dxh, axis=-1, keepdims=True) - xh * jnp.mean(dxh * xh, axis=-1, keepdims=True))
            o_dwb[...] += _colsum(dy)
            dyp[PAD + c * CHUNK:PAD + (c + 1) * CHUNK, :] = dy
            _build_phases(hp, ph, c)
            for j in range(CONV_K):
                prod = dy * _window(hp, ph, c, j + 1)
                dwacc[8 * j:8 * j + 8, :] += jnp.sum(prod.reshape(CHUNK // 8, 8, CONV_W), axis=0)
            yp, cnt = _pool_chunk(pp, ph, n, c)
            dz = dm_ref[rows, CONV_W:2 * CONV_W]
            o_ps[...] += _colsum(dz * _dot(yp, wbd_ref[...]))
            dzs = dz * ps_ref[...]
            o_wbd[...] += _dot(yp, dzs, "tn")
            dv = _dot(dzs, wbd_ref[...], "nt")
            dyv[rows, :] = dv
            wp[PAD + c * CHUNK:PAD + (c + 1) * CHUNK, :] = dv / cnt
        for c in range(nch):
            rows = slice(c * CHUNK, (c + 1) * CHUNK)
            _build_phases(dyp, ph, c)
            dh = jnp.zeros((CHUNK, CONV_W), F32)
            for j in range(CONV_K):
                dh = dh + dw_ref[j:j + 1, :] * _window(dyp, ph, c, 31 - j)
            a = cp_ref[rows, 0:CONV_W].astype(F32)
            sg = _sigmoid(cp_ref[rows, CONV_W:2 * CONV_W].astype(F32))
            dcp_ref[rows, 0:CONV_W] = (dh * sg).astype(BF16)
            dcp_ref[rows, CONV_W:2 * CONV_W] = (dh * a * sg * (1.0 - sg)).astype(BF16)
            wl, wr, _ = _pool_geometry(n, c)
            _build_phases(wp, ph, c)
            dp = -dyv[rows, :]
            for o in range(-8, 8):
                dp = dp + jnp.where(jnp.logical_and(o >= -wl, o <= wr), _window(wp, ph, c, PAD - o), 0.0)
            dcp_ref[rows, 2 * CONV_W:768] = dp.astype(BF16)

        @pl.when(s == cfg.b - 1)
        def _():
            for j in range(CONV_K):
                o_dw[j:j + 1, :] += _colsum(dwacc[8 * j:8 * j + 8, :])

    small_specs = [_full(sh) for sh in _SMALL_SHAPES]
    return pl.pallas_call(
        body, name=name, grid=(cfg.b,),
        in_specs=[_seq_specs(n, blk_off, 768), _seq_specs(n, blk_off, CONV_W), _seq_specs(n, blk_off, 512, 1),
                  *small_specs, _ANY, *small_specs],
        out_specs=[_seq_specs(n, blk_off, 768, 1), *small_specs],
        out_shape=[jax.ShapeDtypeStruct((cfg.r, IN_W), BF16)] + [jax.ShapeDtypeStruct(sh, F32) for sh in _SMALL_SHAPES],
        scratch_shapes=[pltpu.VMEM((n + 2 * PAD, CONV_W), F32), pltpu.VMEM((n + 2 * PAD, CONV_W), F32),
                        pltpu.VMEM((n + 2 * PAD, POOL_W), F32), pltpu.VMEM((n + 2 * PAD, POOL_W), F32),
                        pltpu.VMEM((n, POOL_W), F32), pltpu.VMEM((8 * 32, CONV_W), F32), _PHASES],
        input_output_aliases={9: 0}, compiler_params=_params(1))(cpin, yconv, dmix, dw, dwb, lng, lnb, wbd, ps, *acc_in)


def _place_kv(name, du, dkvl, dkvc, with_ctx, cfg):
    nlb = cfg.nl // TM

    def body(l_ref, c_ref, du_in, o_ref):
        i = pl.program_id(0)
        o_ref[...] = jnp.where(i < nlb, l_ref[...], c_ref[...])

    return pl.pallas_call(
        body, name=name, grid=(cfg.r // TM if with_ctx else nlb,),
        in_specs=[pl.BlockSpec((TM, 256), lambda i: (jnp.minimum(i, nlb - 1), 0)),
                  pl.BlockSpec((TM, 256), lambda i: (jnp.maximum(i - nlb, 0), 0)), _ANY],
        out_specs=pl.BlockSpec((TM, 256), lambda i: (i, 2)), out_shape=jax.ShapeDtypeStruct((cfg.r, IN_W), BF16),
        input_output_aliases={2: 0}, compiler_params=_params(1))(dkvl, dkvc, du)


def _place_ctx_kv_only(name, du, dkvc, cfg):
    nlb = cfg.nl // TM

    def body(c_ref, du_in, o_ref):
        o_ref[...] = jnp.zeros_like(o_ref)
        o_ref[:, ATTN_W:ATTN_W + 256] = c_ref[...]

    return pl.pallas_call(
        body, name=name, grid=(cfg.nc // TM,), in_specs=[pl.BlockSpec((TM, 256), lambda i: (i, 0)), _ANY],
        out_specs=pl.BlockSpec((TM, IN_W), lambda i: (nlb + i, 0)), out_shape=jax.ShapeDtypeStruct((cfg.r, IN_W), BF16),
        input_output_aliases={1: 0}, compiler_params=_params(1))(dkvc, du)


def _ffn_in(name, h, w_ffn_in, layer, nblk, cfg, ride=None):
    def body(h_ref, wg_ref, wu_ref, gu_ref, act_ref):
        hv = h_ref[...]
        g = _dot(hv, wg_ref[...])
        u = _dot(hv, wu_ref[...])
        gu_ref[0] = g.astype(BF16)
        gu_ref[1] = u.astype(BF16)
        act_ref[...] = (g * _sigmoid(g) * u).astype(BF16)

    wspec = lambda base: pl.BlockSpec((None, None, D, HALF_FF), lambda j, i: (layer, base + j, 0, 0))
    return _hosted_call(
        body, ride, name, (2, nblk), [h, w_ffn_in, w_ffn_in],
        [pl.BlockSpec((TM, D), lambda j, i: (i, 0)), wspec(0), wspec(2)],
        [jax.ShapeDtypeStruct((2, cfg.r, D_FF), BF16), jax.ShapeDtypeStruct((cfg.r, D_FF), BF16)],
        [pl.BlockSpec((2, TM, HALF_FF), lambda j, i: (0, i, j)), pl.BlockSpec((TM, HALF_FF), lambda j, i: (i, j))],
        [], _params(2))


def _row_block(rows, cols, max_bytes=1 << 20):
    best = 16
    for t in range(16, rows + 1, 16):
        if rows % t == 0 and t * cols * 4 <= max_bytes:
            best = t
    assert rows % best == 0
    return best


def _pair_add(name, own32, recv, c_idx):
    _, _, s0, s1 = own32.shape
    tr = _row_block(s0, s1)

    def body(c_ref, a_ref, b_ref, o_ref):
        o_ref[...] = (a_ref[...].astype(F32) + b_ref[...].astype(F32)).astype(BF16)

    grid_spec = pltpu.PrefetchScalarGridSpec(
        num_scalar_prefetch=1, grid=(N_SHARD * s0 // tr,),
        in_specs=[pl.BlockSpec((None, tr, s1), lambda i, c: (c[0], i, 0)), pl.BlockSpec((tr, s1), lambda i, c: (i, 0))],
        out_specs=pl.BlockSpec((tr, s1), lambda i, c: (i, 0)))
    out = pl.pallas_call(body, name=name, grid_spec=grid_spec, out_shape=jax.ShapeDtypeStruct((N_SHARD * s0, s1), BF16),
                         compiler_params=_params(1))(c_idx, own32.reshape(2, N_SHARD * s0, s1), recv.reshape(N_SHARD * s0, s1))
    return out.reshape(N_SHARD, s0, s1)


def _shard_sum(name, pair_sum, recv, jc_idx):
    _, s0, s1 = pair_sum.shape
    tr = _row_block(s0, s1)

    def body(jc_ref, a_ref, b_ref, o_ref):
        o_ref[...] = ((a_ref[...].astype(F32) + b_ref[0].astype(F32)) + b_ref[1].astype(F32)) + b_ref[2].astype(F32)

    grid_spec = pltpu.PrefetchScalarGridSpec(
        num_scalar_prefetch=1, grid=(s0 // tr,),
        in_specs=[pl.BlockSpec((None, tr, s1), lambda i, jc: (jc[0], i, 0)), pl.BlockSpec((3, tr, s1), lambda i, jc: (0, i, 0))],
        out_specs=pl.BlockSpec((None, tr, s1), lambda i, jc: (jc[1], i, 0)))
    return pl.pallas_call(body, name=name, grid_spec=grid_spec, out_shape=jax.ShapeDtypeStruct((2, s0, s1), F32),
                          compiler_params=_params(1))(jc_idx, pair_sum, recv)


def _adamw_math(w, g, m, v):
    m = ADAM_B1 * m + (1.0 - ADAM_B1) * g
    v = ADAM_B2 * v + (1.0 - ADAM_B2) * (g * g)
    m_hat = m / (1.0 - ADAM_B1 ** ADAM_STEP)
    v_hat = v / (1.0 - ADAM_B2 ** ADAM_STEP)
    delta = -ADAM_LR * (m_hat / (jnp.sqrt(v_hat) + ADAM_EPS) + ADAM_WD * w)
    return delta, m, v


def _adamw(name, w, g, m, v):
    rows, cols = w.shape
    tr = rows if rows % 16 else _row_block(rows, cols, 1 << 19)

    def body(w_ref, g_ref, m_ref, v_ref, d_ref, mo_ref, vo_ref):
        d, mn, vn = _adamw_math(w_ref[...], g_ref[...], m_ref[...], v_ref[...])
        d_ref[...] = d
        mo_ref[...] = mn
        vo_ref[...] = vn

    spec = pl.BlockSpec((tr, cols), lambda i: (i, 0))
    shape = jax.ShapeDtypeStruct((rows, cols), F32)
    return pl.pallas_call(body, name=name, grid=(rows // tr,), in_specs=[spec] * 4, out_specs=[spec] * 3,
                          out_shape=[shape] * 3, compiler_params=_params(1))(w, g, m, v)


def _adamw_layers(name, w, g_layers, m, v):
    rows, cols = w.shape
    s0 = rows // 2
    tr = _row_block(s0, cols, 1 << 19)
    nb = s0 // tr

    def body(w_ref, g0_ref, g1_ref, m_ref, v_ref, g_ref, d_ref, mo_ref, vo_ref):
        g = jnp.where(pl.program_id(0) < nb, g0_ref[...], g1_ref[...])
        d, mn, vn = _adamw_math(w_ref[...], g, m_ref[...], v_ref[...])
        g_ref[...] = g
        d_ref[...] = d
        mo_ref[...] = mn
        vo_ref[...] = vn

    spec = pl.BlockSpec((tr, cols), lambda i: (i, 0))
    shape = jax.ShapeDtypeStruct((rows, cols), F32)
    return pl.pallas_call(
        body, name=name, grid=(2 * nb,),
        in_specs=[spec, pl.BlockSpec((tr, cols), lambda i: (jnp.minimum(i, nb - 1), 0)),
                  pl.BlockSpec((tr, cols), lambda i: (jnp.maximum(i - nb, 0), 0)), spec, spec],
        out_specs=[spec] * 4, out_shape=[shape] * 4, compiler_params=_params(1))(w, g_layers[0], g_layers[1], m, v)


def _position():
    return lax.axis_index("x"), lax.axis_index("y"), lax.axis_index("c")


def _other_chips(x, y):
    return [(1 - x, y), (x, 1 - y), (1 - x, 1 - y)]


def _run_ride(name, ride):
    n_in, n_out = len(ride.ins), len(ride.out_shape)

    def body(*refs):
        parts = (refs[:n_in], refs[n_in:n_in + n_out], refs[n_in + n_out:])
        ride.start(*parts)
        ride.finish(*parts)

    return pl.pallas_call(
        body, name=name, in_specs=[_ANY] * n_in, out_specs=[_ANY] * n_out, out_shape=ride.out_shape,
        scratch_shapes=ride.scratch, compiler_params=pltpu.CompilerParams(vmem_limit_bytes=VMEM_LIMIT_V7X))(*ride.ins)


def _gather_ride(shards):
    n = len(shards)

    def copies(ins, outs, scr):
        ssem, rsem = scr[n], scr[n + 1]
        x, y, c = _position()
        me, sibling = 2 * x + y, (x, y, 1 - c)

        def remote(src, dst, i, dev):
            return pltpu.make_async_remote_copy(src, dst, ssem.at[i], rsem.at[i], device_id=dev, device_id_type=_MESH)

        fetch_out, fetch_in, pass_out, pass_in = [], [], [], []
        for a, (src, dst) in enumerate(zip(ins, outs)):
            for k, (px, py) in enumerate(_other_chips(x, y)):
                j, i1, i2 = 2 * px + py, 3 * a + k, 3 * n + 3 * a + k
                fetch_out.append(remote(src.at[c], dst.at[me, c], i1, (px, py, c)))
                fetch_in.append(remote(src.at[c], dst.at[j, c], i1, (px, py, c)))
                pass_out.append(remote(dst.at[j, c], dst.at[j, c], i2, sibling))
                pass_in.append(remote(dst.at[j, 1 - c], dst.at[j, 1 - c], i2, sibling))
        return me, fetch_out, fetch_in, pass_out, pass_in

    def start(ins, outs, scr):
        bufs, lsem = scr[:n], scr[n + 2]
        me, fetch_out, _, _, _ = copies(ins, outs, scr)
        for cp in fetch_out:
            cp.start()
        loads = []
        for a, (src, buf) in enumerate(zip(ins, bufs)):
            ld = pltpu.make_async_copy(src, buf, lsem.at[2 * a])
            ld.start()
            loads.append(ld)
        for a, (ld, buf, dst) in enumerate(zip(loads, bufs, outs)):
            ld.wait()
            st = pltpu.make_async_copy(buf, dst.at[me], lsem.at[2 * a + 1])
            st.start()
            st.wait()

    def finish(ins, outs, scr):
        _, fetch_out, fetch_in, pass_out, pass_in = copies(ins, outs, scr)
        for arrived, onward in zip(fetch_in, pass_out):
            arrived.wait_recv()
            onward.start()
        for cp in pass_in:
            cp.wait_recv()
        for cp in fetch_out + pass_out:
            cp.wait_send()

    return _Ride(list(shards), [jax.ShapeDtypeStruct((N_SHARD,) + s.shape, s.dtype) for s in shards],
                 [pltpu.VMEM(s.shape, s.dtype) for s in shards]
                 + [pltpu.SemaphoreType.DMA((6 * n,)), pltpu.SemaphoreType.DMA((6 * n,)), pltpu.SemaphoreType.DMA((2 * n,))],
                 start, finish)


def _comm(name, ins, out_shape, n_remote, plan):
    n_in, n_out = len(ins), len(out_shape)

    def body(*refs):
        plan(refs[:n_in], refs[n_in:n_in + n_out], *refs[n_in + n_out:])

    return pl.pallas_call(
        body, name=name, in_specs=[_ANY] * n_in, out_specs=[_ANY] * n_out, out_shape=out_shape,
        scratch_shapes=[pltpu.SemaphoreType.DMA((n_remote,)), pltpu.SemaphoreType.DMA((n_remote,))])(*ins)


def _send_other_half(name, grads_bf):
    n = len(grads_bf)

    def plan(ins, outs, ssem, rsem):
        x, y, c = _position()
        started = []
        for a, (src, dst) in enumerate(zip(ins, outs)):
            cp = pltpu.make_async_remote_copy(src.at[1 - c], dst, ssem.at[a], rsem.at[a], device_id=(x, y, 1 - c),
                                              device_id_type=_MESH)
            cp.start()
            started.append(cp)
        for cp in started:
            cp.wait_recv()
        for cp in started:
            cp.wait_send()

    shapes = [jax.ShapeDtypeStruct(s.shape[1:], s.dtype) for s in grads_bf]
    return _comm(name, grads_bf, shapes, n, plan)


def _exchange_ride(pair_sums):
    n = len(pair_sums)

    def copies(ins, outs, scr):
        ssem, rsem = scr
        x, y, c = _position()
        return [pltpu.make_async_remote_copy(src.at[2 * px + py], dst.at[k], ssem.at[3 * a + k], rsem.at[3 * a + k],
                                             device_id=(px, py, c), device_id_type=_MESH)
                for a, (src, dst) in enumerate(zip(ins, outs)) for k, (px, py) in enumerate(_other_chips(x, y))]

    def start(ins, outs, scr):
        for cp in copies(ins, outs, scr):
            cp.start()

    def finish(ins, outs, scr):
        for cp in copies(ins, outs, scr):
            cp.wait_recv()
        for cp in copies(ins, outs, scr):
            cp.wait_send()

    return _Ride(list(pair_sums), [jax.ShapeDtypeStruct((3,) + s.shape[1:], s.dtype) for s in pair_sums],
                 [pltpu.SemaphoreType.DMA((3 * n,)), pltpu.SemaphoreType.DMA((3 * n,))], start, finish)


def _swap_reduced(name, grads):
    n = len(grads)

    def body(*refs):
        ins, outs, ssem, rsem = refs[:n], refs[n:2 * n], refs[2 * n], refs[2 * n + 1]
        x, y, c = _position()
        sent = []
        for a, (src, dst) in enumerate(zip(ins, outs)):
            cp = pltpu.make_async_remote_copy(src.at[c], dst.at[c], ssem.at[a], rsem.at[a], device_id=(x, y, 1 - c),
                                              device_id_type=_MESH)
            cp.start()
            sent.append(cp)
        for a, (src, dst) in enumerate(zip(ins, outs)):
            pltpu.make_async_remote_copy(src.at[1 - c], dst.at[1 - c], ssem.at[a], rsem.at[a], device_id=(x, y, 1 - c),
                                         device_id_type=_MESH).wait_recv()
        for cp in sent:
            cp.wait_send()

    return pl.pallas_call(
        body, name=name, in_specs=[_ANY] * n, out_specs=[_ANY] * n,
        out_shape=[jax.ShapeDtypeStruct(g.shape, g.dtype) for g in grads],
        scratch_shapes=[pltpu.SemaphoreType.DMA((n,)), pltpu.SemaphoreType.DMA((n,))],
        input_output_aliases={a: a for a in range(n)})(*grads)


_FLIPS = [(dx, dy, dc) for dx in (0, 1) for dy in (0, 1) for dc in (0, 1) if dx + dy + dc]
_VMEM = pl.BlockSpec(memory_space=pltpu.VMEM)


def _to_all(src_of, dst, ssem, rsem):
    x, y, c = _position()
    me = 4 * x + 2 * y + c
    peers = [((x + dx) % 2, (y + dy) % 2, (c + dc) % 2) for dx, dy, dc in _FLIPS]
    sent = []
    for k, (px, py, pc) in enumerate(peers):
        cp = pltpu.make_async_remote_copy(src_of(2 * px + py), dst.at[me], ssem.at[k], rsem.at[k],
                                          device_id=(px, py, pc), device_id_type=_MESH)
        cp.start()
        sent.append(cp)
    for k, (px, py, pc) in enumerate(peers):
        pltpu.make_async_remote_copy(src_of(2 * px + py), dst.at[4 * px + 2 * py + pc], ssem.at[k], rsem.at[k],
                                     device_id=(px, py, pc), device_id_type=_MESH).wait_recv()
    for cp in sent:
        cp.wait_send()
    return me, 2 * x + y


def _share_small(name, block, total):
    shape = block.shape

    def body(in_ref, out_ref, *scratch):
        buf, ssem, rsem = (out_ref,) + scratch if not total else scratch
        me, _ = _to_all(lambda chip: in_ref, buf, ssem, rsem)
        buf[me] = in_ref[...]
        if total:
            acc = buf[0]
            for d in range(1, 8):
                acc = acc + buf[d]
            out_ref[...] = acc

    sems = [pltpu.SemaphoreType.DMA((7,)), pltpu.SemaphoreType.DMA((7,))]
    return pl.pallas_call(
        body, name=name, in_specs=[_VMEM], out_specs=_VMEM,
        out_shape=jax.ShapeDtypeStruct(shape if total else (8,) + shape, F32),
        scratch_shapes=([pltpu.VMEM((8,) + shape, F32)] if total else []) + sems,
        compiler_params=pltpu.CompilerParams(vmem_limit_bytes=VMEM_LIMIT_V7X))(block)


def _mod_rows_exchange(mv):
    def body(mv_ref, out_ref, ssem, rsem):
        x, y, c = _position()
        me = 2 * x + y
        out_ref[me] = mv_ref[4 * x + 2 * y + c]
        sent = []
        for k, (px, py) in enumerate(_other_chips(x, y)):
            cp = pltpu.make_async_remote_copy(mv_ref.at[4 * px + 2 * py + c], out_ref.at[me], ssem.at[k], rsem.at[k],
                                              device_id=(px, py, c), device_id_type=_MESH)
            cp.start()
            sent.append(cp)
        for k, (px, py) in enumerate(_other_chips(x, y)):
            pltpu.make_async_remote_copy(mv_ref.at[0], out_ref.at[2 * px + py], ssem.at[k], rsem.at[k],
                                         device_id=(px, py, c), device_id_type=_MESH).wait_recv()
        for cp in sent:
            cp.wait_send()

    return pl.pallas_call(
        body, name="mod_rows_exchange", in_specs=[_VMEM], out_specs=_VMEM,
        out_shape=jax.ShapeDtypeStruct((N_SHARD,) + mv.shape[1:], F32),
        scratch_shapes=[pltpu.SemaphoreType.DMA((3,)), pltpu.SemaphoreType.DMA((3,))],
        compiler_params=pltpu.CompilerParams(vmem_limit_bytes=VMEM_LIMIT_V7X))(mv)


def _mod_grad_exchange(dmj, dm_rows):
    def body(dmj_ref, rows_ref, out_ref, bias_ref, ssem, rsem):
        me, chip = _to_all(lambda j: dmj_ref.at[j], out_ref, ssem, rsem)
        out_ref[me] = dmj_ref[chip]
        for l in range(2):
            bias_ref[l] = _colsum(rows_ref[l])

    return pl.pallas_call(
        body, name="mod_grad_exchange", in_specs=[_VMEM, _VMEM], out_specs=[_VMEM, _VMEM],
        out_shape=[jax.ShapeDtypeStruct((8,) + dmj.shape[1:], F32), jax.ShapeDtypeStruct((2, 1, dm_rows.shape[-1]), F32)],
        scratch_shapes=[pltpu.SemaphoreType.DMA((7,)), pltpu.SemaphoreType.DMA((7,))],
        compiler_params=pltpu.CompilerParams(vmem_limit_bytes=VMEM_LIMIT_V7X))(dmj, dm_rows)


def _pack(arrays):
    flat = jnp.concatenate([a.reshape(-1).astype(F32) for a in arrays])
    total = flat.shape[0]
    rows = -(-total // (8 * PACK_LANES)) * 8
    return jnp.pad(flat, (0, rows * PACK_LANES - total)).reshape(rows, PACK_LANES)


def _unpack(pack, shapes):
    flat, out, pos = pack.reshape(-1), [], 0
    for sh in shapes:
        size = int(np.prod(sh)) if len(sh) else 1
        out.append(flat[pos:pos + size].reshape(sh))
        pos += size
    return out


def _block_diag(pw):
    out = jnp.zeros((POOL_W, POOL_W), pw.dtype)
    for g in range(4):
        out = out.at[g * 64:(g + 1) * 64, g * 64:(g + 1) * 64].set(pw[g])
    return out


def _local_step(x, ctx, small, mod3s, loss_target, comm):
    cfg = _Cfg(x.shape[0], x.shape[1], ctx.shape[1])
    assert cfg.seq % TM == 0 and cfg.nc % TM == 0 and cfg.seq % cfg.ctx == 0 and cfg.ctx % CHUNK == 0
    nb_all, nb_lat = cfg.r // TM, cfg.nl // TM
    last = 1
    wf, big = comm.wf, comm.grads
    cos_t, sin_t = _rope_tables(cfg.seq)
    xs = jnp.concatenate([x.reshape(cfg.nl, D), ctx.reshape(cfg.nc, D)], axis=0)
    row = lambda w: pl.BlockSpec((TM, w), lambda i, j, k: (i, 0))
    mod3_spec = pl.BlockSpec((None, 6, D), lambda i, j, k: (cfg.mod_row(i), 0, 0))
    whole = lambda rows: pl.BlockSpec((None, rows, D), lambda i, j, k: (0, 0, 0))

    def conv_params(l):
        dw = jnp.pad(wf[l]["conv_dw"], ((0, 1), (0, 0)))
        return (dw, small["conv_dw_b"][l][None], small["conv_ln_g"][l][None], small["conv_ln_b"][l][None],
                _block_diag(small["pool_w"][l]).astype(BF16), small["pool_scale"][l][None])

    def residual_epi(ig):
        def epi(acc, ex, outs):
            x_ref, m_ref = ex
            outs[0][...] = x_ref[...] + m_ref[ig:ig + 1, :] * acc
            outs[1][...] = acc.astype(BF16)
        return epi

    def hosted(name, call):
        outs, got = call(comm.ride(name))
        comm.landed(name, got)
        return outs

    saved = []
    for l in range(2):
        nb = nb_lat if l == last else nb_all
        wl = wf[l]
        mod3 = mod3s[l]
        h1 = _norm_fwd(f"norm1_fwd{l}", xs, small["norm1_g"][l][None], mod3, 0, 1, nb_all, cfg)
        qkv, cpin = _in_proj(f"in_proj{l}", h1, wl["w_in"], 0, cos_t, sin_t, cfg)
        mix, lse = hosted(f"att_fwd{l}", lambda ride: _att_fwd(f"att_fwd{l}", qkv, small["attn_sink"][l], l != last, cfg, ride))
        prm = conv_params(l)
        mix, yconv = _convpool_fwd(f"convpool_fwd_lat{l}", cpin, mix, None, prm, cfg.seq, 0, cfg)
        if l != last:
            mix, yconv = _convpool_fwd(f"convpool_fwd_ctx{l}", cpin, mix, yconv, prm, cfg.ctx, cfg.nl // cfg.ctx, cfg)
        x1, y1 = _mm(f"out_proj{l}", "nn", (nb, 1, 1), mix, wl["w_out"].reshape(1, D, D), row(D), whole(D),
                     [jax.ShapeDtypeStruct((cfg.r, D), F32), jax.ShapeDtypeStruct((cfg.r, D), BF16)], [row(D), row(D)],
                     extras=[xs, mod3], extra_specs=[row(D), mod3_spec], epi=residual_epi(2))
        h2 = _norm_fwd(f"norm2_fwd{l}", x1, small["norm2_g"][l][None], mod3, 3, 4, nb, cfg)
        gu, act = hosted(f"ffn_in{l}", lambda ride: _ffn_in(f"ffn_in{l}", h2, wl["w_ffn_in"], 0, nb, cfg, ride))
        x2, y2 = _mm(f"ffn_out{l}", "nn", (nb, 1, 1), act, wl["w_ffn_out"].reshape(1, D_FF, D), row(D_FF), whole(D_FF),
                     [jax.ShapeDtypeStruct((cfg.r, D), F32), jax.ShapeDtypeStruct((cfg.r, D), BF16)], [row(D), row(D)],
                     extras=[x1, mod3], extra_specs=[row(D), mod3_spec], epi=residual_epi(5))
        saved.append(dict(mod3=mod3, x0=xs, h1=h1, qkv=qkv, cpin=cpin, mix=mix, yconv=yconv, lse=lse, y1=y1, x1=x1,
                          h2=h2, gu=gu, act=act, y2=y2, prm=prm))
        xs = x2

    dx, loss, d_final_g = _loss_head(xs, loss_target.reshape(cfg.nl, D), small["final_g"][None], cfg)

    sg = {k: [None, None] for k in ("norm1_g", "norm2_g", "conv_dw", "conv_dw_b", "conv_ln_g", "conv_ln_b",
                                    "attn_sink", "pool_w", "pool_scale")}
    dms = [None, None]

    def swiglu_bwd_epi(acc, ex, outs):
        g = ex[0][0].astype(F32)
        u = ex[0][1].astype(F32)
        s = _sigmoid(g)
        outs[0][0] = (acc * u * (s * (1.0 + g * (1.0 - s)))).astype(BF16)
        outs[0][1] = (acc * (g * s)).astype(BF16)

    def halves_epi(acc, ex, outs):
        h = acc.shape[0] // 2
        outs[0][0] = acc[:h].astype(BF16)
        outs[0][1] = acc[h:].astype(BF16)

    def row_shards_epi(n):
        def epi(acc, ex, outs):
            s0 = acc.shape[0] // n
            h = s0 // 2
            for t in range(n):
                for half in range(2):
                    outs[0][half, t] = acc[t * s0 + half * h:t * s0 + (half + 1) * h].astype(BF16)
        return epi

    def col_shards_epi(acc, ex, outs):
        h = acc.shape[0] // 2
        for j in range(N_SHARD):
            for half in range(2):
                outs[0][half, j] = acc[half * h:(half + 1) * h, j * IN_SHARD:(j + 1) * IN_SHARD].astype(BF16)

    for l in (1, 0):
        sv = saved[l]
        mod3 = sv["mod3"]
        nb = nb_lat if l == last else nb_all
        tr = _dw_rows(nb * TM)
        steps = nb * TM // tr
        wl = wf[l]
        dy2, dg2 = _gate_bwd(f"gate2_bwd{l}", dx, sv["y2"], mod3, 5, nb, cfg)
        gu_spec = pl.BlockSpec((2, TM, HALF_FF), lambda i, j, k: (0, i, j))
        df = _mm(f"ffn_out_bwd{l}", "nt", (nb, 2, 1), dy2, wl["w_ffn_out"].reshape(1, D_FF, D), row(D),
                 pl.BlockSpec((None, HALF_FF, D), lambda i, j, k: (0, j, 0)),
                 [jax.ShapeDtypeStruct((2, cfg.r, D_FF), BF16)], [gu_spec], extras=[sv["gu"]], extra_specs=[gu_spec],
                 epi=swiglu_bwd_epi)[0]
        big[l]["w_ffn_out"] = _mm_dw(
            f"dw_ffn_out{l}", sv["act"], dy2, pl.BlockSpec((tr, HALF_FF), lambda i, j, k: (k, j)),
            pl.BlockSpec((tr, D), lambda i, j, k: (k, 0)), (2, N_SHARD, D_FF // 8, D),
            pl.BlockSpec((2, 2, D_FF // 8, D), lambda i, j, k: (0, j, 0, 0)), 2, (HALF_FF, D), steps, row_shards_epi(2))[0]
        ride = comm.ride(f"dw_ffn_in{l}")
        res = _mm_dw(f"dw_ffn_in{l}", sv["h2"], df, pl.BlockSpec((tr, D), lambda i, j, k: (k, 0)),
                     pl.BlockSpec((None, tr, HALF_FF), lambda i, j, k: (j // 2, k, j % 2)), (2, N_SHARD, D // 2, HALF_FF),
                     pl.BlockSpec((2, None, D // 2, HALF_FF), lambda i, j, k: (0, j, 0, 0)), N_SHARD, (D, HALF_FF), steps,
                     halves_epi, ride=ride)
        res, got = res if ride is not None else (res, [])
        big[l]["w_ffn_in"] = res[0]
        comm.landed(f"dw_ffn_in{l}", got)
        dh2 = _proj_bwd(f"ffn_in_bwd{l}", df, pl.BlockSpec((2, TM, D_FF), lambda i: (0, i, 0)),
                        lambda a_ref, j: a_ref[j // 2, :, (j % 2) * HALF_FF:(j % 2 + 1) * HALF_FF], wl["w_ffn_in"], 0, nb, cfg)
        dx1, dsh2, dsc2, dn2 = _norm_bwd(f"norm2_bwd{l}", sv["x1"], dh2, dx, small["norm2_g"][l][None], mod3, 4, nb, False, cfg)
        dy1, dg1 = _gate_bwd(f"gate1_bwd{l}", dx1, sv["y1"], mod3, 2, nb, cfg)
        dmix = _mm(f"out_proj_bwd{l}", "nt", (nb, 1, 1), dy1, wl["w_out"].reshape(1, D, D), row(D), whole(D),
                   [jax.ShapeDtypeStruct((cfg.r, D), F32)], [row(D)])[0]
        big[l]["w_out"] = _mm_dw(
            f"dw_out{l}", sv["mix"], dy1, pl.BlockSpec((tr, D), lambda i, j, k: (k, 0)),
            pl.BlockSpec((tr, D), lambda i, j, k: (k, 0)), (2, N_SHARD, D // 8, D),
            pl.BlockSpec((2, N_SHARD, D // 8, D), lambda i, j, k: (0, 0, 0, 0)), 1, (D, D), steps, row_shards_epi(N_SHARD))[0]
        du, dkvl, dkvc, dsink = hosted(f"att_bwd{l}", lambda ride: _att_bwd(
            f"att_bwd{l}", sv["qkv"], sv["mix"], dmix, sv["lse"], small["attn_sink"][l], cos_t, sin_t, l != last, cfg, ride))
        acc = [du] + [jnp.zeros(sh, F32) for sh in _SMALL_SHAPES]
        acc = _convpool_bwd(f"convpool_bwd_lat{l}", sv["cpin"], sv["yconv"], dmix, sv["prm"], acc, cfg.seq, 0, cfg)
        if l != last:
            acc = _convpool_bwd(f"convpool_bwd_ctx{l}", sv["cpin"], sv["yconv"], dmix, sv["prm"], acc, cfg.ctx,
                                cfg.nl // cfg.ctx, cfg)
        du, g_dw, g_dwb, g_lng, g_lnb, g_wbd, g_ps = acc
        du = _place_kv(f"place_kv{l}", du, dkvl, dkvc, l != last, cfg)
        if l == last:
            du = _place_ctx_kv_only(f"place_ctx_kv{l}", du, dkvc, cfg)
        sg["attn_sink"][l] = dsink[:, 0]
        sg["conv_dw"][l], sg["conv_dw_b"][l], sg["conv_ln_g"][l], sg["conv_ln_b"][l] = g_dw[:CONV_K], g_dwb[0], g_lng[0], g_lnb[0]
        sg["pool_w"][l] = jnp.stack([g_wbd[g * 64:(g + 1) * 64, g * 64:(g + 1) * 64] for g in range(4)])
        sg["pool_scale"][l] = g_ps[0]
        tr_all = _dw_rows(cfg.r)
        big[l]["w_in"] = _mm_dw(
            f"dw_in{l}", sv["h1"], du, pl.BlockSpec((tr_all, D), lambda i, j, k: (k, 0)),
            pl.BlockSpec((tr_all, IN_W), lambda i, j, k: (k, 0)), (2, N_SHARD, D // 2, IN_SHARD),
            pl.BlockSpec((2, N_SHARD, D // 2, IN_SHARD), lambda i, j, k: (0, 0, 0, 0)), 1, (D, IN_W), cfg.r // tr_all,
            col_shards_epi)[0]
        dh1 = _proj_bwd(f"in_proj_bwd{l}", du, pl.BlockSpec((TM, IN_W), lambda i: (i, 0)),
                        lambda a_ref, j: a_ref[:, j * IN_SHARD:(j + 1) * IN_SHARD], wl["w_in"], 0, nb_all, cfg)
        dx, dsh1, dsc1, dn1 = _norm_bwd(f"norm1_bwd{l}", sv["x0"], dh1, dx1, small["norm1_g"][l][None], mod3, 1, nb_all,
                                        l == last, cfg)
        sg["norm1_g"][l], sg["norm2_g"][l] = dn1[0], dn2[0]
        parts = [dsh1, dsc1, dg1, dsh2, dsc2, dg2]
        dm = jnp.concatenate([t[:cfg.b, 0, :] for t in parts], axis=1)
        live = (0, 1) if l == last else range(6)
        dm_ctx = jnp.concatenate([t[cfg.b, 0, :] if i in live else jnp.zeros((D,), F32) for i, t in enumerate(parts)])
        dms[l] = jnp.concatenate([dm, dm_ctx[None, :], jnp.zeros((MOD_ROWS - cfg.b - 1, 6 * D), F32)], axis=0)

    grad_x = dx[:cfg.nl].reshape(x.shape)
    small_grads = {k: jnp.stack(v) for k, v in sg.items()}
    small_grads["final_g"] = d_final_g[0]
    return loss, grad_x, small_grads, dms


_BIG = ("w_in", "w_out", "w_ffn_in", "w_ffn_out")
_TAPS = "conv_dw"
_SMALL = ("c_ctx", "b_mod", "norm1_g", "norm2_g", "conv_dw", "conv_dw_b", "conv_ln_g", "conv_ln_b", "attn_sink",
          "pool_w", "pool_scale", "final_g")
_ORDER = ("c_ctx", "w_mod", "b_mod", "norm1_g", "norm2_g", "w_in", "conv_dw", "conv_dw_b", "conv_ln_g", "conv_ln_b",
          "attn_sink", "pool_w", "pool_scale", "w_out", "w_ffn_in", "w_ffn_out", "final_g")
_GATHER_HOSTS = {"att_fwd0": (0, ("w_out", "w_ffn_in", "w_ffn_out", _TAPS)), "ffn_in0": (1, _BIG + (_TAPS,))}
_REDUCE_HOSTS = {"dw_ffn_in0": (1, _BIG), "att_bwd0": (0, ("w_ffn_in", "w_ffn_out"))}
_FIRST_GATHER = (0, ("w_in",))
_LAST_REDUCE = (0, ("w_in", "w_out"))


class _Comm:
    def __init__(self, w, c_idx, jc_idx):
        self.shapes = {k: w[k].shape[1:] for k in _BIG}
        self.c_idx, self.jc_idx = c_idx, jc_idx
        halves = lambda a: a.reshape(2, a.shape[0] // 2, a.shape[1])
        taps = jnp.pad(w[_TAPS], ((0, 0), (0, 1), (0, 64)))
        cast = {k: w[k].astype(BF16) for k in _BIG}
        self.shards = [{**{k: halves(cast[k][l]) for k in _BIG}, _TAPS: halves(taps[l])} for l in range(2)]
        self.wf = [dict(), dict()]
        self.grads = [dict(), dict()]
        self.reduced = [dict(), dict()]
        self._open = {}
        self.landed("first", _run_ride("gather_first", self.ride("first")))

    def ride(self, host):
        if host == "first" or host in _GATHER_HOSTS:
            layer, keys = _FIRST_GATHER if host == "first" else _GATHER_HOSTS[host]
            return _gather_ride([self.shards[layer][k] for k in keys])
        if host == "last" or host in _REDUCE_HOSTS:
            layer, keys = _LAST_REDUCE if host == "last" else _REDUCE_HOSTS[host]
            tag = f"{layer}_{keys[0]}"
            mine = [self.grads[layer][k] for k in keys]
            other = _send_other_half(f"send_other_half{tag}", mine)
            pair = [_pair_add(f"pair_add{layer}_{k}", a, b, self.c_idx) for k, a, b in zip(keys, mine, other)]
            self._open[host] = pair
            return _exchange_ride(pair)
        return None

    def landed(self, host, got):
        if host == "first" or host in _GATHER_HOSTS:
            layer, keys = _FIRST_GATHER if host == "first" else _GATHER_HOSTS[host]
            for k, f in zip(keys, got):
                if k == _TAPS:
                    taps = f.reshape(N_SHARD, 32, 128)[:, :CONV_K, :64]
                    self.wf[layer][k] = jnp.transpose(taps, (1, 0, 2)).reshape(CONV_K, CONV_W)
                else:
                    self.wf[layer][k] = f.reshape((1, N_SHARD) + self.shapes[k])
        if host == "last" or host in _REDUCE_HOSTS:
            layer, keys = _LAST_REDUCE if host == "last" else _REDUCE_HOSTS[host]
            mine = [_shard_sum(f"shard_sum{layer}_{k}", a, b, self.jc_idx) for k, a, b in zip(keys, self._open.pop(host), got)]
            for k, g in zip(keys, _swap_reduced(f"swap_reduced{layer}_{keys[0]}", mine)):
                self.reduced[layer][k] = g.reshape(self.shapes[k])

    def finish(self):
        self.landed("last", _run_ride("exchange_last", self.ride("last")))


def _conditioning(c, c_ctx, w_mod, b_mod, chip):
    b = c.shape[0]
    block = jnp.concatenate([c, c_ctx[None, :], jnp.zeros((8 - b - 1, D), F32)], axis=0)
    c_all = _share_small("share_c", block, False).reshape(64, D)
    bias = lax.dynamic_slice_in_dim(b_mod, chip * MOD_W, MOD_W, axis=1)
    full = lambda r, q: pl.BlockSpec((r, q), lambda i, j, k: (0, 0))

    def bias_epi(acc, ex, outs):
        outs[0][...] = acc + ex[0][...]

    mv = [_mm(f"mod_fwd{l}", "nn", (1, 1, 1), c_all, w_mod[l], full(64, D), full(D, MOD_W),
              [jax.ShapeDtypeStruct((64, MOD_W), F32)], [full(64, MOD_W)], extras=[bias[l][None]],
              extra_specs=[full(1, MOD_W)], a_fn=_silu, epi=bias_epi)[0] for l in range(2)]
    by_dev = jnp.transpose(jnp.stack(mv).reshape(2, 8, 8, MOD_W), (1, 0, 2, 3))
    rows = jnp.transpose(_mod_rows_exchange(by_dev), (1, 2, 0, 3)).reshape(2, 8, 6 * D)
    rows = jnp.pad(rows, ((0, 0), (0, MOD_ROWS - 8), (0, 0)))
    return [rows[l].reshape(MOD_ROWS, 6, D) for l in range(2)], c_all


def _conditioning_bwd(dms, c_all, w_mod, b):
    dm = jnp.stack([d[:8] for d in dms])
    by_chip = jnp.transpose(dm.reshape(2, 8, N_SHARD, MOD_W), (2, 0, 1, 3))
    gathered, d_bias = _mod_grad_exchange(by_chip, dm)
    dm_all = jnp.transpose(gathered, (1, 0, 2, 3)).reshape(2, 64, MOD_W)
    full = lambda r, q: pl.BlockSpec((r, q), lambda i, j, k: (0, 0))

    def ctx_rows_epi(acc, ex, outs):
        row = lax.broadcasted_iota(jnp.int32, acc.shape, 0) % 8
        outs[0][...] = _colsum(jnp.where(row == b, acc * _dsilu(ex[0][...]), 0.0))

    g_mod, d_ctx = [], jnp.zeros((D,), F32)
    for l in range(2):
        g_mod.append(_mm(f"dw_mod{l}", "tn", (1, 1, 1), c_all, dm_all[l], full(64, D), full(64, MOD_W),
                         [jax.ShapeDtypeStruct((D, MOD_W), F32)], [full(D, MOD_W)], a_fn=_silu)[0])
        part = _mm(f"mod_bwd{l}", "nt", (1, 1, 1), dm_all[l], w_mod[l], full(64, MOD_W), full(D, MOD_W),
                   [jax.ShapeDtypeStruct((1, D), F32)], [full(1, D)], extras=[c_all], extra_specs=[full(64, D)],
                   epi=ctx_rows_epi)[0]
        d_ctx = d_ctx + part[0]
    return g_mod, d_bias[:, 0, :], d_ctx


def kernel(x, c, ctx, c_ctx, w_mod, b_mod, norm1_g, norm2_g, w_in, conv_dw, conv_dw_b, conv_ln_g, conv_ln_b, attn_sink, pool_w, pool_scale, w_out, w_ffn_in, w_ffn_out, final_g, loss_target, m_c_ctx, m_w_mod, m_b_mod, m_norm1_g, m_norm2_g, m_w_in, m_conv_dw, m_conv_dw_b, m_conv_ln_g, m_conv_ln_b, m_attn_sink, m_pool_w, m_pool_scale, m_w_out, m_w_ffn_in, m_w_ffn_out, m_final_g, v_c_ctx, v_w_mod, v_b_mod, v_norm1_g, v_norm2_g, v_w_in, v_conv_dw, v_conv_dw_b, v_conv_ln_g, v_conv_ln_b, v_attn_sink, v_pool_w, v_pool_scale, v_w_out, v_w_ffn_in, v_w_ffn_out, v_final_g):
    w = dict(c_ctx=c_ctx, w_mod=w_mod, b_mod=b_mod, norm1_g=norm1_g, norm2_g=norm2_g, w_in=w_in, conv_dw=conv_dw,
             conv_dw_b=conv_dw_b, conv_ln_g=conv_ln_g, conv_ln_b=conv_ln_b, attn_sink=attn_sink, pool_w=pool_w,
             pool_scale=pool_scale, w_out=w_out, w_ffn_in=w_ffn_in, w_ffn_out=w_ffn_out, final_g=final_g)
    m = dict(c_ctx=m_c_ctx, w_mod=m_w_mod, b_mod=m_b_mod, norm1_g=m_norm1_g, norm2_g=m_norm2_g, w_in=m_w_in,
             conv_dw=m_conv_dw, conv_dw_b=m_conv_dw_b, conv_ln_g=m_conv_ln_g, conv_ln_b=m_conv_ln_b,
             attn_sink=m_attn_sink, pool_w=m_pool_w, pool_scale=m_pool_scale, w_out=m_w_out, w_ffn_in=m_w_ffn_in,
             w_ffn_out=m_w_ffn_out, final_g=m_final_g)
    v = dict(c_ctx=v_c_ctx, w_mod=v_w_mod, b_mod=v_b_mod, norm1_g=v_norm1_g, norm2_g=v_norm2_g, w_in=v_w_in,
             conv_dw=v_conv_dw, conv_dw_b=v_conv_dw_b, conv_ln_g=v_conv_ln_g, conv_ln_b=v_conv_ln_b,
             attn_sink=v_attn_sink, pool_w=v_pool_w, pool_scale=v_pool_scale, w_out=v_w_out, w_ffn_in=v_w_ffn_in,
             w_ffn_out=v_w_ffn_out, final_g=v_final_g)
    xi, yi, ci = _position()
    chip = 2 * xi + yi
    mod3s, c_all = _conditioning(c, c_ctx, w_mod, b_mod, chip)
    comm = _Comm(w, jnp.reshape(ci, (1,)).astype(jnp.int32), jnp.stack([chip, ci]).astype(jnp.int32))
    small = {k: w[k] for k in _SMALL if k not in ("conv_dw", "c_ctx", "b_mod")}
    loss, grad_x, sgrads, dms = _local_step(x, ctx, small, mod3s, loss_target, comm)
    comm.finish()
    g_mod, sgrads["b_mod"], d_ctx = _conditioning_bwd(dms, c_all, w_mod, c.shape[0])
    sgrads["c_ctx"] = 0.5 * d_ctx

    names = list(_SMALL)
    total = _share_small("sum_small", _pack([loss] + [sgrads[k] for k in names]), True)
    parts = _unpack(total, [()] + [sgrads[k].shape for k in names])
    loss_out = parts[0]
    gsmall = dict(zip(names, parts[1:]))
    gsmall["conv_dw"] = lax.dynamic_slice_in_dim(gsmall["conv_dw"], chip * 64, 64, axis=2)

    grads, delta, new_m, new_v = dict(gsmall), {}, {}, {}
    reduced = [{**comm.reduced[l], "w_mod": g_mod[l]} for l in range(2)]
    for k in ("w_mod",) + _BIG:
        s0, s1 = w[k].shape[1:]
        flat = lambda a: a.reshape(2 * s0, s1)
        g_, d_, m_, v_ = _adamw_layers(f"adamw_{k}", flat(w[k]), [reduced[l][k] for l in range(2)], flat(m[k]), flat(v[k]))
        grads[k], delta[k], new_m[k], new_v[k] = [a.reshape(w[k].shape) for a in (g_, d_, m_, v_)]
    d_, m_, v_ = _adamw("adamw_small", _pack([w[k] for k in names]), _pack([gsmall[k] for k in names]),
                        _pack([m[k] for k in names]), _pack([v[k] for k in names]))
    sshapes = [w[k].shape for k in names]
    for k, a, b, e in zip(names, _unpack(d_, sshapes), _unpack(m_, sshapes), _unpack(v_, sshapes)):
        delta[k], new_m[k], new_v[k] = a, b, e
    return (loss_out, grad_x, *[grads[k] for k in _ORDER], *[delta[k] for k in _ORDER],
            *[new_m[k] for k in _ORDER], *[new_v[k] for k in _ORDER])
```

```python
from typing import NamedTuple

import jax
import jax.numpy as jnp
import numpy as np
from jax import lax
from jax.experimental import pallas as pl
from jax.experimental.pallas import tpu as pltpu

F32 = jnp.float32
BF16 = jnp.bfloat16

D = 1024
GRID_W = 64
HEAD_DIM = 64
N_HEADS = 8
ATTN_W = 512
CONV_W = 256
POOL_W = 256
IN_W = 1536
D_FF = 2816
CONV_K = 31
QB = 128
ROPE_BASE = 10000.0
EPS = 1e-6
NEG = -1e30
N_SHARD = 4
IN_SHARD = IN_W // N_SHARD
HALF_FF = D_FF // 2
MOD_W = 6 * D // N_SHARD
MOD_ROWS = 16
PACK_LANES = 128

ADAM_LR = 0.001
ADAM_B1 = 0.9
ADAM_B2 = 0.999
ADAM_EPS = 1e-08
ADAM_WD = 0.01
ADAM_STEP = 10

VMEM_LIMIT_V7X = 56 * 1024 * 1024
TM = 512
TR_MAX = 1024
CHUNK = 256
PAD = 16

_MESH = pl.DeviceIdType.MESH
_ANY = pl.BlockSpec(memory_space=pl.ANY)
_DIMS = {"nn": (((1,), (0,)), ((), ())), "nt": (((1,), (1,)), ((), ())), "tn": (((0,), (0,)), ((), ()))}


class _Cfg(NamedTuple):
    b: int
    seq: int
    ctx: int

    @property
    def nl(self):
        return self.b * self.seq

    @property
    def nc(self):
        return self.b * self.ctx

    @property
    def r(self):
        return self.nl + self.nc

    def mod_row(self, i):
        return jnp.where(i < self.nl // TM, i // (self.seq // TM), self.b)

    def first_of_row(self, i):
        nlb = self.nl // TM
        return jnp.logical_or(jnp.logical_and(i < nlb, i % (self.seq // TM) == 0), i == nlb)


def _params(n_grid=0):
    sem = ("arbitrary",) * n_grid if n_grid else None
    return pltpu.CompilerParams(dimension_semantics=sem, vmem_limit_bytes=VMEM_LIMIT_V7X)


def _dot(a, b, mode="nn"):
    return lax.dot_general(a.astype(BF16), b.astype(BF16), _DIMS[mode], preferred_element_type=F32)


def _sigmoid(x):
    return 1.0 / (1.0 + jnp.exp(-x))


def _silu(x):
    return x * _sigmoid(x)


def _dsilu(x):
    s = _sigmoid(x)
    return s * (1.0 + x * (1.0 - s))


def _colsum(v):
    return jnp.sum(v, axis=0, keepdims=True)


def _dw_rows(rows):
    return TR_MAX if rows % TR_MAX == 0 else TM


def _epi_store(acc, ex, outs):
    for o in outs:
        o[...] = acc.astype(o.dtype)


class _Ride(NamedTuple):
    ins: list
    out_shape: list
    scratch: list
    start: object
    finish: object


class _Hosted(NamedTuple):
    ride: _Ride
    n_in: int
    n_out: int
    grid: tuple

    def split(self, refs):
        n_ri, n_ro, n_rs = len(self.ride.ins), len(self.ride.out_shape), len(self.ride.scratch)
        r_in = refs[self.n_in:self.n_in + n_ri]
        r_out = refs[self.n_in + n_ri + self.n_out:self.n_in + n_ri + self.n_out + n_ro]
        own = refs[:self.n_in] + refs[self.n_in + n_ri:self.n_in + n_ri + self.n_out] + \
            refs[self.n_in + n_ri + self.n_out + n_ro:len(refs) - n_rs]
        return own, (r_in, r_out, refs[len(refs) - n_rs:])

    def start(self, parts):
        ids = [pl.program_id(d) for d in range(len(self.grid))]
        first = ids[0] == 0
        for i in ids[1:]:
            first = jnp.logical_and(first, i == 0)
        pl.when(first)(lambda: self.ride.start(*parts))

    def finish(self, parts):
        ids = [pl.program_id(d) for d in range(len(self.grid))]
        last = ids[0] == self.grid[0] - 1
        for i, g in zip(ids[1:], self.grid[1:]):
            last = jnp.logical_and(last, i == g - 1)
        pl.when(last)(lambda: self.ride.finish(*parts))


def _hosted_call(body, ride, name, grid, ins, in_specs, out_shape, out_specs, scratch, params):
    if ride is None:
        res = pl.pallas_call(body, name=name, grid=grid, in_specs=in_specs, out_specs=out_specs, out_shape=out_shape,
                             scratch_shapes=scratch, compiler_params=params)(*ins)
        return list(res), []
    host = _Hosted(ride, len(ins), len(out_shape), tuple(grid))

    def hosted(*refs):
        own, parts = host.split(refs)
        host.start(parts)
        body(*own)
        host.finish(parts)

    res = pl.pallas_call(
        hosted, name=name, grid=grid, in_specs=list(in_specs) + [_ANY] * len(ride.ins),
        out_specs=list(out_specs) + [_ANY] * len(ride.out_shape), out_shape=list(out_shape) + list(ride.out_shape),
        scratch_shapes=list(scratch) + list(ride.scratch), compiler_params=params)(*ins, *ride.ins)
    return list(res[:len(out_shape)]), list(res[len(out_shape):])


def _mm(name, mode, grid, a, b, a_spec, b_spec, out_shape, out_specs, acc_shape=None, extras=(),
        extra_specs=(), a_fn=None, epi=_epi_store, ride=None):
    nk = grid[2]
    n_ex, n_out = len(extras), len(out_shape)

    def body(*refs):
        a_ref, b_ref = refs[:2]
        ex = refs[2:2 + n_ex]
        outs = refs[2 + n_ex:2 + n_ex + n_out]
        av = a_ref[...]
        if a_fn is not None:
            av = a_fn(av)
        part = _dot(av, b_ref[...], mode)
        if nk == 1:
            epi(part, ex, outs)
        else:
            acc = refs[-1]
            k = pl.program_id(2)

            @pl.when(k == 0)
            def _():
                acc[...] = part

            @pl.when(k > 0)
            def _():
                acc[...] += part

            @pl.when(k == nk - 1)
            def _():
                epi(acc[...], ex, outs)

    scratch = [] if nk == 1 else [pltpu.VMEM(acc_shape, F32)]
    outs, ride_outs = _hosted_call(body, ride, name, grid, [a, b, *extras], [a_spec, b_spec, *extra_specs],
                                   list(out_shape), list(out_specs), scratch, _params(3))
    return outs if ride is None else (outs, ride_outs)


def _mm_dw(name, a, b, a_spec, b_spec, out_shape, out_spec, n_out_blocks, acc_shape, n_steps, epi, a_fn=None,
           extras=(), extra_specs=(), extra_out=(), extra_out_specs=(), ride=None):
    grid = (1, n_out_blocks, n_steps)
    outs = [jax.ShapeDtypeStruct(out_shape, BF16)] + list(extra_out)
    return _mm(name, "tn", grid, a, b, a_spec, b_spec, outs, [out_spec, *extra_out_specs], acc_shape, extras,
               extra_specs, a_fn, epi, ride)


def _proj_norm_bwd(name, a, a_spec, pick, w, x, dres, gvec, mod3, isc, nblk, res_latent_only, cfg):
    ns = w.shape[-1]
    nlb = cfg.nl // TM

    def body(a_ref, w_ref, x_ref, dres_ref, g_ref, m_ref, dx_ref, dsh_ref, dsc_ref, dg_ref):
        i = pl.program_id(0)
        dhv = _dot(pick(a_ref, 0), w_ref[0], "nt")
        for j in range(1, N_SHARD):
            dhv = dhv + _dot(pick(a_ref, j), w_ref[j], "nt")
        xv = x_ref[...]
        r = lax.rsqrt(jnp.mean(xv * xv, axis=-1, keepdims=True) + EPS)
        xh = xv * r
        g = g_ref[...]
        sc1 = 1.0 + m_ref[isc:isc + 1, :]
        t = dhv * xh
        first = cfg.first_of_row(i)
        _accumulate_rows(first, dsh_ref, _colsum(dhv))
        _accumulate_rows(first, dsc_ref, _colsum(t * g))
        _accumulate_rows(i == 0, dg_ref, _colsum(t * sc1))
        dxh = dhv * (g * sc1)
        dxn = r * (dxh - xh * jnp.mean(dxh * xh, axis=-1, keepdims=True))
        if res_latent_only:
            dx_ref[...] = jnp.where(i < nlb, dres_ref[...], 0.0) + dxn
        else:
            dx_ref[...] = dres_ref[...] + dxn

    row = pl.BlockSpec((TM, D), lambda i: (i, 0))
    vec = pl.BlockSpec((1, D), lambda i: (0, 0))
    part = pl.BlockSpec((None, 1, D), lambda i: (cfg.mod_row(i), 0, 0))
    part_shape = jax.ShapeDtypeStruct((MOD_ROWS, 1, D), F32)
    resident = pl.BlockSpec((None, N_SHARD, D, ns), lambda i: (0, 0, 0, 0), pipeline_mode=pl.Buffered(1))
    return pl.pallas_call(
        body, name=name, grid=(nblk,), in_specs=[a_spec, resident, row, row, vec, _mod_spec(cfg)],
        out_specs=[row, part, part, vec],
        out_shape=[jax.ShapeDtypeStruct((cfg.r, D), F32), part_shape, part_shape, jax.ShapeDtypeStruct((1, D), F32)],
        compiler_params=_params(1))(a, w, x, dres, gvec, mod3)


def _mod_spec(cfg):
    return pl.BlockSpec((None, 6, D), lambda i: (cfg.mod_row(i), 0, 0))


def _norm_fwd(name, x, gvec, mod3, ish, isc, nblk, cfg):
    def body(x_ref, g_ref, m_ref, o_ref):
        xv = x_ref[...]
        r = lax.rsqrt(jnp.mean(xv * xv, axis=-1, keepdims=True) + EPS)
        o_ref[...] = (xv * r * g_ref[...] * (1.0 + m_ref[isc:isc + 1, :]) + m_ref[ish:ish + 1, :]).astype(BF16)

    row = pl.BlockSpec((TM, D), lambda i: (i, 0))
    return pl.pallas_call(
        body, name=name, grid=(nblk,),
        in_specs=[row, pl.BlockSpec((1, D), lambda i: (0, 0)), _mod_spec(cfg)], out_specs=row,
        out_shape=jax.ShapeDtypeStruct((cfg.r, D), BF16), compiler_params=_params(1))(x, gvec, mod3)


def _accumulate_rows(first, ref, val):
    @pl.when(first)
    def _():
        ref[...] = val

    @pl.when(jnp.logical_not(first))
    def _():
        ref[...] += val


def _gate_bwd(name, dx, y, mod3, ig, nblk, cfg):
    def body(dx_ref, y_ref, m_ref, dy_ref, dg_ref):
        i = pl.program_id(0)
        dxv = dx_ref[...]
        dy_ref[...] = (dxv * m_ref[ig:ig + 1, :]).astype(BF16)
        _accumulate_rows(cfg.first_of_row(i), dg_ref, _colsum(dxv * y_ref[...].astype(F32)))

    row = pl.BlockSpec((TM, D), lambda i: (i, 0))
    part = pl.BlockSpec((None, 1, D), lambda i: (cfg.mod_row(i), 0, 0))
    return pl.pallas_call(
        body, name=name, grid=(nblk,), in_specs=[row, row, _mod_spec(cfg)], out_specs=[row, part],
        out_shape=[jax.ShapeDtypeStruct((cfg.r, D), BF16), jax.ShapeDtypeStruct((MOD_ROWS, 1, D), F32)],
        compiler_params=_params(1))(dx, y, mod3)


def _loss_head(x, target, gvec, cfg):
    def body(x_ref, t_ref, g_ref, dx_ref, loss_ref, dg_ref):
        i = pl.program_id(0)

        @pl.when(i == 0)
        def _():
            loss_ref[...] = jnp.zeros_like(loss_ref)
            dg_ref[...] = jnp.zeros_like(dg_ref)

        xv = x_ref[...]
        g = g_ref[...]
        r = lax.rsqrt(jnp.mean(xv * xv, axis=-1, keepdims=True) + EPS)
        xh = xv * r
        err = xh * g - t_ref[...]
        loss_ref[...] += (0.5 / D) * _colsum(jnp.sum(err * err, axis=-1, keepdims=True))
        dy = err * (1.0 / D)
        dg_ref[...] += _colsum(dy * xh)
        dxh = dy * g
        dx_ref[...] = r * (dxh - xh * jnp.mean(dxh * xh, axis=-1, keepdims=True))

    row = pl.BlockSpec((TM, D), lambda i: (i, 0))
    vec = pl.BlockSpec((1, D), lambda i: (0, 0))
    return pl.pallas_call(
        body, name="loss_head", grid=(cfg.nl // TM,), in_specs=[row, row, vec],
        out_specs=[row, pl.BlockSpec((1, 1), lambda i: (0, 0)), vec],
        out_shape=[jax.ShapeDtypeStruct((cfg.r, D), F32), jax.ShapeDtypeStruct((1, 1), F32),
                   jax.ShapeDtypeStruct((1, D), F32)],
        compiler_params=_params(1))(x, target, gvec)


def _rope_tables(seq):
    rows = seq // GRID_W
    row = jnp.repeat(jnp.arange(rows), GRID_W).astype(F32)
    col = jnp.tile(jnp.arange(GRID_W), rows).astype(F32)
    half = HEAD_DIM // 2
    inv = ROPE_BASE ** (-jnp.arange(0, half, 2, dtype=F32) / half)
    ar, ac = row[:, None] * inv, col[:, None] * inv
    ang = jnp.concatenate([ar, ar, ac, ac], axis=-1)
    sign = jnp.tile(jnp.concatenate([-jnp.ones((16,), F32), jnp.ones((16,), F32)]), 2)
    cos = jnp.tile(jnp.cos(ang), (1, 2))
    sin = jnp.tile(jnp.sin(ang) * sign, (1, 2))
    cos = jnp.concatenate([cos, jnp.ones((TM, 2 * HEAD_DIM), F32)], axis=0)
    sin = jnp.concatenate([sin, jnp.zeros((TM, 2 * HEAD_DIM), F32)], axis=0)
    return cos, sin


def _rope(x, cos, sin_signed, sign):
    lane = lax.broadcasted_iota(jnp.int32, x.shape, 1)
    low = (lane % 32) < 16
    rot = jnp.where(low, pltpu.roll(x, 112, 1), pltpu.roll(x, 16, 1))
    return x * cos + sign * (rot * sin_signed)


def _in_proj(name, h, w_in, layer, cos_t, sin_t, cfg):
    nlb, bps = cfg.nl // TM, cfg.seq // TM

    def body(h_ref, w_ref, cos_ref, sin_ref, qkv_ref, cp_ref):
        hv = h_ref[...]
        u = jnp.concatenate([_dot(hv, w_ref[j]) for j in range(N_SHARD)], axis=1)
        cos, sin = cos_ref[...], sin_ref[...]
        tiles = []
        for t in range(5):
            y = _rope(u[:, 128 * t:128 * (t + 1)], cos, sin, 1.0)
            tiles.append(y * (HEAD_DIM ** -0.5) if t < 4 else y)
        tiles.append(u[:, 640:768])
        qkv_ref[...] = jnp.concatenate(tiles, axis=1).astype(BF16)
        cp_ref[...] = u[:, 768:IN_W].astype(BF16)

    tab = pl.BlockSpec((TM, 128), lambda i: (jnp.where(i < nlb, i % bps, bps), 0))
    half = pl.BlockSpec((TM, 768), lambda i: (i, 0))
    return pl.pallas_call(
        body, name=name, grid=(cfg.r // TM,),
        in_specs=[pl.BlockSpec((TM, D), lambda i: (i, 0)),
                  pl.BlockSpec((None, N_SHARD, D, IN_SHARD), lambda i: (layer, 0, 0, 0)), tab, tab],
        out_specs=[half, half],
        out_shape=[jax.ShapeDtypeStruct((cfg.r, 768), BF16), jax.ShapeDtypeStruct((cfg.r, 768), BF16)],
        compiler_params=_params(1))(h, w_in, cos_t, sin_t)


def _att_specs(cfg):
    nlb, ncb = cfg.seq // QB, cfg.ctx // QB

    def qblk(s, qb):
        return jnp.where(qb < nlb, s * nlb + qb, cfg.nl // QB + s * ncb + qb - nlb)

    def near(off, col):
        return pl.BlockSpec((QB, 128), lambda s, qb: (s * nlb + jnp.clip(qb + off, 0, nlb - 1), col))

    def ctxs(col):
        return pl.BlockSpec((cfg.ctx, 128), lambda s, qb: (cfg.nl // cfg.ctx + s, col))

    qspec = pl.BlockSpec((QB, ATTN_W), lambda s, qb: (qblk(s, qb), 0))
    kv = [ctxs(4), ctxs(5), near(-1, 4), near(0, 4), near(1, 4), near(-1, 5), near(0, 5), near(1, 5)]
    return qblk, qspec, kv


def _att_scores(qb, nlb, sink_ref, q_ref, k_refs, v_refs, kh):
    is_lat = qb < nlb
    ii = lax.broadcasted_iota(jnp.int32, (4 * QB, 3 * QB), 0) % QB
    col = lax.broadcasted_iota(jnp.int32, (4 * QB, 3 * QB), 1)
    jj, blk = col % QB, col // QB
    off_p = jnp.where(jnp.logical_and(is_lat, qb >= 1), 0.0, NEG)
    off_c = jnp.where(is_lat, 0.0, NEG)
    off_n = jnp.where(jnp.logical_and(is_lat, qb <= nlb - 2), 0.0, NEG)
    inside = jnp.logical_or(blk == 1, jnp.logical_or(jnp.logical_and(blk == 0, jj >= ii),
                                                     jnp.logical_and(blk == 2, jj <= ii)))
    off = jnp.where(blk == 0, off_p, jnp.where(blk == 1, off_c, off_n))
    q4 = jnp.concatenate([q_ref[:, (4 * kh + g) * HEAD_DIM:(4 * kh + g + 1) * HEAD_DIM] for g in range(4)], axis=0)
    rg = lax.broadcasted_iota(jnp.int32, (4 * QB, 1), 0) // QB
    snk = jnp.where(rg == 0, sink_ref[4 * kh],
                    jnp.where(rg == 1, sink_ref[4 * kh + 1], jnp.where(rg == 2, sink_ref[4 * kh + 2], sink_ref[4 * kh + 3])))
    lanes = slice(kh * HEAD_DIM, (kh + 1) * HEAD_DIM)
    kx, vx = k_refs[0][:, lanes], v_refs[0][:, lanes]
    kl = jnp.concatenate([r[:, lanes] for r in k_refs[1:]], axis=0)
    vl = jnp.concatenate([r[:, lanes] for r in v_refs[1:]], axis=0)
    sx = _dot(q4, kx, "nt")
    sl = jnp.where(inside, _dot(q4, kl, "nt"), NEG) + off
    return q4, snk, (kx, kl), (vx, vl), (sx, sl)


def _att_fwd(name, qkv, sink, ctx_queries, cfg, ride=None):
    nlb, ncb = cfg.seq // QB, cfg.ctx // QB
    qblk, qspec, kvspecs = _att_specs(cfg)

    def body(sink_ref, q_ref, kx_ref, vx_ref, kp_ref, kc_ref, kn_ref, vp_ref, vc_ref, vn_ref, o_ref, lse_ref):
        qb = pl.program_id(1)
        for kh in range(2):
            q4, snk, _, vs, ss = _att_scores(qb, nlb, sink_ref, q_ref, (kx_ref, kp_ref, kc_ref, kn_ref),
                                             (vx_ref, vp_ref, vc_ref, vn_ref), kh)
            m = snk
            for s_ in ss:
                m = jnp.maximum(m, jnp.max(s_, axis=-1, keepdims=True))
            den = jnp.exp(snk - m)
            o4 = jnp.zeros((4 * QB, HEAD_DIM), F32)
            for s_, v_ in zip(ss, vs):
                p = jnp.exp(s_ - m)
                den = den + jnp.sum(p, axis=-1, keepdims=True)
                o4 = o4 + _dot(p, v_)
            o4 = o4 / den
            lse = m + jnp.log(den)
            for g in range(4):
                h = 4 * kh + g
                o_ref[:, h * HEAD_DIM:(h + 1) * HEAD_DIM] = o4[g * QB:(g + 1) * QB].astype(BF16)
                lse_ref[:, h:h + 1] = lse[g * QB:(g + 1) * QB]

    return _hosted_call(
        body, ride, name, (cfg.b, nlb + (ncb if ctx_queries else 0)), [sink] + [qkv] * 9,
        [pl.BlockSpec(memory_space=pltpu.SMEM), qspec, *kvspecs],
        [jax.ShapeDtypeStruct((cfg.r, D), BF16), jax.ShapeDtypeStruct((cfg.r, N_HEADS), F32)],
        [pl.BlockSpec((QB, ATTN_W), lambda s, qb: (qblk(s, qb), 0)),
         pl.BlockSpec((QB, N_HEADS), lambda s, qb: (qblk(s, qb), 0))], [], _params(2))


def _att_bwd(name, qkv, mix, dmix, lse, sink, cos_t, sin_t, ctx_queries, cfg, ride=None):
    nlb, ncb = cfg.seq // QB, cfg.ctx // QB
    nqb = nlb + (ncb if ctx_queries else 0)
    qblk, qspec, kvspecs = _att_specs(cfg)

    def body(sink_ref, q_ref, kx_ref, vx_ref, kp_ref, kc_ref, kn_ref, vp_ref, vc_ref, vn_ref, o_ref, do_ref,
             lse_ref, cosq_ref, sinq_ref, cosk_ref, sink_tab_ref, dq_ref, dkvl_ref, dkvc_ref, dsink_ref,
             accl, accc, dqs):
        s_id, qb = pl.program_id(0), pl.program_id(1)

        @pl.when(qb == 0)
        def _():
            accl[...] = jnp.zeros_like(accl)
            accc[...] = jnp.zeros_like(accc)

        @pl.when(jnp.logical_and(s_id == 0, qb == 0))
        def _():
            dsink_ref[...] = jnp.zeros_like(dsink_ref)

        starts = [pl.multiple_of(jnp.clip(qb + off, 0, nlb - 1) * QB, QB) for off in (-1, 0, 1)]
        for kh in range(2):
            q4, snk, ks, vs, ss = _att_scores(qb, nlb, sink_ref, q_ref, (kx_ref, kp_ref, kc_ref, kn_ref),
                                              (vx_ref, vp_ref, vc_ref, vn_ref), kh)
            lanes = slice(kh * HEAD_DIM, (kh + 1) * HEAD_DIM)
            heads =[slice((4 * kh + g) * HEAD_DIM, (4 * kh + g + 1) * HEAD_DIM) for g in range(4)]
            do4 = jnp.concatenate([do_ref[:, hs] for hs in heads], axis=0)
            o4 = jnp.concatenate([o_ref[:, hs] for hs in heads], axis=0).astype(F32)
            lse4 = jnp.concatenate([lse_ref[:, 4 * kh + g:4 * kh + g + 1] for g in range(4)], axis=0)
            delta = jnp.sum(do4 * o4, axis=-1, keepdims=True)
            dq4 = jnp.zeros((4 * QB, HEAD_DIM), F32)
            dks, dvs = [], []
            for s_, k_, v_ in zip(ss, ks, vs):
                p = jnp.exp(s_ - lse4)
                ds = p * (_dot(do4, v_, "nt") - delta)
                dq4 = dq4 + _dot(ds, k_)
                dks.append(_dot(ds, q4, "tn"))
                dvs.append(_dot(p, do4, "tn"))
            accc[:, lanes] += dks[0]
            accc[:, 128 + kh * HEAD_DIM:128 + (kh + 1) * HEAD_DIM] += dvs[0]
            for t, st in enumerate(starts):
                accl[pl.ds(st, QB), lanes] += dks[1][t * QB:(t + 1) * QB]
                accl[pl.ds(st, QB), 128 + kh * HEAD_DIM:128 + (kh + 1) * HEAD_DIM] += dvs[1][t * QB:(t + 1) * QB]
            dsk = -jnp.exp(snk - lse4) * delta
            for g in range(4):
                h = 4 * kh + g
                dsink_ref[h:h + 1, :] += jnp.broadcast_to(_colsum(dsk[g * QB:(g + 1) * QB]), (1, 128))
                dqs[:, heads[g]] = dq4[g * QB:(g + 1) * QB]
        cos, sin = cosq_ref[...], sinq_ref[...]
        dq_ref[...] = jnp.concatenate(
            [_rope(dqs[:, 128 * t:128 * (t + 1)], cos, sin, -1.0) * (HEAD_DIM ** -0.5) for t in range(4)],
            axis=1).astype(BF16)

        @pl.when(qb == nqb - 1)
        def _():
            dk = _rope(accl[:, 0:128], cosk_ref[...], sink_tab_ref[...], -1.0)
            dkvl_ref[...] = jnp.concatenate([dk, accl[:, 128:256]], axis=1).astype(BF16)
            dkvc_ref[...] = accc[...].astype(BF16)

    rowq = lambda w: pl.BlockSpec((QB, w), lambda s, qb: (qblk(s, qb), 0))
    tabq = pl.BlockSpec((QB, 128), lambda s, qb: (jnp.where(qb < nlb, qb, cfg.seq // QB), 0))
    tabk = pl.BlockSpec((cfg.seq, 128), lambda s, qb: (0, 0))
    return _hosted_call(
        body, ride, name, (cfg.b, nqb), [sink] + [qkv] * 9 + [mix, dmix, lse, cos_t, sin_t, cos_t, sin_t],
        [pl.BlockSpec(memory_space=pltpu.SMEM), qspec, *kvspecs, rowq(ATTN_W), rowq(ATTN_W), rowq(N_HEADS),
         tabq, tabq, tabk, tabk],
        [jax.ShapeDtypeStruct((cfg.r, IN_W), BF16), jax.ShapeDtypeStruct((cfg.nl, 256), BF16),
         jax.ShapeDtypeStruct((cfg.nc, 256), BF16), jax.ShapeDtypeStruct((N_HEADS, 128), F32)],
        [rowq(ATTN_W), pl.BlockSpec((cfg.seq, 256), lambda s, qb: (s, 0)),
         pl.BlockSpec((cfg.ctx, 256), lambda s, qb: (s, 0)), pl.BlockSpec((N_HEADS, 128), lambda s, qb: (0, 0))],
        [pltpu.VMEM((cfg.seq, 256), F32), pltpu.VMEM((cfg.ctx, 256), F32), pltpu.VMEM((QB, ATTN_W), F32)], _params(2))


def _pool_geometry(n, c):
    lane = lax.broadcasted_iota(jnp.int32, (1, POOL_W), 1) // HEAD_DIM
    wl = jnp.where(lane == 0, 1, jnp.where(lane == 1, 2, jnp.where(lane == 2, 4, 8)))
    wr = wl - 1
    t = c * CHUNK + lax.broadcasted_iota(jnp.int32, (CHUNK, POOL_W), 0)
    cnt = (jnp.minimum(t + wr, n - 1) - jnp.maximum(t - wl, 0) + 1).astype(F32)
    return wl, wr, cnt


def _build_phases(src, ph, c):
    for s in range(1, 8):
        ph[s - 1] = src[c * CHUNK + s:c * CHUNK + s + CHUNK + 24, :]


def _window(src, ph, c, off):
    a, s = divmod(off, 8)
    if s == 0:
        return src[c * CHUNK + 8 * a:c * CHUNK + 8 * a + CHUNK, :]
    return ph[s - 1, 8 * a:8 * a + CHUNK, :]


def _conv_chunk(hp, ph, dw_ref, dwb_ref, c):
    _build_phases(hp, ph, c)
    acc = jnp.zeros((CHUNK, CONV_W), F32) + dwb_ref[...]
    for j in range(CONV_K):
        acc = acc + dw_ref[j:j + 1, :] * _window(hp, ph, c, j + 1)
    return acc


def _fill_glu(cp_ref, hp, n):
    hp[0:PAD, :] = jnp.zeros((PAD, CONV_W), F32)
    hp[PAD + n:2 * PAD + n, :] = jnp.zeros((PAD, CONV_W), F32)
    for c in range(n // CHUNK):
        rows = slice(c * CHUNK, (c + 1) * CHUNK)
        a = cp_ref[rows, 0:CONV_W].astype(F32)
        g = cp_ref[rows, CONV_W:2 * CONV_W].astype(F32)
        hp[PAD + c * CHUNK:PAD + (c + 1) * CHUNK, :] = a * _sigmoid(g)


def _fill_pool(cp_ref, pp, n):
    pp[0:PAD, :] = jnp.zeros((PAD, POOL_W), F32)
    pp[PAD + n:2 * PAD + n, :] = jnp.zeros((PAD, POOL_W), F32)
    for c in range(n // CHUNK):
        pp[PAD + c * CHUNK:PAD + (c + 1) * CHUNK, :] = cp_ref[c * CHUNK:(c + 1) * CHUNK, 2 * CONV_W:768].astype(F32)


def _pool_chunk(pp, ph, n, c):
    wl, wr, cnt = _pool_geometry(n, c)
    _build_phases(pp, ph, c)
    acc = jnp.zeros((CHUNK, POOL_W), F32)
    for o in range(-8, 8):
        acc = acc + jnp.where(jnp.logical_and(o >= -wl, o <= wr), _window(pp, ph, c, PAD + o), 0.0)
    return acc / cnt - pp[PAD + c * CHUNK:PAD + (c + 1) * CHUNK, :], cnt


def _seq_specs(n, blk_off, width, col=0):
    return pl.BlockSpec((n, width), lambda s: (blk_off + s, col))


def _full(shape):
    return pl.BlockSpec(shape, lambda s: (0,) * len(shape))


_PHASES = pltpu.VMEM((7, CHUNK + 24, CONV_W), F32)


def _convpool_fwd(name, cpin, mix, yconv, prm, n, blk_off, cfg):
    dw, dwb, lng, lnb, wbd, ps = prm
    n_alias = 1 if yconv is None else 2

    def body(*refs):
        cp_ref, dw_ref, dwb_ref, lng_ref, lnb_ref, wbd_ref, ps_ref = refs[:7]
        out_ref, y_ref, hp, pp, ph = refs[7 + n_alias:]
        _fill_glu(cp_ref, hp, n)
        _fill_pool(cp_ref, pp, n)
        for c in range(n // CHUNK):
            rows = slice(c * CHUNK, (c + 1) * CHUNK)
            y = _conv_chunk(hp, ph, dw_ref, dwb_ref, c)
            y_ref[rows, :] = y
            d = y - jnp.mean(y, axis=-1, keepdims=True)
            hn = d * lax.rsqrt(jnp.mean(d * d, axis=-1, keepdims=True) + EPS) * lng_ref[...] + lnb_ref[...]
            out_ref[rows, 0:CONV_W] = (hn * _sigmoid(hn)).astype(BF16)
            yp, _ = _pool_chunk(pp, ph, n, c)
            out_ref[rows, CONV_W:2 * CONV_W] = (_dot(yp, wbd_ref[...]) * ps_ref[...]).astype(BF16)

    through = [mix] if yconv is None else [mix, yconv]
    return pl.pallas_call(
        body, name=name, grid=(cfg.b,),
        in_specs=[_seq_specs(n, blk_off, 768), _full((32, CONV_W)), _full((1, CONV_W)), _full((1, CONV_W)),
                  _full((1, CONV_W)), _full((POOL_W, POOL_W)), _full((1, POOL_W))] + [_ANY] * n_alias,
        out_specs=[_seq_specs(n, blk_off, 512, 1), _seq_specs(n, blk_off, CONV_W)],
        out_shape=[jax.ShapeDtypeStruct((cfg.r, D), BF16), jax.ShapeDtypeStruct((cfg.r, CONV_W), F32)],
        scratch_shapes=[pltpu.VMEM((n + 2 * PAD, CONV_W), F32), pltpu.VMEM((n + 2 * PAD, POOL_W), F32), _PHASES],
        input_output_aliases={7 + i: i for i in range(n_alias)},
        compiler_params=_params(1))(cpin, dw, dwb, lng, lnb, wbd, ps, *through)


_SMALL_SHAPES = [(32, CONV_W), (1, CONV_W), (1, CONV_W), (1, CONV_W), (POOL_W, POOL_W), (1, POOL_W)]


def _convpool_bwd(name, cpin, yconv, dmix, prm, acc_in, n, blk_off, cfg):
    dw, dwb, lng, lnb, wbd, ps = prm
    nch = n // CHUNK

    def body(cp_ref, y_ref, dm_ref, dw_ref, dwb_ref, lng_ref, lnb_ref, wbd_ref, ps_ref, dcp_in,
             a_dw, a_dwb, a_lng, a_lnb, a_wbd, a_ps,
             dcp_ref, o_dw, o_dwb, o_lng, o_lnb, o_wbd, o_ps, hp, dyp, pp, wp, dyv, dwacc, ph):
        s = pl.program_id(0)

        @pl.when(s == 0)
        def _():
            for o_, a_ in ((o_dw, a_dw), (o_dwb, a_dwb), (o_lng, a_lng), (o_lnb, a_lnb), (o_wbd, a_wbd), (o_ps, a_ps)):
                o_[...] = a_[...]
            dwacc[...] = jnp.zeros_like(dwacc)

        _fill_glu(cp_ref, hp, n)
        _fill_pool(cp_ref, pp, n)
        for ref in (dyp, wp):
            ref[0:PAD, :] = jnp.zeros((PAD, CONV_W), F32)
            ref[PAD + n:2 * PAD + n, :] = jnp.zeros((PAD, CONV_W), F32)
        for c in range(nch):
            rows = slice(c * CHUNK, (c + 1) * CHUNK)
            y = y_ref[rows, :]
            d = y - jnp.mean(y, axis=-1, keepdims=True)
            rstd = lax.rsqrt(jnp.mean(d * d, axis=-1, keepdims=True) + EPS)
            xh = d * rstd
            hn = xh * lng_ref[...] + lnb_ref[...]
            sg = _sigmoid(hn)
            dhn = dm_ref[rows, 0:CONV_W] * (sg * (1.0 + hn * (1.0 - sg)))
            o_lnb[...] += _colsum(dhn)
            o_lng[...] += _colsum(dhn * xh)
            dxh = dhn * lng_ref[...]
            dy = rstd * (dxh - jnp.mean(dxh, axis=-1, keepdims=True) - xh * jnp.mean(dxh * xh, axis=-1, keepdims=True))
            o_dwb[...] += _colsum(dy)
            dyp[PAD + c * CHUNK:PAD + (c + 1) * CHUNK, :] = dy
            _build_phases(hp, ph, c)
            for j in range(CONV_K):
                prod = dy * _window(hp, ph, c, j + 1)
                dwacc[8 * j:8 * j + 8, :] += jnp.sum(prod.reshape(CHUNK // 8, 8, CONV_W), axis=0)
            yp, cnt = _pool_chunk(pp, ph, n, c)
            dz = dm_ref[rows, CONV_W:2 * CONV_W]
            o_ps[...] += _colsum(dz * _dot(yp, wbd_ref[...]))
            dzs = dz * ps_ref[...]
            o_wbd[...] += _dot(yp, dzs, "tn")
            dv = _dot(dzs, wbd_ref[...], "nt")
            dyv[rows, :] = dv
            wp[PAD + c * CHUNK:PAD + (c + 1) * CHUNK, :] = dv / cnt
        for c in range(nch):
            rows = slice(c * CHUNK, (c + 1) * CHUNK)
            _build_phases(dyp, ph, c)
            dh = jnp.zeros((CHUNK, CONV_W), F32)
            for j in range(CONV_K):
                dh = dh + dw_ref[j:j + 1, :] * _window(dyp, ph, c, 31 - j)
            a = cp_ref[rows, 0:CONV_W].astype(F32)
            sg = _sigmoid(cp_ref[rows, CONV_W:2 * CONV_W].astype(F32))
            dcp_ref[rows, 0:CONV_W] = (dh * sg).astype(BF16)
            dcp_ref[rows, CONV_W:2 * CONV_W] = (dh * a * sg * (1.0 - sg)).astype(BF16)
            wl, wr, _ = _pool_geometry(n, c)
            _build_phases(wp, ph, c)
            dp = -dyv[rows, :]
            for o in range(-8, 8):
                dp = dp + jnp.where(jnp.logical_and(o >= -wl, o <= wr), _window(wp, ph, c, PAD - o), 0.0)
            dcp_ref[rows, 2 * CONV_W:768] = dp.astype(BF16)

        @pl.when(s == cfg.b - 1)
        def _():
            for j in range(CONV_K):
                o_dw[j:j + 1, :] += _colsum(dwacc[8 * j:8 * j + 8, :])

    small_specs = [_full(sh) for sh in _SMALL_SHAPES]
    return pl.pallas_call(
        body, name=name, grid=(cfg.b,),
        in_specs=[_seq_specs(n, blk_off, 768), _seq_specs(n, blk_off, CONV_W), _seq_specs(n, blk_off, 512, 1),
                  *small_specs, _ANY, *small_specs],
        out_specs=[_seq_specs(n, blk_off, 768, 1), *small_specs],
        out_shape=[jax.ShapeDtypeStruct((cfg.r, IN_W), BF16)] + [jax.ShapeDtypeStruct(sh, F32) for sh in _SMALL_SHAPES],
        scratch_shapes=[pltpu.VMEM((n + 2 * PAD, CONV_W), F32), pltpu.VMEM((n + 2 * PAD, CONV_W), F32),
                        pltpu.VMEM((n + 2 * PAD, POOL_W), F32), pltpu.VMEM((n + 2 * PAD, POOL_W), F32),
                        pltpu.VMEM((n, POOL_W), F32), pltpu.VMEM((8 * 32, CONV_W), F32), _PHASES],
        input_output_aliases={9: 0}, compiler_params=_params(1))(cpin, yconv, dmix, dw, dwb, lng, lnb, wbd, ps, *acc_in)


def _place_kv(name, du, dkvl, dkvc, with_ctx, cfg):
    nlb = cfg.nl // TM

    def body(l_ref, c_ref, du_in, o_ref):
        i = pl.program_id(0)
        o_ref[...] = jnp.where(i < nlb, l_ref[...], c_ref[...])

    return pl.pallas_call(
        body, name=name, grid=(cfg.r // TM if with_ctx else nlb,),
        in_specs=[pl.BlockSpec((TM, 256), lambda i: (jnp.minimum(i, nlb - 1), 0)),
                  pl.BlockSpec((TM, 256), lambda i: (jnp.maximum(i - nlb, 0), 0)), _ANY],
        out_specs=pl.BlockSpec((TM, 256), lambda i: (i, 2)), out_shape=jax.ShapeDtypeStruct((cfg.r, IN_W), BF16),
        input_output_aliases={2: 0}, compiler_params=_params(1))(dkvl, dkvc, du)


def _place_ctx_kv_only(name, du, dkvc, cfg):
    nlb = cfg.nl // TM

    def body(c_ref, du_in, o_ref):
        o_ref[...] = jnp.zeros_like(o_ref)
        o_ref[:, ATTN_W:ATTN_W + 256] = c_ref[...]

    return pl.pallas_call(
        body, name=name, grid=(cfg.nc // TM,), in_specs=[pl.BlockSpec((TM, 256), lambda i: (i, 0)), _ANY],
        out_specs=pl.BlockSpec((TM, IN_W), lambda i: (nlb + i, 0)), out_shape=jax.ShapeDtypeStruct((cfg.r, IN_W), BF16),
        input_output_aliases={1: 0}, compiler_params=_params(1))(dkvc, du)


def _ffn_in(name, h, w_ffn_in, layer, nblk, cfg, ride=None):
    def body(h_ref, wg_ref, wu_ref, gu_ref, act_ref):
        hv = h_ref[...]
        g = _dot(hv, wg_ref[...])
        u = _dot(hv, wu_ref[...])
        gu_ref[0] = g.astype(BF16)
        gu_ref[1] = u.astype(BF16)
        act_ref[...] = (g * _sigmoid(g) * u).astype(BF16)

    wspec = lambda base: pl.BlockSpec((None, None, D, HALF_FF), lambda j, i: (layer, base + j, 0, 0))
    return _hosted_call(
        body, ride, name, (2, nblk), [h, w_ffn_in, w_ffn_in],
        [pl.BlockSpec((TM, D), lambda j, i: (i, 0)), wspec(0), wspec(2)],
        [jax.ShapeDtypeStruct((2, cfg.r, D_FF), BF16), jax.ShapeDtypeStruct((cfg.r, D_FF), BF16)],
        [pl.BlockSpec((2, TM, HALF_FF), lambda j, i: (0, i, j)), pl.BlockSpec((TM, HALF_FF), lambda j, i: (i, j))],
        [], _params(2))


def _row_block(rows, cols, max_bytes=1 << 20):
    best = 16
    for t in range(16, rows + 1, 16):
        if rows % t == 0 and t * cols * 4 <= max_bytes:
            best = t
    assert rows % best == 0
    return best


def _pair_add(name, own32, recv, c_idx):
    _, _, s0, s1 = own32.shape
    tr = _row_block(s0, s1)

    def body(c_ref, a_ref, b_ref, o_ref):
        o_ref[...] = (a_ref[...].astype(F32) + b_ref[...].astype(F32)).astype(BF16)

    grid_spec = pltpu.PrefetchScalarGridSpec(
        num_scalar_prefetch=1, grid=(N_SHARD * s0 // tr,),
        in_specs=[pl.BlockSpec((None, tr, s1), lambda i, c: (c[0], i, 0)), pl.BlockSpec((tr, s1), lambda i, c: (i, 0))],
        out_specs=pl.BlockSpec((tr, s1), lambda i, c: (i, 0)))
    out = pl.pallas_call(body, name=name, grid_spec=grid_spec, out_shape=jax.ShapeDtypeStruct((N_SHARD * s0, s1), BF16),
                         compiler_params=_params(1))(c_idx, own32.reshape(2, N_SHARD * s0, s1), recv.reshape(N_SHARD * s0, s1))
    return out.reshape(N_SHARD, s0, s1)


def _shard_sum(name, pair_sum, recv, jc_idx):
    _, s0, s1 = pair_sum.shape
    tr = _row_block(s0, s1)

    def body(jc_ref, a_ref, b_ref, o_ref):
        o_ref[...] = ((a_ref[...].astype(F32) + b_ref[0].astype(F32)) + b_ref[1].astype(F32)) + b_ref[2].astype(F32)

    grid_spec = pltpu.PrefetchScalarGridSpec(
        num_scalar_prefetch=1, grid=(s0 // tr,),
        in_specs=[pl.BlockSpec((None, tr, s1), lambda i, jc: (jc[0], i, 0)), pl.BlockSpec((3, tr, s1), lambda i, jc: (0, i, 0))],
        out_specs=pl.BlockSpec((None, tr, s1), lambda i, jc: (jc[1], i, 0)))
    return pl.pallas_call(body, name=name, grid_spec=grid_spec, out_shape=jax.ShapeDtypeStruct((2, s0, s1), F32),
                          compiler_params=_params(1))(jc_idx, pair_sum, recv)


def _adamw_math(w, g, m, v):
    m = ADAM_B1 * m + (1.0 - ADAM_B1) * g
    v = ADAM_B2 * v + (1.0 - ADAM_B2) * (g * g)
    m_hat = m / (1.0 - ADAM_B1 ** ADAM_STEP)
    v_hat = v / (1.0 - ADAM_B2 ** ADAM_STEP)
    delta = -ADAM_LR * (m_hat / (jnp.sqrt(v_hat) + ADAM_EPS) + ADAM_WD * w)
    return delta, m, v


def _adamw(name, w, g, m, v):
    rows, cols = w.shape
    tr = rows if rows % 16 else _row_block(rows, cols, 1 << 19)

    def body(w_ref, g_ref, m_ref, v_ref, d_ref, mo_ref, vo_ref):
        d, mn, vn = _adamw_math(w_ref[...], g_ref[...], m_ref[...], v_ref[...])
        d_ref[...] = d
        mo_ref[...] = mn
        vo_ref[...] = vn

    spec = pl.BlockSpec((tr, cols), lambda i: (i, 0))
    shape = jax.ShapeDtypeStruct((rows, cols), F32)
    return pl.pallas_call(body, name=name, grid=(rows // tr,), in_specs=[spec] * 4, out_specs=[spec] * 3,
                          out_shape=[shape] * 3, compiler_params=_params(1))(w, g, m, v)


def _adamw_layers(name, w, g_layers, m, v):
    rows, cols = w.shape
    s0 = rows // 2
    tr = _row_block(s0, cols, 1 << 19)
    nb = s0 // tr

    def body(w_ref, g0_ref, g1_ref, m_ref, v_ref, g_ref, d_ref, mo_ref, vo_ref):
        g = jnp.where(pl.program_id(0) < nb, g0_ref[...], g1_ref[...])
        d, mn, vn = _adamw_math(w_ref[...], g, m_ref[...], v_ref[...])
        g_ref[...] = g
        d_ref[...] = d
        mo_ref[...] = mn
        vo_ref[...] = vn

    spec = pl.BlockSpec((tr, cols), lambda i: (i, 0))
    shape = jax.ShapeDtypeStruct((rows, cols), F32)
    return pl.pallas_call(
        body, name=name, grid=(2 * nb,),
        in_specs=[spec, pl.BlockSpec((tr, cols), lambda i: (jnp.minimum(i, nb - 1), 0)),
                  pl.BlockSpec((tr, cols), lambda i: (jnp.maximum(i - nb, 0), 0)), spec, spec],
        out_specs=[spec] * 4, out_shape=[shape] * 4, compiler_params=_params(1))(w, g_layers[0], g_layers[1], m, v)


def _position():
    return lax.axis_index("x"), lax.axis_index("y"), lax.axis_index("c")


def _other_chips(x, y):
    return [(1 - x, y), (x, 1 - y), (1 - x, 1 - y)]


def _run_ride(name, ride):
    n_in, n_out = len(ride.ins), len(ride.out_shape)

    def body(*refs):
        parts = (refs[:n_in], refs[n_in:n_in + n_out], refs[n_in + n_out:])
        ride.start(*parts)
        ride.finish(*parts)

    return pl.pallas_call(
        body, name=name, in_specs=[_ANY] * n_in, out_specs=[_ANY] * n_out, out_shape=ride.out_shape,
        scratch_shapes=ride.scratch, compiler_params=pltpu.CompilerParams(vmem_limit_bytes=VMEM_LIMIT_V7X))(*ride.ins)


def _gather_ride(shards):
    n = len(shards)

    def copies(ins, outs, scr):
        ssem, rsem = scr[n], scr[n + 1]
        x, y, c = _position()
        me, sibling = 2 * x + y, (x, y, 1 - c)

        def remote(src, dst, i, dev):
            return pltpu.make_async_remote_copy(src, dst, ssem.at[i], rsem.at[i], device_id=dev, device_id_type=_MESH)

        fetch_out, fetch_in, pass_out, pass_in = [], [], [], []
        for a, (src, dst) in enumerate(zip(ins, outs)):
            for k, (px, py) in enumerate(_other_chips(x, y)):
                j, i1, i2 = 2 * px + py, 3 * a + k, 3 * n + 3 * a + k
                fetch_out.append(remote(src.at[c], dst.at[me, c], i1, (px, py, c)))
                fetch_in.append(remote(src.at[c], dst.at[j, c], i1, (px, py, c)))
                pass_out.append(remote(dst.at[j, c], dst.at[j, c], i2, sibling))
                pass_in.append(remote(dst.at[j, 1 - c], dst.at[j, 1 - c], i2, sibling))
        return me, fetch_out, fetch_in, pass_out, pass_in

    def start(ins, outs, scr):
        bufs, lsem = scr[:n], scr[n + 2]
        me, fetch_out, _, _, _ = copies(ins, outs, scr)
        for cp in fetch_out:
            cp.start()
        loads = []
        for a, (src, buf) in enumerate(zip(ins, bufs)):
            ld = pltpu.make_async_copy(src, buf, lsem.at[2 * a])
            ld.start()
            loads.append(ld)
        for a, (ld, buf, dst) in enumerate(zip(loads, bufs, outs)):
            ld.wait()
            st = pltpu.make_async_copy(buf, dst.at[me], lsem.at[2 * a + 1])
            st.start()
            st.wait()

    def finish(ins, outs, scr):
        _, fetch_out, fetch_in, pass_out, pass_in = copies(ins, outs, scr)
        for arrived, onward in zip(fetch_in, pass_out):
            arrived.wait_recv()
            onward.start()
        for cp in pass_in:
            cp.wait_recv()
        for cp in fetch_out + pass_out:
            cp.wait_send()

    return _Ride(list(shards), [jax.ShapeDtypeStruct((N_SHARD,) + s.shape, s.dtype) for s in shards],
                 [pltpu.VMEM(s.shape, s.dtype) for s in shards]
                 + [pltpu.SemaphoreType.DMA((6 * n,)), pltpu.SemaphoreType.DMA((6 * n,)), pltpu.SemaphoreType.DMA((2 * n,))],
                 start, finish)


def _comm(name, ins, out_shape, n_remote, plan):
    n_in, n_out = len(ins), len(out_shape)

    def body(*refs):
        plan(refs[:n_in], refs[n_in:n_in + n_out], *refs[n_in + n_out:])

    return pl.pallas_call(
        body, name=name, in_specs=[_ANY] * n_in, out_specs=[_ANY] * n_out, out_shape=out_shape,
        scratch_shapes=[pltpu.SemaphoreType.DMA((n_remote,)), pltpu.SemaphoreType.DMA((n_remote,))])(*ins)


def _send_other_half(name, grads_bf):
    n = len(grads_bf)

    def plan(ins, outs, ssem, rsem):
        x, y, c = _position()
        started = []
        for a, (src, dst) in enumerate(zip(ins, outs)):
            cp = pltpu.make_async_remote_copy(src.at[1 - c], dst, ssem.at[a], rsem.at[a], device_id=(x, y, 1 - c),
                                              device_id_type=_MESH)
            cp.start()
            started.append(cp)
        for cp in started:
            cp.wait_recv()
        for cp in started:
            cp.wait_send()

    shapes = [jax.ShapeDtypeStruct(s.shape[1:], s.dtype) for s in grads_bf]
    return _comm(name, grads_bf, shapes, n, plan)


def _exchange_ride(pair_sums):
    n = len(pair_sums)

    def copies(ins, outs, scr):
        ssem, rsem = scr
        x, y, c = _position()
        return [pltpu.make_async_remote_copy(src.at[2 * px + py], dst.at[k], ssem.at[3 * a + k], rsem.at[3 * a + k],
                                             device_id=(px, py, c), device_id_type=_MESH)
                for a, (src, dst) in enumerate(zip(ins, outs)) for k, (px, py) in enumerate(_other_chips(x, y))]

    def start(ins, outs, scr):
        for cp in copies(ins, outs, scr):
            cp.start()

    def finish(ins, outs, scr):
        for cp in copies(ins, outs, scr):
            cp.wait_recv()
        for cp in copies(ins, outs, scr):
            cp.wait_send()

    return _Ride(list(pair_sums), [jax.ShapeDtypeStruct((3,) + s.shape[1:], s.dtype) for s in pair_sums],
                 [pltpu.SemaphoreType.DMA((3 * n,)), pltpu.SemaphoreType.DMA((3 * n,))], start, finish)


def _swap_reduced(name, grads):
    n = len(grads)

    def body(*refs):
        ins, outs, ssem, rsem = refs[:n], refs[n:2 * n], refs[2 * n], refs[2 * n + 1]
        x, y, c = _position()
        sent = []
        for a, (src, dst) in enumerate(zip(ins, outs)):
            cp = pltpu.make_async_remote_copy(src.at[c], dst.at[c], ssem.at[a], rsem.at[a], device_id=(x, y, 1 - c),
                                              device_id_type=_MESH)
            cp.start()
            sent.append(cp)
        for a, (src, dst) in enumerate(zip(ins, outs)):
            pltpu.make_async_remote_copy(src.at[1 - c], dst.at[1 - c], ssem.at[a], rsem.at[a], device_id=(x, y, 1 - c),
                                         device_id_type=_MESH).wait_recv()
        for cp in sent:
            cp.wait_send()

    return pl.pallas_call(
        body, name=name, in_specs=[_ANY] * n, out_specs=[_ANY] * n,
        out_shape=[jax.ShapeDtypeStruct(g.shape, g.dtype) for g in grads],
        scratch_shapes=[pltpu.SemaphoreType.DMA((n,)), pltpu.SemaphoreType.DMA((n,))],
        input_output_aliases={a: a for a in range(n)})(*grads)


_FLIPS = [(dx, dy, dc) for dx in (0, 1) for dy in (0, 1) for dc in (0, 1) if dx + dy + dc]
_VMEM = pl.BlockSpec(memory_space=pltpu.VMEM)


def _to_all(src_of, dst, ssem, rsem):
    x, y, c = _position()
    me = 4 * x + 2 * y + c
    peers = [((x + dx) % 2, (y + dy) % 2, (c + dc) % 2) for dx, dy, dc in _FLIPS]
    sent = []
    for k, (px, py, pc) in enumerate(peers):
        cp = pltpu.make_async_remote_copy(src_of(2 * px + py), dst.at[me], ssem.at[k], rsem.at[k],
                                          device_id=(px, py, pc), device_id_type=_MESH)
        cp.start()
        sent.append(cp)
    for k, (px, py, pc) in enumerate(peers):
        pltpu.make_async_remote_copy(src_of(2 * px + py), dst.at[4 * px + 2 * py + pc], ssem.at[k], rsem.at[k],
                                     device_id=(px, py, pc), device_id_type=_MESH).wait_recv()
    for cp in sent:
        cp.wait_send()
    return me, 2 * x + y


def _share_small(name, block, total):
    shape = block.shape

    def body(in_ref, out_ref, *scratch):
        buf, ssem, rsem = (out_ref,) + scratch if not total else scratch
        me, _ = _to_all(lambda chip: in_ref, buf, ssem, rsem)
        buf[me] = in_ref[...]
        if total:
            acc = buf[0]
            for d in range(1, 8):
                acc = acc + buf[d]
            out_ref[...] = acc

    sems = [pltpu.SemaphoreType.DMA((7,)), pltpu.SemaphoreType.DMA((7,))]
    return pl.pallas_call(
        body, name=name, in_specs=[_VMEM], out_specs=_VMEM,
        out_shape=jax.ShapeDtypeStruct(shape if total else (8,) + shape, F32),
        scratch_shapes=([pltpu.VMEM((8,) + shape, F32)] if total else []) + sems,
        compiler_params=pltpu.CompilerParams(vmem_limit_bytes=VMEM_LIMIT_V7X))(block)


def _mod_rows_exchange(mv):
    def body(mv_ref, out_ref, ssem, rsem):
        x, y, c = _position()
        me = 2 * x + y
        out_ref[me] = mv_ref[4 * x + 2 * y + c]
        sent = []
        for k, (px, py) in enumerate(_other_chips(x, y)):
            cp = pltpu.make_async_remote_copy(mv_ref.at[4 * px + 2 * py + c], out_ref.at[me], ssem.at[k], rsem.at[k],
                                              device_id=(px, py, c), device_id_type=_MESH)
            cp.start()
            sent.append(cp)
        for k, (px, py) in enumerate(_other_chips(x, y)):
            pltpu.make_async_remote_copy(mv_ref.at[0], out_ref.at[2 * px + py], ssem.at[k], rsem.at[k],
                                         device_id=(px, py, c), device_id_type=_MESH).wait_recv()
        for cp in sent:
            cp.wait_send()

    return pl.pallas_call(
        body, name="mod_rows_exchange", in_specs=[_VMEM], out_specs=_VMEM,
        out_shape=jax.ShapeDtypeStruct((N_SHARD,) + mv.shape[1:], F32),
        scratch_shapes=[pltpu.SemaphoreType.DMA((3,)), pltpu.SemaphoreType.DMA((3,))],
        compiler_params=pltpu.CompilerParams(vmem_limit_bytes=VMEM_LIMIT_V7X))(mv)


def _mod_grad_exchange(dmj, dm_rows):
    def body(dmj_ref, rows_ref, out_ref, bias_ref, ssem, rsem):
        me, chip = _to_all(lambda j: dmj_ref.at[j], out_ref, ssem, rsem)
        out_ref[me] = dmj_ref[chip]
        for l in range(2):
            bias_ref[l] = _colsum(rows_ref[l])

    return pl.pallas_call(
        body, name="mod_grad_exchange", in_specs=[_VMEM, _VMEM], out_specs=[_VMEM, _VMEM],
        out_shape=[jax.ShapeDtypeStruct((8,) + dmj.shape[1:], F32), jax.ShapeDtypeStruct((2, 1, dm_rows.shape[-1]), F32)],
        scratch_shapes=[pltpu.SemaphoreType.DMA((7,)), pltpu.SemaphoreType.DMA((7,))],
        compiler_params=pltpu.CompilerParams(vmem_limit_bytes=VMEM_LIMIT_V7X))(dmj, dm_rows)


def _pack(arrays):
    flat = jnp.concatenate([a.reshape(-1).astype(F32) for a in arrays])
    total = flat.shape[0]
    rows = -(-total // (8 * PACK_LANES)) * 8
    return jnp.pad(flat, (0, rows * PACK_LANES - total)).reshape(rows, PACK_LANES)


def _unpack(pack, shapes):
    flat, out, pos = pack.reshape(-1), [], 0
    for sh in shapes:
        size = int(np.prod(sh)) if len(sh) else 1
        out.append(flat[pos:pos + size].reshape(sh))
        pos += size
    return out


def _block_diag(pw):
    out = jnp.zeros((POOL_W, POOL_W), pw.dtype)
    for g in range(4):
        out = out.at[g * 64:(g + 1) * 64, g * 64:(g + 1) * 64].set(pw[g])
    return out


def _local_step(x, ctx, small, mod3s, loss_target, comm):
    cfg = _Cfg(x.shape[0], x.shape[1], ctx.shape[1])
    assert cfg.seq % TM == 0 and cfg.nc % TM == 0 and cfg.seq % cfg.ctx == 0 and cfg.ctx % CHUNK == 0
    nb_all, nb_lat = cfg.r // TM, cfg.nl // TM
    last = 1
    wf, big = comm.wf, comm.grads
    cos_t, sin_t = _rope_tables(cfg.seq)
    xs = jnp.concatenate([x.reshape(cfg.nl, D), ctx.reshape(cfg.nc, D)], axis=0)
    row = lambda w: pl.BlockSpec((TM, w), lambda i, j, k: (i, 0))
    mod3_spec = pl.BlockSpec((None, 6, D), lambda i, j, k: (cfg.mod_row(i), 0, 0))
    whole = lambda rows: pl.BlockSpec((None, rows, D), lambda i, j, k: (0, 0, 0))

    def conv_params(l):
        dw = jnp.pad(wf[l]["conv_dw"], ((0, 1), (0, 0)))
        return (dw, small["conv_dw_b"][l][None], small["conv_ln_g"][l][None], small["conv_ln_b"][l][None],
                _block_diag(small["pool_w"][l]).astype(BF16), small["pool_scale"][l][None])

    def residual_epi(ig):
        def epi(acc, ex, outs):
            x_ref, m_ref = ex
            outs[0][...] = x_ref[...] + m_ref[ig:ig + 1, :] * acc
            outs[1][...] = acc.astype(BF16)
        return epi

    def hosted(name, call):
        outs, got = call(comm.ride(name))
        comm.landed(name, got)
        return outs

    saved = []
    for l in range(2):
        nb = nb_lat if l == last else nb_all
        wl = wf[l]
        mod3 = mod3s[l]
        h1 = _norm_fwd(f"norm1_fwd{l}", xs, small["norm1_g"][l][None], mod3, 0, 1, nb_all, cfg)
        qkv, cpin = _in_proj(f"in_proj{l}", h1, wl["w_in"], 0, cos_t, sin_t, cfg)
        mix, lse = hosted(f"att_fwd{l}", lambda ride: _att_fwd(f"att_fwd{l}", qkv, small["attn_sink"][l], l != last, cfg, ride))
        prm = conv_params(l)
        mix, yconv = _convpool_fwd(f"convpool_fwd_lat{l}", cpin, mix, None, prm, cfg.seq, 0, cfg)
        if l != last:
            mix, yconv = _convpool_fwd(f"convpool_fwd_ctx{l}", cpin, mix, yconv, prm, cfg.ctx, cfg.nl // cfg.ctx, cfg)
        x1, y1 = _mm(f"out_proj{l}", "nn", (nb, 1, 1), mix, wl["w_out"].reshape(1, D, D), row(D), whole(D),
                     [jax.ShapeDtypeStruct((cfg.r, D), F32), jax.ShapeDtypeStruct((cfg.r, D), BF16)], [row(D), row(D)],
                     extras=[xs, mod3], extra_specs=[row(D), mod3_spec], epi=residual_epi(2))
        h2 = _norm_fwd(f"norm2_fwd{l}", x1, small["norm2_g"][l][None], mod3, 3, 4, nb, cfg)
        gu, act = hosted(f"ffn_in{l}", lambda ride: _ffn_in(f"ffn_in{l}", h2, wl["w_ffn_in"], 0, nb, cfg, ride))
        x2, y2 = _mm(f"ffn_out{l}", "nn", (nb, 1, 1), act, wl["w_ffn_out"].reshape(1, D_FF, D), row(D_FF), whole(D_FF),
                     [jax.ShapeDtypeStruct((cfg.r, D), F32), jax.ShapeDtypeStruct((cfg.r, D), BF16)], [row(D), row(D)],
                     extras=[x1, mod3], extra_specs=[row(D), mod3_spec], epi=residual_epi(5))
        saved.append(dict(mod3=mod3, x0=xs, h1=h1, qkv=qkv, cpin=cpin, mix=mix, yconv=yconv, lse=lse, y1=y1, x1=x1,
                          h2=h2, gu=gu, act=act, y2=y2, prm=prm))
        xs = x2

    dx, loss, d_final_g = _loss_head(xs, loss_target.reshape(cfg.nl, D), small["final_g"][None], cfg)

    sg = {k: [None, None] for k in ("norm1_g", "norm2_g", "conv_dw", "conv_dw_b", "conv_ln_g", "conv_ln_b",
                                    "attn_sink", "pool_w", "pool_scale")}
    dms = [None, None]

    def swiglu_bwd_epi(acc, ex, outs):
        g = ex[0][0].astype(F32)
        u = ex[0][1].astype(F32)
        s = _sigmoid(g)
        outs[0][0] = (acc * u * (s * (1.0 + g * (1.0 - s)))).astype(BF16)
        outs[0][1] = (acc * (g * s)).astype(BF16)

    def halves_epi(acc, ex, outs):
        h = acc.shape[0] // 2
        outs[0][0] = acc[:h].astype(BF16)
        outs[0][1] = acc[h:].astype(BF16)

    def row_shards_epi(n):
        def epi(acc, ex, outs):
            s0 = acc.shape[0] // n
            h = s0 // 2
            for t in range(n):
                for half in range(2):
                    outs[0][half, t] = acc[t * s0 + half * h:t * s0 + (half + 1) * h].astype(BF16)
        return epi

    def col_shards_epi(acc, ex, outs):
        h = acc.shape[0] // 2
        for j in range(N_SHARD):
            for half in range(2):
                outs[0][half, j] = acc[half * h:(half + 1) * h, j * IN_SHARD:(j + 1) * IN_SHARD].astype(BF16)

    for l in (1, 0):
        sv = saved[l]
        mod3 = sv["mod3"]
        nb = nb_lat if l == last else nb_all
        tr = _dw_rows(nb * TM)
        steps = nb * TM // tr
        wl = wf[l]
        dy2, dg2 = _gate_bwd(f"gate2_bwd{l}", dx, sv["y2"], mod3, 5, nb, cfg)
        gu_spec = pl.BlockSpec((2, TM, HALF_FF), lambda i, j, k: (0, i, j))
        df = _mm(f"ffn_out_bwd{l}", "nt", (nb, 2, 1), dy2, wl["w_ffn_out"].reshape(1, D_FF, D), row(D),
                 pl.BlockSpec((None, HALF_FF, D), lambda i, j, k: (0, j, 0)),
                 [jax.ShapeDtypeStruct((2, cfg.r, D_FF), BF16)], [gu_spec], extras=[sv["gu"]], extra_specs=[gu_spec],
                 epi=swiglu_bwd_epi)[0]
        big[l]["w_ffn_out"] = _mm_dw(
            f"dw_ffn_out{l}", sv["act"], dy2, pl.BlockSpec((tr, HALF_FF), lambda i, j, k: (k, j)),
            pl.BlockSpec((tr, D), lambda i, j, k: (k, 0)), (2, N_SHARD, D_FF // 8, D),
            pl.BlockSpec((2, 2, D_FF // 8, D), lambda i, j, k: (0, j, 0, 0)), 2, (HALF_FF, D), steps, row_shards_epi(2))[0]
        ride = comm.ride(f"dw_ffn_in{l}")
        res = _mm_dw(f"dw_ffn_in{l}", sv["h2"], df, pl.BlockSpec((tr, D), lambda i, j, k: (k, 0)),
                     pl.BlockSpec((None, tr, HALF_FF), lambda i, j, k: (j // 2, k, j % 2)), (2, N_SHARD, D // 2, HALF_FF),
                     pl.BlockSpec((2, None, D // 2, HALF_FF), lambda i, j, k: (0, j, 0, 0)), N_SHARD, (D, HALF_FF), steps,
                     halves_epi, ride=ride)
        res, got = res if ride is not None else (res, [])
        big[l]["w_ffn_in"] = res[0]
        comm.landed(f"dw_ffn_in{l}", got)
        dx1, dsh2, dsc2, dn2 = _proj_norm_bwd(
            f"ffn_in_bwd{l}", df, pl.BlockSpec((2, TM, D_FF), lambda i: (0, i, 0)),
            lambda a_ref, j: a_ref[j // 2, :, (j % 2) * HALF_FF:(j % 2 + 1) * HALF_FF], wl["w_ffn_in"],
            sv["x1"], dx, small["norm2_g"][l][None], mod3, 4, nb, False, cfg)
        dy1, dg1 = _gate_bwd(f"gate1_bwd{l}", dx1, sv["y1"], mod3, 2, nb, cfg)
        dmix = _mm(f"out_proj_bwd{l}", "nt", (nb, 1, 1), dy1, wl["w_out"].reshape(1, D, D), row(D), whole(D),
                   [jax.ShapeDtypeStruct((cfg.r, D), F32)], [row(D)])[0]
        big[l]["w_out"] = _mm_dw(
            f"dw_out{l}", sv["mix"], dy1, pl.BlockSpec((tr, D), lambda i, j, k: (k, 0)),
            pl.BlockSpec((tr, D), lambda i, j, k: (k, 0)), (2, N_SHARD, D // 8, D),
            pl.BlockSpec((2, N_SHARD, D // 8, D), lambda i, j, k: (0, 0, 0, 0)), 1, (D, D), steps, row_shards_epi(N_SHARD))[0]
        du, dkvl, dkvc, dsink = hosted(f"att_bwd{l}", lambda ride: _att_bwd(
            f"att_bwd{l}", sv["qkv"], sv["mix"], dmix, sv["lse"], small["attn_sink"][l], cos_t, sin_t, l != last, cfg, ride))
        acc = [du] + [jnp.zeros(sh, F32) for sh in _SMALL_SHAPES]
        acc = _convpool_bwd(f"convpool_bwd_lat{l}", sv["cpin"], sv["yconv"], dmix, sv["prm"], acc, cfg.seq, 0, cfg)
        if l != last:
            acc = _convpool_bwd(f"convpool_bwd_ctx{l}", sv["cpin"], sv["yconv"], dmix, sv["prm"], acc, cfg.ctx,
                                cfg.nl // cfg.ctx, cfg)
        du, g_dw, g_dwb, g_lng, g_lnb, g_wbd, g_ps = acc
        du = _place_kv(f"place_kv{l}", du, dkvl, dkvc, l != last, cfg)
        if l == last:
            du = _place_ctx_kv_only(f"place_ctx_kv{l}", du, dkvc, cfg)
        sg["attn_sink"][l] = dsink[:, 0]
        sg["conv_dw"][l], sg["conv_dw_b"][l], sg["conv_ln_g"][l], sg["conv_ln_b"][l] = g_dw[:CONV_K], g_dwb[0], g_lng[0], g_lnb[0]
        sg["pool_w"][l] = jnp.stack([g_wbd[g * 64:(g + 1) * 64, g * 64:(g + 1) * 64] for g in range(4)])
        sg["pool_scale"][l] = g_ps[0]
        tr_all = _dw_rows(cfg.r)
        big[l]["w_in"] = _mm_dw(
            f"dw_in{l}", sv["h1"], du, pl.BlockSpec((tr_all, D), lambda i, j, k: (k, 0)),
            pl.BlockSpec((tr_all, IN_W), lambda i, j, k: (k, 0)), (2, N_SHARD, D // 2, IN_SHARD),
            pl.BlockSpec((2, N_SHARD, D // 2, IN_SHARD), lambda i, j, k: (0, 0, 0, 0)), 1, (D, IN_W), cfg.r // tr_all,
            col_shards_epi)[0]
        dx, dsh1, dsc1, dn1 = _proj_norm_bwd(
            f"in_proj_bwd{l}", du, pl.BlockSpec((TM, IN_W), lambda i: (i, 0)),
            lambda a_ref, j: a_ref[:, j * IN_SHARD:(j + 1) * IN_SHARD], wl["w_in"],
            sv["x0"], dx1, small["norm1_g"][l][None], mod3, 1, nb_all, l == last, cfg)
        sg["norm1_g"][l], sg["norm2_g"][l] = dn1[0], dn2[0]
        parts = [dsh1, dsc1, dg1, dsh2, dsc2, dg2]
        dm = jnp.concatenate([t[:cfg.b, 0, :] for t in parts], axis=1)
        live = (0, 1) if l == last else range(6)
        dm_ctx = jnp.concatenate([t[cfg.b, 0, :] if i in live else jnp.zeros((D,), F32) for i, t in enumerate(parts)])
        dms[l] = jnp.concatenate([dm, dm_ctx[None, :], jnp.zeros((MOD_ROWS - cfg.b - 1, 6 * D), F32)], axis=0)

    grad_x = dx[:cfg.nl].reshape(x.shape)
    small_grads = {k: jnp.stack(v) for k, v in sg.items()}
    small_grads["final_g"] = d_final_g[0]
    return loss, grad_x, small_grads, dms


_BIG = ("w_in", "w_out", "w_ffn_in", "w_ffn_out")
_TAPS = "conv_dw"
_SMALL = ("c_ctx", "b_mod", "norm1_g", "norm2_g", "conv_dw", "conv_dw_b", "conv_ln_g", "conv_ln_b", "attn_sink",
          "pool_w", "pool_scale", "final_g")
_ORDER = ("c_ctx", "w_mod", "b_mod", "norm1_g", "norm2_g", "w_in", "conv_dw", "conv_dw_b", "conv_ln_g", "conv_ln_b",
          "attn_sink", "pool_w", "pool_scale", "w_out", "w_ffn_in", "w_ffn_out", "final_g")
_GATHER_HOSTS = {"att_fwd0": (0, ("w_out", "w_ffn_in", "w_ffn_out", _TAPS)), "ffn_in0": (1, _BIG + (_TAPS,))}
_REDUCE_HOSTS = {"dw_ffn_in0": (1, _BIG), "att_bwd0": (0, ("w_ffn_in", "w_ffn_out"))}
_FIRST_GATHER = (0, ("w_in",))
_LAST_REDUCE = (0, ("w_in", "w_out"))


class _Comm:
    def __init__(self, w, c_idx, jc_idx):
        self.shapes = {k: w[k].shape[1:] for k in _BIG}
        self.c_idx, self.jc_idx = c_idx, jc_idx
        halves = lambda a: a.reshape(2, a.shape[0] // 2, a.shape[1])
        taps = jnp.pad(w[_TAPS], ((0, 0), (0, 1), (0, 64)))
        cast = {k: w[k].astype(BF16) for k in _BIG}
        self.shards = [{**{k: halves(cast[k][l]) for k in _BIG}, _TAPS: halves(taps[l])} for l in range(2)]
        self.wf = [dict(), dict()]
        self.grads = [dict(), dict()]
        self.reduced = [dict(), dict()]
        self._open = {}
        self.landed("first", _run_ride("gather_first", self.ride("first")))

    def ride(self, host):
        if host == "first" or host in _GATHER_HOSTS:
            layer, keys = _FIRST_GATHER if host == "first" else _GATHER_HOSTS[host]
            return _gather_ride([self.shards[layer][k] for k in keys])
        if host == "last" or host in _REDUCE_HOSTS:
            layer, keys = _LAST_REDUCE if host == "last" else _REDUCE_HOSTS[host]
            tag = f"{layer}_{keys[0]}"
            mine = [self.grads[layer][k] for k in keys]
            other = _send_other_half(f"send_other_half{tag}", mine)
            pair = [_pair_add(f"pair_add{layer}_{k}", a, b, self.c_idx) for k, a, b in zip(keys, mine, other)]
            self._open[host] = pair
            return _exchange_ride(pair)
        return None

    def landed(self, host, got):
        if host == "first" or host in _GATHER_HOSTS:
            layer, keys = _FIRST_GATHER if host == "first" else _GATHER_HOSTS[host]
            for k, f in zip(keys, got):
                if k == _TAPS:
                    taps = f.reshape(N_SHARD, 32, 128)[:, :CONV_K, :64]
                    self.wf[layer][k] = jnp.transpose(taps, (1, 0, 2)).reshape(CONV_K, CONV_W)
                else:
                    self.wf[layer][k] = f.reshape((1, N_SHARD) + self.shapes[k])
        if host == "last" or host in _REDUCE_HOSTS:
            layer, keys = _LAST_REDUCE if host == "last" else _REDUCE_HOSTS[host]
            mine = [_shard_sum(f"shard_sum{layer}_{k}", a, b, self.jc_idx) for k, a, b in zip(keys, self._open.pop(host), got)]
            for k, g in zip(keys, _swap_reduced(f"swap_reduced{layer}_{keys[0]}", mine)):
                self.reduced[layer][k] = g.reshape(self.shapes[k])

    def finish(self):
        self.landed("last", _run_ride("exchange_last", self.ride("last")))


def _conditioning(c, c_ctx, w_mod, b_mod, chip):
    b = c.shape[0]
    block = jnp.concatenate([c, c_ctx[None, :], jnp.zeros((8 - b - 1, D), F32)], axis=0)
    c_all = _share_small("share_c", block, False).reshape(64, D)
    bias = lax.dynamic_slice_in_dim(b_mod, chip * MOD_W, MOD_W, axis=1)
    full = lambda r, q: pl.BlockSpec((r, q), lambda i, j, k: (0, 0))

    def bias_epi(acc, ex, outs):
        outs[0][...] = acc + ex[0][...]

    mv = [_mm(f"mod_fwd{l}", "nn", (1, 1, 1), c_all, w_mod[l], full(64, D), full(D, MOD_W),
              [jax.ShapeDtypeStruct((64, MOD_W), F32)], [full(64, MOD_W)], extras=[bias[l][None]],
              extra_specs=[full(1, MOD_W)], a_fn=_silu, epi=bias_epi)[0] for l in range(2)]
    by_dev = jnp.transpose(jnp.stack(mv).reshape(2, 8, 8, MOD_W), (1, 0, 2, 3))
    rows = jnp.transpose(_mod_rows_exchange(by_dev), (1, 2, 0, 3)).reshape(2, 8, 6 * D)
    rows = jnp.pad(rows, ((0, 0), (0, MOD_ROWS - 8), (0, 0)))
    return [rows[l].reshape(MOD_ROWS, 6, D) for l in range(2)], c_all


def _conditioning_bwd(dms, c_all, w_mod, b):
    dm = jnp.stack([d[:8] for d in dms])
    by_chip = jnp.transpose(dm.reshape(2, 8, N_SHARD, MOD_W), (2, 0, 1, 3))
    gathered, d_bias = _mod_grad_exchange(by_chip, dm)
    dm_all = jnp.transpose(gathered, (1, 0, 2, 3)).reshape(2, 64, MOD_W)
    full = lambda r, q: pl.BlockSpec((r, q), lambda i, j, k: (0, 0))

    def ctx_rows_epi(acc, ex, outs):
        row = lax.broadcasted_iota(jnp.int32, acc.shape, 0) % 8
        outs[0][...] = _colsum(jnp.where(row == b, acc * _dsilu(ex[0][...]), 0.0))

    g_mod, d_ctx = [], jnp.zeros((D,), F32)
    for l in range(2):
        g_mod.append(_mm(f"dw_mod{l}", "tn", (1, 1, 1), c_all, dm_all[l], full(64, D), full(64, MOD_W),
                         [jax.ShapeDtypeStruct((D, MOD_W), F32)], [full(D, MOD_W)], a_fn=_silu)[0])
        part = _mm(f"mod_bwd{l}", "nt", (1, 1, 1), dm_all[l], w_mod[l], full(64, MOD_W), full(D, MOD_W),
                   [jax.ShapeDtypeStruct((1, D), F32)], [full(1, D)], extras=[c_all], extra_specs=[full(64, D)],
                   epi=ctx_rows_epi)[0]
        d_ctx = d_ctx + part[0]
    return g_mod, d_bias[:, 0, :], d_ctx


def kernel(x, c, ctx, c_ctx, w_mod, b_mod, norm1_g, norm2_g, w_in, conv_dw, conv_dw_b, conv_ln_g, conv_ln_b, attn_sink, pool_w, pool_scale, w_out, w_ffn_in, w_ffn_out, final_g, loss_target, m_c_ctx, m_w_mod, m_b_mod, m_norm1_g, m_norm2_g, m_w_in, m_conv_dw, m_conv_dw_b, m_conv_ln_g, m_conv_ln_b, m_attn_sink, m_pool_w, m_pool_scale, m_w_out, m_w_ffn_in, m_w_ffn_out, m_final_g, v_c_ctx, v_w_mod, v_b_mod, v_norm1_g, v_norm2_g, v_w_in, v_conv_dw, v_conv_dw_b, v_conv_ln_g, v_conv_ln_b, v_attn_sink, v_pool_w, v_pool_scale, v_w_out, v_w_ffn_in, v_w_ffn_out, v_final_g):
    w = dict(c_ctx=c_ctx, w_mod=w_mod, b_mod=b_mod, norm1_g=norm1_g, norm2_g=norm2_g, w_in=w_in, conv_dw=conv_dw,
             conv_dw_b=conv_dw_b, conv_ln_g=conv_ln_g, conv_ln_b=conv_ln_b, attn_sink=attn_sink, pool_w=pool_w,
             pool_scale=pool_scale, w_out=w_out, w_ffn_in=w_ffn_in, w_ffn_out=w_ffn_out, final_g=final_g)
    m = dict(c_ctx=m_c_ctx, w_mod=m_w_mod, b_mod=m_b_mod, norm1_g=m_norm1_g, norm2_g=m_norm2_g, w_in=m_w_in,
             conv_dw=m_conv_dw, conv_dw_b=m_conv_dw_b, conv_ln_g=m_conv_ln_g, conv_ln_b=m_conv_ln_b,
             attn_sink=m_attn_sink, pool_w=m_pool_w, pool_scale=m_pool_scale, w_out=m_w_out, w_ffn_in=m_w_ffn_in,
             w_ffn_out=m_w_ffn_out, final_g=m_final_g)
    v = dict(c_ctx=v_c_ctx, w_mod=v_w_mod, b_mod=v_b_mod, norm1_g=v_norm1_g, norm2_g=v_norm2_g, w_in=v_w_in,
             conv_dw=v_conv_dw, conv_dw_b=v_conv_dw_b, conv_ln_g=v_conv_ln_g, conv_ln_b=v_conv_ln_b,
             attn_sink=v_attn_sink, pool_w=v_pool_w, pool_scale=v_pool_scale, w_out=v_w_out, w_ffn_in=v_w_ffn_in,
             w_ffn_out=v_w_ffn_out, final_g=v_final_g)
    xi, yi, ci = _position()
    chip = 2 * xi + yi
    mod3s, c_all = _conditioning(c, c_ctx, w_mod, b_mod, chip)
    comm = _Comm(w, jnp.reshape(ci, (1,)).astype(jnp.int32), jnp.stack([chip, ci]).astype(jnp.int32))
    small = {k: w[k] for k in _SMALL if k not in ("conv_dw", "c_ctx", "b_mod")}
    loss, grad_x, sgrads, dms = _local_step(x, ctx, small, mod3s, loss_target, comm)
    comm.finish()
    g_mod, sgrads["b_mod"], d_ctx = _conditioning_bwd(dms, c_all, w_mod, c.shape[0])
    sgrads["c_ctx"] = 0.5 * d_ctx

    names = list(_SMALL)
    total = _share_small("sum_small", _pack([loss] + [sgrads[k] for k in names]), True)
    parts = _unpack(total, [()] + [sgrads[k].shape for k in names])
    loss_out = parts[0]
    gsmall = dict(zip(names, parts[1:]))
    gsmall["conv_dw"] = lax.dynamic_slice_in_dim(gsmall["conv_dw"], chip * 64, 64, axis=2)

    grads, delta, new_m, new_v = dict(gsmall), {}, {}, {}
    reduced = [{**comm.reduced[l], "w_mod": g_mod[l]} for l in range(2)]
    for k in ("w_mod",) + _BIG:
        s0, s1 = w[k].shape[1:]
        flat = lambda a: a.reshape(2 * s0, s1)
        g_, d_, m_, v_ = _adamw_layers(f"adamw_{k}", flat(w[k]), [reduced[l][k] for l in range(2)], flat(m[k]), flat(v[k]))
        grads[k], delta[k], new_m[k], new_v[k] = [a.reshape(w[k].shape) for a in (g_, d_, m_, v_)]
    d_, m_, v_ = _adamw("adamw_small", _pack([w[k] for k in names]), _pack([gsmall[k] for k in names]),
                        _pack([m[k] for k in names]), _pack([v[k] for k in names]))
    sshapes = [w[k].shape for k in names]
    for k, a, b, e in zip(names, _unpack(d_, sshapes), _unpack(m_, sshapes), _unpack(v_, sshapes)):
        delta[k], new_m[k], new_v[k] = a, b, e
    return (loss_out, grad_x, *[grads[k] for k in _ORDER], *[delta[k] for k in _ORDER],
            *[new_m[k] for k in _ORDER], *[new_v[k] for k in _ORDER])
```

```python
from typing import NamedTuple

import jax
import jax.numpy as jnp
import numpy as np
from jax import lax
from jax.experimental import pallas as pl
from jax.experimental.pallas import tpu as pltpu

F32 = jnp.float32
BF16 = jnp.bfloat16

D = 1024
GRID_W = 64
HEAD_DIM = 64
N_HEADS = 8
ATTN_W = 512
CONV_W = 256
POOL_W = 256
IN_W = 1536
D_FF = 2816
CONV_K = 31
QB = 128
ROPE_BASE = 10000.0
EPS = 1e-6
NEG = -1e30
N_SHARD = 4
IN_SHARD = IN_W // N_SHARD
HALF_FF = D_FF // 2
MOD_W = 6 * D // N_SHARD
MOD_ROWS = 16
PACK_LANES = 128

ADAM_LR = 0.001
ADAM_B1 = 0.9
ADAM_B2 = 0.999
ADAM_EPS = 1e-08
ADAM_WD = 0.01
ADAM_STEP = 10

VMEM_LIMIT_V7X = 56 * 1024 * 1024
TM = 512
TR_MAX = 1024
CHUNK = 256
PAD = 16

_MESH = pl.DeviceIdType.MESH
_ANY = pl.BlockSpec(memory_space=pl.ANY)
_DIMS = {"nn": (((1,), (0,)), ((), ())), "nt": (((1,), (1,)), ((), ())), "tn": (((0,), (0,)), ((), ()))}


class _Cfg(NamedTuple):
    b: int
    seq: int
    ctx: int

    @property
    def nl(self):
        return self.b * self.seq

    @property
    def nc(self):
        return self.b * self.ctx

    @property
    def r(self):
        return self.nl + self.nc

    def mod_row(self, i):
        return jnp.where(i < self.nl // TM, i // (self.seq // TM), self.b)

    def first_of_row(self, i):
        nlb = self.nl // TM
        return jnp.logical_or(jnp.logical_and(i < nlb, i % (self.seq // TM) == 0), i == nlb)


def _params(n_grid=0):
    sem = ("arbitrary",) * n_grid if n_grid else None
    return pltpu.CompilerParams(dimension_semantics=sem, vmem_limit_bytes=VMEM_LIMIT_V7X)


def _dot(a, b, mode="nn"):
    return lax.dot_general(a.astype(BF16), b.astype(BF16), _DIMS[mode], preferred_element_type=F32)


def _sigmoid(x):
    return 1.0 / (1.0 + jnp.exp(-x))


def _silu(x):
    return x * _sigmoid(x)


def _dsilu(x):
    s = _sigmoid(x)
    return s * (1.0 + x * (1.0 - s))


def _colsum(v):
    return jnp.sum(v, axis=0, keepdims=True)


def _dw_rows(rows):
    return TR_MAX if rows % TR_MAX == 0 else TM


def _epi_store(acc, ex, outs):
    for o in outs:
        o[...] = acc.astype(o.dtype)


class _Ride(NamedTuple):
    ins: list
    out_shape: list
    scratch: list
    start: object
    finish: object


class _Hosted(NamedTuple):
    ride: _Ride
    n_in: int
    n_out: int
    grid: tuple

    def split(self, refs):
        n_ri, n_ro, n_rs = len(self.ride.ins), len(self.ride.out_shape), len(self.ride.scratch)
        r_in = refs[self.n_in:self.n_in + n_ri]
        r_out = refs[self.n_in + n_ri + self.n_out:self.n_in + n_ri + self.n_out + n_ro]
        own = refs[:self.n_in] + refs[self.n_in + n_ri:self.n_in + n_ri + self.n_out] + \
            refs[self.n_in + n_ri + self.n_out + n_ro:len(refs) - n_rs]
        return own, (r_in, r_out, refs[len(refs) - n_rs:])

    def start(self, parts):
        ids = [pl.program_id(d) for d in range(len(self.grid))]
        first = ids[0] == 0
        for i in ids[1:]:
            first = jnp.logical_and(first, i == 0)
        pl.when(first)(lambda: self.ride.start(*parts))

    def finish(self, parts):
        ids = [pl.program_id(d) for d in range(len(self.grid))]
        last = ids[0] == self.grid[0] - 1
        for i, g in zip(ids[1:], self.grid[1:]):
            last = jnp.logical_and(last, i == g - 1)
        pl.when(last)(lambda: self.ride.finish(*parts))


def _hosted_call(body, ride, name, grid, ins, in_specs, out_shape, out_specs, scratch, params):
    if ride is None:
        res = pl.pallas_call(body, name=name, grid=grid, in_specs=in_specs, out_specs=out_specs, out_shape=out_shape,
                             scratch_shapes=scratch, compiler_params=params)(*ins)
        return list(res), []
    host = _Hosted(ride, len(ins), len(out_shape), tuple(grid))

    def hosted(*refs):
        own, parts = host.split(refs)
        host.start(parts)
        body(*own)
        host.finish(parts)

    res = pl.pallas_call(
        hosted, name=name, grid=grid, in_specs=list(in_specs) + [_ANY] * len(ride.ins),
        out_specs=list(out_specs) + [_ANY] * len(ride.out_shape), out_shape=list(out_shape) + list(ride.out_shape),
        scratch_shapes=list(scratch) + list(ride.scratch), compiler_params=params)(*ins, *ride.ins)
    return list(res[:len(out_shape)]), list(res[len(out_shape):])


def _mm(name, mode, grid, a, b, a_spec, b_spec, out_shape, out_specs, acc_shape=None, extras=(),
        extra_specs=(), a_fn=None, epi=_epi_store, ride=None):
    nk = grid[2]
    n_ex, n_out = len(extras), len(out_shape)

    def body(*refs):
        a_ref, b_ref = refs[:2]
        ex = refs[2:2 + n_ex]
        outs = refs[2 + n_ex:2 + n_ex + n_out]
        av = a_ref[...]
        if a_fn is not None:
            av = a_fn(av)
        part = _dot(av, b_ref[...], mode)
        if nk == 1:
            epi(part, ex, outs)
        else:
            acc = refs[-1]
            k = pl.program_id(2)

            @pl.when(k == 0)
            def _():
                acc[...] = part

            @pl.when(k > 0)
            def _():
                acc[...] += part

            @pl.when(k == nk - 1)
            def _():
                epi(acc[...], ex, outs)

    scratch = [] if nk == 1 else [pltpu.VMEM(acc_shape, F32)]
    outs, ride_outs = _hosted_call(body, ride, name, grid, [a, b, *extras], [a_spec, b_spec, *extra_specs],
                                   list(out_shape), list(out_specs), scratch, _params(3))
    return outs if ride is None else (outs, ride_outs)


def _mm_dw(name, a, b, a_spec, b_spec, out_shape, out_spec, n_out_blocks, acc_shape, n_steps, epi, a_fn=None,
           extras=(), extra_specs=(), extra_out=(), extra_out_specs=(), ride=None):
    grid = (1, n_out_blocks, n_steps)
    outs = [jax.ShapeDtypeStruct(out_shape, BF16)] + list(extra_out)
    return _mm(name, "tn", grid, a, b, a_spec, b_spec, outs, [out_spec, *extra_out_specs], acc_shape, extras,
               extra_specs, a_fn, epi, ride)


def _gate_step(i, dxv, y_ref, m_ref, ig, dy_ref, dgate_ref, cfg):
    dy_ref[...] = (dxv * m_ref[ig:ig + 1, :]).astype(BF16)
    _accumulate_rows(cfg.first_of_row(i), dgate_ref, _colsum(dxv * y_ref[...].astype(F32)))


def _proj_norm_bwd(name, a, a_spec, pick, w, x, dres, gvec, mod3, isc, nblk, res_latent_only, gate, cfg):
    ns = w.shape[-1]
    nlb = cfg.nl // TM

    def body(a_ref, w_ref, x_ref, dres_ref, g_ref, m_ref, *rest):
        if gate is None:
            dx_ref, dsh_ref, dsc_ref, dg_ref = rest
        else:
            y_ref, gm_ref, dx_ref, dsh_ref, dsc_ref, dg_ref, dy_ref, dgate_ref = rest
        i = pl.program_id(0)
        dhv = _dot(pick(a_ref, 0), w_ref[0], "nt")
        for j in range(1, N_SHARD):
            dhv = dhv + _dot(pick(a_ref, j), w_ref[j], "nt")
        xv = x_ref[...]
        r = lax.rsqrt(jnp.mean(xv * xv, axis=-1, keepdims=True) + EPS)
        xh = xv * r
        g = g_ref[...]
        sc1 = 1.0 + m_ref[isc:isc + 1, :]
        t = dhv * xh
        first = cfg.first_of_row(i)
        _accumulate_rows(first, dsh_ref, _colsum(dhv))
        _accumulate_rows(first, dsc_ref, _colsum(t * g))
        _accumulate_rows(i == 0, dg_ref, _colsum(t * sc1))
        dxh = dhv * (g * sc1)
        dxn = r * (dxh - xh * jnp.mean(dxh * xh, axis=-1, keepdims=True))
        dxv = (jnp.where(i < nlb, dres_ref[...], 0.0) if res_latent_only else dres_ref[...]) + dxn
        dx_ref[...] = dxv
        if gate is not None:
            _gate_step(i, dxv, y_ref, gm_ref, gate[2], dy_ref, dgate_ref, cfg)

    row = pl.BlockSpec((TM, D), lambda i: (i, 0))
    vec = pl.BlockSpec((1, D), lambda i: (0, 0))
    part = pl.BlockSpec((None, 1, D), lambda i: (cfg.mod_row(i), 0, 0))
    part_shape = jax.ShapeDtypeStruct((MOD_ROWS, 1, D), F32)
    resident = pl.BlockSpec((None, N_SHARD, D, ns), lambda i: (0, 0, 0, 0), pipeline_mode=pl.Buffered(1))
    ins, in_specs = [a, w, x, dres, gvec, mod3], [a_spec, resident, row, row, vec, _mod_spec(cfg)]
    out_specs = [row, part, part, vec]
    out_shape = [jax.ShapeDtypeStruct((cfg.r, D), F32), part_shape, part_shape, jax.ShapeDtypeStruct((1, D), F32)]
    if gate is not None:
        ins, in_specs = ins + list(gate[:2]), in_specs + [row, _mod_spec(cfg)]
        out_specs, out_shape = out_specs + [row, part], out_shape + [jax.ShapeDtypeStruct((cfg.r, D), BF16), part_shape]
    return pl.pallas_call(body, name=name, grid=(nblk,), in_specs=in_specs, out_specs=out_specs, out_shape=out_shape,
                          compiler_params=_params(1))(*ins)


def _mod_spec(cfg):
    return pl.BlockSpec((None, 6, D), lambda i: (cfg.mod_row(i), 0, 0))


def _norm_mod(xv, g, m_ref, ish, isc):
    r = lax.rsqrt(jnp.mean(xv * xv, axis=-1, keepdims=True) + EPS)
    return (xv * r * g * (1.0 + m_ref[isc:isc + 1, :]) + m_ref[ish:ish + 1, :]).astype(BF16)


def _accumulate_rows(first, ref, val):
    @pl.when(first)
    def _():
        ref[...] = val

    @pl.when(jnp.logical_not(first))
    def _():
        ref[...] += val


def _loss_head(x, target, gvec, y, mod3, ig, cfg):
    def body(x_ref, t_ref, g_ref, y_ref, m_ref, dx_ref, loss_ref, dg_ref, dy_ref, dgate_ref):
        i = pl.program_id(0)

        @pl.when(i == 0)
        def _():
            loss_ref[...] = jnp.zeros_like(loss_ref)
            dg_ref[...] = jnp.zeros_like(dg_ref)

        xv = x_ref[...]
        g = g_ref[...]
        r = lax.rsqrt(jnp.mean(xv * xv, axis=-1, keepdims=True) + EPS)
        xh = xv * r
        err = xh * g - t_ref[...]
        loss_ref[...] += (0.5 / D) * _colsum(jnp.sum(err * err, axis=-1, keepdims=True))
        dy = err * (1.0 / D)
        dg_ref[...] += _colsum(dy * xh)
        dxh = dy * g
        dxv = r * (dxh - xh * jnp.mean(dxh * xh, axis=-1, keepdims=True))
        dx_ref[...] = dxv
        _gate_step(i, dxv, y_ref, m_ref, ig, dy_ref, dgate_ref, cfg)

    row = pl.BlockSpec((TM, D), lambda i: (i, 0))
    vec = pl.BlockSpec((1, D), lambda i: (0, 0))
    part = pl.BlockSpec((None, 1, D), lambda i: (cfg.mod_row(i), 0, 0))
    return pl.pallas_call(
        body, name="loss_head", grid=(cfg.nl // TM,), in_specs=[row, row, vec, row, _mod_spec(cfg)],
        out_specs=[row, pl.BlockSpec((1, 1), lambda i: (0, 0)), vec, row, part],
        out_shape=[jax.ShapeDtypeStruct((cfg.r, D), F32), jax.ShapeDtypeStruct((1, 1), F32),
                   jax.ShapeDtypeStruct((1, D), F32), jax.ShapeDtypeStruct((cfg.r, D), BF16),
                   jax.ShapeDtypeStruct((MOD_ROWS, 1, D), F32)],
        compiler_params=_params(1))(x, target, gvec, y, mod3)


def _rope_tables(seq):
    rows = seq // GRID_W
    row = jnp.repeat(jnp.arange(rows), GRID_W).astype(F32)
    col = jnp.tile(jnp.arange(GRID_W), rows).astype(F32)
    half = HEAD_DIM // 2
    inv = ROPE_BASE ** (-jnp.arange(0, half, 2, dtype=F32) / half)
    ar, ac = row[:, None] * inv, col[:, None] * inv
    ang = jnp.concatenate([ar, ar, ac, ac], axis=-1)
    sign = jnp.tile(jnp.concatenate([-jnp.ones((16,), F32), jnp.ones((16,), F32)]), 2)
    cos = jnp.tile(jnp.cos(ang), (1, 2))
    sin = jnp.tile(jnp.sin(ang) * sign, (1, 2))
    cos = jnp.concatenate([cos, jnp.ones((TM, 2 * HEAD_DIM), F32)], axis=0)
    sin = jnp.concatenate([sin, jnp.zeros((TM, 2 * HEAD_DIM), F32)], axis=0)
    return cos, sin


def _rope(x, cos, sin_signed, sign):
    lane = lax.broadcasted_iota(jnp.int32, x.shape, 1)
    low = (lane % 32) < 16
    rot = jnp.where(low, pltpu.roll(x, 112, 1), pltpu.roll(x, 16, 1))
    return x * cos + sign * (rot * sin_signed)


def _in_proj(name, x, gvec, mod3, w_in, layer, cos_t, sin_t, cfg):
    nlb, bps = cfg.nl // TM, cfg.seq // TM

    def body(x_ref, g_ref, m_ref, w_ref, cos_ref, sin_ref, h_ref, qkv_ref, cp_ref):
        hv = _norm_mod(x_ref[...], g_ref[...], m_ref, 0, 1)
        h_ref[...] = hv
        u = jnp.concatenate([_dot(hv, w_ref[j]) for j in range(N_SHARD)], axis=1)
        cos, sin = cos_ref[...], sin_ref[...]
        tiles = []
        for t in range(5):
            y = _rope(u[:, 128 * t:128 * (t + 1)], cos, sin, 1.0)
            tiles.append(y * (HEAD_DIM ** -0.5) if t < 4 else y)
        tiles.append(u[:, 640:768])
        qkv_ref[...] = jnp.concatenate(tiles, axis=1).astype(BF16)
        cp_ref[...] = u[:, 768:IN_W].astype(BF16)

    tab = pl.BlockSpec((TM, 128), lambda i: (jnp.where(i < nlb, i % bps, bps), 0))
    half = pl.BlockSpec((TM, 768), lambda i: (i, 0))
    row = pl.BlockSpec((TM, D), lambda i: (i, 0))
    return pl.pallas_call(
        body, name=name, grid=(cfg.r // TM,),
        in_specs=[row, pl.BlockSpec((1, D), lambda i: (0, 0)), _mod_spec(cfg),
                  pl.BlockSpec((None, N_SHARD, D, IN_SHARD), lambda i: (layer, 0, 0, 0)), tab, tab],
        out_specs=[row, half, half],
        out_shape=[jax.ShapeDtypeStruct((cfg.r, D), BF16), jax.ShapeDtypeStruct((cfg.r, 768), BF16),
                   jax.ShapeDtypeStruct((cfg.r, 768), BF16)],
        compiler_params=_params(1))(x, gvec, mod3, w_in, cos_t, sin_t)


def _att_specs(cfg):
    nlb, ncb = cfg.seq // QB, cfg.ctx // QB

    def qblk(s, qb):
        return jnp.where(qb < nlb, s * nlb + qb, cfg.nl // QB + s * ncb + qb - nlb)

    def near(off, col):
        return pl.BlockSpec((QB, 128), lambda s, qb: (s * nlb + jnp.clip(qb + off, 0, nlb - 1), col))

    def ctxs(col):
        return pl.BlockSpec((cfg.ctx, 128), lambda s, qb: (cfg.nl // cfg.ctx + s, col))

    qspec = pl.BlockSpec((QB, ATTN_W), lambda s, qb: (qblk(s, qb), 0))
    kv = [ctxs(4), ctxs(5), near(-1, 4), near(0, 4), near(1, 4), near(-1, 5), near(0, 5), near(1, 5)]
    return qblk, qspec, kv


def _att_scores(qb, nlb, sink_ref, q_ref, k_refs, v_refs, kh):
    is_lat = qb < nlb
    ii = lax.broadcasted_iota(jnp.int32, (4 * QB, 3 * QB), 0) % QB
    col = lax.broadcasted_iota(jnp.int32, (4 * QB, 3 * QB), 1)
    jj, blk = col % QB, col // QB
    off_p = jnp.where(jnp.logical_and(is_lat, qb >= 1), 0.0, NEG)
    off_c = jnp.where(is_lat, 0.0, NEG)
    off_n = jnp.where(jnp.logical_and(is_lat, qb <= nlb - 2), 0.0, NEG)
    inside = jnp.logical_or(blk == 1, jnp.logical_or(jnp.logical_and(blk == 0, jj >= ii),
                                                     jnp.logical_and(blk == 2, jj <= ii)))
    off = jnp.where(blk == 0, off_p, jnp.where(blk == 1, off_c, off_n))
    q4 = jnp.concatenate([q_ref[:, (4 * kh + g) * HEAD_DIM:(4 * kh + g + 1) * HEAD_DIM] for g in range(4)], axis=0)
    rg = lax.broadcasted_iota(jnp.int32, (4 * QB, 1), 0) // QB
    snk = jnp.where(rg == 0, sink_ref[4 * kh],
                    jnp.where(rg == 1, sink_ref[4 * kh + 1], jnp.where(rg == 2, sink_ref[4 * kh + 2], sink_ref[4 * kh + 3])))
    lanes = slice(kh * HEAD_DIM, (kh + 1) * HEAD_DIM)
    kx, vx = k_refs[0][:, lanes], v_refs[0][:, lanes]
    kl = jnp.concatenate([r[:, lanes] for r in k_refs[1:]], axis=0)
    vl = jnp.concatenate([r[:, lanes] for r in v_refs[1:]], axis=0)
    sx = _dot(q4, kx, "nt")
    sl = jnp.where(inside, _dot(q4, kl, "nt"), NEG) + off
    return q4, snk, (kx, kl), (vx, vl), (sx, sl)


def _att_fwd(name, qkv, sink, ctx_queries, cfg, ride=None):
    nlb, ncb = cfg.seq // QB, cfg.ctx // QB
    qblk, qspec, kvspecs = _att_specs(cfg)

    def body(sink_ref, q_ref, kx_ref, vx_ref, kp_ref, kc_ref, kn_ref, vp_ref, vc_ref, vn_ref, o_ref, lse_ref):
        qb = pl.program_id(1)
        for kh in range(2):
            q4, snk, _, vs, ss = _att_scores(qb, nlb, sink_ref, q_ref, (kx_ref, kp_ref, kc_ref, kn_ref),
                                             (vx_ref, vp_ref, vc_ref, vn_ref), kh)
            m = snk
            for s_ in ss:
                m = jnp.maximum(m, jnp.max(s_, axis=-1, keepdims=True))
            den = jnp.exp(snk - m)
            o4 = jnp.zeros((4 * QB, HEAD_DIM), F32)
            for s_, v_ in zip(ss, vs):
                p = jnp.exp(s_ - m)
                den = den + jnp.sum(p, axis=-1, keepdims=True)
                o4 = o4 + _dot(p, v_)
            o4 = o4 / den
            lse = m + jnp.log(den)
            for g in range(4):
                h = 4 * kh + g
                o_ref[:, h * HEAD_DIM:(h + 1) * HEAD_DIM] = o4[g * QB:(g + 1) * QB].astype(BF16)
                lse_ref[:, h:h + 1] = lse[g * QB:(g + 1) * QB]

    return _hosted_call(
        body, ride, name, (cfg.b, nlb + (ncb if ctx_queries else 0)), [sink] + [qkv] * 9,
        [pl.BlockSpec(memory_space=pltpu.SMEM), qspec, *kvspecs],
        [jax.ShapeDtypeStruct((cfg.r, D), BF16), jax.ShapeDtypeStruct((cfg.r, N_HEADS), F32)],
        [pl.BlockSpec((QB, ATTN_W), lambda s, qb: (qblk(s, qb), 0)),
         pl.BlockSpec((QB, N_HEADS), lambda s, qb: (qblk(s, qb), 0))], [], _params(2))


def _att_bwd(name, qkv, mix, dmix, lse, sink, cos_t, sin_t, ctx_queries, cfg, ride=None):
    nlb, ncb = cfg.seq // QB, cfg.ctx // QB
    nqb = nlb + (ncb if ctx_queries else 0)
    qblk, qspec, kvspecs = _att_specs(cfg)

    def body(sink_ref, q_ref, kx_ref, vx_ref, kp_ref, kc_ref, kn_ref, vp_ref, vc_ref, vn_ref, o_ref, do_ref,
             lse_ref, cosq_ref, sinq_ref, cosk_ref, sink_tab_ref, dq_ref, dkvl_ref, dkvc_ref, dsink_ref,
             accl, accc, dqs):
        s_id, qb = pl.program_id(0), pl.program_id(1)

        @pl.when(qb == 0)
        def _():
            accl[...] = jnp.zeros_like(accl)
            accc[...] = jnp.zeros_like(accc)

        @pl.when(jnp.logical_and(s_id == 0, qb == 0))
        def _():
            dsink_ref[...] = jnp.zeros_like(dsink_ref)

        starts = [pl.multiple_of(jnp.clip(qb + off, 0, nlb - 1) * QB, QB) for off in (-1, 0, 1)]
        for kh in range(2):
            q4, snk, ks, vs, ss = _att_scores(qb, nlb, sink_ref, q_ref, (kx_ref, kp_ref, kc_ref, kn_ref),
                                              (vx_ref, vp_ref, vc_ref, vn_ref), kh)
            lanes = slice(kh * HEAD_DIM, (kh + 1) * HEAD_DIM)
            heads =[slice((4 * kh + g) * HEAD_DIM, (4 * kh + g + 1) * HEAD_DIM) for g in range(4)]
            do4 = jnp.concatenate([do_ref[:, hs] for hs in heads], axis=0)
            o4 = jnp.concatenate([o_ref[:, hs] for hs in heads], axis=0).astype(F32)
            lse4 = jnp.concatenate([lse_ref[:, 4 * kh + g:4 * kh + g + 1] for g in range(4)], axis=0)
            delta = jnp.sum(do4 * o4, axis=-1, keepdims=True)
            dq4 = jnp.zeros((4 * QB, HEAD_DIM), F32)
            dks, dvs = [], []
            for s_, k_, v_ in zip(ss, ks, vs):
                p = jnp.exp(s_ - lse4)
                ds = p * (_dot(do4, v_, "nt") - delta)
                dq4 = dq4 + _dot(ds, k_)
                dks.append(_dot(ds, q4, "tn"))
                dvs.append(_dot(p, do4, "tn"))
            accc[:, lanes] += dks[0]
            accc[:, 128 + kh * HEAD_DIM:128 + (kh + 1) * HEAD_DIM] += dvs[0]
            for t, st in enumerate(starts):
                accl[pl.ds(st, QB), lanes] += dks[1][t * QB:(t + 1) * QB]
                accl[pl.ds(st, QB), 128 + kh * HEAD_DIM:128 + (kh + 1) * HEAD_DIM] += dvs[1][t * QB:(t + 1) * QB]
            dsk = -jnp.exp(snk - lse4) * delta
            for g in range(4):
                h = 4 * kh + g
                dsink_ref[h:h + 1, :] += jnp.broadcast_to(_colsum(dsk[g * QB:(g + 1) * QB]), (1, 128))
                dqs[:, heads[g]] = dq4[g * QB:(g + 1) * QB]
        cos, sin = cosq_ref[...], sinq_ref[...]
        dq_ref[...] = jnp.concatenate(
            [_rope(dqs[:, 128 * t:128 * (t + 1)], cos, sin, -1.0) * (HEAD_DIM ** -0.5) for t in range(4)],
            axis=1).astype(BF16)

        @pl.when(qb == nqb - 1)
        def _():
            dk = _rope(accl[:, 0:128], cosk_ref[...], sink_tab_ref[...], -1.0)
            dkvl_ref[...] = jnp.concatenate([dk, accl[:, 128:256]], axis=1).astype(BF16)
            dkvc_ref[...] = accc[...].astype(BF16)

    rowq = lambda w: pl.BlockSpec((QB, w), lambda s, qb: (qblk(s, qb), 0))
    tabq = pl.BlockSpec((QB, 128), lambda s, qb: (jnp.where(qb < nlb, qb, cfg.seq // QB), 0))
    tabk = pl.BlockSpec((cfg.seq, 128), lambda s, qb: (0, 0))
    return _hosted_call(
        body, ride, name, (cfg.b, nqb), [sink] + [qkv] * 9 + [mix, dmix, lse, cos_t, sin_t, cos_t, sin_t],
        [pl.BlockSpec(memory_space=pltpu.SMEM), qspec, *kvspecs, rowq(ATTN_W), rowq(ATTN_W), rowq(N_HEADS),
         tabq, tabq, tabk, tabk],
        [jax.ShapeDtypeStruct((cfg.r, IN_W), BF16), jax.ShapeDtypeStruct((cfg.nl, 256), BF16),
         jax.ShapeDtypeStruct((cfg.nc, 256), BF16), jax.ShapeDtypeStruct((N_HEADS, 128), F32)],
        [rowq(ATTN_W), pl.BlockSpec((cfg.seq, 256), lambda s, qb: (s, 0)),
         pl.BlockSpec((cfg.ctx, 256), lambda s, qb: (s, 0)), pl.BlockSpec((N_HEADS, 128), lambda s, qb: (0, 0))],
        [pltpu.VMEM((cfg.seq, 256), F32), pltpu.VMEM((cfg.ctx, 256), F32), pltpu.VMEM((QB, ATTN_W), F32)], _params(2))


def _pool_geometry(n, c):
    lane = lax.broadcasted_iota(jnp.int32, (1, POOL_W), 1) // HEAD_DIM
    wl = jnp.where(lane == 0, 1, jnp.where(lane == 1, 2, jnp.where(lane == 2, 4, 8)))
    wr = wl - 1
    t = c * CHUNK + lax.broadcasted_iota(jnp.int32, (CHUNK, POOL_W), 0)
    cnt = (jnp.minimum(t + wr, n - 1) - jnp.maximum(t - wl, 0) + 1).astype(F32)
    return wl, wr, cnt


def _build_phases(src, ph, c):
    for s in range(1, 8):
        ph[s - 1] = src[c * CHUNK + s:c * CHUNK + s + CHUNK + 24, :]


def _window(src, ph, c, off):
    a, s = divmod(off, 8)
    if s == 0:
        return src[c * CHUNK + 8 * a:c * CHUNK + 8 * a + CHUNK, :]
    return ph[s - 1, 8 * a:8 * a + CHUNK, :]


def _conv_chunk(hp, ph, dw_ref, dwb_ref, c):
    _build_phases(hp, ph, c)
    acc = jnp.zeros((CHUNK, CONV_W), F32) + dwb_ref[...]
    for j in range(CONV_K):
        acc = acc + dw_ref[j:j + 1, :] * _window(hp, ph, c, j + 1)
    return acc


def _fill_glu(cp_ref, hp, n):
    hp[0:PAD, :] = jnp.zeros((PAD, CONV_W), F32)
    hp[PAD + n:2 * PAD + n, :] = jnp.zeros((PAD, CONV_W), F32)
    for c in range(n // CHUNK):
        rows = slice(c * CHUNK, (c + 1) * CHUNK)
        a = cp_ref[rows, 0:CONV_W].astype(F32)
        g = cp_ref[rows, CONV_W:2 * CONV_W].astype(F32)
        hp[PAD + c * CHUNK:PAD + (c + 1) * CHUNK, :] = a * _sigmoid(g)


def _fill_pool(cp_ref, pp, n):
    pp[0:PAD, :] = jnp.zeros((PAD, POOL_W), F32)
    pp[PAD + n:2 * PAD + n, :] = jnp.zeros((PAD, POOL_W), F32)
    for c in range(n // CHUNK):
        pp[PAD + c * CHUNK:PAD + (c + 1) * CHUNK, :] = cp_ref[c * CHUNK:(c + 1) * CHUNK, 2 * CONV_W:768].astype(F32)


def _pool_chunk(pp, ph, n, c):
    wl, wr, cnt = _pool_geometry(n, c)
    _build_phases(pp, ph, c)
    acc = jnp.zeros((CHUNK, POOL_W), F32)
    for o in range(-8, 8):
        acc = acc + jnp.where(jnp.logical_and(o >= -wl, o <= wr), _window(pp, ph, c, PAD + o), 0.0)
    return acc / cnt - pp[PAD + c * CHUNK:PAD + (c + 1) * CHUNK, :], cnt


def _seq_specs(n, blk_off, width, col=0):
    return pl.BlockSpec((n, width), lambda s: (blk_off + s, col))


def _full(shape):
    return pl.BlockSpec(shape, lambda s: (0,) * len(shape))


_PHASES = pltpu.VMEM((7, CHUNK + 24, CONV_W), F32)


def _convpool_fwd(name, cpin, mix, yconv, prm, n, blk_off, cfg):
    dw, dwb, lng, lnb, wbd, ps = prm
    n_alias = 1 if yconv is None else 2

    def body(*refs):
        cp_ref, dw_ref, dwb_ref, lng_ref, lnb_ref, wbd_ref, ps_ref = refs[:7]
        out_ref, y_ref, hp, pp, ph = refs[7 + n_alias:]
        _fill_glu(cp_ref, hp, n)
        _fill_pool(cp_ref, pp, n)
        for c in range(n // CHUNK):
            rows = slice(c * CHUNK, (c + 1) * CHUNK)
            y = _conv_chunk(hp, ph, dw_ref, dwb_ref, c)
            y_ref[rows, :] = y
            d = y - jnp.mean(y, axis=-1, keepdims=True)
            hn = d * lax.rsqrt(jnp.mean(d * d, axis=-1, keepdims=True) + EPS) * lng_ref[...] + lnb_ref[...]
            out_ref[rows, 0:CONV_W] = (hn * _sigmoid(hn)).astype(BF16)
            yp, _ = _pool_chunk(pp, ph, n, c)
            out_ref[rows, CONV_W:2 * CONV_W] = (_dot(yp, wbd_ref[...]) * ps_ref[...]).astype(BF16)

    through = [mix] if yconv is None else [mix, yconv]
    return pl.pallas_call(
        body, name=name, grid=(cfg.b,),
        in_specs=[_seq_specs(n, blk_off, 768), _full((32, CONV_W)), _full((1, CONV_W)), _full((1, CONV_W)),
                  _full((1, CONV_W)), _full((POOL_W, POOL_W)), _full((1, POOL_W))] + [_ANY] * n_alias,
        out_specs=[_seq_specs(n, blk_off, 512, 1), _seq_specs(n, blk_off, CONV_W)],
        out_shape=[jax.ShapeDtypeStruct((cfg.r, D), BF16), jax.ShapeDtypeStruct((cfg.r, CONV_W), F32)],
        scratch_shapes=[pltpu.VMEM((n + 2 * PAD, CONV_W), F32), pltpu.VMEM((n + 2 * PAD, POOL_W), F32), _PHASES],
        input_output_aliases={7 + i: i for i in range(n_alias)},
        compiler_params=_params(1))(cpin, dw, dwb, lng, lnb, wbd, ps, *through)


_SMALL_SHAPES = [(32, CONV_W), (1, CONV_W), (1, CONV_W), (1, CONV_W), (POOL_W, POOL_W), (1, POOL_W)]


def _convpool_bwd(name, cpin, yconv, dmix, prm, acc_in, n, blk_off, cfg):
    dw, dwb, lng, lnb, wbd, ps = prm
    nch = n // CHUNK

    def body(cp_ref, y_ref, dm_ref, dw_ref, dwb_ref, lng_ref, lnb_ref, wbd_ref, ps_ref, dcp_in,
             a_dw, a_dwb, a_lng, a_lnb, a_wbd, a_ps,
             dcp_ref, o_dw, o_dwb, o_lng, o_lnb, o_wbd, o_ps, hp, dyp, pp, wp, dyv, dwacc, ph):
        s = pl.program_id(0)

        @pl.when(s == 0)
        def _():
            for o_, a_ in ((o_dw, a_dw), (o_dwb, a_dwb), (o_lng, a_lng), (o_lnb, a_lnb), (o_wbd, a_wbd), (o_ps, a_ps)):
                o_[...] = a_[...]
            dwacc[...] = jnp.zeros_like(dwacc)

        _fill_glu(cp_ref, hp, n)
        _fill_pool(cp_ref, pp, n)
        for ref in (dyp, wp):
            ref[0:PAD, :] = jnp.zeros((PAD, CONV_W), F32)
            ref[PAD + n:2 * PAD + n, :] = jnp.zeros((PAD, CONV_W), F32)
        for c in range(nch):
            rows = slice(c * CHUNK, (c + 1) * CHUNK)
            y = y_ref[rows, :]
            d = y - jnp.mean(y, axis=-1, keepdims=True)
            rstd = lax.rsqrt(jnp.mean(d * d, axis=-1, keepdims=True) + EPS)
            xh = d * rstd
            hn = xh * lng_ref[...] + lnb_ref[...]
            sg = _sigmoid(hn)
            dhn = dm_ref[rows, 0:CONV_W] * (sg * (1.0 + hn * (1.0 - sg)))
            o_lnb[...] += _colsum(dhn)
            o_lng[...] += _colsum(dhn * xh)
            dxh = dhn * lng_ref[...]
            dy = rstd * (dxh - jnp.mean(dxh, axis=-1, keepdims=True) - xh * jnp.mean(dxh * xh, axis=-1, keepdims=True))
            o_dwb[...] += _colsum(dy)
            dyp[PAD + c * CHUNK:PAD + (c + 1) * CHUNK, :] = dy
            _build_phases(hp, ph, c)
            for j in range(CONV_K):
                prod = dy * _window(hp, ph, c, j + 1)
                dwacc[8 * j:8 * j + 8, :] += jnp.sum(prod.reshape(CHUNK // 8, 8, CONV_W), axis=0)
            yp, cnt = _pool_chunk(pp, ph, n, c)
            dz = dm_ref[rows, CONV_W:2 * CONV_W]
            o_ps[...] += _colsum(dz * _dot(yp, wbd_ref[...]))
            dzs = dz * ps_ref[...]
            o_wbd[...] += _dot(yp, dzs, "tn")
            dv = _dot(dzs, wbd_ref[...], "nt")
            dyv[rows, :] = dv
            wp[PAD + c * CHUNK:PAD + (c + 1) * CHUNK, :] = dv / cnt
        for c in range(nch):
            rows = slice(c * CHUNK, (c + 1) * CHUNK)
            _build_phases(dyp, ph, c)
            dh = jnp.zeros((CHUNK, CONV_W), F32)
            for j in range(CONV_K):
                dh = dh + dw_ref[j:j + 1, :] * _window(dyp, ph, c, 31 - j)
            a = cp_ref[rows, 0:CONV_W].astype(F32)
            sg = _sigmoid(cp_ref[rows, CONV_W:2 * CONV_W].astype(F32))
            dcp_ref[rows, 0:CONV_W] = (dh * sg).astype(BF16)
            dcp_ref[rows, CONV_W:2 * CONV_W] = (dh * a * sg * (1.0 - sg)).astype(BF16)
            wl, wr, _ = _pool_geometry(n, c)
            _build_phases(wp, ph, c)
            dp = -dyv[rows, :]
            for o in range(-8, 8):
                dp = dp + jnp.where(jnp.logical_and(o >= -wl, o <= wr), _window(wp, ph, c, PAD - o), 0.0)
            dcp_ref[rows, 2 * CONV_W:768] = dp.astype(BF16)

        @pl.when(s == cfg.b - 1)
        def _():
            for j in range(CONV_K):
                o_dw[j:j + 1, :] += _colsum(dwacc[8 * j:8 * j + 8, :])

    small_specs = [_full(sh) for sh in _SMALL_SHAPES]
    return pl.pallas_call(
        body, name=name, grid=(cfg.b,),
        in_specs=[_seq_specs(n, blk_off, 768), _seq_specs(n, blk_off, CONV_W), _seq_specs(n, blk_off, 512, 1),
                  *small_specs, _ANY, *small_specs],
        out_specs=[_seq_specs(n, blk_off, 768, 1), *small_specs],
        out_shape=[jax.ShapeDtypeStruct((cfg.r, IN_W), BF16)] + [jax.ShapeDtypeStruct(sh, F32) for sh in _SMALL_SHAPES],
        scratch_shapes=[pltpu.VMEM((n + 2 * PAD, CONV_W), F32), pltpu.VMEM((n + 2 * PAD, CONV_W), F32),
                        pltpu.VMEM((n + 2 * PAD, POOL_W), F32), pltpu.VMEM((n + 2 * PAD, POOL_W), F32),
                        pltpu.VMEM((n, POOL_W), F32), pltpu.VMEM((8 * 32, CONV_W), F32), _PHASES],
        input_output_aliases={9: 0}, compiler_params=_params(1))(cpin, yconv, dmix, dw, dwb, lng, lnb, wbd, ps, *acc_in)


def _place_kv(name, du, dkvl, dkvc, with_ctx, cfg):
    nlb = cfg.nl // TM

    def body(l_ref, c_ref, du_in, o_ref):
        i = pl.program_id(0)
        o_ref[...] = jnp.where(i < nlb, l_ref[...], c_ref[...])

    return pl.pallas_call(
        body, name=name, grid=(cfg.r // TM if with_ctx else nlb,),
        in_specs=[pl.BlockSpec((TM, 256), lambda i: (jnp.minimum(i, nlb - 1), 0)),
                  pl.BlockSpec((TM, 256), lambda i: (jnp.maximum(i - nlb, 0), 0)), _ANY],
        out_specs=pl.BlockSpec((TM, 256), lambda i: (i, 2)), out_shape=jax.ShapeDtypeStruct((cfg.r, IN_W), BF16),
        input_output_aliases={2: 0}, compiler_params=_params(1))(dkvl, dkvc, du)


def _place_ctx_kv_only(name, du, dkvc, cfg):
    nlb = cfg.nl // TM

    def body(c_ref, du_in, o_ref):
        o_ref[...] = jnp.zeros_like(o_ref)
        o_ref[:, ATTN_W:ATTN_W + 256] = c_ref[...]

    return pl.pallas_call(
        body, name=name, grid=(cfg.nc // TM,), in_specs=[pl.BlockSpec((TM, 256), lambda i: (i, 0)), _ANY],
        out_specs=pl.BlockSpec((TM, IN_W), lambda i: (nlb + i, 0)), out_shape=jax.ShapeDtypeStruct((cfg.r, IN_W), BF16),
        input_output_aliases={1: 0}, compiler_params=_params(1))(dkvc, du)


def _norm_fwd(name, x, gvec, mod3, ish, isc, nblk, cfg):
    def body(x_ref, g_ref, m_ref, o_ref):
        o_ref[...] = _norm_mod(x_ref[...], g_ref[...], m_ref, ish, isc)

    row = pl.BlockSpec((TM, D), lambda i: (i, 0))
    return pl.pallas_call(
        body, name=name, grid=(nblk,),
        in_specs=[row, pl.BlockSpec((1, D), lambda i: (0, 0)), _mod_spec(cfg)], out_specs=row,
        out_shape=jax.ShapeDtypeStruct((cfg.r, D), BF16), compiler_params=_params(1))(x, gvec, mod3)


def _ffn_in(name, h, w_ffn_in, layer, nblk, cfg, ride=None):
    def body(h_ref, wg_ref, wu_ref, fac_ref, act_ref):
        hv = h_ref[...]
        g = _dot(hv, wg_ref[...])
        u = _dot(hv, wu_ref[...])
        s = _sigmoid(g)
        gs = g * s
        fac_ref[0] = ((s + gs * (1.0 - s)) * u).astype(BF16)
        fac_ref[1] = gs.astype(BF16)
        act_ref[...] = (gs * u).astype(BF16)

    wspec = lambda base: pl.BlockSpec((None, None, D, HALF_FF), lambda j, i: (layer, base + j, 0, 0))
    return _hosted_call(
        body, ride, name, (2, nblk), [h, w_ffn_in, w_ffn_in],
        [pl.BlockSpec((TM, D), lambda j, i: (i, 0)), wspec(0), wspec(2)],
        [jax.ShapeDtypeStruct((2, cfg.r, D_FF), BF16), jax.ShapeDtypeStruct((cfg.r, D_FF), BF16)],
        [pl.BlockSpec((2, TM, HALF_FF), lambda j, i: (0, i, j)), pl.BlockSpec((TM, HALF_FF), lambda j, i: (i, j))],
        [], _params(2))


def _row_block(rows, cols, max_bytes=1 << 20):
    best = 16
    for t in range(16, rows + 1, 16):
        if rows % t == 0 and t * cols * 4 <= max_bytes:
            best = t
    assert rows % best == 0
    return best


def _pair_add(name, own32, recv, c_idx):
    _, _, s0, s1 = own32.shape
    tr = _row_block(s0, s1)

    def body(c_ref, a_ref, b_ref, o_ref):
        o_ref[...] = (a_ref[...].astype(F32) + b_ref[...].astype(F32)).astype(BF16)

    grid_spec = pltpu.PrefetchScalarGridSpec(
        num_scalar_prefetch=1, grid=(N_SHARD * s0 // tr,),
        in_specs=[pl.BlockSpec((None, tr, s1), lambda i, c: (c[0], i, 0)), pl.BlockSpec((tr, s1), lambda i, c: (i, 0))],
        out_specs=pl.BlockSpec((tr, s1), lambda i, c: (i, 0)))
    out = pl.pallas_call(body, name=name, grid_spec=grid_spec, out_shape=jax.ShapeDtypeStruct((N_SHARD * s0, s1), BF16),
                         compiler_params=_params(1))(c_idx, own32.reshape(2, N_SHARD * s0, s1), recv.reshape(N_SHARD * s0, s1))
    return out.reshape(N_SHARD, s0, s1)


def _shard_sum(name, pair_sum, recv, jc_idx):
    _, s0, s1 = pair_sum.shape
    tr = _row_block(s0, s1)

    def body(jc_ref, a_ref, b_ref, o_ref):
        o_ref[...] = ((a_ref[...].astype(F32) + b_ref[0].astype(F32)) + b_ref[1].astype(F32)) + b_ref[2].astype(F32)

    grid_spec = pltpu.PrefetchScalarGridSpec(
        num_scalar_prefetch=1, grid=(s0 // tr,),
        in_specs=[pl.BlockSpec((None, tr, s1), lambda i, jc: (jc[0], i, 0)), pl.BlockSpec((3, tr, s1), lambda i, jc: (0, i, 0))],
        out_specs=pl.BlockSpec((None, tr, s1), lambda i, jc: (jc[1], i, 0)))
    return pl.pallas_call(body, name=name, grid_spec=grid_spec, out_shape=jax.ShapeDtypeStruct((2, s0, s1), F32),
                          compiler_params=_params(1))(jc_idx, pair_sum, recv)


def _adamw_math(w, g, m, v):
    m = ADAM_B1 * m + (1.0 - ADAM_B1) * g
    v = ADAM_B2 * v + (1.0 - ADAM_B2) * (g * g)
    m_hat = m / (1.0 - ADAM_B1 ** ADAM_STEP)
    v_hat = v / (1.0 - ADAM_B2 ** ADAM_STEP)
    delta = -ADAM_LR * (m_hat / (jnp.sqrt(v_hat) + ADAM_EPS) + ADAM_WD * w)
    return delta, m, v


def _adamw(name, w, g, m, v):
    rows, cols = w.shape
    tr = rows if rows % 16 else _row_block(rows, cols, 1 << 19)

    def body(w_ref, g_ref, m_ref, v_ref, d_ref, mo_ref, vo_ref):
        d, mn, vn = _adamw_math(w_ref[...], g_ref[...], m_ref[...], v_ref[...])
        d_ref[...] = d
        mo_ref[...] = mn
        vo_ref[...] = vn

    spec = pl.BlockSpec((tr, cols), lambda i: (i, 0))
    shape = jax.ShapeDtypeStruct((rows, cols), F32)
    return pl.pallas_call(body, name=name, grid=(rows // tr,), in_specs=[spec] * 4, out_specs=[spec] * 3,
                          out_shape=[shape] * 3, compiler_params=_params(1))(w, g, m, v)


def _adamw_layers(name, w, g_layers, m, v):
    rows, cols = w.shape
    s0 = rows // 2
    tr = _row_block(s0, cols, 1 << 19)
    nb = s0 // tr

    def body(w_ref, g0_ref, g1_ref, m_ref, v_ref, g_ref, d_ref, mo_ref, vo_ref):
        g = jnp.where(pl.program_id(0) < nb, g0_ref[...], g1_ref[...])
        d, mn, vn = _adamw_math(w_ref[...], g, m_ref[...], v_ref[...])
        g_ref[...] = g
        d_ref[...] = d
        mo_ref[...] = mn
        vo_ref[...] = vn

    spec = pl.BlockSpec((tr, cols), lambda i: (i, 0))
    shape = jax.ShapeDtypeStruct((rows, cols), F32)
    return pl.pallas_call(
        body, name=name, grid=(2 * nb,),
        in_specs=[spec, pl.BlockSpec((tr, cols), lambda i: (jnp.minimum(i, nb - 1), 0)),
                  pl.BlockSpec((tr, cols), lambda i: (jnp.maximum(i - nb, 0), 0)), spec, spec],
        out_specs=[spec] * 4, out_shape=[shape] * 4, compiler_params=_params(1))(w, g_layers[0], g_layers[1], m, v)


def _position():
    return lax.axis_index("x"), lax.axis_index("y"), lax.axis_index("c")


def _other_chips(x, y):
    return [(1 - x, y), (x, 1 - y), (1 - x, 1 - y)]


def _run_ride(name, ride):
    n_in, n_out = len(ride.ins), len(ride.out_shape)

    def body(*refs):
        parts = (refs[:n_in], refs[n_in:n_in + n_out], refs[n_in + n_out:])
        ride.start(*parts)
        ride.finish(*parts)

    return pl.pallas_call(
        body, name=name, in_specs=[_ANY] * n_in, out_specs=[_ANY] * n_out, out_shape=ride.out_shape,
        scratch_shapes=ride.scratch, compiler_params=pltpu.CompilerParams(vmem_limit_bytes=VMEM_LIMIT_V7X))(*ride.ins)


def _gather_ride(shards):
    n = len(shards)

    def copies(ins, outs, scr):
        ssem, rsem = scr[n], scr[n + 1]
        x, y, c = _position()
        me, sibling = 2 * x + y, (x, y, 1 - c)

        def remote(src, dst, i, dev):
            return pltpu.make_async_remote_copy(src, dst, ssem.at[i], rsem.at[i], device_id=dev, device_id_type=_MESH)

        fetch_out, fetch_in, pass_out, pass_in = [], [], [], []
        for a, (src, dst) in enumerate(zip(ins, outs)):
            for k, (px, py) in enumerate(_other_chips(x, y)):
                j, i1, i2 = 2 * px + py, 3 * a + k, 3 * n + 3 * a + k
                fetch_out.append(remote(src.at[c], dst.at[me, c], i1, (px, py, c)))
                fetch_in.append(remote(src.at[c], dst.at[j, c], i1, (px, py, c)))
                pass_out.append(remote(dst.at[j, c], dst.at[j, c], i2, sibling))
                pass_in.append(remote(dst.at[j, 1 - c], dst.at[j, 1 - c], i2, sibling))
        return me, fetch_out, fetch_in, pass_out, pass_in

    def start(ins, outs, scr):
        bufs, lsem = scr[:n], scr[n + 2]
        me, fetch_out, _, _, _ = copies(ins, outs, scr)
        for cp in fetch_out:
            cp.start()
        loads = []
        for a, (src, buf) in enumerate(zip(ins, bufs)):
            ld = pltpu.make_async_copy(src, buf, lsem.at[2 * a])
            ld.start()
            loads.append(ld)
        for a, (ld, buf, dst) in enumerate(zip(loads, bufs, outs)):
            ld.wait()
            st = pltpu.make_async_copy(buf, dst.at[me], lsem.at[2 * a + 1])
            st.start()
            st.wait()

    def finish(ins, outs, scr):
        _, fetch_out, fetch_in, pass_out, pass_in = copies(ins, outs, scr)
        for arrived, onward in zip(fetch_in, pass_out):
            arrived.wait_recv()
            onward.start()
        for cp in pass_in:
            cp.wait_recv()
        for cp in fetch_out + pass_out:
            cp.wait_send()

    return _Ride(list(shards), [jax.ShapeDtypeStruct((N_SHARD,) + s.shape, s.dtype) for s in shards],
                 [pltpu.VMEM(s.shape, s.dtype) for s in shards]
                 + [pltpu.SemaphoreType.DMA((6 * n,)), pltpu.SemaphoreType.DMA((6 * n,)), pltpu.SemaphoreType.DMA((2 * n,))],
                 start, finish)


def _comm(name, ins, out_shape, n_remote, plan):
    n_in, n_out = len(ins), len(out_shape)

    def body(*refs):
        plan(refs[:n_in], refs[n_in:n_in + n_out], *refs[n_in + n_out:])

    return pl.pallas_call(
        body, name=name, in_specs=[_ANY] * n_in, out_specs=[_ANY] * n_out, out_shape=out_shape,
        scratch_shapes=[pltpu.SemaphoreType.DMA((n_remote,)), pltpu.SemaphoreType.DMA((n_remote,))])(*ins)


def _send_other_half(name, grads_bf):
    n = len(grads_bf)

    def plan(ins, outs, ssem, rsem):
        x, y, c = _position()
        started = []
        for a, (src, dst) in enumerate(zip(ins, outs)):
            cp = pltpu.make_async_remote_copy(src.at[1 - c], dst, ssem.at[a], rsem.at[a], device_id=(x, y, 1 - c),
                                              device_id_type=_MESH)
            cp.start()
            started.append(cp)
        for cp in started:
            cp.wait_recv()
        for cp in started:
            cp.wait_send()

    shapes = [jax.ShapeDtypeStruct(s.shape[1:], s.dtype) for s in grads_bf]
    return _comm(name, grads_bf, shapes, n, plan)


def _exchange_ride(pair_sums):
    n = len(pair_sums)

    def copies(ins, outs, scr):
        ssem, rsem = scr
        x, y, c = _position()
        return [pltpu.make_async_remote_copy(src.at[2 * px + py], dst.at[k], ssem.at[3 * a + k], rsem.at[3 * a + k],
                                             device_id=(px, py, c), device_id_type=_MESH)
                for a, (src, dst) in enumerate(zip(ins, outs)) for k, (px, py) in enumerate(_other_chips(x, y))]

    def start(ins, outs, scr):
        for cp in copies(ins, outs, scr):
            cp.start()

    def finish(ins, outs, scr):
        for cp in copies(ins, outs, scr):
            cp.wait_recv()
        for cp in copies(ins, outs, scr):
            cp.wait_send()

    return _Ride(list(pair_sums), [jax.ShapeDtypeStruct((3,) + s.shape[1:], s.dtype) for s in pair_sums],
                 [pltpu.SemaphoreType.DMA((3 * n,)), pltpu.SemaphoreType.DMA((3 * n,))], start, finish)


def _swap_reduced(name, grads):
    n = len(grads)

    def body(*refs):
        ins, outs, ssem, rsem = refs[:n], refs[n:2 * n], refs[2 * n], refs[2 * n + 1]
        x, y, c = _position()
        sent = []
        for a, (src, dst) in enumerate(zip(ins, outs)):
            cp = pltpu.make_async_remote_copy(src.at[c], dst.at[c], ssem.at[a], rsem.at[a], device_id=(x, y, 1 - c),
                                              device_id_type=_MESH)
            cp.start()
            sent.append(cp)
        for a, (src, dst) in enumerate(zip(ins, outs)):
            pltpu.make_async_remote_copy(src.at[1 - c], dst.at[1 - c], ssem.at[a], rsem.at[a], device_id=(x, y, 1 - c),
                                         device_id_type=_MESH).wait_recv()
        for cp in sent:
            cp.wait_send()

    return pl.pallas_call(
        body, name=name, in_specs=[_ANY] * n, out_specs=[_ANY] * n,
        out_shape=[jax.ShapeDtypeStruct(g.shape, g.dtype) for g in grads],
        scratch_shapes=[pltpu.SemaphoreType.DMA((n,)), pltpu.SemaphoreType.DMA((n,))],
        input_output_aliases={a: a for a in range(n)})(*grads)


_FLIPS = [(dx, dy, dc) for dx in (0, 1) for dy in (0, 1) for dc in (0, 1) if dx + dy + dc]
_VMEM = pl.BlockSpec(memory_space=pltpu.VMEM)


def _to_all(src_of, dst, ssem, rsem):
    x, y, c = _position()
    me = 4 * x + 2 * y + c
    peers = [((x + dx) % 2, (y + dy) % 2, (c + dc) % 2) for dx, dy, dc in _FLIPS]
    sent = []
    for k, (px, py, pc) in enumerate(peers):
        cp = pltpu.make_async_remote_copy(src_of(2 * px + py), dst.at[me], ssem.at[k], rsem.at[k],
                                          device_id=(px, py, pc), device_id_type=_MESH)
        cp.start()
        sent.append(cp)
    for k, (px, py, pc) in enumerate(peers):
        pltpu.make_async_remote_copy(src_of(2 * px + py), dst.at[4 * px + 2 * py + pc], ssem.at[k], rsem.at[k],
                                     device_id=(px, py, pc), device_id_type=_MESH).wait_recv()
    for cp in sent:
        cp.wait_send()
    return me, 2 * x + y


def _share_small(name, block, total):
    shape = block.shape

    def body(in_ref, out_ref, *scratch):
        buf, ssem, rsem = (out_ref,) + scratch if not total else scratch
        me, _ = _to_all(lambda chip: in_ref, buf, ssem, rsem)
        buf[me] = in_ref[...]
        if total:
            acc = buf[0]
            for d in range(1, 8):
                acc = acc + buf[d]
            out_ref[...] = acc

    sems = [pltpu.SemaphoreType.DMA((7,)), pltpu.SemaphoreType.DMA((7,))]
    return pl.pallas_call(
        body, name=name, in_specs=[_VMEM], out_specs=_VMEM,
        out_shape=jax.ShapeDtypeStruct(shape if total else (8,) + shape, F32),
        scratch_shapes=([pltpu.VMEM((8,) + shape, F32)] if total else []) + sems,
        compiler_params=pltpu.CompilerParams(vmem_limit_bytes=VMEM_LIMIT_V7X))(block)


def _mod_rows_exchange(mv):
    def body(mv_ref, out_ref, ssem, rsem):
        x, y, c = _position()
        me = 2 * x + y
        out_ref[me] = mv_ref[4 * x + 2 * y + c]
        sent = []
        for k, (px, py) in enumerate(_other_chips(x, y)):
            cp = pltpu.make_async_remote_copy(mv_ref.at[4 * px + 2 * py + c], out_ref.at[me], ssem.at[k], rsem.at[k],
                                              device_id=(px, py, c), device_id_type=_MESH)
            cp.start()
            sent.append(cp)
        for k, (px, py) in enumerate(_other_chips(x, y)):
            pltpu.make_async_remote_copy(mv_ref.at[0], out_ref.at[2 * px + py], ssem.at[k], rsem.at[k],
                                         device_id=(px, py, c), device_id_type=_MESH).wait_recv()
        for cp in sent:
            cp.wait_send()

    return pl.pallas_call(
        body, name="mod_rows_exchange", in_specs=[_VMEM], out_specs=_VMEM,
        out_shape=jax.ShapeDtypeStruct((N_SHARD,) + mv.shape[1:], F32),
        scratch_shapes=[pltpu.SemaphoreType.DMA((3,)), pltpu.SemaphoreType.DMA((3,))],
        compiler_params=pltpu.CompilerParams(vmem_limit_bytes=VMEM_LIMIT_V7X))(mv)


def _mod_grad_exchange(dmj, dm_rows):
    def body(dmj_ref, rows_ref, out_ref, bias_ref, ssem, rsem):
        me, chip = _to_all(lambda j: dmj_ref.at[j], out_ref, ssem, rsem)
        out_ref[me] = dmj_ref[chip]
        for l in range(2):
            bias_ref[l] = _colsum(rows_ref[l])

    return pl.pallas_call(
        body, name="mod_grad_exchange", in_specs=[_VMEM, _VMEM], out_specs=[_VMEM, _VMEM],
        out_shape=[jax.ShapeDtypeStruct((8,) + dmj.shape[1:], F32), jax.ShapeDtypeStruct((2, 1, dm_rows.shape[-1]), F32)],
        scratch_shapes=[pltpu.SemaphoreType.DMA((7,)), pltpu.SemaphoreType.DMA((7,))],
        compiler_params=pltpu.CompilerParams(vmem_limit_bytes=VMEM_LIMIT_V7X))(dmj, dm_rows)


def _pack(arrays):
    flat = jnp.concatenate([a.reshape(-1).astype(F32) for a in arrays])
    total = flat.shape[0]
    rows = -(-total // (8 * PACK_LANES)) * 8
    return jnp.pad(flat, (0, rows * PACK_LANES - total)).reshape(rows, PACK_LANES)


def _unpack(pack, shapes):
    flat, out, pos = pack.reshape(-1), [], 0
    for sh in shapes:
        size = int(np.prod(sh)) if len(sh) else 1
        out.append(flat[pos:pos + size].reshape(sh))
        pos += size
    return out


def _block_diag(pw):
    out = jnp.zeros((POOL_W, POOL_W), pw.dtype)
    for g in range(4):
        out = out.at[g * 64:(g + 1) * 64, g * 64:(g + 1) * 64].set(pw[g])
    return out


def _local_step(x, ctx, small, mod3s, loss_target, comm):
    cfg = _Cfg(x.shape[0], x.shape[1], ctx.shape[1])
    assert cfg.seq % TM == 0 and cfg.nc % TM == 0 and cfg.seq % cfg.ctx == 0 and cfg.ctx % CHUNK == 0
    nb_all, nb_lat = cfg.r // TM, cfg.nl // TM
    last = 1
    wf, big = comm.wf, comm.grads
    cos_t, sin_t = _rope_tables(cfg.seq)
    xs = jnp.concatenate([x.reshape(cfg.nl, D), ctx.reshape(cfg.nc, D)], axis=0)
    row = lambda w: pl.BlockSpec((TM, w), lambda i, j, k: (i, 0))
    mod3_spec = pl.BlockSpec((None, 6, D), lambda i, j, k: (cfg.mod_row(i), 0, 0))
    whole = lambda rows: pl.BlockSpec((None, rows, D), lambda i, j, k: (0, 0, 0))

    def conv_params(l):
        dw = jnp.pad(wf[l]["conv_dw"], ((0, 1), (0, 0)))
        return (dw, small["conv_dw_b"][l][None], small["conv_ln_g"][l][None], small["conv_ln_b"][l][None],
                _block_diag(small["pool_w"][l]).astype(BF16), small["pool_scale"][l][None])

    def residual_epi(ig):
        def epi(acc, ex, outs):
            x_ref, m_ref = ex
            outs[0][...] = x_ref[...] + m_ref[ig:ig + 1, :] * acc
            outs[1][...] = acc.astype(BF16)
        return epi

    def hosted(name, call):
        outs, got = call(comm.ride(name))
        comm.landed(name, got)
        return outs

    saved = []
    for l in range(2):
        nb = nb_lat if l == last else nb_all
        wl = wf[l]
        mod3 = mod3s[l]
        h1, qkv, cpin = _in_proj(f"in_proj{l}", xs, small["norm1_g"][l][None], mod3, wl["w_in"], 0, cos_t, sin_t, cfg)
        mix, lse = hosted(f"att_fwd{l}", lambda ride: _att_fwd(f"att_fwd{l}", qkv, small["attn_sink"][l], l != last, cfg, ride))
        prm = conv_params(l)
        mix, yconv = _convpool_fwd(f"convpool_fwd_lat{l}", cpin, mix, None, prm, cfg.seq, 0, cfg)
        if l != last:
            mix, yconv = _convpool_fwd(f"convpool_fwd_ctx{l}", cpin, mix, yconv, prm, cfg.ctx, cfg.nl // cfg.ctx, cfg)
        x1, y1 = _mm(f"out_proj{l}", "nn", (nb, 1, 1), mix, wl["w_out"].reshape(1, D, D), row(D), whole(D),
                     [jax.ShapeDtypeStruct((cfg.r, D), F32), jax.ShapeDtypeStruct((cfg.r, D), BF16)], [row(D), row(D)],
                     extras=[xs, mod3], extra_specs=[row(D), mod3_spec], epi=residual_epi(2))
        h2 = _norm_fwd(f"norm2_fwd{l}", x1, small["norm2_g"][l][None], mod3, 3, 4, nb, cfg)
        gu, act = hosted(f"ffn_in{l}", lambda ride: _ffn_in(f"ffn_in{l}", h2, wl["w_ffn_in"], 0, nb, cfg, ride))
        x2, y2 = _mm(f"ffn_out{l}", "nn", (nb, 1, 1), act, wl["w_ffn_out"].reshape(1, D_FF, D), row(D_FF), whole(D_FF),
                     [jax.ShapeDtypeStruct((cfg.r, D), F32), jax.ShapeDtypeStruct((cfg.r, D), BF16)], [row(D), row(D)],
                     extras=[x1, mod3], extra_specs=[row(D), mod3_spec], epi=residual_epi(5))
        saved.append(dict(mod3=mod3, x0=xs, h1=h1, qkv=qkv, cpin=cpin, mix=mix, yconv=yconv, lse=lse, y1=y1, x1=x1,
                          h2=h2, gu=gu, act=act, y2=y2, prm=prm))
        xs = x2

    dx, loss, d_final_g, dy2, dg2 = _loss_head(xs, loss_target.reshape(cfg.nl, D), small["final_g"][None],
                                               saved[last]["y2"], saved[last]["mod3"], 5, cfg)

    sg = {k: [None, None] for k in ("norm1_g", "norm2_g", "conv_dw", "conv_dw_b", "conv_ln_g", "conv_ln_b",
                                    "attn_sink", "pool_w", "pool_scale")}
    dms = [None, None]

    def swiglu_bwd_epi(acc, ex, outs):
        outs[0][0] = (acc * ex[0][0].astype(F32)).astype(BF16)
        outs[0][1] = (acc * ex[0][1].astype(F32)).astype(BF16)

    def halves_epi(acc, ex, outs):
        h = acc.shape[0] // 2
        outs[0][0] = acc[:h].astype(BF16)
        outs[0][1] = acc[h:].astype(BF16)

    def row_shards_epi(n):
        def epi(acc, ex, outs):
            s0 = acc.shape[0] // n
            h = s0 // 2
            for t in range(n):
                for half in range(2):
                    outs[0][half, t] = acc[t * s0 + half * h:t * s0 + (half + 1) * h].astype(BF16)
        return epi

    def col_shards_epi(acc, ex, outs):
        h = acc.shape[0] // 2
        for j in range(N_SHARD):
            for half in range(2):
                outs[0][half, j] = acc[half * h:(half + 1) * h, j * IN_SHARD:(j + 1) * IN_SHARD].astype(BF16)

    for l in (1, 0):
        sv = saved[l]
        mod3 = sv["mod3"]
        nb = nb_lat if l == last else nb_all
        tr = _dw_rows(nb * TM)
        steps = nb * TM // tr
        wl = wf[l]
        gu_spec = pl.BlockSpec((2, TM, HALF_FF), lambda i, j, k: (0, i, j))
        df = _mm(f"ffn_out_bwd{l}", "nt", (nb, 2, 1), dy2, wl["w_ffn_out"].reshape(1, D_FF, D), row(D),
                 pl.BlockSpec((None, HALF_FF, D), lambda i, j, k: (0, j, 0)),
                 [jax.ShapeDtypeStruct((2, cfg.r, D_FF), BF16)], [gu_spec], extras=[sv["gu"]], extra_specs=[gu_spec],
                 epi=swiglu_bwd_epi)[0]
        big[l]["w_ffn_out"] = _mm_dw(
            f"dw_ffn_out{l}", sv["act"], dy2, pl.BlockSpec((tr, HALF_FF), lambda i, j, k: (k, j)),
            pl.BlockSpec((tr, D), lambda i, j, k: (k, 0)), (2, N_SHARD, D_FF // 8, D),
            pl.BlockSpec((2, 2, D_FF // 8, D), lambda i, j, k: (0, j, 0, 0)), 2, (HALF_FF, D), steps, row_shards_epi(2))[0]
        ride = comm.ride(f"dw_ffn_in{l}")
        res = _mm_dw(f"dw_ffn_in{l}", sv["h2"], df, pl.BlockSpec((tr, D), lambda i, j, k: (k, 0)),
                     pl.BlockSpec((None, tr, HALF_FF), lambda i, j, k: (j // 2, k, j % 2)), (2, N_SHARD, D // 2, HALF_FF),
                     pl.BlockSpec((2, None, D // 2, HALF_FF), lambda i, j, k: (0, j, 0, 0)), N_SHARD, (D, HALF_FF), steps,
                     halves_epi, ride=ride)
        res, got = res if ride is not None else (res, [])
        big[l]["w_ffn_in"] = res[0]
        comm.landed(f"dw_ffn_in{l}", got)
        dx1, dsh2, dsc2, dn2, dy1, dg1 = _proj_norm_bwd(
            f"ffn_in_bwd{l}", df, pl.BlockSpec((2, TM, D_FF), lambda i: (0, i, 0)),
            lambda a_ref, j: a_ref[j // 2, :, (j % 2) * HALF_FF:(j % 2 + 1) * HALF_FF], wl["w_ffn_in"],
            sv["x1"], dx, small["norm2_g"][l][None], mod3, 4, nb, False, (sv["y1"], mod3, 2), cfg)
        dmix = _mm(f"out_proj_bwd{l}", "nt", (nb, 1, 1), dy1, wl["w_out"].reshape(1, D, D), row(D), whole(D),
                   [jax.ShapeDtypeStruct((cfg.r, D), F32)], [row(D)])[0]
        big[l]["w_out"] = _mm_dw(
            f"dw_out{l}", sv["mix"], dy1, pl.BlockSpec((tr, D), lambda i, j, k: (k, 0)),
            pl.BlockSpec((tr, D), lambda i, j, k: (k, 0)), (2, N_SHARD, D // 8, D),
            pl.BlockSpec((2, N_SHARD, D // 8, D), lambda i, j, k: (0, 0, 0, 0)), 1, (D, D), steps, row_shards_epi(N_SHARD))[0]
        du, dkvl, dkvc, dsink = hosted(f"att_bwd{l}", lambda ride: _att_bwd(
            f"att_bwd{l}", sv["qkv"], sv["mix"], dmix, sv["lse"], small["attn_sink"][l], cos_t, sin_t, l != last, cfg, ride))
        acc = [du] + [jnp.zeros(sh, F32) for sh in _SMALL_SHAPES]
        acc = _convpool_bwd(f"convpool_bwd_lat{l}", sv["cpin"], sv["yconv"], dmix, sv["prm"], acc, cfg.seq, 0, cfg)
        if l != last:
            acc = _convpool_bwd(f"convpool_bwd_ctx{l}", sv["cpin"], sv["yconv"], dmix, sv["prm"], acc, cfg.ctx,
                                cfg.nl // cfg.ctx, cfg)
        du, g_dw, g_dwb, g_lng, g_lnb, g_wbd, g_ps = acc
        du = _place_kv(f"place_kv{l}", du, dkvl, dkvc, l != last, cfg)
        if l == last:
            du = _place_ctx_kv_only(f"place_ctx_kv{l}", du, dkvc, cfg)
        sg["attn_sink"][l] = dsink[:, 0]
        sg["conv_dw"][l], sg["conv_dw_b"][l], sg["conv_ln_g"][l], sg["conv_ln_b"][l] = g_dw[:CONV_K], g_dwb[0], g_lng[0], g_lnb[0]
        sg["pool_w"][l] = jnp.stack([g_wbd[g * 64:(g + 1) * 64, g * 64:(g + 1) * 64] for g in range(4)])
        sg["pool_scale"][l] = g_ps[0]
        tr_all = _dw_rows(cfg.r)
        big[l]["w_in"] = _mm_dw(
            f"dw_in{l}", sv["h1"], du, pl.BlockSpec((tr_all, D), lambda i, j, k: (k, 0)),
            pl.BlockSpec((tr_all, IN_W), lambda i, j, k: (k, 0)), (2, N_SHARD, D // 2, IN_SHARD),
            pl.BlockSpec((2, N_SHARD, D // 2, IN_SHARD), lambda i, j, k: (0, 0, 0, 0)), 1, (D, IN_W), cfg.r // tr_all,
            col_shards_epi)[0]
        below = (saved[l - 1]["y2"], saved[l - 1]["mod3"], 5) if l > 0 else None
        res = _proj_norm_bwd(
            f"in_proj_bwd{l}", du, pl.BlockSpec((TM, IN_W), lambda i: (i, 0)),
            lambda a_ref, j: a_ref[:, j * IN_SHARD:(j + 1) * IN_SHARD], wl["w_in"],
            sv["x0"], dx1, small["norm1_g"][l][None], mod3, 1, nb_all, l == last, below, cfg)
        dx, dsh1, dsc1, dn1 = res[:4]
        sg["norm1_g"][l], sg["norm2_g"][l] = dn1[0], dn2[0]
        parts = [dsh1, dsc1, dg1, dsh2, dsc2, dg2]
        if below is not None:
            dy2, dg2 = res[4:]
        dm = jnp.concatenate([t[:cfg.b, 0, :] for t in parts], axis=1)
        live = (0, 1) if l == last else range(6)
        dm_ctx = jnp.concatenate([t[cfg.b, 0, :] if i in live else jnp.zeros((D,), F32) for i, t in enumerate(parts)])
        dms[l] = jnp.concatenate([dm, dm_ctx[None, :], jnp.zeros((MOD_ROWS - cfg.b - 1, 6 * D), F32)], axis=0)

    grad_x = dx[:cfg.nl].reshape(x.shape)
    small_grads = {k: jnp.stack(v) for k, v in sg.items()}
    small_grads["final_g"] = d_final_g[0]
    return loss, grad_x, small_grads, dms


_BIG = ("w_in", "w_out", "w_ffn_in", "w_ffn_out")
_TAPS = "conv_dw"
_SMALL = ("c_ctx", "b_mod", "norm1_g", "norm2_g", "conv_dw", "conv_dw_b", "conv_ln_g", "conv_ln_b", "attn_sink",
          "pool_w", "pool_scale", "final_g")
_ORDER = ("c_ctx", "w_mod", "b_mod", "norm1_g", "norm2_g", "w_in", "conv_dw", "conv_dw_b", "conv_ln_g", "conv_ln_b",
          "attn_sink", "pool_w", "pool_scale", "w_out", "w_ffn_in", "w_ffn_out", "final_g")
_GATHER_HOSTS = {"att_fwd0": (0, ("w_out", "w_ffn_in", "w_ffn_out", _TAPS)), "ffn_in0": (1, _BIG + (_TAPS,))}
_REDUCE_HOSTS = {"dw_ffn_in0": (1, _BIG), "att_bwd0": (0, ("w_ffn_in", "w_ffn_out"))}
_FIRST_GATHER = (0, ("w_in",))
_LAST_REDUCE = (0, ("w_in", "w_out"))


class _Comm:
    def __init__(self, w, c_idx, jc_idx):
        self.shapes = {k: w[k].shape[1:] for k in _BIG}
        self.c_idx, self.jc_idx = c_idx, jc_idx
        halves = lambda a: a.reshape(2, a.shape[0] // 2, a.shape[1])
        taps = jnp.pad(w[_TAPS], ((0, 0), (0, 1), (0, 64)))
        cast = {k: w[k].astype(BF16) for k in _BIG}
        self.shards = [{**{k: halves(cast[k][l]) for k in _BIG}, _TAPS: halves(taps[l])} for l in range(2)]
        self.wf = [dict(), dict()]
        self.grads = [dict(), dict()]
        self.reduced = [dict(), dict()]
        self._open = {}
        self.landed("first", _run_ride("gather_first", self.ride("first")))

    def ride(self, host):
        if host == "first" or host in _GATHER_HOSTS:
            layer, keys = _FIRST_GATHER if host == "first" else _GATHER_HOSTS[host]
            return _gather_ride([self.shards[layer][k] for k in keys])
        if host == "last" or host in _REDUCE_HOSTS:
            layer, keys = _LAST_REDUCE if host == "last" else _REDUCE_HOSTS[host]
            tag = f"{layer}_{keys[0]}"
            mine = [self.grads[layer][k] for k in keys]
            other = _send_other_half(f"send_other_half{tag}", mine)
            pair = [_pair_add(f"pair_add{layer}_{k}", a, b, self.c_idx) for k, a, b in zip(keys, mine, other)]
            self._open[host] = pair
            return _exchange_ride(pair)
        return None

    def landed(self, host, got):
        if host == "first" or host in _GATHER_HOSTS:
            layer, keys = _FIRST_GATHER if host == "first" else _GATHER_HOSTS[host]
            for k, f in zip(keys, got):
                if k == _TAPS:
                    taps = f.reshape(N_SHARD, 32, 128)[:, :CONV_K, :64]
                    self.wf[layer][k] = jnp.transpose(taps, (1, 0, 2)).reshape(CONV_K, CONV_W)
                else:
                    self.wf[layer][k] = f.reshape((1, N_SHARD) + self.shapes[k])
        if host == "last" or host in _REDUCE_HOSTS:
            layer, keys = _LAST_REDUCE if host == "last" else _REDUCE_HOSTS[host]
            mine = [_shard_sum(f"shard_sum{layer}_{k}", a, b, self.jc_idx) for k, a, b in zip(keys, self._open.pop(host), got)]
            for k, g in zip(keys, _swap_reduced(f"swap_reduced{layer}_{keys[0]}", mine)):
                self.reduced[layer][k] = g.reshape(self.shapes[k])

    def finish(self):
        self.landed("last", _run_ride("exchange_last", self.ride("last")))


def _conditioning(c, c_ctx, w_mod, b_mod, chip):
    b = c.shape[0]
    block = jnp.concatenate([c, c_ctx[None, :], jnp.zeros((8 - b - 1, D), F32)], axis=0)
    c_all = _share_small("share_c", block, False).reshape(64, D)
    bias = lax.dynamic_slice_in_dim(b_mod, chip * MOD_W, MOD_W, axis=1)
    full = lambda r, q: pl.BlockSpec((r, q), lambda i, j, k: (0, 0))

    def bias_epi(acc, ex, outs):
        outs[0][...] = acc + ex[0][...]

    mv = [_mm(f"mod_fwd{l}", "nn", (1, 1, 1), c_all, w_mod[l], full(64, D), full(D, MOD_W),
              [jax.ShapeDtypeStruct((64, MOD_W), F32)], [full(64, MOD_W)], extras=[bias[l][None]],
              extra_specs=[full(1, MOD_W)], a_fn=_silu, epi=bias_epi)[0] for l in range(2)]
    by_dev = jnp.transpose(jnp.stack(mv).reshape(2, 8, 8, MOD_W), (1, 0, 2, 3))
    rows = jnp.transpose(_mod_rows_exchange(by_dev), (1, 2, 0, 3)).reshape(2, 8, 6 * D)
    rows = jnp.pad(rows, ((0, 0), (0, MOD_ROWS - 8), (0, 0)))
    return [rows[l].reshape(MOD_ROWS, 6, D) for l in range(2)], c_all


def _conditioning_bwd(dms, c_all, w_mod, b):
    dm = jnp.stack([d[:8] for d in dms])
    by_chip = jnp.transpose(dm.reshape(2, 8, N_SHARD, MOD_W), (2, 0, 1, 3))
    gathered, d_bias = _mod_grad_exchange(by_chip, dm)
    dm_all = jnp.transpose(gathered, (1, 0, 2, 3)).reshape(2, 64, MOD_W)
    full = lambda r, q: pl.BlockSpec((r, q), lambda i, j, k: (0, 0))

    def ctx_rows_epi(acc, ex, outs):
        row = lax.broadcasted_iota(jnp.int32, acc.shape, 0) % 8
        outs[0][...] = _colsum(jnp.where(row == b, acc * _dsilu(ex[0][...]), 0.0))

    g_mod, d_ctx = [], jnp.zeros((D,), F32)
    for l in range(2):
        g_mod.append(_mm(f"dw_mod{l}", "tn", (1, 1, 1), c_all, dm_all[l], full(64, D), full(64, MOD_W),
                         [jax.ShapeDtypeStruct((D, MOD_W), F32)], [full(D, MOD_W)], a_fn=_silu)[0])
        part = _mm(f"mod_bwd{l}", "nt", (1, 1, 1), dm_all[l], w_mod[l], full(64, MOD_W), full(D, MOD_W),
                   [jax.ShapeDtypeStruct((1, D), F32)], [full(1, D)], extras=[c_all], extra_specs=[full(64, D)],
                   epi=ctx_rows_epi)[0]
        d_ctx = d_ctx + part[0]
    return g_mod, d_bias[:, 0, :], d_ctx


def kernel(x, c, ctx, c_ctx, w_mod, b_mod, norm1_g, norm2_g, w_in, conv_dw, conv_dw_b, conv_ln_g, conv_ln_b, attn_sink, pool_w, pool_scale, w_out, w_ffn_in, w_ffn_out, final_g, loss_target, m_c_ctx, m_w_mod, m_b_mod, m_norm1_g, m_norm2_g, m_w_in, m_conv_dw, m_conv_dw_b, m_conv_ln_g, m_conv_ln_b, m_attn_sink, m_pool_w, m_pool_scale, m_w_out, m_w_ffn_in, m_w_ffn_out, m_final_g, v_c_ctx, v_w_mod, v_b_mod, v_norm1_g, v_norm2_g, v_w_in, v_conv_dw, v_conv_dw_b, v_conv_ln_g, v_conv_ln_b, v_attn_sink, v_pool_w, v_pool_scale, v_w_out, v_w_ffn_in, v_w_ffn_out, v_final_g):
    w = dict(c_ctx=c_ctx, w_mod=w_mod, b_mod=b_mod, norm1_g=norm1_g, norm2_g=norm2_g, w_in=w_in, conv_dw=conv_dw,
             conv_dw_b=conv_dw_b, conv_ln_g=conv_ln_g, conv_ln_b=conv_ln_b, attn_sink=attn_sink, pool_w=pool_w,
             pool_scale=pool_scale, w_out=w_out, w_ffn_in=w_ffn_in, w_ffn_out=w_ffn_out, final_g=final_g)
    m = dict(c_ctx=m_c_ctx, w_mod=m_w_mod, b_mod=m_b_mod, norm1_g=m_norm1_g, norm2_g=m_norm2_g, w_in=m_w_in,
             conv_dw=m_conv_dw, conv_dw_b=m_conv_dw_b, conv_ln_g=m_conv_ln_g, conv_ln_b=m_conv_ln_b,
             attn_sink=m_attn_sink, pool_w=m_pool_w, pool_scale=m_pool_scale, w_out=m_w_out, w_ffn_in=m_w_ffn_in,
             w_ffn_out=m_w_ffn_out, final_g=m_final_g)
    v = dict(c_ctx=v_c_ctx, w_mod=v_w_mod, b_mod=v_b_mod, norm1_g=v_norm1_g, norm2_g=v_norm2_g, w_in=v_w_in,
             conv_dw=v_conv_dw, conv_dw_b=v_conv_dw_b, conv_ln_g=v_conv_ln_g, conv_ln_b=v_conv_ln_b,
             attn_sink=v_attn_sink, pool_w=v_pool_w, pool_scale=v_pool_scale, w_out=v_w_out, w_ffn_in=v_w_ffn_in,
             w_ffn_out=v_w_ffn_out, final_g=v_final_g)
    xi, yi, ci = _position()
    chip = 2 * xi + yi
    mod3s, c_all = _conditioning(c, c_ctx, w_mod, b_mod, chip)
    comm = _Comm(w, jnp.reshape(ci, (1,)).astype(jnp.int32), jnp.stack([chip, ci]).astype(jnp.int32))
    small = {k: w[k] for k in _SMALL if k not in ("conv_dw", "c_ctx", "b_mod")}
    loss, grad_x, sgrads, dms = _local_step(x, ctx, small, mod3s, loss_target, comm)
    comm.finish()
    g_mod, sgrads["b_mod"], d_ctx = _conditioning_bwd(dms, c_all, w_mod, c.shape[0])
    sgrads["c_ctx"] = 0.5 * d_ctx

    names = list(_SMALL)
    total = _share_small("sum_small", _pack([loss] + [sgrads[k] for k in names]), True)
    parts = _unpack(total, [()] + [sgrads[k].shape for k in names])
    loss_out = parts[0]
    gsmall = dict(zip(names, parts[1:]))
    gsmall["conv_dw"] = lax.dynamic_slice_in_dim(gsmall["conv_dw"], chip * 64, 64, axis=2)

    grads, delta, new_m, new_v = dict(gsmall), {}, {}, {}
    reduced = [{**comm.reduced[l], "w_mod": g_mod[l]} for l in range(2)]
    for k in ("w_mod",) + _BIG:
        s0, s1 = w[k].shape[1:]
        flat = lambda a: a.reshape(2 * s0, s1)
        g_, d_, m_, v_ = _adamw_layers(f"adamw_{k}", flat(w[k]), [reduced[l][k] for l in range(2)], flat(m[k]), flat(v[k]))
        grads[k], delta[k], new_m[k], new_v[k] = [a.reshape(w[k].shape) for a in (g_, d_, m_, v_)]
    d_, m_, v_ = _adamw("adamw_small", _pack([w[k] for k in names]), _pack([gsmall[k] for k in names]),
                        _pack([m[k] for k in names]), _pack([v[k] for k in names]))
    sshapes = [w[k].shape for k in names]
    for k, a, b, e in zip(names, _unpack(d_, sshapes), _unpack(m_, sshapes), _unpack(v_, sshapes)):
        delta[k], new_m[k], new_v[k] = a, b, e
    return (loss_out, grad_x, *[grads[k] for k in _ORDER], *[delta[k] for k in _ORDER],
            *[new_m[k] for k in _ORDER], *[new_v[k] for k in _ORDER])
```

```python
from typing import NamedTuple

import jax
import jax.numpy as jnp
import numpy as np
from jax import lax
from jax.experimental import pallas as pl
from jax.experimental.pallas import tpu as pltpu

F32 = jnp.float32
BF16 = jnp.bfloat16

D = 1024
GRID_W = 64
HEAD_DIM = 64
N_HEADS = 8
ATTN_W = 512
CONV_W = 256
POOL_W = 256
IN_W = 1536
D_FF = 2816
CONV_K = 31
QB = 128
ROPE_BASE = 10000.0
EPS = 1e-6
NEG = -1e30
N_SHARD = 4
IN_SHARD = IN_W // N_SHARD
HALF_FF = D_FF // 2
MOD_W = 6 * D // N_SHARD
MOD_ROWS = 16
PACK_LANES = 128

ADAM_LR = 0.001
ADAM_B1 = 0.9
ADAM_B2 = 0.999
ADAM_EPS = 1e-08
ADAM_WD = 0.01
ADAM_STEP = 10

VMEM_LIMIT_V7X = 56 * 1024 * 1024
TM = 512
TR_MAX = 1024
CHUNK = 256
PAD = 16

_MESH = pl.DeviceIdType.MESH
_ANY = pl.BlockSpec(memory_space=pl.ANY)
_DIMS = {"nn": (((1,), (0,)), ((), ())), "nt": (((1,), (1,)), ((), ())), "tn": (((0,), (0,)), ((), ()))}


class _Cfg(NamedTuple):
    b: int
    seq: int
    ctx: int

    @property
    def nl(self):
        return self.b * self.seq

    @property
    def nc(self):
        return self.b * self.ctx

    @property
    def r(self):
        return self.nl + self.nc

    def mod_row(self, i):
        return jnp.where(i < self.nl // TM, i // (self.seq // TM), self.b)

    def first_of_row(self, i):
        nlb = self.nl // TM
        return jnp.logical_or(jnp.logical_and(i < nlb, i % (self.seq // TM) == 0), i == nlb)


def _params(n_grid=0):
    sem = ("arbitrary",) * n_grid if n_grid else None
    return pltpu.CompilerParams(dimension_semantics=sem, vmem_limit_bytes=VMEM_LIMIT_V7X)


def _dot(a, b, mode="nn"):
    return lax.dot_general(a.astype(BF16), b.astype(BF16), _DIMS[mode], preferred_element_type=F32)


def _sigmoid(x):
    return 1.0 / (1.0 + jnp.exp(-x))


def _silu(x):
    return x * _sigmoid(x)


def _dsilu(x):
    s = _sigmoid(x)
    return s * (1.0 + x * (1.0 - s))


def _colsum(v):
    return jnp.sum(v, axis=0, keepdims=True)


def _dw_rows(rows):
    return TR_MAX if rows % TR_MAX == 0 else TM


def _epi_store(acc, ex, outs):
    for o in outs:
        o[...] = acc.astype(o.dtype)


class _Ride(NamedTuple):
    ins: list
    out_shape: list
    scratch: list
    start: object
    finish: object


class _Hosted(NamedTuple):
    ride: _Ride
    n_in: int
    n_out: int
    grid: tuple

    def split(self, refs):
        n_ri, n_ro, n_rs = len(self.ride.ins), len(self.ride.out_shape), len(self.ride.scratch)
        r_in = refs[self.n_in:self.n_in + n_ri]
        r_out = refs[self.n_in + n_ri + self.n_out:self.n_in + n_ri + self.n_out + n_ro]
        own = refs[:self.n_in] + refs[self.n_in + n_ri:self.n_in + n_ri + self.n_out] + \
            refs[self.n_in + n_ri + self.n_out + n_ro:len(refs) - n_rs]
        return own, (r_in, r_out, refs[len(refs) - n_rs:])

    def start(self, parts):
        ids = [pl.program_id(d) for d in range(len(self.grid))]
        first = ids[0] == 0
        for i in ids[1:]:
            first = jnp.logical_and(first, i == 0)
        pl.when(first)(lambda: self.ride.start(*parts))

    def finish(self, parts):
        ids = [pl.program_id(d) for d in range(len(self.grid))]
        last = ids[0] == self.grid[0] - 1
        for i, g in zip(ids[1:], self.grid[1:]):
            last = jnp.logical_and(last, i == g - 1)
        pl.when(last)(lambda: self.ride.finish(*parts))


def _hosted_call(body, ride, name, grid, ins, in_specs, out_shape, out_specs, scratch, params):
    if ride is None:
        res = pl.pallas_call(body, name=name, grid=grid, in_specs=in_specs, out_specs=out_specs, out_shape=out_shape,
                             scratch_shapes=scratch, compiler_params=params)(*ins)
        return list(res), []
    host = _Hosted(ride, len(ins), len(out_shape), tuple(grid))

    def hosted(*refs):
        own, parts = host.split(refs)
        host.start(parts)
        body(*own)
        host.finish(parts)

    res = pl.pallas_call(
        hosted, name=name, grid=grid, in_specs=list(in_specs) + [_ANY] * len(ride.ins),
        out_specs=list(out_specs) + [_ANY] * len(ride.out_shape), out_shape=list(out_shape) + list(ride.out_shape),
        scratch_shapes=list(scratch) + list(ride.scratch), compiler_params=params)(*ins, *ride.ins)
    return list(res[:len(out_shape)]), list(res[len(out_shape):])


def _mm(name, mode, grid, a, b, a_spec, b_spec, out_shape, out_specs, acc_shape=None, extras=(),
        extra_specs=(), a_fn=None, epi=_epi_store, ride=None):
    nk = grid[2]
    n_ex, n_out = len(extras), len(out_shape)

    def body(*refs):
        a_ref, b_ref = refs[:2]
        ex = refs[2:2 + n_ex]
        outs = refs[2 + n_ex:2 + n_ex + n_out]
        av = a_ref[...]
        if a_fn is not None:
            av = a_fn(av)
        part = _dot(av, b_ref[...], mode)
        if nk == 1:
            epi(part, ex, outs)
        else:
            acc = refs[-1]
            k = pl.program_id(2)

            @pl.when(k == 0)
            def _():
                acc[...] = part

            @pl.when(k > 0)
            def _():
                acc[...] += part

            @pl.when(k == nk - 1)
            def _():
                epi(acc[...], ex, outs)

    scratch = [] if nk == 1 else [pltpu.VMEM(acc_shape, F32)]
    outs, ride_outs = _hosted_call(body, ride, name, grid, [a, b, *extras], [a_spec, b_spec, *extra_specs],
                                   list(out_shape), list(out_specs), scratch, _params(3))
    return outs if ride is None else (outs, ride_outs)


def _mm_dw(name, a, b, a_spec, b_spec, out_shape, out_spec, n_out_blocks, acc_shape, n_steps, epi, a_fn=None,
           extras=(), extra_specs=(), extra_out=(), extra_out_specs=(), ride=None):
    grid = (1, n_out_blocks, n_steps)
    outs = [jax.ShapeDtypeStruct(out_shape, BF16)] + list(extra_out)
    return _mm(name, "tn", grid, a, b, a_spec, b_spec, outs, [out_spec, *extra_out_specs], acc_shape, extras,
               extra_specs, a_fn, epi, ride)


def _gate_step(i, dxv, y_ref, m_ref, ig, dy_ref, dgate_ref, cfg):
    dy_ref[...] = (dxv * m_ref[ig:ig + 1, :]).astype(BF16)
    _accumulate_rows(cfg.first_of_row(i), dgate_ref, _colsum(dxv * y_ref[...].astype(F32)))


def _proj_norm_bwd(name, a, a_spec, pick, w, x, dres, gvec, mod3, isc, nblk, res_latent_only, gate, cfg):
    ns = w.shape[-1]
    nlb = cfg.nl // TM

    def body(a_ref, w_ref, x_ref, dres_ref, g_ref, m_ref, *rest):
        if gate is None:
            dx_ref, dsh_ref, dsc_ref, dg_ref = rest
        else:
            y_ref, gm_ref, dx_ref, dsh_ref, dsc_ref, dg_ref, dy_ref, dgate_ref = rest
        i = pl.program_id(0)
        dhv = _dot(pick(a_ref, 0), w_ref[0], "nt")
        for j in range(1, N_SHARD):
            dhv = dhv + _dot(pick(a_ref, j), w_ref[j], "nt")
        xv = x_ref[...]
        r = lax.rsqrt(jnp.mean(xv * xv, axis=-1, keepdims=True) + EPS)
        xh = xv * r
        g = g_ref[...]
        sc1 = 1.0 + m_ref[isc:isc + 1, :]
        t = dhv * xh
        first = cfg.first_of_row(i)
        _accumulate_rows(first, dsh_ref, _colsum(dhv))
        _accumulate_rows(first, dsc_ref, _colsum(t * g))
        _accumulate_rows(i == 0, dg_ref, _colsum(t * sc1))
        dxh = dhv * (g * sc1)
        dxn = r * (dxh - xh * jnp.mean(dxh * xh, axis=-1, keepdims=True))
        dxv = (jnp.where(i < nlb, dres_ref[...], 0.0) if res_latent_only else dres_ref[...]) + dxn
        dx_ref[...] = dxv
        if gate is not None:
            _gate_step(i, dxv, y_ref, gm_ref, gate[2], dy_ref, dgate_ref, cfg)

    row = pl.BlockSpec((TM, D), lambda i: (i, 0))
    vec = pl.BlockSpec((1, D), lambda i: (0, 0))
    part = pl.BlockSpec((None, 1, D), lambda i: (cfg.mod_row(i), 0, 0))
    part_shape = jax.ShapeDtypeStruct((MOD_ROWS, 1, D), F32)
    resident = pl.BlockSpec((None, N_SHARD, D, ns), lambda i: (0, 0, 0, 0), pipeline_mode=pl.Buffered(1))
    ins, in_specs = [a, w, x, dres, gvec, mod3], [a_spec, resident, row, row, vec, _mod_spec(cfg)]
    out_specs = [row, part, part, vec]
    out_shape = [jax.ShapeDtypeStruct((cfg.r, D), F32), part_shape, part_shape, jax.ShapeDtypeStruct((1, D), F32)]
    if gate is not None:
        ins, in_specs = ins + list(gate[:2]), in_specs + [row, _mod_spec(cfg)]
        out_specs, out_shape = out_specs + [row, part], out_shape + [jax.ShapeDtypeStruct((cfg.r, D), BF16), part_shape]
    return pl.pallas_call(body, name=name, grid=(nblk,), in_specs=in_specs, out_specs=out_specs, out_shape=out_shape,
                          compiler_params=_params(1))(*ins)


def _mod_spec(cfg):
    return pl.BlockSpec((None, 6, D), lambda i: (cfg.mod_row(i), 0, 0))


def _norm_mod(xv, g, m_ref, ish, isc):
    r = lax.rsqrt(jnp.mean(xv * xv, axis=-1, keepdims=True) + EPS)
    return (xv * r * g * (1.0 + m_ref[isc:isc + 1, :]) + m_ref[ish:ish + 1, :]).astype(BF16)


def _accumulate_rows(first, ref, val):
    @pl.when(first)
    def _():
        ref[...] = val

    @pl.when(jnp.logical_not(first))
    def _():
        ref[...] += val


def _loss_head(x, target, gvec, y, mod3, ig, cfg):
    def body(x_ref, t_ref, g_ref, y_ref, m_ref, dx_ref, loss_ref, dg_ref, dy_ref, dgate_ref):
        i = pl.program_id(0)

        @pl.when(i == 0)
        def _():
            loss_ref[...] = jnp.zeros_like(loss_ref)
            dg_ref[...] = jnp.zeros_like(dg_ref)

        xv = x_ref[...]
        g = g_ref[...]
        r = lax.rsqrt(jnp.mean(xv * xv, axis=-1, keepdims=True) + EPS)
        xh = xv * r
        err = xh * g - t_ref[...]
        loss_ref[...] += (0.5 / D) * _colsum(jnp.sum(err * err, axis=-1, keepdims=True))
        dy = err * (1.0 / D)
        dg_ref[...] += _colsum(dy * xh)
        dxh = dy * g
        dxv = r * (dxh - xh * jnp.mean(dxh * xh, axis=-1, keepdims=True))
        dx_ref[...] = dxv
        _gate_step(i, dxv, y_ref, m_ref, ig, dy_ref, dgate_ref, cfg)

    row = pl.BlockSpec((TM, D), lambda i: (i, 0))
    vec = pl.BlockSpec((1, D), lambda i: (0, 0))
    part = pl.BlockSpec((None, 1, D), lambda i: (cfg.mod_row(i), 0, 0))
    return pl.pallas_call(
        body, name="loss_head", grid=(cfg.nl // TM,), in_specs=[row, row, vec, row, _mod_spec(cfg)],
        out_specs=[row, pl.BlockSpec((1, 1), lambda i: (0, 0)), vec, row, part],
        out_shape=[jax.ShapeDtypeStruct((cfg.r, D), F32), jax.ShapeDtypeStruct((1, 1), F32),
                   jax.ShapeDtypeStruct((1, D), F32), jax.ShapeDtypeStruct((cfg.r, D), BF16),
                   jax.ShapeDtypeStruct((MOD_ROWS, 1, D), F32)],
        compiler_params=_params(1))(x, target, gvec, y, mod3)


def _rope_tables(seq):
    rows = seq // GRID_W
    row = jnp.repeat(jnp.arange(rows), GRID_W).astype(F32)
    col = jnp.tile(jnp.arange(GRID_W), rows).astype(F32)
    half = HEAD_DIM // 2
    inv = ROPE_BASE ** (-jnp.arange(0, half, 2, dtype=F32) / half)
    ar, ac = row[:, None] * inv, col[:, None] * inv
    ang = jnp.concatenate([ar, ar, ac, ac], axis=-1)
    sign = jnp.tile(jnp.concatenate([-jnp.ones((16,), F32), jnp.ones((16,), F32)]), 2)
    cos = jnp.tile(jnp.cos(ang), (1, 2))
    sin = jnp.tile(jnp.sin(ang) * sign, (1, 2))
    cos = jnp.concatenate([cos, jnp.ones((TM, 2 * HEAD_DIM), F32)], axis=0)
    sin = jnp.concatenate([sin, jnp.zeros((TM, 2 * HEAD_DIM), F32)], axis=0)
    return cos, sin


def _rope(x, cos, sin_signed, sign):
    lane = lax.broadcasted_iota(jnp.int32, x.shape, 1)
    low = (lane % 32) < 16
    rot = jnp.where(low, pltpu.roll(x, 112, 1), pltpu.roll(x, 16, 1))
    return x * cos + sign * (rot * sin_signed)


def _in_proj(name, x, gvec, mod3, w_in, layer, cos_t, sin_t, cfg):
    nlb, bps = cfg.nl // TM, cfg.seq // TM

    def body(x_ref, g_ref, m_ref, w_ref, cos_ref, sin_ref, h_ref, qkv_ref, cp_ref):
        hv = _norm_mod(x_ref[...], g_ref[...], m_ref, 0, 1)
        h_ref[...] = hv
        u = jnp.concatenate([_dot(hv, w_ref[j]) for j in range(N_SHARD)], axis=1)
        cos, sin = cos_ref[...], sin_ref[...]
        tiles = []
        for t in range(5):
            y = _rope(u[:, 128 * t:128 * (t + 1)], cos, sin, 1.0)
            tiles.append(y * (HEAD_DIM ** -0.5) if t < 4 else y)
        tiles.append(u[:, 640:768])
        qkv_ref[...] = jnp.concatenate(tiles, axis=1).astype(BF16)
        cp_ref[...] = u[:, 768:IN_W].astype(BF16)

    tab = pl.BlockSpec((TM, 128), lambda i: (jnp.where(i < nlb, i % bps, bps), 0))
    half = pl.BlockSpec((TM, 768), lambda i: (i, 0))
    row = pl.BlockSpec((TM, D), lambda i: (i, 0))
    return pl.pallas_call(
        body, name=name, grid=(cfg.r // TM,),
        in_specs=[row, pl.BlockSpec((1, D), lambda i: (0, 0)), _mod_spec(cfg),
                  pl.BlockSpec((None, N_SHARD, D, IN_SHARD), lambda i: (layer, 0, 0, 0)), tab, tab],
        out_specs=[row, half, half],
        out_shape=[jax.ShapeDtypeStruct((cfg.r, D), BF16), jax.ShapeDtypeStruct((cfg.r, 768), BF16),
                   jax.ShapeDtypeStruct((cfg.r, 768), BF16)],
        compiler_params=_params(1))(x, gvec, mod3, w_in, cos_t, sin_t)


def _att_specs(cfg):
    nlb, ncb = cfg.seq // QB, cfg.ctx // QB

    def qblk(s, qb):
        return jnp.where(qb < nlb, s * nlb + qb, cfg.nl // QB + s * ncb + qb - nlb)

    def near(off, col):
        return pl.BlockSpec((QB, 128), lambda s, qb: (s * nlb + jnp.clip(qb + off, 0, nlb - 1), col))

    def ctxs(col):
        return pl.BlockSpec((cfg.ctx, 128), lambda s, qb: (cfg.nl // cfg.ctx + s, col))

    qspec = pl.BlockSpec((QB, ATTN_W), lambda s, qb: (qblk(s, qb), 0))
    kv = [ctxs(4), ctxs(5), near(-1, 4), near(0, 4), near(1, 4), near(-1, 5), near(0, 5), near(1, 5)]
    return qblk, qspec, kv


def _att_scores(qb, nlb, sink_ref, q_ref, k_refs, v_refs, kh):
    is_lat = qb < nlb
    ii = lax.broadcasted_iota(jnp.int32, (4 * QB, 3 * QB), 0) % QB
    col = lax.broadcasted_iota(jnp.int32, (4 * QB, 3 * QB), 1)
    jj, blk = col % QB, col // QB
    off_p = jnp.where(jnp.logical_and(is_lat, qb >= 1), 0.0, NEG)
    off_c = jnp.where(is_lat, 0.0, NEG)
    off_n = jnp.where(jnp.logical_and(is_lat, qb <= nlb - 2), 0.0, NEG)
    inside = jnp.logical_or(blk == 1, jnp.logical_or(jnp.logical_and(blk == 0, jj >= ii),
                                                     jnp.logical_and(blk == 2, jj <= ii)))
    off = jnp.where(blk == 0, off_p, jnp.where(blk == 1, off_c, off_n))
    q4 = jnp.concatenate([q_ref[:, (4 * kh + g) * HEAD_DIM:(4 * kh + g + 1) * HEAD_DIM] for g in range(4)], axis=0)
    rg = lax.broadcasted_iota(jnp.int32, (4 * QB, 1), 0) // QB
    snk = jnp.where(rg == 0, sink_ref[4 * kh],
                    jnp.where(rg == 1, sink_ref[4 * kh + 1], jnp.where(rg == 2, sink_ref[4 * kh + 2], sink_ref[4 * kh + 3])))
    lanes = slice(kh * HEAD_DIM, (kh + 1) * HEAD_DIM)
    kx, vx = k_refs[0][:, lanes], v_refs[0][:, lanes]
    kl = jnp.concatenate([r[:, lanes] for r in k_refs[1:]], axis=0)
    vl = jnp.concatenate([r[:, lanes] for r in v_refs[1:]], axis=0)
    sx = _dot(q4, kx, "nt")
    sl = jnp.where(inside, _dot(q4, kl, "nt"), NEG) + off
    return q4, snk, (kx, kl), (vx, vl), (sx, sl)


def _att_fwd(name, qkv, sink, ctx_queries, cfg, ride=None):
    nlb, ncb = cfg.seq // QB, cfg.ctx // QB
    qblk, qspec, kvspecs = _att_specs(cfg)

    def body(sink_ref, q_ref, kx_ref, vx_ref, kp_ref, kc_ref, kn_ref, vp_ref, vc_ref, vn_ref, o_ref, lse_ref):
        qb = pl.program_id(1)
        for kh in range(2):
            q4, snk, _, vs, ss = _att_scores(qb, nlb, sink_ref, q_ref, (kx_ref, kp_ref, kc_ref, kn_ref),
                                             (vx_ref, vp_ref, vc_ref, vn_ref), kh)
            m = snk
            for s_ in ss:
                m = jnp.maximum(m, jnp.max(s_, axis=-1, keepdims=True))
            den = jnp.exp(snk - m)
            o4 = jnp.zeros((4 * QB, HEAD_DIM), F32)
            for s_, v_ in zip(ss, vs):
                p = jnp.exp(s_ - m)
                den = den + jnp.sum(p, axis=-1, keepdims=True)
                o4 = o4 + _dot(p, v_)
            o4 = o4 / den
            lse = m + jnp.log(den)
            for g in range(4):
                h = 4 * kh + g
                o_ref[:, h * HEAD_DIM:(h + 1) * HEAD_DIM] = o4[g * QB:(g + 1) * QB].astype(BF16)
                lse_ref[:, h:h + 1] = lse[g * QB:(g + 1) * QB]

    return _hosted_call(
        body, ride, name, (cfg.b, nlb + (ncb if ctx_queries else 0)), [sink] + [qkv] * 9,
        [pl.BlockSpec(memory_space=pltpu.SMEM), qspec, *kvspecs],
        [jax.ShapeDtypeStruct((cfg.r, D), BF16), jax.ShapeDtypeStruct((cfg.r, N_HEADS), F32)],
        [pl.BlockSpec((QB, ATTN_W), lambda s, qb: (qblk(s, qb), 0)),
         pl.BlockSpec((QB, N_HEADS), lambda s, qb: (qblk(s, qb), 0))], [], _params(2))


def _att_bwd(name, qkv, mix, dmix, lse, sink, cos_t, sin_t, ctx_queries, cfg, ride=None):
    nlb, ncb = cfg.seq // QB, cfg.ctx // QB
    nqb = nlb + (ncb if ctx_queries else 0)
    qblk, qspec, kvspecs = _att_specs(cfg)

    def body(sink_ref, q_ref, kx_ref, vx_ref, kp_ref, kc_ref, kn_ref, vp_ref, vc_ref, vn_ref, o_ref, do_ref,
             lse_ref, cosq_ref, sinq_ref, cosk_ref, sink_tab_ref, dq_ref, dkvl_ref, dkvc_ref, dsink_ref,
             accl, accc, dqs):
        s_id, qb = pl.program_id(0), pl.program_id(1)

        @pl.when(qb == 0)
        def _():
            accl[...] = jnp.zeros_like(accl)
            accc[...] = jnp.zeros_like(accc)

        @pl.when(jnp.logical_and(s_id == 0, qb == 0))
        def _():
            dsink_ref[...] = jnp.zeros_like(dsink_ref)

        starts = [pl.multiple_of(jnp.clip(qb + off, 0, nlb - 1) * QB, QB) for off in (-1, 0, 1)]
        for kh in range(2):
            q4, snk, ks, vs, ss = _att_scores(qb, nlb, sink_ref, q_ref, (kx_ref, kp_ref, kc_ref, kn_ref),
                                              (vx_ref, vp_ref, vc_ref, vn_ref), kh)
            lanes = slice(kh * HEAD_DIM, (kh + 1) * HEAD_DIM)
            heads =[slice((4 * kh + g) * HEAD_DIM, (4 * kh + g + 1) * HEAD_DIM) for g in range(4)]
            do4 = jnp.concatenate([do_ref[:, hs] for hs in heads], axis=0)
            o4 = jnp.concatenate([o_ref[:, hs] for hs in heads], axis=0).astype(F32)
            lse4 = jnp.concatenate([lse_ref[:, 4 * kh + g:4 * kh + g + 1] for g in range(4)], axis=0)
            delta = jnp.sum(do4 * o4, axis=-1, keepdims=True)
            dq4 = jnp.zeros((4 * QB, HEAD_DIM), F32)
            dks, dvs = [], []
            for s_, k_, v_ in zip(ss, ks, vs):
                p = jnp.exp(s_ - lse4)
                ds = p * (_dot(do4, v_, "nt") - delta)
                dq4 = dq4 + _dot(ds, k_)
                dks.append(_dot(ds, q4, "tn"))
                dvs.append(_dot(p, do4, "tn"))
            accc[:, lanes] += dks[0]
            accc[:, 128 + kh * HEAD_DIM:128 + (kh + 1) * HEAD_DIM] += dvs[0]
            for t, st in enumerate(starts):
                accl[pl.ds(st, QB), lanes] += dks[1][t * QB:(t + 1) * QB]
                accl[pl.ds(st, QB), 128 + kh * HEAD_DIM:128 + (kh + 1) * HEAD_DIM] += dvs[1][t * QB:(t + 1) * QB]
            dsk = -jnp.exp(snk - lse4) * delta
            for g in range(4):
                h = 4 * kh + g
                dsink_ref[h:h + 1, :] += jnp.broadcast_to(_colsum(dsk[g * QB:(g + 1) * QB]), (1, 128))
                dqs[:, heads[g]] = dq4[g * QB:(g + 1) * QB]
        cos, sin = cosq_ref[...], sinq_ref[...]
        dq_ref[...] = jnp.concatenate(
            [_rope(dqs[:, 128 * t:128 * (t + 1)], cos, sin, -1.0) * (HEAD_DIM ** -0.5) for t in range(4)],
            axis=1).astype(BF16)

        @pl.when(qb == nqb - 1)
        def _():
            dk = _rope(accl[:, 0:128], cosk_ref[...], sink_tab_ref[...], -1.0)
            dkvl_ref[...] = jnp.concatenate([dk, accl[:, 128:256]], axis=1).astype(BF16)
            dkvc_ref[...] = accc[...].astype(BF16)

    rowq = lambda w: pl.BlockSpec((QB, w), lambda s, qb: (qblk(s, qb), 0))
    tabq = pl.BlockSpec((QB, 128), lambda s, qb: (jnp.where(qb < nlb, qb, cfg.seq // QB), 0))
    tabk = pl.BlockSpec((cfg.seq, 128), lambda s, qb: (0, 0))
    return _hosted_call(
        body, ride, name, (cfg.b, nqb), [sink] + [qkv] * 9 + [mix, dmix, lse, cos_t, sin_t, cos_t, sin_t],
        [pl.BlockSpec(memory_space=pltpu.SMEM), qspec, *kvspecs, rowq(ATTN_W), rowq(ATTN_W), rowq(N_HEADS),
         tabq, tabq, tabk, tabk],
        [jax.ShapeDtypeStruct((cfg.r, IN_W), BF16), jax.ShapeDtypeStruct((cfg.nl, 256), BF16),
         jax.ShapeDtypeStruct((cfg.nc, 256), BF16), jax.ShapeDtypeStruct((N_HEADS, 128), F32)],
        [rowq(ATTN_W), pl.BlockSpec((cfg.seq, 256), lambda s, qb: (s, 0)),
         pl.BlockSpec((cfg.ctx, 256), lambda s, qb: (s, 0)), pl.BlockSpec((N_HEADS, 128), lambda s, qb: (0, 0))],
        [pltpu.VMEM((cfg.seq, 256), F32), pltpu.VMEM((cfg.ctx, 256), F32), pltpu.VMEM((QB, ATTN_W), F32)], _params(2))


def _pool_geometry(n, c):
    lane = lax.broadcasted_iota(jnp.int32, (1, POOL_W), 1) // HEAD_DIM
    wl = jnp.where(lane == 0, 1, jnp.where(lane == 1, 2, jnp.where(lane == 2, 4, 8)))
    wr = wl - 1
    t = c * CHUNK + lax.broadcasted_iota(jnp.int32, (CHUNK, POOL_W), 0)
    cnt = (jnp.minimum(t + wr, n - 1) - jnp.maximum(t - wl, 0) + 1).astype(F32)
    return wl, wr, cnt


def _build_phases(src, ph, c):
    for s in range(1, 8):
        ph[s - 1] = src[c * CHUNK + s:c * CHUNK + s + CHUNK + 24, :]


def _window(src, ph, c, off):
    a, s = divmod(off, 8)
    if s == 0:
        return src[c * CHUNK + 8 * a:c * CHUNK + 8 * a + CHUNK, :]
    return ph[s - 1, 8 * a:8 * a + CHUNK, :]


def _conv_chunk(hp, ph, dw_ref, dwb_ref, c):
    _build_phases(hp, ph, c)
    acc = jnp.zeros((CHUNK, CONV_W), F32) + dwb_ref[...]
    for j in range(CONV_K):
        acc = acc + dw_ref[j:j + 1, :] * _window(hp, ph, c, j + 1)
    return acc


def _fill_glu(cp_ref, hp, n):
    hp[0:PAD, :] = jnp.zeros((PAD, CONV_W), F32)
    hp[PAD + n:2 * PAD + n, :] = jnp.zeros((PAD, CONV_W), F32)
    for c in range(n // CHUNK):
        rows = slice(c * CHUNK, (c + 1) * CHUNK)
        a = cp_ref[rows, 0:CONV_W].astype(F32)
        g = cp_ref[rows, CONV_W:2 * CONV_W].astype(F32)
        hp[PAD + c * CHUNK:PAD + (c + 1) * CHUNK, :] = a * _sigmoid(g)


def _fill_pool(cp_ref, pp, n):
    pp[0:PAD, :] = jnp.zeros((PAD, POOL_W), F32)
    pp[PAD + n:2 * PAD + n, :] = jnp.zeros((PAD, POOL_W), F32)
    for c in range(n // CHUNK):
        pp[PAD + c * CHUNK:PAD + (c + 1) * CHUNK, :] = cp_ref[c * CHUNK:(c + 1) * CHUNK, 2 * CONV_W:768].astype(F32)


LV = CHUNK + 2 * PAD
_LEVELS = pltpu.VMEM((3, LV + 16, POOL_W), F32)


def _clear_level_edges(lv):
    for b in range(3):
        lv[b, 0:8] = jnp.zeros((8, POOL_W), F32)
        lv[b, 8 + LV:16 + LV] = jnp.zeros((8, POOL_W), F32)


def _window_sums(src, lv, c, lead):
    lv[0, 8:8 + LV] = src[c * CHUNK:c * CHUNK + LV, :]
    lo = 7 if lead < 0 else 8
    lv[1, 8:8 + LV] = lv[0, lo:lo + LV] + lv[0, lo + 1:lo + 1 + LV]
    group = lax.broadcasted_iota(jnp.int32, (1, POOL_W), 1) // HEAD_DIM
    rows = slice(8 + PAD, 8 + PAD + CHUNK)
    res = lv[1, rows]
    cur = 1
    for g, s in ((1, 1), (2, 2), (3, 4)):
        nxt = 3 - cur
        lv[nxt, 8:8 + LV] = lv[cur, 8 - s:8 - s + LV] + lv[cur, 8 + s:8 + s + LV]
        res = jnp.where(group >= g, lv[nxt, rows], res)
        cur = nxt
    return res


def _pool_chunk(pp, lv, n, c):
    _, _, cnt = _pool_geometry(n, c)
    return _window_sums(pp, lv, c, -1) / cnt - pp[PAD + c * CHUNK:PAD + (c + 1) * CHUNK, :], cnt


def _seq_specs(n, blk_off, width, col=0):
    return pl.BlockSpec((n, width), lambda s: (blk_off + s, col))


def _full(shape):
    return pl.BlockSpec(shape, lambda s: (0,) * len(shape))


_PHASES = pltpu.VMEM((7, CHUNK + 24, CONV_W), F32)


def _convpool_fwd(name, cpin, mix, yconv, prm, n, blk_off, cfg):
    dw, dwb, lng, lnb, wbd, ps = prm
    n_alias = 1 if yconv is None else 2

    def body(*refs):
        cp_ref, dw_ref, dwb_ref, lng_ref, lnb_ref, wbd_ref, ps_ref = refs[:7]
        out_ref, y_ref, hp, pp, ph, lv = refs[7 + n_alias:]
        _fill_glu(cp_ref, hp, n)
        _fill_pool(cp_ref, pp, n)
        _clear_level_edges(lv)
        for c in range(n // CHUNK):
            rows = slice(c * CHUNK, (c + 1) * CHUNK)
            y = _conv_chunk(hp, ph, dw_ref, dwb_ref, c)
            y_ref[rows, :] = y
            d = y - jnp.mean(y, axis=-1, keepdims=True)
            hn = d * lax.rsqrt(jnp.mean(d * d, axis=-1, keepdims=True) + EPS) * lng_ref[...] + lnb_ref[...]
            out_ref[rows, 0:CONV_W] = (hn * _sigmoid(hn)).astype(BF16)
            yp, _ = _pool_chunk(pp, lv, n, c)
            out_ref[rows, CONV_W:2 * CONV_W] = (_dot(yp, wbd_ref[...]) * ps_ref[...]).astype(BF16)

    through = [mix] if yconv is None else [mix, yconv]
    return pl.pallas_call(
        body, name=name, grid=(cfg.b,),
        in_specs=[_seq_specs(n, blk_off, 768), _full((32, CONV_W)), _full((1, CONV_W)), _full((1, CONV_W)),
                  _full((1, CONV_W)), _full((POOL_W, POOL_W)), _full((1, POOL_W))] + [_ANY] * n_alias,
        out_specs=[_seq_specs(n, blk_off, 512, 1), _seq_specs(n, blk_off, CONV_W)],
        out_shape=[jax.ShapeDtypeStruct((cfg.r, D), BF16), jax.ShapeDtypeStruct((cfg.r, CONV_W), F32)],
        scratch_shapes=[pltpu.VMEM((n + 2 * PAD, CONV_W), F32), pltpu.VMEM((n + 2 * PAD, POOL_W), F32), _PHASES, _LEVELS],
        input_output_aliases={7 + i: i for i in range(n_alias)},
        compiler_params=_params(1))(cpin, dw, dwb, lng, lnb, wbd, ps, *through)


_SMALL_SHAPES = [(32, CONV_W), (1, CONV_W), (1, CONV_W), (1, CONV_W), (POOL_W, POOL_W), (1, POOL_W)]


def _convpool_bwd(name, cpin, yconv, dmix, prm, acc_in, n, blk_off, cfg):
    dw, dwb, lng, lnb, wbd, ps = prm
    nch = n // CHUNK

    def body(cp_ref, y_ref, dm_ref, dw_ref, dwb_ref, lng_ref, lnb_ref, wbd_ref, ps_ref, dcp_in,
             a_dw, a_dwb, a_lng, a_lnb, a_wbd, a_ps,
             dcp_ref, o_dw, o_dwb, o_lng, o_lnb, o_wbd, o_ps, hp, dyp, pp, wp, dyv, dwacc, ph, lv):
        s = pl.program_id(0)
        _clear_level_edges(lv)

        @pl.when(s == 0)
        def _():
            for o_, a_ in ((o_dw, a_dw), (o_dwb, a_dwb), (o_lng, a_lng), (o_lnb, a_lnb), (o_wbd, a_wbd), (o_ps, a_ps)):
                o_[...] = a_[...]
            dwacc[...] = jnp.zeros_like(dwacc)

        _fill_glu(cp_ref, hp, n)
        _fill_pool(cp_ref, pp, n)
        for ref in (dyp, wp):
            ref[0:PAD, :] = jnp.zeros((PAD, CONV_W), F32)
            ref[PAD + n:2 * PAD + n, :] = jnp.zeros((PAD, CONV_W), F32)
        for c in range(nch):
            rows = slice(c * CHUNK, (c + 1) * CHUNK)
            y = y_ref[rows, :]
            d = y - jnp.mean(y, axis=-1, keepdims=True)
            rstd = lax.rsqrt(jnp.mean(d * d, axis=-1, keepdims=True) + EPS)
            xh = d * rstd
            hn = xh * lng_ref[...] + lnb_ref[...]
            sg = _sigmoid(hn)
            dhn = dm_ref[rows, 0:CONV_W] * (sg * (1.0 + hn * (1.0 - sg)))
            o_lnb[...] += _colsum(dhn)
            o_lng[...] += _colsum(dhn * xh)
            dxh = dhn * lng_ref[...]
            dy = rstd * (dxh - jnp.mean(dxh, axis=-1, keepdims=True) - xh * jnp.mean(dxh * xh, axis=-1, keepdims=True))
            o_dwb[...] += _colsum(dy)
            dyp[PAD + c * CHUNK:PAD + (c + 1) * CHUNK, :] = dy
            _build_phases(hp, ph, c)
            for j in range(CONV_K):
                prod = dy * _window(hp, ph, c, j + 1)
                dwacc[8 * j:8 * j + 8, :] += jnp.sum(prod.reshape(CHUNK // 8, 8, CONV_W), axis=0)
            yp, cnt = _pool_chunk(pp, lv, n, c)
            dz = dm_ref[rows, CONV_W:2 * CONV_W]
            o_ps[...] += _colsum(dz * _dot(yp, wbd_ref[...]))
            dzs = dz * ps_ref[...]
            o_wbd[...] += _dot(yp, dzs, "tn")
            dv = _dot(dzs, wbd_ref[...], "nt")
            dyv[rows, :] = dv
            wp[PAD + c * CHUNK:PAD + (c + 1) * CHUNK, :] = dv / cnt
        for c in range(nch):
            rows = slice(c * CHUNK, (c + 1) * CHUNK)
            _build_phases(dyp, ph, c)
            dh = jnp.zeros((CHUNK, CONV_W), F32)
            for j in range(CONV_K):
                dh = dh + dw_ref[j:j + 1, :] * _window(dyp, ph, c, 31 - j)
            a = cp_ref[rows, 0:CONV_W].astype(F32)
            sg = _sigmoid(cp_ref[rows, CONV_W:2 * CONV_W].astype(F32))
            dcp_ref[rows, 0:CONV_W] = (dh * sg).astype(BF16)
            dcp_ref[rows, CONV_W:2 * CONV_W] = (dh * a * sg * (1.0 - sg)).astype(BF16)
            dcp_ref[rows, 2 * CONV_W:768] = (_window_sums(wp, lv, c, 1) - dyv[rows, :]).astype(BF16)

        @pl.when(s == cfg.b - 1)
        def _():
            for j in range(CONV_K):
                o_dw[j:j + 1, :] += _colsum(dwacc[8 * j:8 * j + 8, :])

    small_specs = [_full(sh) for sh in _SMALL_SHAPES]
    return pl.pallas_call(
        body, name=name, grid=(cfg.b,),
        in_specs=[_seq_specs(n, blk_off, 768), _seq_specs(n, blk_off, CONV_W), _seq_specs(n, blk_off, 512, 1),
                  *small_specs, _ANY, *small_specs],
        out_specs=[_seq_specs(n, blk_off, 768, 1), *small_specs],
        out_shape=[jax.ShapeDtypeStruct((cfg.r, IN_W), BF16)] + [jax.ShapeDtypeStruct(sh, F32) for sh in _SMALL_SHAPES],
        scratch_shapes=[pltpu.VMEM((n + 2 * PAD, CONV_W), F32), pltpu.VMEM((n + 2 * PAD, CONV_W), F32),
                        pltpu.VMEM((n + 2 * PAD, POOL_W), F32), pltpu.VMEM((n + 2 * PAD, POOL_W), F32),
                        pltpu.VMEM((n, POOL_W), F32), pltpu.VMEM((8 * 32, CONV_W), F32), _PHASES, _LEVELS],
        input_output_aliases={9: 0}, compiler_params=_params(1))(cpin, yconv, dmix, dw, dwb, lng, lnb, wbd, ps, *acc_in)


def _place_kv(name, du, dkvl, dkvc, with_ctx, cfg):
    nlb = cfg.nl // TM

    def body(l_ref, c_ref, du_in, o_ref):
        i = pl.program_id(0)
        o_ref[...] = jnp.where(i < nlb, l_ref[...], c_ref[...])

    return pl.pallas_call(
        body, name=name, grid=(cfg.r // TM if with_ctx else nlb,),
        in_specs=[pl.BlockSpec((TM, 256), lambda i: (jnp.minimum(i, nlb - 1), 0)),
                  pl.BlockSpec((TM, 256), lambda i: (jnp.maximum(i - nlb, 0), 0)), _ANY],
        out_specs=pl.BlockSpec((TM, 256), lambda i: (i, 2)), out_shape=jax.ShapeDtypeStruct((cfg.r, IN_W), BF16),
        input_output_aliases={2: 0}, compiler_params=_params(1))(dkvl, dkvc, du)


def _place_ctx_kv_only(name, du, dkvc, cfg):
    nlb = cfg.nl // TM

    def body(c_ref, du_in, o_ref):
        o_ref[...] = jnp.zeros_like(o_ref)
        o_ref[:, ATTN_W:ATTN_W + 256] = c_ref[...]

    return pl.pallas_call(
        body, name=name, grid=(cfg.nc // TM,), in_specs=[pl.BlockSpec((TM, 256), lambda i: (i, 0)), _ANY],
        out_specs=pl.BlockSpec((TM, IN_W), lambda i: (nlb + i, 0)), out_shape=jax.ShapeDtypeStruct((cfg.r, IN_W), BF16),
        input_output_aliases={1: 0}, compiler_params=_params(1))(dkvc, du)


def _norm_fwd(name, x, gvec, mod3, ish, isc, nblk, cfg):
    def body(x_ref, g_ref, m_ref, o_ref):
        o_ref[...] = _norm_mod(x_ref[...], g_ref[...], m_ref, ish, isc)

    row = pl.BlockSpec((TM, D), lambda i: (i, 0))
    return pl.pallas_call(
        body, name=name, grid=(nblk,),
        in_specs=[row, pl.BlockSpec((1, D), lambda i: (0, 0)), _mod_spec(cfg)], out_specs=row,
        out_shape=jax.ShapeDtypeStruct((cfg.r, D), BF16), compiler_params=_params(1))(x, gvec, mod3)


def _ffn_in(name, h, w_ffn_in, layer, nblk, cfg, ride=None):
    def body(h_ref, wg_ref, wu_ref, fac_ref, act_ref):
        hv = h_ref[...]
        g = _dot(hv, wg_ref[...])
        u = _dot(hv, wu_ref[...])
        s = _sigmoid(g)
        gs = g * s
        fac_ref[0] = ((s + gs * (1.0 - s)) * u).astype(BF16)
        fac_ref[1] = gs.astype(BF16)
        act_ref[...] = (gs * u).astype(BF16)

    wspec = lambda base: pl.BlockSpec((None, None, D, HALF_FF), lambda j, i: (layer, base + j, 0, 0))
    return _hosted_call(
        body, ride, name, (2, nblk), [h, w_ffn_in, w_ffn_in],
        [pl.BlockSpec((TM, D), lambda j, i: (i, 0)), wspec(0), wspec(2)],
        [jax.ShapeDtypeStruct((2, cfg.r, D_FF), BF16), jax.ShapeDtypeStruct((cfg.r, D_FF), BF16)],
        [pl.BlockSpec((2, TM, HALF_FF), lambda j, i: (0, i, j)), pl.BlockSpec((TM, HALF_FF), lambda j, i: (i, j))],
        [], _params(2))


def _row_block(rows, cols, max_bytes=1 << 20):
    best = 16
    for t in range(16, rows + 1, 16):
        if rows % t == 0 and t * cols * 4 <= max_bytes:
            best = t
    assert rows % best == 0
    return best


def _pair_add(name, own32, recv, c_idx):
    _, _, s0, s1 = own32.shape
    tr = _row_block(s0, s1)

    def body(c_ref, a_ref, b_ref, o_ref):
        o_ref[...] = (a_ref[...].astype(F32) + b_ref[...].astype(F32)).astype(BF16)

    grid_spec = pltpu.PrefetchScalarGridSpec(
        num_scalar_prefetch=1, grid=(N_SHARD * s0 // tr,),
        in_specs=[pl.BlockSpec((None, tr, s1), lambda i, c: (c[0], i, 0)), pl.BlockSpec((tr, s1), lambda i, c: (i, 0))],
        out_specs=pl.BlockSpec((tr, s1), lambda i, c: (i, 0)))
    out = pl.pallas_call(body, name=name, grid_spec=grid_spec, out_shape=jax.ShapeDtypeStruct((N_SHARD * s0, s1), BF16),
                         compiler_params=_params(1))(c_idx, own32.reshape(2, N_SHARD * s0, s1), recv.reshape(N_SHARD * s0, s1))
    return out.reshape(N_SHARD, s0, s1)


def _shard_sum(name, pair_sum, recv, jc_idx):
    _, s0, s1 = pair_sum.shape
    tr = _row_block(s0, s1)

    def body(jc_ref, a_ref, b_ref, o_ref):
        o_ref[...] = ((a_ref[...].astype(F32) + b_ref[0].astype(F32)) + b_ref[1].astype(F32)) + b_ref[2].astype(F32)

    grid_spec = pltpu.PrefetchScalarGridSpec(
        num_scalar_prefetch=1, grid=(s0 // tr,),
        in_specs=[pl.BlockSpec((None, tr, s1), lambda i, jc: (jc[0], i, 0)), pl.BlockSpec((3, tr, s1), lambda i, jc: (0, i, 0))],
        out_specs=pl.BlockSpec((None, tr, s1), lambda i, jc: (jc[1], i, 0)))
    return pl.pallas_call(body, name=name, grid_spec=grid_spec, out_shape=jax.ShapeDtypeStruct((2, s0, s1), F32),
                          compiler_params=_params(1))(jc_idx, pair_sum, recv)


def _adamw_math(w, g, m, v):
    m = ADAM_B1 * m + (1.0 - ADAM_B1) * g
    v = ADAM_B2 * v + (1.0 - ADAM_B2) * (g * g)
    m_hat = m / (1.0 - ADAM_B1 ** ADAM_STEP)
    v_hat = v / (1.0 - ADAM_B2 ** ADAM_STEP)
    delta = -ADAM_LR * (m_hat / (jnp.sqrt(v_hat) + ADAM_EPS) + ADAM_WD * w)
    return delta, m, v


def _adamw(name, w, g, m, v):
    rows, cols = w.shape
    tr = rows if rows % 16 else _row_block(rows, cols, 1 << 19)

    def body(w_ref, g_ref, m_ref, v_ref, d_ref, mo_ref, vo_ref):
        d, mn, vn = _adamw_math(w_ref[...], g_ref[...], m_ref[...], v_ref[...])
        d_ref[...] = d
        mo_ref[...] = mn
        vo_ref[...] = vn

    spec = pl.BlockSpec((tr, cols), lambda i: (i, 0))
    shape = jax.ShapeDtypeStruct((rows, cols), F32)
    return pl.pallas_call(body, name=name, grid=(rows // tr,), in_specs=[spec] * 4, out_specs=[spec] * 3,
                          out_shape=[shape] * 3, compiler_params=_params(1))(w, g, m, v)


def _adamw_layers(name, w, g_layers, m, v):
    rows, cols = w.shape
    s0 = rows // 2
    tr = _row_block(s0, cols, 1 << 19)
    nb = s0 // tr

    def body(w_ref, g0_ref, g1_ref, m_ref, v_ref, g_ref, d_ref, mo_ref, vo_ref):
        g = jnp.where(pl.program_id(0) < nb, g0_ref[...], g1_ref[...])
        d, mn, vn = _adamw_math(w_ref[...], g, m_ref[...], v_ref[...])
        g_ref[...] = g
        d_ref[...] = d
        mo_ref[...] = mn
        vo_ref[...] = vn

    spec = pl.BlockSpec((tr, cols), lambda i: (i, 0))
    shape = jax.ShapeDtypeStruct((rows, cols), F32)
    return pl.pallas_call(
        body, name=name, grid=(2 * nb,),
        in_specs=[spec, pl.BlockSpec((tr, cols), lambda i: (jnp.minimum(i, nb - 1), 0)),
                  pl.BlockSpec((tr, cols), lambda i: (jnp.maximum(i - nb, 0), 0)), spec, spec],
        out_specs=[spec] * 4, out_shape=[shape] * 4, compiler_params=_params(1))(w, g_layers[0], g_layers[1], m, v)


def _position():
    return lax.axis_index("x"), lax.axis_index("y"), lax.axis_index("c")


def _other_chips(x, y):
    return [(1 - x, y), (x, 1 - y), (1 - x, 1 - y)]


def _run_ride(name, ride):
    n_in, n_out = len(ride.ins), len(ride.out_shape)

    def body(*refs):
        parts = (refs[:n_in], refs[n_in:n_in + n_out], refs[n_in + n_out:])
        ride.start(*parts)
        ride.finish(*parts)

    return pl.pallas_call(
        body, name=name, in_specs=[_ANY] * n_in, out_specs=[_ANY] * n_out, out_shape=ride.out_shape,
        scratch_shapes=ride.scratch, compiler_params=pltpu.CompilerParams(vmem_limit_bytes=VMEM_LIMIT_V7X))(*ride.ins)


def _gather_ride(shards):
    n = len(shards)

    def copies(ins, outs, scr):
        ssem, rsem = scr[n], scr[n + 1]
        x, y, c = _position()
        me, sibling = 2 * x + y, (x, y, 1 - c)

        def remote(src, dst, i, dev):
            return pltpu.make_async_remote_copy(src, dst, ssem.at[i], rsem.at[i], device_id=dev, device_id_type=_MESH)

        fetch_out, fetch_in, pass_out, pass_in = [], [], [], []
        for a, (src, dst) in enumerate(zip(ins, outs)):
            for k, (px, py) in enumerate(_other_chips(x, y)):
                j, i1, i2 = 2 * px + py, 3 * a + k, 3 * n + 3 * a + k
                fetch_out.append(remote(src.at[c], dst.at[me, c], i1, (px, py, c)))
                fetch_in.append(remote(src.at[c], dst.at[j, c], i1, (px, py, c)))
                pass_out.append(remote(dst.at[j, c], dst.at[j, c], i2, sibling))
                pass_in.append(remote(dst.at[j, 1 - c], dst.at[j, 1 - c], i2, sibling))
        return me, fetch_out, fetch_in, pass_out, pass_in

    def start(ins, outs, scr):
        bufs, lsem = scr[:n], scr[n + 2]
        me, fetch_out, _, _, _ = copies(ins, outs, scr)
        for cp in fetch_out:
            cp.start()
        loads = []
        for a, (src, buf) in enumerate(zip(ins, bufs)):
            ld = pltpu.make_async_copy(src, buf, lsem.at[2 * a])
            ld.start()
            loads.append(ld)
        for a, (ld, buf, dst) in enumerate(zip(loads, bufs, outs)):
            ld.wait()
            st = pltpu.make_async_copy(buf, dst.at[me], lsem.at[2 * a + 1])
            st.start()
            st.wait()

    def finish(ins, outs, scr):
        _, fetch_out, fetch_in, pass_out, pass_in = copies(ins, outs, scr)
        for arrived, onward in zip(fetch_in, pass_out):
            arrived.wait_recv()
            onward.start()
        for cp in pass_in:
            cp.wait_recv()
        for cp in fetch_out + pass_out:
            cp.wait_send()

    return _Ride(list(shards), [jax.ShapeDtypeStruct((N_SHARD,) + s.shape, s.dtype) for s in shards],
                 [pltpu.VMEM(s.shape, s.dtype) for s in shards]
                 + [pltpu.SemaphoreType.DMA((6 * n,)), pltpu.SemaphoreType.DMA((6 * n,)), pltpu.SemaphoreType.DMA((2 * n,))],
                 start, finish)


def _comm(name, ins, out_shape, n_remote, plan):
    n_in, n_out = len(ins), len(out_shape)

    def body(*refs):
        plan(refs[:n_in], refs[n_in:n_in + n_out], *refs[n_in + n_out:])

    return pl.pallas_call(
        body, name=name, in_specs=[_ANY] * n_in, out_specs=[_ANY] * n_out, out_shape=out_shape,
        scratch_shapes=[pltpu.SemaphoreType.DMA((n_remote,)), pltpu.SemaphoreType.DMA((n_remote,))])(*ins)


def _send_other_half(name, grads_bf):
    n = len(grads_bf)

    def plan(ins, outs, ssem, rsem):
        x, y, c = _position()
        started = []
        for a, (src, dst) in enumerate(zip(ins, outs)):
            cp = pltpu.make_async_remote_copy(src.at[1 - c], dst, ssem.at[a], rsem.at[a], device_id=(x, y, 1 - c),
                                              device_id_type=_MESH)
            cp.start()
            started.append(cp)
        for cp in started:
            cp.wait_recv()
        for cp in started:
            cp.wait_send()

    shapes = [jax.ShapeDtypeStruct(s.shape[1:], s.dtype) for s in grads_bf]
    return _comm(name, grads_bf, shapes, n, plan)


def _exchange_ride(pair_sums):
    n = len(pair_sums)

    def copies(ins, outs, scr):
        ssem, rsem = scr
        x, y, c = _position()
        return [pltpu.make_async_remote_copy(src.at[2 * px + py], dst.at[k], ssem.at[3 * a + k], rsem.at[3 * a + k],
                                             device_id=(px, py, c), device_id_type=_MESH)
                for a, (src, dst) in enumerate(zip(ins, outs)) for k, (px, py) in enumerate(_other_chips(x, y))]

    def start(ins, outs, scr):
        for cp in copies(ins, outs, scr):
            cp.start()

    def finish(ins, outs, scr):
        for cp in copies(ins, outs, scr):
            cp.wait_recv()
        for cp in copies(ins, outs, scr):
            cp.wait_send()

    return _Ride(list(pair_sums), [jax.ShapeDtypeStruct((3,) + s.shape[1:], s.dtype) for s in pair_sums],
                 [pltpu.SemaphoreType.DMA((3 * n,)), pltpu.SemaphoreType.DMA((3 * n,))], start, finish)


def _swap_reduced(name, grads):
    n = len(grads)

    def body(*refs):
        ins, outs, ssem, rsem = refs[:n], refs[n:2 * n], refs[2 * n], refs[2 * n + 1]
        x, y, c = _position()
        sent = []
        for a, (src, dst) in enumerate(zip(ins, outs)):
            cp = pltpu.make_async_remote_copy(src.at[c], dst.at[c], ssem.at[a], rsem.at[a], device_id=(x, y, 1 - c),
                                              device_id_type=_MESH)
            cp.start()
            sent.append(cp)
        for a, (src, dst) in enumerate(zip(ins, outs)):
            pltpu.make_async_remote_copy(src.at[1 - c], dst.at[1 - c], ssem.at[a], rsem.at[a], device_id=(x, y, 1 - c),
                                         device_id_type=_MESH).wait_recv()
        for cp in sent:
            cp.wait_send()

    return pl.pallas_call(
        body, name=name, in_specs=[_ANY] * n, out_specs=[_ANY] * n,
        out_shape=[jax.ShapeDtypeStruct(g.shape, g.dtype) for g in grads],
        scratch_shapes=[pltpu.SemaphoreType.DMA((n,)), pltpu.SemaphoreType.DMA((n,))],
        input_output_aliases={a: a for a in range(n)})(*grads)


_FLIPS = [(dx, dy, dc) for dx in (0, 1) for dy in (0, 1) for dc in (0, 1) if dx + dy + dc]
_VMEM = pl.BlockSpec(memory_space=pltpu.VMEM)


def _to_all(src_of, dst, ssem, rsem):
    x, y, c = _position()
    me = 4 * x + 2 * y + c
    peers = [((x + dx) % 2, (y + dy) % 2, (c + dc) % 2) for dx, dy, dc in _FLIPS]
    sent = []
    for k, (px, py, pc) in enumerate(peers):
        cp = pltpu.make_async_remote_copy(src_of(2 * px + py), dst.at[me], ssem.at[k], rsem.at[k],
                                          device_id=(px, py, pc), device_id_type=_MESH)
        cp.start()
        sent.append(cp)
    for k, (px, py, pc) in enumerate(peers):
        pltpu.make_async_remote_copy(src_of(2 * px + py), dst.at[4 * px + 2 * py + pc], ssem.at[k], rsem.at[k],
                                     device_id=(px, py, pc), device_id_type=_MESH).wait_recv()
    for cp in sent:
        cp.wait_send()
    return me, 2 * x + y


def _share_small(name, block, total):
    shape = block.shape

    def body(in_ref, out_ref, *scratch):
        buf, ssem, rsem = (out_ref,) + scratch if not total else scratch
        me, _ = _to_all(lambda chip: in_ref, buf, ssem, rsem)
        buf[me] = in_ref[...]
        if total:
            acc = buf[0]
            for d in range(1, 8):
                acc = acc + buf[d]
            out_ref[...] = acc

    sems = [pltpu.SemaphoreType.DMA((7,)), pltpu.SemaphoreType.DMA((7,))]
    return pl.pallas_call(
        body, name=name, in_specs=[_VMEM], out_specs=_VMEM,
        out_shape=jax.ShapeDtypeStruct(shape if total else (8,) + shape, F32),
        scratch_shapes=([pltpu.VMEM((8,) + shape, F32)] if total else []) + sems,
        compiler_params=pltpu.CompilerParams(vmem_limit_bytes=VMEM_LIMIT_V7X))(block)


def _mod_rows_exchange(mv):
    def body(mv_ref, out_ref, ssem, rsem):
        x, y, c = _position()
        me = 2 * x + y
        out_ref[me] = mv_ref[4 * x + 2 * y + c]
        sent = []
        for k, (px, py) in enumerate(_other_chips(x, y)):
            cp = pltpu.make_async_remote_copy(mv_ref.at[4 * px + 2 * py + c], out_ref.at[me], ssem.at[k], rsem.at[k],
                                              device_id=(px, py, c), device_id_type=_MESH)
            cp.start()
            sent.append(cp)
        for k, (px, py) in enumerate(_other_chips(x, y)):
            pltpu.make_async_remote_copy(mv_ref.at[0], out_ref.at[2 * px + py], ssem.at[k], rsem.at[k],
                                         device_id=(px, py, c), device_id_type=_MESH).wait_recv()
        for cp in sent:
            cp.wait_send()

    return pl.pallas_call(
        body, name="mod_rows_exchange", in_specs=[_VMEM], out_specs=_VMEM,
        out_shape=jax.ShapeDtypeStruct((N_SHARD,) + mv.shape[1:], F32),
        scratch_shapes=[pltpu.SemaphoreType.DMA((3,)), pltpu.SemaphoreType.DMA((3,))],
        compiler_params=pltpu.CompilerParams(vmem_limit_bytes=VMEM_LIMIT_V7X))(mv)


def _mod_grad_exchange(dmj, dm_rows):
    def body(dmj_ref, rows_ref, out_ref, bias_ref, ssem, rsem):
        me, chip = _to_all(lambda j: dmj_ref.at[j], out_ref, ssem, rsem)
        out_ref[me] = dmj_ref[chip]
        for l in range(2):
            bias_ref[l] = _colsum(rows_ref[l])

    return pl.pallas_call(
        body, name="mod_grad_exchange", in_specs=[_VMEM, _VMEM], out_specs=[_VMEM, _VMEM],
        out_shape=[jax.ShapeDtypeStruct((8,) + dmj.shape[1:], F32), jax.ShapeDtypeStruct((2, 1, dm_rows.shape[-1]), F32)],
        scratch_shapes=[pltpu.SemaphoreType.DMA((7,)), pltpu.SemaphoreType.DMA((7,))],
        compiler_params=pltpu.CompilerParams(vmem_limit_bytes=VMEM_LIMIT_V7X))(dmj, dm_rows)


def _pack(arrays):
    flat = jnp.concatenate([a.reshape(-1).astype(F32) for a in arrays])
    total = flat.shape[0]
    rows = -(-total // (8 * PACK_LANES)) * 8
    return jnp.pad(flat, (0, rows * PACK_LANES - total)).reshape(rows, PACK_LANES)


def _unpack(pack, shapes):
    flat, out, pos = pack.reshape(-1), [], 0
    for sh in shapes:
        size = int(np.prod(sh)) if len(sh) else 1
        out.append(flat[pos:pos + size].reshape(sh))
        pos += size
    return out


def _block_diag(pw):
    out = jnp.zeros((POOL_W, POOL_W), pw.dtype)
    for g in range(4):
        out = out.at[g * 64:(g + 1) * 64, g * 64:(g + 1) * 64].set(pw[g])
    return out


def _local_step(x, ctx, small, mod3s, loss_target, comm):
    cfg = _Cfg(x.shape[0], x.shape[1], ctx.shape[1])
    assert cfg.seq % TM == 0 and cfg.nc % TM == 0 and cfg.seq % cfg.ctx == 0 and cfg.ctx % CHUNK == 0
    nb_all, nb_lat = cfg.r // TM, cfg.nl // TM
    last = 1
    wf, big = comm.wf, comm.grads
    cos_t, sin_t = _rope_tables(cfg.seq)
    xs = jnp.concatenate([x.reshape(cfg.nl, D), ctx.reshape(cfg.nc, D)], axis=0)
    row = lambda w: pl.BlockSpec((TM, w), lambda i, j, k: (i, 0))
    mod3_spec = pl.BlockSpec((None, 6, D), lambda i, j, k: (cfg.mod_row(i), 0, 0))
    whole = lambda rows: pl.BlockSpec((None, rows, D), lambda i, j, k: (0, 0, 0))

    def conv_params(l):
        dw = jnp.pad(wf[l]["conv_dw"], ((0, 1), (0, 0)))
        return (dw, small["conv_dw_b"][l][None], small["conv_ln_g"][l][None], small["conv_ln_b"][l][None],
                _block_diag(small["pool_w"][l]).astype(BF16), small["pool_scale"][l][None])

    def residual_epi(ig):
        def epi(acc, ex, outs):
            x_ref, m_ref = ex
            outs[0][...] = x_ref[...] + m_ref[ig:ig + 1, :] * acc
            outs[1][...] = acc.astype(BF16)
        return epi

    def hosted(name, call):
        outs, got = call(comm.ride(name))
        comm.landed(name, got)
        return outs

    saved = []
    for l in range(2):
        nb = nb_lat if l == last else nb_all
        wl = wf[l]
        mod3 = mod3s[l]
        h1, qkv, cpin = _in_proj(f"in_proj{l}", xs, small["norm1_g"][l][None], mod3, wl["w_in"], 0, cos_t, sin_t, cfg)
        mix, lse = hosted(f"att_fwd{l}", lambda ride: _att_fwd(f"att_fwd{l}", qkv, small["attn_sink"][l], l != last, cfg, ride))
        prm = conv_params(l)
        mix, yconv = _convpool_fwd(f"convpool_fwd_lat{l}", cpin, mix, None, prm, cfg.seq, 0, cfg)
        if l != last:
            mix, yconv = _convpool_fwd(f"convpool_fwd_ctx{l}", cpin, mix, yconv, prm, cfg.ctx, cfg.nl // cfg.ctx, cfg)
        x1, y1 = _mm(f"out_proj{l}", "nn", (nb, 1, 1), mix, wl["w_out"].reshape(1, D, D), row(D), whole(D),
                     [jax.ShapeDtypeStruct((cfg.r, D), F32), jax.ShapeDtypeStruct((cfg.r, D), BF16)], [row(D), row(D)],
                     extras=[xs, mod3], extra_specs=[row(D), mod3_spec], epi=residual_epi(2))
        h2 = _norm_fwd(f"norm2_fwd{l}", x1, small["norm2_g"][l][None], mod3, 3, 4, nb, cfg)
        gu, act = hosted(f"ffn_in{l}", lambda ride: _ffn_in(f"ffn_in{l}", h2, wl["w_ffn_in"], 0, nb, cfg, ride))
        x2, y2 = _mm(f"ffn_out{l}", "nn", (nb, 1, 1), act, wl["w_ffn_out"].reshape(1, D_FF, D), row(D_FF), whole(D_FF),
                     [jax.ShapeDtypeStruct((cfg.r, D), F32), jax.ShapeDtypeStruct((cfg.r, D), BF16)], [row(D), row(D)],
                     extras=[x1, mod3], extra_specs=[row(D), mod3_spec], epi=residual_epi(5))
        saved.append(dict(mod3=mod3, x0=xs, h1=h1, qkv=qkv, cpin=cpin, mix=mix, yconv=yconv, lse=lse, y1=y1, x1=x1,
                          h2=h2, gu=gu, act=act, y2=y2, prm=prm))
        xs = x2

    dx, loss, d_final_g, dy2, dg2 = _loss_head(xs, loss_target.reshape(cfg.nl, D), small["final_g"][None],
                                               saved[last]["y2"], saved[last]["mod3"], 5, cfg)

    sg = {k: [None, None] for k in ("norm1_g", "norm2_g", "conv_dw", "conv_dw_b", "conv_ln_g", "conv_ln_b",
                                    "attn_sink", "pool_w", "pool_scale")}
    dms = [None, None]

    def swiglu_bwd_epi(acc, ex, outs):
        outs[0][0] = (acc * ex[0][0].astype(F32)).astype(BF16)
        outs[0][1] = (acc * ex[0][1].astype(F32)).astype(BF16)

    def halves_epi(acc, ex, outs):
        h = acc.shape[0] // 2
        outs[0][0] = acc[:h].astype(BF16)
        outs[0][1] = acc[h:].astype(BF16)

    def row_shards_epi(n):
        def epi(acc, ex, outs):
            s0 = acc.shape[0] // n
            h = s0 // 2
            for t in range(n):
                for half in range(2):
                    outs[0][half, t] = acc[t * s0 + half * h:t * s0 + (half + 1) * h].astype(BF16)
        return epi

    def col_shards_epi(acc, ex, outs):
        h = acc.shape[0] // 2
        for j in range(N_SHARD):
            for half in range(2):
                outs[0][half, j] = acc[half * h:(half + 1) * h, j * IN_SHARD:(j + 1) * IN_SHARD].astype(BF16)

    for l in (1, 0):
        sv = saved[l]
        mod3 = sv["mod3"]
        nb = nb_lat if l == last else nb_all
        tr = _dw_rows(nb * TM)
        steps = nb * TM // tr
        wl = wf[l]
        gu_spec = pl.BlockSpec((2, TM, HALF_FF), lambda j, i, k: (0, i, j))
        df = _mm(f"ffn_out_bwd{l}", "nt", (2, nb, 1), dy2, wl["w_ffn_out"].reshape(1, D_FF, D),
                 pl.BlockSpec((TM, D), lambda j, i, k: (i, 0)), pl.BlockSpec((None, HALF_FF, D), lambda j, i, k: (0, j, 0)),
                 [jax.ShapeDtypeStruct((2, cfg.r, D_FF), BF16)], [gu_spec], extras=[sv["gu"]], extra_specs=[gu_spec],
                 epi=swiglu_bwd_epi)[0]
        big[l]["w_ffn_out"] = _mm_dw(
            f"dw_ffn_out{l}", sv["act"], dy2, pl.BlockSpec((tr, HALF_FF), lambda i, j, k: (k, j)),
            pl.BlockSpec((tr, D), lambda i, j, k: (k, 0)), (2, N_SHARD, D_FF // 8, D),
            pl.BlockSpec((2, 2, D_FF // 8, D), lambda i, j, k: (0, j, 0, 0)), 2, (HALF_FF, D), steps, row_shards_epi(2))[0]
        ride = comm.ride(f"dw_ffn_in{l}")
        res = _mm_dw(f"dw_ffn_in{l}", sv["h2"], df, pl.BlockSpec((tr, D), lambda i, j, k: (k, 0)),
                     pl.BlockSpec((None, tr, HALF_FF), lambda i, j, k: (j // 2, k, j % 2)), (2, N_SHARD, D // 2, HALF_FF),
                     pl.BlockSpec((2, None, D // 2, HALF_FF), lambda i, j, k: (0, j, 0, 0)), N_SHARD, (D, HALF_FF), steps,
                     halves_epi, ride=ride)
        res, got = res if ride is not None else (res, [])
        big[l]["w_ffn_in"] = res[0]
        comm.landed(f"dw_ffn_in{l}", got)
        dx1, dsh2, dsc2, dn2, dy1, dg1 = _proj_norm_bwd(
            f"ffn_in_bwd{l}", df, pl.BlockSpec((2, TM, D_FF), lambda i: (0, i, 0)),
            lambda a_ref, j: a_ref[j // 2, :, (j % 2) * HALF_FF:(j % 2 + 1) * HALF_FF], wl["w_ffn_in"],
            sv["x1"], dx, small["norm2_g"][l][None], mod3, 4, nb, False, (sv["y1"], mod3, 2), cfg)
        dmix = _mm(f"out_proj_bwd{l}", "nt", (nb, 1, 1), dy1, wl["w_out"].reshape(1, D, D), row(D), whole(D),
                   [jax.ShapeDtypeStruct((cfg.r, D), F32)], [row(D)])[0]
        big[l]["w_out"] = _mm_dw(
            f"dw_out{l}", sv["mix"], dy1, pl.BlockSpec((tr, D), lambda i, j, k: (k, 0)),
            pl.BlockSpec((tr, D), lambda i, j, k: (k, 0)), (2, N_SHARD, D // 8, D),
            pl.BlockSpec((2, N_SHARD, D // 8, D), lambda i, j, k: (0, 0, 0, 0)), 1, (D, D), steps, row_shards_epi(N_SHARD))[0]
        du, dkvl, dkvc, dsink = hosted(f"att_bwd{l}", lambda ride: _att_bwd(
            f"att_bwd{l}", sv["qkv"], sv["mix"], dmix, sv["lse"], small["attn_sink"][l], cos_t, sin_t, l != last, cfg, ride))
        acc = [du] + [jnp.zeros(sh, F32) for sh in _SMALL_SHAPES]
        acc = _convpool_bwd(f"convpool_bwd_lat{l}", sv["cpin"], sv["yconv"], dmix, sv["prm"], acc, cfg.seq, 0, cfg)
        if l != last:
            acc = _convpool_bwd(f"convpool_bwd_ctx{l}", sv["cpin"], sv["yconv"], dmix, sv["prm"], acc, cfg.ctx,
                                cfg.nl // cfg.ctx, cfg)
        du, g_dw, g_dwb, g_lng, g_lnb, g_wbd, g_ps = acc
        du = _place_kv(f"place_kv{l}", du, dkvl, dkvc, l != last, cfg)
        if l == last:
            du = _place_ctx_kv_only(f"place_ctx_kv{l}", du, dkvc, cfg)
        sg["attn_sink"][l] = dsink[:, 0]
        sg["conv_dw"][l], sg["conv_dw_b"][l], sg["conv_ln_g"][l], sg["conv_ln_b"][l] = g_dw[:CONV_K], g_dwb[0], g_lng[0], g_lnb[0]
        sg["pool_w"][l] = jnp.stack([g_wbd[g * 64:(g + 1) * 64, g * 64:(g + 1) * 64] for g in range(4)])
        sg["pool_scale"][l] = g_ps[0]
        tr_all = _dw_rows(cfg.r)
        big[l]["w_in"] = _mm_dw(
            f"dw_in{l}", sv["h1"], du, pl.BlockSpec((tr_all, D), lambda i, j, k: (k, 0)),
            pl.BlockSpec((tr_all, IN_W), lambda i, j, k: (k, 0)), (2, N_SHARD, D // 2, IN_SHARD),
            pl.BlockSpec((2, N_SHARD, D // 2, IN_SHARD), lambda i, j, k: (0, 0, 0, 0)), 1, (D, IN_W), cfg.r // tr_all,
            col_shards_epi)[0]
        below = (saved[l - 1]["y2"], saved[l - 1]["mod3"], 5) if l > 0 else None
        res = _proj_norm_bwd(
            f"in_proj_bwd{l}", du, pl.BlockSpec((TM, IN_W), lambda i: (i, 0)),
            lambda a_ref, j: a_ref[:, j * IN_SHARD:(j + 1) * IN_SHARD], wl["w_in"],
            sv["x0"], dx1, small["norm1_g"][l][None], mod3, 1, nb_all, l == last, below, cfg)
        dx, dsh1, dsc1, dn1 = res[:4]
        sg["norm1_g"][l], sg["norm2_g"][l] = dn1[0], dn2[0]
        parts = [dsh1, dsc1, dg1, dsh2, dsc2, dg2]
        if below is not None:
            dy2, dg2 = res[4:]
        dm = jnp.concatenate([t[:cfg.b, 0, :] for t in parts], axis=1)
        live = (0, 1) if l == last else range(6)
        dm_ctx = jnp.concatenate([t[cfg.b, 0, :] if i in live else jnp.zeros((D,), F32) for i, t in enumerate(parts)])
        dms[l] = jnp.concatenate([dm, dm_ctx[None, :], jnp.zeros((MOD_ROWS - cfg.b - 1, 6 * D), F32)], axis=0)

    grad_x = dx[:cfg.nl].reshape(x.shape)
    small_grads = {k: jnp.stack(v) for k, v in sg.items()}
    small_grads["final_g"] = d_final_g[0]
    return loss, grad_x, small_grads, dms


_BIG = ("w_in", "w_out", "w_ffn_in", "w_ffn_out")
_TAPS = "conv_dw"
_SMALL = ("c_ctx", "b_mod", "norm1_g", "norm2_g", "conv_dw", "conv_dw_b", "conv_ln_g", "conv_ln_b", "attn_sink",
          "pool_w", "pool_scale", "final_g")
_ORDER = ("c_ctx", "w_mod", "b_mod", "norm1_g", "norm2_g", "w_in", "conv_dw", "conv_dw_b", "conv_ln_g", "conv_ln_b",
          "attn_sink", "pool_w", "pool_scale", "w_out", "w_ffn_in", "w_ffn_out", "final_g")
_GATHER_HOSTS = {"att_fwd0": (0, ("w_out", "w_ffn_in", "w_ffn_out", _TAPS)), "ffn_in0": (1, _BIG + (_TAPS,))}
_REDUCE_HOSTS = {"dw_ffn_in0": (1, _BIG), "att_bwd0": (0, ("w_ffn_in", "w_ffn_out"))}
_FIRST_GATHER = (0, ("w_in",))
_LAST_REDUCE = (0, ("w_in", "w_out"))


class _Comm:
    def __init__(self, w, c_idx, jc_idx):
        self.shapes = {k: w[k].shape[1:] for k in _BIG}
        self.c_idx, self.jc_idx = c_idx, jc_idx
        halves = lambda a: a.reshape(2, a.shape[0] // 2, a.shape[1])
        taps = jnp.pad(w[_TAPS], ((0, 0), (0, 1), (0, 64)))
        cast = {k: w[k].astype(BF16) for k in _BIG}
        self.shards = [{**{k: halves(cast[k][l]) for k in _BIG}, _TAPS: halves(taps[l])} for l in range(2)]
        self.wf = [dict(), dict()]
        self.grads = [dict(), dict()]
        self.reduced = [dict(), dict()]
        self._open = {}
        self.landed("first", _run_ride("gather_first", self.ride("first")))

    def ride(self, host):
        if host == "first" or host in _GATHER_HOSTS:
            layer, keys = _FIRST_GATHER if host == "first" else _GATHER_HOSTS[host]
            return _gather_ride([self.shards[layer][k] for k in keys])
        if host == "last" or host in _REDUCE_HOSTS:
            layer, keys = _LAST_REDUCE if host == "last" else _REDUCE_HOSTS[host]
            tag = f"{layer}_{keys[0]}"
            mine = [self.grads[layer][k] for k in keys]
            other = _send_other_half(f"send_other_half{tag}", mine)
            pair = [_pair_add(f"pair_add{layer}_{k}", a, b, self.c_idx) for k, a, b in zip(keys, mine, other)]
            self._open[host] = pair
            return _exchange_ride(pair)
        return None

    def landed(self, host, got):
        if host == "first" or host in _GATHER_HOSTS:
            layer, keys = _FIRST_GATHER if host == "first" else _GATHER_HOSTS[host]
            for k, f in zip(keys, got):
                if k == _TAPS:
                    taps = f.reshape(N_SHARD, 32, 128)[:, :CONV_K, :64]
                    self.wf[layer][k] = jnp.transpose(taps, (1, 0, 2)).reshape(CONV_K, CONV_W)
                else:
                    self.wf[layer][k] = f.reshape((1, N_SHARD) + self.shapes[k])
        if host == "last" or host in _REDUCE_HOSTS:
            layer, keys = _LAST_REDUCE if host == "last" else _REDUCE_HOSTS[host]
            mine = [_shard_sum(f"shard_sum{layer}_{k}", a, b, self.jc_idx) for k, a, b in zip(keys, self._open.pop(host), got)]
            for k, g in zip(keys, _swap_reduced(f"swap_reduced{layer}_{keys[0]}", mine)):
                self.reduced[layer][k] = g.reshape(self.shapes[k])

    def finish(self):
        self.landed("last", _run_ride("exchange_last", self.ride("last")))


def _conditioning(c, c_ctx, w_mod, b_mod, chip):
    b = c.shape[0]
    block = jnp.concatenate([c, c_ctx[None, :], jnp.zeros((8 - b - 1, D), F32)], axis=0)
    c_all = _share_small("share_c", block, False).reshape(64, D)
    bias = lax.dynamic_slice_in_dim(b_mod, chip * MOD_W, MOD_W, axis=1)
    full = lambda r, q: pl.BlockSpec((r, q), lambda i, j, k: (0, 0))

    def bias_epi(acc, ex, outs):
        outs[0][...] = acc + ex[0][...]

    mv = [_mm(f"mod_fwd{l}", "nn", (1, 1, 1), c_all, w_mod[l], full(64, D), full(D, MOD_W),
              [jax.ShapeDtypeStruct((64, MOD_W), F32)], [full(64, MOD_W)], extras=[bias[l][None]],
              extra_specs=[full(1, MOD_W)], a_fn=_silu, epi=bias_epi)[0] for l in range(2)]
    by_dev = jnp.transpose(jnp.stack(mv).reshape(2, 8, 8, MOD_W), (1, 0, 2, 3))
    rows = jnp.transpose(_mod_rows_exchange(by_dev), (1, 2, 0, 3)).reshape(2, 8, 6 * D)
    rows = jnp.pad(rows, ((0, 0), (0, MOD_ROWS - 8), (0, 0)))
    return [rows[l].reshape(MOD_ROWS, 6, D) for l in range(2)], c_all


def _conditioning_bwd(dms, c_all, w_mod, b):
    dm = jnp.stack([d[:8] for d in dms])
    by_chip = jnp.transpose(dm.reshape(2, 8, N_SHARD, MOD_W), (2, 0, 1, 3))
    gathered, d_bias = _mod_grad_exchange(by_chip, dm)
    dm_all = jnp.transpose(gathered, (1, 0, 2, 3)).reshape(2, 64, MOD_W)
    full = lambda r, q: pl.BlockSpec((r, q), lambda i, j, k: (0, 0))

    def ctx_rows_epi(acc, ex, outs):
        row = lax.broadcasted_iota(jnp.int32, acc.shape, 0) % 8
        outs[0][...] = _colsum(jnp.where(row == b, acc * _dsilu(ex[0][...]), 0.0))

    g_mod, d_ctx = [], jnp.zeros((D,), F32)
    for l in range(2):
        g_mod.append(_mm(f"dw_mod{l}", "tn", (1, 1, 1), c_all, dm_all[l], full(64, D), full(64, MOD_W),
                         [jax.ShapeDtypeStruct((D, MOD_W), F32)], [full(D, MOD_W)], a_fn=_silu)[0])
        part = _mm(f"mod_bwd{l}", "nt", (1, 1, 1), dm_all[l], w_mod[l], full(64, MOD_W), full(D, MOD_W),
                   [jax.ShapeDtypeStruct((1, D), F32)], [full(1, D)], extras=[c_all], extra_specs=[full(64, D)],
                   epi=ctx_rows_epi)[0]
        d_ctx = d_ctx + part[0]
    return g_mod, d_bias[:, 0, :], d_ctx


def kernel(x, c, ctx, c_ctx, w_mod, b_mod, norm1_g, norm2_g, w_in, conv_dw, conv_dw_b, conv_ln_g, conv_ln_b, attn_sink, pool_w, pool_scale, w_out, w_ffn_in, w_ffn_out, final_g, loss_target, m_c_ctx, m_w_mod, m_b_mod, m_norm1_g, m_norm2_g, m_w_in, m_conv_dw, m_conv_dw_b, m_conv_ln_g, m_conv_ln_b, m_attn_sink, m_pool_w, m_pool_scale, m_w_out, m_w_ffn_in, m_w_ffn_out, m_final_g, v_c_ctx, v_w_mod, v_b_mod, v_norm1_g, v_norm2_g, v_w_in, v_conv_dw, v_conv_dw_b, v_conv_ln_g, v_conv_ln_b, v_attn_sink, v_pool_w, v_pool_scale, v_w_out, v_w_ffn_in, v_w_ffn_out, v_final_g):
    w = dict(c_ctx=c_ctx, w_mod=w_mod, b_mod=b_mod, norm1_g=norm1_g, norm2_g=norm2_g, w_in=w_in, conv_dw=conv_dw,
             conv_dw_b=conv_dw_b, conv_ln_g=conv_ln_g, conv_ln_b=conv_ln_b, attn_sink=attn_sink, pool_w=pool_w,
             pool_scale=pool_scale, w_out=w_out, w_ffn_in=w_ffn_in, w_ffn_out=w_ffn_out, final_g=final_g)
    m = dict(c_ctx=m_c_ctx, w_mod=m_w_mod, b_mod=m_b_mod, norm1_g=m_norm1_g, norm2_g=m_norm2_g, w_in=m_w_in,
             conv_dw=m_conv_dw, conv_dw_b=m_conv_dw_b, conv_ln_g=m_conv_ln_g, conv_ln_b=m_conv_ln_b,
             attn_sink=m_attn_sink, pool_w=m_pool_w, pool_scale=m_pool_scale, w_out=m_w_out, w_ffn_in=m_w_ffn_in,
             w_ffn_out=m_w_ffn_out, final_g=m_final_g)
    v = dict(c_ctx=v_c_ctx, w_mod=v_w_mod, b_mod=v_b_mod, norm1_g=v_norm1_g, norm2_g=v_norm2_g, w_in=v_w_in,
             conv_dw=v_conv_dw, conv_dw_b=v_conv_dw_b, conv_ln_g=v_conv_ln_g, conv_ln_b=v_conv_ln_b,
             attn_sink=v_attn_sink, pool_w=v_pool_w, pool_scale=v_pool_scale, w_out=v_w_out, w_ffn_in=v_w_ffn_in,
             w_ffn_out=v_w_ffn_out, final_g=v_final_g)
    xi, yi, ci = _position()
    chip = 2 * xi + yi
    mod3s, c_all = _conditioning(c, c_ctx, w_mod, b_mod, chip)
    comm = _Comm(w, jnp.reshape(ci, (1,)).astype(jnp.int32), jnp.stack([chip, ci]).astype(jnp.int32))
    small = {k: w[k] for k in _SMALL if k not in ("conv_dw", "c_ctx", "b_mod")}
    loss, grad_x, sgrads, dms = _local_step(x, ctx, small, mod3s, loss_target, comm)
    comm.finish()
    g_mod, sgrads["b_mod"], d_ctx = _conditioning_bwd(dms, c_all, w_mod, c.shape[0])
    sgrads["c_ctx"] = 0.5 * d_ctx

    names = list(_SMALL)
    total = _share_small("sum_small", _pack([loss] + [sgrads[k] for k in names]), True)
    parts = _unpack(total, [()] + [sgrads[k].shape for k in names])
    loss_out = parts[0]
    gsmall = dict(zip(names, parts[1:]))
    gsmall["conv_dw"] = lax.dynamic_slice_in_dim(gsmall["conv_dw"], chip * 64, 64, axis=2)

    grads, delta, new_m, new_v = dict(gsmall), {}, {}, {}
    reduced = [{**comm.reduced[l], "w_mod": g_mod[l]} for l in range(2)]
    for k in ("w_mod",) + _BIG:
        s0, s1 = w[k].shape[1:]
        flat = lambda a: a.reshape(2 * s0, s1)
        g_, d_, m_, v_ = _adamw_layers(f"adamw_{k}", flat(w[k]), [reduced[l][k] for l in range(2)], flat(m[k]), flat(v[k]))
        grads[k], delta[k], new_m[k], new_v[k] = [a.reshape(w[k].shape) for a in (g_, d_, m_, v_)]
    d_, m_, v_ = _adamw("adamw_small", _pack([w[k] for k in names]), _pack([gsmall[k] for k in names]),
                        _pack([m[k] for k in names]), _pack([v[k] for k in names]))
    sshapes = [w[k].shape for k in names]
    for k, a, b, e in zip(names, _unpack(d_, sshapes), _unpack(m_, sshapes), _unpack(v_, sshapes)):
        delta[k], new_m[k], new_v[k] = a, b, e
    return (loss_out, grad_x, *[grads[k] for k in _ORDER], *[delta[k] for k in _ORDER],
            *[new_m[k] for k in _ORDER], *[new_v[k] for k in _ORDER])
```

```python
from typing import NamedTuple

import jax
import jax.numpy as jnp
import numpy as np
from jax import lax
from jax.experimental import pallas as pl
from jax.experimental.pallas import tpu as pltpu

F32 = jnp.float32
BF16 = jnp.bfloat16

D = 1024
GRID_W = 64
HEAD_DIM = 64
N_HEADS = 8
ATTN_W = 512
CONV_W = 256
POOL_W = 256
IN_W = 1536
D_FF = 2816
CONV_K = 31
QB = 128
ROPE_BASE = 10000.0
EPS = 1e-6
NEG = -1e30
N_SHARD = 4
IN_SHARD = IN_W // N_SHARD
HALF_FF = D_FF // 2
MOD_W = 6 * D // N_SHARD
MOD_ROWS = 16
PACK_LANES = 128

ADAM_LR = 0.001
ADAM_B1 = 0.9
ADAM_B2 = 0.999
ADAM_EPS = 1e-08
ADAM_WD = 0.01
ADAM_STEP = 10

VMEM_LIMIT_V7X = 56 * 1024 * 1024
TM = 512
TR_MAX = 1024
CHUNK = 256
PAD = 16

_MESH = pl.DeviceIdType.MESH
_ANY = pl.BlockSpec(memory_space=pl.ANY)
_DIMS = {"nn": (((1,), (0,)), ((), ())), "nt": (((1,), (1,)), ((), ())), "tn": (((0,), (0,)), ((), ()))}


class _Cfg(NamedTuple):
    b: int
    seq: int
    ctx: int

    @property
    def nl(self):
        return self.b * self.seq

    @property
    def nc(self):
        return self.b * self.ctx

    @property
    def r(self):
        return self.nl + self.nc

    def mod_row(self, i):
        return jnp.where(i < self.nl // TM, i // (self.seq // TM), self.b)

    def first_of_row(self, i):
        nlb = self.nl // TM
        return jnp.logical_or(jnp.logical_and(i < nlb, i % (self.seq // TM) == 0), i == nlb)


def _params(n_grid=0):
    sem = ("arbitrary",) * n_grid if n_grid else None
    return pltpu.CompilerParams(dimension_semantics=sem, vmem_limit_bytes=VMEM_LIMIT_V7X)


def _dot(a, b, mode="nn"):
    return lax.dot_general(a.astype(BF16), b.astype(BF16), _DIMS[mode], preferred_element_type=F32)


def _sigmoid(x):
    return 1.0 / (1.0 + jnp.exp(-x))


def _silu(x):
    return x * _sigmoid(x)


def _dsilu(x):
    s = _sigmoid(x)
    return s * (1.0 + x * (1.0 - s))


def _colsum(v):
    return jnp.sum(v, axis=0, keepdims=True)


def _dw_rows(rows):
    return TR_MAX if rows % TR_MAX == 0 else TM


def _epi_store(acc, ex, outs):
    for o in outs:
        o[...] = acc.astype(o.dtype)


class _Ride(NamedTuple):
    ins: list
    out_shape: list
    scratch: list
    start: object
    finish: object


class _Hosted(NamedTuple):
    ride: _Ride
    n_in: int
    n_out: int
    grid: tuple

    def split(self, refs):
        n_ri, n_ro, n_rs = len(self.ride.ins), len(self.ride.out_shape), len(self.ride.scratch)
        r_in = refs[self.n_in:self.n_in + n_ri]
        r_out = refs[self.n_in + n_ri + self.n_out:self.n_in + n_ri + self.n_out + n_ro]
        own = refs[:self.n_in] + refs[self.n_in + n_ri:self.n_in + n_ri + self.n_out] + \
            refs[self.n_in + n_ri + self.n_out + n_ro:len(refs) - n_rs]
        return own, (r_in, r_out, refs[len(refs) - n_rs:])

    def start(self, parts):
        ids = [pl.program_id(d) for d in range(len(self.grid))]
        first = ids[0] == 0
        for i in ids[1:]:
            first = jnp.logical_and(first, i == 0)
        pl.when(first)(lambda: self.ride.start(*parts))

    def finish(self, parts):
        ids = [pl.program_id(d) for d in range(len(self.grid))]
        last = ids[0] == self.grid[0] - 1
        for i, g in zip(ids[1:], self.grid[1:]):
            last = jnp.logical_and(last, i == g - 1)
        pl.when(last)(lambda: self.ride.finish(*parts))


def _hosted_call(body, ride, name, grid, ins, in_specs, out_shape, out_specs, scratch, params):
    if ride is None:
        res = pl.pallas_call(body, name=name, grid=grid, in_specs=in_specs, out_specs=out_specs, out_shape=out_shape,
                             scratch_shapes=scratch, compiler_params=params)(*ins)
        return list(res), []
    host = _Hosted(ride, len(ins), len(out_shape), tuple(grid))

    def hosted(*refs):
        own, parts = host.split(refs)
        host.start(parts)
        body(*own)
        host.finish(parts)

    res = pl.pallas_call(
        hosted, name=name, grid=grid, in_specs=list(in_specs) + [_ANY] * len(ride.ins),
        out_specs=list(out_specs) + [_ANY] * len(ride.out_shape), out_shape=list(out_shape) + list(ride.out_shape),
        scratch_shapes=list(scratch) + list(ride.scratch), compiler_params=params)(*ins, *ride.ins)
    return list(res[:len(out_shape)]), list(res[len(out_shape):])


def _mm(name, mode, grid, a, b, a_spec, b_spec, out_shape, out_specs, acc_shape=None, extras=(),
        extra_specs=(), a_fn=None, epi=_epi_store, ride=None):
    nk = grid[2]
    n_ex, n_out = len(extras), len(out_shape)

    def body(*refs):
        a_ref, b_ref = refs[:2]
        ex = refs[2:2 + n_ex]
        outs = refs[2 + n_ex:2 + n_ex + n_out]
        av = a_ref[...]
        if a_fn is not None:
            av = a_fn(av)
        part = _dot(av, b_ref[...], mode)
        if nk == 1:
            epi(part, ex, outs)
        else:
            acc = refs[-1]
            k = pl.program_id(2)

            @pl.when(k == 0)
            def _():
                acc[...] = part

            @pl.when(k > 0)
            def _():
                acc[...] += part

            @pl.when(k == nk - 1)
            def _():
                epi(acc[...], ex, outs)

    scratch = [] if nk == 1 else [pltpu.VMEM(acc_shape, F32)]
    outs, ride_outs = _hosted_call(body, ride, name, grid, [a, b, *extras], [a_spec, b_spec, *extra_specs],
                                   list(out_shape), list(out_specs), scratch, _params(3))
    return outs if ride is None else (outs, ride_outs)


def _mm_dw(name, a, b, a_spec, b_spec, out_shape, out_spec, n_out_blocks, acc_shape, n_steps, epi, a_fn=None,
           extras=(), extra_specs=(), extra_out=(), extra_out_specs=(), ride=None):
    grid = (1, n_out_blocks, n_steps)
    outs = [jax.ShapeDtypeStruct(out_shape, BF16)] + list(extra_out)
    return _mm(name, "tn", grid, a, b, a_spec, b_spec, outs, [out_spec, *extra_out_specs], acc_shape, extras,
               extra_specs, a_fn, epi, ride)


def _gate_step(i, dxv, y_ref, m_ref, ig, dy_ref, dgate_ref, cfg):
    dy_ref[...] = (dxv * m_ref[ig:ig + 1, :]).astype(BF16)
    _accumulate_rows(cfg.first_of_row(i), dgate_ref, _colsum(dxv * y_ref[...].astype(F32)))


def _proj_norm_bwd(name, a, a_spec, pick, w, x, dres, gvec, mod3, isc, nblk, res_latent_only, gate, cfg,
                   dx_latent_only=False):
    ns = w.shape[-1]
    nlb = cfg.nl // TM

    def body(a_ref, w_ref, x_ref, dres_ref, g_ref, m_ref, *rest):
        if gate is None:
            dx_ref, dsh_ref, dsc_ref, dg_ref = rest
        else:
            y_ref, gm_ref, dx_ref, dsh_ref, dsc_ref, dg_ref, dy_ref, dgate_ref = rest
        i = pl.program_id(0)
        dhv = _dot(pick(a_ref, 0), w_ref[0], "nt")
        for j in range(1, N_SHARD):
            dhv = dhv + _dot(pick(a_ref, j), w_ref[j], "nt")
        xv = x_ref[...]
        r = lax.rsqrt(jnp.mean(xv * xv, axis=-1, keepdims=True) + EPS)
        xh = xv * r
        g = g_ref[...]
        sc1 = 1.0 + m_ref[isc:isc + 1, :]
        t = dhv * xh
        first = cfg.first_of_row(i)
        _accumulate_rows(first, dsh_ref, _colsum(dhv))
        _accumulate_rows(first, dsc_ref, _colsum(t * g))
        _accumulate_rows(i == 0, dg_ref, _colsum(t * sc1))
        dxh = dhv * (g * sc1)
        dxn = r * (dxh - xh * jnp.mean(dxh * xh, axis=-1, keepdims=True))
        dxv = (jnp.where(i < nlb, dres_ref[...], 0.0) if res_latent_only else dres_ref[...]) + dxn
        if dx_latent_only:
            @pl.when(i < nlb)
            def _():
                dx_ref[...] = dxv
        else:
            dx_ref[...] = dxv
        if gate is not None:
            _gate_step(i, dxv, y_ref, gm_ref, gate[2], dy_ref, dgate_ref, cfg)

    row = pl.BlockSpec((TM, D), lambda i: (i, 0))
    vec = pl.BlockSpec((1, D), lambda i: (0, 0))
    part = pl.BlockSpec((None, 1, D), lambda i: (cfg.mod_row(i), 0, 0))
    part_shape = jax.ShapeDtypeStruct((MOD_ROWS, 1, D), F32)
    resident = pl.BlockSpec((None, N_SHARD, D, ns), lambda i: (0, 0, 0, 0), pipeline_mode=pl.Buffered(1))
    ins, in_specs = [a, w, x, dres, gvec, mod3], [a_spec, resident, row, row, vec, _mod_spec(cfg)]
    out_specs = [row, part, part, vec]
    out_shape = [jax.ShapeDtypeStruct((cfg.r, D), F32), part_shape, part_shape, jax.ShapeDtypeStruct((1, D), F32)]
    if dx_latent_only:
        out_specs[0] = pl.BlockSpec((TM, D), lambda i: (jnp.minimum(i, nlb - 1), 0))
        out_shape[0] = jax.ShapeDtypeStruct((cfg.nl, D), F32)
    if gate is not None:
        ins, in_specs = ins + list(gate[:2]), in_specs + [row, _mod_spec(cfg)]
        out_specs, out_shape = out_specs + [row, part], out_shape + [jax.ShapeDtypeStruct((cfg.r, D), BF16), part_shape]
    return pl.pallas_call(body, name=name, grid=(nblk,), in_specs=in_specs, out_specs=out_specs, out_shape=out_shape,
                          compiler_params=_params(1))(*ins)


def _mod_spec(cfg):
    return pl.BlockSpec((None, 6, D), lambda i: (cfg.mod_row(i), 0, 0))


def _norm_mod(xv, g, m_ref, ish, isc):
    r = lax.rsqrt(jnp.mean(xv * xv, axis=-1, keepdims=True) + EPS)
    return (xv * r * g * (1.0 + m_ref[isc:isc + 1, :]) + m_ref[ish:ish + 1, :]).astype(BF16)


def _accumulate_rows(first, ref, val):
    @pl.when(first)
    def _():
        ref[...] = val

    @pl.when(jnp.logical_not(first))
    def _():
        ref[...] += val


def _loss_head(x, target, gvec, y, mod3, ig, cfg):
    def body(x_ref, t_ref, g_ref, y_ref, m_ref, dx_ref, loss_ref, dg_ref, dy_ref, dgate_ref):
        i = pl.program_id(0)

        @pl.when(i == 0)
        def _():
            loss_ref[...] = jnp.zeros_like(loss_ref)
            dg_ref[...] = jnp.zeros_like(dg_ref)

        xv = x_ref[...]
        g = g_ref[...]
        r = lax.rsqrt(jnp.mean(xv * xv, axis=-1, keepdims=True) + EPS)
        xh = xv * r
        err = xh * g - t_ref[...]
        loss_ref[...] += (0.5 / D) * _colsum(jnp.sum(err * err, axis=-1, keepdims=True))
        dy = err * (1.0 / D)
        dg_ref[...] += _colsum(dy * xh)
        dxh = dy * g
        dxv = r * (dxh - xh * jnp.mean(dxh * xh, axis=-1, keepdims=True))
        dx_ref[...] = dxv
        _gate_step(i, dxv, y_ref, m_ref, ig, dy_ref, dgate_ref, cfg)

    row = pl.BlockSpec((TM, D), lambda i: (i, 0))
    vec = pl.BlockSpec((1, D), lambda i: (0, 0))
    part = pl.BlockSpec((None, 1, D), lambda i: (cfg.mod_row(i), 0, 0))
    return pl.pallas_call(
        body, name="loss_head", grid=(cfg.nl // TM,), in_specs=[row, row, vec, row, _mod_spec(cfg)],
        out_specs=[row, pl.BlockSpec((1, 1), lambda i: (0, 0)), vec, row, part],
        out_shape=[jax.ShapeDtypeStruct((cfg.r, D), F32), jax.ShapeDtypeStruct((1, 1), F32),
                   jax.ShapeDtypeStruct((1, D), F32), jax.ShapeDtypeStruct((cfg.r, D), BF16),
                   jax.ShapeDtypeStruct((MOD_ROWS, 1, D), F32)],
        compiler_params=_params(1))(x, target, gvec, y, mod3)


def _rope_tables(seq):
    rows = seq // GRID_W
    row = jnp.repeat(jnp.arange(rows), GRID_W).astype(F32)
    col = jnp.tile(jnp.arange(GRID_W), rows).astype(F32)
    half = HEAD_DIM // 2
    inv = ROPE_BASE ** (-jnp.arange(0, half, 2, dtype=F32) / half)
    ar, ac = row[:, None] * inv, col[:, None] * inv
    ang = jnp.concatenate([ar, ar, ac, ac], axis=-1)
    sign = jnp.tile(jnp.concatenate([-jnp.ones((16,), F32), jnp.ones((16,), F32)]), 2)
    cos = jnp.tile(jnp.cos(ang), (1, 2))
    sin = jnp.tile(jnp.sin(ang) * sign, (1, 2))
    cos = jnp.concatenate([cos, jnp.ones((TM, 2 * HEAD_DIM), F32)], axis=0)
    sin = jnp.concatenate([sin, jnp.zeros((TM, 2 * HEAD_DIM), F32)], axis=0)
    return cos, sin


def _rope(x, cos, sin_signed, sign):
    lane = lax.broadcasted_iota(jnp.int32, x.shape, 1)
    low = (lane % 32) < 16
    rot = jnp.where(low, pltpu.roll(x, 112, 1), pltpu.roll(x, 16, 1))
    return x * cos + sign * (rot * sin_signed)


def _in_proj(name, x, gvec, mod3, w_in, layer, cos_t, sin_t, cfg):
    nlb, bps = cfg.nl // TM, cfg.seq // TM

    def body(x_ref, g_ref, m_ref, w_ref, cos_ref, sin_ref, h_ref, qkv_ref, cp_ref):
        hv = _norm_mod(x_ref[...], g_ref[...], m_ref, 0, 1)
        h_ref[...] = hv
        u = jnp.concatenate([_dot(hv, w_ref[j]) for j in range(N_SHARD)], axis=1)
        cos, sin = cos_ref[...], sin_ref[...]
        tiles = []
        for t in range(5):
            y = _rope(u[:, 128 * t:128 * (t + 1)], cos, sin, 1.0)
            tiles.append(y * (HEAD_DIM ** -0.5) if t < 4 else y)
        tiles.append(u[:, 640:768])
        qkv_ref[...] = jnp.concatenate(tiles, axis=1).astype(BF16)
        cp_ref[...] = u[:, 768:IN_W].astype(BF16)

    tab = pl.BlockSpec((TM, 128), lambda i: (jnp.where(i < nlb, i % bps, bps), 0))
    half = pl.BlockSpec((TM, 768), lambda i: (i, 0))
    row = pl.BlockSpec((TM, D), lambda i: (i, 0))
    return pl.pallas_call(
        body, name=name, grid=(cfg.r // TM,),
        in_specs=[row, pl.BlockSpec((1, D), lambda i: (0, 0)), _mod_spec(cfg),
                  pl.BlockSpec((None, N_SHARD, D, IN_SHARD), lambda i: (layer, 0, 0, 0)), tab, tab],
        out_specs=[row, half, half],
        out_shape=[jax.ShapeDtypeStruct((cfg.r, D), BF16), jax.ShapeDtypeStruct((cfg.r, 768), BF16),
                   jax.ShapeDtypeStruct((cfg.r, 768), BF16)],
        compiler_params=_params(1))(x, gvec, mod3, w_in, cos_t, sin_t)


def _att_specs(cfg):
    nlb, ncb = cfg.seq // QB, cfg.ctx // QB

    def qblk(s, qb):
        return jnp.where(qb < nlb, s * nlb + qb, cfg.nl // QB + s * ncb + qb - nlb)

    def near(off, col):
        return pl.BlockSpec((QB, 128), lambda s, qb: (s * nlb + jnp.clip(qb + off, 0, nlb - 1), col))

    def ctxs(col):
        return pl.BlockSpec((cfg.ctx, 128), lambda s, qb: (cfg.nl // cfg.ctx + s, col))

    qspec = pl.BlockSpec((QB, ATTN_W), lambda s, qb: (qblk(s, qb), 0))
    kv = [ctxs(4), ctxs(5), near(-1, 4), near(0, 4), near(1, 4), near(-1, 5), near(0, 5), near(1, 5)]
    return qblk, qspec, kv


def _att_scores(qb, nlb, sink_ref, q_ref, k_refs, v_refs, kh):
    is_lat = qb < nlb
    ii = lax.broadcasted_iota(jnp.int32, (4 * QB, 3 * QB), 0) % QB
    col = lax.broadcasted_iota(jnp.int32, (4 * QB, 3 * QB), 1)
    jj, blk = col % QB, col // QB
    off_p = jnp.where(jnp.logical_and(is_lat, qb >= 1), 0.0, NEG)
    off_c = jnp.where(is_lat, 0.0, NEG)
    off_n = jnp.where(jnp.logical_and(is_lat, qb <= nlb - 2), 0.0, NEG)
    inside = jnp.logical_or(blk == 1, jnp.logical_or(jnp.logical_and(blk == 0, jj >= ii),
                                                     jnp.logical_and(blk == 2, jj <= ii)))
    off = jnp.where(blk == 0, off_p, jnp.where(blk == 1, off_c, off_n))
    q4 = jnp.concatenate([q_ref[:, (4 * kh + g) * HEAD_DIM:(4 * kh + g + 1) * HEAD_DIM] for g in range(4)], axis=0)
    rg = lax.broadcasted_iota(jnp.int32, (4 * QB, 1), 0) // QB
    snk = jnp.where(rg == 0, sink_ref[4 * kh],
                    jnp.where(rg == 1, sink_ref[4 * kh + 1], jnp.where(rg == 2, sink_ref[4 * kh + 2], sink_ref[4 * kh + 3])))
    lanes = slice(kh * HEAD_DIM, (kh + 1) * HEAD_DIM)
    kx, vx = k_refs[0][:, lanes], v_refs[0][:, lanes]
    kl = jnp.concatenate([r[:, lanes] for r in k_refs[1:]], axis=0)
    vl = jnp.concatenate([r[:, lanes] for r in v_refs[1:]], axis=0)
    sx = _dot(q4, kx, "nt")
    sl = jnp.where(inside, _dot(q4, kl, "nt"), NEG) + off
    return q4, snk, (kx, kl), (vx, vl), (sx, sl)


def _att_fwd(name, qkv, sink, ctx_queries, cfg, ride=None):
    nlb, ncb = cfg.seq // QB, cfg.ctx // QB
    qblk, qspec, kvspecs = _att_specs(cfg)

    def body(sink_ref, q_ref, kx_ref, vx_ref, kp_ref, kc_ref, kn_ref, vp_ref, vc_ref, vn_ref, o_ref, lse_ref):
        qb = pl.program_id(1)
        for kh in range(2):
            q4, snk, _, vs, ss = _att_scores(qb, nlb, sink_ref, q_ref, (kx_ref, kp_ref, kc_ref, kn_ref),
                                             (vx_ref, vp_ref, vc_ref, vn_ref), kh)
            m = snk
            for s_ in ss:
                m = jnp.maximum(m, jnp.max(s_, axis=-1, keepdims=True))
            den = jnp.exp(snk - m)
            o4 = jnp.zeros((4 * QB, HEAD_DIM), F32)
            for s_, v_ in zip(ss, vs):
                p = jnp.exp(s_ - m)
                den = den + jnp.sum(p, axis=-1, keepdims=True)
                o4 = o4 + _dot(p, v_)
            o4 = o4 / den
            lse = m + jnp.log(den)
            for g in range(4):
                h = 4 * kh + g
                o_ref[:, h * HEAD_DIM:(h + 1) * HEAD_DIM] = o4[g * QB:(g + 1) * QB].astype(BF16)
                lse_ref[:, h:h + 1] = lse[g * QB:(g + 1) * QB]

    return _hosted_call(
        body, ride, name, (cfg.b, nlb + (ncb if ctx_queries else 0)), [sink] + [qkv] * 9,
        [pl.BlockSpec(memory_space=pltpu.SMEM), qspec, *kvspecs],
        [jax.ShapeDtypeStruct((cfg.r, D), BF16), jax.ShapeDtypeStruct((cfg.r, N_HEADS), F32)],
        [pl.BlockSpec((QB, ATTN_W), lambda s, qb: (qblk(s, qb), 0)),
         pl.BlockSpec((QB, N_HEADS), lambda s, qb: (qblk(s, qb), 0))], [], _params(2))


def _att_bwd(name, qkv, mix, dmix, lse, sink, cos_t, sin_t, ctx_queries, cfg, ride=None):
    nlb, ncb = cfg.seq // QB, cfg.ctx // QB
    nqb = nlb + (ncb if ctx_queries else 0)
    qblk, qspec, kvspecs = _att_specs(cfg)

    def body(sink_ref, q_ref, kx_ref, vx_ref, kp_ref, kc_ref, kn_ref, vp_ref, vc_ref, vn_ref, o_ref, do_ref,
             lse_ref, cosq_ref, sinq_ref, cosk_ref, sink_tab_ref, dq_ref, dkvl_ref, dkvc_ref, dsink_ref,
             accl, accc, dqs):
        s_id, qb = pl.program_id(0), pl.program_id(1)

        @pl.when(qb == 0)
        def _():
            accl[...] = jnp.zeros_like(accl)
            accc[...] = jnp.zeros_like(accc)

        @pl.when(jnp.logical_and(s_id == 0, qb == 0))
        def _():
            dsink_ref[...] = jnp.zeros_like(dsink_ref)

        starts = [pl.multiple_of(jnp.clip(qb + off, 0, nlb - 1) * QB, QB) for off in (-1, 0, 1)]
        for kh in range(2):
            q4, snk, ks, vs, ss = _att_scores(qb, nlb, sink_ref, q_ref, (kx_ref, kp_ref, kc_ref, kn_ref),
                                              (vx_ref, vp_ref, vc_ref, vn_ref), kh)
            lanes = slice(kh * HEAD_DIM, (kh + 1) * HEAD_DIM)
            heads =[slice((4 * kh + g) * HEAD_DIM, (4 * kh + g + 1) * HEAD_DIM) for g in range(4)]
            do4 = jnp.concatenate([do_ref[:, hs] for hs in heads], axis=0)
            o4 = jnp.concatenate([o_ref[:, hs] for hs in heads], axis=0).astype(F32)
            lse4 = jnp.concatenate([lse_ref[:, 4 * kh + g:4 * kh + g + 1] for g in range(4)], axis=0)
            delta = jnp.sum(do4 * o4, axis=-1, keepdims=True)
            dq4 = jnp.zeros((4 * QB, HEAD_DIM), F32)
            dks, dvs = [], []
            for s_, k_, v_ in zip(ss, ks, vs):
                p = jnp.exp(s_ - lse4)
                ds = p * (_dot(do4, v_, "nt") - delta)
                dq4 = dq4 + _dot(ds, k_)
                dks.append(_dot(ds, q4, "tn"))
                dvs.append(_dot(p, do4, "tn"))
            accc[:, lanes] += dks[0]
            accc[:, 128 + kh * HEAD_DIM:128 + (kh + 1) * HEAD_DIM] += dvs[0]
            for t, st in enumerate(starts):
                accl[pl.ds(st, QB), lanes] += dks[1][t * QB:(t + 1) * QB]
                accl[pl.ds(st, QB), 128 + kh * HEAD_DIM:128 + (kh + 1) * HEAD_DIM] += dvs[1][t * QB:(t + 1) * QB]
            dsk = -jnp.exp(snk - lse4) * delta
            for g in range(4):
                h = 4 * kh + g
                dsink_ref[h:h + 1, :] += jnp.broadcast_to(_colsum(dsk[g * QB:(g + 1) * QB]), (1, 128))
                dqs[:, heads[g]] = dq4[g * QB:(g + 1) * QB]
        cos, sin = cosq_ref[...], sinq_ref[...]
        dq_ref[...] = jnp.concatenate(
            [_rope(dqs[:, 128 * t:128 * (t + 1)], cos, sin, -1.0) * (HEAD_DIM ** -0.5) for t in range(4)],
            axis=1).astype(BF16)

        @pl.when(qb == nqb - 1)
        def _():
            dk = _rope(accl[:, 0:128], cosk_ref[...], sink_tab_ref[...], -1.0)
            dkvl_ref[...] = jnp.concatenate([dk, accl[:, 128:256]], axis=1).astype(BF16)
            dkvc_ref[...] = accc[...].astype(BF16)

    rowq = lambda w: pl.BlockSpec((QB, w), lambda s, qb: (qblk(s, qb), 0))
    tabq = pl.BlockSpec((QB, 128), lambda s, qb: (jnp.where(qb < nlb, qb, cfg.seq // QB), 0))
    tabk = pl.BlockSpec((cfg.seq, 128), lambda s, qb: (0, 0))
    return _hosted_call(
        body, ride, name, (cfg.b, nqb), [sink] + [qkv] * 9 + [mix, dmix, lse, cos_t, sin_t, cos_t, sin_t],
        [pl.BlockSpec(memory_space=pltpu.SMEM), qspec, *kvspecs, rowq(ATTN_W), rowq(ATTN_W), rowq(N_HEADS),
         tabq, tabq, tabk, tabk],
        [jax.ShapeDtypeStruct((cfg.r, IN_W), BF16), jax.ShapeDtypeStruct((cfg.nl, 256), BF16),
         jax.ShapeDtypeStruct((cfg.nc, 256), BF16), jax.ShapeDtypeStruct((N_HEADS, 128), F32)],
        [rowq(ATTN_W), pl.BlockSpec((cfg.seq, 256), lambda s, qb: (s, 0)),
         pl.BlockSpec((cfg.ctx, 256), lambda s, qb: (s, 0)), pl.BlockSpec((N_HEADS, 128), lambda s, qb: (0, 0))],
        [pltpu.VMEM((cfg.seq, 256), F32), pltpu.VMEM((cfg.ctx, 256), F32), pltpu.VMEM((QB, ATTN_W), F32)], _params(2))


def _pool_geometry(n, c):
    lane = lax.broadcasted_iota(jnp.int32, (1, POOL_W), 1) // HEAD_DIM
    wl = jnp.where(lane == 0, 1, jnp.where(lane == 1, 2, jnp.where(lane == 2, 4, 8)))
    wr = wl - 1
    t = c * CHUNK + lax.broadcasted_iota(jnp.int32, (CHUNK, POOL_W), 0)
    cnt = (jnp.minimum(t + wr, n - 1) - jnp.maximum(t - wl, 0) + 1).astype(F32)
    return wl, wr, cnt


def _build_phases(src, ph, c):
    for s in range(1, 8):
        ph[s - 1] = src[c * CHUNK + s:c * CHUNK + s + CHUNK + 24, :]


def _window(src, ph, c, off):
    a, s = divmod(off, 8)
    if s == 0:
        return src[c * CHUNK + 8 * a:c * CHUNK + 8 * a + CHUNK, :]
    return ph[s - 1, 8 * a:8 * a + CHUNK, :]


def _conv_chunk(hp, ph, dw_ref, dwb_ref, c):
    _build_phases(hp, ph, c)
    acc = jnp.zeros((CHUNK, CONV_W), F32) + dwb_ref[...]
    for j in range(CONV_K):
        acc = acc + dw_ref[j:j + 1, :] * _window(hp, ph, c, j + 1)
    return acc


def _fill_glu(cp_ref, hp, n):
    hp[0:PAD, :] = jnp.zeros((PAD, CONV_W), F32)
    hp[PAD + n:2 * PAD + n, :] = jnp.zeros((PAD, CONV_W), F32)
    for c in range(n // CHUNK):
        rows = slice(c * CHUNK, (c + 1) * CHUNK)
        a = cp_ref[rows, 0:CONV_W].astype(F32)
        g = cp_ref[rows, CONV_W:2 * CONV_W].astype(F32)
        hp[PAD + c * CHUNK:PAD + (c + 1) * CHUNK, :] = a * _sigmoid(g)


def _fill_pool(cp_ref, pp, n):
    pp[0:PAD, :] = jnp.zeros((PAD, POOL_W), F32)
    pp[PAD + n:2 * PAD + n, :] = jnp.zeros((PAD, POOL_W), F32)
    for c in range(n // CHUNK):
        pp[PAD + c * CHUNK:PAD + (c + 1) * CHUNK, :] = cp_ref[c * CHUNK:(c + 1) * CHUNK, 2 * CONV_W:768].astype(F32)


LV = CHUNK + 2 * PAD
_LEVELS = pltpu.VMEM((3, LV + 16, POOL_W), F32)


def _clear_level_edges(lv):
    for b in range(3):
        lv[b, 0:8] = jnp.zeros((8, POOL_W), F32)
        lv[b, 8 + LV:16 + LV] = jnp.zeros((8, POOL_W), F32)


def _window_sums(src, lv, c, lead):
    lv[0, 8:8 + LV] = src[c * CHUNK:c * CHUNK + LV, :]
    lo = 7 if lead < 0 else 8
    lv[1, 8:8 + LV] = lv[0, lo:lo + LV] + lv[0, lo + 1:lo + 1 + LV]
    group = lax.broadcasted_iota(jnp.int32, (1, POOL_W), 1) // HEAD_DIM
    rows = slice(8 + PAD, 8 + PAD + CHUNK)
    res = lv[1, rows]
    cur = 1
    for g, s in ((1, 1), (2, 2), (3, 4)):
        nxt = 3 - cur
        lv[nxt, 8:8 + LV] = lv[cur, 8 - s:8 - s + LV] + lv[cur, 8 + s:8 + s + LV]
        res = jnp.where(group >= g, lv[nxt, rows], res)
        cur = nxt
    return res


def _pool_chunk(pp, lv, n, c):
    _, _, cnt = _pool_geometry(n, c)
    return _window_sums(pp, lv, c, -1) / cnt - pp[PAD + c * CHUNK:PAD + (c + 1) * CHUNK, :], cnt


def _seq_specs(n, blk_off, width, col=0):
    return pl.BlockSpec((n, width), lambda s: (blk_off + s, col))


def _full(shape):
    return pl.BlockSpec(shape, lambda s: (0,) * len(shape))


_PHASES = pltpu.VMEM((7, CHUNK + 24, CONV_W), F32)


def _convpool_fwd(name, cpin, mix, yconv, prm, n, blk_off, cfg):
    dw, dwb, lng, lnb, wbd, ps = prm
    n_alias = 1 if yconv is None else 2

    def body(*refs):
        cp_ref, dw_ref, dwb_ref, lng_ref, lnb_ref, wbd_ref, ps_ref = refs[:7]
        out_ref, y_ref, hp, pp, ph, lv = refs[7 + n_alias:]
        _fill_glu(cp_ref, hp, n)
        _fill_pool(cp_ref, pp, n)
        _clear_level_edges(lv)
        for c in range(n // CHUNK):
            rows = slice(c * CHUNK, (c + 1) * CHUNK)
            y = _conv_chunk(hp, ph, dw_ref, dwb_ref, c)
            y_ref[rows, :] = y
            d = y - jnp.mean(y, axis=-1, keepdims=True)
            hn = d * lax.rsqrt(jnp.mean(d * d, axis=-1, keepdims=True) + EPS) * lng_ref[...] + lnb_ref[...]
            out_ref[rows, 0:CONV_W] = (hn * _sigmoid(hn)).astype(BF16)
            yp, _ = _pool_chunk(pp, lv, n, c)
            out_ref[rows, CONV_W:2 * CONV_W] = (_dot(yp, wbd_ref[...]) * ps_ref[...]).astype(BF16)

    through = [mix] if yconv is None else [mix, yconv]
    return pl.pallas_call(
        body, name=name, grid=(cfg.b,),
        in_specs=[_seq_specs(n, blk_off, 768), _full((32, CONV_W)), _full((1, CONV_W)), _full((1, CONV_W)),
                  _full((1, CONV_W)), _full((POOL_W, POOL_W)), _full((1, POOL_W))] + [_ANY] * n_alias,
        out_specs=[_seq_specs(n, blk_off, 512, 1), _seq_specs(n, blk_off, CONV_W)],
        out_shape=[jax.ShapeDtypeStruct((cfg.r, D), BF16), jax.ShapeDtypeStruct((cfg.r, CONV_W), F32)],
        scratch_shapes=[pltpu.VMEM((n + 2 * PAD, CONV_W), F32), pltpu.VMEM((n + 2 * PAD, POOL_W), F32), _PHASES, _LEVELS],
        input_output_aliases={7 + i: i for i in range(n_alias)},
        compiler_params=_params(1))(cpin, dw, dwb, lng, lnb, wbd, ps, *through)


_SMALL_SHAPES = [(32, CONV_W), (1, CONV_W), (1, CONV_W), (1, CONV_W), (POOL_W, POOL_W), (1, POOL_W)]


def _convpool_bwd(name, cpin, yconv, dmix, prm, acc_in, n, blk_off, cfg):
    dw, dwb, lng, lnb, wbd, ps = prm
    nch = n // CHUNK

    def body(cp_ref, y_ref, dm_ref, dw_ref, dwb_ref, lng_ref, lnb_ref, wbd_ref, ps_ref, dcp_in,
             a_dw, a_dwb, a_lng, a_lnb, a_wbd, a_ps,
             dcp_ref, o_dw, o_dwb, o_lng, o_lnb, o_wbd, o_ps, hp, dyp, pp, wp, dyv, dwacc, ph, lv):
        s = pl.program_id(0)
        _clear_level_edges(lv)

        @pl.when(s == 0)
        def _():
            for o_, a_ in ((o_dw, a_dw), (o_dwb, a_dwb), (o_lng, a_lng), (o_lnb, a_lnb), (o_wbd, a_wbd), (o_ps, a_ps)):
                o_[...] = a_[...]
            dwacc[...] = jnp.zeros_like(dwacc)

        _fill_glu(cp_ref, hp, n)
        _fill_pool(cp_ref, pp, n)
        for ref in (dyp, wp):
            ref[0:PAD, :] = jnp.zeros((PAD, CONV_W), F32)
            ref[PAD + n:2 * PAD + n, :] = jnp.zeros((PAD, CONV_W), F32)
        for c in range(nch):
            rows = slice(c * CHUNK, (c + 1) * CHUNK)
            y = y_ref[rows, :]
            d = y - jnp.mean(y, axis=-1, keepdims=True)
            rstd = lax.rsqrt(jnp.mean(d * d, axis=-1, keepdims=True) + EPS)
            xh = d * rstd
            hn = xh * lng_ref[...] + lnb_ref[...]
            sg = _sigmoid(hn)
            dhn = dm_ref[rows, 0:CONV_W] * (sg * (1.0 + hn * (1.0 - sg)))
            o_lnb[...] += _colsum(dhn)
            o_lng[...] += _colsum(dhn * xh)
            dxh = dhn * lng_ref[...]
            dy = rstd * (dxh - jnp.mean(dxh, axis=-1, keepdims=True) - xh * jnp.mean(dxh * xh, axis=-1, keepdims=True))
            o_dwb[...] += _colsum(dy)
            dyp[PAD + c * CHUNK:PAD + (c + 1) * CHUNK, :] = dy
            _build_phases(hp, ph, c)
            for j in range(CONV_K):
                prod = dy * _window(hp, ph, c, j + 1)
                dwacc[8 * j:8 * j + 8, :] += jnp.sum(prod.reshape(CHUNK // 8, 8, CONV_W), axis=0)
            yp, cnt = _pool_chunk(pp, lv, n, c)
            dz = dm_ref[rows, CONV_W:2 * CONV_W]
            o_ps[...] += _colsum(dz * _dot(yp, wbd_ref[...]))
            dzs = dz * ps_ref[...]
            o_wbd[...] += _dot(yp, dzs, "tn")
            dv = _dot(dzs, wbd_ref[...], "nt")
            dyv[rows, :] = dv
            wp[PAD + c * CHUNK:PAD + (c + 1) * CHUNK, :] = dv / cnt
        for c in range(nch):
            rows = slice(c * CHUNK, (c + 1) * CHUNK)
            _build_phases(dyp, ph, c)
            dh = jnp.zeros((CHUNK, CONV_W), F32)
            for j in range(CONV_K):
                dh = dh + dw_ref[j:j + 1, :] * _window(dyp, ph, c, 31 - j)
            a = cp_ref[rows, 0:CONV_W].astype(F32)
            sg = _sigmoid(cp_ref[rows, CONV_W:2 * CONV_W].astype(F32))
            dcp_ref[rows, 0:CONV_W] = (dh * sg).astype(BF16)
            dcp_ref[rows, CONV_W:2 * CONV_W] = (dh * a * sg * (1.0 - sg)).astype(BF16)
            dcp_ref[rows, 2 * CONV_W:768] = (_window_sums(wp, lv, c, 1) - dyv[rows, :]).astype(BF16)

        @pl.when(s == cfg.b - 1)
        def _():
            for j in range(CONV_K):
                o_dw[j:j + 1, :] += _colsum(dwacc[8 * j:8 * j + 8, :])

    small_specs = [_full(sh) for sh in _SMALL_SHAPES]
    return pl.pallas_call(
        body, name=name, grid=(cfg.b,),
        in_specs=[_seq_specs(n, blk_off, 768), _seq_specs(n, blk_off, CONV_W), _seq_specs(n, blk_off, 512, 1),
                  *small_specs, _ANY, *small_specs],
        out_specs=[_seq_specs(n, blk_off, 768, 1), *small_specs],
        out_shape=[jax.ShapeDtypeStruct((cfg.r, IN_W), BF16)] + [jax.ShapeDtypeStruct(sh, F32) for sh in _SMALL_SHAPES],
        scratch_shapes=[pltpu.VMEM((n + 2 * PAD, CONV_W), F32), pltpu.VMEM((n + 2 * PAD, CONV_W), F32),
                        pltpu.VMEM((n + 2 * PAD, POOL_W), F32), pltpu.VMEM((n + 2 * PAD, POOL_W), F32),
                        pltpu.VMEM((n, POOL_W), F32), pltpu.VMEM((8 * 32, CONV_W), F32), _PHASES, _LEVELS],
        input_output_aliases={9: 0}, compiler_params=_params(1))(cpin, yconv, dmix, dw, dwb, lng, lnb, wbd, ps, *acc_in)


def _place_kv(name, du, dkvl, dkvc, with_ctx, cfg):
    nlb = cfg.nl // TM

    def body(l_ref, c_ref, du_in, o_ref):
        i = pl.program_id(0)
        o_ref[...] = jnp.where(i < nlb, l_ref[...], c_ref[...])

    return pl.pallas_call(
        body, name=name, grid=(cfg.r // TM if with_ctx else nlb,),
        in_specs=[pl.BlockSpec((TM, 256), lambda i: (jnp.minimum(i, nlb - 1), 0)),
                  pl.BlockSpec((TM, 256), lambda i: (jnp.maximum(i - nlb, 0), 0)), _ANY],
        out_specs=pl.BlockSpec((TM, 256), lambda i: (i, 2)), out_shape=jax.ShapeDtypeStruct((cfg.r, IN_W), BF16),
        input_output_aliases={2: 0}, compiler_params=_params(1))(dkvl, dkvc, du)


def _place_ctx_kv_only(name, du, dkvc, cfg):
    nlb = cfg.nl // TM

    def body(c_ref, du_in, o_ref):
        o_ref[...] = jnp.zeros_like(o_ref)
        o_ref[:, ATTN_W:ATTN_W + 256] = c_ref[...]

    return pl.pallas_call(
        body, name=name, grid=(cfg.nc // TM,), in_specs=[pl.BlockSpec((TM, 256), lambda i: (i, 0)), _ANY],
        out_specs=pl.BlockSpec((TM, IN_W), lambda i: (nlb + i, 0)), out_shape=jax.ShapeDtypeStruct((cfg.r, IN_W), BF16),
        input_output_aliases={1: 0}, compiler_params=_params(1))(dkvc, du)


def _norm_fwd(name, x, gvec, mod3, ish, isc, nblk, cfg):
    def body(x_ref, g_ref, m_ref, o_ref):
        o_ref[...] = _norm_mod(x_ref[...], g_ref[...], m_ref, ish, isc)

    row = pl.BlockSpec((TM, D), lambda i: (i, 0))
    return pl.pallas_call(
        body, name=name, grid=(nblk,),
        in_specs=[row, pl.BlockSpec((1, D), lambda i: (0, 0)), _mod_spec(cfg)], out_specs=row,
        out_shape=jax.ShapeDtypeStruct((cfg.r, D), BF16), compiler_params=_params(1))(x, gvec, mod3)


def _ffn_in(name, h, w_ffn_in, layer, nblk, cfg, ride=None):
    def body(h_ref, wg_ref, wu_ref, fac_ref, act_ref):
        hv = h_ref[...]
        g = _dot(hv, wg_ref[...])
        u = _dot(hv, wu_ref[...])
        s = _sigmoid(g)
        gs = g * s
        fac_ref[0] = ((s + gs * (1.0 - s)) * u).astype(BF16)
        fac_ref[1] = gs.astype(BF16)
        act_ref[...] = (gs * u).astype(BF16)

    wspec = lambda base: pl.BlockSpec((None, None, D, HALF_FF), lambda j, i: (layer, base + j, 0, 0))
    return _hosted_call(
        body, ride, name, (2, nblk), [h, w_ffn_in, w_ffn_in],
        [pl.BlockSpec((TM, D), lambda j, i: (i, 0)), wspec(0), wspec(2)],
        [jax.ShapeDtypeStruct((2, cfg.r, D_FF), BF16), jax.ShapeDtypeStruct((cfg.r, D_FF), BF16)],
        [pl.BlockSpec((2, TM, HALF_FF), lambda j, i: (0, i, j)), pl.BlockSpec((TM, HALF_FF), lambda j, i: (i, j))],
        [], _params(2))


def _row_block(rows, cols, max_bytes=1 << 20):
    best = 16
    for t in range(16, rows + 1, 16):
        if rows % t == 0 and t * cols * 4 <= max_bytes:
            best = t
    assert rows % best == 0
    return best


def _pair_add(name, own32, recv, c_idx):
    _, _, s0, s1 = own32.shape
    tr = _row_block(s0, s1)

    def body(c_ref, a_ref, b_ref, o_ref):
        o_ref[...] = (a_ref[...].astype(F32) + b_ref[...].astype(F32)).astype(BF16)

    grid_spec = pltpu.PrefetchScalarGridSpec(
        num_scalar_prefetch=1, grid=(N_SHARD * s0 // tr,),
        in_specs=[pl.BlockSpec((None, tr, s1), lambda i, c: (c[0], i, 0)), pl.BlockSpec((tr, s1), lambda i, c: (i, 0))],
        out_specs=pl.BlockSpec((tr, s1), lambda i, c: (i, 0)))
    out = pl.pallas_call(body, name=name, grid_spec=grid_spec, out_shape=jax.ShapeDtypeStruct((N_SHARD * s0, s1), BF16),
                         compiler_params=_params(1))(c_idx, own32.reshape(2, N_SHARD * s0, s1), recv.reshape(N_SHARD * s0, s1))
    return out.reshape(N_SHARD, s0, s1)


def _shard_sum(name, pair_sum, recv, jc_idx):
    _, s0, s1 = pair_sum.shape
    tr = _row_block(s0, s1)

    def body(jc_ref, a_ref, b_ref, o_ref):
        o_ref[...] = ((a_ref[...].astype(F32) + b_ref[0].astype(F32)) + b_ref[1].astype(F32)) + b_ref[2].astype(F32)

    grid_spec = pltpu.PrefetchScalarGridSpec(
        num_scalar_prefetch=1, grid=(s0 // tr,),
        in_specs=[pl.BlockSpec((None, tr, s1), lambda i, jc: (jc[0], i, 0)), pl.BlockSpec((3, tr, s1), lambda i, jc: (0, i, 0))],
        out_specs=pl.BlockSpec((None, tr, s1), lambda i, jc: (jc[1], i, 0)))
    return pl.pallas_call(body, name=name, grid_spec=grid_spec, out_shape=jax.ShapeDtypeStruct((2, s0, s1), F32),
                          compiler_params=_params(1))(jc_idx, pair_sum, recv)


def _adamw_math(w, g, m, v):
    m = ADAM_B1 * m + (1.0 - ADAM_B1) * g
    v = ADAM_B2 * v + (1.0 - ADAM_B2) * (g * g)
    m_hat = m / (1.0 - ADAM_B1 ** ADAM_STEP)
    v_hat = v / (1.0 - ADAM_B2 ** ADAM_STEP)
    delta = -ADAM_LR * (m_hat / (jnp.sqrt(v_hat) + ADAM_EPS) + ADAM_WD * w)
    return delta, m, v


def _adamw(name, w, g, m, v):
    rows, cols = w.shape
    tr = rows if rows % 16 else _row_block(rows, cols, 1 << 19)

    def body(w_ref, g_ref, m_ref, v_ref, d_ref, mo_ref, vo_ref):
        d, mn, vn = _adamw_math(w_ref[...], g_ref[...], m_ref[...], v_ref[...])
        d_ref[...] = d
        mo_ref[...] = mn
        vo_ref[...] = vn

    spec = pl.BlockSpec((tr, cols), lambda i: (i, 0))
    shape = jax.ShapeDtypeStruct((rows, cols), F32)
    return pl.pallas_call(body, name=name, grid=(rows // tr,), in_specs=[spec] * 4, out_specs=[spec] * 3,
                          out_shape=[shape] * 3, compiler_params=_params(1))(w, g, m, v)


def _adamw_layers(name, w, g_layers, m, v, ride=None):
    rows, cols = w.shape
    s0 = rows // 2
    tr = _row_block(s0, cols, 1 << 19)
    nb = s0 // tr

    def body(w_ref, g0_ref, g1_ref, m_ref, v_ref, g_ref, d_ref, mo_ref, vo_ref):
        g = jnp.where(pl.program_id(0) < nb, g0_ref[...], g1_ref[...])
        d, mn, vn = _adamw_math(w_ref[...], g, m_ref[...], v_ref[...])
        g_ref[...] = g
        d_ref[...] = d
        mo_ref[...] = mn
        vo_ref[...] = vn

    spec = pl.BlockSpec((tr, cols), lambda i: (i, 0))
    shape = jax.ShapeDtypeStruct((rows, cols), F32)
    return _hosted_call(
        body, ride, name, (2 * nb,), [w, g_layers[0], g_layers[1], m, v],
        [spec, pl.BlockSpec((tr, cols), lambda i: (jnp.minimum(i, nb - 1), 0)),
         pl.BlockSpec((tr, cols), lambda i: (jnp.maximum(i - nb, 0), 0)), spec, spec],
        [shape] * 4, [spec] * 4, [], _params(1))


def _position():
    return lax.axis_index("x"), lax.axis_index("y"), lax.axis_index("c")


def _other_chips(x, y):
    return [(1 - x, y), (x, 1 - y), (1 - x, 1 - y)]


def _run_ride(name, ride):
    n_in, n_out = len(ride.ins), len(ride.out_shape)

    def body(*refs):
        parts = (refs[:n_in], refs[n_in:n_in + n_out], refs[n_in + n_out:])
        ride.start(*parts)
        ride.finish(*parts)

    return pl.pallas_call(
        body, name=name, in_specs=[_ANY] * n_in, out_specs=[_ANY] * n_out, out_shape=ride.out_shape,
        scratch_shapes=ride.scratch, compiler_params=pltpu.CompilerParams(vmem_limit_bytes=VMEM_LIMIT_V7X))(*ride.ins)


def _gather_ride(shards):
    n = len(shards)

    def copies(ins, outs, scr):
        ssem, rsem = scr[n], scr[n + 1]
        x, y, c = _position()
        me, sibling = 2 * x + y, (x, y, 1 - c)

        def remote(src, dst, i, dev):
            return pltpu.make_async_remote_copy(src, dst, ssem.at[i], rsem.at[i], device_id=dev, device_id_type=_MESH)

        fetch_out, fetch_in, pass_out, pass_in = [], [], [], []
        for a, (src, dst) in enumerate(zip(ins, outs)):
            for k, (px, py) in enumerate(_other_chips(x, y)):
                j, i1, i2 = 2 * px + py, 3 * a + k, 3 * n + 3 * a + k
                fetch_out.append(remote(src.at[c], dst.at[me, c], i1, (px, py, c)))
                fetch_in.append(remote(src.at[c], dst.at[j, c], i1, (px, py, c)))
                pass_out.append(remote(dst.at[j, c], dst.at[j, c], i2, sibling))
                pass_in.append(remote(dst.at[j, 1 - c], dst.at[j, 1 - c], i2, sibling))
        return me, fetch_out, fetch_in, pass_out, pass_in

    def start(ins, outs, scr):
        bufs, lsem = scr[:n], scr[n + 2]
        me, fetch_out, _, _, _ = copies(ins, outs, scr)
        for cp in fetch_out:
            cp.start()
        loads = []
        for a, (src, buf) in enumerate(zip(ins, bufs)):
            ld = pltpu.make_async_copy(src, buf, lsem.at[2 * a])
            ld.start()
            loads.append(ld)
        for a, (ld, buf, dst) in enumerate(zip(loads, bufs, outs)):
            ld.wait()
            st = pltpu.make_async_copy(buf, dst.at[me], lsem.at[2 * a + 1])
            st.start()
            st.wait()

    def finish(ins, outs, scr):
        _, fetch_out, fetch_in, pass_out, pass_in = copies(ins, outs, scr)
        for arrived, onward in zip(fetch_in, pass_out):
            arrived.wait_recv()
            onward.start()
        for cp in pass_in:
            cp.wait_recv()
        for cp in fetch_out + pass_out:
            cp.wait_send()

    return _Ride(list(shards), [jax.ShapeDtypeStruct((N_SHARD,) + s.shape, s.dtype) for s in shards],
                 [pltpu.VMEM(s.shape, s.dtype) for s in shards]
                 + [pltpu.SemaphoreType.DMA((6 * n,)), pltpu.SemaphoreType.DMA((6 * n,)), pltpu.SemaphoreType.DMA((2 * n,))],
                 start, finish)


def _comm(name, ins, out_shape, n_remote, plan):
    n_in, n_out = len(ins), len(out_shape)

    def body(*refs):
        plan(refs[:n_in], refs[n_in:n_in + n_out], *refs[n_in + n_out:])

    return pl.pallas_call(
        body, name=name, in_specs=[_ANY] * n_in, out_specs=[_ANY] * n_out, out_shape=out_shape,
        scratch_shapes=[pltpu.SemaphoreType.DMA((n_remote,)), pltpu.SemaphoreType.DMA((n_remote,))])(*ins)


def _send_other_half(name, grads_bf):
    n = len(grads_bf)

    def plan(ins, outs, ssem, rsem):
        x, y, c = _position()
        started = []
        for a, (src, dst) in enumerate(zip(ins, outs)):
            cp = pltpu.make_async_remote_copy(src.at[1 - c], dst, ssem.at[a], rsem.at[a], device_id=(x, y, 1 - c),
                                              device_id_type=_MESH)
            cp.start()
            started.append(cp)
        for cp in started:
            cp.wait_recv()
        for cp in started:
            cp.wait_send()

    shapes = [jax.ShapeDtypeStruct(s.shape[1:], s.dtype) for s in grads_bf]
    return _comm(name, grads_bf, shapes, n, plan)


def _exchange_ride(pair_sums):
    n = len(pair_sums)

    def copies(ins, outs, scr):
        ssem, rsem = scr
        x, y, c = _position()
        return [pltpu.make_async_remote_copy(src.at[2 * px + py], dst.at[k], ssem.at[3 * a + k], rsem.at[3 * a + k],
                                             device_id=(px, py, c), device_id_type=_MESH)
                for a, (src, dst) in enumerate(zip(ins, outs)) for k, (px, py) in enumerate(_other_chips(x, y))]

    def start(ins, outs, scr):
        for cp in copies(ins, outs, scr):
            cp.start()

    def finish(ins, outs, scr):
        for cp in copies(ins, outs, scr):
            cp.wait_recv()
        for cp in copies(ins, outs, scr):
            cp.wait_send()

    return _Ride(list(pair_sums), [jax.ShapeDtypeStruct((3,) + s.shape[1:], s.dtype) for s in pair_sums],
                 [pltpu.SemaphoreType.DMA((3 * n,)), pltpu.SemaphoreType.DMA((3 * n,))], start, finish)


def _swap_reduced(name, grads):
    n = len(grads)

    def body(*refs):
        ins, outs, ssem, rsem = refs[:n], refs[n:2 * n], refs[2 * n], refs[2 * n + 1]
        x, y, c = _position()
        sent = []
        for a, (src, dst) in enumerate(zip(ins, outs)):
            cp = pltpu.make_async_remote_copy(src.at[c], dst.at[c], ssem.at[a], rsem.at[a], device_id=(x, y, 1 - c),
                                              device_id_type=_MESH)
            cp.start()
            sent.append(cp)
        for a, (src, dst) in enumerate(zip(ins, outs)):
            pltpu.make_async_remote_copy(src.at[1 - c], dst.at[1 - c], ssem.at[a], rsem.at[a], device_id=(x, y, 1 - c),
                                         device_id_type=_MESH).wait_recv()
        for cp in sent:
            cp.wait_send()

    return pl.pallas_call(
        body, name=name, in_specs=[_ANY] * n, out_specs=[_ANY] * n,
        out_shape=[jax.ShapeDtypeStruct(g.shape, g.dtype) for g in grads],
        scratch_shapes=[pltpu.SemaphoreType.DMA((n,)), pltpu.SemaphoreType.DMA((n,))],
        input_output_aliases={a: a for a in range(n)})(*grads)


_FLIPS = [(dx, dy, dc) for dx in (0, 1) for dy in (0, 1) for dc in (0, 1) if dx + dy + dc]
_VMEM = pl.BlockSpec(memory_space=pltpu.VMEM)


def _to_all(src_of, dst, ssem, rsem):
    x, y, c = _position()
    me = 4 * x + 2 * y + c
    peers = [((x + dx) % 2, (y + dy) % 2, (c + dc) % 2) for dx, dy, dc in _FLIPS]
    sent = []
    for k, (px, py, pc) in enumerate(peers):
        cp = pltpu.make_async_remote_copy(src_of(2 * px + py), dst.at[me], ssem.at[k], rsem.at[k],
                                          device_id=(px, py, pc), device_id_type=_MESH)
        cp.start()
        sent.append(cp)
    for k, (px, py, pc) in enumerate(peers):
        pltpu.make_async_remote_copy(src_of(2 * px + py), dst.at[4 * px + 2 * py + pc], ssem.at[k], rsem.at[k],
                                     device_id=(px, py, pc), device_id_type=_MESH).wait_recv()
    for cp in sent:
        cp.wait_send()
    return me, 2 * x + y


def _share_small(name, block, total):
    shape = block.shape

    def body(in_ref, out_ref, *scratch):
        buf, ssem, rsem = (out_ref,) + scratch if not total else scratch
        me, _ = _to_all(lambda chip: in_ref, buf, ssem, rsem)
        buf[me] = in_ref[...]
        if total:
            acc = buf[0]
            for d in range(1, 8):
                acc = acc + buf[d]
            out_ref[...] = acc

    sems = [pltpu.SemaphoreType.DMA((7,)), pltpu.SemaphoreType.DMA((7,))]
    return pl.pallas_call(
        body, name=name, in_specs=[_VMEM], out_specs=_VMEM,
        out_shape=jax.ShapeDtypeStruct(shape if total else (8,) + shape, F32),
        scratch_shapes=([pltpu.VMEM((8,) + shape, F32)] if total else []) + sems,
        compiler_params=pltpu.CompilerParams(vmem_limit_bytes=VMEM_LIMIT_V7X))(block)


def _mod_rows_exchange(mv):
    def body(mv_ref, out_ref, ssem, rsem):
        x, y, c = _position()
        me = 2 * x + y
        out_ref[me] = mv_ref[4 * x + 2 * y + c]
        sent = []
        for k, (px, py) in enumerate(_other_chips(x, y)):
            cp = pltpu.make_async_remote_copy(mv_ref.at[4 * px + 2 * py + c], out_ref.at[me], ssem.at[k], rsem.at[k],
                                              device_id=(px, py, c), device_id_type=_MESH)
            cp.start()
            sent.append(cp)
        for k, (px, py) in enumerate(_other_chips(x, y)):
            pltpu.make_async_remote_copy(mv_ref.at[0], out_ref.at[2 * px + py], ssem.at[k], rsem.at[k],
                                         device_id=(px, py, c), device_id_type=_MESH).wait_recv()
        for cp in sent:
            cp.wait_send()

    return pl.pallas_call(
        body, name="mod_rows_exchange", in_specs=[_VMEM], out_specs=_VMEM,
        out_shape=jax.ShapeDtypeStruct((N_SHARD,) + mv.shape[1:], F32),
        scratch_shapes=[pltpu.SemaphoreType.DMA((3,)), pltpu.SemaphoreType.DMA((3,))],
        compiler_params=pltpu.CompilerParams(vmem_limit_bytes=VMEM_LIMIT_V7X))(mv)


def _mod_grad_exchange(dmj, dm_rows):
    def body(dmj_ref, rows_ref, out_ref, bias_ref, ssem, rsem):
        me, chip = _to_all(lambda j: dmj_ref.at[j], out_ref, ssem, rsem)
        out_ref[me] = dmj_ref[chip]
        for l in range(2):
            bias_ref[l] = _colsum(rows_ref[l])

    return pl.pallas_call(
        body, name="mod_grad_exchange", in_specs=[_VMEM, _VMEM], out_specs=[_VMEM, _VMEM],
        out_shape=[jax.ShapeDtypeStruct((8,) + dmj.shape[1:], F32), jax.ShapeDtypeStruct((2, 1, dm_rows.shape[-1]), F32)],
        scratch_shapes=[pltpu.SemaphoreType.DMA((7,)), pltpu.SemaphoreType.DMA((7,))],
        compiler_params=pltpu.CompilerParams(vmem_limit_bytes=VMEM_LIMIT_V7X))(dmj, dm_rows)


def _pack(arrays):
    flat = jnp.concatenate([a.reshape(-1).astype(F32) for a in arrays])
    total = flat.shape[0]
    rows = -(-total // (8 * PACK_LANES)) * 8
    return jnp.pad(flat, (0, rows * PACK_LANES - total)).reshape(rows, PACK_LANES)


def _unpack(pack, shapes):
    flat, out, pos = pack.reshape(-1), [], 0
    for sh in shapes:
        size = int(np.prod(sh)) if len(sh) else 1
        out.append(flat[pos:pos + size].reshape(sh))
        pos += size
    return out


def _block_diag(pw):
    out = jnp.zeros((POOL_W, POOL_W), pw.dtype)
    for g in range(4):
        out = out.at[g * 64:(g + 1) * 64, g * 64:(g + 1) * 64].set(pw[g])
    return out


def _local_step(x, ctx, small, mod3s, loss_target, comm):
    cfg = _Cfg(x.shape[0], x.shape[1], ctx.shape[1])
    assert cfg.seq % TM == 0 and cfg.nc % TM == 0 and cfg.seq % cfg.ctx == 0 and cfg.ctx % CHUNK == 0
    nb_all, nb_lat = cfg.r // TM, cfg.nl // TM
    last = 1
    wf, big = comm.wf, comm.grads
    cos_t, sin_t = _rope_tables(cfg.seq)
    xs = jnp.concatenate([x.reshape(cfg.nl, D), ctx.reshape(cfg.nc, D)], axis=0)
    row = lambda w: pl.BlockSpec((TM, w), lambda i, j, k: (i, 0))
    mod3_spec = pl.BlockSpec((None, 6, D), lambda i, j, k: (cfg.mod_row(i), 0, 0))
    whole = lambda rows: pl.BlockSpec((None, rows, D), lambda i, j, k: (0, 0, 0))

    def conv_params(l):
        dw = jnp.pad(wf[l]["conv_dw"], ((0, 1), (0, 0)))
        return (dw, small["conv_dw_b"][l][None], small["conv_ln_g"][l][None], small["conv_ln_b"][l][None],
                _block_diag(small["pool_w"][l]).astype(BF16), small["pool_scale"][l][None])

    def residual_epi(ig):
        def epi(acc, ex, outs):
            x_ref, m_ref = ex
            outs[0][...] = x_ref[...] + m_ref[ig:ig + 1, :] * acc
            outs[1][...] = acc.astype(BF16)
        return epi

    def hosted(name, call):
        outs, got = call(comm.ride(name))
        comm.landed(name, got)
        return outs

    saved = []
    for l in range(2):
        nb = nb_lat if l == last else nb_all
        wl = wf[l]
        mod3 = mod3s[l]
        h1, qkv, cpin = _in_proj(f"in_proj{l}", xs, small["norm1_g"][l][None], mod3, wl["w_in"], 0, cos_t, sin_t, cfg)
        mix, lse = hosted(f"att_fwd{l}", lambda ride: _att_fwd(f"att_fwd{l}", qkv, small["attn_sink"][l], l != last, cfg, ride))
        prm = conv_params(l)
        mix, yconv = _convpool_fwd(f"convpool_fwd_lat{l}", cpin, mix, None, prm, cfg.seq, 0, cfg)
        if l != last:
            mix, yconv = _convpool_fwd(f"convpool_fwd_ctx{l}", cpin, mix, yconv, prm, cfg.ctx, cfg.nl // cfg.ctx, cfg)
        x1, y1 = _mm(f"out_proj{l}", "nn", (nb, 1, 1), mix, wl["w_out"].reshape(1, D, D), row(D), whole(D),
                     [jax.ShapeDtypeStruct((cfg.r, D), F32), jax.ShapeDtypeStruct((cfg.r, D), BF16)], [row(D), row(D)],
                     extras=[xs, mod3], extra_specs=[row(D), mod3_spec], epi=residual_epi(2))
        h2 = _norm_fwd(f"norm2_fwd{l}", x1, small["norm2_g"][l][None], mod3, 3, 4, nb, cfg)
        gu, act = hosted(f"ffn_in{l}", lambda ride: _ffn_in(f"ffn_in{l}", h2, wl["w_ffn_in"], 0, nb, cfg, ride))
        x2, y2 = _mm(f"ffn_out{l}", "nn", (nb, 1, 1), act, wl["w_ffn_out"].reshape(1, D_FF, D), row(D_FF), whole(D_FF),
                     [jax.ShapeDtypeStruct((cfg.r, D), F32), jax.ShapeDtypeStruct((cfg.r, D), BF16)], [row(D), row(D)],
                     extras=[x1, mod3], extra_specs=[row(D), mod3_spec], epi=residual_epi(5))
        saved.append(dict(mod3=mod3, x0=xs, h1=h1, qkv=qkv, cpin=cpin, mix=mix, yconv=yconv, lse=lse, y1=y1, x1=x1,
                          h2=h2, gu=gu, act=act, y2=y2, prm=prm))
        xs = x2

    dx, loss, d_final_g, dy2, dg2 = _loss_head(xs, loss_target.reshape(cfg.nl, D), small["final_g"][None],
                                               saved[last]["y2"], saved[last]["mod3"], 5, cfg)

    sg = {k: [None, None] for k in ("norm1_g", "norm2_g", "conv_dw", "conv_dw_b", "conv_ln_g", "conv_ln_b",
                                    "attn_sink", "pool_w", "pool_scale")}
    dms = [None, None]

    def swiglu_bwd_epi(acc, ex, outs):
        outs[0][0] = (acc * ex[0][0].astype(F32)).astype(BF16)
        outs[0][1] = (acc * ex[0][1].astype(F32)).astype(BF16)

    def halves_epi(acc, ex, outs):
        h = acc.shape[0] // 2
        outs[0][0] = acc[:h].astype(BF16)
        outs[0][1] = acc[h:].astype(BF16)

    def row_shards_epi(n):
        def epi(acc, ex, outs):
            s0 = acc.shape[0] // n
            h = s0 // 2
            for t in range(n):
                for half in range(2):
                    outs[0][half, t] = acc[t * s0 + half * h:t * s0 + (half + 1) * h].astype(BF16)
        return epi

    def col_shards_epi(acc, ex, outs):
        h = acc.shape[0] // 2
        for j in range(N_SHARD):
            for half in range(2):
                outs[0][half, j] = acc[half * h:(half + 1) * h, j * IN_SHARD:(j + 1) * IN_SHARD].astype(BF16)

    for l in (1, 0):
        sv = saved[l]
        mod3 = sv["mod3"]
        nb = nb_lat if l == last else nb_all
        tr = _dw_rows(nb * TM)
        steps = nb * TM // tr
        wl = wf[l]
        gu_spec = pl.BlockSpec((2, TM, HALF_FF), lambda j, i, k: (0, i, j))
        df = _mm(f"ffn_out_bwd{l}", "nt", (2, nb, 1), dy2, wl["w_ffn_out"].reshape(1, D_FF, D),
                 pl.BlockSpec((TM, D), lambda j, i, k: (i, 0)), pl.BlockSpec((None, HALF_FF, D), lambda j, i, k: (0, j, 0)),
                 [jax.ShapeDtypeStruct((2, cfg.r, D_FF), BF16)], [gu_spec], extras=[sv["gu"]], extra_specs=[gu_spec],
                 epi=swiglu_bwd_epi)[0]
        big[l]["w_ffn_out"] = _mm_dw(
            f"dw_ffn_out{l}", sv["act"], dy2, pl.BlockSpec((tr, HALF_FF), lambda i, j, k: (k, j)),
            pl.BlockSpec((tr, D), lambda i, j, k: (k, 0)), (2, N_SHARD, D_FF // 8, D),
            pl.BlockSpec((2, 2, D_FF // 8, D), lambda i, j, k: (0, j, 0, 0)), 2, (HALF_FF, D), steps, row_shards_epi(2))[0]
        ride = comm.ride(f"dw_ffn_in{l}")
        res = _mm_dw(f"dw_ffn_in{l}", sv["h2"], df, pl.BlockSpec((tr, D), lambda i, j, k: (k, 0)),
                     pl.BlockSpec((None, tr, HALF_FF), lambda i, j, k: (j // 2, k, j % 2)), (2, N_SHARD, D // 2, HALF_FF),
                     pl.BlockSpec((2, None, D // 2, HALF_FF), lambda i, j, k: (0, j, 0, 0)), N_SHARD, (D, HALF_FF), steps,
                     halves_epi, ride=ride)
        res, got = res if ride is not None else (res, [])
        big[l]["w_ffn_in"] = res[0]
        comm.landed(f"dw_ffn_in{l}", got)
        dx1, dsh2, dsc2, dn2, dy1, dg1 = _proj_norm_bwd(
            f"ffn_in_bwd{l}", df, pl.BlockSpec((2, TM, D_FF), lambda i: (0, i, 0)),
            lambda a_ref, j: a_ref[j // 2, :, (j % 2) * HALF_FF:(j % 2 + 1) * HALF_FF], wl["w_ffn_in"],
            sv["x1"], dx, small["norm2_g"][l][None], mod3, 4, nb, False, (sv["y1"], mod3, 2), cfg)
        dmix = _mm(f"out_proj_bwd{l}", "nt", (nb, 1, 1), dy1, wl["w_out"].reshape(1, D, D), row(D), whole(D),
                   [jax.ShapeDtypeStruct((cfg.r, D), F32)], [row(D)])[0]
        big[l]["w_out"] = _mm_dw(
            f"dw_out{l}", sv["mix"], dy1, pl.BlockSpec((tr, D), lambda i, j, k: (k, 0)),
            pl.BlockSpec((tr, D), lambda i, j, k: (k, 0)), (2, N_SHARD, D // 8, D),
            pl.BlockSpec((2, N_SHARD, D // 8, D), lambda i, j, k: (0, 0, 0, 0)), 1, (D, D), steps, row_shards_epi(N_SHARD))[0]
        du, dkvl, dkvc, dsink = hosted(f"att_bwd{l}", lambda ride: _att_bwd(
            f"att_bwd{l}", sv["qkv"], sv["mix"], dmix, sv["lse"], small["attn_sink"][l], cos_t, sin_t, l != last, cfg, ride))
        acc = [du] + [jnp.zeros(sh, F32) for sh in _SMALL_SHAPES]
        acc = _convpool_bwd(f"convpool_bwd_lat{l}", sv["cpin"], sv["yconv"], dmix, sv["prm"], acc, cfg.seq, 0, cfg)
        if l != last:
            acc = _convpool_bwd(f"convpool_bwd_ctx{l}", sv["cpin"], sv["yconv"], dmix, sv["prm"], acc, cfg.ctx,
                                cfg.nl // cfg.ctx, cfg)
        du, g_dw, g_dwb, g_lng, g_lnb, g_wbd, g_ps = acc
        du = _place_kv(f"place_kv{l}", du, dkvl, dkvc, l != last, cfg)
        if l == last:
            du = _place_ctx_kv_only(f"place_ctx_kv{l}", du, dkvc, cfg)
        sg["attn_sink"][l] = dsink[:, 0]
        sg["conv_dw"][l], sg["conv_dw_b"][l], sg["conv_ln_g"][l], sg["conv_ln_b"][l] = g_dw[:CONV_K], g_dwb[0], g_lng[0], g_lnb[0]
        sg["pool_w"][l] = jnp.stack([g_wbd[g * 64:(g + 1) * 64, g * 64:(g + 1) * 64] for g in range(4)])
        sg["pool_scale"][l] = g_ps[0]
        tr_all = _dw_rows(cfg.r)
        big[l]["w_in"] = _mm_dw(
            f"dw_in{l}", sv["h1"], du, pl.BlockSpec((tr_all, D), lambda i, j, k: (k, 0)),
            pl.BlockSpec((tr_all, IN_W), lambda i, j, k: (k, 0)), (2, N_SHARD, D // 2, IN_SHARD),
            pl.BlockSpec((2, N_SHARD, D // 2, IN_SHARD), lambda i, j, k: (0, 0, 0, 0)), 1, (D, IN_W), cfg.r // tr_all,
            col_shards_epi)[0]
        below = (saved[l - 1]["y2"], saved[l - 1]["mod3"], 5) if l > 0 else None
        res = _proj_norm_bwd(
            f"in_proj_bwd{l}", du, pl.BlockSpec((TM, IN_W), lambda i: (i, 0)),
            lambda a_ref, j: a_ref[:, j * IN_SHARD:(j + 1) * IN_SHARD], wl["w_in"],
            sv["x0"], dx1, small["norm1_g"][l][None], mod3, 1, nb_all, l == last, below, cfg, dx_latent_only=l == 0)
        dx, dsh1, dsc1, dn1 = res[:4]
        sg["norm1_g"][l], sg["norm2_g"][l] = dn1[0], dn2[0]
        parts = [dsh1, dsc1, dg1, dsh2, dsc2, dg2]
        if below is not None:
            dy2, dg2 = res[4:]
        dm = jnp.concatenate([t[:cfg.b, 0, :] for t in parts], axis=1)
        live = (0, 1) if l == last else range(6)
        dm_ctx = jnp.concatenate([t[cfg.b, 0, :] if i in live else jnp.zeros((D,), F32) for i, t in enumerate(parts)])
        dms[l] = jnp.concatenate([dm, dm_ctx[None, :], jnp.zeros((MOD_ROWS - cfg.b - 1, 6 * D), F32)], axis=0)

    grad_x = dx.reshape(x.shape)
    small_grads = {k: jnp.stack(v) for k, v in sg.items()}
    small_grads["final_g"] = d_final_g[0]
    return loss, grad_x, small_grads, dms


_BIG = ("w_in", "w_out", "w_ffn_in", "w_ffn_out")
_TAPS = "conv_dw"
_SMALL = ("c_ctx", "b_mod", "norm1_g", "norm2_g", "conv_dw", "conv_dw_b", "conv_ln_g", "conv_ln_b", "attn_sink",
          "pool_w", "pool_scale", "final_g")
_ORDER = ("c_ctx", "w_mod", "b_mod", "norm1_g", "norm2_g", "w_in", "conv_dw", "conv_dw_b", "conv_ln_g", "conv_ln_b",
          "attn_sink", "pool_w", "pool_scale", "w_out", "w_ffn_in", "w_ffn_out", "final_g")
_GATHER_HOSTS = {
    "first": ((0, "w_in"),),
    "att_fwd0": ((0, "w_out"), (0, "w_ffn_in"), (0, _TAPS)),
    "ffn_in0": ((0, "w_ffn_out"), (1, "w_in"), (1, "w_out"), (1, _TAPS), (1, "w_ffn_out")),
    "att_fwd1": ((1, "w_ffn_in"),),
}
_REDUCE_HOSTS = {
    "dw_ffn_in0": tuple((1, k) for k in _BIG),
    "att_bwd0": ((0, "w_ffn_in"), (0, "w_ffn_out")),
    "last": ((0, "w_in"), (0, "w_out")),
}


class _Comm:
    def __init__(self, w, c_idx, jc_idx):
        self.shapes = {k: w[k].shape[1:] for k in _BIG}
        self.c_idx, self.jc_idx = c_idx, jc_idx
        halves = lambda a: a.reshape(2, a.shape[0] // 2, a.shape[1])
        taps = jnp.pad(w[_TAPS], ((0, 0), (0, 1), (0, 64)))
        cast = {k: w[k].astype(BF16) for k in _BIG}
        self.shards = [{**{k: halves(cast[k][l]) for k in _BIG}, _TAPS: halves(taps[l])} for l in range(2)]
        self.wf = [dict(), dict()]
        self.grads = [dict(), dict()]
        self.reduced = [dict(), dict()]
        self._open = {}
        self.landed("first", _run_ride("gather_first", self.ride("first")))

    def ride(self, host):
        if host in _GATHER_HOSTS:
            return _gather_ride([self.shards[layer][k] for layer, k in _GATHER_HOSTS[host]])
        if host in _REDUCE_HOSTS:
            what = _REDUCE_HOSTS[host]
            mine = [self.grads[layer][k] for layer, k in what]
            other = _send_other_half(f"send_other_half_{host}", mine)
            pair = [_pair_add(f"pair_add{layer}_{k}", a, b, self.c_idx) for (layer, k), a, b in zip(what, mine, other)]
            self._open[host] = pair
            return _exchange_ride(pair)
        return None

    def landed(self, host, got):
        if host in _GATHER_HOSTS:
            for (layer, k), f in zip(_GATHER_HOSTS[host], got):
                if k == _TAPS:
                    taps = f.reshape(N_SHARD, 32, 128)[:, :CONV_K, :64]
                    self.wf[layer][k] = jnp.transpose(taps, (1, 0, 2)).reshape(CONV_K, CONV_W)
                else:
                    self.wf[layer][k] = f.reshape((1, N_SHARD) + self.shapes[k])
        if host in _REDUCE_HOSTS:
            what = _REDUCE_HOSTS[host]
            mine = [_shard_sum(f"shard_sum{layer}_{k}", a, b, self.jc_idx)
                    for (layer, k), a, b in zip(what, self._open.pop(host), got)]
            for (layer, k), g in zip(what, _swap_reduced(f"swap_reduced_{host}", mine)):
                self.reduced[layer][k] = g.reshape(self.shapes[k])


def _conditioning(c, c_ctx, w_mod, b_mod, chip):
    b = c.shape[0]
    block = jnp.concatenate([c, c_ctx[None, :], jnp.zeros((8 - b - 1, D), F32)], axis=0)
    c_all = _share_small("share_c", block, False).reshape(64, D)
    bias = lax.dynamic_slice_in_dim(b_mod, chip * MOD_W, MOD_W, axis=1)
    full = lambda r, q: pl.BlockSpec((r, q), lambda i, j, k: (0, 0))

    def bias_epi(acc, ex, outs):
        outs[0][...] = acc + ex[0][...]

    mv = [_mm(f"mod_fwd{l}", "nn", (1, 1, 1), c_all, w_mod[l], full(64, D), full(D, MOD_W),
              [jax.ShapeDtypeStruct((64, MOD_W), F32)], [full(64, MOD_W)], extras=[bias[l][None]],
              extra_specs=[full(1, MOD_W)], a_fn=_silu, epi=bias_epi)[0] for l in range(2)]
    by_dev = jnp.transpose(jnp.stack(mv).reshape(2, 8, 8, MOD_W), (1, 0, 2, 3))
    rows = jnp.transpose(_mod_rows_exchange(by_dev), (1, 2, 0, 3)).reshape(2, 8, 6 * D)
    rows = jnp.pad(rows, ((0, 0), (0, MOD_ROWS - 8), (0, 0)))
    return [rows[l].reshape(MOD_ROWS, 6, D) for l in range(2)], c_all


def _conditioning_bwd(dms, c_all, w_mod, b):
    dm = jnp.stack([d[:8] for d in dms])
    by_chip = jnp.transpose(dm.reshape(2, 8, N_SHARD, MOD_W), (2, 0, 1, 3))
    gathered, d_bias = _mod_grad_exchange(by_chip, dm)
    dm_all = jnp.transpose(gathered, (1, 0, 2, 3)).reshape(2, 64, MOD_W)
    full = lambda r, q: pl.BlockSpec((r, q), lambda i, j, k: (0, 0))

    def ctx_rows_epi(acc, ex, outs):
        row = lax.broadcasted_iota(jnp.int32, acc.shape, 0) % 8
        outs[0][...] = _colsum(jnp.where(row == b, acc * _dsilu(ex[0][...]), 0.0))

    g_mod, d_ctx = [], jnp.zeros((D,), F32)
    for l in range(2):
        g_mod.append(_mm(f"dw_mod{l}", "tn", (1, 1, 1), c_all, dm_all[l], full(64, D), full(64, MOD_W),
                         [jax.ShapeDtypeStruct((D, MOD_W), F32)], [full(D, MOD_W)], a_fn=_silu)[0])
        part = _mm(f"mod_bwd{l}", "nt", (1, 1, 1), dm_all[l], w_mod[l], full(64, MOD_W), full(D, MOD_W),
                   [jax.ShapeDtypeStruct((1, D), F32)], [full(1, D)], extras=[c_all], extra_specs=[full(64, D)],
                   epi=ctx_rows_epi)[0]
        d_ctx = d_ctx + part[0]
    return g_mod, d_bias[:, 0, :], d_ctx


def kernel(x, c, ctx, c_ctx, w_mod, b_mod, norm1_g, norm2_g, w_in, conv_dw, conv_dw_b, conv_ln_g, conv_ln_b, attn_sink, pool_w, pool_scale, w_out, w_ffn_in, w_ffn_out, final_g, loss_target, m_c_ctx, m_w_mod, m_b_mod, m_norm1_g, m_norm2_g, m_w_in, m_conv_dw, m_conv_dw_b, m_conv_ln_g, m_conv_ln_b, m_attn_sink, m_pool_w, m_pool_scale, m_w_out, m_w_ffn_in, m_w_ffn_out, m_final_g, v_c_ctx, v_w_mod, v_b_mod, v_norm1_g, v_norm2_g, v_w_in, v_conv_dw, v_conv_dw_b, v_conv_ln_g, v_conv_ln_b, v_attn_sink, v_pool_w, v_pool_scale, v_w_out, v_w_ffn_in, v_w_ffn_out, v_final_g):
    w = dict(c_ctx=c_ctx, w_mod=w_mod, b_mod=b_mod, norm1_g=norm1_g, norm2_g=norm2_g, w_in=w_in, conv_dw=conv_dw,
             conv_dw_b=conv_dw_b, conv_ln_g=conv_ln_g, conv_ln_b=conv_ln_b, attn_sink=attn_sink, pool_w=pool_w,
             pool_scale=pool_scale, w_out=w_out, w_ffn_in=w_ffn_in, w_ffn_out=w_ffn_out, final_g=final_g)
    m = dict(c_ctx=m_c_ctx, w_mod=m_w_mod, b_mod=m_b_mod, norm1_g=m_norm1_g, norm2_g=m_norm2_g, w_in=m_w_in,
             conv_dw=m_conv_dw, conv_dw_b=m_conv_dw_b, conv_ln_g=m_conv_ln_g, conv_ln_b=m_conv_ln_b,
             attn_sink=m_attn_sink, pool_w=m_pool_w, pool_scale=m_pool_scale, w_out=m_w_out, w_ffn_in=m_w_ffn_in,
             w_ffn_out=m_w_ffn_out, final_g=m_final_g)
    v = dict(c_ctx=v_c_ctx, w_mod=v_w_mod, b_mod=v_b_mod, norm1_g=v_norm1_g, norm2_g=v_norm2_g, w_in=v_w_in,
             conv_dw=v_conv_dw, conv_dw_b=v_conv_dw_b, conv_ln_g=v_conv_ln_g, conv_ln_b=v_conv_ln_b,
             attn_sink=v_attn_sink, pool_w=v_pool_w, pool_scale=v_pool_scale, w_out=v_w_out, w_ffn_in=v_w_ffn_in,
             w_ffn_out=v_w_ffn_out, final_g=v_final_g)
    xi, yi, ci = _position()
    chip = 2 * xi + yi
    mod3s, c_all = _conditioning(c, c_ctx, w_mod, b_mod, chip)
    comm = _Comm(w, jnp.reshape(ci, (1,)).astype(jnp.int32), jnp.stack([chip, ci]).astype(jnp.int32))
    small = {k: w[k] for k in _SMALL if k not in ("conv_dw", "c_ctx", "b_mod")}
    loss, grad_x, sgrads, dms = _local_step(x, ctx, small, mod3s, loss_target, comm)
    g_mod, sgrads["b_mod"], d_ctx = _conditioning_bwd(dms, c_all, w_mod, c.shape[0])
    sgrads["c_ctx"] = 0.5 * d_ctx

    names = list(_SMALL)
    total = _share_small("sum_small", _pack([loss] + [sgrads[k] for k in names]), True)
    parts = _unpack(total, [()] + [sgrads[k].shape for k in names])
    loss_out = parts[0]
    gsmall = dict(zip(names, parts[1:]))
    gsmall["conv_dw"] = lax.dynamic_slice_in_dim(gsmall["conv_dw"], chip * 64, 64, axis=2)

    grads, delta, new_m, new_v = dict(gsmall), {}, {}, {}
    for k in ("w_mod",) + _BIG:
        s0, s1 = w[k].shape[1:]
        flat = lambda a: a.reshape(2 * s0, s1)
        g_layers = g_mod if k == "w_mod" else [comm.reduced[l][k] for l in range(2)]
        outs, got = _adamw_layers(f"adamw_{k}", flat(w[k]), g_layers, flat(m[k]), flat(v[k]),
                                  comm.ride("last") if k == "w_mod" else None)
        if k == "w_mod":
            comm.landed("last", got)
        grads[k], delta[k], new_m[k], new_v[k] = [a.reshape(w[k].shape) for a in outs]
    d_, m_, v_ = _adamw("adamw_small", _pack([w[k] for k in names]), _pack([gsmall[k] for k in names]),
                        _pack([m[k] for k in names]), _pack([v[k] for k in names]))
    sshapes = [w[k].shape for k in names]
    for k, a, b, e in zip(names, _unpack(d_, sshapes), _unpack(m_, sshapes), _unpack(v_, sshapes)):
        delta[k], new_m[k], new_v[k] = a, b, e
    return (loss_out, grad_x, *[grads[k] for k in _ORDER], *[delta[k] for k in _ORDER],
            *[new_m[k] for k in _ORDER], *[new_v[k] for k in _ORDER])
```

```python
from typing import NamedTuple

import jax
import jax.numpy as jnp
import numpy as np
from jax import lax
from jax.experimental import pallas as pl
from jax.experimental.pallas import tpu as pltpu

F32 = jnp.float32
BF16 = jnp.bfloat16

D = 1024
GRID_W = 64
HEAD_DIM = 64
N_HEADS = 8
ATTN_W = 512
CONV_W = 256
POOL_W = 256
IN_W = 1536
D_FF = 2816
CONV_K = 31
QB = 128
ROPE_BASE = 10000.0
EPS = 1e-6
NEG = -1e30
N_SHARD = 4
IN_SHARD = IN_W // N_SHARD
HALF_FF = D_FF // 2
MOD_W = 6 * D // N_SHARD
MOD_ROWS = 16
PACK_LANES = 128

ADAM_LR = 0.001
ADAM_B1 = 0.9
ADAM_B2 = 0.999
ADAM_EPS = 1e-08
ADAM_WD = 0.01
ADAM_STEP = 10

VMEM_LIMIT_V7X = 56 * 1024 * 1024
TM = 512
TR_MAX = 1024
CHUNK = 256
PAD = 16

_MESH = pl.DeviceIdType.MESH
_ANY = pl.BlockSpec(memory_space=pl.ANY)
_DIMS = {"nn": (((1,), (0,)), ((), ())), "nt": (((1,), (1,)), ((), ())), "tn": (((0,), (0,)), ((), ()))}


class _Cfg(NamedTuple):
    b: int
    seq: int
    ctx: int

    @property
    def nl(self):
        return self.b * self.seq

    @property
    def nc(self):
        return self.b * self.ctx

    @property
    def r(self):
        return self.nl + self.nc

    def mod_row(self, i):
        return jnp.where(i < self.nl // TM, i // (self.seq // TM), self.b)

    def first_of_row(self, i):
        nlb = self.nl // TM
        return jnp.logical_or(jnp.logical_and(i < nlb, i % (self.seq // TM) == 0), i == nlb)


def _params(n_grid=0):
    sem = ("arbitrary",) * n_grid if n_grid else None
    return pltpu.CompilerParams(dimension_semantics=sem, vmem_limit_bytes=VMEM_LIMIT_V7X)


def _dot(a, b, mode="nn"):
    return lax.dot_general(a.astype(BF16), b.astype(BF16), _DIMS[mode], preferred_element_type=F32)


def _sigmoid(x):
    return 1.0 / (1.0 + jnp.exp(-x))


def _silu(x):
    return x * _sigmoid(x)


def _dsilu(x):
    s = _sigmoid(x)
    return s * (1.0 + x * (1.0 - s))


def _colsum(v):
    return jnp.sum(v, axis=0, keepdims=True)


def _dw_rows(rows):
    return TR_MAX if rows % TR_MAX == 0 else TM


def _epi_store(acc, ex, outs):
    for o in outs:
        o[...] = acc.astype(o.dtype)


class _Ride(NamedTuple):
    ins: list
    out_shape: list
    scratch: list
    start: object
    finish: object


class _Hosted(NamedTuple):
    ride: _Ride
    n_in: int
    n_out: int
    grid: tuple

    def split(self, refs):
        n_ri, n_ro, n_rs = len(self.ride.ins), len(self.ride.out_shape), len(self.ride.scratch)
        r_in = refs[self.n_in:self.n_in + n_ri]
        r_out = refs[self.n_in + n_ri + self.n_out:self.n_in + n_ri + self.n_out + n_ro]
        own = refs[:self.n_in] + refs[self.n_in + n_ri:self.n_in + n_ri + self.n_out] + \
            refs[self.n_in + n_ri + self.n_out + n_ro:len(refs) - n_rs]
        return own, (r_in, r_out, refs[len(refs) - n_rs:])

    def start(self, parts):
        ids = [pl.program_id(d) for d in range(len(self.grid))]
        first = ids[0] == 0
        for i in ids[1:]:
            first = jnp.logical_and(first, i == 0)
        pl.when(first)(lambda: self.ride.start(*parts))

    def finish(self, parts):
        ids = [pl.program_id(d) for d in range(len(self.grid))]
        last = ids[0] == self.grid[0] - 1
        for i, g in zip(ids[1:], self.grid[1:]):
            last = jnp.logical_and(last, i == g - 1)
        pl.when(last)(lambda: self.ride.finish(*parts))


def _hosted_call(body, ride, name, grid, ins, in_specs, out_shape, out_specs, scratch, params):
    if ride is None:
        res = pl.pallas_call(body, name=name, grid=grid, in_specs=in_specs, out_specs=out_specs, out_shape=out_shape,
                             scratch_shapes=scratch, compiler_params=params)(*ins)
        return list(res), []
    host = _Hosted(ride, len(ins), len(out_shape), tuple(grid))

    def hosted(*refs):
        own, parts = host.split(refs)
        host.start(parts)
        body(*own)
        host.finish(parts)

    res = pl.pallas_call(
        hosted, name=name, grid=grid, in_specs=list(in_specs) + [_ANY] * len(ride.ins),
        out_specs=list(out_specs) + [_ANY] * len(ride.out_shape), out_shape=list(out_shape) + list(ride.out_shape),
        scratch_shapes=list(scratch) + list(ride.scratch), compiler_params=params)(*ins, *ride.ins)
    return list(res[:len(out_shape)]), list(res[len(out_shape):])


def _mm(name, mode, grid, a, b, a_spec, b_spec, out_shape, out_specs, acc_shape=None, extras=(),
        extra_specs=(), a_fn=None, epi=_epi_store, ride=None):
    nk = grid[2]
    n_ex, n_out = len(extras), len(out_shape)

    def body(*refs):
        a_ref, b_ref = refs[:2]
        ex = refs[2:2 + n_ex]
        outs = refs[2 + n_ex:2 + n_ex + n_out]
        av = a_ref[...]
        if a_fn is not None:
            av = a_fn(av)
        part = _dot(av, b_ref[...], mode)
        if nk == 1:
            epi(part, ex, outs)
        else:
            acc = refs[-1]
            k = pl.program_id(2)

            @pl.when(k == 0)
            def _():
                acc[...] = part

            @pl.when(k > 0)
            def _():
                acc[...] += part

            @pl.when(k == nk - 1)
            def _():
                epi(acc[...], ex, outs)

    scratch = [] if nk == 1 else [pltpu.VMEM(acc_shape, F32)]
    outs, ride_outs = _hosted_call(body, ride, name, grid, [a, b, *extras], [a_spec, b_spec, *extra_specs],
                                   list(out_shape), list(out_specs), scratch, _params(3))
    return outs if ride is None else (outs, ride_outs)


def _mm_dw(name, a, b, a_spec, b_spec, out_shape, out_spec, n_out_blocks, acc_shape, n_steps, epi, a_fn=None,
           extras=(), extra_specs=(), extra_out=(), extra_out_specs=(), ride=None):
    grid = (1, n_out_blocks, n_steps)
    outs = [jax.ShapeDtypeStruct(out_shape, BF16)] + list(extra_out)
    return _mm(name, "tn", grid, a, b, a_spec, b_spec, outs, [out_spec, *extra_out_specs], acc_shape, extras,
               extra_specs, a_fn, epi, ride)


def _gate_step(i, dxv, y_ref, m_ref, ig, dy_ref, dgate_ref, cfg):
    dy_ref[...] = (dxv * m_ref[ig:ig + 1, :]).astype(BF16)
    _accumulate_rows(cfg.first_of_row(i), dgate_ref, _colsum(dxv * y_ref[...].astype(F32)))


def _proj_norm_bwd(name, a, a_spec, pick, w, x, dres, gvec, mod3, isc, nblk, res_latent_only, gate, cfg,
                   dx_latent_only=False):
    ns = w.shape[-1]
    nlb = cfg.nl // TM

    def body(a_ref, w_ref, x_ref, dres_ref, g_ref, m_ref, *rest):
        if gate is None:
            dx_ref, dsh_ref, dsc_ref, dg_ref = rest
        else:
            y_ref, gm_ref, dx_ref, dsh_ref, dsc_ref, dg_ref, dy_ref, dgate_ref = rest
        i = pl.program_id(0)
        dhv = _dot(pick(a_ref, 0), w_ref[0], "nt")
        for j in range(1, N_SHARD):
            dhv = dhv + _dot(pick(a_ref, j), w_ref[j], "nt")
        xv = x_ref[...]
        r = lax.rsqrt(jnp.mean(xv * xv, axis=-1, keepdims=True) + EPS)
        xh = xv * r
        g = g_ref[...]
        sc1 = 1.0 + m_ref[isc:isc + 1, :]
        t = dhv * xh
        first = cfg.first_of_row(i)
        _accumulate_rows(first, dsh_ref, _colsum(dhv))
        _accumulate_rows(first, dsc_ref, _colsum(t * g))
        _accumulate_rows(i == 0, dg_ref, _colsum(t * sc1))
        dxh = dhv * (g * sc1)
        dxn = r * (dxh - xh * jnp.mean(dxh * xh, axis=-1, keepdims=True))
        dxv = (jnp.where(i < nlb, dres_ref[...], 0.0) if res_latent_only else dres_ref[...]) + dxn
        if dx_latent_only:
            @pl.when(i < nlb)
            def _():
                dx_ref[...] = dxv
        else:
            dx_ref[...] = dxv
        if gate is not None:
            _gate_step(i, dxv, y_ref, gm_ref, gate[2], dy_ref, dgate_ref, cfg)

    row = pl.BlockSpec((TM, D), lambda i: (i, 0))
    vec = pl.BlockSpec((1, D), lambda i: (0, 0))
    part = pl.BlockSpec((None, 1, D), lambda i: (cfg.mod_row(i), 0, 0))
    part_shape = jax.ShapeDtypeStruct((MOD_ROWS, 1, D), F32)
    resident = pl.BlockSpec((None, N_SHARD, D, ns), lambda i: (0, 0, 0, 0), pipeline_mode=pl.Buffered(1))
    ins, in_specs = [a, w, x, dres, gvec, mod3], [a_spec, resident, row, row, vec, _mod_spec(cfg)]
    out_specs = [row, part, part, vec]
    out_shape = [jax.ShapeDtypeStruct((cfg.r, D), F32), part_shape, part_shape, jax.ShapeDtypeStruct((1, D), F32)]
    if dx_latent_only:
        out_specs[0] = pl.BlockSpec((TM, D), lambda i: (jnp.minimum(i, nlb - 1), 0))
        out_shape[0] = jax.ShapeDtypeStruct((cfg.nl, D), F32)
    if gate is not None:
        ins, in_specs = ins + list(gate[:2]), in_specs + [row, _mod_spec(cfg)]
        out_specs, out_shape = out_specs + [row, part], out_shape + [jax.ShapeDtypeStruct((cfg.r, D), BF16), part_shape]
    return pl.pallas_call(body, name=name, grid=(nblk,), in_specs=in_specs, out_specs=out_specs, out_shape=out_shape,
                          compiler_params=_params(1))(*ins)


def _mod_spec(cfg):
    return pl.BlockSpec((None, 6, D), lambda i: (cfg.mod_row(i), 0, 0))


def _norm_mod(xv, g, m_ref, ish, isc):
    r = lax.rsqrt(jnp.mean(xv * xv, axis=-1, keepdims=True) + EPS)
    return (xv * r * g * (1.0 + m_ref[isc:isc + 1, :]) + m_ref[ish:ish + 1, :]).astype(BF16)


def _accumulate_rows(first, ref, val):
    @pl.when(first)
    def _():
        ref[...] = val

    @pl.when(jnp.logical_not(first))
    def _():
        ref[...] += val


def _loss_head(x, target, gvec, y, mod3, ig, cfg):
    def body(x_ref, t_ref, g_ref, y_ref, m_ref, dx_ref, loss_ref, dg_ref, dy_ref, dgate_ref):
        i = pl.program_id(0)

        @pl.when(i == 0)
        def _():
            loss_ref[...] = jnp.zeros_like(loss_ref)
            dg_ref[...] = jnp.zeros_like(dg_ref)

        xv = x_ref[...]
        g = g_ref[...]
        r = lax.rsqrt(jnp.mean(xv * xv, axis=-1, keepdims=True) + EPS)
        xh = xv * r
        err = xh * g - t_ref[...]
        loss_ref[...] += (0.5 / D) * _colsum(jnp.sum(err * err, axis=-1, keepdims=True))
        dy = err * (1.0 / D)
        dg_ref[...] += _colsum(dy * xh)
        dxh = dy * g
        dxv = r * (dxh - xh * jnp.mean(dxh * xh, axis=-1, keepdims=True))
        dx_ref[...] = dxv
        _gate_step(i, dxv, y_ref, m_ref, ig, dy_ref, dgate_ref, cfg)

    row = pl.BlockSpec((TM, D), lambda i: (i, 0))
    vec = pl.BlockSpec((1, D), lambda i: (0, 0))
    part = pl.BlockSpec((None, 1, D), lambda i: (cfg.mod_row(i), 0, 0))
    return pl.pallas_call(
        body, name="loss_head", grid=(cfg.nl // TM,), in_specs=[row, row, vec, row, _mod_spec(cfg)],
        out_specs=[row, pl.BlockSpec((1, 1), lambda i: (0, 0)), vec, row, part],
        out_shape=[jax.ShapeDtypeStruct((cfg.r, D), F32), jax.ShapeDtypeStruct((1, 1), F32),
                   jax.ShapeDtypeStruct((1, D), F32), jax.ShapeDtypeStruct((cfg.r, D), BF16),
                   jax.ShapeDtypeStruct((MOD_ROWS, 1, D), F32)],
        compiler_params=_params(1))(x, target, gvec, y, mod3)


def _rope_tables(seq):
    rows = seq // GRID_W
    row = jnp.repeat(jnp.arange(rows), GRID_W).astype(F32)
    col = jnp.tile(jnp.arange(GRID_W), rows).astype(F32)
    half = HEAD_DIM // 2
    inv = ROPE_BASE ** (-jnp.arange(0, half, 2, dtype=F32) / half)
    ar, ac = row[:, None] * inv, col[:, None] * inv
    ang = jnp.concatenate([ar, ar, ac, ac], axis=-1)
    sign = jnp.tile(jnp.concatenate([-jnp.ones((16,), F32), jnp.ones((16,), F32)]), 2)
    cos = jnp.tile(jnp.cos(ang), (1, 2))
    sin = jnp.tile(jnp.sin(ang) * sign, (1, 2))
    cos = jnp.concatenate([cos, jnp.ones((TM, 2 * HEAD_DIM), F32)], axis=0)
    sin = jnp.concatenate([sin, jnp.zeros((TM, 2 * HEAD_DIM), F32)], axis=0)
    return cos, sin


def _rope(x, cos, sin_signed, sign):
    lane = lax.broadcasted_iota(jnp.int32, x.shape, 1)
    low = (lane % 32) < 16
    rot = jnp.where(low, pltpu.roll(x, 112, 1), pltpu.roll(x, 16, 1))
    return x * cos + sign * (rot * sin_signed)


def _in_proj(name, x, gvec, mod3, w_in, layer, cos_t, sin_t, cfg):
    nlb, bps = cfg.nl // TM, cfg.seq // TM

    def body(x_ref, g_ref, m_ref, w_ref, cos_ref, sin_ref, h_ref, qkv_ref, cp_ref):
        hv = _norm_mod(x_ref[...], g_ref[...], m_ref, 0, 1)
        h_ref[...] = hv
        u = jnp.concatenate([_dot(hv, w_ref[j]) for j in range(N_SHARD)], axis=1)
        cos, sin = cos_ref[...], sin_ref[...]
        tiles = []
        for t in range(5):
            y = _rope(u[:, 128 * t:128 * (t + 1)], cos, sin, 1.0)
            tiles.append(y * (HEAD_DIM ** -0.5) if t < 4 else y)
        tiles.append(u[:, 640:768])
        qkv_ref[...] = jnp.concatenate(tiles, axis=1).astype(BF16)
        cp_ref[...] = u[:, 768:IN_W].astype(BF16)

    tab = pl.BlockSpec((TM, 128), lambda i: (jnp.where(i < nlb, i % bps, bps), 0))
    half = pl.BlockSpec((TM, 768), lambda i: (i, 0))
    row = pl.BlockSpec((TM, D), lambda i: (i, 0))
    return pl.pallas_call(
        body, name=name, grid=(cfg.r // TM,),
        in_specs=[row, pl.BlockSpec((1, D), lambda i: (0, 0)), _mod_spec(cfg),
                  pl.BlockSpec((None, N_SHARD, D, IN_SHARD), lambda i: (layer, 0, 0, 0)), tab, tab],
        out_specs=[row, half, half],
        out_shape=[jax.ShapeDtypeStruct((cfg.r, D), BF16), jax.ShapeDtypeStruct((cfg.r, 768), BF16),
                   jax.ShapeDtypeStruct((cfg.r, 768), BF16)],
        compiler_params=_params(1))(x, gvec, mod3, w_in, cos_t, sin_t)


def _att_specs(cfg):
    nlb, ncb = cfg.seq // QB, cfg.ctx // QB

    def qblk(s, qb):
        return jnp.where(qb < nlb, s * nlb + qb, cfg.nl // QB + s * ncb + qb - nlb)

    def near(off, col):
        return pl.BlockSpec((QB, 128), lambda s, qb: (s * nlb + jnp.clip(qb + off, 0, nlb - 1), col))

    def ctxs(col):
        return pl.BlockSpec((cfg.ctx, 128), lambda s, qb: (cfg.nl // cfg.ctx + s, col))

    qspec = pl.BlockSpec((QB, ATTN_W), lambda s, qb: (qblk(s, qb), 0))
    kv = [ctxs(4), ctxs(5), near(-1, 4), near(0, 4), near(1, 4), near(-1, 5), near(0, 5), near(1, 5)]
    return qblk, qspec, kv


def _att_scores(qb, nlb, sink_ref, q_ref, k_refs, v_refs, kh):
    is_lat = qb < nlb
    ii = lax.broadcasted_iota(jnp.int32, (4 * QB, 3 * QB), 0) % QB
    col = lax.broadcasted_iota(jnp.int32, (4 * QB, 3 * QB), 1)
    jj, blk = col % QB, col // QB
    off_p = jnp.where(jnp.logical_and(is_lat, qb >= 1), 0.0, NEG)
    off_c = jnp.where(is_lat, 0.0, NEG)
    off_n = jnp.where(jnp.logical_and(is_lat, qb <= nlb - 2), 0.0, NEG)
    inside = jnp.logical_or(blk == 1, jnp.logical_or(jnp.logical_and(blk == 0, jj >= ii),
                                                     jnp.logical_and(blk == 2, jj <= ii)))
    off = jnp.where(blk == 0, off_p, jnp.where(blk == 1, off_c, off_n))
    q4 = jnp.concatenate([q_ref[:, (4 * kh + g) * HEAD_DIM:(4 * kh + g + 1) * HEAD_DIM] for g in range(4)], axis=0)
    rg = lax.broadcasted_iota(jnp.int32, (4 * QB, 1), 0) // QB
    snk = jnp.where(rg == 0, sink_ref[4 * kh],
                    jnp.where(rg == 1, sink_ref[4 * kh + 1], jnp.where(rg == 2, sink_ref[4 * kh + 2], sink_ref[4 * kh + 3])))
    lanes = slice(kh * HEAD_DIM, (kh + 1) * HEAD_DIM)
    kx, vx = k_refs[0][:, lanes], v_refs[0][:, lanes]
    kl = jnp.concatenate([r[:, lanes] for r in k_refs[1:]], axis=0)
    vl = jnp.concatenate([r[:, lanes] for r in v_refs[1:]], axis=0)
    sx = _dot(q4, kx, "nt")
    sl = jnp.where(inside, _dot(q4, kl, "nt"), NEG) + off
    return q4, snk, (kx, kl), (vx, vl), (sx, sl)


def _att_fwd(name, qkv, sink, ctx_queries, cfg, ride=None):
    nlb, ncb = cfg.seq // QB, cfg.ctx // QB
    qblk, qspec, kvspecs = _att_specs(cfg)

    def body(sink_ref, q_ref, kx_ref, vx_ref, kp_ref, kc_ref, kn_ref, vp_ref, vc_ref, vn_ref, o_ref, lse_ref):
        qb = pl.program_id(1)
        for kh in range(2):
            q4, snk, _, vs, ss = _att_scores(qb, nlb, sink_ref, q_ref, (kx_ref, kp_ref, kc_ref, kn_ref),
                                             (vx_ref, vp_ref, vc_ref, vn_ref), kh)
            m = snk
            for s_ in ss:
                m = jnp.maximum(m, jnp.max(s_, axis=-1, keepdims=True))
            den = jnp.exp(snk - m)
            o4 = jnp.zeros((4 * QB, HEAD_DIM), F32)
            for s_, v_ in zip(ss, vs):
                p = jnp.exp(s_ - m)
                den = den + jnp.sum(p, axis=-1, keepdims=True)
                o4 = o4 + _dot(p, v_)
            o4 = o4 / den
            lse = m + jnp.log(den)
            for g in range(4):
                h = 4 * kh + g
                o_ref[:, h * HEAD_DIM:(h + 1) * HEAD_DIM] = o4[g * QB:(g + 1) * QB].astype(BF16)
                lse_ref[:, h:h + 1] = lse[g * QB:(g + 1) * QB]

    return _hosted_call(
        body, ride, name, (cfg.b, nlb + (ncb if ctx_queries else 0)), [sink] + [qkv] * 9,
        [pl.BlockSpec(memory_space=pltpu.SMEM), qspec, *kvspecs],
        [jax.ShapeDtypeStruct((cfg.r, D), BF16), jax.ShapeDtypeStruct((cfg.r, N_HEADS), F32)],
        [pl.BlockSpec((QB, ATTN_W), lambda s, qb: (qblk(s, qb), 0)),
         pl.BlockSpec((QB, N_HEADS), lambda s, qb: (qblk(s, qb), 0))], [], _params(2))


def _att_bwd(name, qkv, mix, dmix, lse, sink, cos_t, sin_t, ctx_queries, cfg, ride=None):
    nlb, ncb = cfg.seq // QB, cfg.ctx // QB
    nqb = nlb + (ncb if ctx_queries else 0)
    qblk, qspec, kvspecs = _att_specs(cfg)

    def body(sink_ref, q_ref, kx_ref, vx_ref, kp_ref, kc_ref, kn_ref, vp_ref, vc_ref, vn_ref, o_ref, do_ref,
             lse_ref, cosq_ref, sinq_ref, cosk_ref, sink_tab_ref, dq_ref, dkvl_ref, dkvc_ref, dsink_ref,
             accl, accc, dqs):
        s_id, qb = pl.program_id(0), pl.program_id(1)

        @pl.when(qb == 0)
        def _():
            accl[...] = jnp.zeros_like(accl)
            accc[...] = jnp.zeros_like(accc)

        @pl.when(jnp.logical_and(s_id == 0, qb == 0))
        def _():
            dsink_ref[...] = jnp.zeros_like(dsink_ref)

        starts = [pl.multiple_of(jnp.clip(qb + off, 0, nlb - 1) * QB, QB) for off in (-1, 0, 1)]
        for kh in range(2):
            q4, snk, ks, vs, ss = _att_scores(qb, nlb, sink_ref, q_ref, (kx_ref, kp_ref, kc_ref, kn_ref),
                                              (vx_ref, vp_ref, vc_ref, vn_ref), kh)
            lanes = slice(kh * HEAD_DIM, (kh + 1) * HEAD_DIM)
            heads =[slice((4 * kh + g) * HEAD_DIM, (4 * kh + g + 1) * HEAD_DIM) for g in range(4)]
            do4 = jnp.concatenate([do_ref[:, hs] for hs in heads], axis=0)
            o4 = jnp.concatenate([o_ref[:, hs] for hs in heads], axis=0).astype(F32)
            lse4 = jnp.concatenate([lse_ref[:, 4 * kh + g:4 * kh + g + 1] for g in range(4)], axis=0)
            delta = jnp.sum(do4 * o4, axis=-1, keepdims=True)
            dq4 = jnp.zeros((4 * QB, HEAD_DIM), F32)
            dks, dvs = [], []
            for s_, k_, v_ in zip(ss, ks, vs):
                p = jnp.exp(s_ - lse4)
                ds = p * (_dot(do4, v_, "nt") - delta)
                dq4 = dq4 + _dot(ds, k_)
                dks.append(_dot(ds, q4, "tn"))
                dvs.append(_dot(p, do4, "tn"))
            accc[:, lanes] += dks[0]
            accc[:, 128 + kh * HEAD_DIM:128 + (kh + 1) * HEAD_DIM] += dvs[0]
            for t, st in enumerate(starts):
                accl[pl.ds(st, QB), lanes] += dks[1][t * QB:(t + 1) * QB]
                accl[pl.ds(st, QB), 128 + kh * HEAD_DIM:128 + (kh + 1) * HEAD_DIM] += dvs[1][t * QB:(t + 1) * QB]
            dsk = -jnp.exp(snk - lse4) * delta
            for g in range(4):
                h = 4 * kh + g
                dsink_ref[h:h + 1, :] += jnp.broadcast_to(_colsum(dsk[g * QB:(g + 1) * QB]), (1, 128))
                dqs[:, heads[g]] = dq4[g * QB:(g + 1) * QB]
        cos, sin = cosq_ref[...], sinq_ref[...]
        dq_ref[...] = jnp.concatenate(
            [_rope(dqs[:, 128 * t:128 * (t + 1)], cos, sin, -1.0) * (HEAD_DIM ** -0.5) for t in range(4)],
            axis=1).astype(BF16)

        @pl.when(qb == nqb - 1)
        def _():
            dk = _rope(accl[:, 0:128], cosk_ref[...], sink_tab_ref[...], -1.0)
            dkvl_ref[...] = jnp.concatenate([dk, accl[:, 128:256]], axis=1).astype(BF16)
            dkvc_ref[...] = accc[...].astype(BF16)

    rowq = lambda w: pl.BlockSpec((QB, w), lambda s, qb: (qblk(s, qb), 0))
    tabq = pl.BlockSpec((QB, 128), lambda s, qb: (jnp.where(qb < nlb, qb, cfg.seq // QB), 0))
    tabk = pl.BlockSpec((cfg.seq, 128), lambda s, qb: (0, 0))
    return _hosted_call(
        body, ride, name, (cfg.b, nqb), [sink] + [qkv] * 9 + [mix, dmix, lse, cos_t, sin_t, cos_t, sin_t],
        [pl.BlockSpec(memory_space=pltpu.SMEM), qspec, *kvspecs, rowq(ATTN_W), rowq(ATTN_W), rowq(N_HEADS),
         tabq, tabq, tabk, tabk],
        [jax.ShapeDtypeStruct((cfg.r, IN_W), BF16), jax.ShapeDtypeStruct((cfg.nl, 256), BF16),
         jax.ShapeDtypeStruct((cfg.nc, 256), BF16), jax.ShapeDtypeStruct((N_HEADS, 128), F32)],
        [rowq(ATTN_W), pl.BlockSpec((cfg.seq, 256), lambda s, qb: (s, 0)),
         pl.BlockSpec((cfg.ctx, 256), lambda s, qb: (s, 0)), pl.BlockSpec((N_HEADS, 128), lambda s, qb: (0, 0))],
        [pltpu.VMEM((cfg.seq, 256), F32), pltpu.VMEM((cfg.ctx, 256), F32), pltpu.VMEM((QB, ATTN_W), F32)], _params(2))


def _pool_geometry(n, c):
    lane = lax.broadcasted_iota(jnp.int32, (1, POOL_W), 1) // HEAD_DIM
    wl = jnp.where(lane == 0, 1, jnp.where(lane == 1, 2, jnp.where(lane == 2, 4, 8)))
    wr = wl - 1
    t = c * CHUNK + lax.broadcasted_iota(jnp.int32, (CHUNK, POOL_W), 0)
    cnt = (jnp.minimum(t + wr, n - 1) - jnp.maximum(t - wl, 0) + 1).astype(F32)
    return wl, wr, cnt


def _build_phases(src, ph, c):
    for s in range(1, 8):
        ph[s - 1] = src[c * CHUNK + s:c * CHUNK + s + CHUNK + 24, :]


def _window(src, ph, c, off):
    a, s = divmod(off, 8)
    if s == 0:
        return src[c * CHUNK + 8 * a:c * CHUNK + 8 * a + CHUNK, :]
    return ph[s - 1, 8 * a:8 * a + CHUNK, :]


def _conv_chunk(hp, ph, dw_ref, dwb_ref, c):
    _build_phases(hp, ph, c)
    acc = jnp.zeros((CHUNK, CONV_W), F32) + dwb_ref[...]
    for j in range(CONV_K):
        acc = acc + dw_ref[j:j + 1, :] * _window(hp, ph, c, j + 1)
    return acc


def _fill_glu(cp_ref, hp, n):
    hp[0:PAD, :] = jnp.zeros((PAD, CONV_W), F32)
    hp[PAD + n:2 * PAD + n, :] = jnp.zeros((PAD, CONV_W), F32)
    for c in range(n // CHUNK):
        rows = slice(c * CHUNK, (c + 1) * CHUNK)
        a = cp_ref[rows, 0:CONV_W].astype(F32)
        g = cp_ref[rows, CONV_W:2 * CONV_W].astype(F32)
        hp[PAD + c * CHUNK:PAD + (c + 1) * CHUNK, :] = a * _sigmoid(g)


def _fill_pool(cp_ref, pp, n):
    pp[0:PAD, :] = jnp.zeros((PAD, POOL_W), F32)
    pp[PAD + n:2 * PAD + n, :] = jnp.zeros((PAD, POOL_W), F32)
    for c in range(n // CHUNK):
        pp[PAD + c * CHUNK:PAD + (c + 1) * CHUNK, :] = cp_ref[c * CHUNK:(c + 1) * CHUNK, 2 * CONV_W:768].astype(F32)


LV = CHUNK + 2 * PAD
_LEVELS = pltpu.VMEM((3, LV + 16, POOL_W), F32)


def _clear_level_edges(lv):
    for b in range(3):
        lv[b, 0:8] = jnp.zeros((8, POOL_W), F32)
        lv[b, 8 + LV:16 + LV] = jnp.zeros((8, POOL_W), F32)


def _window_sums(src, lv, c, lead):
    lv[0, 8:8 + LV] = src[c * CHUNK:c * CHUNK + LV, :]
    lo = 7 if lead < 0 else 8
    lv[1, 8:8 + LV] = lv[0, lo:lo + LV] + lv[0, lo + 1:lo + 1 + LV]
    group = lax.broadcasted_iota(jnp.int32, (1, POOL_W), 1) // HEAD_DIM
    rows = slice(8 + PAD, 8 + PAD + CHUNK)
    res = lv[1, rows]
    cur = 1
    for g, s in ((1, 1), (2, 2), (3, 4)):
        nxt = 3 - cur
        lv[nxt, 8:8 + LV] = lv[cur, 8 - s:8 - s + LV] + lv[cur, 8 + s:8 + s + LV]
        res = jnp.where(group >= g, lv[nxt, rows], res)
        cur = nxt
    return res


def _pool_chunk(pp, lv, n, c):
    _, _, cnt = _pool_geometry(n, c)
    return _window_sums(pp, lv, c, -1) / cnt - pp[PAD + c * CHUNK:PAD + (c + 1) * CHUNK, :], cnt


def _seq_specs(n, blk_off, width, col=0):
    return pl.BlockSpec((n, width), lambda s: (blk_off + s, col))


def _full(shape):
    return pl.BlockSpec(shape, lambda s: (0,) * len(shape))


_PHASES = pltpu.VMEM((7, CHUNK + 24, CONV_W), F32)


def _convpool_fwd(name, cpin, mix, yconv, prm, n, blk_off, cfg):
    dw, dwb, lng, lnb, wbd, ps = prm
    n_alias = 1 if yconv is None else 2

    def body(*refs):
        cp_ref, dw_ref, dwb_ref, lng_ref, lnb_ref, wbd_ref, ps_ref = refs[:7]
        out_ref, y_ref, hp, pp, ph, lv = refs[7 + n_alias:]
        _fill_glu(cp_ref, hp, n)
        _fill_pool(cp_ref, pp, n)
        _clear_level_edges(lv)
        for c in range(n // CHUNK):
            rows = slice(c * CHUNK, (c + 1) * CHUNK)
            y = _conv_chunk(hp, ph, dw_ref, dwb_ref, c)
            y_ref[rows, :] = y
            d = y - jnp.mean(y, axis=-1, keepdims=True)
            hn = d * lax.rsqrt(jnp.mean(d * d, axis=-1, keepdims=True) + EPS) * lng_ref[...] + lnb_ref[...]
            out_ref[rows, 0:CONV_W] = (hn * _sigmoid(hn)).astype(BF16)
            yp, _ = _pool_chunk(pp, lv, n, c)
            out_ref[rows, CONV_W:2 * CONV_W] = (_dot(yp, wbd_ref[...]) * ps_ref[...]).astype(BF16)

    through = [mix] if yconv is None else [mix, yconv]
    return pl.pallas_call(
        body, name=name, grid=(cfg.b,),
        in_specs=[_seq_specs(n, blk_off, 768), _full((32, CONV_W)), _full((1, CONV_W)), _full((1, CONV_W)),
                  _full((1, CONV_W)), _full((POOL_W, POOL_W)), _full((1, POOL_W))] + [_ANY] * n_alias,
        out_specs=[_seq_specs(n, blk_off, 512, 1), _seq_specs(n, blk_off, CONV_W)],
        out_shape=[jax.ShapeDtypeStruct((cfg.r, D), BF16), jax.ShapeDtypeStruct((cfg.r, CONV_W), F32)],
        scratch_shapes=[pltpu.VMEM((n + 2 * PAD, CONV_W), F32), pltpu.VMEM((n + 2 * PAD, POOL_W), F32), _PHASES, _LEVELS],
        input_output_aliases={7 + i: i for i in range(n_alias)},
        compiler_params=_params(1))(cpin, dw, dwb, lng, lnb, wbd, ps, *through)


_SMALL_SHAPES = [(32, CONV_W), (1, CONV_W), (1, CONV_W), (1, CONV_W), (POOL_W, POOL_W), (1, POOL_W)]


def _convpool_bwd(name, cpin, yconv, dmix, prm, acc_in, n, blk_off, cfg):
    dw, dwb, lng, lnb, wbd, ps = prm
    nch = n // CHUNK

    def body(cp_ref, y_ref, dm_ref, dw_ref, dwb_ref, lng_ref, lnb_ref, wbd_ref, ps_ref, dcp_in,
             a_dw, a_dwb, a_lng, a_lnb, a_wbd, a_ps,
             dcp_ref, o_dw, o_dwb, o_lng, o_lnb, o_wbd, o_ps, hp, dyp, pp, wp, dyv, dwacc, ph, lv):
        s = pl.program_id(0)
        _clear_level_edges(lv)

        @pl.when(s == 0)
        def _():
            for o_, a_ in ((o_dw, a_dw), (o_dwb, a_dwb), (o_lng, a_lng), (o_lnb, a_lnb), (o_wbd, a_wbd), (o_ps, a_ps)):
                o_[...] = a_[...]
            dwacc[...] = jnp.zeros_like(dwacc)

        _fill_glu(cp_ref, hp, n)
        _fill_pool(cp_ref, pp, n)
        for ref in (dyp, wp):
            ref[0:PAD, :] = jnp.zeros((PAD, CONV_W), F32)
            ref[PAD + n:2 * PAD + n, :] = jnp.zeros((PAD, CONV_W), F32)
        for c in range(nch):
            rows = slice(c * CHUNK, (c + 1) * CHUNK)
            y = y_ref[rows, :]
            d = y - jnp.mean(y, axis=-1, keepdims=True)
            rstd = lax.rsqrt(jnp.mean(d * d, axis=-1, keepdims=True) + EPS)
            xh = d * rstd
            hn = xh * lng_ref[...] + lnb_ref[...]
            sg = _sigmoid(hn)
            dhn = dm_ref[rows, 0:CONV_W] * (sg * (1.0 + hn * (1.0 - sg)))
            o_lnb[...] += _colsum(dhn)
            o_lng[...] += _colsum(dhn * xh)
            dxh = dhn * lng_ref[...]
            dy = rstd * (dxh - jnp.mean(dxh, axis=-1, keepdims=True) - xh * jnp.mean(dxh * xh, axis=-1, keepdims=True))
            o_dwb[...] += _colsum(dy)
            dyp[PAD + c * CHUNK:PAD + (c + 1) * CHUNK, :] = dy
            _build_phases(hp, ph, c)
            for j in range(CONV_K):
                prod = dy * _window(hp, ph, c, j + 1)
                dwacc[8 * j:8 * j + 8, :] += jnp.sum(prod.reshape(CHUNK // 8, 8, CONV_W), axis=0)
            yp, cnt = _pool_chunk(pp, lv, n, c)
            dz = dm_ref[rows, CONV_W:2 * CONV_W]
            o_ps[...] += _colsum(dz * _dot(yp, wbd_ref[...]))
            dzs = dz * ps_ref[...]
            o_wbd[...] += _dot(yp, dzs, "tn")
            dv = _dot(dzs, wbd_ref[...], "nt")
            dyv[rows, :] = dv
            wp[PAD + c * CHUNK:PAD + (c + 1) * CHUNK, :] = dv / cnt
        for c in range(nch):
            rows = slice(c * CHUNK, (c + 1) * CHUNK)
            _build_phases(dyp, ph, c)
            dh = jnp.zeros((CHUNK, CONV_W), F32)
            for j in range(CONV_K):
                dh = dh + dw_ref[j:j + 1, :] * _window(dyp, ph, c, 31 - j)
            a = cp_ref[rows, 0:CONV_W].astype(F32)
            sg = _sigmoid(cp_ref[rows, CONV_W:2 * CONV_W].astype(F32))
            dcp_ref[rows, 0:CONV_W] = (dh * sg).astype(BF16)
            dcp_ref[rows, CONV_W:2 * CONV_W] = (dh * a * sg * (1.0 - sg)).astype(BF16)
            dcp_ref[rows, 2 * CONV_W:768] = (_window_sums(wp, lv, c, 1) - dyv[rows, :]).astype(BF16)

        @pl.when(s == cfg.b - 1)
        def _():
            for j in range(CONV_K):
                o_dw[j:j + 1, :] += _colsum(dwacc[8 * j:8 * j + 8, :])

    small_specs = [_full(sh) for sh in _SMALL_SHAPES]
    return pl.pallas_call(
        body, name=name, grid=(cfg.b,),
        in_specs=[_seq_specs(n, blk_off, 768), _seq_specs(n, blk_off, CONV_W), _seq_specs(n, blk_off, 512, 1),
                  *small_specs, _ANY, *small_specs],
        out_specs=[_seq_specs(n, blk_off, 768, 1), *small_specs],
        out_shape=[jax.ShapeDtypeStruct((cfg.r, IN_W), BF16)] + [jax.ShapeDtypeStruct(sh, F32) for sh in _SMALL_SHAPES],
        scratch_shapes=[pltpu.VMEM((n + 2 * PAD, CONV_W), F32), pltpu.VMEM((n + 2 * PAD, CONV_W), F32),
                        pltpu.VMEM((n + 2 * PAD, POOL_W), F32), pltpu.VMEM((n + 2 * PAD, POOL_W), F32),
                        pltpu.VMEM((n, POOL_W), F32), pltpu.VMEM((8 * 32, CONV_W), F32), _PHASES, _LEVELS],
        input_output_aliases={9: 0}, compiler_params=_params(1))(cpin, yconv, dmix, dw, dwb, lng, lnb, wbd, ps, *acc_in)


def _place_kv(name, du, dkvl, dkvc, with_ctx, cfg):
    nlb = cfg.nl // TM

    def body(l_ref, c_ref, du_in, o_ref):
        i = pl.program_id(0)
        o_ref[...] = jnp.where(i < nlb, l_ref[...], c_ref[...])

    return pl.pallas_call(
        body, name=name, grid=(cfg.r // TM if with_ctx else nlb,),
        in_specs=[pl.BlockSpec((TM, 256), lambda i: (jnp.minimum(i, nlb - 1), 0)),
                  pl.BlockSpec((TM, 256), lambda i: (jnp.maximum(i - nlb, 0), 0)), _ANY],
        out_specs=pl.BlockSpec((TM, 256), lambda i: (i, 2)), out_shape=jax.ShapeDtypeStruct((cfg.r, IN_W), BF16),
        input_output_aliases={2: 0}, compiler_params=_params(1))(dkvl, dkvc, du)


def _place_ctx_kv_only(name, du, dkvc, cfg):
    nlb = cfg.nl // TM

    def body(c_ref, du_in, o_ref):
        o_ref[...] = jnp.zeros_like(o_ref)
        o_ref[:, ATTN_W:ATTN_W + 256] = c_ref[...]

    return pl.pallas_call(
        body, name=name, grid=(cfg.nc // TM,), in_specs=[pl.BlockSpec((TM, 256), lambda i: (i, 0)), _ANY],
        out_specs=pl.BlockSpec((TM, IN_W), lambda i: (nlb + i, 0)), out_shape=jax.ShapeDtypeStruct((cfg.r, IN_W), BF16),
        input_output_aliases={1: 0}, compiler_params=_params(1))(dkvc, du)


def _norm_fwd(name, x, gvec, mod3, ish, isc, nblk, cfg):
    def body(x_ref, g_ref, m_ref, o_ref):
        o_ref[...] = _norm_mod(x_ref[...], g_ref[...], m_ref, ish, isc)

    row = pl.BlockSpec((TM, D), lambda i: (i, 0))
    return pl.pallas_call(
        body, name=name, grid=(nblk,),
        in_specs=[row, pl.BlockSpec((1, D), lambda i: (0, 0)), _mod_spec(cfg)], out_specs=row,
        out_shape=jax.ShapeDtypeStruct((cfg.r, D), BF16), compiler_params=_params(1))(x, gvec, mod3)


def _ffn_in(name, h, w_ffn_in, layer, nblk, cfg, ride=None):
    def body(h_ref, wg_ref, wu_ref, fac_ref, act_ref):
        hv = h_ref[...]
        g = _dot(hv, wg_ref[...])
        u = _dot(hv, wu_ref[...])
        s = _sigmoid(g)
        gs = g * s
        fac_ref[0] = ((s + gs * (1.0 - s)) * u).astype(BF16)
        fac_ref[1] = gs.astype(BF16)
        act_ref[...] = (gs * u).astype(BF16)

    wspec = lambda base: pl.BlockSpec((None, None, D, HALF_FF), lambda j, i: (layer, base + j, 0, 0))
    return _hosted_call(
        body, ride, name, (2, nblk), [h, w_ffn_in, w_ffn_in],
        [pl.BlockSpec((TM, D), lambda j, i: (i, 0)), wspec(0), wspec(2)],
        [jax.ShapeDtypeStruct((2, cfg.r, D_FF), BF16), jax.ShapeDtypeStruct((cfg.r, D_FF), BF16)],
        [pl.BlockSpec((2, TM, HALF_FF), lambda j, i: (0, i, j)), pl.BlockSpec((TM, HALF_FF), lambda j, i: (i, j))],
        [], _params(2))


def _row_block(rows, cols, max_bytes=1 << 20):
    best = 16
    for t in range(16, rows + 1, 16):
        if rows % t == 0 and t * cols * 4 <= max_bytes:
            best = t
    assert rows % best == 0
    return best


def _pair_add(name, own32, recv, c_idx):
    _, _, s0, s1 = own32.shape
    tr = _row_block(s0, s1)

    def body(c_ref, a_ref, b_ref, o_ref):
        o_ref[...] = (a_ref[...].astype(F32) + b_ref[...].astype(F32)).astype(BF16)

    grid_spec = pltpu.PrefetchScalarGridSpec(
        num_scalar_prefetch=1, grid=(N_SHARD * s0 // tr,),
        in_specs=[pl.BlockSpec((None, tr, s1), lambda i, c: (c[0], i, 0)), pl.BlockSpec((tr, s1), lambda i, c: (i, 0))],
        out_specs=pl.BlockSpec((tr, s1), lambda i, c: (i, 0)))
    out = pl.pallas_call(body, name=name, grid_spec=grid_spec, out_shape=jax.ShapeDtypeStruct((N_SHARD * s0, s1), BF16),
                         compiler_params=_params(1))(c_idx, own32.reshape(2, N_SHARD * s0, s1), recv.reshape(N_SHARD * s0, s1))
    return out.reshape(N_SHARD, s0, s1)


def _shard_sum(name, pair_sum, recv, jc_idx):
    _, s0, s1 = pair_sum.shape
    tr = _row_block(s0, s1)

    def body(jc_ref, a_ref, b_ref, o_ref):
        o_ref[...] = ((a_ref[...].astype(F32) + b_ref[0].astype(F32)) + b_ref[1].astype(F32)) + b_ref[2].astype(F32)

    grid_spec = pltpu.PrefetchScalarGridSpec(
        num_scalar_prefetch=1, grid=(s0 // tr,),
        in_specs=[pl.BlockSpec((None, tr, s1), lambda i, jc: (jc[0], i, 0)), pl.BlockSpec((3, tr, s1), lambda i, jc: (0, i, 0))],
        out_specs=pl.BlockSpec((None, tr, s1), lambda i, jc: (jc[1], i, 0)))
    return pl.pallas_call(body, name=name, grid_spec=grid_spec, out_shape=jax.ShapeDtypeStruct((2, s0, s1), F32),
                          compiler_params=_params(1))(jc_idx, pair_sum, recv)


def _adamw_math(w, g, m, v):
    m = ADAM_B1 * m + (1.0 - ADAM_B1) * g
    v = ADAM_B2 * v + (1.0 - ADAM_B2) * (g * g)
    m_hat = m / (1.0 - ADAM_B1 ** ADAM_STEP)
    v_hat = v / (1.0 - ADAM_B2 ** ADAM_STEP)
    delta = -ADAM_LR * (m_hat / (jnp.sqrt(v_hat) + ADAM_EPS) + ADAM_WD * w)
    return delta, m, v


def _adamw(name, w, g, m, v):
    rows, cols = w.shape
    tr = rows if rows % 16 else _row_block(rows, cols, 1 << 19)

    def body(w_ref, g_ref, m_ref, v_ref, d_ref, mo_ref, vo_ref):
        d, mn, vn = _adamw_math(w_ref[...], g_ref[...], m_ref[...], v_ref[...])
        d_ref[...] = d
        mo_ref[...] = mn
        vo_ref[...] = vn

    spec = pl.BlockSpec((tr, cols), lambda i: (i, 0))
    shape = jax.ShapeDtypeStruct((rows, cols), F32)
    return pl.pallas_call(body, name=name, grid=(rows // tr,), in_specs=[spec] * 4, out_specs=[spec] * 3,
                          out_shape=[shape] * 3, compiler_params=_params(1))(w, g, m, v)


def _adamw_layers(name, w, g_layers, m, v, ride=None):
    rows, cols = w.shape
    s0 = rows // 2
    tr = _row_block(s0, cols, 1 << 19)
    nb = s0 // tr

    def body(w_ref, g0_ref, g1_ref, m_ref, v_ref, g_ref, d_ref, mo_ref, vo_ref):
        g = jnp.where(pl.program_id(0) < nb, g0_ref[...], g1_ref[...])
        d, mn, vn = _adamw_math(w_ref[...], g, m_ref[...], v_ref[...])
        g_ref[...] = g
        d_ref[...] = d
        mo_ref[...] = mn
        vo_ref[...] = vn

    spec = pl.BlockSpec((tr, cols), lambda i: (i, 0))
    shape = jax.ShapeDtypeStruct((rows, cols), F32)
    return _hosted_call(
        body, ride, name, (2 * nb,), [w, g_layers[0], g_layers[1], m, v],
        [spec, pl.BlockSpec((tr, cols), lambda i: (jnp.minimum(i, nb - 1), 0)),
         pl.BlockSpec((tr, cols), lambda i: (jnp.maximum(i - nb, 0), 0)), spec, spec],
        [shape] * 4, [spec] * 4, [], _params(1))


def _position():
    return lax.axis_index("x"), lax.axis_index("y"), lax.axis_index("c")


def _other_chips(x, y):
    return [(1 - x, y), (x, 1 - y), (1 - x, 1 - y)]


def _run_ride(name, ride):
    n_in, n_out = len(ride.ins), len(ride.out_shape)

    def body(*refs):
        parts = (refs[:n_in], refs[n_in:n_in + n_out], refs[n_in + n_out:])
        ride.start(*parts)
        ride.finish(*parts)

    return pl.pallas_call(
        body, name=name, in_specs=[_ANY] * n_in, out_specs=[_ANY] * n_out, out_shape=ride.out_shape,
        scratch_shapes=ride.scratch, compiler_params=pltpu.CompilerParams(vmem_limit_bytes=VMEM_LIMIT_V7X))(*ride.ins)


def _gather_ride(shards):
    n = len(shards)

    def copies(ins, outs, scr):
        ssem, rsem = scr[n], scr[n + 1]
        x, y, c = _position()
        me, sibling = 2 * x + y, (x, y, 1 - c)

        def remote(src, dst, i, dev):
            return pltpu.make_async_remote_copy(src, dst, ssem.at[i], rsem.at[i], device_id=dev, device_id_type=_MESH)

        fetch_out, fetch_in, pass_out, pass_in = [], [], [], []
        for a, (src, dst) in enumerate(zip(ins, outs)):
            for k, (px, py) in enumerate(_other_chips(x, y)):
                j, i1, i2 = 2 * px + py, 3 * a + k, 3 * n + 3 * a + k
                fetch_out.append(remote(src.at[c], dst.at[me, c], i1, (px, py, c)))
                fetch_in.append(remote(src.at[c], dst.at[j, c], i1, (px, py, c)))
                pass_out.append(remote(dst.at[j, c], dst.at[j, c], i2, sibling))
                pass_in.append(remote(dst.at[j, 1 - c], dst.at[j, 1 - c], i2, sibling))
        return me, fetch_out, fetch_in, pass_out, pass_in

    def start(ins, outs, scr):
        bufs, lsem = scr[:n], scr[n + 2]
        me, fetch_out, _, _, _ = copies(ins, outs, scr)
        for cp in fetch_out:
            cp.start()
        loads = []
        for a, (src, buf) in enumerate(zip(ins, bufs)):
            ld = pltpu.make_async_copy(src, buf, lsem.at[2 * a])
            ld.start()
            loads.append(ld)
        for a, (ld, buf, dst) in enumerate(zip(loads, bufs, outs)):
            ld.wait()
            st = pltpu.make_async_copy(buf, dst.at[me], lsem.at[2 * a + 1])
            st.start()
            st.wait()

    def finish(ins, outs, scr):
        _, fetch_out, fetch_in, pass_out, pass_in = copies(ins, outs, scr)
        for arrived, onward in zip(fetch_in, pass_out):
            arrived.wait_recv()
            onward.start()
        for cp in pass_in:
            cp.wait_recv()
        for cp in fetch_out + pass_out:
            cp.wait_send()

    return _Ride(list(shards), [jax.ShapeDtypeStruct((N_SHARD,) + s.shape, s.dtype) for s in shards],
                 [pltpu.VMEM(s.shape, s.dtype) for s in shards]
                 + [pltpu.SemaphoreType.DMA((6 * n,)), pltpu.SemaphoreType.DMA((6 * n,)), pltpu.SemaphoreType.DMA((2 * n,))],
                 start, finish)


def _comm(name, ins, out_shape, n_remote, plan):
    n_in, n_out = len(ins), len(out_shape)

    def body(*refs):
        plan(refs[:n_in], refs[n_in:n_in + n_out], *refs[n_in + n_out:])

    return pl.pallas_call(
        body, name=name, in_specs=[_ANY] * n_in, out_specs=[_ANY] * n_out, out_shape=out_shape,
        scratch_shapes=[pltpu.SemaphoreType.DMA((n_remote,)), pltpu.SemaphoreType.DMA((n_remote,))])(*ins)


def _send_other_half(name, grads_bf):
    n = len(grads_bf)

    def plan(ins, outs, ssem, rsem):
        x, y, c = _position()
        started = []
        for a, (src, dst) in enumerate(zip(ins, outs)):
            cp = pltpu.make_async_remote_copy(src.at[1 - c], dst, ssem.at[a], rsem.at[a], device_id=(x, y, 1 - c),
                                              device_id_type=_MESH)
            cp.start()
            started.append(cp)
        for cp in started:
            cp.wait_recv()
        for cp in started:
            cp.wait_send()

    shapes = [jax.ShapeDtypeStruct(s.shape[1:], s.dtype) for s in grads_bf]
    return _comm(name, grads_bf, shapes, n, plan)


def _exchange_ride(pair_sums):
    n = len(pair_sums)

    def copies(ins, outs, scr):
        ssem, rsem = scr
        x, y, c = _position()
        return [pltpu.make_async_remote_copy(src.at[2 * px + py], dst.at[k], ssem.at[3 * a + k], rsem.at[3 * a + k],
                                             device_id=(px, py, c), device_id_type=_MESH)
                for a, (src, dst) in enumerate(zip(ins, outs)) for k, (px, py) in enumerate(_other_chips(x, y))]

    def start(ins, outs, scr):
        for cp in copies(ins, outs, scr):
            cp.start()

    def finish(ins, outs, scr):
        for cp in copies(ins, outs, scr):
            cp.wait_recv()
        for cp in copies(ins, outs, scr):
            cp.wait_send()

    return _Ride(list(pair_sums), [jax.ShapeDtypeStruct((3,) + s.shape[1:], s.dtype) for s in pair_sums],
                 [pltpu.SemaphoreType.DMA((3 * n,)), pltpu.SemaphoreType.DMA((3 * n,))], start, finish)


def _swap_reduced(name, grads):
    n = len(grads)

    def body(*refs):
        ins, outs, ssem, rsem = refs[:n], refs[n:2 * n], refs[2 * n], refs[2 * n + 1]
        x, y, c = _position()
        sent = []
        for a, (src, dst) in enumerate(zip(ins, outs)):
            cp = pltpu.make_async_remote_copy(src.at[c], dst.at[c], ssem.at[a], rsem.at[a], device_id=(x, y, 1 - c),
                                              device_id_type=_MESH)
            cp.start()
            sent.append(cp)
        for a, (src, dst) in enumerate(zip(ins, outs)):
            pltpu.make_async_remote_copy(src.at[1 - c], dst.at[1 - c], ssem.at[a], rsem.at[a], device_id=(x, y, 1 - c),
                                         device_id_type=_MESH).wait_recv()
        for cp in sent:
            cp.wait_send()

    return pl.pallas_call(
        body, name=name, in_specs=[_ANY] * n, out_specs=[_ANY] * n,
        out_shape=[jax.ShapeDtypeStruct(g.shape, g.dtype) for g in grads],
        scratch_shapes=[pltpu.SemaphoreType.DMA((n,)), pltpu.SemaphoreType.DMA((n,))],
        input_output_aliases={a: a for a in range(n)})(*grads)


_FLIPS = [(dx, dy, dc) for dx in (0, 1) for dy in (0, 1) for dc in (0, 1) if dx + dy + dc]
_VMEM = pl.BlockSpec(memory_space=pltpu.VMEM)


def _to_all(src_of, dst, ssem, rsem):
    x, y, c = _position()
    me = 4 * x + 2 * y + c
    peers = [((x + dx) % 2, (y + dy) % 2, (c + dc) % 2) for dx, dy, dc in _FLIPS]
    sent = []
    for k, (px, py, pc) in enumerate(peers):
        cp = pltpu.make_async_remote_copy(src_of(2 * px + py), dst.at[me], ssem.at[k], rsem.at[k],
                                          device_id=(px, py, pc), device_id_type=_MESH)
        cp.start()
        sent.append(cp)
    for k, (px, py, pc) in enumerate(peers):
        pltpu.make_async_remote_copy(src_of(2 * px + py), dst.at[4 * px + 2 * py + pc], ssem.at[k], rsem.at[k],
                                     device_id=(px, py, pc), device_id_type=_MESH).wait_recv()
    for cp in sent:
        cp.wait_send()
    return me, 2 * x + y


def _share_small(name, block, total):
    shape = block.shape

    def body(in_ref, out_ref, *scratch):
        buf, ssem, rsem = (out_ref,) + scratch if not total else scratch
        me, _ = _to_all(lambda chip: in_ref, buf, ssem, rsem)
        buf[me] = in_ref[...]
        if total:
            acc = buf[0]
            for d in range(1, 8):
                acc = acc + buf[d]
            out_ref[...] = acc

    sems = [pltpu.SemaphoreType.DMA((7,)), pltpu.SemaphoreType.DMA((7,))]
    return pl.pallas_call(
        body, name=name, in_specs=[_VMEM], out_specs=_VMEM,
        out_shape=jax.ShapeDtypeStruct(shape if total else (8,) + shape, F32),
        scratch_shapes=([pltpu.VMEM((8,) + shape, F32)] if total else []) + sems,
        compiler_params=pltpu.CompilerParams(vmem_limit_bytes=VMEM_LIMIT_V7X))(block)


def _mod_rows_exchange(mv):
    def body(mv_ref, out_ref, ssem, rsem):
        x, y, c = _position()
        me = 2 * x + y
        out_ref[me] = mv_ref[4 * x + 2 * y + c]
        sent = []
        for k, (px, py) in enumerate(_other_chips(x, y)):
            cp = pltpu.make_async_remote_copy(mv_ref.at[4 * px + 2 * py + c], out_ref.at[me], ssem.at[k], rsem.at[k],
                                              device_id=(px, py, c), device_id_type=_MESH)
            cp.start()
            sent.append(cp)
        for k, (px, py) in enumerate(_other_chips(x, y)):
            pltpu.make_async_remote_copy(mv_ref.at[0], out_ref.at[2 * px + py], ssem.at[k], rsem.at[k],
                                         device_id=(px, py, c), device_id_type=_MESH).wait_recv()
        for cp in sent:
            cp.wait_send()

    return pl.pallas_call(
        body, name="mod_rows_exchange", in_specs=[_VMEM], out_specs=_VMEM,
        out_shape=jax.ShapeDtypeStruct((N_SHARD,) + mv.shape[1:], F32),
        scratch_shapes=[pltpu.SemaphoreType.DMA((3,)), pltpu.SemaphoreType.DMA((3,))],
        compiler_params=pltpu.CompilerParams(vmem_limit_bytes=VMEM_LIMIT_V7X))(mv)


def _mod_grad_exchange(dmj, dm_rows):
    def body(dmj_ref, rows_ref, out_ref, bias_ref, ssem, rsem):
        me, chip = _to_all(lambda j: dmj_ref.at[j], out_ref, ssem, rsem)
        out_ref[me] = dmj_ref[chip]
        for l in range(2):
            bias_ref[l] = _colsum(rows_ref[l])

    return pl.pallas_call(
        body, name="mod_grad_exchange", in_specs=[_VMEM, _VMEM], out_specs=[_VMEM, _VMEM],
        out_shape=[jax.ShapeDtypeStruct((8,) + dmj.shape[1:], F32), jax.ShapeDtypeStruct((2, 1, dm_rows.shape[-1]), F32)],
        scratch_shapes=[pltpu.SemaphoreType.DMA((7,)), pltpu.SemaphoreType.DMA((7,))],
        compiler_params=pltpu.CompilerParams(vmem_limit_bytes=VMEM_LIMIT_V7X))(dmj, dm_rows)


def _pack(arrays):
    flat = jnp.concatenate([a.reshape(-1).astype(F32) for a in arrays])
    total = flat.shape[0]
    rows = -(-total // (8 * PACK_LANES)) * 8
    return jnp.pad(flat, (0, rows * PACK_LANES - total)).reshape(rows, PACK_LANES)


def _unpack(pack, shapes):
    flat, out, pos = pack.reshape(-1), [], 0
    for sh in shapes:
        size = int(np.prod(sh)) if len(sh) else 1
        out.append(flat[pos:pos + size].reshape(sh))
        pos += size
    return out


def _block_diag(pw):
    out = jnp.zeros((POOL_W, POOL_W), pw.dtype)
    for g in range(4):
        out = out.at[g * 64:(g + 1) * 64, g * 64:(g + 1) * 64].set(pw[g])
    return out


def _local_step(x, ctx, small, mod3s, loss_target, comm):
    cfg = _Cfg(x.shape[0], x.shape[1], ctx.shape[1])
    assert cfg.seq % TM == 0 and cfg.nc % TM == 0 and cfg.seq % cfg.ctx == 0 and cfg.ctx % CHUNK == 0
    nb_all, nb_lat = cfg.r // TM, cfg.nl // TM
    last = 1
    wf, big = comm.wf, comm.grads
    cos_t, sin_t = _rope_tables(cfg.seq)
    xs = jnp.concatenate([x.reshape(cfg.nl, D), ctx.reshape(cfg.nc, D)], axis=0)
    row = lambda w: pl.BlockSpec((TM, w), lambda i, j, k: (i, 0))
    mod3_spec = pl.BlockSpec((None, 6, D), lambda i, j, k: (cfg.mod_row(i), 0, 0))
    whole = lambda rows: pl.BlockSpec((None, rows, D), lambda i, j, k: (0, 0, 0))

    def conv_params(l):
        dw = jnp.pad(wf[l]["conv_dw"], ((0, 1), (0, 0)))
        return (dw, small["conv_dw_b"][l][None], small["conv_ln_g"][l][None], small["conv_ln_b"][l][None],
                _block_diag(small["pool_w"][l]).astype(BF16), small["pool_scale"][l][None])

    def residual_epi(ig):
        def epi(acc, ex, outs):
            x_ref, m_ref = ex
            outs[0][...] = x_ref[...] + m_ref[ig:ig + 1, :] * acc
            outs[1][...] = acc.astype(BF16)
        return epi

    def hosted(name, call):
        outs, got = call(comm.ride(name))
        comm.landed(name, got)
        return outs

    saved = []
    for l in range(2):
        nb = nb_lat if l == last else nb_all
        wl = wf[l]
        mod3 = mod3s[l]
        h1, qkv, cpin = _in_proj(f"in_proj{l}", xs, small["norm1_g"][l][None], mod3, wl["w_in"], 0, cos_t, sin_t, cfg)
        mix, lse = hosted(f"att_fwd{l}", lambda ride: _att_fwd(f"att_fwd{l}", qkv, small["attn_sink"][l], l != last, cfg, ride))
        prm = conv_params(l)
        mix, yconv = _convpool_fwd(f"convpool_fwd_lat{l}", cpin, mix, None, prm, cfg.seq, 0, cfg)
        if l != last:
            mix, yconv = _convpool_fwd(f"convpool_fwd_ctx{l}", cpin, mix, yconv, prm, cfg.ctx, cfg.nl // cfg.ctx, cfg)
        x1, y1 = _mm(f"out_proj{l}", "nn", (nb, 1, 1), mix, wl["w_out"].reshape(1, D, D), row(D), whole(D),
                     [jax.ShapeDtypeStruct((cfg.r, D), F32), jax.ShapeDtypeStruct((cfg.r, D), BF16)], [row(D), row(D)],
                     extras=[xs, mod3], extra_specs=[row(D), mod3_spec], epi=residual_epi(2))
        h2 = _norm_fwd(f"norm2_fwd{l}", x1, small["norm2_g"][l][None], mod3, 3, 4, nb, cfg)
        gu, act = hosted(f"ffn_in{l}", lambda ride: _ffn_in(f"ffn_in{l}", h2, wl["w_ffn_in"], 0, nb, cfg, ride))
        x2, y2 = _mm(f"ffn_out{l}", "nn", (nb, 1, 1), act, wl["w_ffn_out"].reshape(1, D_FF, D), row(D_FF), whole(D_FF),
                     [jax.ShapeDtypeStruct((cfg.r, D), F32), jax.ShapeDtypeStruct((cfg.r, D), BF16)], [row(D), row(D)],
                     extras=[x1, mod3], extra_specs=[row(D), mod3_spec], epi=residual_epi(5))
        saved.append(dict(mod3=mod3, x0=xs, h1=h1, qkv=qkv, cpin=cpin, mix=mix, yconv=yconv, lse=lse, y1=y1, x1=x1,
                          h2=h2, gu=gu, act=act, y2=y2, prm=prm))
        xs = x2

    dx, loss, d_final_g, dy2, dg2 = _loss_head(xs, loss_target.reshape(cfg.nl, D), small["final_g"][None],
                                               saved[last]["y2"], saved[last]["mod3"], 5, cfg)

    sg = {k: [None, None] for k in ("norm1_g", "norm2_g", "conv_dw", "conv_dw_b", "conv_ln_g", "conv_ln_b",
                                    "attn_sink", "pool_w", "pool_scale")}
    dms = [None, None]

    def swiglu_bwd_epi(acc, ex, outs):
        outs[0][0] = (acc * ex[0][0].astype(F32)).astype(BF16)
        outs[0][1] = (acc * ex[0][1].astype(F32)).astype(BF16)

    def halves_epi(acc, ex, outs):
        h = acc.shape[0] // 2
        outs[0][0] = acc[:h].astype(BF16)
        outs[0][1] = acc[h:].astype(BF16)

    def row_shards_epi(n):
        def epi(acc, ex, outs):
            s0 = acc.shape[0] // n
            h = s0 // 2
            for t in range(n):
                for half in range(2):
                    outs[0][half, t] = acc[t * s0 + half * h:t * s0 + (half + 1) * h].astype(BF16)
        return epi

    def col_shards_epi(acc, ex, outs):
        h = acc.shape[0] // 2
        for j in range(N_SHARD):
            for half in range(2):
                outs[0][half, j] = acc[half * h:(half + 1) * h, j * IN_SHARD:(j + 1) * IN_SHARD].astype(BF16)

    for l in (1, 0):
        sv = saved[l]
        mod3 = sv["mod3"]
        nb = nb_lat if l == last else nb_all
        tr = _dw_rows(nb * TM)
        steps = nb * TM // tr
        wl = wf[l]
        gu_spec = pl.BlockSpec((2, TM, HALF_FF), lambda j, i, k: (0, i, j))
        df = _mm(f"ffn_out_bwd{l}", "nt", (2, nb, 1), dy2, wl["w_ffn_out"].reshape(1, D_FF, D),
                 pl.BlockSpec((TM, D), lambda j, i, k: (i, 0)), pl.BlockSpec((None, HALF_FF, D), lambda j, i, k: (0, j, 0)),
                 [jax.ShapeDtypeStruct((2, cfg.r, D_FF), BF16)], [gu_spec], extras=[sv["gu"]], extra_specs=[gu_spec],
                 epi=swiglu_bwd_epi)[0]
        big[l]["w_ffn_out"] = _mm_dw(
            f"dw_ffn_out{l}", sv["act"], dy2, pl.BlockSpec((tr, HALF_FF), lambda i, j, k: (k, j)),
            pl.BlockSpec((tr, D), lambda i, j, k: (k, 0)), (2, N_SHARD, D_FF // 8, D),
            pl.BlockSpec((2, 2, D_FF // 8, D), lambda i, j, k: (0, j, 0, 0)), 2, (HALF_FF, D), steps, row_shards_epi(2))[0]
        ride = comm.ride(f"dw_ffn_in{l}")
        res = _mm_dw(f"dw_ffn_in{l}", sv["h2"], df, pl.BlockSpec((tr, D), lambda i, j, k: (k, 0)),
                     pl.BlockSpec((None, tr, HALF_FF), lambda i, j, k: (j // 2, k, j % 2)), (2, N_SHARD, D // 2, HALF_FF),
                     pl.BlockSpec((2, None, D // 2, HALF_FF), lambda i, j, k: (0, j, 0, 0)), N_SHARD, (D, HALF_FF), steps,
                     halves_epi, ride=ride)
        res, got = res if ride is not None else (res, [])
        big[l]["w_ffn_in"] = res[0]
        comm.landed(f"dw_ffn_in{l}", got)
        dx1, dsh2, dsc2, dn2, dy1, dg1 = _proj_norm_bwd(
            f"ffn_in_bwd{l}", df, pl.BlockSpec((2, TM, D_FF), lambda i: (0, i, 0)),
            lambda a_ref, j: a_ref[j // 2, :, (j % 2) * HALF_FF:(j % 2 + 1) * HALF_FF], wl["w_ffn_in"],
            sv["x1"], dx, small["norm2_g"][l][None], mod3, 4, nb, False, (sv["y1"], mod3, 2), cfg)
        dmix = _mm(f"out_proj_bwd{l}", "nt", (nb, 1, 1), dy1, wl["w_out"].reshape(1, D, D), row(D), whole(D),
                   [jax.ShapeDtypeStruct((cfg.r, D), F32)], [row(D)])[0]
        big[l]["w_out"] = _mm_dw(
            f"dw_out{l}", sv["mix"], dy1, pl.BlockSpec((tr, D), lambda i, j, k: (k, 0)),
            pl.BlockSpec((tr, D), lambda i, j, k: (k, 0)), (2, N_SHARD, D // 8, D),
            pl.BlockSpec((2, N_SHARD, D // 8, D), lambda i, j, k: (0, 0, 0, 0)), 1, (D, D), steps, row_shards_epi(N_SHARD))[0]
        du, dkvl, dkvc, dsink = hosted(f"att_bwd{l}", lambda ride: _att_bwd(
            f"att_bwd{l}", sv["qkv"], sv["mix"], dmix, sv["lse"], small["attn_sink"][l], cos_t, sin_t, l != last, cfg, ride))
        acc = [du] + [jnp.zeros(sh, F32) for sh in _SMALL_SHAPES]
        acc = _convpool_bwd(f"convpool_bwd_lat{l}", sv["cpin"], sv["yconv"], dmix, sv["prm"], acc, cfg.seq, 0, cfg)
        if l != last:
            acc = _convpool_bwd(f"convpool_bwd_ctx{l}", sv["cpin"], sv["yconv"], dmix, sv["prm"], acc, cfg.ctx,
                                cfg.nl // cfg.ctx, cfg)
        du, g_dw, g_dwb, g_lng, g_lnb, g_wbd, g_ps = acc
        du = _place_kv(f"place_kv{l}", du, dkvl, dkvc, l != last, cfg)
        if l == last:
            du = _place_ctx_kv_only(f"place_ctx_kv{l}", du, dkvc, cfg)
        sg["attn_sink"][l] = dsink[:, 0]
        sg["conv_dw"][l], sg["conv_dw_b"][l], sg["conv_ln_g"][l], sg["conv_ln_b"][l] = g_dw[:CONV_K], g_dwb[0], g_lng[0], g_lnb[0]
        sg["pool_w"][l] = jnp.stack([g_wbd[g * 64:(g + 1) * 64, g * 64:(g + 1) * 64] for g in range(4)])
        sg["pool_scale"][l] = g_ps[0]
        tr_all = _dw_rows(cfg.r)
        big[l]["w_in"] = _mm_dw(
            f"dw_in{l}", sv["h1"], du, pl.BlockSpec((tr_all, D), lambda i, j, k: (k, 0)),
            pl.BlockSpec((tr_all, IN_W), lambda i, j, k: (k, 0)), (2, N_SHARD, D // 2, IN_SHARD),
            pl.BlockSpec((2, N_SHARD, D // 2, IN_SHARD), lambda i, j, k: (0, 0, 0, 0)), 1, (D, IN_W), cfg.r // tr_all,
            col_shards_epi)[0]
        below = (saved[l - 1]["y2"], saved[l - 1]["mod3"], 5) if l > 0 else None
        res = _proj_norm_bwd(
            f"in_proj_bwd{l}", du, pl.BlockSpec((TM, IN_W), lambda i: (i, 0)),
            lambda a_ref, j: a_ref[:, j * IN_SHARD:(j + 1) * IN_SHARD], wl["w_in"],
            sv["x0"], dx1, small["norm1_g"][l][None], mod3, 1, nb_all, l == last, below, cfg, dx_latent_only=l == 0)
        dx, dsh1, dsc1, dn1 = res[:4]
        sg["norm1_g"][l], sg["norm2_g"][l] = dn1[0], dn2[0]
        parts = [dsh1, dsc1, dg1, dsh2, dsc2, dg2]
        if below is not None:
            dy2, dg2 = res[4:]
        dm = jnp.concatenate([t[:cfg.b, 0, :] for t in parts], axis=1)
        live = (0, 1) if l == last else range(6)
        dm_ctx = jnp.concatenate([t[cfg.b, 0, :] if i in live else jnp.zeros((D,), F32) for i, t in enumerate(parts)])
        dms[l] = jnp.concatenate([dm, dm_ctx[None, :], jnp.zeros((MOD_ROWS - cfg.b - 1, 6 * D), F32)], axis=0)

    grad_x = dx.reshape(x.shape)
    small_grads = {k: jnp.stack(v) for k, v in sg.items()}
    small_grads["final_g"] = d_final_g[0]
    return loss, grad_x, small_grads, dms


_BIG = ("w_in", "w_out", "w_ffn_in", "w_ffn_out")
_TAPS = "conv_dw"
_SMALL = ("c_ctx", "b_mod", "norm1_g", "norm2_g", "conv_dw", "conv_dw_b", "conv_ln_g", "conv_ln_b", "attn_sink",
          "pool_w", "pool_scale", "final_g")
_ORDER = ("c_ctx", "w_mod", "b_mod", "norm1_g", "norm2_g", "w_in", "conv_dw", "conv_dw_b", "conv_ln_g", "conv_ln_b",
          "attn_sink", "pool_w", "pool_scale", "w_out", "w_ffn_in", "w_ffn_out", "final_g")
_GATHER_HOSTS = {
    "first": ((0, "w_in"),),
    "att_fwd0": ((0, "w_out"), (0, "w_ffn_in"), (0, "w_ffn_out"), (0, _TAPS)),
    "ffn_in0": tuple((1, k) for k in _BIG + (_TAPS,)),
}
_REDUCE_HOSTS = {
    "dw_ffn_in0": tuple((1, k) for k in _BIG),
    "att_bwd0": ((0, "w_ffn_in"), (0, "w_ffn_out")),
    "last": ((0, "w_in"), (0, "w_out")),
}


class _Comm:
    def __init__(self, w, c_idx, jc_idx):
        self.shapes = {k: w[k].shape[1:] for k in _BIG}
        self.c_idx, self.jc_idx = c_idx, jc_idx
        halves = lambda a: a.reshape(2, a.shape[0] // 2, a.shape[1])
        taps = jnp.pad(w[_TAPS], ((0, 0), (0, 1), (0, 64)))
        cast = {k: w[k].astype(BF16) for k in _BIG}
        self.shards = [{**{k: halves(cast[k][l]) for k in _BIG}, _TAPS: halves(taps[l])} for l in range(2)]
        self.wf = [dict(), dict()]
        self.grads = [dict(), dict()]
        self.reduced = [dict(), dict()]
        self._open = {}
        self.landed("first", _run_ride("gather_first", self.ride("first")))

    def ride(self, host):
        if host in _GATHER_HOSTS:
            return _gather_ride([self.shards[layer][k] for layer, k in _GATHER_HOSTS[host]])
        if host in _REDUCE_HOSTS:
            what = _REDUCE_HOSTS[host]
            mine = [self.grads[layer][k] for layer, k in what]
            other = _send_other_half(f"send_other_half_{host}", mine)
            pair = [_pair_add(f"pair_add{layer}_{k}", a, b, self.c_idx) for (layer, k), a, b in zip(what, mine, other)]
            self._open[host] = pair
            return _exchange_ride(pair)
        return None

    def landed(self, host, got):
        if host in _GATHER_HOSTS:
            for (layer, k), f in zip(_GATHER_HOSTS[host], got):
                if k == _TAPS:
                    taps = f.reshape(N_SHARD, 32, 128)[:, :CONV_K, :64]
                    self.wf[layer][k] = jnp.transpose(taps, (1, 0, 2)).reshape(CONV_K, CONV_W)
                else:
                    self.wf[layer][k] = f.reshape((1, N_SHARD) + self.shapes[k])
        if host in _REDUCE_HOSTS:
            what = _REDUCE_HOSTS[host]
            mine = [_shard_sum(f"shard_sum{layer}_{k}", a, b, self.jc_idx)
                    for (layer, k), a, b in zip(what, self._open.pop(host), got)]
            for (layer, k), g in zip(what, _swap_reduced(f"swap_reduced_{host}", mine)):
                self.reduced[layer][k] = g.reshape(self.shapes[k])


def _conditioning(c, c_ctx, w_mod, b_mod, chip):
    b = c.shape[0]
    block = jnp.concatenate([c, c_ctx[None, :], jnp.zeros((8 - b - 1, D), F32)], axis=0)
    c_all = _share_small("share_c", block, False).reshape(64, D)
    bias = lax.dynamic_slice_in_dim(b_mod, chip * MOD_W, MOD_W, axis=1)
    full = lambda r, q: pl.BlockSpec((r, q), lambda i, j, k: (0, 0))

    def bias_epi(acc, ex, outs):
        outs[0][...] = acc + ex[0][...]

    mv = [_mm(f"mod_fwd{l}", "nn", (1, 1, 1), c_all, w_mod[l], full(64, D), full(D, MOD_W),
              [jax.ShapeDtypeStruct((64, MOD_W), F32)], [full(64, MOD_W)], extras=[bias[l][None]],
              extra_specs=[full(1, MOD_W)], a_fn=_silu, epi=bias_epi)[0] for l in range(2)]
    by_dev = jnp.transpose(jnp.stack(mv).reshape(2, 8, 8, MOD_W), (1, 0, 2, 3))
    rows = jnp.transpose(_mod_rows_exchange(by_dev), (1, 2, 0, 3)).reshape(2, 8, 6 * D)
    rows = jnp.pad(rows, ((0, 0), (0, MOD_ROWS - 8), (0, 0)))
    return [rows[l].reshape(MOD_ROWS, 6, D) for l in range(2)], c_all


def _conditioning_bwd(dms, c_all, w_mod, b):
    dm = jnp.stack([d[:8] for d in dms])
    by_chip = jnp.transpose(dm.reshape(2, 8, N_SHARD, MOD_W), (2, 0, 1, 3))
    gathered, d_bias = _mod_grad_exchange(by_chip, dm)
    dm_all = jnp.transpose(gathered, (1, 0, 2, 3)).reshape(2, 64, MOD_W)
    full = lambda r, q: pl.BlockSpec((r, q), lambda i, j, k: (0, 0))

    def ctx_rows_epi(acc, ex, outs):
        row = lax.broadcasted_iota(jnp.int32, acc.shape, 0) % 8
        outs[0][...] = _colsum(jnp.where(row == b, acc * _dsilu(ex[0][...]), 0.0))

    g_mod, d_ctx = [], jnp.zeros((D,), F32)
    for l in range(2):
        g_mod.append(_mm(f"dw_mod{l}", "tn", (1, 1, 1), c_all, dm_all[l], full(64, D), full(64, MOD_W),
                         [jax.ShapeDtypeStruct((D, MOD_W), F32)], [full(D, MOD_W)], a_fn=_silu)[0])
        part = _mm(f"mod_bwd{l}", "nt", (1, 1, 1), dm_all[l], w_mod[l], full(64, MOD_W), full(D, MOD_W),
                   [jax.ShapeDtypeStruct((1, D), F32)], [full(1, D)], extras=[c_all], extra_specs=[full(64, D)],
                   epi=ctx_rows_epi)[0]
        d_ctx = d_ctx + part[0]
    return g_mod, d_bias[:, 0, :], d_ctx


def kernel(x, c, ctx, c_ctx, w_mod, b_mod, norm1_g, norm2_g, w_in, conv_dw, conv_dw_b, conv_ln_g, conv_ln_b, attn_sink, pool_w, pool_scale, w_out, w_ffn_in, w_ffn_out, final_g, loss_target, m_c_ctx, m_w_mod, m_b_mod, m_norm1_g, m_norm2_g, m_w_in, m_conv_dw, m_conv_dw_b, m_conv_ln_g, m_conv_ln_b, m_attn_sink, m_pool_w, m_pool_scale, m_w_out, m_w_ffn_in, m_w_ffn_out, m_final_g, v_c_ctx, v_w_mod, v_b_mod, v_norm1_g, v_norm2_g, v_w_in, v_conv_dw, v_conv_dw_b, v_conv_ln_g, v_conv_ln_b, v_attn_sink, v_pool_w, v_pool_scale, v_w_out, v_w_ffn_in, v_w_ffn_out, v_final_g):
    w = dict(c_ctx=c_ctx, w_mod=w_mod, b_mod=b_mod, norm1_g=norm1_g, norm2_g=norm2_g, w_in=w_in, conv_dw=conv_dw,
             conv_dw_b=conv_dw_b, conv_ln_g=conv_ln_g, conv_ln_b=conv_ln_b, attn_sink=attn_sink, pool_w=pool_w,
             pool_scale=pool_scale, w_out=w_out, w_ffn_in=w_ffn_in, w_ffn_out=w_ffn_out, final_g=final_g)
    m = dict(c_ctx=m_c_ctx, w_mod=m_w_mod, b_mod=m_b_mod, norm1_g=m_norm1_g, norm2_g=m_norm2_g, w_in=m_w_in,
             conv_dw=m_conv_dw, conv_dw_b=m_conv_dw_b, conv_ln_g=m_conv_ln_g, conv_ln_b=m_conv_ln_b,
             attn_sink=m_attn_sink, pool_w=m_pool_w, pool_scale=m_pool_scale, w_out=m_w_out, w_ffn_in=m_w_ffn_in,
             w_ffn_out=m_w_ffn_out, final_g=m_final_g)
    v = dict(c_ctx=v_c_ctx, w_mod=v_w_mod, b_mod=v_b_mod, norm1_g=v_norm1_g, norm2_g=v_norm2_g, w_in=v_w_in,
             conv_dw=v_conv_dw, conv_dw_b=v_conv_dw_b, conv_ln_g=v_conv_ln_g, conv_ln_b=v_conv_ln_b,
             attn_sink=v_attn_sink, pool_w=v_pool_w, pool_scale=v_pool_scale, w_out=v_w_out, w_ffn_in=v_w_ffn_in,
             w_ffn_out=v_w_ffn_out, final_g=v_final_g)
    xi, yi, ci = _position()
    chip = 2 * xi + yi
    mod3s, c_all = _conditioning(c, c_ctx, w_mod, b_mod, chip)
    comm = _Comm(w, jnp.reshape(ci, (1,)).astype(jnp.int32), jnp.stack([chip, ci]).astype(jnp.int32))
    small = {k: w[k] for k in _SMALL if k not in ("conv_dw", "c_ctx", "b_mod")}
    loss, grad_x, sgrads, dms = _local_step(x, ctx, small, mod3s, loss_target, comm)
    g_mod, sgrads["b_mod"], d_ctx = _conditioning_bwd(dms, c_all, w_mod, c.shape[0])
    sgrads["c_ctx"] = 0.5 * d_ctx

    names = list(_SMALL)
    total = _share_small("sum_small", _pack([loss] + [sgrads[k] for k in names]), True)
    parts = _unpack(total, [()] + [sgrads[k].shape for k in names])
    loss_out = parts[0]
    gsmall = dict(zip(names, parts[1:]))
    gsmall["conv_dw"] = lax.dynamic_slice_in_dim(gsmall["conv_dw"], chip * 64, 64, axis=2)

    grads, delta, new_m, new_v = dict(gsmall), {}, {}, {}
    for k in ("w_mod",) + _BIG:
        s0, s1 = w[k].shape[1:]
        flat = lambda a: a.reshape(2 * s0, s1)
        g_layers = g_mod if k == "w_mod" else [comm.reduced[l][k] for l in range(2)]
        outs, got = _adamw_layers(f"adamw_{k}", flat(w[k]), g_layers, flat(m[k]), flat(v[k]),
                                  comm.ride("last") if k == "w_mod" else None)
        if k == "w_mod":
            comm.landed("last", got)
        grads[k], delta[k], new_m[k], new_v[k] = [a.reshape(w[k].shape) for a in outs]
    d_, m_, v_ = _adamw("adamw_small", _pack([w[k] for k in names]), _pack([gsmall[k] for k in names]),
                        _pack([m[k] for k in names]), _pack([v[k] for k in names]))
    sshapes = [w[k].shape for k in names]
    for k, a, b, e in zip(names, _unpack(d_, sshapes), _unpack(m_, sshapes), _unpack(v_, sshapes)):
        delta[k], new_m[k], new_v[k] = a, b, e
    return (loss_out, grad_x, *[grads[k] for k in _ORDER], *[delta[k] for k in _ORDER],
            *[new_m[k] for k in _ORDER], *[new_v[k] for k in _ORDER])
```

```python
from typing import NamedTuple

import jax
import jax.numpy as jnp
import numpy as np
from jax import lax
from jax.experimental import pallas as pl
from jax.experimental.pallas import tpu as pltpu

F32 = jnp.float32
BF16 = jnp.bfloat16

D = 1024
GRID_W = 64
HEAD_DIM = 64
N_HEADS = 8
ATTN_W = 512
CONV_W = 256
POOL_W = 256
IN_W = 1536
D_FF = 2816
CONV_K = 31
QB = 128
ROPE_BASE = 10000.0
EPS = 1e-6
NEG = -1e30
N_SHARD = 4
IN_SHARD = IN_W // N_SHARD
HALF_FF = D_FF // 2
MOD_W = 6 * D // N_SHARD
MOD_ROWS = 16
PACK_LANES = 128

ADAM_LR = 0.001
ADAM_B1 = 0.9
ADAM_B2 = 0.999
ADAM_EPS = 1e-08
ADAM_WD = 0.01
ADAM_STEP = 10

VMEM_LIMIT_V7X = 56 * 1024 * 1024
TM = 512
TR_MAX = 1024
CHUNK = 256
PAD = 16

_MESH = pl.DeviceIdType.MESH
_ANY = pl.BlockSpec(memory_space=pl.ANY)
_DIMS = {"nn": (((1,), (0,)), ((), ())), "nt": (((1,), (1,)), ((), ())), "tn": (((0,), (0,)), ((), ()))}


class _Cfg(NamedTuple):
    b: int
    seq: int
    ctx: int

    @property
    def nl(self):
        return self.b * self.seq

    @property
    def nc(self):
        return self.b * self.ctx

    @property
    def r(self):
        return self.nl + self.nc

    def mod_row(self, i):
        return jnp.where(i < self.nl // TM, i // (self.seq // TM), self.b)

    def first_of_row(self, i):
        nlb = self.nl // TM
        return jnp.logical_or(jnp.logical_and(i < nlb, i % (self.seq // TM) == 0), i == nlb)


def _params(n_grid=0):
    sem = ("arbitrary",) * n_grid if n_grid else None
    return pltpu.CompilerParams(dimension_semantics=sem, vmem_limit_bytes=VMEM_LIMIT_V7X)


def _dot(a, b, mode="nn"):
    return lax.dot_general(a.astype(BF16), b.astype(BF16), _DIMS[mode], preferred_element_type=F32)


def _sigmoid(x):
    return 1.0 / (1.0 + jnp.exp(-x))


def _silu(x):
    return x * _sigmoid(x)


def _dsilu(x):
    s = _sigmoid(x)
    return s * (1.0 + x * (1.0 - s))


def _colsum(v):
    return jnp.sum(v, axis=0, keepdims=True)


def _dw_rows(rows):
    return TR_MAX if rows % TR_MAX == 0 else TM


def _epi_store(acc, ex, outs):
    for o in outs:
        o[...] = acc.astype(o.dtype)


class _Ride(NamedTuple):
    ins: list
    out_shape: list
    scratch: list
    start: object
    finish: object


class _Hosted(NamedTuple):
    ride: _Ride
    n_in: int
    n_out: int
    grid: tuple

    def split(self, refs):
        n_ri, n_ro, n_rs = len(self.ride.ins), len(self.ride.out_shape), len(self.ride.scratch)
        r_in = refs[self.n_in:self.n_in + n_ri]
        r_out = refs[self.n_in + n_ri + self.n_out:self.n_in + n_ri + self.n_out + n_ro]
        own = refs[:self.n_in] + refs[self.n_in + n_ri:self.n_in + n_ri + self.n_out] + \
            refs[self.n_in + n_ri + self.n_out + n_ro:len(refs) - n_rs]
        return own, (r_in, r_out, refs[len(refs) - n_rs:])

    def start(self, parts):
        ids = [pl.program_id(d) for d in range(len(self.grid))]
        first = ids[0] == 0
        for i in ids[1:]:
            first = jnp.logical_and(first, i == 0)
        pl.when(first)(lambda: self.ride.start(*parts))

    def finish(self, parts):
        ids = [pl.program_id(d) for d in range(len(self.grid))]
        last = ids[0] == self.grid[0] - 1
        for i, g in zip(ids[1:], self.grid[1:]):
            last = jnp.logical_and(last, i == g - 1)
        pl.when(last)(lambda: self.ride.finish(*parts))


def _hosted_call(body, ride, name, grid, ins, in_specs, out_shape, out_specs, scratch, params):
    if ride is None:
        res = pl.pallas_call(body, name=name, grid=grid, in_specs=in_specs, out_specs=out_specs, out_shape=out_shape,
                             scratch_shapes=scratch, compiler_params=params)(*ins)
        return list(res), []
    host = _Hosted(ride, len(ins), len(out_shape), tuple(grid))

    def hosted(*refs):
        own, parts = host.split(refs)
        host.start(parts)
        body(*own)
        host.finish(parts)

    res = pl.pallas_call(
        hosted, name=name, grid=grid, in_specs=list(in_specs) + [_ANY] * len(ride.ins),
        out_specs=list(out_specs) + [_ANY] * len(ride.out_shape), out_shape=list(out_shape) + list(ride.out_shape),
        scratch_shapes=list(scratch) + list(ride.scratch), compiler_params=params)(*ins, *ride.ins)
    return list(res[:len(out_shape)]), list(res[len(out_shape):])


def _mm(name, mode, grid, a, b, a_spec, b_spec, out_shape, out_specs, acc_shape=None, extras=(),
        extra_specs=(), a_fn=None, epi=_epi_store, ride=None):
    nk = grid[2]
    n_ex, n_out = len(extras), len(out_shape)

    def body(*refs):
        a_ref, b_ref = refs[:2]
        ex = refs[2:2 + n_ex]
        outs = refs[2 + n_ex:2 + n_ex + n_out]
        av = a_ref[...]
        if a_fn is not None:
            av = a_fn(av)
        part = _dot(av, b_ref[...], mode)
        if nk == 1:
            epi(part, ex, outs)
        else:
            acc = refs[-1]
            k = pl.program_id(2)

            @pl.when(k == 0)
            def _():
                acc[...] = part

            @pl.when(k > 0)
            def _():
                acc[...] += part

            @pl.when(k == nk - 1)
            def _():
                epi(acc[...], ex, outs)

    scratch = [] if nk == 1 else [pltpu.VMEM(acc_shape, F32)]
    outs, ride_outs = _hosted_call(body, ride, name, grid, [a, b, *extras], [a_spec, b_spec, *extra_specs],
                                   list(out_shape), list(out_specs), scratch, _params(3))
    return outs if ride is None else (outs, ride_outs)


def _mm_dw(name, a, b, a_spec, b_spec, out_shape, out_spec, n_out_blocks, acc_shape, n_steps, epi, a_fn=None,
           extras=(), extra_specs=(), extra_out=(), extra_out_specs=(), ride=None):
    grid = (1, n_out_blocks, n_steps)
    outs = [jax.ShapeDtypeStruct(out_shape, BF16)] + list(extra_out)
    return _mm(name, "tn", grid, a, b, a_spec, b_spec, outs, [out_spec, *extra_out_specs], acc_shape, extras,
               extra_specs, a_fn, epi, ride)


def _gate_step(i, dxv, y_ref, m_ref, ig, dy_ref, dgate_ref, cfg):
    dy_ref[...] = (dxv * m_ref[ig:ig + 1, :]).astype(BF16)
    _accumulate_rows(cfg.first_of_row(i), dgate_ref, _colsum(dxv * y_ref[...].astype(F32)))


def _proj_norm_bwd(name, a, a_spec, pick, w, x, dres, gvec, mod3, isc, nblk, res_latent_only, gate, cfg,
                   dx_latent_only=False):
    ns = w.shape[-1]
    nlb = cfg.nl // TM

    def body(a_ref, w_ref, x_ref, dres_ref, g_ref, m_ref, *rest):
        if gate is None:
            dx_ref, dsh_ref, dsc_ref, dg_ref = rest
        else:
            y_ref, gm_ref, dx_ref, dsh_ref, dsc_ref, dg_ref, dy_ref, dgate_ref = rest
        i = pl.program_id(0)
        dhv = _dot(pick(a_ref, 0), w_ref[0], "nt")
        for j in range(1, N_SHARD):
            dhv = dhv + _dot(pick(a_ref, j), w_ref[j], "nt")
        xv = x_ref[...]
        r = lax.rsqrt(jnp.mean(xv * xv, axis=-1, keepdims=True) + EPS)
        xh = xv * r
        g = g_ref[...]
        sc1 = 1.0 + m_ref[isc:isc + 1, :]
        t = dhv * xh
        first = cfg.first_of_row(i)
        _accumulate_rows(first, dsh_ref, _colsum(dhv))
        _accumulate_rows(first, dsc_ref, _colsum(t * g))
        _accumulate_rows(i == 0, dg_ref, _colsum(t * sc1))
        dxh = dhv * (g * sc1)
        dxn = r * (dxh - xh * jnp.mean(dxh * xh, axis=-1, keepdims=True))
        dxv = (jnp.where(i < nlb, dres_ref[...], 0.0) if res_latent_only else dres_ref[...]) + dxn
        if dx_latent_only:
            @pl.when(i < nlb)
            def _():
                dx_ref[...] = dxv
        else:
            dx_ref[...] = dxv
        if gate is not None:
            _gate_step(i, dxv, y_ref, gm_ref, gate[2], dy_ref, dgate_ref, cfg)

    row = pl.BlockSpec((TM, D), lambda i: (i, 0))
    vec = pl.BlockSpec((1, D), lambda i: (0, 0))
    part = pl.BlockSpec((None, 1, D), lambda i: (cfg.mod_row(i), 0, 0))
    part_shape = jax.ShapeDtypeStruct((MOD_ROWS, 1, D), F32)
    resident = pl.BlockSpec((None, N_SHARD, D, ns), lambda i: (0, 0, 0, 0), pipeline_mode=pl.Buffered(1))
    ins, in_specs = [a, w, x, dres, gvec, mod3], [a_spec, resident, row, row, vec, _mod_spec(cfg)]
    out_specs = [row, part, part, vec]
    out_shape = [jax.ShapeDtypeStruct((cfg.r, D), F32), part_shape, part_shape, jax.ShapeDtypeStruct((1, D), F32)]
    if dx_latent_only:
        out_specs[0] = pl.BlockSpec((TM, D), lambda i: (jnp.minimum(i, nlb - 1), 0))
        out_shape[0] = jax.ShapeDtypeStruct((cfg.nl, D), F32)
    if gate is not None:
        ins, in_specs = ins + list(gate[:2]), in_specs + [row, _mod_spec(cfg)]
        out_specs, out_shape = out_specs + [row, part], out_shape + [jax.ShapeDtypeStruct((cfg.r, D), BF16), part_shape]
    return pl.pallas_call(body, name=name, grid=(nblk,), in_specs=in_specs, out_specs=out_specs, out_shape=out_shape,
                          compiler_params=_params(1))(*ins)


def _mod_spec(cfg):
    return pl.BlockSpec((None, 6, D), lambda i: (cfg.mod_row(i), 0, 0))


def _norm_mod(xv, g, m_ref, ish, isc):
    r = lax.rsqrt(jnp.mean(xv * xv, axis=-1, keepdims=True) + EPS)
    return (xv * r * g * (1.0 + m_ref[isc:isc + 1, :]) + m_ref[ish:ish + 1, :]).astype(BF16)


def _accumulate_rows(first, ref, val):
    @pl.when(first)
    def _():
        ref[...] = val

    @pl.when(jnp.logical_not(first))
    def _():
        ref[...] += val


def _loss_head(x, target, gvec, y, mod3, ig, cfg):
    def body(x_ref, t_ref, g_ref, y_ref, m_ref, dx_ref, loss_ref, dg_ref, dy_ref, dgate_ref):
        i = pl.program_id(0)

        @pl.when(i == 0)
        def _():
            loss_ref[...] = jnp.zeros_like(loss_ref)
            dg_ref[...] = jnp.zeros_like(dg_ref)

        xv = x_ref[...]
        g = g_ref[...]
        r = lax.rsqrt(jnp.mean(xv * xv, axis=-1, keepdims=True) + EPS)
        xh = xv * r
        err = xh * g - t_ref[...]
        loss_ref[...] += (0.5 / D) * _colsum(jnp.sum(err * err, axis=-1, keepdims=True))
        dy = err * (1.0 / D)
        dg_ref[...] += _colsum(dy * xh)
        dxh = dy * g
        dxv = r * (dxh - xh * jnp.mean(dxh * xh, axis=-1, keepdims=True))
        dx_ref[...] = dxv
        _gate_step(i, dxv, y_ref, m_ref, ig, dy_ref, dgate_ref, cfg)

    row = pl.BlockSpec((TM, D), lambda i: (i, 0))
    vec = pl.BlockSpec((1, D), lambda i: (0, 0))
    part = pl.BlockSpec((None, 1, D), lambda i: (cfg.mod_row(i), 0, 0))
    return pl.pallas_call(
        body, name="loss_head", grid=(cfg.nl // TM,), in_specs=[row, row, vec, row, _mod_spec(cfg)],
        out_specs=[row, pl.BlockSpec((1, 1), lambda i: (0, 0)), vec, row, part],
        out_shape=[jax.ShapeDtypeStruct((cfg.r, D), F32), jax.ShapeDtypeStruct((1, 1), F32),
                   jax.ShapeDtypeStruct((1, D), F32), jax.ShapeDtypeStruct((cfg.r, D), BF16),
                   jax.ShapeDtypeStruct((MOD_ROWS, 1, D), F32)],
        compiler_params=_params(1))(x, target, gvec, y, mod3)


def _rope_tables(seq):
    rows = seq // GRID_W
    row = jnp.repeat(jnp.arange(rows), GRID_W).astype(F32)
    col = jnp.tile(jnp.arange(GRID_W), rows).astype(F32)
    half = HEAD_DIM // 2
    inv = ROPE_BASE ** (-jnp.arange(0, half, 2, dtype=F32) / half)
    ar, ac = row[:, None] * inv, col[:, None] * inv
    ang = jnp.concatenate([ar, ar, ac, ac], axis=-1)
    sign = jnp.tile(jnp.concatenate([-jnp.ones((16,), F32), jnp.ones((16,), F32)]), 2)
    cos = jnp.tile(jnp.cos(ang), (1, 2))
    sin = jnp.tile(jnp.sin(ang) * sign, (1, 2))
    cos = jnp.concatenate([cos, jnp.ones((TM, 2 * HEAD_DIM), F32)], axis=0)
    sin = jnp.concatenate([sin, jnp.zeros((TM, 2 * HEAD_DIM), F32)], axis=0)
    return cos, sin


def _rope(x, cos, sin_signed, sign):
    lane = lax.broadcasted_iota(jnp.int32, x.shape, 1)
    low = (lane % 32) < 16
    rot = jnp.where(low, pltpu.roll(x, 112, 1), pltpu.roll(x, 16, 1))
    return x * cos + sign * (rot * sin_signed)


def _in_proj(name, x, gvec, mod3, w_in, layer, cos_t, sin_t, cfg):
    nlb, bps = cfg.nl // TM, cfg.seq // TM

    def body(x_ref, g_ref, m_ref, w_ref, cos_ref, sin_ref, h_ref, qkv_ref, cp_ref):
        hv = _norm_mod(x_ref[...], g_ref[...], m_ref, 0, 1)
        h_ref[...] = hv
        u = jnp.concatenate([_dot(hv, w_ref[j]) for j in range(N_SHARD)], axis=1)
        cos, sin = cos_ref[...], sin_ref[...]
        tiles = []
        for t in range(5):
            y = _rope(u[:, 128 * t:128 * (t + 1)], cos, sin, 1.0)
            tiles.append(y * (HEAD_DIM ** -0.5) if t < 4 else y)
        tiles.append(u[:, 640:768])
        qkv_ref[...] = jnp.concatenate(tiles, axis=1).astype(BF16)
        cp_ref[...] = u[:, 768:IN_W].astype(BF16)

    tab = pl.BlockSpec((TM, 128), lambda i: (jnp.where(i < nlb, i % bps, bps), 0))
    half = pl.BlockSpec((TM, 768), lambda i: (i, 0))
    row = pl.BlockSpec((TM, D), lambda i: (i, 0))
    return pl.pallas_call(
        body, name=name, grid=(cfg.r // TM,),
        in_specs=[row, pl.BlockSpec((1, D), lambda i: (0, 0)), _mod_spec(cfg),
                  pl.BlockSpec((None, N_SHARD, D, IN_SHARD), lambda i: (layer, 0, 0, 0)), tab, tab],
        out_specs=[row, half, half],
        out_shape=[jax.ShapeDtypeStruct((cfg.r, D), BF16), jax.ShapeDtypeStruct((cfg.r, 768), BF16),
                   jax.ShapeDtypeStruct((cfg.r, 768), BF16)],
        compiler_params=_params(1))(x, gvec, mod3, w_in, cos_t, sin_t)


def _att_specs(cfg):
    nlb, ncb = cfg.seq // QB, cfg.ctx // QB

    def qblk(s, qb):
        return jnp.where(qb < nlb, s * nlb + qb, cfg.nl // QB + s * ncb + qb - nlb)

    def near(off, col):
        return pl.BlockSpec((QB, 128), lambda s, qb: (s * nlb + jnp.clip(qb + off, 0, nlb - 1), col))

    def ctxs(col):
        return pl.BlockSpec((cfg.ctx, 128), lambda s, qb: (cfg.nl // cfg.ctx + s, col))

    qspec = pl.BlockSpec((QB, ATTN_W), lambda s, qb: (qblk(s, qb), 0))
    kv = [ctxs(4), ctxs(5), near(-1, 4), near(0, 4), near(1, 4), near(-1, 5), near(0, 5), near(1, 5)]
    return qblk, qspec, kv


def _att_scores(qb, nlb, sink_ref, q_ref, k_refs, v_refs, kh):
    is_lat = qb < nlb
    ii = lax.broadcasted_iota(jnp.int32, (4 * QB, 3 * QB), 0) % QB
    col = lax.broadcasted_iota(jnp.int32, (4 * QB, 3 * QB), 1)
    jj, blk = col % QB, col // QB
    off_p = jnp.where(jnp.logical_and(is_lat, qb >= 1), 0.0, NEG)
    off_c = jnp.where(is_lat, 0.0, NEG)
    off_n = jnp.where(jnp.logical_and(is_lat, qb <= nlb - 2), 0.0, NEG)
    inside = jnp.logical_or(blk == 1, jnp.logical_or(jnp.logical_and(blk == 0, jj >= ii),
                                                     jnp.logical_and(blk == 2, jj <= ii)))
    off = jnp.where(blk == 0, off_p, jnp.where(blk == 1, off_c, off_n))
    q4 = jnp.concatenate([q_ref[:, (4 * kh + g) * HEAD_DIM:(4 * kh + g + 1) * HEAD_DIM] for g in range(4)], axis=0)
    rg = lax.broadcasted_iota(jnp.int32, (4 * QB, 1), 0) // QB
    snk = jnp.where(rg == 0, sink_ref[4 * kh],
                    jnp.where(rg == 1, sink_ref[4 * kh + 1], jnp.where(rg == 2, sink_ref[4 * kh + 2], sink_ref[4 * kh + 3])))
    lanes = slice(kh * HEAD_DIM, (kh + 1) * HEAD_DIM)
    kx, vx = k_refs[0][:, lanes], v_refs[0][:, lanes]
    kl = jnp.concatenate([r[:, lanes] for r in k_refs[1:]], axis=0)
    vl = jnp.concatenate([r[:, lanes] for r in v_refs[1:]], axis=0)
    sx = _dot(q4, kx, "nt")
    sl = jnp.where(inside, _dot(q4, kl, "nt"), NEG) + off
    return q4, snk, (kx, kl), (vx, vl), (sx, sl)


def _att_fwd(name, qkv, sink, ctx_queries, cfg, ride=None):
    nlb, ncb = cfg.seq // QB, cfg.ctx // QB
    qblk, qspec, kvspecs = _att_specs(cfg)

    def body(sink_ref, q_ref, kx_ref, vx_ref, kp_ref, kc_ref, kn_ref, vp_ref, vc_ref, vn_ref, o_ref, lse_ref):
        qb = pl.program_id(1)
        for kh in range(2):
            q4, snk, _, vs, ss = _att_scores(qb, nlb, sink_ref, q_ref, (kx_ref, kp_ref, kc_ref, kn_ref),
                                             (vx_ref, vp_ref, vc_ref, vn_ref), kh)
            m = snk
            for s_ in ss:
                m = jnp.maximum(m, jnp.max(s_, axis=-1, keepdims=True))
            den = jnp.exp(snk - m)
            o4 = jnp.zeros((4 * QB, HEAD_DIM), F32)
            for s_, v_ in zip(ss, vs):
                p = jnp.exp(s_ - m)
                den = den + jnp.sum(p, axis=-1, keepdims=True)
                o4 = o4 + _dot(p, v_)
            o4 = o4 / den
            lse = m + jnp.log(den)
            for g in range(4):
                h = 4 * kh + g
                o_ref[:, h * HEAD_DIM:(h + 1) * HEAD_DIM] = o4[g * QB:(g + 1) * QB].astype(BF16)
                lse_ref[:, h:h + 1] = lse[g * QB:(g + 1) * QB]

    return _hosted_call(
        body, ride, name, (cfg.b, nlb + (ncb if ctx_queries else 0)), [sink] + [qkv] * 9,
        [pl.BlockSpec(memory_space=pltpu.SMEM), qspec, *kvspecs],
        [jax.ShapeDtypeStruct((cfg.r, D), BF16), jax.ShapeDtypeStruct((cfg.r, N_HEADS), F32)],
        [pl.BlockSpec((QB, ATTN_W), lambda s, qb: (qblk(s, qb), 0)),
         pl.BlockSpec((QB, N_HEADS), lambda s, qb: (qblk(s, qb), 0))], [], _params(2))


def _att_bwd(name, qkv, mix, dmix, lse, sink, cos_t, sin_t, ctx_queries, cfg, ride=None):
    nlb, ncb = cfg.seq // QB, cfg.ctx // QB
    nqb = nlb + (ncb if ctx_queries else 0)
    qblk, qspec, kvspecs = _att_specs(cfg)

    def body(sink_ref, q_ref, kx_ref, vx_ref, kp_ref, kc_ref, kn_ref, vp_ref, vc_ref, vn_ref, o_ref, do_ref,
             lse_ref, cosq_ref, sinq_ref, cosk_ref, sink_tab_ref, dq_ref, dkvl_ref, dkvc_ref, dsink_ref,
             accl, accc, dqs):
        s_id, qb = pl.program_id(0), pl.program_id(1)

        @pl.when(qb == 0)
        def _():
            accl[...] = jnp.zeros_like(accl)
            accc[...] = jnp.zeros_like(accc)

        @pl.when(jnp.logical_and(s_id == 0, qb == 0))
        def _():
            dsink_ref[...] = jnp.zeros_like(dsink_ref)

        starts = [pl.multiple_of(jnp.clip(qb + off, 0, nlb - 1) * QB, QB) for off in (-1, 0, 1)]
        for kh in range(2):
            q4, snk, ks, vs, ss = _att_scores(qb, nlb, sink_ref, q_ref, (kx_ref, kp_ref, kc_ref, kn_ref),
                                              (vx_ref, vp_ref, vc_ref, vn_ref), kh)
            lanes = slice(kh * HEAD_DIM, (kh + 1) * HEAD_DIM)
            heads =[slice((4 * kh + g) * HEAD_DIM, (4 * kh + g + 1) * HEAD_DIM) for g in range(4)]
            do4 = jnp.concatenate([do_ref[:, hs] for hs in heads], axis=0)
            o4 = jnp.concatenate([o_ref[:, hs] for hs in heads], axis=0).astype(F32)
            lse4 = jnp.concatenate([lse_ref[:, 4 * kh + g:4 * kh + g + 1] for g in range(4)], axis=0)
            delta = jnp.sum(do4 * o4, axis=-1, keepdims=True)
            dq4 = jnp.zeros((4 * QB, HEAD_DIM), F32)
            dks, dvs = [], []
            for s_, k_, v_ in zip(ss, ks, vs):
                p = jnp.exp(s_ - lse4)
                ds = p * (_dot(do4, v_, "nt") - delta)
                dq4 = dq4 + _dot(ds, k_)
                dks.append(_dot(ds, q4, "tn"))
                dvs.append(_dot(p, do4, "tn"))
            accc[:, lanes] += dks[0]
            accc[:, 128 + kh * HEAD_DIM:128 + (kh + 1) * HEAD_DIM] += dvs[0]
            for t, st in enumerate(starts):
                accl[pl.ds(st, QB), lanes] += dks[1][t * QB:(t + 1) * QB]
                accl[pl.ds(st, QB), 128 + kh * HEAD_DIM:128 + (kh + 1) * HEAD_DIM] += dvs[1][t * QB:(t + 1) * QB]
            dsk = -jnp.exp(snk - lse4) * delta
            for g in range(4):
                h = 4 * kh + g
                dsink_ref[h:h + 1, :] += jnp.broadcast_to(_colsum(dsk[g * QB:(g + 1) * QB]), (1, 128))
                dqs[:, heads[g]] = dq4[g * QB:(g + 1) * QB]
        cos, sin = cosq_ref[...], sinq_ref[...]
        dq_ref[...] = jnp.concatenate(
            [_rope(dqs[:, 128 * t:128 * (t + 1)], cos, sin, -1.0) * (HEAD_DIM ** -0.5) for t in range(4)],
            axis=1).astype(BF16)

        @pl.when(qb == nqb - 1)
        def _():
            dk = _rope(accl[:, 0:128], cosk_ref[...], sink_tab_ref[...], -1.0)
            dkvl_ref[...] = jnp.concatenate([dk, accl[:, 128:256]], axis=1).astype(BF16)
            dkvc_ref[...] = accc[...].astype(BF16)

    rowq = lambda w: pl.BlockSpec((QB, w), lambda s, qb: (qblk(s, qb), 0))
    tabq = pl.BlockSpec((QB, 128), lambda s, qb: (jnp.where(qb < nlb, qb, cfg.seq // QB), 0))
    tabk = pl.BlockSpec((cfg.seq, 128), lambda s, qb: (0, 0))
    return _hosted_call(
        body, ride, name, (cfg.b, nqb), [sink] + [qkv] * 9 + [mix, dmix, lse, cos_t, sin_t, cos_t, sin_t],
        [pl.BlockSpec(memory_space=pltpu.SMEM), qspec, *kvspecs, rowq(ATTN_W), rowq(ATTN_W), rowq(N_HEADS),
         tabq, tabq, tabk, tabk],
        [jax.ShapeDtypeStruct((cfg.r, IN_W), BF16), jax.ShapeDtypeStruct((cfg.nl, 256), BF16),
         jax.ShapeDtypeStruct((cfg.nc, 256), BF16), jax.ShapeDtypeStruct((N_HEADS, 128), F32)],
        [rowq(ATTN_W), pl.BlockSpec((cfg.seq, 256), lambda s, qb: (s, 0)),
         pl.BlockSpec((cfg.ctx, 256), lambda s, qb: (s, 0)), pl.BlockSpec((N_HEADS, 128), lambda s, qb: (0, 0))],
        [pltpu.VMEM((cfg.seq, 256), F32), pltpu.VMEM((cfg.ctx, 256), F32), pltpu.VMEM((QB, ATTN_W), F32)], _params(2))


def _pool_geometry(n, c):
    lane = lax.broadcasted_iota(jnp.int32, (1, POOL_W), 1) // HEAD_DIM
    wl = jnp.where(lane == 0, 1, jnp.where(lane == 1, 2, jnp.where(lane == 2, 4, 8)))
    wr = wl - 1
    t = c * CHUNK + lax.broadcasted_iota(jnp.int32, (CHUNK, POOL_W), 0)
    cnt = (jnp.minimum(t + wr, n - 1) - jnp.maximum(t - wl, 0) + 1).astype(F32)
    return wl, wr, cnt


def _build_phases(src, ph, c):
    for s in range(1, 8):
        ph[s - 1] = src[c * CHUNK + s:c * CHUNK + s + CHUNK + 24, :]


def _window(src, ph, c, off):
    a, s = divmod(off, 8)
    if s == 0:
        return src[c * CHUNK + 8 * a:c * CHUNK + 8 * a + CHUNK, :]
    return ph[s - 1, 8 * a:8 * a + CHUNK, :]


def _conv_chunk(hp, ph, dw_ref, dwb_ref, c):
    _build_phases(hp, ph, c)
    acc = jnp.zeros((CHUNK, CONV_W), F32) + dwb_ref[...]
    for j in range(CONV_K):
        acc = acc + dw_ref[j:j + 1, :] * _window(hp, ph, c, j + 1)
    return acc


def _fill_glu(cp_ref, hp, n):
    hp[0:PAD, :] = jnp.zeros((PAD, CONV_W), F32)
    hp[PAD + n:2 * PAD + n, :] = jnp.zeros((PAD, CONV_W), F32)
    for c in range(n // CHUNK):
        rows = slice(c * CHUNK, (c + 1) * CHUNK)
        a = cp_ref[rows, 0:CONV_W].astype(F32)
        g = cp_ref[rows, CONV_W:2 * CONV_W].astype(F32)
        hp[PAD + c * CHUNK:PAD + (c + 1) * CHUNK, :] = a * _sigmoid(g)


def _fill_pool(cp_ref, pp, n):
    pp[0:PAD, :] = jnp.zeros((PAD, POOL_W), F32)
    pp[PAD + n:2 * PAD + n, :] = jnp.zeros((PAD, POOL_W), F32)
    for c in range(n // CHUNK):
        pp[PAD + c * CHUNK:PAD + (c + 1) * CHUNK, :] = cp_ref[c * CHUNK:(c + 1) * CHUNK, 2 * CONV_W:768].astype(F32)


LV = CHUNK + 2 * PAD
_LEVELS = pltpu.VMEM((3, LV + 16, POOL_W), F32)


def _clear_level_edges(lv):
    for b in range(3):
        lv[b, 0:8] = jnp.zeros((8, POOL_W), F32)
        lv[b, 8 + LV:16 + LV] = jnp.zeros((8, POOL_W), F32)


def _window_sums(src, lv, c, lead):
    lv[0, 8:8 + LV] = src[c * CHUNK:c * CHUNK + LV, :]
    lo = 7 if lead < 0 else 8
    lv[1, 8:8 + LV] = lv[0, lo:lo + LV] + lv[0, lo + 1:lo + 1 + LV]
    group = lax.broadcasted_iota(jnp.int32, (1, POOL_W), 1) // HEAD_DIM
    rows = slice(8 + PAD, 8 + PAD + CHUNK)
    res = lv[1, rows]
    cur = 1
    for g, s in ((1, 1), (2, 2), (3, 4)):
        nxt = 3 - cur
        lv[nxt, 8:8 + LV] = lv[cur, 8 - s:8 - s + LV] + lv[cur, 8 + s:8 + s + LV]
        res = jnp.where(group >= g, lv[nxt, rows], res)
        cur = nxt
    return res


def _pool_chunk(pp, lv, n, c):
    _, _, cnt = _pool_geometry(n, c)
    return _window_sums(pp, lv, c, -1) / cnt - pp[PAD + c * CHUNK:PAD + (c + 1) * CHUNK, :], cnt


def _seq_specs(n, blk_off, width, col=0):
    return pl.BlockSpec((n, width), lambda s: (blk_off + s, col))


def _full(shape):
    return pl.BlockSpec(shape, lambda s: (0,) * len(shape))


_PHASES = pltpu.VMEM((7, CHUNK + 24, CONV_W), F32)


def _convpool_fwd(name, cpin, mix, yconv, prm, n, blk_off, cfg):
    dw, dwb, lng, lnb, wbd, ps = prm
    n_alias = 1 if yconv is None else 2

    def body(*refs):
        cp_ref, dw_ref, dwb_ref, lng_ref, lnb_ref, wbd_ref, ps_ref = refs[:7]
        out_ref, y_ref, hp, pp, ph, lv = refs[7 + n_alias:]
        _fill_glu(cp_ref, hp, n)
        _fill_pool(cp_ref, pp, n)
        _clear_level_edges(lv)
        for c in range(n // CHUNK):
            rows = slice(c * CHUNK, (c + 1) * CHUNK)
            y = _conv_chunk(hp, ph, dw_ref, dwb_ref, c)
            y_ref[rows, :] = y
            d = y - jnp.mean(y, axis=-1, keepdims=True)
            hn = d * lax.rsqrt(jnp.mean(d * d, axis=-1, keepdims=True) + EPS) * lng_ref[...] + lnb_ref[...]
            out_ref[rows, 0:CONV_W] = (hn * _sigmoid(hn)).astype(BF16)
            yp, _ = _pool_chunk(pp, lv, n, c)
            out_ref[rows, CONV_W:2 * CONV_W] = (_dot(yp, wbd_ref[...]) * ps_ref[...]).astype(BF16)

    through = [mix] if yconv is None else [mix, yconv]
    return pl.pallas_call(
        body, name=name, grid=(cfg.b,),
        in_specs=[_seq_specs(n, blk_off, 768), _full((32, CONV_W)), _full((1, CONV_W)), _full((1, CONV_W)),
                  _full((1, CONV_W)), _full((POOL_W, POOL_W)), _full((1, POOL_W))] + [_ANY] * n_alias,
        out_specs=[_seq_specs(n, blk_off, 512, 1), _seq_specs(n, blk_off, CONV_W)],
        out_shape=[jax.ShapeDtypeStruct((cfg.r, D), BF16), jax.ShapeDtypeStruct((cfg.r, CONV_W), F32)],
        scratch_shapes=[pltpu.VMEM((n + 2 * PAD, CONV_W), F32), pltpu.VMEM((n + 2 * PAD, POOL_W), F32), _PHASES, _LEVELS],
        input_output_aliases={7 + i: i for i in range(n_alias)},
        compiler_params=_params(1))(cpin, dw, dwb, lng, lnb, wbd, ps, *through)


_SMALL_SHAPES = [(32, CONV_W), (1, CONV_W), (1, CONV_W), (1, CONV_W), (POOL_W, POOL_W), (1, POOL_W)]


def _convpool_bwd(name, cpin, yconv, dmix, prm, acc_in, n, blk_off, cfg):
    dw, dwb, lng, lnb, wbd, ps = prm
    nch = n // CHUNK

    def body(cp_ref, y_ref, dm_ref, dw_ref, dwb_ref, lng_ref, lnb_ref, wbd_ref, ps_ref, dcp_in,
             a_dw, a_dwb, a_lng, a_lnb, a_wbd, a_ps,
             dcp_ref, o_dw, o_dwb, o_lng, o_lnb, o_wbd, o_ps, hp, dyp, pp, wp, dyv, dwacc, ph, lv):
        s = pl.program_id(0)
        _clear_level_edges(lv)

        @pl.when(s == 0)
        def _():
            for o_, a_ in ((o_dw, a_dw), (o_dwb, a_dwb), (o_lng, a_lng), (o_lnb, a_lnb), (o_wbd, a_wbd), (o_ps, a_ps)):
                o_[...] = a_[...]
            dwacc[...] = jnp.zeros_like(dwacc)

        _fill_glu(cp_ref, hp, n)
        _fill_pool(cp_ref, pp, n)
        for ref in (dyp, wp):
            ref[0:PAD, :] = jnp.zeros((PAD, CONV_W), F32)
            ref[PAD + n:2 * PAD + n, :] = jnp.zeros((PAD, CONV_W), F32)
        for c in range(nch):
            rows = slice(c * CHUNK, (c + 1) * CHUNK)
            y = y_ref[rows, :]
            d = y - jnp.mean(y, axis=-1, keepdims=True)
            rstd = lax.rsqrt(jnp.mean(d * d, axis=-1, keepdims=True) + EPS)
            xh = d * rstd
            hn = xh * lng_ref[...] + lnb_ref[...]
            sg = _sigmoid(hn)
            dhn = dm_ref[rows, 0:CONV_W] * (sg * (1.0 + hn * (1.0 - sg)))
            o_lnb[...] += _colsum(dhn)
            o_lng[...] += _colsum(dhn * xh)
            dxh = dhn * lng_ref[...]
            dy = rstd * (dxh - jnp.mean(dxh, axis=-1, keepdims=True) - xh * jnp.mean(dxh * xh, axis=-1, keepdims=True))
            o_dwb[...] += _colsum(dy)
            dyp[PAD + c * CHUNK:PAD + (c + 1) * CHUNK, :] = dy
            _build_phases(hp, ph, c)
            for j in range(CONV_K):
                prod = dy * _window(hp, ph, c, j + 1)
                dwacc[8 * j:8 * j + 8, :] += jnp.sum(prod.reshape(CHUNK // 8, 8, CONV_W), axis=0)
            yp, cnt = _pool_chunk(pp, lv, n, c)
            dz = dm_ref[rows, CONV_W:2 * CONV_W]
            o_ps[...] += _colsum(dz * _dot(yp, wbd_ref[...]))
            dzs = dz * ps_ref[...]
            o_wbd[...] += _dot(yp, dzs, "tn")
            dv = _dot(dzs, wbd_ref[...], "nt")
            dyv[rows, :] = dv
            wp[PAD + c * CHUNK:PAD + (c + 1) * CHUNK, :] = dv / cnt
        for c in range(nch):
            rows = slice(c * CHUNK, (c + 1) * CHUNK)
            _build_phases(dyp, ph, c)
            dh = jnp.zeros((CHUNK, CONV_W), F32)
            for j in range(CONV_K):
                dh = dh + dw_ref[j:j + 1, :] * _window(dyp, ph, c, 31 - j)
            a = cp_ref[rows, 0:CONV_W].astype(F32)
            sg = _sigmoid(cp_ref[rows, CONV_W:2 * CONV_W].astype(F32))
            dcp_ref[rows, 0:CONV_W] = (dh * sg).astype(BF16)
            dcp_ref[rows, CONV_W:2 * CONV_W] = (dh * a * sg * (1.0 - sg)).astype(BF16)
            dcp_ref[rows, 2 * CONV_W:768] = (_window_sums(wp, lv, c, 1) - dyv[rows, :]).astype(BF16)

        @pl.when(s == cfg.b - 1)
        def _():
            for j in range(CONV_K):
                o_dw[j:j + 1, :] += _colsum(dwacc[8 * j:8 * j + 8, :])

    small_specs = [_full(sh) for sh in _SMALL_SHAPES]
    return pl.pallas_call(
        body, name=name, grid=(cfg.b,),
        in_specs=[_seq_specs(n, blk_off, 768), _seq_specs(n, blk_off, CONV_W), _seq_specs(n, blk_off, 512, 1),
                  *small_specs, _ANY, *small_specs],
        out_specs=[_seq_specs(n, blk_off, 768, 1), *small_specs],
        out_shape=[jax.ShapeDtypeStruct((cfg.r, IN_W), BF16)] + [jax.ShapeDtypeStruct(sh, F32) for sh in _SMALL_SHAPES],
        scratch_shapes=[pltpu.VMEM((n + 2 * PAD, CONV_W), F32), pltpu.VMEM((n + 2 * PAD, CONV_W), F32),
                        pltpu.VMEM((n + 2 * PAD, POOL_W), F32), pltpu.VMEM((n + 2 * PAD, POOL_W), F32),
                        pltpu.VMEM((n, POOL_W), F32), pltpu.VMEM((8 * 32, CONV_W), F32), _PHASES, _LEVELS],
        input_output_aliases={9: 0}, compiler_params=_params(1))(cpin, yconv, dmix, dw, dwb, lng, lnb, wbd, ps, *acc_in)


def _place_kv(name, du, dkvl, dkvc, with_ctx, cfg):
    nlb = cfg.nl // TM

    def body(l_ref, c_ref, du_in, o_ref):
        i = pl.program_id(0)
        o_ref[...] = jnp.where(i < nlb, l_ref[...], c_ref[...])

    return pl.pallas_call(
        body, name=name, grid=(cfg.r // TM if with_ctx else nlb,),
        in_specs=[pl.BlockSpec((TM, 256), lambda i: (jnp.minimum(i, nlb - 1), 0)),
                  pl.BlockSpec((TM, 256), lambda i: (jnp.maximum(i - nlb, 0), 0)), _ANY],
        out_specs=pl.BlockSpec((TM, 256), lambda i: (i, 2)), out_shape=jax.ShapeDtypeStruct((cfg.r, IN_W), BF16),
        input_output_aliases={2: 0}, compiler_params=_params(1))(dkvl, dkvc, du)


def _place_ctx_kv_only(name, du, dkvc, cfg):
    nlb = cfg.nl // TM

    def body(c_ref, du_in, o_ref):
        o_ref[...] = jnp.zeros_like(o_ref)
        o_ref[:, ATTN_W:ATTN_W + 256] = c_ref[...]

    return pl.pallas_call(
        body, name=name, grid=(cfg.nc // TM,), in_specs=[pl.BlockSpec((TM, 256), lambda i: (i, 0)), _ANY],
        out_specs=pl.BlockSpec((TM, IN_W), lambda i: (nlb + i, 0)), out_shape=jax.ShapeDtypeStruct((cfg.r, IN_W), BF16),
        input_output_aliases={1: 0}, compiler_params=_params(1))(dkvc, du)


def _norm_fwd(name, x, gvec, mod3, ish, isc, nblk, cfg):
    def body(x_ref, g_ref, m_ref, o_ref):
        o_ref[...] = _norm_mod(x_ref[...], g_ref[...], m_ref, ish, isc)

    row = pl.BlockSpec((TM, D), lambda i: (i, 0))
    return pl.pallas_call(
        body, name=name, grid=(nblk,),
        in_specs=[row, pl.BlockSpec((1, D), lambda i: (0, 0)), _mod_spec(cfg)], out_specs=row,
        out_shape=jax.ShapeDtypeStruct((cfg.r, D), BF16), compiler_params=_params(1))(x, gvec, mod3)


def _ffn_in(name, h, w_ffn_in, layer, nblk, cfg, ride=None):
    def body(h_ref, wg_ref, wu_ref, fac_ref, act_ref):
        hv = h_ref[...]
        g = _dot(hv, wg_ref[...])
        u = _dot(hv, wu_ref[...])
        s = _sigmoid(g)
        gs = g * s
        fac_ref[0] = ((s + gs * (1.0 - s)) * u).astype(BF16)
        fac_ref[1] = gs.astype(BF16)
        act_ref[...] = (gs * u).astype(BF16)

    wspec = lambda base: pl.BlockSpec((None, None, D, HALF_FF), lambda j, i: (layer, base + j, 0, 0))
    return _hosted_call(
        body, ride, name, (2, nblk), [h, w_ffn_in, w_ffn_in],
        [pl.BlockSpec((TM, D), lambda j, i: (i, 0)), wspec(0), wspec(2)],
        [jax.ShapeDtypeStruct((2, cfg.r, D_FF), BF16), jax.ShapeDtypeStruct((cfg.r, D_FF), BF16)],
        [pl.BlockSpec((2, TM, HALF_FF), lambda j, i: (0, i, j)), pl.BlockSpec((TM, HALF_FF), lambda j, i: (i, j))],
        [], _params(2))


def _row_block(rows, cols, max_bytes=1 << 20):
    best = 16
    for t in range(16, rows + 1, 16):
        if rows % t == 0 and t * cols * 4 <= max_bytes:
            best = t
    assert rows % best == 0
    return best


def _pair_add(name, own32, recv, c_idx):
    _, _, s0, s1 = own32.shape
    tr = _row_block(s0, s1)

    def body(c_ref, a_ref, b_ref, o_ref):
        o_ref[...] = (a_ref[...].astype(F32) + b_ref[...].astype(F32)).astype(BF16)

    grid_spec = pltpu.PrefetchScalarGridSpec(
        num_scalar_prefetch=1, grid=(N_SHARD * s0 // tr,),
        in_specs=[pl.BlockSpec((None, tr, s1), lambda i, c: (c[0], i, 0)), pl.BlockSpec((tr, s1), lambda i, c: (i, 0))],
        out_specs=pl.BlockSpec((tr, s1), lambda i, c: (i, 0)))
    out = pl.pallas_call(body, name=name, grid_spec=grid_spec, out_shape=jax.ShapeDtypeStruct((N_SHARD * s0, s1), BF16),
                         compiler_params=_params(1))(c_idx, own32.reshape(2, N_SHARD * s0, s1), recv.reshape(N_SHARD * s0, s1))
    return out.reshape(N_SHARD, s0, s1)


def _shard_sum(name, pair_sum, recv, jc_idx):
    _, s0, s1 = pair_sum.shape
    tr = _row_block(s0, s1)

    def body(jc_ref, a_ref, b_ref, o_ref):
        o_ref[...] = ((a_ref[...].astype(F32) + b_ref[0].astype(F32)) + b_ref[1].astype(F32)) + b_ref[2].astype(F32)

    grid_spec = pltpu.PrefetchScalarGridSpec(
        num_scalar_prefetch=1, grid=(s0 // tr,),
        in_specs=[pl.BlockSpec((None, tr, s1), lambda i, jc: (jc[0], i, 0)), pl.BlockSpec((3, tr, s1), lambda i, jc: (0, i, 0))],
        out_specs=pl.BlockSpec((None, tr, s1), lambda i, jc: (jc[1], i, 0)))
    return pl.pallas_call(body, name=name, grid_spec=grid_spec, out_shape=jax.ShapeDtypeStruct((2, s0, s1), F32),
                          compiler_params=_params(1))(jc_idx, pair_sum, recv)


def _adamw_math(w, g, m, v):
    m = ADAM_B1 * m + (1.0 - ADAM_B1) * g
    v = ADAM_B2 * v + (1.0 - ADAM_B2) * (g * g)
    m_hat = m / (1.0 - ADAM_B1 ** ADAM_STEP)
    v_hat = v / (1.0 - ADAM_B2 ** ADAM_STEP)
    delta = -ADAM_LR * (m_hat / (jnp.sqrt(v_hat) + ADAM_EPS) + ADAM_WD * w)
    return delta, m, v


def _adamw(name, w, g, m, v):
    rows, cols = w.shape
    tr = rows if rows % 16 else _row_block(rows, cols, 1 << 19)

    def body(w_ref, g_ref, m_ref, v_ref, d_ref, mo_ref, vo_ref):
        d, mn, vn = _adamw_math(w_ref[...], g_ref[...], m_ref[...], v_ref[...])
        d_ref[...] = d
        mo_ref[...] = mn
        vo_ref[...] = vn

    spec = pl.BlockSpec((tr, cols), lambda i: (i, 0))
    shape = jax.ShapeDtypeStruct((rows, cols), F32)
    return pl.pallas_call(body, name=name, grid=(rows // tr,), in_specs=[spec] * 4, out_specs=[spec] * 3,
                          out_shape=[shape] * 3, compiler_params=_params(1))(w, g, m, v)


def _adamw_layers(name, w, g_layers, m, v):
    rows, cols = w.shape
    s0 = rows // 2
    tr = _row_block(s0, cols, 1 << 19)
    nb = s0 // tr

    def body(w_ref, g0_ref, g1_ref, m_ref, v_ref, g_ref, d_ref, mo_ref, vo_ref):
        g = jnp.where(pl.program_id(0) < nb, g0_ref[...], g1_ref[...])
        d, mn, vn = _adamw_math(w_ref[...], g, m_ref[...], v_ref[...])
        g_ref[...] = g
        d_ref[...] = d
        mo_ref[...] = mn
        vo_ref[...] = vn

    spec = pl.BlockSpec((tr, cols), lambda i: (i, 0))
    shape = jax.ShapeDtypeStruct((rows, cols), F32)
    return pl.pallas_call(
        body, name=name, grid=(2 * nb,),
        in_specs=[spec, pl.BlockSpec((tr, cols), lambda i: (jnp.minimum(i, nb - 1), 0)),
                  pl.BlockSpec((tr, cols), lambda i: (jnp.maximum(i - nb, 0), 0)), spec, spec],
        out_specs=[spec] * 4, out_shape=[shape] * 4, compiler_params=_params(1))(w, g_layers[0], g_layers[1], m, v)


def _position():
    return lax.axis_index("x"), lax.axis_index("y"), lax.axis_index("c")


def _other_chips(x, y):
    return [(1 - x, y), (x, 1 - y), (1 - x, 1 - y)]


def _run_ride(name, ride):
    n_in, n_out = len(ride.ins), len(ride.out_shape)

    def body(*refs):
        parts = (refs[:n_in], refs[n_in:n_in + n_out], refs[n_in + n_out:])
        ride.start(*parts)
        ride.finish(*parts)

    return pl.pallas_call(
        body, name=name, in_specs=[_ANY] * n_in, out_specs=[_ANY] * n_out, out_shape=ride.out_shape,
        scratch_shapes=ride.scratch, compiler_params=pltpu.CompilerParams(vmem_limit_bytes=VMEM_LIMIT_V7X))(*ride.ins)


def _gather_ride(shards):
    n = len(shards)

    def copies(ins, outs, scr):
        ssem, rsem = scr[n], scr[n + 1]
        x, y, c = _position()
        me, sibling = 2 * x + y, (x, y, 1 - c)

        def remote(src, dst, i, dev):
            return pltpu.make_async_remote_copy(src, dst, ssem.at[i], rsem.at[i], device_id=dev, device_id_type=_MESH)

        fetch_out, fetch_in, pass_out, pass_in = [], [], [], []
        for a, (src, dst) in enumerate(zip(ins, outs)):
            for k, (px, py) in enumerate(_other_chips(x, y)):
                j, i1, i2 = 2 * px + py, 3 * a + k, 3 * n + 3 * a + k
                fetch_out.append(remote(src.at[c], dst.at[me, c], i1, (px, py, c)))
                fetch_in.append(remote(src.at[c], dst.at[j, c], i1, (px, py, c)))
                pass_out.append(remote(dst.at[j, c], dst.at[j, c], i2, sibling))
                pass_in.append(remote(dst.at[j, 1 - c], dst.at[j, 1 - c], i2, sibling))
        return me, fetch_out, fetch_in, pass_out, pass_in

    def start(ins, outs, scr):
        bufs, lsem = scr[:n], scr[n + 2]
        me, fetch_out, _, _, _ = copies(ins, outs, scr)
        for cp in fetch_out:
            cp.start()
        loads = []
        for a, (src, buf) in enumerate(zip(ins, bufs)):
            ld = pltpu.make_async_copy(src, buf, lsem.at[2 * a])
            ld.start()
            loads.append(ld)
        for a, (ld, buf, dst) in enumerate(zip(loads, bufs, outs)):
            ld.wait()
            st = pltpu.make_async_copy(buf, dst.at[me], lsem.at[2 * a + 1])
            st.start()
            st.wait()

    def finish(ins, outs, scr):
        _, fetch_out, fetch_in, pass_out, pass_in = copies(ins, outs, scr)
        for arrived, onward in zip(fetch_in, pass_out):
            arrived.wait_recv()
            onward.start()
        for cp in pass_in:
            cp.wait_recv()
        for cp in fetch_out + pass_out:
            cp.wait_send()

    return _Ride(list(shards), [jax.ShapeDtypeStruct((N_SHARD,) + s.shape, s.dtype) for s in shards],
                 [pltpu.VMEM(s.shape, s.dtype) for s in shards]
                 + [pltpu.SemaphoreType.DMA((6 * n,)), pltpu.SemaphoreType.DMA((6 * n,)), pltpu.SemaphoreType.DMA((2 * n,))],
                 start, finish)


def _comm(name, ins, out_shape, n_remote, plan):
    n_in, n_out = len(ins), len(out_shape)

    def body(*refs):
        plan(refs[:n_in], refs[n_in:n_in + n_out], *refs[n_in + n_out:])

    return pl.pallas_call(
        body, name=name, in_specs=[_ANY] * n_in, out_specs=[_ANY] * n_out, out_shape=out_shape,
        scratch_shapes=[pltpu.SemaphoreType.DMA((n_remote,)), pltpu.SemaphoreType.DMA((n_remote,))])(*ins)


def _send_other_half(name, grads_bf):
    n = len(grads_bf)

    def plan(ins, outs, ssem, rsem):
        x, y, c = _position()
        started = []
        for a, (src, dst) in enumerate(zip(ins, outs)):
            cp = pltpu.make_async_remote_copy(src.at[1 - c], dst, ssem.at[a], rsem.at[a], device_id=(x, y, 1 - c),
                                              device_id_type=_MESH)
            cp.start()
            started.append(cp)
        for cp in started:
            cp.wait_recv()
        for cp in started:
            cp.wait_send()

    shapes = [jax.ShapeDtypeStruct(s.shape[1:], s.dtype) for s in grads_bf]
    return _comm(name, grads_bf, shapes, n, plan)


def _exchange_ride(pair_sums):
    n = len(pair_sums)

    def copies(ins, outs, scr):
        ssem, rsem = scr
        x, y, c = _position()
        return [pltpu.make_async_remote_copy(src.at[2 * px + py], dst.at[k], ssem.at[3 * a + k], rsem.at[3 * a + k],
                                             device_id=(px, py, c), device_id_type=_MESH)
                for a, (src, dst) in enumerate(zip(ins, outs)) for k, (px, py) in enumerate(_other_chips(x, y))]

    def start(ins, outs, scr):
        for cp in copies(ins, outs, scr):
            cp.start()

    def finish(ins, outs, scr):
        for cp in copies(ins, outs, scr):
            cp.wait_recv()
        for cp in copies(ins, outs, scr):
            cp.wait_send()

    return _Ride(list(pair_sums), [jax.ShapeDtypeStruct((3,) + s.shape[1:], s.dtype) for s in pair_sums],
                 [pltpu.SemaphoreType.DMA((3 * n,)), pltpu.SemaphoreType.DMA((3 * n,))], start, finish)


def _swap_reduced(name, grads):
    n = len(grads)

    def body(*refs):
        ins, outs, ssem, rsem = refs[:n], refs[n:2 * n], refs[2 * n], refs[2 * n + 1]
        x, y, c = _position()
        sent = []
        for a, (src, dst) in enumerate(zip(ins, outs)):
            cp = pltpu.make_async_remote_copy(src.at[c], dst.at[c], ssem.at[a], rsem.at[a], device_id=(x, y, 1 - c),
                                              device_id_type=_MESH)
            cp.start()
            sent.append(cp)
        for a, (src, dst) in enumerate(zip(ins, outs)):
            pltpu.make_async_remote_copy(src.at[1 - c], dst.at[1 - c], ssem.at[a], rsem.at[a], device_id=(x, y, 1 - c),
                                         device_id_type=_MESH).wait_recv()
        for cp in sent:
            cp.wait_send()

    return pl.pallas_call(
        body, name=name, in_specs=[_ANY] * n, out_specs=[_ANY] * n,
        out_shape=[jax.ShapeDtypeStruct(g.shape, g.dtype) for g in grads],
        scratch_shapes=[pltpu.SemaphoreType.DMA((n,)), pltpu.SemaphoreType.DMA((n,))],
        input_output_aliases={a: a for a in range(n)})(*grads)


_FLIPS = [(dx, dy, dc) for dx in (0, 1) for dy in (0, 1) for dc in (0, 1) if dx + dy + dc]
_VMEM = pl.BlockSpec(memory_space=pltpu.VMEM)


def _to_all(src_of, dst, ssem, rsem):
    x, y, c = _position()
    me = 4 * x + 2 * y + c
    peers = [((x + dx) % 2, (y + dy) % 2, (c + dc) % 2) for dx, dy, dc in _FLIPS]
    sent = []
    for k, (px, py, pc) in enumerate(peers):
        cp = pltpu.make_async_remote_copy(src_of(2 * px + py), dst.at[me], ssem.at[k], rsem.at[k],
                                          device_id=(px, py, pc), device_id_type=_MESH)
        cp.start()
        sent.append(cp)
    for k, (px, py, pc) in enumerate(peers):
        pltpu.make_async_remote_copy(src_of(2 * px + py), dst.at[4 * px + 2 * py + pc], ssem.at[k], rsem.at[k],
                                     device_id=(px, py, pc), device_id_type=_MESH).wait_recv()
    for cp in sent:
        cp.wait_send()
    return me, 2 * x + y


def _share_small(name, block, total):
    shape = block.shape

    def body(in_ref, out_ref, *scratch):
        buf, ssem, rsem = (out_ref,) + scratch if not total else scratch
        me, _ = _to_all(lambda chip: in_ref, buf, ssem, rsem)
        buf[me] = in_ref[...]
        if total:
            acc = buf[0]
            for d in range(1, 8):
                acc = acc + buf[d]
            out_ref[...] = acc

    sems = [pltpu.SemaphoreType.DMA((7,)), pltpu.SemaphoreType.DMA((7,))]
    return pl.pallas_call(
        body, name=name, in_specs=[_VMEM], out_specs=_VMEM,
        out_shape=jax.ShapeDtypeStruct(shape if total else (8,) + shape, F32),
        scratch_shapes=([pltpu.VMEM((8,) + shape, F32)] if total else []) + sems,
        compiler_params=pltpu.CompilerParams(vmem_limit_bytes=VMEM_LIMIT_V7X))(block)


def _mod_rows_exchange(mv):
    def body(mv_ref, out_ref, ssem, rsem):
        x, y, c = _position()
        me = 2 * x + y
        out_ref[me] = mv_ref[4 * x + 2 * y + c]
        sent = []
        for k, (px, py) in enumerate(_other_chips(x, y)):
            cp = pltpu.make_async_remote_copy(mv_ref.at[4 * px + 2 * py + c], out_ref.at[me], ssem.at[k], rsem.at[k],
                                              device_id=(px, py, c), device_id_type=_MESH)
            cp.start()
            sent.append(cp)
        for k, (px, py) in enumerate(_other_chips(x, y)):
            pltpu.make_async_remote_copy(mv_ref.at[0], out_ref.at[2 * px + py], ssem.at[k], rsem.at[k],
                                         device_id=(px, py, c), device_id_type=_MESH).wait_recv()
        for cp in sent:
            cp.wait_send()

    return pl.pallas_call(
        body, name="mod_rows_exchange", in_specs=[_VMEM], out_specs=_VMEM,
        out_shape=jax.ShapeDtypeStruct((N_SHARD,) + mv.shape[1:], F32),
        scratch_shapes=[pltpu.SemaphoreType.DMA((3,)), pltpu.SemaphoreType.DMA((3,))],
        compiler_params=pltpu.CompilerParams(vmem_limit_bytes=VMEM_LIMIT_V7X))(mv)


def _mod_grad_exchange(dmj, dm_rows):
    def body(dmj_ref, rows_ref, out_ref, bias_ref, ssem, rsem):
        me, chip = _to_all(lambda j: dmj_ref.at[j], out_ref, ssem, rsem)
        out_ref[me] = dmj_ref[chip]
        for l in range(2):
            bias_ref[l] = _colsum(rows_ref[l])

    return pl.pallas_call(
        body, name="mod_grad_exchange", in_specs=[_VMEM, _VMEM], out_specs=[_VMEM, _VMEM],
        out_shape=[jax.ShapeDtypeStruct((8,) + dmj.shape[1:], F32), jax.ShapeDtypeStruct((2, 1, dm_rows.shape[-1]), F32)],
        scratch_shapes=[pltpu.SemaphoreType.DMA((7,)), pltpu.SemaphoreType.DMA((7,))],
        compiler_params=pltpu.CompilerParams(vmem_limit_bytes=VMEM_LIMIT_V7X))(dmj, dm_rows)


def _pack(arrays):
    flat = jnp.concatenate([a.reshape(-1).astype(F32) for a in arrays])
    total = flat.shape[0]
    rows = -(-total // (8 * PACK_LANES)) * 8
    return jnp.pad(flat, (0, rows * PACK_LANES - total)).reshape(rows, PACK_LANES)


def _unpack(pack, shapes):
    flat, out, pos = pack.reshape(-1), [], 0
    for sh in shapes:
        size = int(np.prod(sh)) if len(sh) else 1
        out.append(flat[pos:pos + size].reshape(sh))
        pos += size
    return out


def _block_diag(pw):
    out = jnp.zeros((POOL_W, POOL_W), pw.dtype)
    for g in range(4):
        out = out.at[g * 64:(g + 1) * 64, g * 64:(g + 1) * 64].set(pw[g])
    return out


def _local_step(x, ctx, small, mod3s, loss_target, comm):
    cfg = _Cfg(x.shape[0], x.shape[1], ctx.shape[1])
    assert cfg.seq % TM == 0 and cfg.nc % TM == 0 and cfg.seq % cfg.ctx == 0 and cfg.ctx % CHUNK == 0
    nb_all, nb_lat = cfg.r // TM, cfg.nl // TM
    last = 1
    wf, big = comm.wf, comm.grads
    cos_t, sin_t = _rope_tables(cfg.seq)
    xs = jnp.concatenate([x.reshape(cfg.nl, D), ctx.reshape(cfg.nc, D)], axis=0)
    row = lambda w: pl.BlockSpec((TM, w), lambda i, j, k: (i, 0))
    mod3_spec = pl.BlockSpec((None, 6, D), lambda i, j, k: (cfg.mod_row(i), 0, 0))
    whole = lambda rows: pl.BlockSpec((None, rows, D), lambda i, j, k: (0, 0, 0))

    def conv_params(l):
        dw = jnp.pad(wf[l]["conv_dw"], ((0, 1), (0, 0)))
        return (dw, small["conv_dw_b"][l][None], small["conv_ln_g"][l][None], small["conv_ln_b"][l][None],
                _block_diag(small["pool_w"][l]).astype(BF16), small["pool_scale"][l][None])

    def residual_epi(ig):
        def epi(acc, ex, outs):
            x_ref, m_ref = ex
            outs[0][...] = x_ref[...] + m_ref[ig:ig + 1, :] * acc
            outs[1][...] = acc.astype(BF16)
        return epi

    def hosted(name, call):
        outs, got = call(comm.ride(name))
        comm.landed(name, got)
        return outs

    saved = []
    for l in range(2):
        nb = nb_lat if l == last else nb_all
        wl = wf[l]
        mod3 = mod3s[l]
        h1, qkv, cpin = _in_proj(f"in_proj{l}", xs, small["norm1_g"][l][None], mod3, wl["w_in"], 0, cos_t, sin_t, cfg)
        mix, lse = hosted(f"att_fwd{l}", lambda ride: _att_fwd(f"att_fwd{l}", qkv, small["attn_sink"][l], l != last, cfg, ride))
        prm = conv_params(l)
        mix, yconv = _convpool_fwd(f"convpool_fwd_lat{l}", cpin, mix, None, prm, cfg.seq, 0, cfg)
        if l != last:
            mix, yconv = _convpool_fwd(f"convpool_fwd_ctx{l}", cpin, mix, yconv, prm, cfg.ctx, cfg.nl // cfg.ctx, cfg)
        x1, y1 = _mm(f"out_proj{l}", "nn", (nb, 1, 1), mix, wl["w_out"].reshape(1, D, D), row(D), whole(D),
                     [jax.ShapeDtypeStruct((cfg.r, D), F32), jax.ShapeDtypeStruct((cfg.r, D), BF16)], [row(D), row(D)],
                     extras=[xs, mod3], extra_specs=[row(D), mod3_spec], epi=residual_epi(2))
        h2 = _norm_fwd(f"norm2_fwd{l}", x1, small["norm2_g"][l][None], mod3, 3, 4, nb, cfg)
        gu, act = hosted(f"ffn_in{l}", lambda ride: _ffn_in(f"ffn_in{l}", h2, wl["w_ffn_in"], 0, nb, cfg, ride))
        x2, y2 = _mm(f"ffn_out{l}", "nn", (nb, 1, 1), act, wl["w_ffn_out"].reshape(1, D_FF, D), row(D_FF), whole(D_FF),
                     [jax.ShapeDtypeStruct((cfg.r, D), F32), jax.ShapeDtypeStruct((cfg.r, D), BF16)], [row(D), row(D)],
                     extras=[x1, mod3], extra_specs=[row(D), mod3_spec], epi=residual_epi(5))
        saved.append(dict(mod3=mod3, x0=xs, h1=h1, qkv=qkv, cpin=cpin, mix=mix, yconv=yconv, lse=lse, y1=y1, x1=x1,
                          h2=h2, gu=gu, act=act, y2=y2, prm=prm))
        xs = x2

    dx, loss, d_final_g, dy2, dg2 = _loss_head(xs, loss_target.reshape(cfg.nl, D), small["final_g"][None],
                                               saved[last]["y2"], saved[last]["mod3"], 5, cfg)

    sg = {k: [None, None] for k in ("norm1_g", "norm2_g", "conv_dw", "conv_dw_b", "conv_ln_g", "conv_ln_b",
                                    "attn_sink", "pool_w", "pool_scale")}
    dms = [None, None]

    def swiglu_bwd_epi(acc, ex, outs):
        outs[0][0] = (acc * ex[0][0].astype(F32)).astype(BF16)
        outs[0][1] = (acc * ex[0][1].astype(F32)).astype(BF16)

    def halves_epi(acc, ex, outs):
        h = acc.shape[0] // 2
        outs[0][0] = acc[:h].astype(BF16)
        outs[0][1] = acc[h:].astype(BF16)

    def row_shards_epi(n):
        def epi(acc, ex, outs):
            s0 = acc.shape[0] // n
            h = s0 // 2
            for t in range(n):
                for half in range(2):
                    outs[0][half, t] = acc[t * s0 + half * h:t * s0 + (half + 1) * h].astype(BF16)
        return epi

    def col_shards_epi(acc, ex, outs):
        h = acc.shape[0] // 2
        for j in range(N_SHARD):
            for half in range(2):
                outs[0][half, j] = acc[half * h:(half + 1) * h, j * IN_SHARD:(j + 1) * IN_SHARD].astype(BF16)

    for l in (1, 0):
        sv = saved[l]
        mod3 = sv["mod3"]
        nb = nb_lat if l == last else nb_all
        tr = _dw_rows(nb * TM)
        steps = nb * TM // tr
        wl = wf[l]
        gu_spec = pl.BlockSpec((2, TM, HALF_FF), lambda j, i, k: (0, i, j))
        df = _mm(f"ffn_out_bwd{l}", "nt", (2, nb, 1), dy2, wl["w_ffn_out"].reshape(1, D_FF, D),
                 pl.BlockSpec((TM, D), lambda j, i, k: (i, 0)), pl.BlockSpec((None, HALF_FF, D), lambda j, i, k: (0, j, 0)),
                 [jax.ShapeDtypeStruct((2, cfg.r, D_FF), BF16)], [gu_spec], extras=[sv["gu"]], extra_specs=[gu_spec],
                 epi=swiglu_bwd_epi)[0]
        big[l]["w_ffn_out"] = _mm_dw(
            f"dw_ffn_out{l}", sv["act"], dy2, pl.BlockSpec((tr, HALF_FF), lambda i, j, k: (k, j)),
            pl.BlockSpec((tr, D), lambda i, j, k: (k, 0)), (2, N_SHARD, D_FF // 8, D),
            pl.BlockSpec((2, 2, D_FF // 8, D), lambda i, j, k: (0, j, 0, 0)), 2, (HALF_FF, D), steps, row_shards_epi(2))[0]
        ride = comm.ride(f"dw_ffn_in{l}")
        res = _mm_dw(f"dw_ffn_in{l}", sv["h2"], df, pl.BlockSpec((tr, D), lambda i, j, k: (k, 0)),
                     pl.BlockSpec((None, tr, HALF_FF), lambda i, j, k: (j // 2, k, j % 2)), (2, N_SHARD, D // 2, HALF_FF),
                     pl.BlockSpec((2, None, D // 2, HALF_FF), lambda i, j, k: (0, j, 0, 0)), N_SHARD, (D, HALF_FF), steps,
                     halves_epi, ride=ride)
        res, got = res if ride is not None else (res, [])
        big[l]["w_ffn_in"] = res[0]
        comm.landed(f"dw_ffn_in{l}", got)
        dx1, dsh2, dsc2, dn2, dy1, dg1 = _proj_norm_bwd(
            f"ffn_in_bwd{l}", df, pl.BlockSpec((2, TM, D_FF), lambda i: (0, i, 0)),
            lambda a_ref, j: a_ref[j // 2, :, (j % 2) * HALF_FF:(j % 2 + 1) * HALF_FF], wl["w_ffn_in"],
            sv["x1"], dx, small["norm2_g"][l][None], mod3, 4, nb, False, (sv["y1"], mod3, 2), cfg)
        dmix = _mm(f"out_proj_bwd{l}", "nt", (nb, 1, 1), dy1, wl["w_out"].reshape(1, D, D), row(D), whole(D),
                   [jax.ShapeDtypeStruct((cfg.r, D), F32)], [row(D)])[0]
        big[l]["w_out"] = _mm_dw(
            f"dw_out{l}", sv["mix"], dy1, pl.BlockSpec((tr, D), lambda i, j, k: (k, 0)),
            pl.BlockSpec((tr, D), lambda i, j, k: (k, 0)), (2, N_SHARD, D // 8, D),
            pl.BlockSpec((2, N_SHARD, D // 8, D), lambda i, j, k: (0, 0, 0, 0)), 1, (D, D), steps, row_shards_epi(N_SHARD))[0]
        du, dkvl, dkvc, dsink = hosted(f"att_bwd{l}", lambda ride: _att_bwd(
            f"att_bwd{l}", sv["qkv"], sv["mix"], dmix, sv["lse"], small["attn_sink"][l], cos_t, sin_t, l != last, cfg, ride))
        acc = [du] + [jnp.zeros(sh, F32) for sh in _SMALL_SHAPES]
        acc = _convpool_bwd(f"convpool_bwd_lat{l}", sv["cpin"], sv["yconv"], dmix, sv["prm"], acc, cfg.seq, 0, cfg)
        if l != last:
            acc = _convpool_bwd(f"convpool_bwd_ctx{l}", sv["cpin"], sv["yconv"], dmix, sv["prm"], acc, cfg.ctx,
                                cfg.nl // cfg.ctx, cfg)
        du, g_dw, g_dwb, g_lng, g_lnb, g_wbd, g_ps = acc
        du = _place_kv(f"place_kv{l}", du, dkvl, dkvc, l != last, cfg)
        if l == last:
            du = _place_ctx_kv_only(f"place_ctx_kv{l}", du, dkvc, cfg)
        sg["attn_sink"][l] = dsink[:, 0]
        sg["conv_dw"][l], sg["conv_dw_b"][l], sg["conv_ln_g"][l], sg["conv_ln_b"][l] = g_dw[:CONV_K], g_dwb[0], g_lng[0], g_lnb[0]
        sg["pool_w"][l] = jnp.stack([g_wbd[g * 64:(g + 1) * 64, g * 64:(g + 1) * 64] for g in range(4)])
        sg["pool_scale"][l] = g_ps[0]
        tr_all = _dw_rows(cfg.r)
        big[l]["w_in"] = _mm_dw(
            f"dw_in{l}", sv["h1"], du, pl.BlockSpec((tr_all, D), lambda i, j, k: (k, 0)),
            pl.BlockSpec((tr_all, IN_W), lambda i, j, k: (k, 0)), (2, N_SHARD, D // 2, IN_SHARD),
            pl.BlockSpec((2, N_SHARD, D // 2, IN_SHARD), lambda i, j, k: (0, 0, 0, 0)), 1, (D, IN_W), cfg.r // tr_all,
            col_shards_epi)[0]
        below = (saved[l - 1]["y2"], saved[l - 1]["mod3"], 5) if l > 0 else None
        res = _proj_norm_bwd(
            f"in_proj_bwd{l}", du, pl.BlockSpec((TM, IN_W), lambda i: (i, 0)),
            lambda a_ref, j: a_ref[:, j * IN_SHARD:(j + 1) * IN_SHARD], wl["w_in"],
            sv["x0"], dx1, small["norm1_g"][l][None], mod3, 1, nb_all, l == last, below, cfg, dx_latent_only=l == 0)
        dx, dsh1, dsc1, dn1 = res[:4]
        sg["norm1_g"][l], sg["norm2_g"][l] = dn1[0], dn2[0]
        parts = [dsh1, dsc1, dg1, dsh2, dsc2, dg2]
        if below is not None:
            dy2, dg2 = res[4:]
        dm = jnp.concatenate([t[:cfg.b, 0, :] for t in parts], axis=1)
        live = (0, 1) if l == last else range(6)
        dm_ctx = jnp.concatenate([t[cfg.b, 0, :] if i in live else jnp.zeros((D,), F32) for i, t in enumerate(parts)])
        dms[l] = jnp.concatenate([dm, dm_ctx[None, :], jnp.zeros((MOD_ROWS - cfg.b - 1, 6 * D), F32)], axis=0)

    grad_x = dx.reshape(x.shape)
    small_grads = {k: jnp.stack(v) for k, v in sg.items()}
    small_grads["final_g"] = d_final_g[0]
    return loss, grad_x, small_grads, dms


_BIG = ("w_in", "w_out", "w_ffn_in", "w_ffn_out")
_TAPS = "conv_dw"
_SMALL = ("c_ctx", "b_mod", "norm1_g", "norm2_g", "conv_dw", "conv_dw_b", "conv_ln_g", "conv_ln_b", "attn_sink",
          "pool_w", "pool_scale", "final_g")
_ORDER = ("c_ctx", "w_mod", "b_mod", "norm1_g", "norm2_g", "w_in", "conv_dw", "conv_dw_b", "conv_ln_g", "conv_ln_b",
          "attn_sink", "pool_w", "pool_scale", "w_out", "w_ffn_in", "w_ffn_out", "final_g")
_GATHER_HOSTS = {
    "first": ((0, "w_in"),),
    "att_fwd0": ((0, "w_out"), (0, "w_ffn_in"), (0, "w_ffn_out"), (0, _TAPS)),
    "ffn_in0": tuple((1, k) for k in _BIG + (_TAPS,)),
}
_REDUCE_HOSTS = {
    "dw_ffn_in0": tuple((1, k) for k in _BIG),
    "att_bwd0": ((0, "w_ffn_in"), (0, "w_ffn_out"), (0, "w_out")),
    "last": ((0, "w_in"),),
}


class _Comm:
    def __init__(self, w, c_idx, jc_idx):
        self.shapes = {k: w[k].shape[1:] for k in _BIG}
        self.c_idx, self.jc_idx = c_idx, jc_idx
        halves = lambda a: a.reshape(2, a.shape[0] // 2, a.shape[1])
        taps = jnp.pad(w[_TAPS], ((0, 0), (0, 1), (0, 64)))
        cast = {k: w[k].astype(BF16) for k in _BIG}
        self.shards = [{**{k: halves(cast[k][l]) for k in _BIG}, _TAPS: halves(taps[l])} for l in range(2)]
        self.wf = [dict(), dict()]
        self.grads = [dict(), dict()]
        self.reduced = [dict(), dict()]
        self._open = {}
        self.landed("first", _run_ride("gather_first", self.ride("first")))

    def ride(self, host):
        if host in _GATHER_HOSTS:
            return _gather_ride([self.shards[layer][k] for layer, k in _GATHER_HOSTS[host]])
        if host in _REDUCE_HOSTS:
            what = _REDUCE_HOSTS[host]
            mine = [self.grads[layer][k] for layer, k in what]
            other = _send_other_half(f"send_other_half_{host}", mine)
            pair = [_pair_add(f"pair_add{layer}_{k}", a, b, self.c_idx) for (layer, k), a, b in zip(what, mine, other)]
            self._open[host] = pair
            return _exchange_ride(pair)
        return None

    def landed(self, host, got):
        if host in _GATHER_HOSTS:
            for (layer, k), f in zip(_GATHER_HOSTS[host], got):
                if k == _TAPS:
                    taps = f.reshape(N_SHARD, 32, 128)[:, :CONV_K, :64]
                    self.wf[layer][k] = jnp.transpose(taps, (1, 0, 2)).reshape(CONV_K, CONV_W)
                else:
                    self.wf[layer][k] = f.reshape((1, N_SHARD) + self.shapes[k])
        if host in _REDUCE_HOSTS:
            what = _REDUCE_HOSTS[host]
            mine = [_shard_sum(f"shard_sum{layer}_{k}", a, b, self.jc_idx)
                    for (layer, k), a, b in zip(what, self._open.pop(host), got)]
            for (layer, k), g in zip(what, _swap_reduced(f"swap_reduced_{host}", mine)):
                self.reduced[layer][k] = g.reshape(self.shapes[k])


def _conditioning(c, c_ctx, w_mod, b_mod, chip):
    b = c.shape[0]
    block = jnp.concatenate([c, c_ctx[None, :], jnp.zeros((8 - b - 1, D), F32)], axis=0)
    c_all = _share_small("share_c", block, False).reshape(64, D)
    bias = lax.dynamic_slice_in_dim(b_mod, chip * MOD_W, MOD_W, axis=1)
    full = lambda r, q: pl.BlockSpec((r, q), lambda i, j, k: (0, 0))

    def bias_epi(acc, ex, outs):
        outs[0][...] = acc + ex[0][...]

    mv = [_mm(f"mod_fwd{l}", "nn", (1, 1, 1), c_all, w_mod[l], full(64, D), full(D, MOD_W),
              [jax.ShapeDtypeStruct((64, MOD_W), F32)], [full(64, MOD_W)], extras=[bias[l][None]],
              extra_specs=[full(1, MOD_W)], a_fn=_silu, epi=bias_epi)[0] for l in range(2)]
    by_dev = jnp.transpose(jnp.stack(mv).reshape(2, 8, 8, MOD_W), (1, 0, 2, 3))
    rows = jnp.transpose(_mod_rows_exchange(by_dev), (1, 2, 0, 3)).reshape(2, 8, 6 * D)
    rows = jnp.pad(rows, ((0, 0), (0, MOD_ROWS - 8), (0, 0)))
    return [rows[l].reshape(MOD_ROWS, 6, D) for l in range(2)], c_all


def _conditioning_bwd(dms, c_all, w_mod, b):
    dm = jnp.stack([d[:8] for d in dms])
    by_chip = jnp.transpose(dm.reshape(2, 8, N_SHARD, MOD_W), (2, 0, 1, 3))
    gathered, d_bias = _mod_grad_exchange(by_chip, dm)
    dm_all = jnp.transpose(gathered, (1, 0, 2, 3)).reshape(2, 64, MOD_W)
    full = lambda r, q: pl.BlockSpec((r, q), lambda i, j, k: (0, 0))

    def ctx_rows_epi(acc, ex, outs):
        row = lax.broadcasted_iota(jnp.int32, acc.shape, 0) % 8
        outs[0][...] = _colsum(jnp.where(row == b, acc * _dsilu(ex[0][...]), 0.0))

    g_mod, d_ctx = [], jnp.zeros((D,), F32)
    for l in range(2):
        g_mod.append(_mm(f"dw_mod{l}", "tn", (1, 1, 1), c_all, dm_all[l], full(64, D), full(64, MOD_W),
                         [jax.ShapeDtypeStruct((D, MOD_W), F32)], [full(D, MOD_W)], a_fn=_silu)[0])
        part = _mm(f"mod_bwd{l}", "nt", (1, 1, 1), dm_all[l], w_mod[l], full(64, MOD_W), full(D, MOD_W),
                   [jax.ShapeDtypeStruct((1, D), F32)], [full(1, D)], extras=[c_all], extra_specs=[full(64, D)],
                   epi=ctx_rows_epi)[0]
        d_ctx = d_ctx + part[0]
    return g_mod, d_bias[:, 0, :], d_ctx


def kernel(x, c, ctx, c_ctx, w_mod, b_mod, norm1_g, norm2_g, w_in, conv_dw, conv_dw_b, conv_ln_g, conv_ln_b, attn_sink, pool_w, pool_scale, w_out, w_ffn_in, w_ffn_out, final_g, loss_target, m_c_ctx, m_w_mod, m_b_mod, m_norm1_g, m_norm2_g, m_w_in, m_conv_dw, m_conv_dw_b, m_conv_ln_g, m_conv_ln_b, m_attn_sink, m_pool_w, m_pool_scale, m_w_out, m_w_ffn_in, m_w_ffn_out, m_final_g, v_c_ctx, v_w_mod, v_b_mod, v_norm1_g, v_norm2_g, v_w_in, v_conv_dw, v_conv_dw_b, v_conv_ln_g, v_conv_ln_b, v_attn_sink, v_pool_w, v_pool_scale, v_w_out, v_w_ffn_in, v_w_ffn_out, v_final_g):
    w = dict(c_ctx=c_ctx, w_mod=w_mod, b_mod=b_mod, norm1_g=norm1_g, norm2_g=norm2_g, w_in=w_in, conv_dw=conv_dw,
             conv_dw_b=conv_dw_b, conv_ln_g=conv_ln_g, conv_ln_b=conv_ln_b, attn_sink=attn_sink, pool_w=pool_w,
             pool_scale=pool_scale, w_out=w_out, w_ffn_in=w_ffn_in, w_ffn_out=w_ffn_out, final_g=final_g)
    m = dict(c_ctx=m_c_ctx, w_mod=m_w_mod, b_mod=m_b_mod, norm1_g=m_norm1_g, norm2_g=m_norm2_g, w_in=m_w_in,
             conv_dw=m_conv_dw, conv_dw_b=m_conv_dw_b, conv_ln_g=m_conv_ln_g, conv_ln_b=m_conv_ln_b,
             attn_sink=m_attn_sink, pool_w=m_pool_w, pool_scale=m_pool_scale, w_out=m_w_out, w_ffn_in=m_w_ffn_in,
             w_ffn_out=m_w_ffn_out, final_g=m_final_g)
    v = dict(c_ctx=v_c_ctx, w_mod=v_w_mod, b_mod=v_b_mod, norm1_g=v_norm1_g, norm2_g=v_norm2_g, w_in=v_w_in,
             conv_dw=v_conv_dw, conv_dw_b=v_conv_dw_b, conv_ln_g=v_conv_ln_g, conv_ln_b=v_conv_ln_b,
             attn_sink=v_attn_sink, pool_w=v_pool_w, pool_scale=v_pool_scale, w_out=v_w_out, w_ffn_in=v_w_ffn_in,
             w_ffn_out=v_w_ffn_out, final_g=v_final_g)
    xi, yi, ci = _position()
    chip = 2 * xi + yi
    mod3s, c_all = _conditioning(c, c_ctx, w_mod, b_mod, chip)
    comm = _Comm(w, jnp.reshape(ci, (1,)).astype(jnp.int32), jnp.stack([chip, ci]).astype(jnp.int32))
    small = {k: w[k] for k in _SMALL if k not in ("conv_dw", "c_ctx", "b_mod")}
    loss, grad_x, sgrads, dms = _local_step(x, ctx, small, mod3s, loss_target, comm)
    comm.landed("last", _run_ride("exchange_last", comm.ride("last")))
    g_mod, sgrads["b_mod"], d_ctx = _conditioning_bwd(dms, c_all, w_mod, c.shape[0])
    sgrads["c_ctx"] = 0.5 * d_ctx

    names = list(_SMALL)
    total = _share_small("sum_small", _pack([loss] + [sgrads[k] for k in names]), True)
    parts = _unpack(total, [()] + [sgrads[k].shape for k in names])
    loss_out = parts[0]
    gsmall = dict(zip(names, parts[1:]))
    gsmall["conv_dw"] = lax.dynamic_slice_in_dim(gsmall["conv_dw"], chip * 64, 64, axis=2)

    grads, delta, new_m, new_v = dict(gsmall), {}, {}, {}
    for k in ("w_mod",) + _BIG:
        s0, s1 = w[k].shape[1:]
        flat = lambda a: a.reshape(2 * s0, s1)
        g_layers = g_mod if k == "w_mod" else [comm.reduced[l][k] for l in range(2)]
        outs = _adamw_layers(f"adamw_{k}", flat(w[k]), g_layers, flat(m[k]), flat(v[k]))
        grads[k], delta[k], new_m[k], new_v[k] = [a.reshape(w[k].shape) for a in outs]
    d_, m_, v_ = _adamw("adamw_small", _pack([w[k] for k in names]), _pack([gsmall[k] for k in names]),
                        _pack([m[k] for k in names]), _pack([v[k] for k in names]))
    sshapes = [w[k].shape for k in names]
    for k, a, b, e in zip(names, _unpack(d_, sshapes), _unpack(m_, sshapes), _unpack(v_, sshapes)):
        delta[k], new_m[k], new_v[k] = a, b, e
    return (loss_out, grad_x, *[grads[k] for k in _ORDER], *[delta[k] for k in _ORDER],
            *[new_m[k] for k in _ORDER], *[new_v[k] for k in _ORDER])
```

```python
from typing import NamedTuple

import jax
import jax.numpy as jnp
import numpy as np
from jax import lax
from jax.experimental import pallas as pl
from jax.experimental.pallas import tpu as pltpu

F32 = jnp.float32
BF16 = jnp.bfloat16

D = 1024
GRID_W = 64
HEAD_DIM = 64
N_HEADS = 8
ATTN_W = 512
CONV_W = 256
POOL_W = 256
IN_W = 1536
D_FF = 2816
CONV_K = 31
QB = 128
ROPE_BASE = 10000.0
EPS = 1e-6
NEG = -1e30
N_SHARD = 4
IN_SHARD = IN_W // N_SHARD
HALF_FF = D_FF // 2
MOD_W = 6 * D // N_SHARD
MOD_ROWS = 16
PACK_LANES = 128

ADAM_LR = 0.001
ADAM_B1 = 0.9
ADAM_B2 = 0.999
ADAM_EPS = 1e-08
ADAM_WD = 0.01
ADAM_STEP = 10

VMEM_LIMIT_V7X = 56 * 1024 * 1024
TM = 512
TR_MAX = 1024
CHUNK = 256
PAD = 16

_MESH = pl.DeviceIdType.MESH
_ANY = pl.BlockSpec(memory_space=pl.ANY)
_DIMS = {"nn": (((1,), (0,)), ((), ())), "nt": (((1,), (1,)), ((), ())), "tn": (((0,), (0,)), ((), ()))}


class _Cfg(NamedTuple):
    b: int
    seq: int
    ctx: int

    @property
    def nl(self):
        return self.b * self.seq

    @property
    def nc(self):
        return self.b * self.ctx

    @property
    def r(self):
        return self.nl + self.nc

    def mod_row(self, i):
        return jnp.where(i < self.nl // TM, i // (self.seq // TM), self.b)

    def first_of_row(self, i):
        nlb = self.nl // TM
        return jnp.logical_or(jnp.logical_and(i < nlb, i % (self.seq // TM) == 0), i == nlb)


def _params(n_grid=0):
    sem = ("arbitrary",) * n_grid if n_grid else None
    return pltpu.CompilerParams(dimension_semantics=sem, vmem_limit_bytes=VMEM_LIMIT_V7X)


def _dot(a, b, mode="nn"):
    return lax.dot_general(a.astype(BF16), b.astype(BF16), _DIMS[mode], preferred_element_type=F32)


def _sigmoid(x):
    return 1.0 / (1.0 + jnp.exp(-x))


def _silu(x):
    return x * _sigmoid(x)


def _dsilu(x):
    s = _sigmoid(x)
    return s * (1.0 + x * (1.0 - s))


def _colsum(v):
    return jnp.sum(v, axis=0, keepdims=True)


def _dw_rows(rows):
    return TR_MAX if rows % TR_MAX == 0 else TM


def _epi_store(acc, ex, outs):
    for o in outs:
        o[...] = acc.astype(o.dtype)


class _Ride(NamedTuple):
    ins: list
    out_shape: list
    scratch: list
    start: object
    finish: object


class _Hosted(NamedTuple):
    ride: _Ride
    n_in: int
    n_out: int
    grid: tuple

    def split(self, refs):
        n_ri, n_ro, n_rs = len(self.ride.ins), len(self.ride.out_shape), len(self.ride.scratch)
        r_in = refs[self.n_in:self.n_in + n_ri]
        r_out = refs[self.n_in + n_ri + self.n_out:self.n_in + n_ri + self.n_out + n_ro]
        own = refs[:self.n_in] + refs[self.n_in + n_ri:self.n_in + n_ri + self.n_out] + \
            refs[self.n_in + n_ri + self.n_out + n_ro:len(refs) - n_rs]
        return own, (r_in, r_out, refs[len(refs) - n_rs:])

    def start(self, parts):
        ids = [pl.program_id(d) for d in range(len(self.grid))]
        first = ids[0] == 0
        for i in ids[1:]:
            first = jnp.logical_and(first, i == 0)
        pl.when(first)(lambda: self.ride.start(*parts))

    def finish(self, parts):
        ids = [pl.program_id(d) for d in range(len(self.grid))]
        last = ids[0] == self.grid[0] - 1
        for i, g in zip(ids[1:], self.grid[1:]):
            last = jnp.logical_and(last, i == g - 1)
        pl.when(last)(lambda: self.ride.finish(*parts))


def _hosted_call(body, ride, name, grid, ins, in_specs, out_shape, out_specs, scratch, params):
    if ride is None:
        res = pl.pallas_call(body, name=name, grid=grid, in_specs=in_specs, out_specs=out_specs, out_shape=out_shape,
                             scratch_shapes=scratch, compiler_params=params)(*ins)
        return list(res), []
    host = _Hosted(ride, len(ins), len(out_shape), tuple(grid))

    def hosted(*refs):
        own, parts = host.split(refs)
        host.start(parts)
        body(*own)
        host.finish(parts)

    res = pl.pallas_call(
        hosted, name=name, grid=grid, in_specs=list(in_specs) + [_ANY] * len(ride.ins),
        out_specs=list(out_specs) + [_ANY] * len(ride.out_shape), out_shape=list(out_shape) + list(ride.out_shape),
        scratch_shapes=list(scratch) + list(ride.scratch), compiler_params=params)(*ins, *ride.ins)
    return list(res[:len(out_shape)]), list(res[len(out_shape):])


def _mm(name, mode, grid, a, b, a_spec, b_spec, out_shape, out_specs, acc_shape=None, extras=(),
        extra_specs=(), a_fn=None, epi=_epi_store, ride=None):
    nk = grid[2]
    n_ex, n_out = len(extras), len(out_shape)

    def body(*refs):
        a_ref, b_ref = refs[:2]
        ex = refs[2:2 + n_ex]
        outs = refs[2 + n_ex:2 + n_ex + n_out]
        av = a_ref[...]
        if a_fn is not None:
            av = a_fn(av)
        part = _dot(av, b_ref[...], mode)
        if nk == 1:
            epi(part, ex, outs)
        else:
            acc = refs[-1]
            k = pl.program_id(2)

            @pl.when(k == 0)
            def _():
                acc[...] = part

            @pl.when(k > 0)
            def _():
                acc[...] += part

            @pl.when(k == nk - 1)
            def _():
                epi(acc[...], ex, outs)

    scratch = [] if nk == 1 else [pltpu.VMEM(acc_shape, F32)]
    outs, ride_outs = _hosted_call(body, ride, name, grid, [a, b, *extras], [a_spec, b_spec, *extra_specs],
                                   list(out_shape), list(out_specs), scratch, _params(3))
    return outs if ride is None else (outs, ride_outs)


def _mm_dw(name, a, b, a_spec, b_spec, out_shape, out_spec, n_out_blocks, acc_shape, n_steps, epi, a_fn=None,
           extras=(), extra_specs=(), extra_out=(), extra_out_specs=(), ride=None):
    grid = (1, n_out_blocks, n_steps)
    outs = [jax.ShapeDtypeStruct(out_shape, BF16)] + list(extra_out)
    return _mm(name, "tn", grid, a, b, a_spec, b_spec, outs, [out_spec, *extra_out_specs], acc_shape, extras,
               extra_specs, a_fn, epi, ride)


def _gate_step(i, dxv, y_ref, m_ref, ig, dy_ref, dgate_ref, cfg):
    dy_ref[...] = (dxv * m_ref[ig:ig + 1, :]).astype(BF16)
    _accumulate_rows(cfg.first_of_row(i), dgate_ref, _colsum(dxv * y_ref[...].astype(F32)))


def _proj_norm_bwd(name, a, a_spec, pick, w, x, dres, gvec, mod3, isc, nblk, res_latent_only, gate, cfg,
                   dx_latent_only=False):
    ns = w.shape[-1]
    nlb = cfg.nl // TM

    def body(a_ref, w_ref, x_ref, dres_ref, g_ref, m_ref, *rest):
        if gate is None:
            dx_ref, dsh_ref, dsc_ref, dg_ref = rest
        else:
            y_ref, gm_ref, dx_ref, dsh_ref, dsc_ref, dg_ref, dy_ref, dgate_ref = rest
        i = pl.program_id(0)
        dhv = _dot(pick(a_ref, 0), w_ref[0], "nt")
        for j in range(1, N_SHARD):
            dhv = dhv + _dot(pick(a_ref, j), w_ref[j], "nt")
        xv = x_ref[...]
        r = lax.rsqrt(jnp.mean(xv * xv, axis=-1, keepdims=True) + EPS)
        xh = xv * r
        g = g_ref[...]
        sc1 = 1.0 + m_ref[isc:isc + 1, :]
        t = dhv * xh
        first = cfg.first_of_row(i)
        _accumulate_rows(first, dsh_ref, _colsum(dhv))
        _accumulate_rows(first, dsc_ref, _colsum(t * g))
        _accumulate_rows(i == 0, dg_ref, _colsum(t * sc1))
        dxh = dhv * (g * sc1)
        dxn = r * (dxh - xh * jnp.mean(dxh * xh, axis=-1, keepdims=True))
        dxv = (jnp.where(i < nlb, dres_ref[...], 0.0) if res_latent_only else dres_ref[...]) + dxn
        if dx_latent_only:
            @pl.when(i < nlb)
            def _():
                dx_ref[...] = dxv
        else:
            dx_ref[...] = dxv
        if gate is not None:
            _gate_step(i, dxv, y_ref, gm_ref, gate[2], dy_ref, dgate_ref, cfg)

    row = pl.BlockSpec((TM, D), lambda i: (i, 0))
    vec = pl.BlockSpec((1, D), lambda i: (0, 0))
    part = pl.BlockSpec((None, 1, D), lambda i: (cfg.mod_row(i), 0, 0))
    part_shape = jax.ShapeDtypeStruct((MOD_ROWS, 1, D), F32)
    resident = pl.BlockSpec((None, N_SHARD, D, ns), lambda i: (0, 0, 0, 0), pipeline_mode=pl.Buffered(1))
    ins, in_specs = [a, w, x, dres, gvec, mod3], [a_spec, resident, row, row, vec, _mod_spec(cfg)]
    out_specs = [row, part, part, vec]
    out_shape = [jax.ShapeDtypeStruct((cfg.r, D), F32), part_shape, part_shape, jax.ShapeDtypeStruct((1, D), F32)]
    if dx_latent_only:
        out_specs[0] = pl.BlockSpec((TM, D), lambda i: (jnp.minimum(i, nlb - 1), 0))
        out_shape[0] = jax.ShapeDtypeStruct((cfg.nl, D), F32)
    if gate is not None:
        ins, in_specs = ins + list(gate[:2]), in_specs + [row, _mod_spec(cfg)]
        out_specs, out_shape = out_specs + [row, part], out_shape + [jax.ShapeDtypeStruct((cfg.r, D), BF16), part_shape]
    return pl.pallas_call(body, name=name, grid=(nblk,), in_specs=in_specs, out_specs=out_specs, out_shape=out_shape,
                          compiler_params=_params(1))(*ins)


def _mod_spec(cfg):
    return pl.BlockSpec((None, 6, D), lambda i: (cfg.mod_row(i), 0, 0))


def _norm_mod(xv, g, m_ref, ish, isc):
    r = lax.rsqrt(jnp.mean(xv * xv, axis=-1, keepdims=True) + EPS)
    return (xv * r * g * (1.0 + m_ref[isc:isc + 1, :]) + m_ref[ish:ish + 1, :]).astype(BF16)


def _accumulate_rows(first, ref, val):
    @pl.when(first)
    def _():
        ref[...] = val

    @pl.when(jnp.logical_not(first))
    def _():
        ref[...] += val


def _loss_head(x, target, gvec, y, mod3, ig, cfg):
    def body(x_ref, t_ref, g_ref, y_ref, m_ref, dx_ref, loss_ref, dg_ref, dy_ref, dgate_ref):
        i = pl.program_id(0)

        @pl.when(i == 0)
        def _():
            loss_ref[...] = jnp.zeros_like(loss_ref)
            dg_ref[...] = jnp.zeros_like(dg_ref)

        xv = x_ref[...]
        g = g_ref[...]
        r = lax.rsqrt(jnp.mean(xv * xv, axis=-1, keepdims=True) + EPS)
        xh = xv * r
        err = xh * g - t_ref[...]
        loss_ref[...] += (0.5 / D) * _colsum(jnp.sum(err * err, axis=-1, keepdims=True))
        dy = err * (1.0 / D)
        dg_ref[...] += _colsum(dy * xh)
        dxh = dy * g
        dxv = r * (dxh - xh * jnp.mean(dxh * xh, axis=-1, keepdims=True))
        dx_ref[...] = dxv
        _gate_step(i, dxv, y_ref, m_ref, ig, dy_ref, dgate_ref, cfg)

    row = pl.BlockSpec((TM, D), lambda i: (i, 0))
    vec = pl.BlockSpec((1, D), lambda i: (0, 0))
    part = pl.BlockSpec((None, 1, D), lambda i: (cfg.mod_row(i), 0, 0))
    return pl.pallas_call(
        body, name="loss_head", grid=(cfg.nl // TM,), in_specs=[row, row, vec, row, _mod_spec(cfg)],
        out_specs=[row, pl.BlockSpec((1, 1), lambda i: (0, 0)), vec, row, part],
        out_shape=[jax.ShapeDtypeStruct((cfg.r, D), F32), jax.ShapeDtypeStruct((1, 1), F32),
                   jax.ShapeDtypeStruct((1, D), F32), jax.ShapeDtypeStruct((cfg.r, D), BF16),
                   jax.ShapeDtypeStruct((MOD_ROWS, 1, D), F32)],
        compiler_params=_params(1))(x, target, gvec, y, mod3)


def _rope_tables(seq):
    rows = seq // GRID_W
    row = jnp.repeat(jnp.arange(rows), GRID_W).astype(F32)
    col = jnp.tile(jnp.arange(GRID_W), rows).astype(F32)
    half = HEAD_DIM // 2
    inv = ROPE_BASE ** (-jnp.arange(0, half, 2, dtype=F32) / half)
    ar, ac = row[:, None] * inv, col[:, None] * inv
    ang = jnp.concatenate([ar, ar, ac, ac], axis=-1)
    sign = jnp.tile(jnp.concatenate([-jnp.ones((16,), F32), jnp.ones((16,), F32)]), 2)
    cos = jnp.tile(jnp.cos(ang), (1, 2))
    sin = jnp.tile(jnp.sin(ang) * sign, (1, 2))
    cos = jnp.concatenate([cos, jnp.ones((TM, 2 * HEAD_DIM), F32)], axis=0)
    sin = jnp.concatenate([sin, jnp.zeros((TM, 2 * HEAD_DIM), F32)], axis=0)
    return cos, sin


def _rope(x, cos, sin_signed, sign):
    lane = lax.broadcasted_iota(jnp.int32, x.shape, 1)
    low = (lane % 32) < 16
    rot = jnp.where(low, pltpu.roll(x, 112, 1), pltpu.roll(x, 16, 1))
    return x * cos + sign * (rot * sin_signed)


def _in_proj(name, x, gvec, mod3, w_in, layer, cos_t, sin_t, cfg):
    nlb, bps = cfg.nl // TM, cfg.seq // TM

    def body(x_ref, g_ref, m_ref, w_ref, cos_ref, sin_ref, h_ref, qkv_ref, cp_ref):
        hv = _norm_mod(x_ref[...], g_ref[...], m_ref, 0, 1)
        h_ref[...] = hv
        u = jnp.concatenate([_dot(hv, w_ref[j]) for j in range(N_SHARD)], axis=1)
        cos, sin = cos_ref[...], sin_ref[...]
        tiles = []
        for t in range(5):
            y = _rope(u[:, 128 * t:128 * (t + 1)], cos, sin, 1.0)
            tiles.append(y * (HEAD_DIM ** -0.5) if t < 4 else y)
        tiles.append(u[:, 640:768])
        qkv_ref[...] = jnp.concatenate(tiles, axis=1).astype(BF16)
        cp_ref[...] = u[:, 768:IN_W].astype(BF16)

    tab = pl.BlockSpec((TM, 128), lambda i: (jnp.where(i < nlb, i % bps, bps), 0))
    half = pl.BlockSpec((TM, 768), lambda i: (i, 0))
    row = pl.BlockSpec((TM, D), lambda i: (i, 0))
    return pl.pallas_call(
        body, name=name, grid=(cfg.r // TM,),
        in_specs=[row, pl.BlockSpec((1, D), lambda i: (0, 0)), _mod_spec(cfg),
                  pl.BlockSpec((None, N_SHARD, D, IN_SHARD), lambda i: (layer, 0, 0, 0)), tab, tab],
        out_specs=[row, half, half],
        out_shape=[jax.ShapeDtypeStruct((cfg.r, D), BF16), jax.ShapeDtypeStruct((cfg.r, 768), BF16),
                   jax.ShapeDtypeStruct((cfg.r, 768), BF16)],
        compiler_params=_params(1))(x, gvec, mod3, w_in, cos_t, sin_t)


def _att_specs(cfg):
    nlb, ncb = cfg.seq // QB, cfg.ctx // QB

    def qblk(s, qb):
        return jnp.where(qb < nlb, s * nlb + qb, cfg.nl // QB + s * ncb + qb - nlb)

    def near(off, col):
        return pl.BlockSpec((QB, 128), lambda s, qb: (s * nlb + jnp.clip(qb + off, 0, nlb - 1), col))

    def ctxs(col):
        return pl.BlockSpec((cfg.ctx, 128), lambda s, qb: (cfg.nl // cfg.ctx + s, col))

    qspec = pl.BlockSpec((QB, ATTN_W), lambda s, qb: (qblk(s, qb), 0))
    kv = [ctxs(4), ctxs(5), near(-1, 4), near(0, 4), near(1, 4), near(-1, 5), near(0, 5), near(1, 5)]
    return qblk, qspec, kv


def _att_scores(qb, nlb, sink_ref, q_ref, k_refs, v_refs, kh):
    is_lat = qb < nlb
    ii = lax.broadcasted_iota(jnp.int32, (4 * QB, 3 * QB), 0) % QB
    col = lax.broadcasted_iota(jnp.int32, (4 * QB, 3 * QB), 1)
    jj, blk = col % QB, col // QB
    off_p = jnp.where(jnp.logical_and(is_lat, qb >= 1), 0.0, NEG)
    off_c = jnp.where(is_lat, 0.0, NEG)
    off_n = jnp.where(jnp.logical_and(is_lat, qb <= nlb - 2), 0.0, NEG)
    inside = jnp.logical_or(blk == 1, jnp.logical_or(jnp.logical_and(blk == 0, jj >= ii),
                                                     jnp.logical_and(blk == 2, jj <= ii)))
    off = jnp.where(blk == 0, off_p, jnp.where(blk == 1, off_c, off_n))
    q4 = jnp.concatenate([q_ref[:, (4 * kh + g) * HEAD_DIM:(4 * kh + g + 1) * HEAD_DIM] for g in range(4)], axis=0)
    rg = lax.broadcasted_iota(jnp.int32, (4 * QB, 1), 0) // QB
    snk = jnp.where(rg == 0, sink_ref[4 * kh],
                    jnp.where(rg == 1, sink_ref[4 * kh + 1], jnp.where(rg == 2, sink_ref[4 * kh + 2], sink_ref[4 * kh + 3])))
    lanes = slice(kh * HEAD_DIM, (kh + 1) * HEAD_DIM)
    kx, vx = k_refs[0][:, lanes], v_refs[0][:, lanes]
    kl = jnp.concatenate([r[:, lanes] for r in k_refs[1:]], axis=0)
    vl = jnp.concatenate([r[:, lanes] for r in v_refs[1:]], axis=0)
    sx = _dot(q4, kx, "nt")
    sl = jnp.where(inside, _dot(q4, kl, "nt"), NEG) + off
    return q4, snk, (kx, kl), (vx, vl), (sx, sl)


def _att_fwd(name, qkv, sink, ctx_queries, cfg, ride=None):
    nlb, ncb = cfg.seq // QB, cfg.ctx // QB
    qblk, qspec, kvspecs = _att_specs(cfg)

    def body(sink_ref, q_ref, kx_ref, vx_ref, kp_ref, kc_ref, kn_ref, vp_ref, vc_ref, vn_ref, o_ref, lse_ref):
        qb = pl.program_id(1)
        for kh in range(2):
            q4, snk, _, vs, ss = _att_scores(qb, nlb, sink_ref, q_ref, (kx_ref, kp_ref, kc_ref, kn_ref),
                                             (vx_ref, vp_ref, vc_ref, vn_ref), kh)
            m = snk
            for s_ in ss:
                m = jnp.maximum(m, jnp.max(s_, axis=-1, keepdims=True))
            den = jnp.exp(snk - m)
            o4 = jnp.zeros((4 * QB, HEAD_DIM), F32)
            for s_, v_ in zip(ss, vs):
                p = jnp.exp(s_ - m)
                den = den + jnp.sum(p, axis=-1, keepdims=True)
                o4 = o4 + _dot(p, v_)
            o4 = o4 / den
            lse = m + jnp.log(den)
            for g in range(4):
                h = 4 * kh + g
                o_ref[:, h * HEAD_DIM:(h + 1) * HEAD_DIM] = o4[g * QB:(g + 1) * QB].astype(BF16)
                lse_ref[:, h:h + 1] = lse[g * QB:(g + 1) * QB]

    return _hosted_call(
        body, ride, name, (cfg.b, nlb + (ncb if ctx_queries else 0)), [sink] + [qkv] * 9,
        [pl.BlockSpec(memory_space=pltpu.SMEM), qspec, *kvspecs],
        [jax.ShapeDtypeStruct((cfg.r, D), BF16), jax.ShapeDtypeStruct((cfg.r, N_HEADS), F32)],
        [pl.BlockSpec((QB, ATTN_W), lambda s, qb: (qblk(s, qb), 0)),
         pl.BlockSpec((QB, N_HEADS), lambda s, qb: (qblk(s, qb), 0))], [], _params(2))


def _att_bwd(name, qkv, mix, dmix, lse, sink, cos_t, sin_t, ctx_queries, cfg, ride=None):
    nlb, ncb = cfg.seq // QB, cfg.ctx // QB
    nqb = nlb + (ncb if ctx_queries else 0)
    qblk, qspec, kvspecs = _att_specs(cfg)

    def body(sink_ref, q_ref, kx_ref, vx_ref, kp_ref, kc_ref, kn_ref, vp_ref, vc_ref, vn_ref, o_ref, do_ref,
             lse_ref, cosq_ref, sinq_ref, cosk_ref, sink_tab_ref, dq_ref, dkvl_ref, dkvc_ref, dsink_ref,
             accl, accc, dqs):
        s_id, qb = pl.program_id(0), pl.program_id(1)

        @pl.when(qb == 0)
        def _():
            accl[...] = jnp.zeros_like(accl)
            accc[...] = jnp.zeros_like(accc)

        @pl.when(jnp.logical_and(s_id == 0, qb == 0))
        def _():
            dsink_ref[...] = jnp.zeros_like(dsink_ref)

        starts = [pl.multiple_of(jnp.clip(qb + off, 0, nlb - 1) * QB, QB) for off in (-1, 0, 1)]
        for kh in range(2):
            q4, snk, ks, vs, ss = _att_scores(qb, nlb, sink_ref, q_ref, (kx_ref, kp_ref, kc_ref, kn_ref),
                                              (vx_ref, vp_ref, vc_ref, vn_ref), kh)
            lanes = slice(kh * HEAD_DIM, (kh + 1) * HEAD_DIM)
            heads =[slice((4 * kh + g) * HEAD_DIM, (4 * kh + g + 1) * HEAD_DIM) for g in range(4)]
            do4 = jnp.concatenate([do_ref[:, hs] for hs in heads], axis=0)
            o4 = jnp.concatenate([o_ref[:, hs] for hs in heads], axis=0).astype(F32)
            lse4 = jnp.concatenate([lse_ref[:, 4 * kh + g:4 * kh + g + 1] for g in range(4)], axis=0)
            delta = jnp.sum(do4 * o4, axis=-1, keepdims=True)
            dq4 = jnp.zeros((4 * QB, HEAD_DIM), F32)
            dks, dvs = [], []
            for s_, k_, v_ in zip(ss, ks, vs):
                p = jnp.exp(s_ - lse4)
                ds = p * (_dot(do4, v_, "nt") - delta)
                dq4 = dq4 + _dot(ds, k_)
                dks.append(_dot(ds, q4, "tn"))
                dvs.append(_dot(p, do4, "tn"))
            accc[:, lanes] += dks[0]
            accc[:, 128 + kh * HEAD_DIM:128 + (kh + 1) * HEAD_DIM] += dvs[0]
            for t, st in enumerate(starts):
                accl[pl.ds(st, QB), lanes] += dks[1][t * QB:(t + 1) * QB]
                accl[pl.ds(st, QB), 128 + kh * HEAD_DIM:128 + (kh + 1) * HEAD_DIM] += dvs[1][t * QB:(t + 1) * QB]
            dsk = -jnp.exp(snk - lse4) * delta
            for g in range(4):
                h = 4 * kh + g
                dsink_ref[h:h + 1, :] += jnp.broadcast_to(_colsum(dsk[g * QB:(g + 1) * QB]), (1, 128))
                dqs[:, heads[g]] = dq4[g * QB:(g + 1) * QB]
        cos, sin = cosq_ref[...], sinq_ref[...]
        dq_ref[...] = jnp.concatenate(
            [_rope(dqs[:, 128 * t:128 * (t + 1)], cos, sin, -1.0) * (HEAD_DIM ** -0.5) for t in range(4)],
            axis=1).astype(BF16)

        @pl.when(qb == nqb - 1)
        def _():
            dk = _rope(accl[:, 0:128], cosk_ref[...], sink_tab_ref[...], -1.0)
            dkvl_ref[...] = jnp.concatenate([dk, accl[:, 128:256]], axis=1).astype(BF16)
            dkvc_ref[...] = accc[...].astype(BF16)

    rowq = lambda w: pl.BlockSpec((QB, w), lambda s, qb: (qblk(s, qb), 0))
    tabq = pl.BlockSpec((QB, 128), lambda s, qb: (jnp.where(qb < nlb, qb, cfg.seq // QB), 0))
    tabk = pl.BlockSpec((cfg.seq, 128), lambda s, qb: (0, 0))
    return _hosted_call(
        body, ride, name, (cfg.b, nqb), [sink] + [qkv] * 9 + [mix, dmix, lse, cos_t, sin_t, cos_t, sin_t],
        [pl.BlockSpec(memory_space=pltpu.SMEM), qspec, *kvspecs, rowq(ATTN_W), rowq(ATTN_W), rowq(N_HEADS),
         tabq, tabq, tabk, tabk],
        [jax.ShapeDtypeStruct((cfg.r, IN_W), BF16), jax.ShapeDtypeStruct((cfg.nl, 256), BF16),
         jax.ShapeDtypeStruct((cfg.nc, 256), BF16), jax.ShapeDtypeStruct((N_HEADS, 128), F32)],
        [rowq(ATTN_W), pl.BlockSpec((cfg.seq, 256), lambda s, qb: (s, 0)),
         pl.BlockSpec((cfg.ctx, 256), lambda s, qb: (s, 0)), pl.BlockSpec((N_HEADS, 128), lambda s, qb: (0, 0))],
        [pltpu.VMEM((cfg.seq, 256), F32), pltpu.VMEM((cfg.ctx, 256), F32), pltpu.VMEM((QB, ATTN_W), F32)], _params(2))


def _pool_geometry(n, c):
    lane = lax.broadcasted_iota(jnp.int32, (1, POOL_W), 1) // HEAD_DIM
    wl = jnp.where(lane == 0, 1, jnp.where(lane == 1, 2, jnp.where(lane == 2, 4, 8)))
    wr = wl - 1
    t = c * CHUNK + lax.broadcasted_iota(jnp.int32, (CHUNK, POOL_W), 0)
    cnt = (jnp.minimum(t + wr, n - 1) - jnp.maximum(t - wl, 0) + 1).astype(F32)
    return wl, wr, cnt


def _build_phases(src, ph, c):
    for s in range(1, 8):
        ph[s - 1] = src[c * CHUNK + s:c * CHUNK + s + CHUNK + 24, :]


def _window(src, ph, c, off):
    a, s = divmod(off, 8)
    if s == 0:
        return src[c * CHUNK + 8 * a:c * CHUNK + 8 * a + CHUNK, :]
    return ph[s - 1, 8 * a:8 * a + CHUNK, :]


def _conv_chunk(hp, ph, dw_ref, dwb_ref, c):
    _build_phases(hp, ph, c)
    acc = jnp.zeros((CHUNK, CONV_W), F32) + dwb_ref[...]
    for j in range(CONV_K):
        acc = acc + dw_ref[j:j + 1, :] * _window(hp, ph, c, j + 1)
    return acc


def _fill_glu(cp_ref, hp, n):
    hp[0:PAD, :] = jnp.zeros((PAD, CONV_W), F32)
    hp[PAD + n:2 * PAD + n, :] = jnp.zeros((PAD, CONV_W), F32)
    for c in range(n // CHUNK):
        rows = slice(c * CHUNK, (c + 1) * CHUNK)
        a = cp_ref[rows, 0:CONV_W].astype(F32)
        g = cp_ref[rows, CONV_W:2 * CONV_W].astype(F32)
        hp[PAD + c * CHUNK:PAD + (c + 1) * CHUNK, :] = a * _sigmoid(g)


def _fill_pool(cp_ref, pp, n):
    pp[0:PAD, :] = jnp.zeros((PAD, POOL_W), F32)
    pp[PAD + n:2 * PAD + n, :] = jnp.zeros((PAD, POOL_W), F32)
    for c in range(n // CHUNK):
        pp[PAD + c * CHUNK:PAD + (c + 1) * CHUNK, :] = cp_ref[c * CHUNK:(c + 1) * CHUNK, 2 * CONV_W:768].astype(F32)


LV = CHUNK + 2 * PAD
_LEVELS = pltpu.VMEM((3, LV + 16, POOL_W), F32)


def _clear_level_edges(lv):
    for b in range(3):
        lv[b, 0:8] = jnp.zeros((8, POOL_W), F32)
        lv[b, 8 + LV:16 + LV] = jnp.zeros((8, POOL_W), F32)


def _window_sums(src, lv, c, lead):
    lv[0, 8:8 + LV] = src[c * CHUNK:c * CHUNK + LV, :]
    lo = 7 if lead < 0 else 8
    lv[1, 8:8 + LV] = lv[0, lo:lo + LV] + lv[0, lo + 1:lo + 1 + LV]
    group = lax.broadcasted_iota(jnp.int32, (1, POOL_W), 1) // HEAD_DIM
    rows = slice(8 + PAD, 8 + PAD + CHUNK)
    res = lv[1, rows]
    cur = 1
    for g, s in ((1, 1), (2, 2), (3, 4)):
        nxt = 3 - cur
        lv[nxt, 8:8 + LV] = lv[cur, 8 - s:8 - s + LV] + lv[cur, 8 + s:8 + s + LV]
        res = jnp.where(group >= g, lv[nxt, rows], res)
        cur = nxt
    return res


def _pool_chunk(pp, lv, n, c):
    _, _, cnt = _pool_geometry(n, c)
    return _window_sums(pp, lv, c, -1) / cnt - pp[PAD + c * CHUNK:PAD + (c + 1) * CHUNK, :], cnt


def _seq_specs(n, blk_off, width, col=0):
    return pl.BlockSpec((n, width), lambda s: (blk_off + s, col))


def _full(shape):
    return pl.BlockSpec(shape, lambda s: (0,) * len(shape))


_PHASES = pltpu.VMEM((7, CHUNK + 24, CONV_W), F32)


def _convpool_fwd(name, cpin, mix, yconv, prm, n, blk_off, cfg):
    dw, dwb, lng, lnb, wbd, ps = prm
    n_alias = 1 if yconv is None else 2

    def body(*refs):
        cp_ref, dw_ref, dwb_ref, lng_ref, lnb_ref, wbd_ref, ps_ref = refs[:7]
        out_ref, y_ref, hp, pp, ph, lv = refs[7 + n_alias:]
        _fill_glu(cp_ref, hp, n)
        _fill_pool(cp_ref, pp, n)
        _clear_level_edges(lv)
        for c in range(n // CHUNK):
            rows = slice(c * CHUNK, (c + 1) * CHUNK)
            y = _conv_chunk(hp, ph, dw_ref, dwb_ref, c)
            y_ref[rows, :] = y
            d = y - jnp.mean(y, axis=-1, keepdims=True)
            hn = d * lax.rsqrt(jnp.mean(d * d, axis=-1, keepdims=True) + EPS) * lng_ref[...] + lnb_ref[...]
            out_ref[rows, 0:CONV_W] = (hn * _sigmoid(hn)).astype(BF16)
            yp, _ = _pool_chunk(pp, lv, n, c)
            out_ref[rows, CONV_W:2 * CONV_W] = (_dot(yp, wbd_ref[...]) * ps_ref[...]).astype(BF16)

    through = [mix] if yconv is None else [mix, yconv]
    return pl.pallas_call(
        body, name=name, grid=(cfg.b,),
        in_specs=[_seq_specs(n, blk_off, 768), _full((32, CONV_W)), _full((1, CONV_W)), _full((1, CONV_W)),
                  _full((1, CONV_W)), _full((POOL_W, POOL_W)), _full((1, POOL_W))] + [_ANY] * n_alias,
        out_specs=[_seq_specs(n, blk_off, 512, 1), _seq_specs(n, blk_off, CONV_W)],
        out_shape=[jax.ShapeDtypeStruct((cfg.r, D), BF16), jax.ShapeDtypeStruct((cfg.r, CONV_W), F32)],
        scratch_shapes=[pltpu.VMEM((n + 2 * PAD, CONV_W), F32), pltpu.VMEM((n + 2 * PAD, POOL_W), F32), _PHASES, _LEVELS],
        input_output_aliases={7 + i: i for i in range(n_alias)},
        compiler_params=_params(1))(cpin, dw, dwb, lng, lnb, wbd, ps, *through)


_SMALL_SHAPES = [(32, CONV_W), (1, CONV_W), (1, CONV_W), (1, CONV_W), (POOL_W, POOL_W), (1, POOL_W)]


def _convpool_bwd(name, cpin, yconv, dmix, prm, acc_in, n, blk_off, cfg):
    dw, dwb, lng, lnb, wbd, ps = prm
    nch = n // CHUNK

    def body(cp_ref, y_ref, dm_ref, dw_ref, dwb_ref, lng_ref, lnb_ref, wbd_ref, ps_ref, dcp_in,
             a_dw, a_dwb, a_lng, a_lnb, a_wbd, a_ps,
             dcp_ref, o_dw, o_dwb, o_lng, o_lnb, o_wbd, o_ps, hp, dyp, pp, wp, dyv, dwacc, ph, lv):
        s = pl.program_id(0)
        _clear_level_edges(lv)

        @pl.when(s == 0)
        def _():
            for o_, a_ in ((o_dw, a_dw), (o_dwb, a_dwb), (o_lng, a_lng), (o_lnb, a_lnb), (o_wbd, a_wbd), (o_ps, a_ps)):
                o_[...] = a_[...]
            dwacc[...] = jnp.zeros_like(dwacc)

        _fill_glu(cp_ref, hp, n)
        _fill_pool(cp_ref, pp, n)
        for ref in (dyp, wp):
            ref[0:PAD, :] = jnp.zeros((PAD, CONV_W), F32)
            ref[PAD + n:2 * PAD + n, :] = jnp.zeros((PAD, CONV_W), F32)
        for c in range(nch):
            rows = slice(c * CHUNK, (c + 1) * CHUNK)
            y = y_ref[rows, :]
            d = y - jnp.mean(y, axis=-1, keepdims=True)
            rstd = lax.rsqrt(jnp.mean(d * d, axis=-1, keepdims=True) + EPS)
            xh = d * rstd
            hn = xh * lng_ref[...] + lnb_ref[...]
            sg = _sigmoid(hn)
            dhn = dm_ref[rows, 0:CONV_W] * (sg * (1.0 + hn * (1.0 - sg)))
            o_lnb[...] += _colsum(dhn)
            o_lng[...] += _colsum(dhn * xh)
            dxh = dhn * lng_ref[...]
            dy = rstd * (dxh - jnp.mean(dxh, axis=-1, keepdims=True) - xh * jnp.mean(dxh * xh, axis=-1, keepdims=True))
            o_dwb[...] += _colsum(dy)
            dyp[PAD + c * CHUNK:PAD + (c + 1) * CHUNK, :] = dy
            _build_phases(hp, ph, c)
            for j in range(CONV_K):
                prod = dy * _window(hp, ph, c, j + 1)
                dwacc[8 * j:8 * j + 8, :] += jnp.sum(prod.reshape(CHUNK // 8, 8, CONV_W), axis=0)
            yp, cnt = _pool_chunk(pp, lv, n, c)
            dz = dm_ref[rows, CONV_W:2 * CONV_W]
            o_ps[...] += _colsum(dz * _dot(yp, wbd_ref[...]))
            dzs = dz * ps_ref[...]
            o_wbd[...] += _dot(yp, dzs, "tn")
            dv = _dot(dzs, wbd_ref[...], "nt")
            dyv[rows, :] = dv
            wp[PAD + c * CHUNK:PAD + (c + 1) * CHUNK, :] = dv / cnt
        for c in range(nch):
            rows = slice(c * CHUNK, (c + 1) * CHUNK)
            _build_phases(dyp, ph, c)
            dh = jnp.zeros((CHUNK, CONV_W), F32)
            for j in range(CONV_K):
                dh = dh + dw_ref[j:j + 1, :] * _window(dyp, ph, c, 31 - j)
            a = cp_ref[rows, 0:CONV_W].astype(F32)
            sg = _sigmoid(cp_ref[rows, CONV_W:2 * CONV_W].astype(F32))
            dcp_ref[rows, 0:CONV_W] = (dh * sg).astype(BF16)
            dcp_ref[rows, CONV_W:2 * CONV_W] = (dh * a * sg * (1.0 - sg)).astype(BF16)
            dcp_ref[rows, 2 * CONV_W:768] = (_window_sums(wp, lv, c, 1) - dyv[rows, :]).astype(BF16)

        @pl.when(s == cfg.b - 1)
        def _():
            for j in range(CONV_K):
                o_dw[j:j + 1, :] += _colsum(dwacc[8 * j:8 * j + 8, :])

    small_specs = [_full(sh) for sh in _SMALL_SHAPES]
    return pl.pallas_call(
        body, name=name, grid=(cfg.b,),
        in_specs=[_seq_specs(n, blk_off, 768), _seq_specs(n, blk_off, CONV_W), _seq_specs(n, blk_off, 512, 1),
                  *small_specs, _ANY, *small_specs],
        out_specs=[_seq_specs(n, blk_off, 768, 1), *small_specs],
        out_shape=[jax.ShapeDtypeStruct((cfg.r, IN_W), BF16)] + [jax.ShapeDtypeStruct(sh, F32) for sh in _SMALL_SHAPES],
        scratch_shapes=[pltpu.VMEM((n + 2 * PAD, CONV_W), F32), pltpu.VMEM((n + 2 * PAD, CONV_W), F32),
                        pltpu.VMEM((n + 2 * PAD, POOL_W), F32), pltpu.VMEM((n + 2 * PAD, POOL_W), F32),
                        pltpu.VMEM((n, POOL_W), F32), pltpu.VMEM((8 * 32, CONV_W), F32), _PHASES, _LEVELS],
        input_output_aliases={9: 0}, compiler_params=_params(1))(cpin, yconv, dmix, dw, dwb, lng, lnb, wbd, ps, *acc_in)


def _place_kv(name, du, dkvl, dkvc, with_ctx, cfg):
    nlb = cfg.nl // TM

    def body(l_ref, c_ref, du_in, o_ref):
        i = pl.program_id(0)
        o_ref[...] = jnp.where(i < nlb, l_ref[...], c_ref[...])

    return pl.pallas_call(
        body, name=name, grid=(cfg.r // TM if with_ctx else nlb,),
        in_specs=[pl.BlockSpec((TM, 256), lambda i: (jnp.minimum(i, nlb - 1), 0)),
                  pl.BlockSpec((TM, 256), lambda i: (jnp.maximum(i - nlb, 0), 0)), _ANY],
        out_specs=pl.BlockSpec((TM, 256), lambda i: (i, 2)), out_shape=jax.ShapeDtypeStruct((cfg.r, IN_W), BF16),
        input_output_aliases={2: 0}, compiler_params=_params(1))(dkvl, dkvc, du)


def _place_ctx_kv_only(name, du, dkvc, cfg):
    nlb = cfg.nl // TM

    def body(c_ref, du_in, o_ref):
        o_ref[...] = jnp.zeros_like(o_ref)
        o_ref[:, ATTN_W:ATTN_W + 256] = c_ref[...]

    return pl.pallas_call(
        body, name=name, grid=(cfg.nc // TM,), in_specs=[pl.BlockSpec((TM, 256), lambda i: (i, 0)), _ANY],
        out_specs=pl.BlockSpec((TM, IN_W), lambda i: (nlb + i, 0)), out_shape=jax.ShapeDtypeStruct((cfg.r, IN_W), BF16),
        input_output_aliases={1: 0}, compiler_params=_params(1))(dkvc, du)


def _ffn_in(name, h, w_ffn_in, layer, nblk, cfg, ride=None):
    def body(h_ref, wg_ref, wu_ref, fac_ref, act_ref):
        hv = h_ref[...]
        g = _dot(hv, wg_ref[...])
        u = _dot(hv, wu_ref[...])
        s = _sigmoid(g)
        gs = g * s
        fac_ref[0] = ((s + gs * (1.0 - s)) * u).astype(BF16)
        fac_ref[1] = gs.astype(BF16)
        act_ref[...] = (gs * u).astype(BF16)

    wspec = lambda base: pl.BlockSpec((None, None, D, HALF_FF), lambda j, i: (layer, base + j, 0, 0))
    return _hosted_call(
        body, ride, name, (2, nblk), [h, w_ffn_in, w_ffn_in],
        [pl.BlockSpec((TM, D), lambda j, i: (i, 0)), wspec(0), wspec(2)],
        [jax.ShapeDtypeStruct((2, cfg.r, D_FF), BF16), jax.ShapeDtypeStruct((cfg.r, D_FF), BF16)],
        [pl.BlockSpec((2, TM, HALF_FF), lambda j, i: (0, i, j)), pl.BlockSpec((TM, HALF_FF), lambda j, i: (i, j))],
        [], _params(2))


def _row_block(rows, cols, max_bytes=1 << 20):
    best = 16
    for t in range(16, rows + 1, 16):
        if rows % t == 0 and t * cols * 4 <= max_bytes:
            best = t
    assert rows % best == 0
    return best


def _pair_add(name, own32, recv, c_idx):
    _, _, s0, s1 = own32.shape
    tr = _row_block(s0, s1)

    def body(c_ref, a_ref, b_ref, o_ref):
        o_ref[...] = (a_ref[...].astype(F32) + b_ref[...].astype(F32)).astype(BF16)

    grid_spec = pltpu.PrefetchScalarGridSpec(
        num_scalar_prefetch=1, grid=(N_SHARD * s0 // tr,),
        in_specs=[pl.BlockSpec((None, tr, s1), lambda i, c: (c[0], i, 0)), pl.BlockSpec((tr, s1), lambda i, c: (i, 0))],
        out_specs=pl.BlockSpec((tr, s1), lambda i, c: (i, 0)))
    out = pl.pallas_call(body, name=name, grid_spec=grid_spec, out_shape=jax.ShapeDtypeStruct((N_SHARD * s0, s1), BF16),
                         compiler_params=_params(1))(c_idx, own32.reshape(2, N_SHARD * s0, s1), recv.reshape(N_SHARD * s0, s1))
    return out.reshape(N_SHARD, s0, s1)


def _shard_sum(name, pair_sum, recv, jc_idx):
    _, s0, s1 = pair_sum.shape
    tr = _row_block(s0, s1)

    def body(jc_ref, a_ref, b_ref, o_ref):
        o_ref[...] = ((a_ref[...].astype(F32) + b_ref[0].astype(F32)) + b_ref[1].astype(F32)) + b_ref[2].astype(F32)

    grid_spec = pltpu.PrefetchScalarGridSpec(
        num_scalar_prefetch=1, grid=(s0 // tr,),
        in_specs=[pl.BlockSpec((None, tr, s1), lambda i, jc: (jc[0], i, 0)), pl.BlockSpec((3, tr, s1), lambda i, jc: (0, i, 0))],
        out_specs=pl.BlockSpec((None, tr, s1), lambda i, jc: (jc[1], i, 0)))
    return pl.pallas_call(body, name=name, grid_spec=grid_spec, out_shape=jax.ShapeDtypeStruct((2, s0, s1), F32),
                          compiler_params=_params(1))(jc_idx, pair_sum, recv)


def _adamw_math(w, g, m, v):
    m = ADAM_B1 * m + (1.0 - ADAM_B1) * g
    v = ADAM_B2 * v + (1.0 - ADAM_B2) * (g * g)
    m_hat = m / (1.0 - ADAM_B1 ** ADAM_STEP)
    v_hat = v / (1.0 - ADAM_B2 ** ADAM_STEP)
    delta = -ADAM_LR * (m_hat / (jnp.sqrt(v_hat) + ADAM_EPS) + ADAM_WD * w)
    return delta, m, v


def _adamw(name, w, g, m, v):
    rows, cols = w.shape
    tr = rows if rows % 16 else _row_block(rows, cols, 1 << 19)

    def body(w_ref, g_ref, m_ref, v_ref, d_ref, mo_ref, vo_ref):
        d, mn, vn = _adamw_math(w_ref[...], g_ref[...], m_ref[...], v_ref[...])
        d_ref[...] = d
        mo_ref[...] = mn
        vo_ref[...] = vn

    spec = pl.BlockSpec((tr, cols), lambda i: (i, 0))
    shape = jax.ShapeDtypeStruct((rows, cols), F32)
    return pl.pallas_call(body, name=name, grid=(rows // tr,), in_specs=[spec] * 4, out_specs=[spec] * 3,
                          out_shape=[shape] * 3, compiler_params=_params(1))(w, g, m, v)


def _adamw_layers(name, w, g_layers, m, v):
    rows, cols = w.shape
    s0 = rows // 2
    tr = _row_block(s0, cols, 1 << 19)
    nb = s0 // tr

    def body(w_ref, g0_ref, g1_ref, m_ref, v_ref, g_ref, d_ref, mo_ref, vo_ref):
        g = jnp.where(pl.program_id(0) < nb, g0_ref[...], g1_ref[...])
        d, mn, vn = _adamw_math(w_ref[...], g, m_ref[...], v_ref[...])
        g_ref[...] = g
        d_ref[...] = d
        mo_ref[...] = mn
        vo_ref[...] = vn

    spec = pl.BlockSpec((tr, cols), lambda i: (i, 0))
    shape = jax.ShapeDtypeStruct((rows, cols), F32)
    return pl.pallas_call(
        body, name=name, grid=(2 * nb,),
        in_specs=[spec, pl.BlockSpec((tr, cols), lambda i: (jnp.minimum(i, nb - 1), 0)),
                  pl.BlockSpec((tr, cols), lambda i: (jnp.maximum(i - nb, 0), 0)), spec, spec],
        out_specs=[spec] * 4, out_shape=[shape] * 4, compiler_params=_params(1))(w, g_layers[0], g_layers[1], m, v)


def _position():
    return lax.axis_index("x"), lax.axis_index("y"), lax.axis_index("c")


def _other_chips(x, y):
    return [(1 - x, y), (x, 1 - y), (1 - x, 1 - y)]


def _run_ride(name, ride):
    n_in, n_out = len(ride.ins), len(ride.out_shape)

    def body(*refs):
        parts = (refs[:n_in], refs[n_in:n_in + n_out], refs[n_in + n_out:])
        ride.start(*parts)
        ride.finish(*parts)

    return pl.pallas_call(
        body, name=name, in_specs=[_ANY] * n_in, out_specs=[_ANY] * n_out, out_shape=ride.out_shape,
        scratch_shapes=ride.scratch, compiler_params=pltpu.CompilerParams(vmem_limit_bytes=VMEM_LIMIT_V7X))(*ride.ins)


def _gather_ride(shards):
    n = len(shards)

    def copies(ins, outs, scr):
        ssem, rsem = scr[n], scr[n + 1]
        x, y, c = _position()
        me, sibling = 2 * x + y, (x, y, 1 - c)

        def remote(src, dst, i, dev):
            return pltpu.make_async_remote_copy(src, dst, ssem.at[i], rsem.at[i], device_id=dev, device_id_type=_MESH)

        fetch_out, fetch_in, pass_out, pass_in = [], [], [], []
        for a, (src, dst) in enumerate(zip(ins, outs)):
            for k, (px, py) in enumerate(_other_chips(x, y)):
                j, i1, i2 = 2 * px + py, 3 * a + k, 3 * n + 3 * a + k
                fetch_out.append(remote(src.at[c], dst.at[me, c], i1, (px, py, c)))
                fetch_in.append(remote(src.at[c], dst.at[j, c], i1, (px, py, c)))
                pass_out.append(remote(dst.at[j, c], dst.at[j, c], i2, sibling))
                pass_in.append(remote(dst.at[j, 1 - c], dst.at[j, 1 - c], i2, sibling))
        return me, fetch_out, fetch_in, pass_out, pass_in

    def start(ins, outs, scr):
        bufs, lsem = scr[:n], scr[n + 2]
        me, fetch_out, _, _, _ = copies(ins, outs, scr)
        for cp in fetch_out:
            cp.start()
        loads = []
        for a, (src, buf) in enumerate(zip(ins, bufs)):
            ld = pltpu.make_async_copy(src, buf, lsem.at[2 * a])
            ld.start()
            loads.append(ld)
        for a, (ld, buf, dst) in enumerate(zip(loads, bufs, outs)):
            ld.wait()
            st = pltpu.make_async_copy(buf, dst.at[me], lsem.at[2 * a + 1])
            st.start()
            st.wait()

    def finish(ins, outs, scr):
        _, fetch_out, fetch_in, pass_out, pass_in = copies(ins, outs, scr)
        for arrived, onward in zip(fetch_in, pass_out):
            arrived.wait_recv()
            onward.start()
        for cp in pass_in:
            cp.wait_recv()
        for cp in fetch_out + pass_out:
            cp.wait_send()

    return _Ride(list(shards), [jax.ShapeDtypeStruct((N_SHARD,) + s.shape, s.dtype) for s in shards],
                 [pltpu.VMEM(s.shape, s.dtype) for s in shards]
                 + [pltpu.SemaphoreType.DMA((6 * n,)), pltpu.SemaphoreType.DMA((6 * n,)), pltpu.SemaphoreType.DMA((2 * n,))],
                 start, finish)


def _comm(name, ins, out_shape, n_remote, plan):
    n_in, n_out = len(ins), len(out_shape)

    def body(*refs):
        plan(refs[:n_in], refs[n_in:n_in + n_out], *refs[n_in + n_out:])

    return pl.pallas_call(
        body, name=name, in_specs=[_ANY] * n_in, out_specs=[_ANY] * n_out, out_shape=out_shape,
        scratch_shapes=[pltpu.SemaphoreType.DMA((n_remote,)), pltpu.SemaphoreType.DMA((n_remote,))])(*ins)


def _send_other_half(name, grads_bf):
    n = len(grads_bf)

    def plan(ins, outs, ssem, rsem):
        x, y, c = _position()
        started = []
        for a, (src, dst) in enumerate(zip(ins, outs)):
            cp = pltpu.make_async_remote_copy(src.at[1 - c], dst, ssem.at[a], rsem.at[a], device_id=(x, y, 1 - c),
                                              device_id_type=_MESH)
            cp.start()
            started.append(cp)
        for cp in started:
            cp.wait_recv()
        for cp in started:
            cp.wait_send()

    shapes = [jax.ShapeDtypeStruct(s.shape[1:], s.dtype) for s in grads_bf]
    return _comm(name, grads_bf, shapes, n, plan)


def _exchange_ride(pair_sums):
    n = len(pair_sums)

    def copies(ins, outs, scr):
        ssem, rsem = scr
        x, y, c = _position()
        return [pltpu.make_async_remote_copy(src.at[2 * px + py], dst.at[k], ssem.at[3 * a + k], rsem.at[3 * a + k],
                                             device_id=(px, py, c), device_id_type=_MESH)
                for a, (src, dst) in enumerate(zip(ins, outs)) for k, (px, py) in enumerate(_other_chips(x, y))]

    def start(ins, outs, scr):
        for cp in copies(ins, outs, scr):
            cp.start()

    def finish(ins, outs, scr):
        for cp in copies(ins, outs, scr):
            cp.wait_recv()
        for cp in copies(ins, outs, scr):
            cp.wait_send()

    return _Ride(list(pair_sums), [jax.ShapeDtypeStruct((3,) + s.shape[1:], s.dtype) for s in pair_sums],
                 [pltpu.SemaphoreType.DMA((3 * n,)), pltpu.SemaphoreType.DMA((3 * n,))], start, finish)


def _swap_reduced(name, grads):
    n = len(grads)

    def body(*refs):
        ins, outs, ssem, rsem = refs[:n], refs[n:2 * n], refs[2 * n], refs[2 * n + 1]
        x, y, c = _position()
        sent = []
        for a, (src, dst) in enumerate(zip(ins, outs)):
            cp = pltpu.make_async_remote_copy(src.at[c], dst.at[c], ssem.at[a], rsem.at[a], device_id=(x, y, 1 - c),
                                              device_id_type=_MESH)
            cp.start()
            sent.append(cp)
        for a, (src, dst) in enumerate(zip(ins, outs)):
            pltpu.make_async_remote_copy(src.at[1 - c], dst.at[1 - c], ssem.at[a], rsem.at[a], device_id=(x, y, 1 - c),
                                         device_id_type=_MESH).wait_recv()
        for cp in sent:
            cp.wait_send()

    return pl.pallas_call(
        body, name=name, in_specs=[_ANY] * n, out_specs=[_ANY] * n,
        out_shape=[jax.ShapeDtypeStruct(g.shape, g.dtype) for g in grads],
        scratch_shapes=[pltpu.SemaphoreType.DMA((n,)), pltpu.SemaphoreType.DMA((n,))],
        input_output_aliases={a: a for a in range(n)})(*grads)


_FLIPS = [(dx, dy, dc) for dx in (0, 1) for dy in (0, 1) for dc in (0, 1) if dx + dy + dc]
_VMEM = pl.BlockSpec(memory_space=pltpu.VMEM)


def _to_all(src_of, dst, ssem, rsem):
    x, y, c = _position()
    me = 4 * x + 2 * y + c
    peers = [((x + dx) % 2, (y + dy) % 2, (c + dc) % 2) for dx, dy, dc in _FLIPS]
    sent = []
    for k, (px, py, pc) in enumerate(peers):
        cp = pltpu.make_async_remote_copy(src_of(2 * px + py), dst.at[me], ssem.at[k], rsem.at[k],
                                          device_id=(px, py, pc), device_id_type=_MESH)
        cp.start()
        sent.append(cp)
    for k, (px, py, pc) in enumerate(peers):
        pltpu.make_async_remote_copy(src_of(2 * px + py), dst.at[4 * px + 2 * py + pc], ssem.at[k], rsem.at[k],
                                     device_id=(px, py, pc), device_id_type=_MESH).wait_recv()
    for cp in sent:
        cp.wait_send()
    return me, 2 * x + y


def _share_small(name, block, total):
    shape = block.shape

    def body(in_ref, out_ref, *scratch):
        buf, ssem, rsem = (out_ref,) + scratch if not total else scratch
        me, _ = _to_all(lambda chip: in_ref, buf, ssem, rsem)
        buf[me] = in_ref[...]
        if total:
            acc = buf[0]
            for d in range(1, 8):
                acc = acc + buf[d]
            out_ref[...] = acc

    sems = [pltpu.SemaphoreType.DMA((7,)), pltpu.SemaphoreType.DMA((7,))]
    return pl.pallas_call(
        body, name=name, in_specs=[_VMEM], out_specs=_VMEM,
        out_shape=jax.ShapeDtypeStruct(shape if total else (8,) + shape, F32),
        scratch_shapes=([pltpu.VMEM((8,) + shape, F32)] if total else []) + sems,
        compiler_params=pltpu.CompilerParams(vmem_limit_bytes=VMEM_LIMIT_V7X))(block)


def _mod_rows_exchange(mv):
    def body(mv_ref, out_ref, ssem, rsem):
        x, y, c = _position()
        me = 2 * x + y
        out_ref[me] = mv_ref[4 * x + 2 * y + c]
        sent = []
        for k, (px, py) in enumerate(_other_chips(x, y)):
            cp = pltpu.make_async_remote_copy(mv_ref.at[4 * px + 2 * py + c], out_ref.at[me], ssem.at[k], rsem.at[k],
                                              device_id=(px, py, c), device_id_type=_MESH)
            cp.start()
            sent.append(cp)
        for k, (px, py) in enumerate(_other_chips(x, y)):
            pltpu.make_async_remote_copy(mv_ref.at[0], out_ref.at[2 * px + py], ssem.at[k], rsem.at[k],
                                         device_id=(px, py, c), device_id_type=_MESH).wait_recv()
        for cp in sent:
            cp.wait_send()

    return pl.pallas_call(
        body, name="mod_rows_exchange", in_specs=[_VMEM], out_specs=_VMEM,
        out_shape=jax.ShapeDtypeStruct((N_SHARD,) + mv.shape[1:], F32),
        scratch_shapes=[pltpu.SemaphoreType.DMA((3,)), pltpu.SemaphoreType.DMA((3,))],
        compiler_params=pltpu.CompilerParams(vmem_limit_bytes=VMEM_LIMIT_V7X))(mv)


def _mod_grad_exchange(dmj, dm_rows):
    def body(dmj_ref, rows_ref, out_ref, bias_ref, ssem, rsem):
        me, chip = _to_all(lambda j: dmj_ref.at[j], out_ref, ssem, rsem)
        out_ref[me] = dmj_ref[chip]
        for l in range(2):
            bias_ref[l] = _colsum(rows_ref[l])

    return pl.pallas_call(
        body, name="mod_grad_exchange", in_specs=[_VMEM, _VMEM], out_specs=[_VMEM, _VMEM],
        out_shape=[jax.ShapeDtypeStruct((8,) + dmj.shape[1:], F32), jax.ShapeDtypeStruct((2, 1, dm_rows.shape[-1]), F32)],
        scratch_shapes=[pltpu.SemaphoreType.DMA((7,)), pltpu.SemaphoreType.DMA((7,))],
        compiler_params=pltpu.CompilerParams(vmem_limit_bytes=VMEM_LIMIT_V7X))(dmj, dm_rows)


def _pack(arrays):
    flat = jnp.concatenate([a.reshape(-1).astype(F32) for a in arrays])
    total = flat.shape[0]
    rows = -(-total // (8 * PACK_LANES)) * 8
    return jnp.pad(flat, (0, rows * PACK_LANES - total)).reshape(rows, PACK_LANES)


def _unpack(pack, shapes):
    flat, out, pos = pack.reshape(-1), [], 0
    for sh in shapes:
        size = int(np.prod(sh)) if len(sh) else 1
        out.append(flat[pos:pos + size].reshape(sh))
        pos += size
    return out


def _block_diag(pw):
    out = jnp.zeros((POOL_W, POOL_W), pw.dtype)
    for g in range(4):
        out = out.at[g * 64:(g + 1) * 64, g * 64:(g + 1) * 64].set(pw[g])
    return out


def _local_step(x, ctx, small, mod3s, loss_target, comm):
    cfg = _Cfg(x.shape[0], x.shape[1], ctx.shape[1])
    assert cfg.seq % TM == 0 and cfg.nc % TM == 0 and cfg.seq % cfg.ctx == 0 and cfg.ctx % CHUNK == 0
    nb_all, nb_lat = cfg.r // TM, cfg.nl // TM
    last = 1
    wf, big = comm.wf, comm.grads
    cos_t, sin_t = _rope_tables(cfg.seq)
    xs = jnp.concatenate([x.reshape(cfg.nl, D), ctx.reshape(cfg.nc, D)], axis=0)
    row = lambda w: pl.BlockSpec((TM, w), lambda i, j, k: (i, 0))
    mod3_spec = pl.BlockSpec((None, 6, D), lambda i, j, k: (cfg.mod_row(i), 0, 0))
    whole = lambda rows: pl.BlockSpec((None, rows, D), lambda i, j, k: (0, 0, 0))

    def conv_params(l):
        dw = jnp.pad(wf[l]["conv_dw"], ((0, 1), (0, 0)))
        return (dw, small["conv_dw_b"][l][None], small["conv_ln_g"][l][None], small["conv_ln_b"][l][None],
                _block_diag(small["pool_w"][l]).astype(BF16), small["pool_scale"][l][None])

    def residual_epi(ig):
        def epi(acc, ex, outs):
            x_ref, m_ref = ex
            outs[0][...] = x_ref[...] + m_ref[ig:ig + 1, :] * acc
            outs[1][...] = acc.astype(BF16)
        return epi

    def mixer_epi(acc, ex, outs):
        x_ref, m_ref, g_ref = ex
        x1 = x_ref[...] + m_ref[2:3, :] * acc
        outs[0][...] = x1
        outs[1][...] = acc.astype(BF16)
        outs[2][...] = _norm_mod(x1, g_ref[...], m_ref, 3, 4)

    def hosted(name, call):
        outs, got = call(comm.ride(name))
        comm.landed(name, got)
        return outs

    saved = []
    for l in range(2):
        nb = nb_lat if l == last else nb_all
        wl = wf[l]
        mod3 = mod3s[l]
        h1, qkv, cpin = _in_proj(f"in_proj{l}", xs, small["norm1_g"][l][None], mod3, wl["w_in"], 0, cos_t, sin_t, cfg)
        mix, lse = hosted(f"att_fwd{l}", lambda ride: _att_fwd(f"att_fwd{l}", qkv, small["attn_sink"][l], l != last, cfg, ride))
        prm = conv_params(l)
        mix, yconv = _convpool_fwd(f"convpool_fwd_lat{l}", cpin, mix, None, prm, cfg.seq, 0, cfg)
        if l != last:
            mix, yconv = _convpool_fwd(f"convpool_fwd_ctx{l}", cpin, mix, yconv, prm, cfg.ctx, cfg.nl // cfg.ctx, cfg)
        x1, y1, h2 = _mm(f"out_proj{l}", "nn", (nb, 1, 1), mix, wl["w_out"].reshape(1, D, D), row(D), whole(D),
                         [jax.ShapeDtypeStruct((cfg.r, D), F32)] + [jax.ShapeDtypeStruct((cfg.r, D), BF16)] * 2, [row(D)] * 3,
                         extras=[xs, mod3, small["norm2_g"][l][None]],
                         extra_specs=[row(D), mod3_spec, pl.BlockSpec((1, D), lambda i, j, k: (0, 0))], epi=mixer_epi)
        gu, act = hosted(f"ffn_in{l}", lambda ride: _ffn_in(f"ffn_in{l}", h2, wl["w_ffn_in"], 0, nb, cfg, ride))
        x2, y2 = _mm(f"ffn_out{l}", "nn", (nb, 1, 1), act, wl["w_ffn_out"].reshape(1, D_FF, D), row(D_FF), whole(D_FF),
                     [jax.ShapeDtypeStruct((cfg.r, D), F32), jax.ShapeDtypeStruct((cfg.r, D), BF16)], [row(D), row(D)],
                     extras=[x1, mod3], extra_specs=[row(D), mod3_spec], epi=residual_epi(5))
        saved.append(dict(mod3=mod3, x0=xs, h1=h1, qkv=qkv, cpin=cpin, mix=mix, yconv=yconv, lse=lse, y1=y1, x1=x1,
                          h2=h2, gu=gu, act=act, y2=y2, prm=prm))
        xs = x2

    dx, loss, d_final_g, dy2, dg2 = _loss_head(xs, loss_target.reshape(cfg.nl, D), small["final_g"][None],
                                               saved[last]["y2"], saved[last]["mod3"], 5, cfg)

    sg = {k: [None, None] for k in ("norm1_g", "norm2_g", "conv_dw", "conv_dw_b", "conv_ln_g", "conv_ln_b",
                                    "attn_sink", "pool_w", "pool_scale")}
    dms = [None, None]

    def swiglu_bwd_epi(acc, ex, outs):
        outs[0][0] = (acc * ex[0][0].astype(F32)).astype(BF16)
        outs[0][1] = (acc * ex[0][1].astype(F32)).astype(BF16)

    def halves_epi(acc, ex, outs):
        h = acc.shape[0] // 2
        outs[0][0] = acc[:h].astype(BF16)
        outs[0][1] = acc[h:].astype(BF16)

    def row_shards_epi(n):
        def epi(acc, ex, outs):
            s0 = acc.shape[0] // n
            h = s0 // 2
            for t in range(n):
                for half in range(2):
                    outs[0][half, t] = acc[t * s0 + half * h:t * s0 + (half + 1) * h].astype(BF16)
        return epi

    def col_shards_epi(acc, ex, outs):
        h = acc.shape[0] // 2
        for j in range(N_SHARD):
            for half in range(2):
                outs[0][half, j] = acc[half * h:(half + 1) * h, j * IN_SHARD:(j + 1) * IN_SHARD].astype(BF16)

    for l in (1, 0):
        sv = saved[l]
        mod3 = sv["mod3"]
        nb = nb_lat if l == last else nb_all
        tr = _dw_rows(nb * TM)
        steps = nb * TM // tr
        wl = wf[l]
        gu_spec = pl.BlockSpec((2, TM, HALF_FF), lambda j, i, k: (0, i, j))
        df = _mm(f"ffn_out_bwd{l}", "nt", (2, nb, 1), dy2, wl["w_ffn_out"].reshape(1, D_FF, D),
                 pl.BlockSpec((TM, D), lambda j, i, k: (i, 0)), pl.BlockSpec((None, HALF_FF, D), lambda j, i, k: (0, j, 0)),
                 [jax.ShapeDtypeStruct((2, cfg.r, D_FF), BF16)], [gu_spec], extras=[sv["gu"]], extra_specs=[gu_spec],
                 epi=swiglu_bwd_epi)[0]
        big[l]["w_ffn_out"] = _mm_dw(
            f"dw_ffn_out{l}", sv["act"], dy2, pl.BlockSpec((tr, HALF_FF), lambda i, j, k: (k, j)),
            pl.BlockSpec((tr, D), lambda i, j, k: (k, 0)), (2, N_SHARD, D_FF // 8, D),
            pl.BlockSpec((2, 2, D_FF // 8, D), lambda i, j, k: (0, j, 0, 0)), 2, (HALF_FF, D), steps, row_shards_epi(2))[0]
        ride = comm.ride(f"dw_ffn_in{l}")
        res = _mm_dw(f"dw_ffn_in{l}", sv["h2"], df, pl.BlockSpec((tr, D), lambda i, j, k: (k, 0)),
                     pl.BlockSpec((None, tr, HALF_FF), lambda i, j, k: (j // 2, k, j % 2)), (2, N_SHARD, D // 2, HALF_FF),
                     pl.BlockSpec((2, None, D // 2, HALF_FF), lambda i, j, k: (0, j, 0, 0)), N_SHARD, (D, HALF_FF), steps,
                     halves_epi, ride=ride)
        res, got = res if ride is not None else (res, [])
        big[l]["w_ffn_in"] = res[0]
        comm.landed(f"dw_ffn_in{l}", got)
        dx1, dsh2, dsc2, dn2, dy1, dg1 = _proj_norm_bwd(
            f"ffn_in_bwd{l}", df, pl.BlockSpec((2, TM, D_FF), lambda i: (0, i, 0)),
            lambda a_ref, j: a_ref[j // 2, :, (j % 2) * HALF_FF:(j % 2 + 1) * HALF_FF], wl["w_ffn_in"],
            sv["x1"], dx, small["norm2_g"][l][None], mod3, 4, nb, False, (sv["y1"], mod3, 2), cfg)
        dmix = _mm(f"out_proj_bwd{l}", "nt", (nb, 1, 1), dy1, wl["w_out"].reshape(1, D, D), row(D), whole(D),
                   [jax.ShapeDtypeStruct((cfg.r, D), F32)], [row(D)])[0]
        big[l]["w_out"] = _mm_dw(
            f"dw_out{l}", sv["mix"], dy1, pl.BlockSpec((tr, D), lambda i, j, k: (k, 0)),
            pl.BlockSpec((tr, D), lambda i, j, k: (k, 0)), (2, N_SHARD, D // 8, D),
            pl.BlockSpec((2, N_SHARD, D // 8, D), lambda i, j, k: (0, 0, 0, 0)), 1, (D, D), steps, row_shards_epi(N_SHARD))[0]
        du, dkvl, dkvc, dsink = hosted(f"att_bwd{l}", lambda ride: _att_bwd(
            f"att_bwd{l}", sv["qkv"], sv["mix"], dmix, sv["lse"], small["attn_sink"][l], cos_t, sin_t, l != last, cfg, ride))
        acc = [du] + [jnp.zeros(sh, F32) for sh in _SMALL_SHAPES]
        acc = _convpool_bwd(f"convpool_bwd_lat{l}", sv["cpin"], sv["yconv"], dmix, sv["prm"], acc, cfg.seq, 0, cfg)
        if l != last:
            acc = _convpool_bwd(f"convpool_bwd_ctx{l}", sv["cpin"], sv["yconv"], dmix, sv["prm"], acc, cfg.ctx,
                                cfg.nl // cfg.ctx, cfg)
        du, g_dw, g_dwb, g_lng, g_lnb, g_wbd, g_ps = acc
        du = _place_kv(f"place_kv{l}", du, dkvl, dkvc, l != last, cfg)
        if l == last:
            du = _place_ctx_kv_only(f"place_ctx_kv{l}", du, dkvc, cfg)
        sg["attn_sink"][l] = dsink[:, 0]
        sg["conv_dw"][l], sg["conv_dw_b"][l], sg["conv_ln_g"][l], sg["conv_ln_b"][l] = g_dw[:CONV_K], g_dwb[0], g_lng[0], g_lnb[0]
        sg["pool_w"][l] = jnp.stack([g_wbd[g * 64:(g + 1) * 64, g * 64:(g + 1) * 64] for g in range(4)])
        sg["pool_scale"][l] = g_ps[0]
        tr_all = _dw_rows(cfg.r)
        big[l]["w_in"] = _mm_dw(
            f"dw_in{l}", sv["h1"], du, pl.BlockSpec((tr_all, D), lambda i, j, k: (k, 0)),
            pl.BlockSpec((tr_all, IN_W), lambda i, j, k: (k, 0)), (2, N_SHARD, D // 2, IN_SHARD),
            pl.BlockSpec((2, N_SHARD, D // 2, IN_SHARD), lambda i, j, k: (0, 0, 0, 0)), 1, (D, IN_W), cfg.r // tr_all,
            col_shards_epi)[0]
        below = (saved[l - 1]["y2"], saved[l - 1]["mod3"], 5) if l > 0 else None
        res = _proj_norm_bwd(
            f"in_proj_bwd{l}", du, pl.BlockSpec((TM, IN_W), lambda i: (i, 0)),
            lambda a_ref, j: a_ref[:, j * IN_SHARD:(j + 1) * IN_SHARD], wl["w_in"],
            sv["x0"], dx1, small["norm1_g"][l][None], mod3, 1, nb_all, l == last, below, cfg, dx_latent_only=l == 0)
        dx, dsh1, dsc1, dn1 = res[:4]
        sg["norm1_g"][l], sg["norm2_g"][l] = dn1[0], dn2[0]
        parts = [dsh1, dsc1, dg1, dsh2, dsc2, dg2]
        if below is not None:
            dy2, dg2 = res[4:]
        dm = jnp.concatenate([t[:cfg.b, 0, :] for t in parts], axis=1)
        live = (0, 1) if l == last else range(6)
        dm_ctx = jnp.concatenate([t[cfg.b, 0, :] if i in live else jnp.zeros((D,), F32) for i, t in enumerate(parts)])
        dms[l] = jnp.concatenate([dm, dm_ctx[None, :], jnp.zeros((MOD_ROWS - cfg.b - 1, 6 * D), F32)], axis=0)

    grad_x = dx.reshape(x.shape)
    small_grads = {k: jnp.stack(v) for k, v in sg.items()}
    small_grads["final_g"] = d_final_g[0]
    return loss, grad_x, small_grads, dms


_BIG = ("w_in", "w_out", "w_ffn_in", "w_ffn_out")
_TAPS = "conv_dw"
_SMALL = ("c_ctx", "b_mod", "norm1_g", "norm2_g", "conv_dw", "conv_dw_b", "conv_ln_g", "conv_ln_b", "attn_sink",
          "pool_w", "pool_scale", "final_g")
_ORDER = ("c_ctx", "w_mod", "b_mod", "norm1_g", "norm2_g", "w_in", "conv_dw", "conv_dw_b", "conv_ln_g", "conv_ln_b",
          "attn_sink", "pool_w", "pool_scale", "w_out", "w_ffn_in", "w_ffn_out", "final_g")
_GATHER_HOSTS = {
    "first": ((0, "w_in"),),
    "att_fwd0": ((0, "w_out"), (0, "w_ffn_in"), (0, "w_ffn_out"), (0, _TAPS)),
    "ffn_in0": tuple((1, k) for k in _BIG + (_TAPS,)),
}
_REDUCE_HOSTS = {
    "dw_ffn_in0": tuple((1, k) for k in _BIG),
    "att_bwd0": ((0, "w_ffn_in"), (0, "w_ffn_out"), (0, "w_out")),
    "last": ((0, "w_in"),),
}


class _Comm:
    def __init__(self, w, c_idx, jc_idx):
        self.shapes = {k: w[k].shape[1:] for k in _BIG}
        self.c_idx, self.jc_idx = c_idx, jc_idx
        halves = lambda a: a.reshape(2, a.shape[0] // 2, a.shape[1])
        taps = jnp.pad(w[_TAPS], ((0, 0), (0, 1), (0, 64)))
        cast = {k: w[k].astype(BF16) for k in _BIG}
        self.shards = [{**{k: halves(cast[k][l]) for k in _BIG}, _TAPS: halves(taps[l])} for l in range(2)]
        self.wf = [dict(), dict()]
        self.grads = [dict(), dict()]
        self.reduced = [dict(), dict()]
        self._open = {}
        self.landed("first", _run_ride("gather_first", self.ride("first")))

    def ride(self, host):
        if host in _GATHER_HOSTS:
            return _gather_ride([self.shards[layer][k] for layer, k in _GATHER_HOSTS[host]])
        if host in _REDUCE_HOSTS:
            what = _REDUCE_HOSTS[host]
            mine = [self.grads[layer][k] for layer, k in what]
            other = _send_other_half(f"send_other_half_{host}", mine)
            pair = [_pair_add(f"pair_add{layer}_{k}", a, b, self.c_idx) for (layer, k), a, b in zip(what, mine, other)]
            self._open[host] = pair
            return _exchange_ride(pair)
        return None

    def landed(self, host, got):
        if host in _GATHER_HOSTS:
            for (layer, k), f in zip(_GATHER_HOSTS[host], got):
                if k == _TAPS:
                    taps = f.reshape(N_SHARD, 32, 128)[:, :CONV_K, :64]
                    self.wf[layer][k] = jnp.transpose(taps, (1, 0, 2)).reshape(CONV_K, CONV_W)
                else:
                    self.wf[layer][k] = f.reshape((1, N_SHARD) + self.shapes[k])
        if host in _REDUCE_HOSTS:
            what = _REDUCE_HOSTS[host]
            mine = [_shard_sum(f"shard_sum{layer}_{k}", a, b, self.jc_idx)
                    for (layer, k), a, b in zip(what, self._open.pop(host), got)]
            for (layer, k), g in zip(what, _swap_reduced(f"swap_reduced_{host}", mine)):
                self.reduced[layer][k] = g.reshape(self.shapes[k])


def _conditioning(c, c_ctx, w_mod, b_mod, chip):
    b = c.shape[0]
    block = jnp.concatenate([c, c_ctx[None, :], jnp.zeros((8 - b - 1, D), F32)], axis=0)
    c_all = _share_small("share_c", block, False).reshape(64, D)
    bias = lax.dynamic_slice_in_dim(b_mod, chip * MOD_W, MOD_W, axis=1)
    full = lambda r, q: pl.BlockSpec((r, q), lambda i, j, k: (0, 0))

    def bias_epi(acc, ex, outs):
        outs[0][...] = acc + ex[0][...]

    mv = [_mm(f"mod_fwd{l}", "nn", (1, 1, 1), c_all, w_mod[l], full(64, D), full(D, MOD_W),
              [jax.ShapeDtypeStruct((64, MOD_W), F32)], [full(64, MOD_W)], extras=[bias[l][None]],
              extra_specs=[full(1, MOD_W)], a_fn=_silu, epi=bias_epi)[0] for l in range(2)]
    by_dev = jnp.transpose(jnp.stack(mv).reshape(2, 8, 8, MOD_W), (1, 0, 2, 3))
    rows = jnp.transpose(_mod_rows_exchange(by_dev), (1, 2, 0, 3)).reshape(2, 8, 6 * D)
    rows = jnp.pad(rows, ((0, 0), (0, MOD_ROWS - 8), (0, 0)))
    return [rows[l].reshape(MOD_ROWS, 6, D) for l in range(2)], c_all


def _conditioning_bwd(dms, c_all, w_mod, b):
    dm = jnp.stack([d[:8] for d in dms])
    by_chip = jnp.transpose(dm.reshape(2, 8, N_SHARD, MOD_W), (2, 0, 1, 3))
    gathered, d_bias = _mod_grad_exchange(by_chip, dm)
    dm_all = jnp.transpose(gathered, (1, 0, 2, 3)).reshape(2, 64, MOD_W)
    full = lambda r, q: pl.BlockSpec((r, q), lambda i, j, k: (0, 0))

    def ctx_rows_epi(acc, ex, outs):
        row = lax.broadcasted_iota(jnp.int32, acc.shape, 0) % 8
        outs[0][...] = _colsum(jnp.where(row == b, acc * _dsilu(ex[0][...]), 0.0))

    g_mod, d_ctx = [], jnp.zeros((D,), F32)
    for l in range(2):
        g_mod.append(_mm(f"dw_mod{l}", "tn", (1, 1, 1), c_all, dm_all[l], full(64, D), full(64, MOD_W),
                         [jax.ShapeDtypeStruct((D, MOD_W), F32)], [full(D, MOD_W)], a_fn=_silu)[0])
        part = _mm(f"mod_bwd{l}", "nt", (1, 1, 1), dm_all[l], w_mod[l], full(64, MOD_W), full(D, MOD_W),
                   [jax.ShapeDtypeStruct((1, D), F32)], [full(1, D)], extras=[c_all], extra_specs=[full(64, D)],
                   epi=ctx_rows_epi)[0]
        d_ctx = d_ctx + part[0]
    return g_mod, d_bias[:, 0, :], d_ctx


def kernel(x, c, ctx, c_ctx, w_mod, b_mod, norm1_g, norm2_g, w_in, conv_dw, conv_dw_b, conv_ln_g, conv_ln_b, attn_sink, pool_w, pool_scale, w_out, w_ffn_in, w_ffn_out, final_g, loss_target, m_c_ctx, m_w_mod, m_b_mod, m_norm1_g, m_norm2_g, m_w_in, m_conv_dw, m_conv_dw_b, m_conv_ln_g, m_conv_ln_b, m_attn_sink, m_pool_w, m_pool_scale, m_w_out, m_w_ffn_in, m_w_ffn_out, m_final_g, v_c_ctx, v_w_mod, v_b_mod, v_norm1_g, v_norm2_g, v_w_in, v_conv_dw, v_conv_dw_b, v_conv_ln_g, v_conv_ln_b, v_attn_sink, v_pool_w, v_pool_scale, v_w_out, v_w_ffn_in, v_w_ffn_out, v_final_g):
    w = dict(c_ctx=c_ctx, w_mod=w_mod, b_mod=b_mod, norm1_g=norm1_g, norm2_g=norm2_g, w_in=w_in, conv_dw=conv_dw,
             conv_dw_b=conv_dw_b, conv_ln_g=conv_ln_g, conv_ln_b=conv_ln_b, attn_sink=attn_sink, pool_w=pool_w,
             pool_scale=pool_scale, w_out=w_out, w_ffn_in=w_ffn_in, w_ffn_out=w_ffn_out, final_g=final_g)
    m = dict(c_ctx=m_c_ctx, w_mod=m_w_mod, b_mod=m_b_mod, norm1_g=m_norm1_g, norm2_g=m_norm2_g, w_in=m_w_in,
             conv_dw=m_conv_dw, conv_dw_b=m_conv_dw_b, conv_ln_g=m_conv_ln_g, conv_ln_b=m_conv_ln_b,
             attn_sink=m_attn_sink, pool_w=m_pool_w, pool_scale=m_pool_scale, w_out=m_w_out, w_ffn_in=m_w_ffn_in,
             w_ffn_out=m_w_ffn_out, final_g=m_final_g)
    v = dict(c_ctx=v_c_ctx, w_mod=v_w_mod, b_mod=v_b_mod, norm1_g=v_norm1_g, norm2_g=v_norm2_g, w_in=v_w_in,
             conv_dw=v_conv_dw, conv_dw_b=v_conv_dw_b, conv_ln_g=v_conv_ln_g, conv_ln_b=v_conv_ln_b,
             attn_sink=v_attn_sink, pool_w=v_pool_w, pool_scale=v_pool_scale, w_out=v_w_out, w_ffn_in=v_w_ffn_in,
             w_ffn_out=v_w_ffn_out, final_g=v_final_g)
    xi, yi, ci = _position()
    chip = 2 * xi + yi
    mod3s, c_all = _conditioning(c, c_ctx, w_mod, b_mod, chip)
    comm = _Comm(w, jnp.reshape(ci, (1,)).astype(jnp.int32), jnp.stack([chip, ci]).astype(jnp.int32))
    small = {k: w[k] for k in _SMALL if k not in ("conv_dw", "c_ctx", "b_mod")}
    loss, grad_x, sgrads, dms = _local_step(x, ctx, small, mod3s, loss_target, comm)
    comm.landed("last", _run_ride("exchange_last", comm.ride("last")))
    g_mod, sgrads["b_mod"], d_ctx = _conditioning_bwd(dms, c_all, w_mod, c.shape[0])
    sgrads["c_ctx"] = 0.5 * d_ctx

    names = list(_SMALL)
    total = _share_small("sum_small", _pack([loss] + [sgrads[k] for k in names]), True)
    parts = _unpack(total, [()] + [sgrads[k].shape for k in names])
    loss_out = parts[0]
    gsmall = dict(zip(names, parts[1:]))
    gsmall["conv_dw"] = lax.dynamic_slice_in_dim(gsmall["conv_dw"], chip * 64, 64, axis=2)

    grads, delta, new_m, new_v = dict(gsmall), {}, {}, {}
    for k in ("w_mod",) + _BIG:
        s0, s1 = w[k].shape[1:]
        flat = lambda a: a.reshape(2 * s0, s1)
        g_layers = g_mod if k == "w_mod" else [comm.reduced[l][k] for l in range(2)]
        outs = _adamw_layers(f"adamw_{k}", flat(w[k]), g_layers, flat(m[k]), flat(v[k]))
        grads[k], delta[k], new_m[k], new_v[k] = [a.reshape(w[k].shape) for a in outs]
    d_, m_, v_ = _adamw("adamw_small", _pack([w[k] for k in names]), _pack([gsmall[k] for k in names]),
                        _pack([m[k] for k in names]), _pack([v[k] for k in names]))
    sshapes = [w[k].shape for k in names]
    for k, a, b, e in zip(names, _unpack(d_, sshapes), _unpack(m_, sshapes), _unpack(v_, sshapes)):
        delta[k], new_m[k], new_v[k] = a, b, e
    return (loss_out, grad_x, *[grads[k] for k in _ORDER], *[delta[k] for k in _ORDER],
            *[new_m[k] for k in _ORDER], *[new_v[k] for k in _ORDER])
```

```python
from typing import NamedTuple

import jax
import jax.numpy as jnp
import numpy as np
from jax import lax
from jax.experimental import pallas as pl
from jax.experimental.pallas import tpu as pltpu

F32 = jnp.float32
BF16 = jnp.bfloat16

D = 1024
GRID_W = 64
HEAD_DIM = 64
N_HEADS = 8
ATTN_W = 512
CONV_W = 256
POOL_W = 256
IN_W = 1536
D_FF = 2816
CONV_K = 31
QB = 128
ROPE_BASE = 10000.0
EPS = 1e-6
NEG = -1e30
N_SHARD = 4
IN_SHARD = IN_W // N_SHARD
HALF_FF = D_FF // 2
MOD_W = 6 * D // N_SHARD
MOD_ROWS = 16
PACK_LANES = 128

ADAM_LR = 0.001
ADAM_B1 = 0.9
ADAM_B2 = 0.999
ADAM_EPS = 1e-08
ADAM_WD = 0.01
ADAM_STEP = 10

VMEM_LIMIT_V7X = 56 * 1024 * 1024
TM = 512
TR_MAX = 1024
CHUNK = 256
PAD = 16

_MESH = pl.DeviceIdType.MESH
_ANY = pl.BlockSpec(memory_space=pl.ANY)
_DIMS = {"nn": (((1,), (0,)), ((), ())), "nt": (((1,), (1,)), ((), ())), "tn": (((0,), (0,)), ((), ()))}


class _Cfg(NamedTuple):
    b: int
    seq: int
    ctx: int

    @property
    def nl(self):
        return self.b * self.seq

    @property
    def nc(self):
        return self.b * self.ctx

    @property
    def r(self):
        return self.nl + self.nc

    def mod_row(self, i):
        return jnp.where(i < self.nl // TM, i // (self.seq // TM), self.b)

    def first_of_row(self, i):
        nlb = self.nl // TM
        return jnp.logical_or(jnp.logical_and(i < nlb, i % (self.seq // TM) == 0), i == nlb)


def _params(n_grid=0):
    sem = ("arbitrary",) * n_grid if n_grid else None
    return pltpu.CompilerParams(dimension_semantics=sem, vmem_limit_bytes=VMEM_LIMIT_V7X)


def _dot(a, b, mode="nn"):
    return lax.dot_general(a.astype(BF16), b.astype(BF16), _DIMS[mode], preferred_element_type=F32)


def _sigmoid(x):
    return 1.0 / (1.0 + jnp.exp(-x))


def _silu(x):
    return x * _sigmoid(x)


def _dsilu(x):
    s = _sigmoid(x)
    return s * (1.0 + x * (1.0 - s))


def _colsum(v):
    return jnp.sum(v, axis=0, keepdims=True)


def _dw_rows(rows):
    return TR_MAX if rows % TR_MAX == 0 else TM


def _epi_store(acc, ex, outs):
    for o in outs:
        o[...] = acc.astype(o.dtype)


class _Ride(NamedTuple):
    ins: list
    out_shape: list
    scratch: list
    start: object
    finish: object


class _Hosted(NamedTuple):
    ride: _Ride
    n_in: int
    n_out: int
    grid: tuple

    def split(self, refs):
        n_ri, n_ro, n_rs = len(self.ride.ins), len(self.ride.out_shape), len(self.ride.scratch)
        r_in = refs[self.n_in:self.n_in + n_ri]
        r_out = refs[self.n_in + n_ri + self.n_out:self.n_in + n_ri + self.n_out + n_ro]
        own = refs[:self.n_in] + refs[self.n_in + n_ri:self.n_in + n_ri + self.n_out] + \
            refs[self.n_in + n_ri + self.n_out + n_ro:len(refs) - n_rs]
        return own, (r_in, r_out, refs[len(refs) - n_rs:])

    def start(self, parts):
        ids = [pl.program_id(d) for d in range(len(self.grid))]
        first = ids[0] == 0
        for i in ids[1:]:
            first = jnp.logical_and(first, i == 0)
        pl.when(first)(lambda: self.ride.start(*parts))

    def finish(self, parts):
        ids = [pl.program_id(d) for d in range(len(self.grid))]
        last = ids[0] == self.grid[0] - 1
        for i, g in zip(ids[1:], self.grid[1:]):
            last = jnp.logical_and(last, i == g - 1)
        pl.when(last)(lambda: self.ride.finish(*parts))


def _hosted_call(body, ride, name, grid, ins, in_specs, out_shape, out_specs, scratch, params):
    if ride is None:
        res = pl.pallas_call(body, name=name, grid=grid, in_specs=in_specs, out_specs=out_specs, out_shape=out_shape,
                             scratch_shapes=scratch, compiler_params=params)(*ins)
        return list(res), []
    host = _Hosted(ride, len(ins), len(out_shape), tuple(grid))

    def hosted(*refs):
        own, parts = host.split(refs)
        host.start(parts)
        body(*own)
        host.finish(parts)

    res = pl.pallas_call(
        hosted, name=name, grid=grid, in_specs=list(in_specs) + [_ANY] * len(ride.ins),
        out_specs=list(out_specs) + [_ANY] * len(ride.out_shape), out_shape=list(out_shape) + list(ride.out_shape),
        scratch_shapes=list(scratch) + list(ride.scratch), compiler_params=params)(*ins, *ride.ins)
    return list(res[:len(out_shape)]), list(res[len(out_shape):])


def _mm(name, mode, grid, a, b, a_spec, b_spec, out_shape, out_specs, acc_shape=None, extras=(),
        extra_specs=(), a_fn=None, epi=_epi_store, ride=None):
    nk = grid[2]
    n_ex, n_out = len(extras), len(out_shape)

    def body(*refs):
        a_ref, b_ref = refs[:2]
        ex = refs[2:2 + n_ex]
        outs = refs[2 + n_ex:2 + n_ex + n_out]
        av = a_ref[...]
        if a_fn is not None:
            av = a_fn(av)
        part = _dot(av, b_ref[...], mode)
        if nk == 1:
            epi(part, ex, outs)
        else:
            acc = refs[-1]
            k = pl.program_id(2)

            @pl.when(k == 0)
            def _():
                acc[...] = part

            @pl.when(k > 0)
            def _():
                acc[...] += part

            @pl.when(k == nk - 1)
            def _():
                epi(acc[...], ex, outs)

    scratch = [] if nk == 1 else [pltpu.VMEM(acc_shape, F32)]
    outs, ride_outs = _hosted_call(body, ride, name, grid, [a, b, *extras], [a_spec, b_spec, *extra_specs],
                                   list(out_shape), list(out_specs), scratch, _params(3))
    return outs if ride is None else (outs, ride_outs)


def _mm_dw(name, a, b, a_spec, b_spec, out_shape, out_spec, n_out_blocks, acc_shape, n_steps, epi, a_fn=None,
           extras=(), extra_specs=(), extra_out=(), extra_out_specs=(), ride=None):
    grid = (1, n_out_blocks, n_steps)
    outs = [jax.ShapeDtypeStruct(out_shape, BF16)] + list(extra_out)
    return _mm(name, "tn", grid, a, b, a_spec, b_spec, outs, [out_spec, *extra_out_specs], acc_shape, extras,
               extra_specs, a_fn, epi, ride)


def _gate_step(i, dxv, y_ref, m_ref, ig, dy_ref, dgate_ref, cfg):
    dy_ref[...] = (dxv * m_ref[ig:ig + 1, :]).astype(BF16)
    _accumulate_rows(cfg.first_of_row(i), dgate_ref, _colsum(dxv * y_ref[...].astype(F32)))


def _proj_norm_bwd(name, a, a_spec, pick, w, x, dres, gvec, mod3, isc, nblk, res_latent_only, gate, cfg,
                   dx_latent_only=False):
    ns = w.shape[-1]
    nlb = cfg.nl // TM

    def body(a_ref, w_ref, x_ref, dres_ref, g_ref, m_ref, *rest):
        if gate is None:
            dx_ref, dsh_ref, dsc_ref, dg_ref = rest
        else:
            y_ref, gm_ref, dx_ref, dsh_ref, dsc_ref, dg_ref, dy_ref, dgate_ref = rest
        i = pl.program_id(0)
        dhv = _dot(pick(a_ref, 0), w_ref[0], "nt")
        for j in range(1, N_SHARD):
            dhv = dhv + _dot(pick(a_ref, j), w_ref[j], "nt")
        xv = x_ref[...]
        r = lax.rsqrt(jnp.mean(xv * xv, axis=-1, keepdims=True) + EPS)
        xh = xv * r
        g = g_ref[...]
        sc1 = 1.0 + m_ref[isc:isc + 1, :]
        t = dhv * xh
        first = cfg.first_of_row(i)
        _accumulate_rows(first, dsh_ref, _colsum(dhv))
        _accumulate_rows(first, dsc_ref, _colsum(t * g))
        _accumulate_rows(i == 0, dg_ref, _colsum(t * sc1))
        dxh = dhv * (g * sc1)
        dxn = r * (dxh - xh * jnp.mean(dxh * xh, axis=-1, keepdims=True))
        dxv = (jnp.where(i < nlb, dres_ref[...], 0.0) if res_latent_only else dres_ref[...]) + dxn
        if dx_latent_only:
            @pl.when(i < nlb)
            def _():
                dx_ref[...] = dxv
        else:
            dx_ref[...] = dxv
        if gate is not None:
            _gate_step(i, dxv, y_ref, gm_ref, gate[2], dy_ref, dgate_ref, cfg)

    row = pl.BlockSpec((TM, D), lambda i: (i, 0))
    vec = pl.BlockSpec((1, D), lambda i: (0, 0))
    part = pl.BlockSpec((None, 1, D), lambda i: (cfg.mod_row(i), 0, 0))
    part_shape = jax.ShapeDtypeStruct((MOD_ROWS, 1, D), F32)
    resident = pl.BlockSpec((None, N_SHARD, D, ns), lambda i: (0, 0, 0, 0), pipeline_mode=pl.Buffered(1))
    ins, in_specs = [a, w, x, dres, gvec, mod3], [a_spec, resident, row, row, vec, _mod_spec(cfg)]
    out_specs = [row, part, part, vec]
    out_shape = [jax.ShapeDtypeStruct((cfg.r, D), F32), part_shape, part_shape, jax.ShapeDtypeStruct((1, D), F32)]
    if dx_latent_only:
        out_specs[0] = pl.BlockSpec((TM, D), lambda i: (jnp.minimum(i, nlb - 1), 0))
        out_shape[0] = jax.ShapeDtypeStruct((cfg.nl, D), F32)
    if gate is not None:
        ins, in_specs = ins + list(gate[:2]), in_specs + [row, _mod_spec(cfg)]
        out_specs, out_shape = out_specs + [row, part], out_shape + [jax.ShapeDtypeStruct((cfg.r, D), BF16), part_shape]
    return pl.pallas_call(body, name=name, grid=(nblk,), in_specs=in_specs, out_specs=out_specs, out_shape=out_shape,
                          compiler_params=_params(1))(*ins)


def _mod_spec(cfg):
    return pl.BlockSpec((None, 6, D), lambda i: (cfg.mod_row(i), 0, 0))


def _norm_mod(xv, g, m_ref, ish, isc):
    r = lax.rsqrt(jnp.mean(xv * xv, axis=-1, keepdims=True) + EPS)
    return (xv * r * g * (1.0 + m_ref[isc:isc + 1, :]) + m_ref[ish:ish + 1, :]).astype(BF16)


def _accumulate_rows(first, ref, val):
    @pl.when(first)
    def _():
        ref[...] = val

    @pl.when(jnp.logical_not(first))
    def _():
        ref[...] += val


def _rope_tables(seq):
    rows = seq // GRID_W
    row = jnp.repeat(jnp.arange(rows), GRID_W).astype(F32)
    col = jnp.tile(jnp.arange(GRID_W), rows).astype(F32)
    half = HEAD_DIM // 2
    inv = ROPE_BASE ** (-jnp.arange(0, half, 2, dtype=F32) / half)
    ar, ac = row[:, None] * inv, col[:, None] * inv
    ang = jnp.concatenate([ar, ar, ac, ac], axis=-1)
    sign = jnp.tile(jnp.concatenate([-jnp.ones((16,), F32), jnp.ones((16,), F32)]), 2)
    cos = jnp.tile(jnp.cos(ang), (1, 2))
    sin = jnp.tile(jnp.sin(ang) * sign, (1, 2))
    cos = jnp.concatenate([cos, jnp.ones((TM, 2 * HEAD_DIM), F32)], axis=0)
    sin = jnp.concatenate([sin, jnp.zeros((TM, 2 * HEAD_DIM), F32)], axis=0)
    return cos, sin


def _rope(x, cos, sin_signed, sign):
    lane = lax.broadcasted_iota(jnp.int32, x.shape, 1)
    low = (lane % 32) < 16
    rot = jnp.where(low, pltpu.roll(x, 112, 1), pltpu.roll(x, 16, 1))
    return x * cos + sign * (rot * sin_signed)


def _in_proj(name, x, gvec, mod3, w_in, layer, cos_t, sin_t, cfg):
    nlb, bps = cfg.nl // TM, cfg.seq // TM

    def body(x_ref, g_ref, m_ref, w_ref, cos_ref, sin_ref, h_ref, qkv_ref, cp_ref):
        hv = _norm_mod(x_ref[...], g_ref[...], m_ref, 0, 1)
        h_ref[...] = hv
        u = jnp.concatenate([_dot(hv, w_ref[j]) for j in range(N_SHARD)], axis=1)
        cos, sin = cos_ref[...], sin_ref[...]
        tiles = []
        for t in range(5):
            y = _rope(u[:, 128 * t:128 * (t + 1)], cos, sin, 1.0)
            tiles.append(y * (HEAD_DIM ** -0.5) if t < 4 else y)
        tiles.append(u[:, 640:768])
        qkv_ref[...] = jnp.concatenate(tiles, axis=1).astype(BF16)
        cp_ref[...] = u[:, 768:IN_W].astype(BF16)

    tab = pl.BlockSpec((TM, 128), lambda i: (jnp.where(i < nlb, i % bps, bps), 0))
    half = pl.BlockSpec((TM, 768), lambda i: (i, 0))
    row = pl.BlockSpec((TM, D), lambda i: (i, 0))
    return pl.pallas_call(
        body, name=name, grid=(cfg.r // TM,),
        in_specs=[row, pl.BlockSpec((1, D), lambda i: (0, 0)), _mod_spec(cfg),
                  pl.BlockSpec((None, N_SHARD, D, IN_SHARD), lambda i: (layer, 0, 0, 0)), tab, tab],
        out_specs=[row, half, half],
        out_shape=[jax.ShapeDtypeStruct((cfg.r, D), BF16), jax.ShapeDtypeStruct((cfg.r, 768), BF16),
                   jax.ShapeDtypeStruct((cfg.r, 768), BF16)],
        compiler_params=_params(1))(x, gvec, mod3, w_in, cos_t, sin_t)


def _att_specs(cfg):
    nlb, ncb = cfg.seq // QB, cfg.ctx // QB

    def qblk(s, qb):
        return jnp.where(qb < nlb, s * nlb + qb, cfg.nl // QB + s * ncb + qb - nlb)

    def near(off, col):
        return pl.BlockSpec((QB, 128), lambda s, qb: (s * nlb + jnp.clip(qb + off, 0, nlb - 1), col))

    def ctxs(col):
        return pl.BlockSpec((cfg.ctx, 128), lambda s, qb: (cfg.nl // cfg.ctx + s, col))

    qspec = pl.BlockSpec((QB, ATTN_W), lambda s, qb: (qblk(s, qb), 0))
    kv = [ctxs(4), ctxs(5), near(-1, 4), near(0, 4), near(1, 4), near(-1, 5), near(0, 5), near(1, 5)]
    return qblk, qspec, kv


def _att_scores(qb, nlb, sink_ref, q_ref, k_refs, v_refs, kh):
    is_lat = qb < nlb
    ii = lax.broadcasted_iota(jnp.int32, (4 * QB, 3 * QB), 0) % QB
    col = lax.broadcasted_iota(jnp.int32, (4 * QB, 3 * QB), 1)
    jj, blk = col % QB, col // QB
    off_p = jnp.where(jnp.logical_and(is_lat, qb >= 1), 0.0, NEG)
    off_c = jnp.where(is_lat, 0.0, NEG)
    off_n = jnp.where(jnp.logical_and(is_lat, qb <= nlb - 2), 0.0, NEG)
    inside = jnp.logical_or(blk == 1, jnp.logical_or(jnp.logical_and(blk == 0, jj >= ii),
                                                     jnp.logical_and(blk == 2, jj <= ii)))
    off = jnp.where(blk == 0, off_p, jnp.where(blk == 1, off_c, off_n))
    q4 = jnp.concatenate([q_ref[:, (4 * kh + g) * HEAD_DIM:(4 * kh + g + 1) * HEAD_DIM] for g in range(4)], axis=0)
    rg = lax.broadcasted_iota(jnp.int32, (4 * QB, 1), 0) // QB
    snk = jnp.where(rg == 0, sink_ref[4 * kh],
                    jnp.where(rg == 1, sink_ref[4 * kh + 1], jnp.where(rg == 2, sink_ref[4 * kh + 2], sink_ref[4 * kh + 3])))
    lanes = slice(kh * HEAD_DIM, (kh + 1) * HEAD_DIM)
    kx, vx = k_refs[0][:, lanes], v_refs[0][:, lanes]
    kl = jnp.concatenate([r[:, lanes] for r in k_refs[1:]], axis=0)
    vl = jnp.concatenate([r[:, lanes] for r in v_refs[1:]], axis=0)
    sx = _dot(q4, kx, "nt")
    sl = jnp.where(inside, _dot(q4, kl, "nt"), NEG) + off
    return q4, snk, (kx, kl), (vx, vl), (sx, sl)


def _att_fwd(name, qkv, sink, ctx_queries, cfg, ride=None):
    nlb, ncb = cfg.seq // QB, cfg.ctx // QB
    qblk, qspec, kvspecs = _att_specs(cfg)

    def body(sink_ref, q_ref, kx_ref, vx_ref, kp_ref, kc_ref, kn_ref, vp_ref, vc_ref, vn_ref, o_ref, lse_ref):
        qb = pl.program_id(1)
        for kh in range(2):
            q4, snk, _, vs, ss = _att_scores(qb, nlb, sink_ref, q_ref, (kx_ref, kp_ref, kc_ref, kn_ref),
                                             (vx_ref, vp_ref, vc_ref, vn_ref), kh)
            m = snk
            for s_ in ss:
                m = jnp.maximum(m, jnp.max(s_, axis=-1, keepdims=True))
            den = jnp.exp(snk - m)
            o4 = jnp.zeros((4 * QB, HEAD_DIM), F32)
            for s_, v_ in zip(ss, vs):
                p = jnp.exp(s_ - m)
                den = den + jnp.sum(p, axis=-1, keepdims=True)
                o4 = o4 + _dot(p, v_)
            o4 = o4 / den
            lse = m + jnp.log(den)
            for g in range(4):
                h = 4 * kh + g
                o_ref[:, h * HEAD_DIM:(h + 1) * HEAD_DIM] = o4[g * QB:(g + 1) * QB].astype(BF16)
                lse_ref[:, h:h + 1] = lse[g * QB:(g + 1) * QB]

    return _hosted_call(
        body, ride, name, (cfg.b, nlb + (ncb if ctx_queries else 0)), [sink] + [qkv] * 9,
        [pl.BlockSpec(memory_space=pltpu.SMEM), qspec, *kvspecs],
        [jax.ShapeDtypeStruct((cfg.r, D), BF16), jax.ShapeDtypeStruct((cfg.r, N_HEADS), F32)],
        [pl.BlockSpec((QB, ATTN_W), lambda s, qb: (qblk(s, qb), 0)),
         pl.BlockSpec((QB, N_HEADS), lambda s, qb: (qblk(s, qb), 0))], [], _params(2))


def _att_bwd(name, qkv, mix, dmix, lse, sink, cos_t, sin_t, ctx_queries, cfg, ride=None):
    nlb, ncb = cfg.seq // QB, cfg.ctx // QB
    nqb = nlb + (ncb if ctx_queries else 0)
    qblk, qspec, kvspecs = _att_specs(cfg)

    def body(sink_ref, q_ref, kx_ref, vx_ref, kp_ref, kc_ref, kn_ref, vp_ref, vc_ref, vn_ref, o_ref, do_ref,
             lse_ref, cosq_ref, sinq_ref, cosk_ref, sink_tab_ref, dq_ref, dkvl_ref, dkvc_ref, dsink_ref,
             accl, accc, dqs):
        s_id, qb = pl.program_id(0), pl.program_id(1)

        @pl.when(qb == 0)
        def _():
            accl[...] = jnp.zeros_like(accl)
            accc[...] = jnp.zeros_like(accc)

        @pl.when(jnp.logical_and(s_id == 0, qb == 0))
        def _():
            dsink_ref[...] = jnp.zeros_like(dsink_ref)

        starts = [pl.multiple_of(jnp.clip(qb + off, 0, nlb - 1) * QB, QB) for off in (-1, 0, 1)]
        for kh in range(2):
            q4, snk, ks, vs, ss = _att_scores(qb, nlb, sink_ref, q_ref, (kx_ref, kp_ref, kc_ref, kn_ref),
                                              (vx_ref, vp_ref, vc_ref, vn_ref), kh)
            lanes = slice(kh * HEAD_DIM, (kh + 1) * HEAD_DIM)
            heads =[slice((4 * kh + g) * HEAD_DIM, (4 * kh + g + 1) * HEAD_DIM) for g in range(4)]
            do4 = jnp.concatenate([do_ref[:, hs] for hs in heads], axis=0)
            o4 = jnp.concatenate([o_ref[:, hs] for hs in heads], axis=0).astype(F32)
            lse4 = jnp.concatenate([lse_ref[:, 4 * kh + g:4 * kh + g + 1] for g in range(4)], axis=0)
            delta = jnp.sum(do4 * o4, axis=-1, keepdims=True)
            dq4 = jnp.zeros((4 * QB, HEAD_DIM), F32)
            dks, dvs = [], []
            for s_, k_, v_ in zip(ss, ks, vs):
                p = jnp.exp(s_ - lse4)
                ds = p * (_dot(do4, v_, "nt") - delta)
                dq4 = dq4 + _dot(ds, k_)
                dks.append(_dot(ds, q4, "tn"))
                dvs.append(_dot(p, do4, "tn"))
            accc[:, lanes] += dks[0]
            accc[:, 128 + kh * HEAD_DIM:128 + (kh + 1) * HEAD_DIM] += dvs[0]
            for t, st in enumerate(starts):
                accl[pl.ds(st, QB), lanes] += dks[1][t * QB:(t + 1) * QB]
                accl[pl.ds(st, QB), 128 + kh * HEAD_DIM:128 + (kh + 1) * HEAD_DIM] += dvs[1][t * QB:(t + 1) * QB]
            dsk = -jnp.exp(snk - lse4) * delta
            for g in range(4):
                h = 4 * kh + g
                dsink_ref[h:h + 1, :] += jnp.broadcast_to(_colsum(dsk[g * QB:(g + 1) * QB]), (1, 128))
                dqs[:, heads[g]] = dq4[g * QB:(g + 1) * QB]
        cos, sin = cosq_ref[...], sinq_ref[...]
        dq_ref[...] = jnp.concatenate(
            [_rope(dqs[:, 128 * t:128 * (t + 1)], cos, sin, -1.0) * (HEAD_DIM ** -0.5) for t in range(4)],
            axis=1).astype(BF16)

        @pl.when(qb == nqb - 1)
        def _():
            dk = _rope(accl[:, 0:128], cosk_ref[...], sink_tab_ref[...], -1.0)
            dkvl_ref[...] = jnp.concatenate([dk, accl[:, 128:256]], axis=1).astype(BF16)
            dkvc_ref[...] = accc[...].astype(BF16)

    rowq = lambda w: pl.BlockSpec((QB, w), lambda s, qb: (qblk(s, qb), 0))
    tabq = pl.BlockSpec((QB, 128), lambda s, qb: (jnp.where(qb < nlb, qb, cfg.seq // QB), 0))
    tabk = pl.BlockSpec((cfg.seq, 128), lambda s, qb: (0, 0))
    return _hosted_call(
        body, ride, name, (cfg.b, nqb), [sink] + [qkv] * 9 + [mix, dmix, lse, cos_t, sin_t, cos_t, sin_t],
        [pl.BlockSpec(memory_space=pltpu.SMEM), qspec, *kvspecs, rowq(ATTN_W), rowq(ATTN_W), rowq(N_HEADS),
         tabq, tabq, tabk, tabk],
        [jax.ShapeDtypeStruct((cfg.r, IN_W), BF16), jax.ShapeDtypeStruct((cfg.nl, 256), BF16),
         jax.ShapeDtypeStruct((cfg.nc, 256), BF16), jax.ShapeDtypeStruct((N_HEADS, 128), F32)],
        [rowq(ATTN_W), pl.BlockSpec((cfg.seq, 256), lambda s, qb: (s, 0)),
         pl.BlockSpec((cfg.ctx, 256), lambda s, qb: (s, 0)), pl.BlockSpec((N_HEADS, 128), lambda s, qb: (0, 0))],
        [pltpu.VMEM((cfg.seq, 256), F32), pltpu.VMEM((cfg.ctx, 256), F32), pltpu.VMEM((QB, ATTN_W), F32)], _params(2))


def _pool_geometry(n, c):
    lane = lax.broadcasted_iota(jnp.int32, (1, POOL_W), 1) // HEAD_DIM
    wl = jnp.where(lane == 0, 1, jnp.where(lane == 1, 2, jnp.where(lane == 2, 4, 8)))
    wr = wl - 1
    t = c * CHUNK + lax.broadcasted_iota(jnp.int32, (CHUNK, POOL_W), 0)
    cnt = (jnp.minimum(t + wr, n - 1) - jnp.maximum(t - wl, 0) + 1).astype(F32)
    return wl, wr, cnt


def _build_phases(src, ph, c):
    for s in range(1, 8):
        ph[s - 1] = src[c * CHUNK + s:c * CHUNK + s + CHUNK + 24, :]


def _window(src, ph, c, off):
    a, s = divmod(off, 8)
    if s == 0:
        return src[c * CHUNK + 8 * a:c * CHUNK + 8 * a + CHUNK, :]
    return ph[s - 1, 8 * a:8 * a + CHUNK, :]


def _conv_chunk(hp, ph, dw_ref, dwb_ref, c):
    _build_phases(hp, ph, c)
    acc = jnp.zeros((CHUNK, CONV_W), F32) + dwb_ref[...]
    for j in range(CONV_K):
        acc = acc + dw_ref[j:j + 1, :] * _window(hp, ph, c, j + 1)
    return acc


def _fill_glu(cp_ref, hp, n):
    hp[0:PAD, :] = jnp.zeros((PAD, CONV_W), F32)
    hp[PAD + n:2 * PAD + n, :] = jnp.zeros((PAD, CONV_W), F32)
    for c in range(n // CHUNK):
        rows = slice(c * CHUNK, (c + 1) * CHUNK)
        a = cp_ref[rows, 0:CONV_W].astype(F32)
        g = cp_ref[rows, CONV_W:2 * CONV_W].astype(F32)
        hp[PAD + c * CHUNK:PAD + (c + 1) * CHUNK, :] = a * _sigmoid(g)


def _fill_pool(cp_ref, pp, n):
    pp[0:PAD, :] = jnp.zeros((PAD, POOL_W), F32)
    pp[PAD + n:2 * PAD + n, :] = jnp.zeros((PAD, POOL_W), F32)
    for c in range(n // CHUNK):
        pp[PAD + c * CHUNK:PAD + (c + 1) * CHUNK, :] = cp_ref[c * CHUNK:(c + 1) * CHUNK, 2 * CONV_W:768].astype(F32)


LV = CHUNK + 2 * PAD
_LEVELS = pltpu.VMEM((3, LV + 16, POOL_W), F32)


def _clear_level_edges(lv):
    for b in range(3):
        lv[b, 0:8] = jnp.zeros((8, POOL_W), F32)
        lv[b, 8 + LV:16 + LV] = jnp.zeros((8, POOL_W), F32)


def _window_sums(src, lv, c, lead):
    lv[0, 8:8 + LV] = src[c * CHUNK:c * CHUNK + LV, :]
    lo = 7 if lead < 0 else 8
    lv[1, 8:8 + LV] = lv[0, lo:lo + LV] + lv[0, lo + 1:lo + 1 + LV]
    group = lax.broadcasted_iota(jnp.int32, (1, POOL_W), 1) // HEAD_DIM
    rows = slice(8 + PAD, 8 + PAD + CHUNK)
    res = lv[1, rows]
    cur = 1
    for g, s in ((1, 1), (2, 2), (3, 4)):
        nxt = 3 - cur
        lv[nxt, 8:8 + LV] = lv[cur, 8 - s:8 - s + LV] + lv[cur, 8 + s:8 + s + LV]
        res = jnp.where(group >= g, lv[nxt, rows], res)
        cur = nxt
    return res


def _pool_chunk(pp, lv, n, c):
    _, _, cnt = _pool_geometry(n, c)
    return _window_sums(pp, lv, c, -1) / cnt - pp[PAD + c * CHUNK:PAD + (c + 1) * CHUNK, :], cnt


def _seq_specs(n, blk_off, width, col=0):
    return pl.BlockSpec((n, width), lambda s: (blk_off + s, col))


def _full(shape):
    return pl.BlockSpec(shape, lambda s: (0,) * len(shape))


_PHASES = pltpu.VMEM((7, CHUNK + 24, CONV_W), F32)


def _convpool_fwd(name, cpin, mix, yconv, prm, n, blk_off, cfg):
    dw, dwb, lng, lnb, wbd, ps = prm
    n_alias = 1 if yconv is None else 2

    def body(*refs):
        cp_ref, dw_ref, dwb_ref, lng_ref, lnb_ref, wbd_ref, ps_ref = refs[:7]
        out_ref, y_ref, hp, pp, ph, lv = refs[7 + n_alias:]
        _fill_glu(cp_ref, hp, n)
        _fill_pool(cp_ref, pp, n)
        _clear_level_edges(lv)
        for c in range(n // CHUNK):
            rows = slice(c * CHUNK, (c + 1) * CHUNK)
            y = _conv_chunk(hp, ph, dw_ref, dwb_ref, c)
            y_ref[rows, :] = y
            d = y - jnp.mean(y, axis=-1, keepdims=True)
            hn = d * lax.rsqrt(jnp.mean(d * d, axis=-1, keepdims=True) + EPS) * lng_ref[...] + lnb_ref[...]
            out_ref[rows, 0:CONV_W] = (hn * _sigmoid(hn)).astype(BF16)
            yp, _ = _pool_chunk(pp, lv, n, c)
            out_ref[rows, CONV_W:2 * CONV_W] = (_dot(yp, wbd_ref[...]) * ps_ref[...]).astype(BF16)

    through = [mix] if yconv is None else [mix, yconv]
    return pl.pallas_call(
        body, name=name, grid=(cfg.b,),
        in_specs=[_seq_specs(n, blk_off, 768), _full((32, CONV_W)), _full((1, CONV_W)), _full((1, CONV_W)),
                  _full((1, CONV_W)), _full((POOL_W, POOL_W)), _full((1, POOL_W))] + [_ANY] * n_alias,
        out_specs=[_seq_specs(n, blk_off, 512, 1), _seq_specs(n, blk_off, CONV_W)],
        out_shape=[jax.ShapeDtypeStruct((cfg.r, D), BF16), jax.ShapeDtypeStruct((cfg.r, CONV_W), F32)],
        scratch_shapes=[pltpu.VMEM((n + 2 * PAD, CONV_W), F32), pltpu.VMEM((n + 2 * PAD, POOL_W), F32), _PHASES, _LEVELS],
        input_output_aliases={7 + i: i for i in range(n_alias)},
        compiler_params=_params(1))(cpin, dw, dwb, lng, lnb, wbd, ps, *through)


_SMALL_SHAPES = [(32, CONV_W), (1, CONV_W), (1, CONV_W), (1, CONV_W), (POOL_W, POOL_W), (1, POOL_W)]


def _convpool_bwd(name, cpin, yconv, dmix, prm, acc_in, n, blk_off, cfg):
    dw, dwb, lng, lnb, wbd, ps = prm
    nch = n // CHUNK

    def body(cp_ref, y_ref, dm_ref, dw_ref, dwb_ref, lng_ref, lnb_ref, wbd_ref, ps_ref, dcp_in,
             a_dw, a_dwb, a_lng, a_lnb, a_wbd, a_ps,
             dcp_ref, o_dw, o_dwb, o_lng, o_lnb, o_wbd, o_ps, hp, dyp, pp, wp, dyv, dwacc, ph, lv):
        s = pl.program_id(0)
        _clear_level_edges(lv)

        @pl.when(s == 0)
        def _():
            for o_, a_ in ((o_dw, a_dw), (o_dwb, a_dwb), (o_lng, a_lng), (o_lnb, a_lnb), (o_wbd, a_wbd), (o_ps, a_ps)):
                o_[...] = a_[...]
            dwacc[...] = jnp.zeros_like(dwacc)

        _fill_glu(cp_ref, hp, n)
        _fill_pool(cp_ref, pp, n)
        for ref in (dyp, wp):
            ref[0:PAD, :] = jnp.zeros((PAD, CONV_W), F32)
            ref[PAD + n:2 * PAD + n, :] = jnp.zeros((PAD, CONV_W), F32)
        for c in range(nch):
            rows = slice(c * CHUNK, (c + 1) * CHUNK)
            y = y_ref[rows, :]
            d = y - jnp.mean(y, axis=-1, keepdims=True)
            rstd = lax.rsqrt(jnp.mean(d * d, axis=-1, keepdims=True) + EPS)
            xh = d * rstd
            hn = xh * lng_ref[...] + lnb_ref[...]
            sg = _sigmoid(hn)
            dhn = dm_ref[rows, 0:CONV_W] * (sg * (1.0 + hn * (1.0 - sg)))
            o_lnb[...] += _colsum(dhn)
            o_lng[...] += _colsum(dhn * xh)
            dxh = dhn * lng_ref[...]
            dy = rstd * (dxh - jnp.mean(dxh, axis=-1, keepdims=True) - xh * jnp.mean(dxh * xh, axis=-1, keepdims=True))
            o_dwb[...] += _colsum(dy)
            dyp[PAD + c * CHUNK:PAD + (c + 1) * CHUNK, :] = dy
            _build_phases(hp, ph, c)
            for j in range(CONV_K):
                prod = dy * _window(hp, ph, c, j + 1)
                dwacc[8 * j:8 * j + 8, :] += jnp.sum(prod.reshape(CHUNK // 8, 8, CONV_W), axis=0)
            yp, cnt = _pool_chunk(pp, lv, n, c)
            dz = dm_ref[rows, CONV_W:2 * CONV_W]
            o_ps[...] += _colsum(dz * _dot(yp, wbd_ref[...]))
            dzs = dz * ps_ref[...]
            o_wbd[...] += _dot(yp, dzs, "tn")
            dv = _dot(dzs, wbd_ref[...], "nt")
            dyv[rows, :] = dv
            wp[PAD + c * CHUNK:PAD + (c + 1) * CHUNK, :] = dv / cnt
        for c in range(nch):
            rows = slice(c * CHUNK, (c + 1) * CHUNK)
            _build_phases(dyp, ph, c)
            dh = jnp.zeros((CHUNK, CONV_W), F32)
            for j in range(CONV_K):
                dh = dh + dw_ref[j:j + 1, :] * _window(dyp, ph, c, 31 - j)
            a = cp_ref[rows, 0:CONV_W].astype(F32)
            sg = _sigmoid(cp_ref[rows, CONV_W:2 * CONV_W].astype(F32))
            dcp_ref[rows, 0:CONV_W] = (dh * sg).astype(BF16)
            dcp_ref[rows, CONV_W:2 * CONV_W] = (dh * a * sg * (1.0 - sg)).astype(BF16)
            dcp_ref[rows, 2 * CONV_W:768] = (_window_sums(wp, lv, c, 1) - dyv[rows, :]).astype(BF16)

        @pl.when(s == cfg.b - 1)
        def _():
            for j in range(CONV_K):
                o_dw[j:j + 1, :] += _colsum(dwacc[8 * j:8 * j + 8, :])

    small_specs = [_full(sh) for sh in _SMALL_SHAPES]
    return pl.pallas_call(
        body, name=name, grid=(cfg.b,),
        in_specs=[_seq_specs(n, blk_off, 768), _seq_specs(n, blk_off, CONV_W), _seq_specs(n, blk_off, 512, 1),
                  *small_specs, _ANY, *small_specs],
        out_specs=[_seq_specs(n, blk_off, 768, 1), *small_specs],
        out_shape=[jax.ShapeDtypeStruct((cfg.r, IN_W), BF16)] + [jax.ShapeDtypeStruct(sh, F32) for sh in _SMALL_SHAPES],
        scratch_shapes=[pltpu.VMEM((n + 2 * PAD, CONV_W), F32), pltpu.VMEM((n + 2 * PAD, CONV_W), F32),
                        pltpu.VMEM((n + 2 * PAD, POOL_W), F32), pltpu.VMEM((n + 2 * PAD, POOL_W), F32),
                        pltpu.VMEM((n, POOL_W), F32), pltpu.VMEM((8 * 32, CONV_W), F32), _PHASES, _LEVELS],
        input_output_aliases={9: 0}, compiler_params=_params(1))(cpin, yconv, dmix, dw, dwb, lng, lnb, wbd, ps, *acc_in)


def _place_kv(name, du, dkvl, dkvc, with_ctx, cfg):
    nlb = cfg.nl // TM

    def body(l_ref, c_ref, du_in, o_ref):
        i = pl.program_id(0)
        o_ref[...] = jnp.where(i < nlb, l_ref[...], c_ref[...])

    return pl.pallas_call(
        body, name=name, grid=(cfg.r // TM if with_ctx else nlb,),
        in_specs=[pl.BlockSpec((TM, 256), lambda i: (jnp.minimum(i, nlb - 1), 0)),
                  pl.BlockSpec((TM, 256), lambda i: (jnp.maximum(i - nlb, 0), 0)), _ANY],
        out_specs=pl.BlockSpec((TM, 256), lambda i: (i, 2)), out_shape=jax.ShapeDtypeStruct((cfg.r, IN_W), BF16),
        input_output_aliases={2: 0}, compiler_params=_params(1))(dkvl, dkvc, du)


def _place_ctx_kv_only(name, du, dkvc, cfg):
    nlb = cfg.nl // TM

    def body(c_ref, du_in, o_ref):
        o_ref[...] = jnp.zeros_like(o_ref)
        o_ref[:, ATTN_W:ATTN_W + 256] = c_ref[...]

    return pl.pallas_call(
        body, name=name, grid=(cfg.nc // TM,), in_specs=[pl.BlockSpec((TM, 256), lambda i: (i, 0)), _ANY],
        out_specs=pl.BlockSpec((TM, IN_W), lambda i: (nlb + i, 0)), out_shape=jax.ShapeDtypeStruct((cfg.r, IN_W), BF16),
        input_output_aliases={1: 0}, compiler_params=_params(1))(dkvc, du)


def _ffn_in(name, h, w_ffn_in, layer, nblk, cfg, ride=None):
    def body(h_ref, wg_ref, wu_ref, fac_ref, act_ref):
        hv = h_ref[...]
        g = _dot(hv, wg_ref[...])
        u = _dot(hv, wu_ref[...])
        s = _sigmoid(g)
        gs = g * s
        fac_ref[0] = ((s + gs * (1.0 - s)) * u).astype(BF16)
        fac_ref[1] = gs.astype(BF16)
        act_ref[...] = (gs * u).astype(BF16)

    wspec = lambda base: pl.BlockSpec((None, None, D, HALF_FF), lambda j, i: (layer, base + j, 0, 0))
    return _hosted_call(
        body, ride, name, (2, nblk), [h, w_ffn_in, w_ffn_in],
        [pl.BlockSpec((TM, D), lambda j, i: (i, 0)), wspec(0), wspec(2)],
        [jax.ShapeDtypeStruct((2, cfg.r, D_FF), BF16), jax.ShapeDtypeStruct((cfg.r, D_FF), BF16)],
        [pl.BlockSpec((2, TM, HALF_FF), lambda j, i: (0, i, j)), pl.BlockSpec((TM, HALF_FF), lambda j, i: (i, j))],
        [], _params(2))


def _row_block(rows, cols, max_bytes=1 << 20):
    best = 16
    for t in range(16, rows + 1, 16):
        if rows % t == 0 and t * cols * 4 <= max_bytes:
            best = t
    assert rows % best == 0
    return best


def _pair_add(name, own32, recv, c_idx):
    _, _, s0, s1 = own32.shape
    tr = _row_block(s0, s1)

    def body(c_ref, a_ref, b_ref, o_ref):
        o_ref[...] = (a_ref[...].astype(F32) + b_ref[...].astype(F32)).astype(BF16)

    grid_spec = pltpu.PrefetchScalarGridSpec(
        num_scalar_prefetch=1, grid=(N_SHARD * s0 // tr,),
        in_specs=[pl.BlockSpec((None, tr, s1), lambda i, c: (c[0], i, 0)), pl.BlockSpec((tr, s1), lambda i, c: (i, 0))],
        out_specs=pl.BlockSpec((tr, s1), lambda i, c: (i, 0)))
    out = pl.pallas_call(body, name=name, grid_spec=grid_spec, out_shape=jax.ShapeDtypeStruct((N_SHARD * s0, s1), BF16),
                         compiler_params=_params(1))(c_idx, own32.reshape(2, N_SHARD * s0, s1), recv.reshape(N_SHARD * s0, s1))
    return out.reshape(N_SHARD, s0, s1)


def _shard_sum(name, pair_sum, recv, jc_idx):
    _, s0, s1 = pair_sum.shape
    tr = _row_block(s0, s1)

    def body(jc_ref, a_ref, b_ref, o_ref):
        o_ref[...] = ((a_ref[...].astype(F32) + b_ref[0].astype(F32)) + b_ref[1].astype(F32)) + b_ref[2].astype(F32)

    grid_spec = pltpu.PrefetchScalarGridSpec(
        num_scalar_prefetch=1, grid=(s0 // tr,),
        in_specs=[pl.BlockSpec((None, tr, s1), lambda i, jc: (jc[0], i, 0)), pl.BlockSpec((3, tr, s1), lambda i, jc: (0, i, 0))],
        out_specs=pl.BlockSpec((None, tr, s1), lambda i, jc: (jc[1], i, 0)))
    return pl.pallas_call(body, name=name, grid_spec=grid_spec, out_shape=jax.ShapeDtypeStruct((2, s0, s1), F32),
                          compiler_params=_params(1))(jc_idx, pair_sum, recv)


def _adamw_math(w, g, m, v):
    m = ADAM_B1 * m + (1.0 - ADAM_B1) * g
    v = ADAM_B2 * v + (1.0 - ADAM_B2) * (g * g)
    m_hat = m / (1.0 - ADAM_B1 ** ADAM_STEP)
    v_hat = v / (1.0 - ADAM_B2 ** ADAM_STEP)
    delta = -ADAM_LR * (m_hat / (jnp.sqrt(v_hat) + ADAM_EPS) + ADAM_WD * w)
    return delta, m, v


def _adamw(name, w, g, m, v):
    rows, cols = w.shape
    tr = rows if rows % 16 else _row_block(rows, cols, 1 << 19)

    def body(w_ref, g_ref, m_ref, v_ref, d_ref, mo_ref, vo_ref):
        d, mn, vn = _adamw_math(w_ref[...], g_ref[...], m_ref[...], v_ref[...])
        d_ref[...] = d
        mo_ref[...] = mn
        vo_ref[...] = vn

    spec = pl.BlockSpec((tr, cols), lambda i: (i, 0))
    shape = jax.ShapeDtypeStruct((rows, cols), F32)
    return pl.pallas_call(body, name=name, grid=(rows // tr,), in_specs=[spec] * 4, out_specs=[spec] * 3,
                          out_shape=[shape] * 3, compiler_params=_params(1))(w, g, m, v)


def _adamw_layers(name, w, g_layers, m, v):
    rows, cols = w.shape
    s0 = rows // 2
    tr = _row_block(s0, cols, 1 << 19)
    nb = s0 // tr

    def body(w_ref, g0_ref, g1_ref, m_ref, v_ref, g_ref, d_ref, mo_ref, vo_ref):
        g = jnp.where(pl.program_id(0) < nb, g0_ref[...], g1_ref[...])
        d, mn, vn = _adamw_math(w_ref[...], g, m_ref[...], v_ref[...])
        g_ref[...] = g
        d_ref[...] = d
        mo_ref[...] = mn
        vo_ref[...] = vn

    spec = pl.BlockSpec((tr, cols), lambda i: (i, 0))
    shape = jax.ShapeDtypeStruct((rows, cols), F32)
    return pl.pallas_call(
        body, name=name, grid=(2 * nb,),
        in_specs=[spec, pl.BlockSpec((tr, cols), lambda i: (jnp.minimum(i, nb - 1), 0)),
                  pl.BlockSpec((tr, cols), lambda i: (jnp.maximum(i - nb, 0), 0)), spec, spec],
        out_specs=[spec] * 4, out_shape=[shape] * 4, compiler_params=_params(1))(w, g_layers[0], g_layers[1], m, v)


def _position():
    return lax.axis_index("x"), lax.axis_index("y"), lax.axis_index("c")


def _other_chips(x, y):
    return [(1 - x, y), (x, 1 - y), (1 - x, 1 - y)]


def _run_ride(name, ride):
    n_in, n_out = len(ride.ins), len(ride.out_shape)

    def body(*refs):
        parts = (refs[:n_in], refs[n_in:n_in + n_out], refs[n_in + n_out:])
        ride.start(*parts)
        ride.finish(*parts)

    return pl.pallas_call(
        body, name=name, in_specs=[_ANY] * n_in, out_specs=[_ANY] * n_out, out_shape=ride.out_shape,
        scratch_shapes=ride.scratch, compiler_params=pltpu.CompilerParams(vmem_limit_bytes=VMEM_LIMIT_V7X))(*ride.ins)


def _gather_ride(shards):
    n = len(shards)

    def copies(ins, outs, scr):
        ssem, rsem = scr[n], scr[n + 1]
        x, y, c = _position()
        me, sibling = 2 * x + y, (x, y, 1 - c)

        def remote(src, dst, i, dev):
            return pltpu.make_async_remote_copy(src, dst, ssem.at[i], rsem.at[i], device_id=dev, device_id_type=_MESH)

        fetch_out, fetch_in, pass_out, pass_in = [], [], [], []
        for a, (src, dst) in enumerate(zip(ins, outs)):
            for k, (px, py) in enumerate(_other_chips(x, y)):
                j, i1, i2 = 2 * px + py, 3 * a + k, 3 * n + 3 * a + k
                fetch_out.append(remote(src.at[c], dst.at[me, c], i1, (px, py, c)))
                fetch_in.append(remote(src.at[c], dst.at[j, c], i1, (px, py, c)))
                pass_out.append(remote(dst.at[j, c], dst.at[j, c], i2, sibling))
                pass_in.append(remote(dst.at[j, 1 - c], dst.at[j, 1 - c], i2, sibling))
        return me, fetch_out, fetch_in, pass_out, pass_in

    def start(ins, outs, scr):
        bufs, lsem = scr[:n], scr[n + 2]
        me, fetch_out, _, _, _ = copies(ins, outs, scr)
        for cp in fetch_out:
            cp.start()
        loads = []
        for a, (src, buf) in enumerate(zip(ins, bufs)):
            ld = pltpu.make_async_copy(src, buf, lsem.at[2 * a])
            ld.start()
            loads.append(ld)
        for a, (ld, buf, dst) in enumerate(zip(loads, bufs, outs)):
            ld.wait()
            st = pltpu.make_async_copy(buf, dst.at[me], lsem.at[2 * a + 1])
            st.start()
            st.wait()

    def finish(ins, outs, scr):
        _, fetch_out, fetch_in, pass_out, pass_in = copies(ins, outs, scr)
        for arrived, onward in zip(fetch_in, pass_out):
            arrived.wait_recv()
            onward.start()
        for cp in pass_in:
            cp.wait_recv()
        for cp in fetch_out + pass_out:
            cp.wait_send()

    return _Ride(list(shards), [jax.ShapeDtypeStruct((N_SHARD,) + s.shape, s.dtype) for s in shards],
                 [pltpu.VMEM(s.shape, s.dtype) for s in shards]
                 + [pltpu.SemaphoreType.DMA((6 * n,)), pltpu.SemaphoreType.DMA((6 * n,)), pltpu.SemaphoreType.DMA((2 * n,))],
                 start, finish)


def _comm(name, ins, out_shape, n_remote, plan):
    n_in, n_out = len(ins), len(out_shape)

    def body(*refs):
        plan(refs[:n_in], refs[n_in:n_in + n_out], *refs[n_in + n_out:])

    return pl.pallas_call(
        body, name=name, in_specs=[_ANY] * n_in, out_specs=[_ANY] * n_out, out_shape=out_shape,
        scratch_shapes=[pltpu.SemaphoreType.DMA((n_remote,)), pltpu.SemaphoreType.DMA((n_remote,))])(*ins)


def _send_other_half(name, grads_bf):
    n = len(grads_bf)

    def plan(ins, outs, ssem, rsem):
        x, y, c = _position()
        started = []
        for a, (src, dst) in enumerate(zip(ins, outs)):
            cp = pltpu.make_async_remote_copy(src.at[1 - c], dst, ssem.at[a], rsem.at[a], device_id=(x, y, 1 - c),
                                              device_id_type=_MESH)
            cp.start()
            started.append(cp)
        for cp in started:
            cp.wait_recv()
        for cp in started:
            cp.wait_send()

    shapes = [jax.ShapeDtypeStruct(s.shape[1:], s.dtype) for s in grads_bf]
    return _comm(name, grads_bf, shapes, n, plan)


def _exchange_ride(pair_sums):
    n = len(pair_sums)

    def copies(ins, outs, scr):
        ssem, rsem = scr
        x, y, c = _position()
        return [pltpu.make_async_remote_copy(src.at[2 * px + py], dst.at[k], ssem.at[3 * a + k], rsem.at[3 * a + k],
                                             device_id=(px, py, c), device_id_type=_MESH)
                for a, (src, dst) in enumerate(zip(ins, outs)) for k, (px, py) in enumerate(_other_chips(x, y))]

    def start(ins, outs, scr):
        for cp in copies(ins, outs, scr):
            cp.start()

    def finish(ins, outs, scr):
        for cp in copies(ins, outs, scr):
            cp.wait_recv()
        for cp in copies(ins, outs, scr):
            cp.wait_send()

    return _Ride(list(pair_sums), [jax.ShapeDtypeStruct((3,) + s.shape[1:], s.dtype) for s in pair_sums],
                 [pltpu.SemaphoreType.DMA((3 * n,)), pltpu.SemaphoreType.DMA((3 * n,))], start, finish)


def _swap_reduced(name, grads):
    n = len(grads)

    def body(*refs):
        ins, outs, ssem, rsem = refs[:n], refs[n:2 * n], refs[2 * n], refs[2 * n + 1]
        x, y, c = _position()
        sent = []
        for a, (src, dst) in enumerate(zip(ins, outs)):
            cp = pltpu.make_async_remote_copy(src.at[c], dst.at[c], ssem.at[a], rsem.at[a], device_id=(x, y, 1 - c),
                                              device_id_type=_MESH)
            cp.start()
            sent.append(cp)
        for a, (src, dst) in enumerate(zip(ins, outs)):
            pltpu.make_async_remote_copy(src.at[1 - c], dst.at[1 - c], ssem.at[a], rsem.at[a], device_id=(x, y, 1 - c),
                                         device_id_type=_MESH).wait_recv()
        for cp in sent:
            cp.wait_send()

    return pl.pallas_call(
        body, name=name, in_specs=[_ANY] * n, out_specs=[_ANY] * n,
        out_shape=[jax.ShapeDtypeStruct(g.shape, g.dtype) for g in grads],
        scratch_shapes=[pltpu.SemaphoreType.DMA((n,)), pltpu.SemaphoreType.DMA((n,))],
        input_output_aliases={a: a for a in range(n)})(*grads)


_FLIPS = [(dx, dy, dc) for dx in (0, 1) for dy in (0, 1) for dc in (0, 1) if dx + dy + dc]
_VMEM = pl.BlockSpec(memory_space=pltpu.VMEM)


def _to_all(src_of, dst, ssem, rsem):
    x, y, c = _position()
    me = 4 * x + 2 * y + c
    peers = [((x + dx) % 2, (y + dy) % 2, (c + dc) % 2) for dx, dy, dc in _FLIPS]
    sent = []
    for k, (px, py, pc) in enumerate(peers):
        cp = pltpu.make_async_remote_copy(src_of(2 * px + py), dst.at[me], ssem.at[k], rsem.at[k],
                                          device_id=(px, py, pc), device_id_type=_MESH)
        cp.start()
        sent.append(cp)
    for k, (px, py, pc) in enumerate(peers):
        pltpu.make_async_remote_copy(src_of(2 * px + py), dst.at[4 * px + 2 * py + pc], ssem.at[k], rsem.at[k],
                                     device_id=(px, py, pc), device_id_type=_MESH).wait_recv()
    for cp in sent:
        cp.wait_send()
    return me, 2 * x + y


def _share_small(name, block, total):
    shape = block.shape

    def body(in_ref, out_ref, *scratch):
        buf, ssem, rsem = (out_ref,) + scratch if not total else scratch
        me, _ = _to_all(lambda chip: in_ref, buf, ssem, rsem)
        buf[me] = in_ref[...]
        if total:
            acc = buf[0]
            for d in range(1, 8):
                acc = acc + buf[d]
            out_ref[...] = acc

    sems = [pltpu.SemaphoreType.DMA((7,)), pltpu.SemaphoreType.DMA((7,))]
    return pl.pallas_call(
        body, name=name, in_specs=[_VMEM], out_specs=_VMEM,
        out_shape=jax.ShapeDtypeStruct(shape if total else (8,) + shape, F32),
        scratch_shapes=([pltpu.VMEM((8,) + shape, F32)] if total else []) + sems,
        compiler_params=pltpu.CompilerParams(vmem_limit_bytes=VMEM_LIMIT_V7X))(block)


def _mod_rows_exchange(mv):
    def body(mv_ref, out_ref, ssem, rsem):
        x, y, c = _position()
        me = 2 * x + y
        out_ref[me] = mv_ref[4 * x + 2 * y + c]
        sent = []
        for k, (px, py) in enumerate(_other_chips(x, y)):
            cp = pltpu.make_async_remote_copy(mv_ref.at[4 * px + 2 * py + c], out_ref.at[me], ssem.at[k], rsem.at[k],
                                              device_id=(px, py, c), device_id_type=_MESH)
            cp.start()
            sent.append(cp)
        for k, (px, py) in enumerate(_other_chips(x, y)):
            pltpu.make_async_remote_copy(mv_ref.at[0], out_ref.at[2 * px + py], ssem.at[k], rsem.at[k],
                                         device_id=(px, py, c), device_id_type=_MESH).wait_recv()
        for cp in sent:
            cp.wait_send()

    return pl.pallas_call(
        body, name="mod_rows_exchange", in_specs=[_VMEM], out_specs=_VMEM,
        out_shape=jax.ShapeDtypeStruct((N_SHARD,) + mv.shape[1:], F32),
        scratch_shapes=[pltpu.SemaphoreType.DMA((3,)), pltpu.SemaphoreType.DMA((3,))],
        compiler_params=pltpu.CompilerParams(vmem_limit_bytes=VMEM_LIMIT_V7X))(mv)


def _mod_grad_exchange(dmj, dm_rows):
    def body(dmj_ref, rows_ref, out_ref, bias_ref, ssem, rsem):
        me, chip = _to_all(lambda j: dmj_ref.at[j], out_ref, ssem, rsem)
        out_ref[me] = dmj_ref[chip]
        for l in range(2):
            bias_ref[l] = _colsum(rows_ref[l])

    return pl.pallas_call(
        body, name="mod_grad_exchange", in_specs=[_VMEM, _VMEM], out_specs=[_VMEM, _VMEM],
        out_shape=[jax.ShapeDtypeStruct((8,) + dmj.shape[1:], F32), jax.ShapeDtypeStruct((2, 1, dm_rows.shape[-1]), F32)],
        scratch_shapes=[pltpu.SemaphoreType.DMA((7,)), pltpu.SemaphoreType.DMA((7,))],
        compiler_params=pltpu.CompilerParams(vmem_limit_bytes=VMEM_LIMIT_V7X))(dmj, dm_rows)


def _pack(arrays):
    flat = jnp.concatenate([a.reshape(-1).astype(F32) for a in arrays])
    total = flat.shape[0]
    rows = -(-total // (8 * PACK_LANES)) * 8
    return jnp.pad(flat, (0, rows * PACK_LANES - total)).reshape(rows, PACK_LANES)


def _unpack(pack, shapes):
    flat, out, pos = pack.reshape(-1), [], 0
    for sh in shapes:
        size = int(np.prod(sh)) if len(sh) else 1
        out.append(flat[pos:pos + size].reshape(sh))
        pos += size
    return out


def _block_diag(pw):
    out = jnp.zeros((POOL_W, POOL_W), pw.dtype)
    for g in range(4):
        out = out.at[g * 64:(g + 1) * 64, g * 64:(g + 1) * 64].set(pw[g])
    return out


def _local_step(x, ctx, small, mod3s, loss_target, comm):
    cfg = _Cfg(x.shape[0], x.shape[1], ctx.shape[1])
    assert cfg.seq % TM == 0 and cfg.nc % TM == 0 and cfg.seq % cfg.ctx == 0 and cfg.ctx % CHUNK == 0
    nb_all, nb_lat = cfg.r // TM, cfg.nl // TM
    last = 1
    wf, big = comm.wf, comm.grads
    cos_t, sin_t = _rope_tables(cfg.seq)
    xs = jnp.concatenate([x.reshape(cfg.nl, D), ctx.reshape(cfg.nc, D)], axis=0)
    row = lambda w: pl.BlockSpec((TM, w), lambda i, j, k: (i, 0))
    mod3_spec = pl.BlockSpec((None, 6, D), lambda i, j, k: (cfg.mod_row(i), 0, 0))
    whole = lambda rows: pl.BlockSpec((None, rows, D), lambda i, j, k: (0, 0, 0))

    def conv_params(l):
        dw = jnp.pad(wf[l]["conv_dw"], ((0, 1), (0, 0)))
        return (dw, small["conv_dw_b"][l][None], small["conv_ln_g"][l][None], small["conv_ln_b"][l][None],
                _block_diag(small["pool_w"][l]).astype(BF16), small["pool_scale"][l][None])

    def residual_epi(ig):
        def epi(acc, ex, outs):
            x_ref, m_ref = ex
            outs[0][...] = x_ref[...] + m_ref[ig:ig + 1, :] * acc
            outs[1][...] = acc.astype(BF16)
        return epi

    def mixer_epi(acc, ex, outs):
        x_ref, m_ref, g_ref = ex
        x1 = x_ref[...] + m_ref[2:3, :] * acc
        outs[0][...] = x1
        outs[1][...] = acc.astype(BF16)
        outs[2][...] = _norm_mod(x1, g_ref[...], m_ref, 3, 4)

    def const(shape):
        return pl.BlockSpec(shape, lambda i, j, k: (0,) * len(shape))

    def loss_epi(acc, ex, outs):
        x_ref, m_ref, t_ref, g_ref = ex
        dx_ref, loss_ref, dfg_ref, dy_ref, dgate_ref = outs
        i = pl.program_id(0)
        gate = m_ref[5:6, :]
        xv = x_ref[...] + gate * acc
        g = g_ref[...]
        r = lax.rsqrt(jnp.mean(xv * xv, axis=-1, keepdims=True) + EPS)
        xh = xv * r
        err = xh * g - t_ref[...]
        _accumulate_rows(i == 0, loss_ref, (0.5 / D) * _colsum(jnp.sum(err * err, axis=-1, keepdims=True)))
        dy = err * (1.0 / D)
        _accumulate_rows(i == 0, dfg_ref, _colsum(dy * xh))
        dxh = dy * g
        dxv = r * (dxh - xh * jnp.mean(dxh * xh, axis=-1, keepdims=True))
        dx_ref[...] = dxv
        dy_ref[...] = (dxv * gate).astype(BF16)
        _accumulate_rows(cfg.first_of_row(i), dgate_ref, _colsum(dxv * acc))

    def hosted(name, call):
        outs, got = call(comm.ride(name))
        comm.landed(name, got)
        return outs

    saved = []
    for l in range(2):
        nb = nb_lat if l == last else nb_all
        wl = wf[l]
        mod3 = mod3s[l]
        h1, qkv, cpin = _in_proj(f"in_proj{l}", xs, small["norm1_g"][l][None], mod3, wl["w_in"], 0, cos_t, sin_t, cfg)
        mix, lse = hosted(f"att_fwd{l}", lambda ride: _att_fwd(f"att_fwd{l}", qkv, small["attn_sink"][l], l != last, cfg, ride))
        prm = conv_params(l)
        mix, yconv = _convpool_fwd(f"convpool_fwd_lat{l}", cpin, mix, None, prm, cfg.seq, 0, cfg)
        if l != last:
            mix, yconv = _convpool_fwd(f"convpool_fwd_ctx{l}", cpin, mix, yconv, prm, cfg.ctx, cfg.nl // cfg.ctx, cfg)
        x1, y1, h2 = _mm(f"out_proj{l}", "nn", (nb, 1, 1), mix, wl["w_out"].reshape(1, D, D), row(D), whole(D),
                         [jax.ShapeDtypeStruct((cfg.r, D), F32)] + [jax.ShapeDtypeStruct((cfg.r, D), BF16)] * 2, [row(D)] * 3,
                         extras=[xs, mod3, small["norm2_g"][l][None]],
                         extra_specs=[row(D), mod3_spec, pl.BlockSpec((1, D), lambda i, j, k: (0, 0))], epi=mixer_epi)
        gu, act = hosted(f"ffn_in{l}", lambda ride: _ffn_in(f"ffn_in{l}", h2, wl["w_ffn_in"], 0, nb, cfg, ride))
        if l != last:
            xs_next, y2 = _mm(f"ffn_out{l}", "nn", (nb, 1, 1), act, wl["w_ffn_out"].reshape(1, D_FF, D), row(D_FF),
                              whole(D_FF), [jax.ShapeDtypeStruct((cfg.r, D), F32), jax.ShapeDtypeStruct((cfg.r, D), BF16)],
                              [row(D), row(D)], extras=[x1, mod3], extra_specs=[row(D), mod3_spec], epi=residual_epi(5))
        else:
            xs_next, y2 = None, None
            dx, loss, d_final_g, dy2, dg2 = _mm(
                f"ffn_out{l}", "nn", (nb, 1, 1), act, wl["w_ffn_out"].reshape(1, D_FF, D), row(D_FF), whole(D_FF),
                [jax.ShapeDtypeStruct((cfg.r, D), F32), jax.ShapeDtypeStruct((1, 1), F32), jax.ShapeDtypeStruct((1, D), F32),
                 jax.ShapeDtypeStruct((cfg.r, D), BF16), jax.ShapeDtypeStruct((MOD_ROWS, 1, D), F32)],
                [row(D), const((1, 1)), const((1, D)), row(D),
                 pl.BlockSpec((None, 1, D), lambda i, j, k: (cfg.mod_row(i), 0, 0))],
                extras=[x1, mod3, loss_target.reshape(cfg.nl, D), small["final_g"][None]],
                extra_specs=[row(D), mod3_spec, row(D), const((1, D))], epi=loss_epi)
        saved.append(dict(mod3=mod3, x0=xs, h1=h1, qkv=qkv, cpin=cpin, mix=mix, yconv=yconv, lse=lse, y1=y1, x1=x1,
                          h2=h2, gu=gu, act=act, y2=y2, prm=prm))
        xs = xs_next

    sg = {k: [None, None] for k in ("norm1_g", "norm2_g", "conv_dw", "conv_dw_b", "conv_ln_g", "conv_ln_b",
                                    "attn_sink", "pool_w", "pool_scale")}
    dms = [None, None]

    def swiglu_bwd_epi(acc, ex, outs):
        outs[0][0] = (acc * ex[0][0].astype(F32)).astype(BF16)
        outs[0][1] = (acc * ex[0][1].astype(F32)).astype(BF16)

    def halves_epi(acc, ex, outs):
        h = acc.shape[0] // 2
        outs[0][0] = acc[:h].astype(BF16)
        outs[0][1] = acc[h:].astype(BF16)

    def row_shards_epi(n):
        def epi(acc, ex, outs):
            s0 = acc.shape[0] // n
            h = s0 // 2
            for t in range(n):
                for half in range(2):
                    outs[0][half, t] = acc[t * s0 + half * h:t * s0 + (half + 1) * h].astype(BF16)
        return epi

    def col_shards_epi(acc, ex, outs):
        h = acc.shape[0] // 2
        for j in range(N_SHARD):
            for half in range(2):
                outs[0][half, j] = acc[half * h:(half + 1) * h, j * IN_SHARD:(j + 1) * IN_SHARD].astype(BF16)

    for l in (1, 0):
        sv = saved[l]
        mod3 = sv["mod3"]
        nb = nb_lat if l == last else nb_all
        tr = _dw_rows(nb * TM)
        steps = nb * TM // tr
        wl = wf[l]
        gu_spec = pl.BlockSpec((2, TM, HALF_FF), lambda j, i, k: (0, i, j))
        df = _mm(f"ffn_out_bwd{l}", "nt", (2, nb, 1), dy2, wl["w_ffn_out"].reshape(1, D_FF, D),
                 pl.BlockSpec((TM, D), lambda j, i, k: (i, 0)), pl.BlockSpec((None, HALF_FF, D), lambda j, i, k: (0, j, 0)),
                 [jax.ShapeDtypeStruct((2, cfg.r, D_FF), BF16)], [gu_spec], extras=[sv["gu"]], extra_specs=[gu_spec],
                 epi=swiglu_bwd_epi)[0]
        big[l]["w_ffn_out"] = _mm_dw(
            f"dw_ffn_out{l}", sv["act"], dy2, pl.BlockSpec((tr, HALF_FF), lambda i, j, k: (k, j)),
            pl.BlockSpec((tr, D), lambda i, j, k: (k, 0)), (2, N_SHARD, D_FF // 8, D),
            pl.BlockSpec((2, 2, D_FF // 8, D), lambda i, j, k: (0, j, 0, 0)), 2, (HALF_FF, D), steps, row_shards_epi(2))[0]
        ride = comm.ride(f"dw_ffn_in{l}")
        res = _mm_dw(f"dw_ffn_in{l}", sv["h2"], df, pl.BlockSpec((tr, D), lambda i, j, k: (k, 0)),
                     pl.BlockSpec((None, tr, HALF_FF), lambda i, j, k: (j // 2, k, j % 2)), (2, N_SHARD, D // 2, HALF_FF),
                     pl.BlockSpec((2, None, D // 2, HALF_FF), lambda i, j, k: (0, j, 0, 0)), N_SHARD, (D, HALF_FF), steps,
                     halves_epi, ride=ride)
        res, got = res if ride is not None else (res, [])
        big[l]["w_ffn_in"] = res[0]
        comm.landed(f"dw_ffn_in{l}", got)
        dx1, dsh2, dsc2, dn2, dy1, dg1 = _proj_norm_bwd(
            f"ffn_in_bwd{l}", df, pl.BlockSpec((2, TM, D_FF), lambda i: (0, i, 0)),
            lambda a_ref, j: a_ref[j // 2, :, (j % 2) * HALF_FF:(j % 2 + 1) * HALF_FF], wl["w_ffn_in"],
            sv["x1"], dx, small["norm2_g"][l][None], mod3, 4, nb, False, (sv["y1"], mod3, 2), cfg)
        dmix = _mm(f"out_proj_bwd{l}", "nt", (nb, 1, 1), dy1, wl["w_out"].reshape(1, D, D), row(D), whole(D),
                   [jax.ShapeDtypeStruct((cfg.r, D), F32)], [row(D)])[0]
        big[l]["w_out"] = _mm_dw(
            f"dw_out{l}", sv["mix"], dy1, pl.BlockSpec((tr, D), lambda i, j, k: (k, 0)),
            pl.BlockSpec((tr, D), lambda i, j, k: (k, 0)), (2, N_SHARD, D // 8, D),
            pl.BlockSpec((2, N_SHARD, D // 8, D), lambda i, j, k: (0, 0, 0, 0)), 1, (D, D), steps, row_shards_epi(N_SHARD))[0]
        du, dkvl, dkvc, dsink = hosted(f"att_bwd{l}", lambda ride: _att_bwd(
            f"att_bwd{l}", sv["qkv"], sv["mix"], dmix, sv["lse"], small["attn_sink"][l], cos_t, sin_t, l != last, cfg, ride))
        acc = [du] + [jnp.zeros(sh, F32) for sh in _SMALL_SHAPES]
        acc = _convpool_bwd(f"convpool_bwd_lat{l}", sv["cpin"], sv["yconv"], dmix, sv["prm"], acc, cfg.seq, 0, cfg)
        if l != last:
            acc = _convpool_bwd(f"convpool_bwd_ctx{l}", sv["cpin"], sv["yconv"], dmix, sv["prm"], acc, cfg.ctx,
                                cfg.nl // cfg.ctx, cfg)
        du, g_dw, g_dwb, g_lng, g_lnb, g_wbd, g_ps = acc
        du = _place_kv(f"place_kv{l}", du, dkvl, dkvc, l != last, cfg)
        if l == last:
            du = _place_ctx_kv_only(f"place_ctx_kv{l}", du, dkvc, cfg)
        sg["attn_sink"][l] = dsink[:, 0]
        sg["conv_dw"][l], sg["conv_dw_b"][l], sg["conv_ln_g"][l], sg["conv_ln_b"][l] = g_dw[:CONV_K], g_dwb[0], g_lng[0], g_lnb[0]
        sg["pool_w"][l] = jnp.stack([g_wbd[g * 64:(g + 1) * 64, g * 64:(g + 1) * 64] for g in range(4)])
        sg["pool_scale"][l] = g_ps[0]
        tr_all = _dw_rows(cfg.r)
        big[l]["w_in"] = _mm_dw(
            f"dw_in{l}", sv["h1"], du, pl.BlockSpec((tr_all, D), lambda i, j, k: (k, 0)),
            pl.BlockSpec((tr_all, IN_W), lambda i, j, k: (k, 0)), (2, N_SHARD, D // 2, IN_SHARD),
            pl.BlockSpec((2, N_SHARD, D // 2, IN_SHARD), lambda i, j, k: (0, 0, 0, 0)), 1, (D, IN_W), cfg.r // tr_all,
            col_shards_epi)[0]
        below = (saved[l - 1]["y2"], saved[l - 1]["mod3"], 5) if l > 0 else None
        res = _proj_norm_bwd(
            f"in_proj_bwd{l}", du, pl.BlockSpec((TM, IN_W), lambda i: (i, 0)),
            lambda a_ref, j: a_ref[:, j * IN_SHARD:(j + 1) * IN_SHARD], wl["w_in"],
            sv["x0"], dx1, small["norm1_g"][l][None], mod3, 1, nb_all, l == last, below, cfg, dx_latent_only=l == 0)
        dx, dsh1, dsc1, dn1 = res[:4]
        sg["norm1_g"][l], sg["norm2_g"][l] = dn1[0], dn2[0]
        parts = [dsh1, dsc1, dg1, dsh2, dsc2, dg2]
        if below is not None:
            dy2, dg2 = res[4:]
        dm = jnp.concatenate([t[:cfg.b, 0, :] for t in parts], axis=1)
        live = (0, 1) if l == last else range(6)
        dm_ctx = jnp.concatenate([t[cfg.b, 0, :] if i in live else jnp.zeros((D,), F32) for i, t in enumerate(parts)])
        dms[l] = jnp.concatenate([dm, dm_ctx[None, :], jnp.zeros((MOD_ROWS - cfg.b - 1, 6 * D), F32)], axis=0)

    grad_x = dx.reshape(x.shape)
    small_grads = {k: jnp.stack(v) for k, v in sg.items()}
    small_grads["final_g"] = d_final_g[0]
    return loss, grad_x, small_grads, dms


_BIG = ("w_in", "w_out", "w_ffn_in", "w_ffn_out")
_TAPS = "conv_dw"
_SMALL = ("c_ctx", "b_mod", "norm1_g", "norm2_g", "conv_dw", "conv_dw_b", "conv_ln_g", "conv_ln_b", "attn_sink",
          "pool_w", "pool_scale", "final_g")
_ORDER = ("c_ctx", "w_mod", "b_mod", "norm1_g", "norm2_g", "w_in", "conv_dw", "conv_dw_b", "conv_ln_g", "conv_ln_b",
          "attn_sink", "pool_w", "pool_scale", "w_out", "w_ffn_in", "w_ffn_out", "final_g")
_GATHER_HOSTS = {
    "first": ((0, "w_in"),),
    "att_fwd0": ((0, "w_out"), (0, "w_ffn_in"), (0, "w_ffn_out"), (0, _TAPS)),
    "ffn_in0": tuple((1, k) for k in _BIG + (_TAPS,)),
}
_REDUCE_HOSTS = {
    "dw_ffn_in0": tuple((1, k) for k in _BIG),
    "att_bwd0": ((0, "w_ffn_in"), (0, "w_ffn_out"), (0, "w_out")),
    "last": ((0, "w_in"),),
}


class _Comm:
    def __init__(self, w, c_idx, jc_idx):
        self.shapes = {k: w[k].shape[1:] for k in _BIG}
        self.c_idx, self.jc_idx = c_idx, jc_idx
        halves = lambda a: a.reshape(2, a.shape[0] // 2, a.shape[1])
        taps = jnp.pad(w[_TAPS], ((0, 0), (0, 1), (0, 64)))
        cast = {k: w[k].astype(BF16) for k in _BIG}
        self.shards = [{**{k: halves(cast[k][l]) for k in _BIG}, _TAPS: halves(taps[l])} for l in range(2)]
        self.wf = [dict(), dict()]
        self.grads = [dict(), dict()]
        self.reduced = [dict(), dict()]
        self._open = {}
        self.landed("first", _run_ride("gather_first", self.ride("first")))

    def ride(self, host):
        if host in _GATHER_HOSTS:
            return _gather_ride([self.shards[layer][k] for layer, k in _GATHER_HOSTS[host]])
        if host in _REDUCE_HOSTS:
            what = _REDUCE_HOSTS[host]
            mine = [self.grads[layer][k] for layer, k in what]
            other = _send_other_half(f"send_other_half_{host}", mine)
            pair = [_pair_add(f"pair_add{layer}_{k}", a, b, self.c_idx) for (layer, k), a, b in zip(what, mine, other)]
            self._open[host] = pair
            return _exchange_ride(pair)
        return None

    def landed(self, host, got):
        if host in _GATHER_HOSTS:
            for (layer, k), f in zip(_GATHER_HOSTS[host], got):
                if k == _TAPS:
                    taps = f.reshape(N_SHARD, 32, 128)[:, :CONV_K, :64]
                    self.wf[layer][k] = jnp.transpose(taps, (1, 0, 2)).reshape(CONV_K, CONV_W)
                else:
                    self.wf[layer][k] = f.reshape((1, N_SHARD) + self.shapes[k])
        if host in _REDUCE_HOSTS:
            what = _REDUCE_HOSTS[host]
            mine = [_shard_sum(f"shard_sum{layer}_{k}", a, b, self.jc_idx)
                    for (layer, k), a, b in zip(what, self._open.pop(host), got)]
            for (layer, k), g in zip(what, _swap_reduced(f"swap_reduced_{host}", mine)):
                self.reduced[layer][k] = g.reshape(self.shapes[k])


def _conditioning(c, c_ctx, w_mod, b_mod, chip):
    b = c.shape[0]
    block = jnp.concatenate([c, c_ctx[None, :], jnp.zeros((8 - b - 1, D), F32)], axis=0)
    c_all = _share_small("share_c", block, False).reshape(64, D)
    bias = lax.dynamic_slice_in_dim(b_mod, chip * MOD_W, MOD_W, axis=1)
    full = lambda r, q: pl.BlockSpec((r, q), lambda i, j, k: (0, 0))

    def bias_epi(acc, ex, outs):
        outs[0][...] = acc + ex[0][...]

    mv = [_mm(f"mod_fwd{l}", "nn", (1, 1, 1), c_all, w_mod[l], full(64, D), full(D, MOD_W),
              [jax.ShapeDtypeStruct((64, MOD_W), F32)], [full(64, MOD_W)], extras=[bias[l][None]],
              extra_specs=[full(1, MOD_W)], a_fn=_silu, epi=bias_epi)[0] for l in range(2)]
    by_dev = jnp.transpose(jnp.stack(mv).reshape(2, 8, 8, MOD_W), (1, 0, 2, 3))
    rows = jnp.transpose(_mod_rows_exchange(by_dev), (1, 2, 0, 3)).reshape(2, 8, 6 * D)
    rows = jnp.pad(rows, ((0, 0), (0, MOD_ROWS - 8), (0, 0)))
    return [rows[l].reshape(MOD_ROWS, 6, D) for l in range(2)], c_all


def _conditioning_bwd(dms, c_all, w_mod, b):
    dm = jnp.stack([d[:8] for d in dms])
    by_chip = jnp.transpose(dm.reshape(2, 8, N_SHARD, MOD_W), (2, 0, 1, 3))
    gathered, d_bias = _mod_grad_exchange(by_chip, dm)
    dm_all = jnp.transpose(gathered, (1, 0, 2, 3)).reshape(2, 64, MOD_W)
    full = lambda r, q: pl.BlockSpec((r, q), lambda i, j, k: (0, 0))

    def ctx_rows_epi(acc, ex, outs):
        row = lax.broadcasted_iota(jnp.int32, acc.shape, 0) % 8
        outs[0][...] = _colsum(jnp.where(row == b, acc * _dsilu(ex[0][...]), 0.0))

    g_mod, d_ctx = [], jnp.zeros((D,), F32)
    for l in range(2):
        g_mod.append(_mm(f"dw_mod{l}", "tn", (1, 1, 1), c_all, dm_all[l], full(64, D), full(64, MOD_W),
                         [jax.ShapeDtypeStruct((D, MOD_W), F32)], [full(D, MOD_W)], a_fn=_silu)[0])
        part = _mm(f"mod_bwd{l}", "nt", (1, 1, 1), dm_all[l], w_mod[l], full(64, MOD_W), full(D, MOD_W),
                   [jax.ShapeDtypeStruct((1, D), F32)], [full(1, D)], extras=[c_all], extra_specs=[full(64, D)],
                   epi=ctx_rows_epi)[0]
        d_ctx = d_ctx + part[0]
    return g_mod, d_bias[:, 0, :], d_ctx


def kernel(x, c, ctx, c_ctx, w_mod, b_mod, norm1_g, norm2_g, w_in, conv_dw, conv_dw_b, conv_ln_g, conv_ln_b, attn_sink, pool_w, pool_scale, w_out, w_ffn_in, w_ffn_out, final_g, loss_target, m_c_ctx, m_w_mod, m_b_mod, m_norm1_g, m_norm2_g, m_w_in, m_conv_dw, m_conv_dw_b, m_conv_ln_g, m_conv_ln_b, m_attn_sink, m_pool_w, m_pool_scale, m_w_out, m_w_ffn_in, m_w_ffn_out, m_final_g, v_c_ctx, v_w_mod, v_b_mod, v_norm1_g, v_norm2_g, v_w_in, v_conv_dw, v_conv_dw_b, v_conv_ln_g, v_conv_ln_b, v_attn_sink, v_pool_w, v_pool_scale, v_w_out, v_w_ffn_in, v_w_ffn_out, v_final_g):
    w = dict(c_ctx=c_ctx, w_mod=w_mod, b_mod=b_mod, norm1_g=norm1_g, norm2_g=norm2_g, w_in=w_in, conv_dw=conv_dw,
             conv_dw_b=conv_dw_b, conv_ln_g=conv_ln_g, conv_ln_b=conv_ln_b, attn_sink=attn_sink, pool_w=pool_w,
             pool_scale=pool_scale, w_out=w_out, w_ffn_in=w_ffn_in, w_ffn_out=w_ffn_out, final_g=final_g)
    m = dict(c_ctx=m_c_ctx, w_mod=m_w_mod, b_mod=m_b_mod, norm1_g=m_norm1_g, norm2_g=m_norm2_g, w_in=m_w_in,
             conv_dw=m_conv_dw, conv_dw_b=m_conv_dw_b, conv_ln_g=m_conv_ln_g, conv_ln_b=m_conv_ln_b,
             attn_sink=m_attn_sink, pool_w=m_pool_w, pool_scale=m_pool_scale, w_out=m_w_out, w_ffn_in=m_w_ffn_in,
             w_ffn_out=m_w_ffn_out, final_g=m_final_g)
    v = dict(c_ctx=v_c_ctx, w_mod=v_w_mod, b_mod=v_b_mod, norm1_g=v_norm1_g, norm2_g=v_norm2_g, w_in=v_w_in,
             conv_dw=v_conv_dw, conv_dw_b=v_conv_dw_b, conv_ln_g=v_conv_ln_g, conv_ln_b=v_conv_ln_b,
             attn_sink=v_attn_sink, pool_w=v_pool_w, pool_scale=v_pool_scale, w_out=v_w_out, w_ffn_in=v_w_ffn_in,
             w_ffn_out=v_w_ffn_out, final_g=v_final_g)
    xi, yi, ci = _position()
    chip = 2 * xi + yi
    mod3s, c_all = _conditioning(c, c_ctx, w_mod, b_mod, chip)
    comm = _Comm(w, jnp.reshape(ci, (1,)).astype(jnp.int32), jnp.stack([chip, ci]).astype(jnp.int32))
    small = {k: w[k] for k in _SMALL if k not in ("conv_dw", "c_ctx", "b_mod")}
    loss, grad_x, sgrads, dms = _local_step(x, ctx, small, mod3s, loss_target, comm)
    comm.landed("last", _run_ride("exchange_last", comm.ride("last")))
    g_mod, sgrads["b_mod"], d_ctx = _conditioning_bwd(dms, c_all, w_mod, c.shape[0])
    sgrads["c_ctx"] = 0.5 * d_ctx

    names = list(_SMALL)
    total = _share_small("sum_small", _pack([loss] + [sgrads[k] for k in names]), True)
    parts = _unpack(total, [()] + [sgrads[k].shape for k in names])
    loss_out = parts[0]
    gsmall = dict(zip(names, parts[1:]))
    gsmall["conv_dw"] = lax.dynamic_slice_in_dim(gsmall["conv_dw"], chip * 64, 64, axis=2)

    grads, delta, new_m, new_v = dict(gsmall), {}, {}, {}
    for k in ("w_mod",) + _BIG:
        s0, s1 = w[k].shape[1:]
        flat = lambda a: a.reshape(2 * s0, s1)
        g_layers = g_mod if k == "w_mod" else [comm.reduced[l][k] for l in range(2)]
        outs = _adamw_layers(f"adamw_{k}", flat(w[k]), g_layers, flat(m[k]), flat(v[k]))
        grads[k], delta[k], new_m[k], new_v[k] = [a.reshape(w[k].shape) for a in outs]
    d_, m_, v_ = _adamw("adamw_small", _pack([w[k] for k in names]), _pack([gsmall[k] for k in names]),
                        _pack([m[k] for k in names]), _pack([v[k] for k in names]))
    sshapes = [w[k].shape for k in names]
    for k, a, b, e in zip(names, _unpack(d_, sshapes), _unpack(m_, sshapes), _unpack(v_, sshapes)):
        delta[k], new_m[k], new_v[k] = a, b, e
    return (loss_out, grad_x, *[grads[k] for k in _ORDER], *[delta[k] for k in _ORDER],
            *[new_m[k] for k in _ORDER], *[new_v[k] for k in _ORDER])
```

```python
from typing import NamedTuple

import jax
import jax.numpy as jnp
import numpy as np
from jax import lax
from jax.experimental import pallas as pl
from jax.experimental.pallas import tpu as pltpu

F32 = jnp.float32
BF16 = jnp.bfloat16

D = 1024
GRID_W = 64
HEAD_DIM = 64
N_HEADS = 8
ATTN_W = 512
CONV_W = 256
POOL_W = 256
IN_W = 1536
D_FF = 2816
CONV_K = 31
QB = 128
ROPE_BASE = 10000.0
EPS = 1e-6
NEG = -1e30
N_SHARD = 4
IN_SHARD = IN_W // N_SHARD
HALF_FF = D_FF // 2
MOD_W = 6 * D // N_SHARD
MOD_ROWS = 16
PACK_LANES = 128

ADAM_LR = 0.001
ADAM_B1 = 0.9
ADAM_B2 = 0.999
ADAM_EPS = 1e-08
ADAM_WD = 0.01
ADAM_STEP = 10

VMEM_LIMIT_V7X = 56 * 1024 * 1024
TM = 512
TR_MAX = 1024
CHUNK = 256
PAD = 16

_MESH = pl.DeviceIdType.MESH
_ANY = pl.BlockSpec(memory_space=pl.ANY)
_DIMS = {"nn": (((1,), (0,)), ((), ())), "nt": (((1,), (1,)), ((), ())), "tn": (((0,), (0,)), ((), ()))}


class _Cfg(NamedTuple):
    b: int
    seq: int
    ctx: int

    @property
    def nl(self):
        return self.b * self.seq

    @property
    def nc(self):
        return self.b * self.ctx

    @property
    def r(self):
        return self.nl + self.nc

    def mod_row(self, i):
        return jnp.where(i < self.nl // TM, i // (self.seq // TM), self.b)

    def first_of_row(self, i):
        nlb = self.nl // TM
        return jnp.logical_or(jnp.logical_and(i < nlb, i % (self.seq // TM) == 0), i == nlb)


def _params(n_grid=0):
    sem = ("arbitrary",) * n_grid if n_grid else None
    return pltpu.CompilerParams(dimension_semantics=sem, vmem_limit_bytes=VMEM_LIMIT_V7X)


def _dot(a, b, mode="nn"):
    return lax.dot_general(a.astype(BF16), b.astype(BF16), _DIMS[mode], preferred_element_type=F32)


def _sigmoid(x):
    return 1.0 / (1.0 + jnp.exp(-x))


def _silu(x):
    return x * _sigmoid(x)


def _dsilu(x):
    s = _sigmoid(x)
    return s * (1.0 + x * (1.0 - s))


def _colsum(v):
    return jnp.sum(v, axis=0, keepdims=True)


def _dw_rows(rows):
    return TR_MAX if rows % TR_MAX == 0 else TM


def _epi_store(acc, ex, outs):
    for o in outs:
        o[...] = acc.astype(o.dtype)


class _Ride(NamedTuple):
    ins: list
    out_shape: list
    scratch: list
    start: object
    finish: object


class _Hosted(NamedTuple):
    ride: _Ride
    n_in: int
    n_out: int
    grid: tuple

    def split(self, refs):
        n_ri, n_ro, n_rs = len(self.ride.ins), len(self.ride.out_shape), len(self.ride.scratch)
        r_in = refs[self.n_in:self.n_in + n_ri]
        r_out = refs[self.n_in + n_ri + self.n_out:self.n_in + n_ri + self.n_out + n_ro]
        own = refs[:self.n_in] + refs[self.n_in + n_ri:self.n_in + n_ri + self.n_out] + \
            refs[self.n_in + n_ri + self.n_out + n_ro:len(refs) - n_rs]
        return own, (r_in, r_out, refs[len(refs) - n_rs:])

    def start(self, parts):
        ids = [pl.program_id(d) for d in range(len(self.grid))]
        first = ids[0] == 0
        for i in ids[1:]:
            first = jnp.logical_and(first, i == 0)
        pl.when(first)(lambda: self.ride.start(*parts))

    def finish(self, parts):
        ids = [pl.program_id(d) for d in range(len(self.grid))]
        last = ids[0] == self.grid[0] - 1
        for i, g in zip(ids[1:], self.grid[1:]):
            last = jnp.logical_and(last, i == g - 1)
        pl.when(last)(lambda: self.ride.finish(*parts))


def _hosted_call(body, ride, name, grid, ins, in_specs, out_shape, out_specs, scratch, params):
    if ride is None:
        res = pl.pallas_call(body, name=name, grid=grid, in_specs=in_specs, out_specs=out_specs, out_shape=out_shape,
                             scratch_shapes=scratch, compiler_params=params)(*ins)
        return list(res), []
    host = _Hosted(ride, len(ins), len(out_shape), tuple(grid))

    def hosted(*refs):
        own, parts = host.split(refs)
        host.start(parts)
        body(*own)
        host.finish(parts)

    res = pl.pallas_call(
        hosted, name=name, grid=grid, in_specs=list(in_specs) + [_ANY] * len(ride.ins),
        out_specs=list(out_specs) + [_ANY] * len(ride.out_shape), out_shape=list(out_shape) + list(ride.out_shape),
        scratch_shapes=list(scratch) + list(ride.scratch), compiler_params=params)(*ins, *ride.ins)
    return list(res[:len(out_shape)]), list(res[len(out_shape):])


def _mm(name, mode, grid, a, b, a_spec, b_spec, out_shape, out_specs, acc_shape=None, extras=(),
        extra_specs=(), a_fn=None, epi=_epi_store, ride=None):
    nk = grid[2]
    n_ex, n_out = len(extras), len(out_shape)

    def body(*refs):
        a_ref, b_ref = refs[:2]
        ex = refs[2:2 + n_ex]
        outs = refs[2 + n_ex:2 + n_ex + n_out]
        av = a_ref[...]
        if a_fn is not None:
            av = a_fn(av)
        part = _dot(av, b_ref[...], mode)
        if nk == 1:
            epi(part, ex, outs)
        else:
            acc = refs[-1]
            k = pl.program_id(2)

            @pl.when(k == 0)
            def _():
                acc[...] = part

            @pl.when(k > 0)
            def _():
                acc[...] += part

            @pl.when(k == nk - 1)
            def _():
                epi(acc[...], ex, outs)

    scratch = [] if nk == 1 else [pltpu.VMEM(acc_shape, F32)]
    outs, ride_outs = _hosted_call(body, ride, name, grid, [a, b, *extras], [a_spec, b_spec, *extra_specs],
                                   list(out_shape), list(out_specs), scratch, _params(3))
    return outs if ride is None else (outs, ride_outs)


def _mm_dw(name, a, b, a_spec, b_spec, out_shape, out_spec, n_out_blocks, acc_shape, n_steps, epi, a_fn=None,
           extras=(), extra_specs=(), extra_out=(), extra_out_specs=(), ride=None):
    grid = (1, n_out_blocks, n_steps)
    outs = [jax.ShapeDtypeStruct(out_shape, BF16)] + list(extra_out)
    return _mm(name, "tn", grid, a, b, a_spec, b_spec, outs, [out_spec, *extra_out_specs], acc_shape, extras,
               extra_specs, a_fn, epi, ride)


def _gate_step(i, dxv, y_ref, m_ref, ig, dy_ref, dgate_ref, cfg):
    dy_ref[...] = (dxv * m_ref[ig:ig + 1, :]).astype(BF16)
    _accumulate_rows(cfg.first_of_row(i), dgate_ref, _colsum(dxv * y_ref[...].astype(F32)))


def _proj_norm_bwd(name, a, a_spec, pick, w, x, dres, gvec, mod3, isc, nblk, res_latent_only, gate, cfg,
                   dx_latent_only=False):
    ns = w.shape[-1]
    nlb = cfg.nl // TM

    def body(a_ref, w_ref, x_ref, dres_ref, g_ref, m_ref, *rest):
        if gate is None:
            dx_ref, dsh_ref, dsc_ref, dg_ref = rest
        else:
            y_ref, gm_ref, dx_ref, dsh_ref, dsc_ref, dg_ref, dy_ref, dgate_ref = rest
        i = pl.program_id(0)
        dhv = _dot(pick(a_ref, 0), w_ref[0], "nt")
        for j in range(1, N_SHARD):
            dhv = dhv + _dot(pick(a_ref, j), w_ref[j], "nt")
        xv = x_ref[...]
        r = lax.rsqrt(jnp.mean(xv * xv, axis=-1, keepdims=True) + EPS)
        xh = xv * r
        g = g_ref[...]
        sc1 = 1.0 + m_ref[isc:isc + 1, :]
        t = dhv * xh
        first = cfg.first_of_row(i)
        _accumulate_rows(first, dsh_ref, _colsum(dhv))
        _accumulate_rows(first, dsc_ref, _colsum(t * g))
        _accumulate_rows(i == 0, dg_ref, _colsum(t * sc1))
        dxh = dhv * (g * sc1)
        dxn = r * (dxh - xh * jnp.mean(dxh * xh, axis=-1, keepdims=True))
        dxv = (jnp.where(i < nlb, dres_ref[...], 0.0) if res_latent_only else dres_ref[...]) + dxn
        if dx_latent_only:
            @pl.when(i < nlb)
            def _():
                dx_ref[...] = dxv
        else:
            dx_ref[...] = dxv
        if gate is not None:
            _gate_step(i, dxv, y_ref, gm_ref, gate[2], dy_ref, dgate_ref, cfg)

    row = pl.BlockSpec((TM, D), lambda i: (i, 0))
    vec = pl.BlockSpec((1, D), lambda i: (0, 0))
    part = pl.BlockSpec((None, 1, D), lambda i: (cfg.mod_row(i), 0, 0))
    part_shape = jax.ShapeDtypeStruct((MOD_ROWS, 1, D), F32)
    resident = pl.BlockSpec((None, N_SHARD, D, ns), lambda i: (0, 0, 0, 0), pipeline_mode=pl.Buffered(1))
    ins, in_specs = [a, w, x, dres, gvec, mod3], [a_spec, resident, row, row, vec, _mod_spec(cfg)]
    out_specs = [row, part, part, vec]
    out_shape = [jax.ShapeDtypeStruct((cfg.r, D), F32), part_shape, part_shape, jax.ShapeDtypeStruct((1, D), F32)]
    if dx_latent_only:
        out_specs[0] = pl.BlockSpec((TM, D), lambda i: (jnp.minimum(i, nlb - 1), 0))
        out_shape[0] = jax.ShapeDtypeStruct((cfg.nl, D), F32)
    if gate is not None:
        ins, in_specs = ins + list(gate[:2]), in_specs + [row, _mod_spec(cfg)]
        out_specs, out_shape = out_specs + [row, part], out_shape + [jax.ShapeDtypeStruct((cfg.r, D), BF16), part_shape]
    return pl.pallas_call(body, name=name, grid=(nblk,), in_specs=in_specs, out_specs=out_specs, out_shape=out_shape,
                          compiler_params=_params(1))(*ins)


def _mod_spec(cfg):
    return pl.BlockSpec((None, 6, D), lambda i: (cfg.mod_row(i), 0, 0))


def _norm_mod(xv, g, m_ref, ish, isc):
    r = lax.rsqrt(jnp.mean(xv * xv, axis=-1, keepdims=True) + EPS)
    return (xv * r * g * (1.0 + m_ref[isc:isc + 1, :]) + m_ref[ish:ish + 1, :]).astype(BF16)


def _accumulate_rows(first, ref, val):
    @pl.when(first)
    def _():
        ref[...] = val

    @pl.when(jnp.logical_not(first))
    def _():
        ref[...] += val


def _rope_tables(seq):
    rows = seq // GRID_W
    row = jnp.repeat(jnp.arange(rows), GRID_W).astype(F32)
    col = jnp.tile(jnp.arange(GRID_W), rows).astype(F32)
    half = HEAD_DIM // 2
    inv = ROPE_BASE ** (-jnp.arange(0, half, 2, dtype=F32) / half)
    ar, ac = row[:, None] * inv, col[:, None] * inv
    ang = jnp.concatenate([ar, ar, ac, ac], axis=-1)
    sign = jnp.tile(jnp.concatenate([-jnp.ones((16,), F32), jnp.ones((16,), F32)]), 2)
    cos = jnp.tile(jnp.cos(ang), (1, 2))
    sin = jnp.tile(jnp.sin(ang) * sign, (1, 2))
    cos = jnp.concatenate([cos, jnp.ones((TM, 2 * HEAD_DIM), F32)], axis=0)
    sin = jnp.concatenate([sin, jnp.zeros((TM, 2 * HEAD_DIM), F32)], axis=0)
    return cos, sin


def _rope(x, cos, sin_signed, sign):
    lane = lax.broadcasted_iota(jnp.int32, x.shape, 1)
    low = (lane % 32) < 16
    rot = jnp.where(low, pltpu.roll(x, 112, 1), pltpu.roll(x, 16, 1))
    return x * cos + sign * (rot * sin_signed)


def _in_proj(name, x, gvec, mod3, w_in, layer, cos_t, sin_t, cfg):
    nlb, bps = cfg.nl // TM, cfg.seq // TM
    joined = isinstance(x, tuple)

    def body(*refs):
        if joined:
            lat_ref, ctx_ref, g_ref, m_ref, w_ref, cos_ref, sin_ref, xs_ref, h_ref, qkv_ref, cp_ref = refs
            xv = jnp.where(pl.program_id(0) < nlb, lat_ref[...], ctx_ref[...])
            xs_ref[...] = xv
        else:
            x_ref, g_ref, m_ref, w_ref, cos_ref, sin_ref, h_ref, qkv_ref, cp_ref = refs
            xv = x_ref[...]
        hv = _norm_mod(xv, g_ref[...], m_ref, 0, 1)
        h_ref[...] = hv
        u = jnp.concatenate([_dot(hv, w_ref[j]) for j in range(N_SHARD)], axis=1)
        cos, sin = cos_ref[...], sin_ref[...]
        tiles = []
        for t in range(5):
            y = _rope(u[:, 128 * t:128 * (t + 1)], cos, sin, 1.0)
            tiles.append(y * (HEAD_DIM ** -0.5) if t < 4 else y)
        tiles.append(u[:, 640:768])
        qkv_ref[...] = jnp.concatenate(tiles, axis=1).astype(BF16)
        cp_ref[...] = u[:, 768:IN_W].astype(BF16)

    tab = pl.BlockSpec((TM, 128), lambda i: (jnp.where(i < nlb, i % bps, bps), 0))
    half = pl.BlockSpec((TM, 768), lambda i: (i, 0))
    row = pl.BlockSpec((TM, D), lambda i: (i, 0))
    x_in, x_specs = [x], [row]
    out_specs = [row, half, half]
    out_shape = [jax.ShapeDtypeStruct((cfg.r, D), BF16), jax.ShapeDtypeStruct((cfg.r, 768), BF16),
                 jax.ShapeDtypeStruct((cfg.r, 768), BF16)]
    if joined:
        x_in = list(x)
        x_specs = [pl.BlockSpec((TM, D), lambda i: (jnp.minimum(i, nlb - 1), 0)),
                   pl.BlockSpec((TM, D), lambda i: (jnp.maximum(i - nlb, 0), 0))]
        out_specs, out_shape = [row] + out_specs, [jax.ShapeDtypeStruct((cfg.r, D), F32)] + out_shape
    return pl.pallas_call(
        body, name=name, grid=(cfg.r // TM,),
        in_specs=x_specs + [pl.BlockSpec((1, D), lambda i: (0, 0)), _mod_spec(cfg),
                            pl.BlockSpec((None, N_SHARD, D, IN_SHARD), lambda i: (layer, 0, 0, 0)), tab, tab],
        out_specs=out_specs, out_shape=out_shape, compiler_params=_params(1))(*x_in, gvec, mod3, w_in, cos_t, sin_t)


def _att_specs(cfg):
    nlb, ncb = cfg.seq // QB, cfg.ctx // QB

    def qblk(s, qb):
        return jnp.where(qb < nlb, s * nlb + qb, cfg.nl // QB + s * ncb + qb - nlb)

    def near(off, col):
        return pl.BlockSpec((QB, 128), lambda s, qb: (s * nlb + jnp.clip(qb + off, 0, nlb - 1), col))

    def ctxs(col):
        return pl.BlockSpec((cfg.ctx, 128), lambda s, qb: (cfg.nl // cfg.ctx + s, col))

    qspec = pl.BlockSpec((QB, ATTN_W), lambda s, qb: (qblk(s, qb), 0))
    kv = [ctxs(4), ctxs(5), near(-1, 4), near(0, 4), near(1, 4), near(-1, 5), near(0, 5), near(1, 5)]
    return qblk, qspec, kv


def _att_scores(qb, nlb, sink_ref, q_ref, k_refs, v_refs, kh):
    is_lat = qb < nlb
    ii = lax.broadcasted_iota(jnp.int32, (4 * QB, 3 * QB), 0) % QB
    col = lax.broadcasted_iota(jnp.int32, (4 * QB, 3 * QB), 1)
    jj, blk = col % QB, col // QB
    off_p = jnp.where(jnp.logical_and(is_lat, qb >= 1), 0.0, NEG)
    off_c = jnp.where(is_lat, 0.0, NEG)
    off_n = jnp.where(jnp.logical_and(is_lat, qb <= nlb - 2), 0.0, NEG)
    inside = jnp.logical_or(blk == 1, jnp.logical_or(jnp.logical_and(blk == 0, jj >= ii),
                                                     jnp.logical_and(blk == 2, jj <= ii)))
    off = jnp.where(blk == 0, off_p, jnp.where(blk == 1, off_c, off_n))
    q4 = jnp.concatenate([q_ref[:, (4 * kh + g) * HEAD_DIM:(4 * kh + g + 1) * HEAD_DIM] for g in range(4)], axis=0)
    rg = lax.broadcasted_iota(jnp.int32, (4 * QB, 1), 0) // QB
    snk = jnp.where(rg == 0, sink_ref[4 * kh],
                    jnp.where(rg == 1, sink_ref[4 * kh + 1], jnp.where(rg == 2, sink_ref[4 * kh + 2], sink_ref[4 * kh + 3])))
    lanes = slice(kh * HEAD_DIM, (kh + 1) * HEAD_DIM)
    kx, vx = k_refs[0][:, lanes], v_refs[0][:, lanes]
    kl = jnp.concatenate([r[:, lanes] for r in k_refs[1:]], axis=0)
    vl = jnp.concatenate([r[:, lanes] for r in v_refs[1:]], axis=0)
    sx = _dot(q4, kx, "nt")
    sl = jnp.where(inside, _dot(q4, kl, "nt"), NEG) + off
    return q4, snk, (kx, kl), (vx, vl), (sx, sl)


def _att_fwd(name, qkv, sink, ctx_queries, cfg, ride=None):
    nlb, ncb = cfg.seq // QB, cfg.ctx // QB
    qblk, qspec, kvspecs = _att_specs(cfg)

    def body(sink_ref, q_ref, kx_ref, vx_ref, kp_ref, kc_ref, kn_ref, vp_ref, vc_ref, vn_ref, o_ref, lse_ref):
        qb = pl.program_id(1)
        for kh in range(2):
            q4, snk, _, vs, ss = _att_scores(qb, nlb, sink_ref, q_ref, (kx_ref, kp_ref, kc_ref, kn_ref),
                                             (vx_ref, vp_ref, vc_ref, vn_ref), kh)
            m = snk
            for s_ in ss:
                m = jnp.maximum(m, jnp.max(s_, axis=-1, keepdims=True))
            den = jnp.exp(snk - m)
            o4 = jnp.zeros((4 * QB, HEAD_DIM), F32)
            for s_, v_ in zip(ss, vs):
                p = jnp.exp(s_ - m)
                den = den + jnp.sum(p, axis=-1, keepdims=True)
                o4 = o4 + _dot(p, v_)
            o4 = o4 / den
            lse = m + jnp.log(den)
            for g in range(4):
                h = 4 * kh + g
                o_ref[:, h * HEAD_DIM:(h + 1) * HEAD_DIM] = o4[g * QB:(g + 1) * QB].astype(BF16)
                lse_ref[:, h:h + 1] = lse[g * QB:(g + 1) * QB]

    return _hosted_call(
        body, ride, name, (cfg.b, nlb + (ncb if ctx_queries else 0)), [sink] + [qkv] * 9,
        [pl.BlockSpec(memory_space=pltpu.SMEM), qspec, *kvspecs],
        [jax.ShapeDtypeStruct((cfg.r, D), BF16), jax.ShapeDtypeStruct((cfg.r, N_HEADS), F32)],
        [pl.BlockSpec((QB, ATTN_W), lambda s, qb: (qblk(s, qb), 0)),
         pl.BlockSpec((QB, N_HEADS), lambda s, qb: (qblk(s, qb), 0))], [], _params(2))


def _att_bwd(name, qkv, mix, dmix, lse, sink, cos_t, sin_t, ctx_queries, cfg, ride=None):
    nlb, ncb = cfg.seq // QB, cfg.ctx // QB
    nqb = nlb + (ncb if ctx_queries else 0)
    qblk, qspec, kvspecs = _att_specs(cfg)

    def body(sink_ref, q_ref, kx_ref, vx_ref, kp_ref, kc_ref, kn_ref, vp_ref, vc_ref, vn_ref, o_ref, do_ref,
             lse_ref, cosq_ref, sinq_ref, cosk_ref, sink_tab_ref, dq_ref, dkvl_ref, dkvc_ref, dsink_ref,
             accl, accc, dqs):
        s_id, qb = pl.program_id(0), pl.program_id(1)

        @pl.when(qb == 0)
        def _():
            accl[...] = jnp.zeros_like(accl)
            accc[...] = jnp.zeros_like(accc)

        @pl.when(jnp.logical_and(s_id == 0, qb == 0))
        def _():
            dsink_ref[...] = jnp.zeros_like(dsink_ref)

        starts = [pl.multiple_of(jnp.clip(qb + off, 0, nlb - 1) * QB, QB) for off in (-1, 0, 1)]
        for kh in range(2):
            q4, snk, ks, vs, ss = _att_scores(qb, nlb, sink_ref, q_ref, (kx_ref, kp_ref, kc_ref, kn_ref),
                                              (vx_ref, vp_ref, vc_ref, vn_ref), kh)
            lanes = slice(kh * HEAD_DIM, (kh + 1) * HEAD_DIM)
            heads =[slice((4 * kh + g) * HEAD_DIM, (4 * kh + g + 1) * HEAD_DIM) for g in range(4)]
            do4 = jnp.concatenate([do_ref[:, hs] for hs in heads], axis=0)
            o4 = jnp.concatenate([o_ref[:, hs] for hs in heads], axis=0).astype(F32)
            lse4 = jnp.concatenate([lse_ref[:, 4 * kh + g:4 * kh + g + 1] for g in range(4)], axis=0)
            delta = jnp.sum(do4 * o4, axis=-1, keepdims=True)
            dq4 = jnp.zeros((4 * QB, HEAD_DIM), F32)
            dks, dvs = [], []
            for s_, k_, v_ in zip(ss, ks, vs):
                p = jnp.exp(s_ - lse4)
                ds = p * (_dot(do4, v_, "nt") - delta)
                dq4 = dq4 + _dot(ds, k_)
                dks.append(_dot(ds, q4, "tn"))
                dvs.append(_dot(p, do4, "tn"))
            accc[:, lanes] += dks[0]
            accc[:, 128 + kh * HEAD_DIM:128 + (kh + 1) * HEAD_DIM] += dvs[0]
            for t, st in enumerate(starts):
                accl[pl.ds(st, QB), lanes] += dks[1][t * QB:(t + 1) * QB]
                accl[pl.ds(st, QB), 128 + kh * HEAD_DIM:128 + (kh + 1) * HEAD_DIM] += dvs[1][t * QB:(t + 1) * QB]
            dsk = -jnp.exp(snk - lse4) * delta
            for g in range(4):
                h = 4 * kh + g
                dsink_ref[h:h + 1, :] += jnp.broadcast_to(_colsum(dsk[g * QB:(g + 1) * QB]), (1, 128))
                dqs[:, heads[g]] = dq4[g * QB:(g + 1) * QB]
        cos, sin = cosq_ref[...], sinq_ref[...]
        dq_ref[...] = jnp.concatenate(
            [_rope(dqs[:, 128 * t:128 * (t + 1)], cos, sin, -1.0) * (HEAD_DIM ** -0.5) for t in range(4)],
            axis=1).astype(BF16)

        @pl.when(qb == nqb - 1)
        def _():
            dk = _rope(accl[:, 0:128], cosk_ref[...], sink_tab_ref[...], -1.0)
            dkvl_ref[...] = jnp.concatenate([dk, accl[:, 128:256]], axis=1).astype(BF16)
            dkvc_ref[...] = accc[...].astype(BF16)

    rowq = lambda w: pl.BlockSpec((QB, w), lambda s, qb: (qblk(s, qb), 0))
    tabq = pl.BlockSpec((QB, 128), lambda s, qb: (jnp.where(qb < nlb, qb, cfg.seq // QB), 0))
    tabk = pl.BlockSpec((cfg.seq, 128), lambda s, qb: (0, 0))
    return _hosted_call(
        body, ride, name, (cfg.b, nqb), [sink] + [qkv] * 9 + [mix, dmix, lse, cos_t, sin_t, cos_t, sin_t],
        [pl.BlockSpec(memory_space=pltpu.SMEM), qspec, *kvspecs, rowq(ATTN_W), rowq(ATTN_W), rowq(N_HEADS),
         tabq, tabq, tabk, tabk],
        [jax.ShapeDtypeStruct((cfg.r, IN_W), BF16), jax.ShapeDtypeStruct((cfg.nl, 256), BF16),
         jax.ShapeDtypeStruct((cfg.nc, 256), BF16), jax.ShapeDtypeStruct((N_HEADS, 128), F32)],
        [rowq(ATTN_W), pl.BlockSpec((cfg.seq, 256), lambda s, qb: (s, 0)),
         pl.BlockSpec((cfg.ctx, 256), lambda s, qb: (s, 0)), pl.BlockSpec((N_HEADS, 128), lambda s, qb: (0, 0))],
        [pltpu.VMEM((cfg.seq, 256), F32), pltpu.VMEM((cfg.ctx, 256), F32), pltpu.VMEM((QB, ATTN_W), F32)], _params(2))


def _pool_geometry(n, c):
    lane = lax.broadcasted_iota(jnp.int32, (1, POOL_W), 1) // HEAD_DIM
    wl = jnp.where(lane == 0, 1, jnp.where(lane == 1, 2, jnp.where(lane == 2, 4, 8)))
    wr = wl - 1
    t = c * CHUNK + lax.broadcasted_iota(jnp.int32, (CHUNK, POOL_W), 0)
    cnt = (jnp.minimum(t + wr, n - 1) - jnp.maximum(t - wl, 0) + 1).astype(F32)
    return wl, wr, cnt


def _build_phases(src, ph, c):
    for s in range(1, 8):
        ph[s - 1] = src[c * CHUNK + s:c * CHUNK + s + CHUNK + 24, :]


def _window(src, ph, c, off):
    a, s = divmod(off, 8)
    if s == 0:
        return src[c * CHUNK + 8 * a:c * CHUNK + 8 * a + CHUNK, :]
    return ph[s - 1, 8 * a:8 * a + CHUNK, :]


def _conv_chunk(hp, ph, dw_ref, dwb_ref, c):
    _build_phases(hp, ph, c)
    acc = jnp.zeros((CHUNK, CONV_W), F32) + dwb_ref[...]
    for j in range(CONV_K):
        acc = acc + dw_ref[j:j + 1, :] * _window(hp, ph, c, j + 1)
    return acc


def _fill_glu(cp_ref, hp, n):
    hp[0:PAD, :] = jnp.zeros((PAD, CONV_W), F32)
    hp[PAD + n:2 * PAD + n, :] = jnp.zeros((PAD, CONV_W), F32)
    for c in range(n // CHUNK):
        rows = slice(c * CHUNK, (c + 1) * CHUNK)
        a = cp_ref[rows, 0:CONV_W].astype(F32)
        g = cp_ref[rows, CONV_W:2 * CONV_W].astype(F32)
        hp[PAD + c * CHUNK:PAD + (c + 1) * CHUNK, :] = a * _sigmoid(g)


def _fill_pool(cp_ref, pp, n):
    pp[0:PAD, :] = jnp.zeros((PAD, POOL_W), F32)
    pp[PAD + n:2 * PAD + n, :] = jnp.zeros((PAD, POOL_W), F32)
    for c in range(n // CHUNK):
        pp[PAD + c * CHUNK:PAD + (c + 1) * CHUNK, :] = cp_ref[c * CHUNK:(c + 1) * CHUNK, 2 * CONV_W:768].astype(F32)


LV = CHUNK + 2 * PAD
_LEVELS = pltpu.VMEM((3, LV + 16, POOL_W), F32)


def _clear_level_edges(lv):
    for b in range(3):
        lv[b, 0:8] = jnp.zeros((8, POOL_W), F32)
        lv[b, 8 + LV:16 + LV] = jnp.zeros((8, POOL_W), F32)


def _window_sums(src, lv, c, lead):
    lv[0, 8:8 + LV] = src[c * CHUNK:c * CHUNK + LV, :]
    lo = 7 if lead < 0 else 8
    lv[1, 8:8 + LV] = lv[0, lo:lo + LV] + lv[0, lo + 1:lo + 1 + LV]
    group = lax.broadcasted_iota(jnp.int32, (1, POOL_W), 1) // HEAD_DIM
    rows = slice(8 + PAD, 8 + PAD + CHUNK)
    res = lv[1, rows]
    cur = 1
    for g, s in ((1, 1), (2, 2), (3, 4)):
        nxt = 3 - cur
        lv[nxt, 8:8 + LV] = lv[cur, 8 - s:8 - s + LV] + lv[cur, 8 + s:8 + s + LV]
        res = jnp.where(group >= g, lv[nxt, rows], res)
        cur = nxt
    return res


def _pool_chunk(pp, lv, n, c):
    _, _, cnt = _pool_geometry(n, c)
    return _window_sums(pp, lv, c, -1) / cnt - pp[PAD + c * CHUNK:PAD + (c + 1) * CHUNK, :], cnt


def _seq_specs(n, blk_off, width, col=0):
    return pl.BlockSpec((n, width), lambda s: (blk_off + s, col))


def _full(shape):
    return pl.BlockSpec(shape, lambda s: (0,) * len(shape))


_PHASES = pltpu.VMEM((7, CHUNK + 24, CONV_W), F32)


def _convpool_fwd(name, cpin, mix, yconv, prm, n, blk_off, cfg):
    dw, dwb, lng, lnb, wbd, ps = prm
    n_alias = 1 if yconv is None else 2

    def body(*refs):
        cp_ref, dw_ref, dwb_ref, lng_ref, lnb_ref, wbd_ref, ps_ref = refs[:7]
        out_ref, y_ref, hp, pp, ph, lv = refs[7 + n_alias:]
        _fill_glu(cp_ref, hp, n)
        _fill_pool(cp_ref, pp, n)
        _clear_level_edges(lv)
        for c in range(n // CHUNK):
            rows = slice(c * CHUNK, (c + 1) * CHUNK)
            y = _conv_chunk(hp, ph, dw_ref, dwb_ref, c)
            y_ref[rows, :] = y
            d = y - jnp.mean(y, axis=-1, keepdims=True)
            hn = d * lax.rsqrt(jnp.mean(d * d, axis=-1, keepdims=True) + EPS) * lng_ref[...] + lnb_ref[...]
            out_ref[rows, 0:CONV_W] = (hn * _sigmoid(hn)).astype(BF16)
            yp, _ = _pool_chunk(pp, lv, n, c)
            out_ref[rows, CONV_W:2 * CONV_W] = (_dot(yp, wbd_ref[...]) * ps_ref[...]).astype(BF16)

    through = [mix] if yconv is None else [mix, yconv]
    return pl.pallas_call(
        body, name=name, grid=(cfg.b,),
        in_specs=[_seq_specs(n, blk_off, 768), _full((32, CONV_W)), _full((1, CONV_W)), _full((1, CONV_W)),
                  _full((1, CONV_W)), _full((POOL_W, POOL_W)), _full((1, POOL_W))] + [_ANY] * n_alias,
        out_specs=[_seq_specs(n, blk_off, 512, 1), _seq_specs(n, blk_off, CONV_W)],
        out_shape=[jax.ShapeDtypeStruct((cfg.r, D), BF16), jax.ShapeDtypeStruct((cfg.r, CONV_W), F32)],
        scratch_shapes=[pltpu.VMEM((n + 2 * PAD, CONV_W), F32), pltpu.VMEM((n + 2 * PAD, POOL_W), F32), _PHASES, _LEVELS],
        input_output_aliases={7 + i: i for i in range(n_alias)},
        compiler_params=_params(1))(cpin, dw, dwb, lng, lnb, wbd, ps, *through)


_SMALL_SHAPES = [(32, CONV_W), (1, CONV_W), (1, CONV_W), (1, CONV_W), (POOL_W, POOL_W), (1, POOL_W)]


def _convpool_bwd(name, cpin, yconv, dmix, prm, acc_in, n, blk_off, cfg):
    dw, dwb, lng, lnb, wbd, ps = prm
    nch = n // CHUNK

    def body(cp_ref, y_ref, dm_ref, dw_ref, dwb_ref, lng_ref, lnb_ref, wbd_ref, ps_ref, dcp_in,
             a_dw, a_dwb, a_lng, a_lnb, a_wbd, a_ps,
             dcp_ref, o_dw, o_dwb, o_lng, o_lnb, o_wbd, o_ps, hp, dyp, pp, wp, dyv, dwacc, ph, lv):
        s = pl.program_id(0)
        _clear_level_edges(lv)

        @pl.when(s == 0)
        def _():
            for o_, a_ in ((o_dw, a_dw), (o_dwb, a_dwb), (o_lng, a_lng), (o_lnb, a_lnb), (o_wbd, a_wbd), (o_ps, a_ps)):
                o_[...] = a_[...]
            dwacc[...] = jnp.zeros_like(dwacc)

        _fill_glu(cp_ref, hp, n)
        _fill_pool(cp_ref, pp, n)
        for ref in (dyp, wp):
            ref[0:PAD, :] = jnp.zeros((PAD, CONV_W), F32)
            ref[PAD + n:2 * PAD + n, :] = jnp.zeros((PAD, CONV_W), F32)
        for c in range(nch):
            rows = slice(c * CHUNK, (c + 1) * CHUNK)
            y = y_ref[rows, :]
            d = y - jnp.mean(y, axis=-1, keepdims=True)
            rstd = lax.rsqrt(jnp.mean(d * d, axis=-1, keepdims=True) + EPS)
            xh = d * rstd
            hn = xh * lng_ref[...] + lnb_ref[...]
            sg = _sigmoid(hn)
            dhn = dm_ref[rows, 0:CONV_W] * (sg * (1.0 + hn * (1.0 - sg)))
            o_lnb[...] += _colsum(dhn)
            o_lng[...] += _colsum(dhn * xh)
            dxh = dhn * lng_ref[...]
            dy = rstd * (dxh - jnp.mean(dxh, axis=-1, keepdims=True) - xh * jnp.mean(dxh * xh, axis=-1, keepdims=True))
            o_dwb[...] += _colsum(dy)
            dyp[PAD + c * CHUNK:PAD + (c + 1) * CHUNK, :] = dy
            _build_phases(hp, ph, c)
            for j in range(CONV_K):
                prod = dy * _window(hp, ph, c, j + 1)
                dwacc[8 * j:8 * j + 8, :] += jnp.sum(prod.reshape(CHUNK // 8, 8, CONV_W), axis=0)
            yp, cnt = _pool_chunk(pp, lv, n, c)
            dz = dm_ref[rows, CONV_W:2 * CONV_W]
            o_ps[...] += _colsum(dz * _dot(yp, wbd_ref[...]))
            dzs = dz * ps_ref[...]
            o_wbd[...] += _dot(yp, dzs, "tn")
            dv = _dot(dzs, wbd_ref[...], "nt")
            dyv[rows, :] = dv
            wp[PAD + c * CHUNK:PAD + (c + 1) * CHUNK, :] = dv / cnt
        for c in range(nch):
            rows = slice(c * CHUNK, (c + 1) * CHUNK)
            _build_phases(dyp, ph, c)
            dh = jnp.zeros((CHUNK, CONV_W), F32)
            for j in range(CONV_K):
                dh = dh + dw_ref[j:j + 1, :] * _window(dyp, ph, c, 31 - j)
            a = cp_ref[rows, 0:CONV_W].astype(F32)
            sg = _sigmoid(cp_ref[rows, CONV_W:2 * CONV_W].astype(F32))
            dcp_ref[rows, 0:CONV_W] = (dh * sg).astype(BF16)
            dcp_ref[rows, CONV_W:2 * CONV_W] = (dh * a * sg * (1.0 - sg)).astype(BF16)
            dcp_ref[rows, 2 * CONV_W:768] = (_window_sums(wp, lv, c, 1) - dyv[rows, :]).astype(BF16)

        @pl.when(s == cfg.b - 1)
        def _():
            for j in range(CONV_K):
                o_dw[j:j + 1, :] += _colsum(dwacc[8 * j:8 * j + 8, :])

    small_specs = [_full(sh) for sh in _SMALL_SHAPES]
    return pl.pallas_call(
        body, name=name, grid=(cfg.b,),
        in_specs=[_seq_specs(n, blk_off, 768), _seq_specs(n, blk_off, CONV_W), _seq_specs(n, blk_off, 512, 1),
                  *small_specs, _ANY, *small_specs],
        out_specs=[_seq_specs(n, blk_off, 768, 1), *small_specs],
        out_shape=[jax.ShapeDtypeStruct((cfg.r, IN_W), BF16)] + [jax.ShapeDtypeStruct(sh, F32) for sh in _SMALL_SHAPES],
        scratch_shapes=[pltpu.VMEM((n + 2 * PAD, CONV_W), F32), pltpu.VMEM((n + 2 * PAD, CONV_W), F32),
                        pltpu.VMEM((n + 2 * PAD, POOL_W), F32), pltpu.VMEM((n + 2 * PAD, POOL_W), F32),
                        pltpu.VMEM((n, POOL_W), F32), pltpu.VMEM((8 * 32, CONV_W), F32), _PHASES, _LEVELS],
        input_output_aliases={9: 0}, compiler_params=_params(1))(cpin, yconv, dmix, dw, dwb, lng, lnb, wbd, ps, *acc_in)


def _place_kv(name, du, dkvl, dkvc, with_ctx, cfg):
    nlb = cfg.nl // TM

    def body(l_ref, c_ref, du_in, o_ref):
        i = pl.program_id(0)
        o_ref[...] = jnp.where(i < nlb, l_ref[...], c_ref[...])

    return pl.pallas_call(
        body, name=name, grid=(cfg.r // TM if with_ctx else nlb,),
        in_specs=[pl.BlockSpec((TM, 256), lambda i: (jnp.minimum(i, nlb - 1), 0)),
                  pl.BlockSpec((TM, 256), lambda i: (jnp.maximum(i - nlb, 0), 0)), _ANY],
        out_specs=pl.BlockSpec((TM, 256), lambda i: (i, 2)), out_shape=jax.ShapeDtypeStruct((cfg.r, IN_W), BF16),
        input_output_aliases={2: 0}, compiler_params=_params(1))(dkvl, dkvc, du)


def _place_ctx_kv_only(name, du, dkvc, cfg):
    nlb = cfg.nl // TM

    def body(c_ref, du_in, o_ref):
        o_ref[...] = jnp.zeros_like(o_ref)
        o_ref[:, ATTN_W:ATTN_W + 256] = c_ref[...]

    return pl.pallas_call(
        body, name=name, grid=(cfg.nc // TM,), in_specs=[pl.BlockSpec((TM, 256), lambda i: (i, 0)), _ANY],
        out_specs=pl.BlockSpec((TM, IN_W), lambda i: (nlb + i, 0)), out_shape=jax.ShapeDtypeStruct((cfg.r, IN_W), BF16),
        input_output_aliases={1: 0}, compiler_params=_params(1))(dkvc, du)


def _ffn_in(name, h, w_ffn_in, layer, nblk, cfg, ride=None):
    def body(h_ref, wg_ref, wu_ref, fac_ref, act_ref):
        hv = h_ref[...]
        g = _dot(hv, wg_ref[...])
        u = _dot(hv, wu_ref[...])
        s = _sigmoid(g)
        gs = g * s
        fac_ref[0] = ((s + gs * (1.0 - s)) * u).astype(BF16)
        fac_ref[1] = gs.astype(BF16)
        act_ref[...] = (gs * u).astype(BF16)

    wspec = lambda base: pl.BlockSpec((None, None, D, HALF_FF), lambda j, i: (layer, base + j, 0, 0))
    return _hosted_call(
        body, ride, name, (2, nblk), [h, w_ffn_in, w_ffn_in],
        [pl.BlockSpec((TM, D), lambda j, i: (i, 0)), wspec(0), wspec(2)],
        [jax.ShapeDtypeStruct((2, cfg.r, D_FF), BF16), jax.ShapeDtypeStruct((cfg.r, D_FF), BF16)],
        [pl.BlockSpec((2, TM, HALF_FF), lambda j, i: (0, i, j)), pl.BlockSpec((TM, HALF_FF), lambda j, i: (i, j))],
        [], _params(2))


def _row_block(rows, cols, max_bytes=1 << 20):
    best = 16
    for t in range(16, rows + 1, 16):
        if rows % t == 0 and t * cols * 4 <= max_bytes:
            best = t
    assert rows % best == 0
    return best


def _pair_add(name, own32, recv, c_idx):
    _, _, s0, s1 = own32.shape
    tr = _row_block(s0, s1)

    def body(c_ref, a_ref, b_ref, o_ref):
        o_ref[...] = (a_ref[...].astype(F32) + b_ref[...].astype(F32)).astype(BF16)

    grid_spec = pltpu.PrefetchScalarGridSpec(
        num_scalar_prefetch=1, grid=(N_SHARD * s0 // tr,),
        in_specs=[pl.BlockSpec((None, tr, s1), lambda i, c: (c[0], i, 0)), pl.BlockSpec((tr, s1), lambda i, c: (i, 0))],
        out_specs=pl.BlockSpec((tr, s1), lambda i, c: (i, 0)))
    out = pl.pallas_call(body, name=name, grid_spec=grid_spec, out_shape=jax.ShapeDtypeStruct((N_SHARD * s0, s1), BF16),
                         compiler_params=_params(1))(c_idx, own32.reshape(2, N_SHARD * s0, s1), recv.reshape(N_SHARD * s0, s1))
    return out.reshape(N_SHARD, s0, s1)


def _shard_sum(name, pair_sum, recv, jc_idx):
    _, s0, s1 = pair_sum.shape
    tr = _row_block(s0, s1)

    def body(jc_ref, a_ref, b_ref, o_ref):
        o_ref[...] = ((a_ref[...].astype(F32) + b_ref[0].astype(F32)) + b_ref[1].astype(F32)) + b_ref[2].astype(F32)

    grid_spec = pltpu.PrefetchScalarGridSpec(
        num_scalar_prefetch=1, grid=(s0 // tr,),
        in_specs=[pl.BlockSpec((None, tr, s1), lambda i, jc: (jc[0], i, 0)), pl.BlockSpec((3, tr, s1), lambda i, jc: (0, i, 0))],
        out_specs=pl.BlockSpec((None, tr, s1), lambda i, jc: (jc[1], i, 0)))
    return pl.pallas_call(body, name=name, grid_spec=grid_spec, out_shape=jax.ShapeDtypeStruct((2, s0, s1), F32),
                          compiler_params=_params(1))(jc_idx, pair_sum, recv)


def _adamw_math(w, g, m, v):
    m = ADAM_B1 * m + (1.0 - ADAM_B1) * g
    v = ADAM_B2 * v + (1.0 - ADAM_B2) * (g * g)
    m_hat = m / (1.0 - ADAM_B1 ** ADAM_STEP)
    v_hat = v / (1.0 - ADAM_B2 ** ADAM_STEP)
    delta = -ADAM_LR * (m_hat / (jnp.sqrt(v_hat) + ADAM_EPS) + ADAM_WD * w)
    return delta, m, v


def _adamw(name, w, g, m, v):
    rows, cols = w.shape
    tr = rows if rows % 16 else _row_block(rows, cols, 1 << 19)

    def body(w_ref, g_ref, m_ref, v_ref, d_ref, mo_ref, vo_ref):
        d, mn, vn = _adamw_math(w_ref[...], g_ref[...], m_ref[...], v_ref[...])
        d_ref[...] = d
        mo_ref[...] = mn
        vo_ref[...] = vn

    spec = pl.BlockSpec((tr, cols), lambda i: (i, 0))
    shape = jax.ShapeDtypeStruct((rows, cols), F32)
    return pl.pallas_call(body, name=name, grid=(rows // tr,), in_specs=[spec] * 4, out_specs=[spec] * 3,
                          out_shape=[shape] * 3, compiler_params=_params(1))(w, g, m, v)


def _adamw_layers(name, w, g_layers, m, v):
    rows, cols = w.shape
    s0 = rows // 2
    tr = _row_block(s0, cols, 1 << 19)
    nb = s0 // tr

    def body(w_ref, g0_ref, g1_ref, m_ref, v_ref, g_ref, d_ref, mo_ref, vo_ref):
        g = jnp.where(pl.program_id(0) < nb, g0_ref[...], g1_ref[...])
        d, mn, vn = _adamw_math(w_ref[...], g, m_ref[...], v_ref[...])
        g_ref[...] = g
        d_ref[...] = d
        mo_ref[...] = mn
        vo_ref[...] = vn

    spec = pl.BlockSpec((tr, cols), lambda i: (i, 0))
    shape = jax.ShapeDtypeStruct((rows, cols), F32)
    return pl.pallas_call(
        body, name=name, grid=(2 * nb,),
        in_specs=[spec, pl.BlockSpec((tr, cols), lambda i: (jnp.minimum(i, nb - 1), 0)),
                  pl.BlockSpec((tr, cols), lambda i: (jnp.maximum(i - nb, 0), 0)), spec, spec],
        out_specs=[spec] * 4, out_shape=[shape] * 4, compiler_params=_params(1))(w, g_layers[0], g_layers[1], m, v)


def _position():
    return lax.axis_index("x"), lax.axis_index("y"), lax.axis_index("c")


def _other_chips(x, y):
    return [(1 - x, y), (x, 1 - y), (1 - x, 1 - y)]


def _run_ride(name, ride):
    n_in, n_out = len(ride.ins), len(ride.out_shape)

    def body(*refs):
        parts = (refs[:n_in], refs[n_in:n_in + n_out], refs[n_in + n_out:])
        ride.start(*parts)
        ride.finish(*parts)

    return pl.pallas_call(
        body, name=name, in_specs=[_ANY] * n_in, out_specs=[_ANY] * n_out, out_shape=ride.out_shape,
        scratch_shapes=ride.scratch, compiler_params=pltpu.CompilerParams(vmem_limit_bytes=VMEM_LIMIT_V7X))(*ride.ins)


def _gather_ride(shards):
    n = len(shards)

    def copies(ins, outs, scr):
        ssem, rsem = scr[n], scr[n + 1]
        x, y, c = _position()
        me, sibling = 2 * x + y, (x, y, 1 - c)

        def remote(src, dst, i, dev):
            return pltpu.make_async_remote_copy(src, dst, ssem.at[i], rsem.at[i], device_id=dev, device_id_type=_MESH)

        fetch_out, fetch_in, pass_out, pass_in = [], [], [], []
        for a, (src, dst) in enumerate(zip(ins, outs)):
            for k, (px, py) in enumerate(_other_chips(x, y)):
                j, i1, i2 = 2 * px + py, 3 * a + k, 3 * n + 3 * a + k
                fetch_out.append(remote(src.at[c], dst.at[me, c], i1, (px, py, c)))
                fetch_in.append(remote(src.at[c], dst.at[j, c], i1, (px, py, c)))
                pass_out.append(remote(dst.at[j, c], dst.at[j, c], i2, sibling))
                pass_in.append(remote(dst.at[j, 1 - c], dst.at[j, 1 - c], i2, sibling))
        return me, fetch_out, fetch_in, pass_out, pass_in

    def start(ins, outs, scr):
        bufs, lsem = scr[:n], scr[n + 2]
        me, fetch_out, _, _, _ = copies(ins, outs, scr)
        for cp in fetch_out:
            cp.start()
        loads = []
        for a, (src, buf) in enumerate(zip(ins, bufs)):
            ld = pltpu.make_async_copy(src, buf, lsem.at[2 * a])
            ld.start()
            loads.append(ld)
        for a, (ld, buf, dst) in enumerate(zip(loads, bufs, outs)):
            ld.wait()
            st = pltpu.make_async_copy(buf, dst.at[me], lsem.at[2 * a + 1])
            st.start()
            st.wait()

    def finish(ins, outs, scr):
        _, fetch_out, fetch_in, pass_out, pass_in = copies(ins, outs, scr)
        for arrived, onward in zip(fetch_in, pass_out):
            arrived.wait_recv()
            onward.start()
        for cp in pass_in:
            cp.wait_recv()
        for cp in fetch_out + pass_out:
            cp.wait_send()

    return _Ride(list(shards), [jax.ShapeDtypeStruct((N_SHARD,) + s.shape, s.dtype) for s in shards],
                 [pltpu.VMEM(s.shape, s.dtype) for s in shards]
                 + [pltpu.SemaphoreType.DMA((6 * n,)), pltpu.SemaphoreType.DMA((6 * n,)), pltpu.SemaphoreType.DMA((2 * n,))],
                 start, finish)


def _comm(name, ins, out_shape, n_remote, plan):
    n_in, n_out = len(ins), len(out_shape)

    def body(*refs):
        plan(refs[:n_in], refs[n_in:n_in + n_out], *refs[n_in + n_out:])

    return pl.pallas_call(
        body, name=name, in_specs=[_ANY] * n_in, out_specs=[_ANY] * n_out, out_shape=out_shape,
        scratch_shapes=[pltpu.SemaphoreType.DMA((n_remote,)), pltpu.SemaphoreType.DMA((n_remote,))])(*ins)


def _send_other_half(name, grads_bf):
    n = len(grads_bf)

    def plan(ins, outs, ssem, rsem):
        x, y, c = _position()
        started = []
        for a, (src, dst) in enumerate(zip(ins, outs)):
            cp = pltpu.make_async_remote_copy(src.at[1 - c], dst, ssem.at[a], rsem.at[a], device_id=(x, y, 1 - c),
                                              device_id_type=_MESH)
            cp.start()
            started.append(cp)
        for cp in started:
            cp.wait_recv()
        for cp in started:
            cp.wait_send()

    shapes = [jax.ShapeDtypeStruct(s.shape[1:], s.dtype) for s in grads_bf]
    return _comm(name, grads_bf, shapes, n, plan)


def _exchange_ride(pair_sums):
    n = len(pair_sums)

    def copies(ins, outs, scr):
        ssem, rsem = scr
        x, y, c = _position()
        return [pltpu.make_async_remote_copy(src.at[2 * px + py], dst.at[k], ssem.at[3 * a + k], rsem.at[3 * a + k],
                                             device_id=(px, py, c), device_id_type=_MESH)
                for a, (src, dst) in enumerate(zip(ins, outs)) for k, (px, py) in enumerate(_other_chips(x, y))]

    def start(ins, outs, scr):
        for cp in copies(ins, outs, scr):
            cp.start()

    def finish(ins, outs, scr):
        for cp in copies(ins, outs, scr):
            cp.wait_recv()
        for cp in copies(ins, outs, scr):
            cp.wait_send()

    return _Ride(list(pair_sums), [jax.ShapeDtypeStruct((3,) + s.shape[1:], s.dtype) for s in pair_sums],
                 [pltpu.SemaphoreType.DMA((3 * n,)), pltpu.SemaphoreType.DMA((3 * n,))], start, finish)


def _swap_reduced(name, grads):
    n = len(grads)

    def body(*refs):
        ins, outs, ssem, rsem = refs[:n], refs[n:2 * n], refs[2 * n], refs[2 * n + 1]
        x, y, c = _position()
        sent = []
        for a, (src, dst) in enumerate(zip(ins, outs)):
            cp = pltpu.make_async_remote_copy(src.at[c], dst.at[c], ssem.at[a], rsem.at[a], device_id=(x, y, 1 - c),
                                              device_id_type=_MESH)
            cp.start()
            sent.append(cp)
        for a, (src, dst) in enumerate(zip(ins, outs)):
            pltpu.make_async_remote_copy(src.at[1 - c], dst.at[1 - c], ssem.at[a], rsem.at[a], device_id=(x, y, 1 - c),
                                         device_id_type=_MESH).wait_recv()
        for cp in sent:
            cp.wait_send()

    return pl.pallas_call(
        body, name=name, in_specs=[_ANY] * n, out_specs=[_ANY] * n,
        out_shape=[jax.ShapeDtypeStruct(g.shape, g.dtype) for g in grads],
        scratch_shapes=[pltpu.SemaphoreType.DMA((n,)), pltpu.SemaphoreType.DMA((n,))],
        input_output_aliases={a: a for a in range(n)})(*grads)


_FLIPS = [(dx, dy, dc) for dx in (0, 1) for dy in (0, 1) for dc in (0, 1) if dx + dy + dc]
_VMEM = pl.BlockSpec(memory_space=pltpu.VMEM)


def _to_all(src_of, dst, ssem, rsem):
    x, y, c = _position()
    me = 4 * x + 2 * y + c
    peers = [((x + dx) % 2, (y + dy) % 2, (c + dc) % 2) for dx, dy, dc in _FLIPS]
    sent = []
    for k, (px, py, pc) in enumerate(peers):
        cp = pltpu.make_async_remote_copy(src_of(2 * px + py), dst.at[me], ssem.at[k], rsem.at[k],
                                          device_id=(px, py, pc), device_id_type=_MESH)
        cp.start()
        sent.append(cp)
    for k, (px, py, pc) in enumerate(peers):
        pltpu.make_async_remote_copy(src_of(2 * px + py), dst.at[4 * px + 2 * py + pc], ssem.at[k], rsem.at[k],
                                     device_id=(px, py, pc), device_id_type=_MESH).wait_recv()
    for cp in sent:
        cp.wait_send()
    return me, 2 * x + y


def _share_small(name, block, total):
    shape = block.shape

    def body(in_ref, out_ref, *scratch):
        buf, ssem, rsem = (out_ref,) + scratch if not total else scratch
        me, _ = _to_all(lambda chip: in_ref, buf, ssem, rsem)
        buf[me] = in_ref[...]
        if total:
            acc = buf[0]
            for d in range(1, 8):
                acc = acc + buf[d]
            out_ref[...] = acc

    sems = [pltpu.SemaphoreType.DMA((7,)), pltpu.SemaphoreType.DMA((7,))]
    return pl.pallas_call(
        body, name=name, in_specs=[_VMEM], out_specs=_VMEM,
        out_shape=jax.ShapeDtypeStruct(shape if total else (8,) + shape, F32),
        scratch_shapes=([pltpu.VMEM((8,) + shape, F32)] if total else []) + sems,
        compiler_params=pltpu.CompilerParams(vmem_limit_bytes=VMEM_LIMIT_V7X))(block)


def _mod_rows_exchange(mv):
    def body(mv_ref, out_ref, ssem, rsem):
        x, y, c = _position()
        me = 2 * x + y
        out_ref[me] = mv_ref[4 * x + 2 * y + c]
        sent = []
        for k, (px, py) in enumerate(_other_chips(x, y)):
            cp = pltpu.make_async_remote_copy(mv_ref.at[4 * px + 2 * py + c], out_ref.at[me], ssem.at[k], rsem.at[k],
                                              device_id=(px, py, c), device_id_type=_MESH)
            cp.start()
            sent.append(cp)
        for k, (px, py) in enumerate(_other_chips(x, y)):
            pltpu.make_async_remote_copy(mv_ref.at[0], out_ref.at[2 * px + py], ssem.at[k], rsem.at[k],
                                         device_id=(px, py, c), device_id_type=_MESH).wait_recv()
        for cp in sent:
            cp.wait_send()

    return pl.pallas_call(
        body, name="mod_rows_exchange", in_specs=[_VMEM], out_specs=_VMEM,
        out_shape=jax.ShapeDtypeStruct((N_SHARD,) + mv.shape[1:], F32),
        scratch_shapes=[pltpu.SemaphoreType.DMA((3,)), pltpu.SemaphoreType.DMA((3,))],
        compiler_params=pltpu.CompilerParams(vmem_limit_bytes=VMEM_LIMIT_V7X))(mv)


def _mod_grad_exchange(dmj, dm_rows):
    def body(dmj_ref, rows_ref, out_ref, bias_ref, ssem, rsem):
        me, chip = _to_all(lambda j: dmj_ref.at[j], out_ref, ssem, rsem)
        out_ref[me] = dmj_ref[chip]
        for l in range(2):
            bias_ref[l] = _colsum(rows_ref[l])

    return pl.pallas_call(
        body, name="mod_grad_exchange", in_specs=[_VMEM, _VMEM], out_specs=[_VMEM, _VMEM],
        out_shape=[jax.ShapeDtypeStruct((8,) + dmj.shape[1:], F32), jax.ShapeDtypeStruct((2, 1, dm_rows.shape[-1]), F32)],
        scratch_shapes=[pltpu.SemaphoreType.DMA((7,)), pltpu.SemaphoreType.DMA((7,))],
        compiler_params=pltpu.CompilerParams(vmem_limit_bytes=VMEM_LIMIT_V7X))(dmj, dm_rows)


def _pack(arrays):
    flat = jnp.concatenate([a.reshape(-1).astype(F32) for a in arrays])
    total = flat.shape[0]
    rows = -(-total // (8 * PACK_LANES)) * 8
    return jnp.pad(flat, (0, rows * PACK_LANES - total)).reshape(rows, PACK_LANES)


def _unpack(pack, shapes):
    flat, out, pos = pack.reshape(-1), [], 0
    for sh in shapes:
        size = int(np.prod(sh)) if len(sh) else 1
        out.append(flat[pos:pos + size].reshape(sh))
        pos += size
    return out


def _block_diag(pw):
    out = jnp.zeros((POOL_W, POOL_W), pw.dtype)
    for g in range(4):
        out = out.at[g * 64:(g + 1) * 64, g * 64:(g + 1) * 64].set(pw[g])
    return out


def _local_step(x, ctx, small, mod3s, loss_target, comm):
    cfg = _Cfg(x.shape[0], x.shape[1], ctx.shape[1])
    assert cfg.seq % TM == 0 and cfg.nc % TM == 0 and cfg.seq % cfg.ctx == 0 and cfg.ctx % CHUNK == 0
    nb_all, nb_lat = cfg.r // TM, cfg.nl // TM
    last = 1
    wf, big = comm.wf, comm.grads
    cos_t, sin_t = _rope_tables(cfg.seq)
    xs = (x.reshape(cfg.nl, D), ctx.reshape(cfg.nc, D))
    row = lambda w: pl.BlockSpec((TM, w), lambda i, j, k: (i, 0))
    mod3_spec = pl.BlockSpec((None, 6, D), lambda i, j, k: (cfg.mod_row(i), 0, 0))
    whole = lambda rows: pl.BlockSpec((None, rows, D), lambda i, j, k: (0, 0, 0))

    def conv_params(l):
        dw = jnp.pad(wf[l]["conv_dw"], ((0, 1), (0, 0)))
        return (dw, small["conv_dw_b"][l][None], small["conv_ln_g"][l][None], small["conv_ln_b"][l][None],
                _block_diag(small["pool_w"][l]).astype(BF16), small["pool_scale"][l][None])

    def residual_epi(ig):
        def epi(acc, ex, outs):
            x_ref, m_ref = ex
            outs[0][...] = x_ref[...] + m_ref[ig:ig + 1, :] * acc
            outs[1][...] = acc.astype(BF16)
        return epi

    def mixer_epi(acc, ex, outs):
        x_ref, m_ref, g_ref = ex
        x1 = x_ref[...] + m_ref[2:3, :] * acc
        outs[0][...] = x1
        outs[1][...] = acc.astype(BF16)
        outs[2][...] = _norm_mod(x1, g_ref[...], m_ref, 3, 4)

    def const(shape):
        return pl.BlockSpec(shape, lambda i, j, k: (0,) * len(shape))

    def loss_epi(acc, ex, outs):
        x_ref, m_ref, t_ref, g_ref = ex
        dx_ref, loss_ref, dfg_ref, dy_ref, dgate_ref = outs
        i = pl.program_id(0)
        gate = m_ref[5:6, :]
        xv = x_ref[...] + gate * acc
        g = g_ref[...]
        r = lax.rsqrt(jnp.mean(xv * xv, axis=-1, keepdims=True) + EPS)
        xh = xv * r
        err = xh * g - t_ref[...]
        _accumulate_rows(i == 0, loss_ref, (0.5 / D) * _colsum(jnp.sum(err * err, axis=-1, keepdims=True)))
        dy = err * (1.0 / D)
        _accumulate_rows(i == 0, dfg_ref, _colsum(dy * xh))
        dxh = dy * g
        dxv = r * (dxh - xh * jnp.mean(dxh * xh, axis=-1, keepdims=True))
        dx_ref[...] = dxv
        dy_ref[...] = (dxv * gate).astype(BF16)
        _accumulate_rows(cfg.first_of_row(i), dgate_ref, _colsum(dxv * acc))

    def hosted(name, call):
        outs, got = call(comm.ride(name))
        comm.landed(name, got)
        return outs

    saved = []
    for l in range(2):
        nb = nb_lat if l == last else nb_all
        wl = wf[l]
        mod3 = mod3s[l]
        res = _in_proj(f"in_proj{l}", xs, small["norm1_g"][l][None], mod3, wl["w_in"], 0, cos_t, sin_t, cfg)
        if isinstance(xs, tuple):
            xs = res[0]
        h1, qkv, cpin = res[-3:]
        mix, lse = hosted(f"att_fwd{l}", lambda ride: _att_fwd(f"att_fwd{l}", qkv, small["attn_sink"][l], l != last, cfg, ride))
        prm = conv_params(l)
        mix, yconv = _convpool_fwd(f"convpool_fwd_lat{l}", cpin, mix, None, prm, cfg.seq, 0, cfg)
        if l != last:
            mix, yconv = _convpool_fwd(f"convpool_fwd_ctx{l}", cpin, mix, yconv, prm, cfg.ctx, cfg.nl // cfg.ctx, cfg)
        x1, y1, h2 = _mm(f"out_proj{l}", "nn", (nb, 1, 1), mix, wl["w_out"].reshape(1, D, D), row(D), whole(D),
                         [jax.ShapeDtypeStruct((cfg.r, D), F32)] + [jax.ShapeDtypeStruct((cfg.r, D), BF16)] * 2, [row(D)] * 3,
                         extras=[xs, mod3, small["norm2_g"][l][None]],
                         extra_specs=[row(D), mod3_spec, pl.BlockSpec((1, D), lambda i, j, k: (0, 0))], epi=mixer_epi)
        gu, act = hosted(f"ffn_in{l}", lambda ride: _ffn_in(f"ffn_in{l}", h2, wl["w_ffn_in"], 0, nb, cfg, ride))
        if l != last:
            xs_next, y2 = _mm(f"ffn_out{l}", "nn", (nb, 1, 1), act, wl["w_ffn_out"].reshape(1, D_FF, D), row(D_FF),
                              whole(D_FF), [jax.ShapeDtypeStruct((cfg.r, D), F32), jax.ShapeDtypeStruct((cfg.r, D), BF16)],
                              [row(D), row(D)], extras=[x1, mod3], extra_specs=[row(D), mod3_spec], epi=residual_epi(5))
        else:
            xs_next, y2 = None, None
            dx, loss, d_final_g, dy2, dg2 = _mm(
                f"ffn_out{l}", "nn", (nb, 1, 1), act, wl["w_ffn_out"].reshape(1, D_FF, D), row(D_FF), whole(D_FF),
                [jax.ShapeDtypeStruct((cfg.r, D), F32), jax.ShapeDtypeStruct((1, 1), F32), jax.ShapeDtypeStruct((1, D), F32),
                 jax.ShapeDtypeStruct((cfg.r, D), BF16), jax.ShapeDtypeStruct((MOD_ROWS, 1, D), F32)],
                [row(D), const((1, 1)), const((1, D)), row(D),
                 pl.BlockSpec((None, 1, D), lambda i, j, k: (cfg.mod_row(i), 0, 0))],
                extras=[x1, mod3, loss_target.reshape(cfg.nl, D), small["final_g"][None]],
                extra_specs=[row(D), mod3_spec, row(D), const((1, D))], epi=loss_epi)
        saved.append(dict(mod3=mod3, x0=xs, h1=h1, qkv=qkv, cpin=cpin, mix=mix, yconv=yconv, lse=lse, y1=y1, x1=x1,
                          h2=h2, gu=gu, act=act, y2=y2, prm=prm))
        xs = xs_next

    sg = {k: [None, None] for k in ("norm1_g", "norm2_g", "conv_dw", "conv_dw_b", "conv_ln_g", "conv_ln_b",
                                    "attn_sink", "pool_w", "pool_scale")}
    dms = [None, None]

    def swiglu_bwd_epi(acc, ex, outs):
        outs[0][0] = (acc * ex[0][0].astype(F32)).astype(BF16)
        outs[0][1] = (acc * ex[0][1].astype(F32)).astype(BF16)

    def halves_epi(acc, ex, outs):
        h = acc.shape[0] // 2
        outs[0][0] = acc[:h].astype(BF16)
        outs[0][1] = acc[h:].astype(BF16)

    def row_shards_epi(n):
        def epi(acc, ex, outs):
            s0 = acc.shape[0] // n
            h = s0 // 2
            for t in range(n):
                for half in range(2):
                    outs[0][half, t] = acc[t * s0 + half * h:t * s0 + (half + 1) * h].astype(BF16)
        return epi

    def col_shards_epi(acc, ex, outs):
        h = acc.shape[0] // 2
        for j in range(N_SHARD):
            for half in range(2):
                outs[0][half, j] = acc[half * h:(half + 1) * h, j * IN_SHARD:(j + 1) * IN_SHARD].astype(BF16)

    for l in (1, 0):
        sv = saved[l]
        mod3 = sv["mod3"]
        nb = nb_lat if l == last else nb_all
        tr = _dw_rows(nb * TM)
        steps = nb * TM // tr
        wl = wf[l]
        gu_spec = pl.BlockSpec((2, TM, HALF_FF), lambda j, i, k: (0, i, j))
        df = _mm(f"ffn_out_bwd{l}", "nt", (2, nb, 1), dy2, wl["w_ffn_out"].reshape(1, D_FF, D),
                 pl.BlockSpec((TM, D), lambda j, i, k: (i, 0)), pl.BlockSpec((None, HALF_FF, D), lambda j, i, k: (0, j, 0)),
                 [jax.ShapeDtypeStruct((2, cfg.r, D_FF), BF16)], [gu_spec], extras=[sv["gu"]], extra_specs=[gu_spec],
                 epi=swiglu_bwd_epi)[0]
        big[l]["w_ffn_out"] = _mm_dw(
            f"dw_ffn_out{l}", sv["act"], dy2, pl.BlockSpec((tr, HALF_FF), lambda i, j, k: (k, j)),
            pl.BlockSpec((tr, D), lambda i, j, k: (k, 0)), (2, N_SHARD, D_FF // 8, D),
            pl.BlockSpec((2, 2, D_FF // 8, D), lambda i, j, k: (0, j, 0, 0)), 2, (HALF_FF, D), steps, row_shards_epi(2))[0]
        ride = comm.ride(f"dw_ffn_in{l}")
        res = _mm_dw(f"dw_ffn_in{l}", sv["h2"], df, pl.BlockSpec((tr, D), lambda i, j, k: (k, 0)),
                     pl.BlockSpec((None, tr, HALF_FF), lambda i, j, k: (j // 2, k, j % 2)), (2, N_SHARD, D // 2, HALF_FF),
                     pl.BlockSpec((2, None, D // 2, HALF_FF), lambda i, j, k: (0, j, 0, 0)), N_SHARD, (D, HALF_FF), steps,
                     halves_epi, ride=ride)
        res, got = res if ride is not None else (res, [])
        big[l]["w_ffn_in"] = res[0]
        comm.landed(f"dw_ffn_in{l}", got)
        dx1, dsh2, dsc2, dn2, dy1, dg1 = _proj_norm_bwd(
            f"ffn_in_bwd{l}", df, pl.BlockSpec((2, TM, D_FF), lambda i: (0, i, 0)),
            lambda a_ref, j: a_ref[j // 2, :, (j % 2) * HALF_FF:(j % 2 + 1) * HALF_FF], wl["w_ffn_in"],
            sv["x1"], dx, small["norm2_g"][l][None], mod3, 4, nb, False, (sv["y1"], mod3, 2), cfg)
        dmix = _mm(f"out_proj_bwd{l}", "nt", (nb, 1, 1), dy1, wl["w_out"].reshape(1, D, D), row(D), whole(D),
                   [jax.ShapeDtypeStruct((cfg.r, D), F32)], [row(D)])[0]
        big[l]["w_out"] = _mm_dw(
            f"dw_out{l}", sv["mix"], dy1, pl.BlockSpec((tr, D), lambda i, j, k: (k, 0)),
            pl.BlockSpec((tr, D), lambda i, j, k: (k, 0)), (2, N_SHARD, D // 8, D),
            pl.BlockSpec((2, N_SHARD, D // 8, D), lambda i, j, k: (0, 0, 0, 0)), 1, (D, D), steps, row_shards_epi(N_SHARD))[0]
        du, dkvl, dkvc, dsink = hosted(f"att_bwd{l}", lambda ride: _att_bwd(
            f"att_bwd{l}", sv["qkv"], sv["mix"], dmix, sv["lse"], small["attn_sink"][l], cos_t, sin_t, l != last, cfg, ride))
        acc = [du] + [jnp.zeros(sh, F32) for sh in _SMALL_SHAPES]
        acc = _convpool_bwd(f"convpool_bwd_lat{l}", sv["cpin"], sv["yconv"], dmix, sv["prm"], acc, cfg.seq, 0, cfg)
        if l != last:
            acc = _convpool_bwd(f"convpool_bwd_ctx{l}", sv["cpin"], sv["yconv"], dmix, sv["prm"], acc, cfg.ctx,
                                cfg.nl // cfg.ctx, cfg)
        du, g_dw, g_dwb, g_lng, g_lnb, g_wbd, g_ps = acc
        du = _place_kv(f"place_kv{l}", du, dkvl, dkvc, l != last, cfg)
        if l == last:
            du = _place_ctx_kv_only(f"place_ctx_kv{l}", du, dkvc, cfg)
        sg["attn_sink"][l] = dsink[:, 0]
        sg["conv_dw"][l], sg["conv_dw_b"][l], sg["conv_ln_g"][l], sg["conv_ln_b"][l] = g_dw[:CONV_K], g_dwb[0], g_lng[0], g_lnb[0]
        sg["pool_w"][l] = jnp.stack([g_wbd[g * 64:(g + 1) * 64, g * 64:(g + 1) * 64] for g in range(4)])
        sg["pool_scale"][l] = g_ps[0]
        tr_all = _dw_rows(cfg.r)
        big[l]["w_in"] = _mm_dw(
            f"dw_in{l}", sv["h1"], du, pl.BlockSpec((tr_all, D), lambda i, j, k: (k, 0)),
            pl.BlockSpec((tr_all, IN_W), lambda i, j, k: (k, 0)), (2, N_SHARD, D // 2, IN_SHARD),
            pl.BlockSpec((2, N_SHARD, D // 2, IN_SHARD), lambda i, j, k: (0, 0, 0, 0)), 1, (D, IN_W), cfg.r // tr_all,
            col_shards_epi)[0]
        below = (saved[l - 1]["y2"], saved[l - 1]["mod3"], 5) if l > 0 else None
        res = _proj_norm_bwd(
            f"in_proj_bwd{l}", du, pl.BlockSpec((TM, IN_W), lambda i: (i, 0)),
            lambda a_ref, j: a_ref[:, j * IN_SHARD:(j + 1) * IN_SHARD], wl["w_in"],
            sv["x0"], dx1, small["norm1_g"][l][None], mod3, 1, nb_all, l == last, below, cfg, dx_latent_only=l == 0)
        dx, dsh1, dsc1, dn1 = res[:4]
        sg["norm1_g"][l], sg["norm2_g"][l] = dn1[0], dn2[0]
        parts = [dsh1, dsc1, dg1, dsh2, dsc2, dg2]
        if below is not None:
            dy2, dg2 = res[4:]
        dm = jnp.concatenate([t[:cfg.b, 0, :] for t in parts], axis=1)
        live = (0, 1) if l == last else range(6)
        dm_ctx = jnp.concatenate([t[cfg.b, 0, :] if i in live else jnp.zeros((D,), F32) for i, t in enumerate(parts)])
        dms[l] = jnp.concatenate([dm, dm_ctx[None, :], jnp.zeros((MOD_ROWS - cfg.b - 1, 6 * D), F32)], axis=0)

    grad_x = dx.reshape(x.shape)
    small_grads = {k: jnp.stack(v) for k, v in sg.items()}
    small_grads["final_g"] = d_final_g[0]
    return loss, grad_x, small_grads, dms


_BIG = ("w_in", "w_out", "w_ffn_in", "w_ffn_out")
_TAPS = "conv_dw"
_SMALL = ("c_ctx", "b_mod", "norm1_g", "norm2_g", "conv_dw", "conv_dw_b", "conv_ln_g", "conv_ln_b", "attn_sink",
          "pool_w", "pool_scale", "final_g")
_ORDER = ("c_ctx", "w_mod", "b_mod", "norm1_g", "norm2_g", "w_in", "conv_dw", "conv_dw_b", "conv_ln_g", "conv_ln_b",
          "attn_sink", "pool_w", "pool_scale", "w_out", "w_ffn_in", "w_ffn_out", "final_g")
_GATHER_HOSTS = {
    "first": ((0, "w_in"),),
    "att_fwd0": ((0, "w_out"), (0, "w_ffn_in"), (0, "w_ffn_out"), (0, _TAPS)),
    "ffn_in0": tuple((1, k) for k in _BIG + (_TAPS,)),
}
_REDUCE_HOSTS = {
    "dw_ffn_in0": tuple((1, k) for k in _BIG),
    "att_bwd0": ((0, "w_ffn_in"), (0, "w_ffn_out"), (0, "w_out")),
    "last": ((0, "w_in"),),
}


class _Comm:
    def __init__(self, w, c_idx, jc_idx):
        self.shapes = {k: w[k].shape[1:] for k in _BIG}
        self.c_idx, self.jc_idx = c_idx, jc_idx
        halves = lambda a: a.reshape(2, a.shape[0] // 2, a.shape[1])
        taps = jnp.pad(w[_TAPS], ((0, 0), (0, 1), (0, 64)))
        cast = {k: w[k].astype(BF16) for k in _BIG}
        self.shards = [{**{k: halves(cast[k][l]) for k in _BIG}, _TAPS: halves(taps[l])} for l in range(2)]
        self.wf = [dict(), dict()]
        self.grads = [dict(), dict()]
        self.reduced = [dict(), dict()]
        self._open = {}
        self.landed("first", _run_ride("gather_first", self.ride("first")))

    def ride(self, host):
        if host in _GATHER_HOSTS:
            return _gather_ride([self.shards[layer][k] for layer, k in _GATHER_HOSTS[host]])
        if host in _REDUCE_HOSTS:
            what = _REDUCE_HOSTS[host]
            mine = [self.grads[layer][k] for layer, k in what]
            other = _send_other_half(f"send_other_half_{host}", mine)
            pair = [_pair_add(f"pair_add{layer}_{k}", a, b, self.c_idx) for (layer, k), a, b in zip(what, mine, other)]
            self._open[host] = pair
            return _exchange_ride(pair)
        return None

    def landed(self, host, got):
        if host in _GATHER_HOSTS:
            for (layer, k), f in zip(_GATHER_HOSTS[host], got):
                if k == _TAPS:
                    taps = f.reshape(N_SHARD, 32, 128)[:, :CONV_K, :64]
                    self.wf[layer][k] = jnp.transpose(taps, (1, 0, 2)).reshape(CONV_K, CONV_W)
                else:
                    self.wf[layer][k] = f.reshape((1, N_SHARD) + self.shapes[k])
        if host in _REDUCE_HOSTS:
            what = _REDUCE_HOSTS[host]
            mine = [_shard_sum(f"shard_sum{layer}_{k}", a, b, self.jc_idx)
                    for (layer, k), a, b in zip(what, self._open.pop(host), got)]
            for (layer, k), g in zip(what, _swap_reduced(f"swap_reduced_{host}", mine)):
                self.reduced[layer][k] = g.reshape(self.shapes[k])


def _conditioning(c, c_ctx, w_mod, b_mod, chip):
    b = c.shape[0]
    block = jnp.concatenate([c, c_ctx[None, :], jnp.zeros((8 - b - 1, D), F32)], axis=0)
    c_all = _share_small("share_c", block, False).reshape(64, D)
    bias = lax.dynamic_slice_in_dim(b_mod, chip * MOD_W, MOD_W, axis=1)
    full = lambda r, q: pl.BlockSpec((r, q), lambda i, j, k: (0, 0))

    def bias_epi(acc, ex, outs):
        outs[0][...] = acc + ex[0][...]

    mv = [_mm(f"mod_fwd{l}", "nn", (1, 1, 1), c_all, w_mod[l], full(64, D), full(D, MOD_W),
              [jax.ShapeDtypeStruct((64, MOD_W), F32)], [full(64, MOD_W)], extras=[bias[l][None]],
              extra_specs=[full(1, MOD_W)], a_fn=_silu, epi=bias_epi)[0] for l in range(2)]
    by_dev = jnp.transpose(jnp.stack(mv).reshape(2, 8, 8, MOD_W), (1, 0, 2, 3))
    rows = jnp.transpose(_mod_rows_exchange(by_dev), (1, 2, 0, 3)).reshape(2, 8, 6 * D)
    rows = jnp.pad(rows, ((0, 0), (0, MOD_ROWS - 8), (0, 0)))
    return [rows[l].reshape(MOD_ROWS, 6, D) for l in range(2)], c_all


def _conditioning_bwd(dms, c_all, w_mod, b):
    dm = jnp.stack([d[:8] for d in dms])
    by_chip = jnp.transpose(dm.reshape(2, 8, N_SHARD, MOD_W), (2, 0, 1, 3))
    gathered, d_bias = _mod_grad_exchange(by_chip, dm)
    dm_all = jnp.transpose(gathered, (1, 0, 2, 3)).reshape(2, 64, MOD_W)
    full = lambda r, q: pl.BlockSpec((r, q), lambda i, j, k: (0, 0))

    def ctx_rows_epi(acc, ex, outs):
        row = lax.broadcasted_iota(jnp.int32, acc.shape, 0) % 8
        outs[0][...] = _colsum(jnp.where(row == b, acc * _dsilu(ex[0][...]), 0.0))

    g_mod, d_ctx = [], jnp.zeros((D,), F32)
    for l in range(2):
        g_mod.append(_mm(f"dw_mod{l}", "tn", (1, 1, 1), c_all, dm_all[l], full(64, D), full(64, MOD_W),
                         [jax.ShapeDtypeStruct((D, MOD_W), F32)], [full(D, MOD_W)], a_fn=_silu)[0])
        part = _mm(f"mod_bwd{l}", "nt", (1, 1, 1), dm_all[l], w_mod[l], full(64, MOD_W), full(D, MOD_W),
                   [jax.ShapeDtypeStruct((1, D), F32)], [full(1, D)], extras=[c_all], extra_specs=[full(64, D)],
                   epi=ctx_rows_epi)[0]
        d_ctx = d_ctx + part[0]
    return g_mod, d_bias[:, 0, :], d_ctx


def kernel(x, c, ctx, c_ctx, w_mod, b_mod, norm1_g, norm2_g, w_in, conv_dw, conv_dw_b, conv_ln_g, conv_ln_b, attn_sink, pool_w, pool_scale, w_out, w_ffn_in, w_ffn_out, final_g, loss_target, m_c_ctx, m_w_mod, m_b_mod, m_norm1_g, m_norm2_g, m_w_in, m_conv_dw, m_conv_dw_b, m_conv_ln_g, m_conv_ln_b, m_attn_sink, m_pool_w, m_pool_scale, m_w_out, m_w_ffn_in, m_w_ffn_out, m_final_g, v_c_ctx, v_w_mod, v_b_mod, v_norm1_g, v_norm2_g, v_w_in, v_conv_dw, v_conv_dw_b, v_conv_ln_g, v_conv_ln_b, v_attn_sink, v_pool_w, v_pool_scale, v_w_out, v_w_ffn_in, v_w_ffn_out, v_final_g):
    w = dict(c_ctx=c_ctx, w_mod=w_mod, b_mod=b_mod, norm1_g=norm1_g, norm2_g=norm2_g, w_in=w_in, conv_dw=conv_dw,
             conv_dw_b=conv_dw_b, conv_ln_g=conv_ln_g, conv_ln_b=conv_ln_b, attn_sink=attn_sink, pool_w=pool_w,
             pool_scale=pool_scale, w_out=w_out, w_ffn_in=w_ffn_in, w_ffn_out=w_ffn_out, final_g=final_g)
    m = dict(c_ctx=m_c_ctx, w_mod=m_w_mod, b_mod=m_b_mod, norm1_g=m_norm1_g, norm2_g=m_norm2_g, w_in=m_w_in,
             conv_dw=m_conv_dw, conv_dw_b=m_conv_dw_b, conv_ln_g=m_conv_ln_g, conv_ln_b=m_conv_ln_b,
             attn_sink=m_attn_sink, pool_w=m_pool_w, pool_scale=m_pool_scale, w_out=m_w_out, w_ffn_in=m_w_ffn_in,
             w_ffn_out=m_w_ffn_out, final_g=m_final_g)
    v = dict(c_ctx=v_c_ctx, w_mod=v_w_mod, b_mod=v_b_mod, norm1_g=v_norm1_g, norm2_g=v_norm2_g, w_in=v_w_in,
             conv_dw=v_conv_dw, conv_dw_b=v_conv_dw_b, conv_ln_g=v_conv_ln_g, conv_ln_b=v_conv_ln_b,
             attn_sink=v_attn_sink, pool_w=v_pool_w, pool_scale=v_pool_scale, w_out=v_w_out, w_ffn_in=v_w_ffn_in,
             w_ffn_out=v_w_ffn_out, final_g=v_final_g)
    xi, yi, ci = _position()
    chip = 2 * xi + yi
    mod3s, c_all = _conditioning(c, c_ctx, w_mod, b_mod, chip)
    comm = _Comm(w, jnp.reshape(ci, (1,)).astype(jnp.int32), jnp.stack([chip, ci]).astype(jnp.int32))
    small = {k: w[k] for k in _SMALL if k not in ("conv_dw", "c_ctx", "b_mod")}
    loss, grad_x, sgrads, dms = _local_step(x, ctx, small, mod3s, loss_target, comm)
    comm.landed("last", _run_ride("exchange_last", comm.ride("last")))
    g_mod, sgrads["b_mod"], d_ctx = _conditioning_bwd(dms, c_all, w_mod, c.shape[0])
    sgrads["c_ctx"] = 0.5 * d_ctx

    names = list(_SMALL)
    total = _share_small("sum_small", _pack([loss] + [sgrads[k] for k in names]), True)
    parts = _unpack(total, [()] + [sgrads[k].shape for k in names])
    loss_out = parts[0]
    gsmall = dict(zip(names, parts[1:]))
    gsmall["conv_dw"] = lax.dynamic_slice_in_dim(gsmall["conv_dw"], chip * 64, 64, axis=2)

    grads, delta, new_m, new_v = dict(gsmall), {}, {}, {}
    for k in ("w_mod",) + _BIG:
        s0, s1 = w[k].shape[1:]
        flat = lambda a: a.reshape(2 * s0, s1)
        g_layers = g_mod if k == "w_mod" else [comm.reduced[l][k] for l in range(2)]
        outs = _adamw_layers(f"adamw_{k}", flat(w[k]), g_layers, flat(m[k]), flat(v[k]))
        grads[k], delta[k], new_m[k], new_v[k] = [a.reshape(w[k].shape) for a in outs]
    d_, m_, v_ = _adamw("adamw_small", _pack([w[k] for k in names]), _pack([gsmall[k] for k in names]),
                        _pack([m[k] for k in names]), _pack([v[k] for k in names]))
    sshapes = [w[k].shape for k in names]
    for k, a, b, e in zip(names, _unpack(d_, sshapes), _unpack(m_, sshapes), _unpack(v_, sshapes)):
        delta[k], new_m[k], new_v[k] = a, b, e
    return (loss_out, grad_x, *[grads[k] for k in _ORDER], *[delta[k] for k in _ORDER],
            *[new_m[k] for k in _ORDER], *[new_v[k] for k in _ORDER])
```

```python
from typing import NamedTuple

import jax
import jax.numpy as jnp
import numpy as np
from jax import lax
from jax.experimental import pallas as pl
from jax.experimental.pallas import tpu as pltpu

F32 = jnp.float32
BF16 = jnp.bfloat16

D = 1024
GRID_W = 64
HEAD_DIM = 64
N_HEADS = 8
ATTN_W = 512
CONV_W = 256
POOL_W = 256
IN_W = 1536
D_FF = 2816
CONV_K = 31
QB = 128
ROPE_BASE = 10000.0
EPS = 1e-6
NEG = -1e30
N_SHARD = 4
IN_SHARD = IN_W // N_SHARD
HALF_FF = D_FF // 2
MOD_W = 6 * D // N_SHARD
MOD_ROWS = 16
PACK_LANES = 128

ADAM_LR = 0.001
ADAM_B1 = 0.9
ADAM_B2 = 0.999
ADAM_EPS = 1e-08
ADAM_WD = 0.01
ADAM_STEP = 10

VMEM_LIMIT_V7X = 56 * 1024 * 1024
TM = 512
TR_MAX = 1024
CHUNK = 256
PAD = 16

_MESH = pl.DeviceIdType.MESH
_ANY = pl.BlockSpec(memory_space=pl.ANY)
_DIMS = {"nn": (((1,), (0,)), ((), ())), "nt": (((1,), (1,)), ((), ())), "tn": (((0,), (0,)), ((), ()))}


class _Cfg(NamedTuple):
    b: int
    seq: int
    ctx: int

    @property
    def nl(self):
        return self.b * self.seq

    @property
    def nc(self):
        return self.b * self.ctx

    @property
    def r(self):
        return self.nl + self.nc

    def mod_row(self, i):
        return jnp.where(i < self.nl // TM, i // (self.seq // TM), self.b)

    def first_of_row(self, i):
        nlb = self.nl // TM
        return jnp.logical_or(jnp.logical_and(i < nlb, i % (self.seq // TM) == 0), i == nlb)


def _params(n_grid=0):
    sem = ("arbitrary",) * n_grid if n_grid else None
    return pltpu.CompilerParams(dimension_semantics=sem, vmem_limit_bytes=VMEM_LIMIT_V7X)


def _dot(a, b, mode="nn"):
    return lax.dot_general(a.astype(BF16), b.astype(BF16), _DIMS[mode], preferred_element_type=F32)


def _sigmoid(x):
    return 1.0 / (1.0 + jnp.exp(-x))


def _silu(x):
    return x * _sigmoid(x)


def _dsilu(x):
    s = _sigmoid(x)
    return s * (1.0 + x * (1.0 - s))


def _colsum(v):
    return jnp.sum(v, axis=0, keepdims=True)


def _dw_rows(rows):
    return TR_MAX if rows % TR_MAX == 0 else TM


def _epi_store(acc, ex, outs):
    for o in outs:
        o[...] = acc.astype(o.dtype)


class _Ride(NamedTuple):
    ins: list
    out_shape: list
    scratch: list
    start: object
    finish: object


class _Hosted(NamedTuple):
    ride: _Ride
    n_in: int
    n_out: int
    grid: tuple

    def split(self, refs):
        n_ri, n_ro, n_rs = len(self.ride.ins), len(self.ride.out_shape), len(self.ride.scratch)
        r_in = refs[self.n_in:self.n_in + n_ri]
        r_out = refs[self.n_in + n_ri + self.n_out:self.n_in + n_ri + self.n_out + n_ro]
        own = refs[:self.n_in] + refs[self.n_in + n_ri:self.n_in + n_ri + self.n_out] + \
            refs[self.n_in + n_ri + self.n_out + n_ro:len(refs) - n_rs]
        return own, (r_in, r_out, refs[len(refs) - n_rs:])

    def start(self, parts):
        ids = [pl.program_id(d) for d in range(len(self.grid))]
        first = ids[0] == 0
        for i in ids[1:]:
            first = jnp.logical_and(first, i == 0)
        pl.when(first)(lambda: self.ride.start(*parts))

    def finish(self, parts):
        ids = [pl.program_id(d) for d in range(len(self.grid))]
        last = ids[0] == self.grid[0] - 1
        for i, g in zip(ids[1:], self.grid[1:]):
            last = jnp.logical_and(last, i == g - 1)
        pl.when(last)(lambda: self.ride.finish(*parts))


def _hosted_call(body, ride, name, grid, ins, in_specs, out_shape, out_specs, scratch, params):
    if ride is None:
        res = pl.pallas_call(body, name=name, grid=grid, in_specs=in_specs, out_specs=out_specs, out_shape=out_shape,
                             scratch_shapes=scratch, compiler_params=params)(*ins)
        return list(res), []
    host = _Hosted(ride, len(ins), len(out_shape), tuple(grid))

    def hosted(*refs):
        own, parts = host.split(refs)
        host.start(parts)
        body(*own)
        host.finish(parts)

    res = pl.pallas_call(
        hosted, name=name, grid=grid, in_specs=list(in_specs) + [_ANY] * len(ride.ins),
        out_specs=list(out_specs) + [_ANY] * len(ride.out_shape), out_shape=list(out_shape) + list(ride.out_shape),
        scratch_shapes=list(scratch) + list(ride.scratch), compiler_params=params)(*ins, *ride.ins)
    return list(res[:len(out_shape)]), list(res[len(out_shape):])


def _mm(name, mode, grid, a, b, a_spec, b_spec, out_shape, out_specs, acc_shape=None, extras=(),
        extra_specs=(), a_fn=None, epi=_epi_store, ride=None):
    nk = grid[2]
    n_ex, n_out = len(extras), len(out_shape)

    def body(*refs):
        a_ref, b_ref = refs[:2]
        ex = refs[2:2 + n_ex]
        outs = refs[2 + n_ex:2 + n_ex + n_out]
        av = a_ref[...]
        if a_fn is not None:
            av = a_fn(av)
        part = _dot(av, b_ref[...], mode)
        if nk == 1:
            epi(part, ex, outs)
        else:
            acc = refs[-1]
            k = pl.program_id(2)

            @pl.when(k == 0)
            def _():
                acc[...] = part

            @pl.when(k > 0)
            def _():
                acc[...] += part

            @pl.when(k == nk - 1)
            def _():
                epi(acc[...], ex, outs)

    scratch = [] if nk == 1 else [pltpu.VMEM(acc_shape, F32)]
    outs, ride_outs = _hosted_call(body, ride, name, grid, [a, b, *extras], [a_spec, b_spec, *extra_specs],
                                   list(out_shape), list(out_specs), scratch, _params(3))
    return outs if ride is None else (outs, ride_outs)


def _mm_dw(name, a, b, a_spec, b_spec, out_shape, out_spec, n_out_blocks, acc_shape, n_steps, epi, a_fn=None,
           extras=(), extra_specs=(), extra_out=(), extra_out_specs=(), ride=None):
    grid = (1, n_out_blocks, n_steps)
    outs = [jax.ShapeDtypeStruct(out_shape, BF16)] + list(extra_out)
    return _mm(name, "tn", grid, a, b, a_spec, b_spec, outs, [out_spec, *extra_out_specs], acc_shape, extras,
               extra_specs, a_fn, epi, ride)


def _gate_step(i, dxv, y_ref, m_ref, ig, dy_ref, dgate_ref, cfg):
    dy_ref[...] = (dxv * m_ref[ig:ig + 1, :]).astype(BF16)
    _accumulate_rows(cfg.first_of_row(i), dgate_ref, _colsum(dxv * y_ref[...].astype(F32)))


def _proj_norm_bwd(name, a, a_spec, pick, w, x, dres, gvec, mod3, isc, nblk, res_latent_only, gate, cfg,
                   dx_latent_only=False):
    ns = w.shape[-1]
    nlb = cfg.nl // TM

    def body(a_ref, w_ref, x_ref, dres_ref, g_ref, m_ref, *rest):
        if gate is None:
            dx_ref, dsh_ref, dsc_ref, dg_ref = rest
        else:
            y_ref, gm_ref, dx_ref, dsh_ref, dsc_ref, dg_ref, dy_ref, dgate_ref = rest
        i = pl.program_id(0)
        dhv = _dot(pick(a_ref, 0), w_ref[0], "nt")
        for j in range(1, N_SHARD):
            dhv = dhv + _dot(pick(a_ref, j), w_ref[j], "nt")
        xv = x_ref[...]
        r = lax.rsqrt(jnp.mean(xv * xv, axis=-1, keepdims=True) + EPS)
        xh = xv * r
        g = g_ref[...]
        sc1 = 1.0 + m_ref[isc:isc + 1, :]
        t = dhv * xh
        first = cfg.first_of_row(i)
        _accumulate_rows(first, dsh_ref, _colsum(dhv))
        _accumulate_rows(first, dsc_ref, _colsum(t * g))
        _accumulate_rows(i == 0, dg_ref, _colsum(t * sc1))
        dxh = dhv * (g * sc1)
        dxn = r * (dxh - xh * jnp.mean(dxh * xh, axis=-1, keepdims=True))
        dxv = (jnp.where(i < nlb, dres_ref[...], 0.0) if res_latent_only else dres_ref[...]) + dxn
        if dx_latent_only:
            @pl.when(i < nlb)
            def _():
                dx_ref[...] = dxv
        else:
            dx_ref[...] = dxv
        if gate is not None:
            _gate_step(i, dxv, y_ref, gm_ref, gate[2], dy_ref, dgate_ref, cfg)

    row = pl.BlockSpec((TM, D), lambda i: (i, 0))
    vec = pl.BlockSpec((1, D), lambda i: (0, 0))
    part = pl.BlockSpec((None, 1, D), lambda i: (cfg.mod_row(i), 0, 0))
    part_shape = jax.ShapeDtypeStruct((MOD_ROWS, 1, D), F32)
    resident = pl.BlockSpec((None, N_SHARD, D, ns), lambda i: (0, 0, 0, 0), pipeline_mode=pl.Buffered(1))
    ins, in_specs = [a, w, x, dres, gvec, mod3], [a_spec, resident, row, row, vec, _mod_spec(cfg)]
    out_specs = [row, part, part, vec]
    out_shape = [jax.ShapeDtypeStruct((cfg.r, D), F32), part_shape, part_shape, jax.ShapeDtypeStruct((1, D), F32)]
    if dx_latent_only:
        out_specs[0] = pl.BlockSpec((TM, D), lambda i: (jnp.minimum(i, nlb - 1), 0))
        out_shape[0] = jax.ShapeDtypeStruct((cfg.nl, D), F32)
    if gate is not None:
        ins, in_specs = ins + list(gate[:2]), in_specs + [row, _mod_spec(cfg)]
        out_specs, out_shape = out_specs + [row, part], out_shape + [jax.ShapeDtypeStruct((cfg.r, D), BF16), part_shape]
    return pl.pallas_call(body, name=name, grid=(nblk,), in_specs=in_specs, out_specs=out_specs, out_shape=out_shape,
                          compiler_params=_params(1))(*ins)


def _mod_spec(cfg):
    return pl.BlockSpec((None, 6, D), lambda i: (cfg.mod_row(i), 0, 0))


def _norm_mod(xv, g, m_ref, ish, isc):
    r = lax.rsqrt(jnp.mean(xv * xv, axis=-1, keepdims=True) + EPS)
    return (xv * r * g * (1.0 + m_ref[isc:isc + 1, :]) + m_ref[ish:ish + 1, :]).astype(BF16)


def _accumulate_rows(first, ref, val):
    @pl.when(first)
    def _():
        ref[...] = val

    @pl.when(jnp.logical_not(first))
    def _():
        ref[...] += val


def _rope_tables(seq):
    rows = seq // GRID_W
    row = jnp.repeat(jnp.arange(rows), GRID_W).astype(F32)
    col = jnp.tile(jnp.arange(GRID_W), rows).astype(F32)
    half = HEAD_DIM // 2
    inv = ROPE_BASE ** (-jnp.arange(0, half, 2, dtype=F32) / half)
    ar, ac = row[:, None] * inv, col[:, None] * inv
    ang = jnp.concatenate([ar, ar, ac, ac], axis=-1)
    sign = jnp.tile(jnp.concatenate([-jnp.ones((16,), F32), jnp.ones((16,), F32)]), 2)
    cos = jnp.tile(jnp.cos(ang), (1, 2))
    sin = jnp.tile(jnp.sin(ang) * sign, (1, 2))
    cos = jnp.concatenate([cos, jnp.ones((TM, 2 * HEAD_DIM), F32)], axis=0)
    sin = jnp.concatenate([sin, jnp.zeros((TM, 2 * HEAD_DIM), F32)], axis=0)
    return cos, sin


def _rope(x, cos, sin_signed, sign):
    lane = lax.broadcasted_iota(jnp.int32, x.shape, 1)
    low = (lane % 32) < 16
    rot = jnp.where(low, pltpu.roll(x, 112, 1), pltpu.roll(x, 16, 1))
    return x * cos + sign * (rot * sin_signed)


def _in_proj(name, x, gvec, mod3, w_in, layer, cos_t, sin_t, cfg):
    nlb, bps = cfg.nl // TM, cfg.seq // TM
    joined = isinstance(x, tuple)

    def body(*refs):
        if joined:
            lat_ref, ctx_ref, g_ref, m_ref, w_ref, cos_ref, sin_ref, xs_ref, h_ref, qkv_ref, cp_ref = refs
            xv = jnp.where(pl.program_id(0) < nlb, lat_ref[...], ctx_ref[...])
            xs_ref[...] = xv
        else:
            x_ref, g_ref, m_ref, w_ref, cos_ref, sin_ref, h_ref, qkv_ref, cp_ref = refs
            xv = x_ref[...]
        hv = _norm_mod(xv, g_ref[...], m_ref, 0, 1)
        h_ref[...] = hv
        u = jnp.concatenate([_dot(hv, w_ref[j]) for j in range(N_SHARD)], axis=1)
        cos, sin = cos_ref[...], sin_ref[...]
        tiles = []
        for t in range(5):
            y = _rope(u[:, 128 * t:128 * (t + 1)], cos, sin, 1.0)
            tiles.append(y * (HEAD_DIM ** -0.5) if t < 4 else y)
        tiles.append(u[:, 640:768])
        qkv_ref[...] = jnp.concatenate(tiles, axis=1).astype(BF16)
        cp_ref[...] = u[:, 768:IN_W].astype(BF16)

    tab = pl.BlockSpec((TM, 128), lambda i: (jnp.where(i < nlb, i % bps, bps), 0))
    half = pl.BlockSpec((TM, 768), lambda i: (i, 0))
    row = pl.BlockSpec((TM, D), lambda i: (i, 0))
    x_in, x_specs = [x], [row]
    out_specs = [row, half, half]
    out_shape = [jax.ShapeDtypeStruct((cfg.r, D), BF16), jax.ShapeDtypeStruct((cfg.r, 768), BF16),
                 jax.ShapeDtypeStruct((cfg.r, 768), BF16)]
    if joined:
        x_in = list(x)
        x_specs = [pl.BlockSpec((TM, D), lambda i: (jnp.minimum(i, nlb - 1), 0)),
                   pl.BlockSpec((TM, D), lambda i: (jnp.maximum(i - nlb, 0), 0))]
        out_specs, out_shape = [row] + out_specs, [jax.ShapeDtypeStruct((cfg.r, D), F32)] + out_shape
    return pl.pallas_call(
        body, name=name, grid=(cfg.r // TM,),
        in_specs=x_specs + [pl.BlockSpec((1, D), lambda i: (0, 0)), _mod_spec(cfg),
                            pl.BlockSpec((None, N_SHARD, D, IN_SHARD), lambda i: (layer, 0, 0, 0)), tab, tab],
        out_specs=out_specs, out_shape=out_shape, compiler_params=_params(1))(*x_in, gvec, mod3, w_in, cos_t, sin_t)


def _att_specs(cfg):
    nlb, ncb = cfg.seq // QB, cfg.ctx // QB

    def qblk(s, qb):
        return jnp.where(qb < nlb, s * nlb + qb, cfg.nl // QB + s * ncb + qb - nlb)

    def near(off, col):
        return pl.BlockSpec((QB, 128), lambda s, qb: (s * nlb + jnp.clip(qb + off, 0, nlb - 1), col))

    def ctxs(col):
        return pl.BlockSpec((cfg.ctx, 128), lambda s, qb: (cfg.nl // cfg.ctx + s, col))

    qspec = pl.BlockSpec((QB, ATTN_W), lambda s, qb: (qblk(s, qb), 0))
    kv = [ctxs(4), ctxs(5), near(-1, 4), near(0, 4), near(1, 4), near(-1, 5), near(0, 5), near(1, 5)]
    return qblk, qspec, kv


def _att_scores(qb, nlb, sink_ref, q_ref, k_refs, v_refs, kh):
    is_lat = qb < nlb
    ii = lax.broadcasted_iota(jnp.int32, (4 * QB, 3 * QB), 0) % QB
    col = lax.broadcasted_iota(jnp.int32, (4 * QB, 3 * QB), 1)
    jj, blk = col % QB, col // QB
    off_p = jnp.where(jnp.logical_and(is_lat, qb >= 1), 0.0, NEG)
    off_c = jnp.where(is_lat, 0.0, NEG)
    off_n = jnp.where(jnp.logical_and(is_lat, qb <= nlb - 2), 0.0, NEG)
    inside = jnp.logical_or(blk == 1, jnp.logical_or(jnp.logical_and(blk == 0, jj >= ii),
                                                     jnp.logical_and(blk == 2, jj <= ii)))
    off = jnp.where(blk == 0, off_p, jnp.where(blk == 1, off_c, off_n))
    q4 = jnp.concatenate([q_ref[:, (4 * kh + g) * HEAD_DIM:(4 * kh + g + 1) * HEAD_DIM] for g in range(4)], axis=0)
    rg = lax.broadcasted_iota(jnp.int32, (4 * QB, 1), 0) // QB
    snk = jnp.where(rg == 0, sink_ref[4 * kh],
                    jnp.where(rg == 1, sink_ref[4 * kh + 1], jnp.where(rg == 2, sink_ref[4 * kh + 2], sink_ref[4 * kh + 3])))
    lanes = slice(kh * HEAD_DIM, (kh + 1) * HEAD_DIM)
    kx, vx = k_refs[0][:, lanes], v_refs[0][:, lanes]
    kl = jnp.concatenate([r[:, lanes] for r in k_refs[1:]], axis=0)
    vl = jnp.concatenate([r[:, lanes] for r in v_refs[1:]], axis=0)
    sx = _dot(q4, kx, "nt")
    sl = jnp.where(inside, _dot(q4, kl, "nt"), NEG) + off
    return q4, snk, (kx, kl), (vx, vl), (sx, sl)


def _att_fwd(name, qkv, sink, ctx_queries, cfg, ride=None):
    nlb, ncb = cfg.seq // QB, cfg.ctx // QB
    qblk, qspec, kvspecs = _att_specs(cfg)

    def body(sink_ref, q_ref, kx_ref, vx_ref, kp_ref, kc_ref, kn_ref, vp_ref, vc_ref, vn_ref, o_ref, lse_ref):
        qb = pl.program_id(1)
        for kh in range(2):
            q4, snk, _, vs, ss = _att_scores(qb, nlb, sink_ref, q_ref, (kx_ref, kp_ref, kc_ref, kn_ref),
                                             (vx_ref, vp_ref, vc_ref, vn_ref), kh)
            m = snk
            for s_ in ss:
                m = jnp.maximum(m, jnp.max(s_, axis=-1, keepdims=True))
            den = jnp.exp(snk - m)
            o4 = jnp.zeros((4 * QB, HEAD_DIM), F32)
            for s_, v_ in zip(ss, vs):
                p = jnp.exp(s_ - m)
                den = den + jnp.sum(p, axis=-1, keepdims=True)
                o4 = o4 + _dot(p, v_)
            o4 = o4 / den
            lse = m + jnp.log(den)
            for g in range(4):
                h = 4 * kh + g
                o_ref[:, h * HEAD_DIM:(h + 1) * HEAD_DIM] = o4[g * QB:(g + 1) * QB].astype(BF16)
                lse_ref[:, h:h + 1] = lse[g * QB:(g + 1) * QB]

    return _hosted_call(
        body, ride, name, (cfg.b, nlb + (ncb if ctx_queries else 0)), [sink] + [qkv] * 9,
        [pl.BlockSpec(memory_space=pltpu.SMEM), qspec, *kvspecs],
        [jax.ShapeDtypeStruct((cfg.r, D), BF16), jax.ShapeDtypeStruct((cfg.r, N_HEADS), F32)],
        [pl.BlockSpec((QB, ATTN_W), lambda s, qb: (qblk(s, qb), 0)),
         pl.BlockSpec((QB, N_HEADS), lambda s, qb: (qblk(s, qb), 0))], [], _params(2))


def _att_bwd(name, qkv, mix, dmix, lse, sink, cos_t, sin_t, ctx_queries, cfg, ride=None):
    nlb, ncb = cfg.seq // QB, cfg.ctx // QB
    nqb = nlb + (ncb if ctx_queries else 0)
    qblk, qspec, kvspecs = _att_specs(cfg)

    def body(sink_ref, q_ref, kx_ref, vx_ref, kp_ref, kc_ref, kn_ref, vp_ref, vc_ref, vn_ref, o_ref, do_ref,
             lse_ref, cosq_ref, sinq_ref, cosk_ref, sink_tab_ref, dq_ref, dkvl_ref, dkvc_ref, dsink_ref,
             accl, accc, dqs):
        s_id, qb = pl.program_id(0), pl.program_id(1)

        @pl.when(qb == 0)
        def _():
            accl[...] = jnp.zeros_like(accl)
            accc[...] = jnp.zeros_like(accc)

        @pl.when(jnp.logical_and(s_id == 0, qb == 0))
        def _():
            dsink_ref[...] = jnp.zeros_like(dsink_ref)

        starts = [pl.multiple_of(jnp.clip(qb + off, 0, nlb - 1) * QB, QB) for off in (-1, 0, 1)]
        for kh in range(2):
            q4, snk, ks, vs, ss = _att_scores(qb, nlb, sink_ref, q_ref, (kx_ref, kp_ref, kc_ref, kn_ref),
                                              (vx_ref, vp_ref, vc_ref, vn_ref), kh)
            lanes = slice(kh * HEAD_DIM, (kh + 1) * HEAD_DIM)
            heads =[slice((4 * kh + g) * HEAD_DIM, (4 * kh + g + 1) * HEAD_DIM) for g in range(4)]
            do4 = jnp.concatenate([do_ref[:, hs] for hs in heads], axis=0)
            o4 = jnp.concatenate([o_ref[:, hs] for hs in heads], axis=0).astype(F32)
            lse4 = jnp.concatenate([lse_ref[:, 4 * kh + g:4 * kh + g + 1] for g in range(4)], axis=0)
            delta = jnp.sum(do4 * o4, axis=-1, keepdims=True)
            dq4 = jnp.zeros((4 * QB, HEAD_DIM), F32)
            dks, dvs = [], []
            for s_, k_, v_ in zip(ss, ks, vs):
                p = jnp.exp(s_ - lse4)
                ds = p * (_dot(do4, v_, "nt") - delta)
                dq4 = dq4 + _dot(ds, k_)
                dks.append(_dot(ds, q4, "tn"))
                dvs.append(_dot(p, do4, "tn"))
            accc[:, lanes] += dks[0]
            accc[:, 128 + kh * HEAD_DIM:128 + (kh + 1) * HEAD_DIM] += dvs[0]
            for t, st in enumerate(starts):
                accl[pl.ds(st, QB), lanes] += dks[1][t * QB:(t + 1) * QB]
                accl[pl.ds(st, QB), 128 + kh * HEAD_DIM:128 + (kh + 1) * HEAD_DIM] += dvs[1][t * QB:(t + 1) * QB]
            dsk = -jnp.exp(snk - lse4) * delta
            for g in range(4):
                h = 4 * kh + g
                dsink_ref[h:h + 1, :] += jnp.broadcast_to(_colsum(dsk[g * QB:(g + 1) * QB]), (1, 128))
                dqs[:, heads[g]] = dq4[g * QB:(g + 1) * QB]
        cos, sin = cosq_ref[...], sinq_ref[...]
        dq_ref[...] = jnp.concatenate(
            [_rope(dqs[:, 128 * t:128 * (t + 1)], cos, sin, -1.0) * (HEAD_DIM ** -0.5) for t in range(4)],
            axis=1).astype(BF16)

        @pl.when(qb == nqb - 1)
        def _():
            dk = _rope(accl[:, 0:128], cosk_ref[...], sink_tab_ref[...], -1.0)
            dkvl_ref[...] = jnp.concatenate([dk, accl[:, 128:256]], axis=1).astype(BF16)
            dkvc_ref[...] = accc[...].astype(BF16)

    rowq = lambda w: pl.BlockSpec((QB, w), lambda s, qb: (qblk(s, qb), 0))
    tabq = pl.BlockSpec((QB, 128), lambda s, qb: (jnp.where(qb < nlb, qb, cfg.seq // QB), 0))
    tabk = pl.BlockSpec((cfg.seq, 128), lambda s, qb: (0, 0))
    return _hosted_call(
        body, ride, name, (cfg.b, nqb), [sink] + [qkv] * 9 + [mix, dmix, lse, cos_t, sin_t, cos_t, sin_t],
        [pl.BlockSpec(memory_space=pltpu.SMEM), qspec, *kvspecs, rowq(ATTN_W), rowq(ATTN_W), rowq(N_HEADS),
         tabq, tabq, tabk, tabk],
        [jax.ShapeDtypeStruct((cfg.r, IN_W), BF16), jax.ShapeDtypeStruct((cfg.nl, 256), BF16),
         jax.ShapeDtypeStruct((cfg.nc, 256), BF16), jax.ShapeDtypeStruct((N_HEADS, 128), F32)],
        [rowq(ATTN_W), pl.BlockSpec((cfg.seq, 256), lambda s, qb: (s, 0)),
         pl.BlockSpec((cfg.ctx, 256), lambda s, qb: (s, 0)), pl.BlockSpec((N_HEADS, 128), lambda s, qb: (0, 0))],
        [pltpu.VMEM((cfg.seq, 256), F32), pltpu.VMEM((cfg.ctx, 256), F32), pltpu.VMEM((QB, ATTN_W), F32)], _params(2))


def _pool_geometry(n, c):
    lane = lax.broadcasted_iota(jnp.int32, (1, POOL_W), 1) // HEAD_DIM
    wl = jnp.where(lane == 0, 1, jnp.where(lane == 1, 2, jnp.where(lane == 2, 4, 8)))
    wr = wl - 1
    t = c * CHUNK + lax.broadcasted_iota(jnp.int32, (CHUNK, POOL_W), 0)
    cnt = (jnp.minimum(t + wr, n - 1) - jnp.maximum(t - wl, 0) + 1).astype(F32)
    return wl, wr, cnt


def _build_phases(src, ph, c):
    for s in range(1, 8):
        ph[s - 1] = src[c * CHUNK + s:c * CHUNK + s + CHUNK + 24, :]


def _window(src, ph, c, off):
    a, s = divmod(off, 8)
    if s == 0:
        return src[c * CHUNK + 8 * a:c * CHUNK + 8 * a + CHUNK, :]
    return ph[s - 1, 8 * a:8 * a + CHUNK, :]


def _conv_chunk(hp, ph, dw_ref, dwb_ref, c):
    _build_phases(hp, ph, c)
    acc = jnp.zeros((CHUNK, CONV_W), F32) + dwb_ref[...]
    for j in range(CONV_K):
        acc = acc + dw_ref[j:j + 1, :] * _window(hp, ph, c, j + 1)
    return acc


def _fill_glu(cp_ref, hp, n):
    hp[0:PAD, :] = jnp.zeros((PAD, CONV_W), F32)
    hp[PAD + n:2 * PAD + n, :] = jnp.zeros((PAD, CONV_W), F32)
    for c in range(n // CHUNK):
        rows = slice(c * CHUNK, (c + 1) * CHUNK)
        a = cp_ref[rows, 0:CONV_W].astype(F32)
        g = cp_ref[rows, CONV_W:2 * CONV_W].astype(F32)
        hp[PAD + c * CHUNK:PAD + (c + 1) * CHUNK, :] = a * _sigmoid(g)


def _fill_pool(cp_ref, pp, n):
    pp[0:PAD, :] = jnp.zeros((PAD, POOL_W), F32)
    pp[PAD + n:2 * PAD + n, :] = jnp.zeros((PAD, POOL_W), F32)
    for c in range(n // CHUNK):
        pp[PAD + c * CHUNK:PAD + (c + 1) * CHUNK, :] = cp_ref[c * CHUNK:(c + 1) * CHUNK, 2 * CONV_W:768].astype(F32)


LV = CHUNK + 2 * PAD
_LEVELS = pltpu.VMEM((3, LV + 16, POOL_W), F32)


def _clear_level_edges(lv):
    for b in range(3):
        lv[b, 0:8] = jnp.zeros((8, POOL_W), F32)
        lv[b, 8 + LV:16 + LV] = jnp.zeros((8, POOL_W), F32)


def _window_sums(src, lv, c, lead):
    lv[0, 8:8 + LV] = src[c * CHUNK:c * CHUNK + LV, :]
    lo = 7 if lead < 0 else 8
    lv[1, 8:8 + LV] = lv[0, lo:lo + LV] + lv[0, lo + 1:lo + 1 + LV]
    group = lax.broadcasted_iota(jnp.int32, (1, POOL_W), 1) // HEAD_DIM
    rows = slice(8 + PAD, 8 + PAD + CHUNK)
    res = lv[1, rows]
    cur = 1
    for g, s in ((1, 1), (2, 2), (3, 4)):
        nxt = 3 - cur
        lv[nxt, 8:8 + LV] = lv[cur, 8 - s:8 - s + LV] + lv[cur, 8 + s:8 + s + LV]
        res = jnp.where(group >= g, lv[nxt, rows], res)
        cur = nxt
    return res


def _pool_chunk(pp, lv, n, c):
    _, _, cnt = _pool_geometry(n, c)
    return _window_sums(pp, lv, c, -1) / cnt - pp[PAD + c * CHUNK:PAD + (c + 1) * CHUNK, :], cnt


def _seq_specs(n, blk_off, width, col=0):
    return pl.BlockSpec((n, width), lambda s: (blk_off + s, col))


def _full(shape):
    return pl.BlockSpec(shape, lambda s: (0,) * len(shape))


_PHASES = pltpu.VMEM((7, CHUNK + 24, CONV_W), F32)


def _convpool_fwd(name, cpin, mix, yconv, prm, n, blk_off, cfg):
    dw, dwb, lng, lnb, wbd, ps = prm
    n_alias = 1 if yconv is None else 2

    def body(*refs):
        cp_ref, dw_ref, dwb_ref, lng_ref, lnb_ref, wbd_ref, ps_ref = refs[:7]
        out_ref, y_ref, hp, pp, ph, lv = refs[7 + n_alias:]
        _fill_glu(cp_ref, hp, n)
        _fill_pool(cp_ref, pp, n)
        _clear_level_edges(lv)
        for c in range(n // CHUNK):
            rows = slice(c * CHUNK, (c + 1) * CHUNK)
            y = _conv_chunk(hp, ph, dw_ref, dwb_ref, c)
            y_ref[rows, :] = y
            d = y - jnp.mean(y, axis=-1, keepdims=True)
            hn = d * lax.rsqrt(jnp.mean(d * d, axis=-1, keepdims=True) + EPS) * lng_ref[...] + lnb_ref[...]
            out_ref[rows, 0:CONV_W] = (hn * _sigmoid(hn)).astype(BF16)
            yp, _ = _pool_chunk(pp, lv, n, c)
            out_ref[rows, CONV_W:2 * CONV_W] = (_dot(yp, wbd_ref[...]) * ps_ref[...]).astype(BF16)

    through = [mix] if yconv is None else [mix, yconv]
    return pl.pallas_call(
        body, name=name, grid=(cfg.b,),
        in_specs=[_seq_specs(n, blk_off, 768), _full((32, CONV_W)), _full((1, CONV_W)), _full((1, CONV_W)),
                  _full((1, CONV_W)), _full((POOL_W, POOL_W)), _full((1, POOL_W))] + [_ANY] * n_alias,
        out_specs=[_seq_specs(n, blk_off, 512, 1), _seq_specs(n, blk_off, CONV_W)],
        out_shape=[jax.ShapeDtypeStruct((cfg.r, D), BF16), jax.ShapeDtypeStruct((cfg.r, CONV_W), F32)],
        scratch_shapes=[pltpu.VMEM((n + 2 * PAD, CONV_W), F32), pltpu.VMEM((n + 2 * PAD, POOL_W), F32), _PHASES, _LEVELS],
        input_output_aliases={7 + i: i for i in range(n_alias)},
        compiler_params=_params(1))(cpin, dw, dwb, lng, lnb, wbd, ps, *through)


_SMALL_SHAPES = [(32, CONV_W), (1, CONV_W), (1, CONV_W), (1, CONV_W), (POOL_W, POOL_W), (1, POOL_W)]


def _convpool_bwd(name, cpin, yconv, dmix, prm, acc_in, n, blk_off, cfg):
    dw, dwb, lng, lnb, wbd, ps = prm
    nch = n // CHUNK

    def body(cp_ref, y_ref, dm_ref, dw_ref, dwb_ref, lng_ref, lnb_ref, wbd_ref, ps_ref, dcp_in,
             a_dw, a_dwb, a_lng, a_lnb, a_wbd, a_ps,
             dcp_ref, o_dw, o_dwb, o_lng, o_lnb, o_wbd, o_ps, hp, dyp, pp, wp, dyv, dwacc, ph, lv):
        s = pl.program_id(0)
        _clear_level_edges(lv)

        @pl.when(s == 0)
        def _():
            for o_, a_ in ((o_dw, a_dw), (o_dwb, a_dwb), (o_lng, a_lng), (o_lnb, a_lnb), (o_wbd, a_wbd), (o_ps, a_ps)):
                o_[...] = a_[...]
            dwacc[...] = jnp.zeros_like(dwacc)

        _fill_glu(cp_ref, hp, n)
        _fill_pool(cp_ref, pp, n)
        for ref in (dyp, wp):
            ref[0:PAD, :] = jnp.zeros((PAD, CONV_W), F32)
            ref[PAD + n:2 * PAD + n, :] = jnp.zeros((PAD, CONV_W), F32)
        for c in range(nch):
            rows = slice(c * CHUNK, (c + 1) * CHUNK)
            y = y_ref[rows, :]
            d = y - jnp.mean(y, axis=-1, keepdims=True)
            rstd = lax.rsqrt(jnp.mean(d * d, axis=-1, keepdims=True) + EPS)
            xh = d * rstd
            hn = xh * lng_ref[...] + lnb_ref[...]
            sg = _sigmoid(hn)
            dhn = dm_ref[rows, 0:CONV_W] * (sg * (1.0 + hn * (1.0 - sg)))
            o_lnb[...] += _colsum(dhn)
            o_lng[...] += _colsum(dhn * xh)
            dxh = dhn * lng_ref[...]
            dy = rstd * (dxh - jnp.mean(dxh, axis=-1, keepdims=True) - xh * jnp.mean(dxh * xh, axis=-1, keepdims=True))
            o_dwb[...] += _colsum(dy)
            dyp[PAD + c * CHUNK:PAD + (c + 1) * CHUNK, :] = dy
            _build_phases(hp, ph, c)
            for j in range(CONV_K):
                prod = dy * _window(hp, ph, c, j + 1)
                dwacc[8 * j:8 * j + 8, :] += jnp.sum(prod.reshape(CHUNK // 8, 8, CONV_W), axis=0)
            yp, cnt = _pool_chunk(pp, lv, n, c)
            dz = dm_ref[rows, CONV_W:2 * CONV_W]
            o_ps[...] += _colsum(dz * _dot(yp, wbd_ref[...]))
            dzs = dz * ps_ref[...]
            o_wbd[...] += _dot(yp, dzs, "tn")
            dv = _dot(dzs, wbd_ref[...], "nt")
            dyv[rows, :] = dv
            wp[PAD + c * CHUNK:PAD + (c + 1) * CHUNK, :] = dv / cnt
        for c in range(nch):
            rows = slice(c * CHUNK, (c + 1) * CHUNK)
            _build_phases(dyp, ph, c)
            dh = jnp.zeros((CHUNK, CONV_W), F32)
            for j in range(CONV_K):
                dh = dh + dw_ref[j:j + 1, :] * _window(dyp, ph, c, 31 - j)
            a = cp_ref[rows, 0:CONV_W].astype(F32)
            sg = _sigmoid(cp_ref[rows, CONV_W:2 * CONV_W].astype(F32))
            dcp_ref[rows, 0:CONV_W] = (dh * sg).astype(BF16)
            dcp_ref[rows, CONV_W:2 * CONV_W] = (dh * a * sg * (1.0 - sg)).astype(BF16)
            dcp_ref[rows, 2 * CONV_W:768] = (_window_sums(wp, lv, c, 1) - dyv[rows, :]).astype(BF16)

        @pl.when(s == cfg.b - 1)
        def _():
            for j in range(CONV_K):
                o_dw[j:j + 1, :] += _colsum(dwacc[8 * j:8 * j + 8, :])

    small_specs = [_full(sh) for sh in _SMALL_SHAPES]
    return pl.pallas_call(
        body, name=name, grid=(cfg.b,),
        in_specs=[_seq_specs(n, blk_off, 768), _seq_specs(n, blk_off, CONV_W), _seq_specs(n, blk_off, 512, 1),
                  *small_specs, _ANY, *small_specs],
        out_specs=[_seq_specs(n, blk_off, 768, 1), *small_specs],
        out_shape=[jax.ShapeDtypeStruct((cfg.r, IN_W), BF16)] + [jax.ShapeDtypeStruct(sh, F32) for sh in _SMALL_SHAPES],
        scratch_shapes=[pltpu.VMEM((n + 2 * PAD, CONV_W), F32), pltpu.VMEM((n + 2 * PAD, CONV_W), F32),
                        pltpu.VMEM((n + 2 * PAD, POOL_W), F32), pltpu.VMEM((n + 2 * PAD, POOL_W), F32),
                        pltpu.VMEM((n, POOL_W), F32), pltpu.VMEM((8 * 32, CONV_W), F32), _PHASES, _LEVELS],
        input_output_aliases={9: 0}, compiler_params=_params(1))(cpin, yconv, dmix, dw, dwb, lng, lnb, wbd, ps, *acc_in)


def _place_kv(name, du, dkvl, dkvc, with_ctx, cfg):
    nlb = cfg.nl // TM

    def body(l_ref, c_ref, du_in, o_ref):
        i = pl.program_id(0)
        o_ref[...] = jnp.where(i < nlb, l_ref[...], c_ref[...])

    return pl.pallas_call(
        body, name=name, grid=(cfg.r // TM if with_ctx else nlb,),
        in_specs=[pl.BlockSpec((TM, 256), lambda i: (jnp.minimum(i, nlb - 1), 0)),
                  pl.BlockSpec((TM, 256), lambda i: (jnp.maximum(i - nlb, 0), 0)), _ANY],
        out_specs=pl.BlockSpec((TM, 256), lambda i: (i, 2)), out_shape=jax.ShapeDtypeStruct((cfg.r, IN_W), BF16),
        input_output_aliases={2: 0}, compiler_params=_params(1))(dkvl, dkvc, du)


def _place_ctx_kv_only(name, du, dkvc, cfg):
    nlb = cfg.nl // TM

    def body(c_ref, du_in, o_ref):
        o_ref[...] = jnp.zeros_like(o_ref)
        o_ref[:, ATTN_W:ATTN_W + 256] = c_ref[...]

    return pl.pallas_call(
        body, name=name, grid=(cfg.nc // TM,), in_specs=[pl.BlockSpec((TM, 256), lambda i: (i, 0)), _ANY],
        out_specs=pl.BlockSpec((TM, IN_W), lambda i: (nlb + i, 0)), out_shape=jax.ShapeDtypeStruct((cfg.r, IN_W), BF16),
        input_output_aliases={1: 0}, compiler_params=_params(1))(dkvc, du)


def _ffn_in(name, h, w_ffn_in, layer, nblk, cfg, ride=None):
    def body(h_ref, wg_ref, wu_ref, fac_ref, act_ref):
        hv = h_ref[...]
        g = _dot(hv, wg_ref[...])
        u = _dot(hv, wu_ref[...])
        s = _sigmoid(g)
        gs = g * s
        fac_ref[0] = ((s + gs * (1.0 - s)) * u).astype(BF16)
        fac_ref[1] = gs.astype(BF16)
        act_ref[...] = (gs * u).astype(BF16)

    wspec = lambda base: pl.BlockSpec((None, None, D, HALF_FF), lambda j, i: (layer, base + j, 0, 0))
    return _hosted_call(
        body, ride, name, (2, nblk), [h, w_ffn_in, w_ffn_in],
        [pl.BlockSpec((TM, D), lambda j, i: (i, 0)), wspec(0), wspec(2)],
        [jax.ShapeDtypeStruct((2, cfg.r, D_FF), BF16), jax.ShapeDtypeStruct((cfg.r, D_FF), BF16)],
        [pl.BlockSpec((2, TM, HALF_FF), lambda j, i: (0, i, j)), pl.BlockSpec((TM, HALF_FF), lambda j, i: (i, j))],
        [], _params(2))


def _row_block(rows, cols, max_bytes=1 << 20):
    best = 16
    for t in range(16, rows + 1, 16):
        if rows % t == 0 and t * cols * 4 <= max_bytes:
            best = t
    assert rows % best == 0
    return best


def _pair_add(name, own32, recv, c_idx):
    _, _, s0, s1 = own32.shape
    tr = _row_block(s0, s1)

    def body(c_ref, a_ref, b_ref, o_ref):
        o_ref[...] = (a_ref[...].astype(F32) + b_ref[...].astype(F32)).astype(BF16)

    grid_spec = pltpu.PrefetchScalarGridSpec(
        num_scalar_prefetch=1, grid=(N_SHARD * s0 // tr,),
        in_specs=[pl.BlockSpec((None, tr, s1), lambda i, c: (c[0], i, 0)), pl.BlockSpec((tr, s1), lambda i, c: (i, 0))],
        out_specs=pl.BlockSpec((tr, s1), lambda i, c: (i, 0)))
    out = pl.pallas_call(body, name=name, grid_spec=grid_spec, out_shape=jax.ShapeDtypeStruct((N_SHARD * s0, s1), BF16),
                         compiler_params=_params(1))(c_idx, own32.reshape(2, N_SHARD * s0, s1), recv.reshape(N_SHARD * s0, s1))
    return out.reshape(N_SHARD, s0, s1)


def _shard_sum(name, pair_sum, recv, jc_idx):
    _, s0, s1 = pair_sum.shape
    tr = _row_block(s0, s1)

    def body(jc_ref, a_ref, b_ref, o_ref):
        o_ref[...] = ((a_ref[...].astype(F32) + b_ref[0].astype(F32)) + b_ref[1].astype(F32)) + b_ref[2].astype(F32)

    grid_spec = pltpu.PrefetchScalarGridSpec(
        num_scalar_prefetch=1, grid=(s0 // tr,),
        in_specs=[pl.BlockSpec((None, tr, s1), lambda i, jc: (jc[0], i, 0)), pl.BlockSpec((3, tr, s1), lambda i, jc: (0, i, 0))],
        out_specs=pl.BlockSpec((None, tr, s1), lambda i, jc: (jc[1], i, 0)))
    return pl.pallas_call(body, name=name, grid_spec=grid_spec, out_shape=jax.ShapeDtypeStruct((2, s0, s1), F32),
                          compiler_params=_params(1))(jc_idx, pair_sum, recv)


def _adamw_math(w, g, m, v):
    m = ADAM_B1 * m + (1.0 - ADAM_B1) * g
    v = ADAM_B2 * v + (1.0 - ADAM_B2) * (g * g)
    m_hat = m / (1.0 - ADAM_B1 ** ADAM_STEP)
    v_hat = v / (1.0 - ADAM_B2 ** ADAM_STEP)
    delta = -ADAM_LR * (m_hat / (jnp.sqrt(v_hat) + ADAM_EPS) + ADAM_WD * w)
    return delta, m, v


def _adamw(name, w, g, m, v):
    rows, cols = w.shape
    tr = rows if rows % 16 else _row_block(rows, cols, 1 << 19)

    def body(w_ref, g_ref, m_ref, v_ref, d_ref, mo_ref, vo_ref):
        d, mn, vn = _adamw_math(w_ref[...], g_ref[...], m_ref[...], v_ref[...])
        d_ref[...] = d
        mo_ref[...] = mn
        vo_ref[...] = vn

    spec = pl.BlockSpec((tr, cols), lambda i: (i, 0))
    shape = jax.ShapeDtypeStruct((rows, cols), F32)
    return pl.pallas_call(body, name=name, grid=(rows // tr,), in_specs=[spec] * 4, out_specs=[spec] * 3,
                          out_shape=[shape] * 3, compiler_params=_params(1))(w, g, m, v)


def _adamw_layers(name, w, g_layers, m, v):
    rows, cols = w.shape
    s0 = rows // 2
    tr = _row_block(s0, cols, 1 << 19)
    nb = s0 // tr

    def body(w_ref, g0_ref, g1_ref, m_ref, v_ref, g_ref, d_ref, mo_ref, vo_ref):
        g = jnp.where(pl.program_id(0) < nb, g0_ref[...], g1_ref[...])
        d, mn, vn = _adamw_math(w_ref[...], g, m_ref[...], v_ref[...])
        g_ref[...] = g
        d_ref[...] = d
        mo_ref[...] = mn
        vo_ref[...] = vn

    spec = pl.BlockSpec((tr, cols), lambda i: (i, 0))
    shape = jax.ShapeDtypeStruct((rows, cols), F32)
    return pl.pallas_call(
        body, name=name, grid=(2 * nb,),
        in_specs=[spec, pl.BlockSpec((tr, cols), lambda i: (jnp.minimum(i, nb - 1), 0)),
                  pl.BlockSpec((tr, cols), lambda i: (jnp.maximum(i - nb, 0), 0)), spec, spec],
        out_specs=[spec] * 4, out_shape=[shape] * 4, compiler_params=_params(1))(w, g_layers[0], g_layers[1], m, v)


def _position():
    return lax.axis_index("x"), lax.axis_index("y"), lax.axis_index("c")


def _other_chips(x, y):
    return [(1 - x, y), (x, 1 - y), (1 - x, 1 - y)]


def _run_ride(name, ride):
    n_in, n_out = len(ride.ins), len(ride.out_shape)

    def body(*refs):
        parts = (refs[:n_in], refs[n_in:n_in + n_out], refs[n_in + n_out:])
        ride.start(*parts)
        ride.finish(*parts)

    return pl.pallas_call(
        body, name=name, in_specs=[_ANY] * n_in, out_specs=[_ANY] * n_out, out_shape=ride.out_shape,
        scratch_shapes=ride.scratch, compiler_params=pltpu.CompilerParams(vmem_limit_bytes=VMEM_LIMIT_V7X))(*ride.ins)


def _gather_ride(shards):
    n = len(shards)

    def copies(ins, outs, scr):
        ssem, rsem = scr[n], scr[n + 1]
        x, y, c = _position()
        me, sibling = 2 * x + y, (x, y, 1 - c)

        def remote(src, dst, i, dev):
            return pltpu.make_async_remote_copy(src, dst, ssem.at[i], rsem.at[i], device_id=dev, device_id_type=_MESH)

        fetch_out, fetch_in, pass_out, pass_in = [], [], [], []
        for a, (src, dst) in enumerate(zip(ins, outs)):
            for k, (px, py) in enumerate(_other_chips(x, y)):
                j, i1, i2 = 2 * px + py, 3 * a + k, 3 * n + 3 * a + k
                fetch_out.append(remote(src.at[c], dst.at[me, c], i1, (px, py, c)))
                fetch_in.append(remote(src.at[c], dst.at[j, c], i1, (px, py, c)))
                pass_out.append(remote(dst.at[j, c], dst.at[j, c], i2, sibling))
                pass_in.append(remote(dst.at[j, 1 - c], dst.at[j, 1 - c], i2, sibling))
        return me, fetch_out, fetch_in, pass_out, pass_in

    def start(ins, outs, scr):
        bufs, lsem = scr[:n], scr[n + 2]
        me, fetch_out, _, _, _ = copies(ins, outs, scr)
        for cp in fetch_out:
            cp.start()
        loads = []
        for a, (src, buf) in enumerate(zip(ins, bufs)):
            ld = pltpu.make_async_copy(src, buf, lsem.at[2 * a])
            ld.start()
            loads.append(ld)
        for a, (ld, buf, dst) in enumerate(zip(loads, bufs, outs)):
            ld.wait()
            st = pltpu.make_async_copy(buf, dst.at[me], lsem.at[2 * a + 1])
            st.start()
            st.wait()

    def finish(ins, outs, scr):
        _, fetch_out, fetch_in, pass_out, pass_in = copies(ins, outs, scr)
        for arrived, onward in zip(fetch_in, pass_out):
            arrived.wait_recv()
            onward.start()
        for cp in pass_in:
            cp.wait_recv()
        for cp in fetch_out + pass_out:
            cp.wait_send()

    return _Ride(list(shards), [jax.ShapeDtypeStruct((N_SHARD,) + s.shape, s.dtype) for s in shards],
                 [pltpu.VMEM(s.shape, s.dtype) for s in shards]
                 + [pltpu.SemaphoreType.DMA((6 * n,)), pltpu.SemaphoreType.DMA((6 * n,)), pltpu.SemaphoreType.DMA((2 * n,))],
                 start, finish)


def _comm(name, ins, out_shape, n_remote, plan):
    n_in, n_out = len(ins), len(out_shape)

    def body(*refs):
        plan(refs[:n_in], refs[n_in:n_in + n_out], *refs[n_in + n_out:])

    return pl.pallas_call(
        body, name=name, in_specs=[_ANY] * n_in, out_specs=[_ANY] * n_out, out_shape=out_shape,
        scratch_shapes=[pltpu.SemaphoreType.DMA((n_remote,)), pltpu.SemaphoreType.DMA((n_remote,))])(*ins)


def _send_other_half(name, grads_bf):
    n = len(grads_bf)

    def plan(ins, outs, ssem, rsem):
        x, y, c = _position()
        started = []
        for a, (src, dst) in enumerate(zip(ins, outs)):
            cp = pltpu.make_async_remote_copy(src.at[1 - c], dst, ssem.at[a], rsem.at[a], device_id=(x, y, 1 - c),
                                              device_id_type=_MESH)
            cp.start()
            started.append(cp)
        for cp in started:
            cp.wait_recv()
        for cp in started:
            cp.wait_send()

    shapes = [jax.ShapeDtypeStruct(s.shape[1:], s.dtype) for s in grads_bf]
    return _comm(name, grads_bf, shapes, n, plan)


def _exchange_ride(pair_sums):
    n = len(pair_sums)

    def copies(ins, outs, scr):
        ssem, rsem = scr
        x, y, c = _position()
        return [pltpu.make_async_remote_copy(src.at[2 * px + py], dst.at[k], ssem.at[3 * a + k], rsem.at[3 * a + k],
                                             device_id=(px, py, c), device_id_type=_MESH)
                for a, (src, dst) in enumerate(zip(ins, outs)) for k, (px, py) in enumerate(_other_chips(x, y))]

    def start(ins, outs, scr):
        for cp in copies(ins, outs, scr):
            cp.start()

    def finish(ins, outs, scr):
        for cp in copies(ins, outs, scr):
            cp.wait_recv()
        for cp in copies(ins, outs, scr):
            cp.wait_send()

    return _Ride(list(pair_sums), [jax.ShapeDtypeStruct((3,) + s.shape[1:], s.dtype) for s in pair_sums],
                 [pltpu.SemaphoreType.DMA((3 * n,)), pltpu.SemaphoreType.DMA((3 * n,))], start, finish)


def _swap_reduced(name, grads):
    n = len(grads)

    def body(*refs):
        ins, outs, ssem, rsem = refs[:n], refs[n:2 * n], refs[2 * n], refs[2 * n + 1]
        x, y, c = _position()
        sent = []
        for a, (src, dst) in enumerate(zip(ins, outs)):
            cp = pltpu.make_async_remote_copy(src.at[c], dst.at[c], ssem.at[a], rsem.at[a], device_id=(x, y, 1 - c),
                                              device_id_type=_MESH)
            cp.start()
            sent.append(cp)
        for a, (src, dst) in enumerate(zip(ins, outs)):
            pltpu.make_async_remote_copy(src.at[1 - c], dst.at[1 - c], ssem.at[a], rsem.at[a], device_id=(x, y, 1 - c),
                                         device_id_type=_MESH).wait_recv()
        for cp in sent:
            cp.wait_send()

    return pl.pallas_call(
        body, name=name, in_specs=[_ANY] * n, out_specs=[_ANY] * n,
        out_shape=[jax.ShapeDtypeStruct(g.shape, g.dtype) for g in grads],
        scratch_shapes=[pltpu.SemaphoreType.DMA((n,)), pltpu.SemaphoreType.DMA((n,))],
        input_output_aliases={a: a for a in range(n)})(*grads)


_FLIPS = [(dx, dy, dc) for dx in (0, 1) for dy in (0, 1) for dc in (0, 1) if dx + dy + dc]
_VMEM = pl.BlockSpec(memory_space=pltpu.VMEM)


def _to_all(src_of, dst, ssem, rsem):
    x, y, c = _position()
    me = 4 * x + 2 * y + c
    peers = [((x + dx) % 2, (y + dy) % 2, (c + dc) % 2) for dx, dy, dc in _FLIPS]
    sent = []
    for k, (px, py, pc) in enumerate(peers):
        cp = pltpu.make_async_remote_copy(src_of(2 * px + py), dst.at[me], ssem.at[k], rsem.at[k],
                                          device_id=(px, py, pc), device_id_type=_MESH)
        cp.start()
        sent.append(cp)
    for k, (px, py, pc) in enumerate(peers):
        pltpu.make_async_remote_copy(src_of(2 * px + py), dst.at[4 * px + 2 * py + pc], ssem.at[k], rsem.at[k],
                                     device_id=(px, py, pc), device_id_type=_MESH).wait_recv()
    for cp in sent:
        cp.wait_send()
    return me, 2 * x + y


def _share_small(name, block, total):
    shape = block.shape

    def body(in_ref, out_ref, *scratch):
        buf, ssem, rsem = (out_ref,) + scratch if not total else scratch
        me, _ = _to_all(lambda chip: in_ref, buf, ssem, rsem)
        buf[me] = in_ref[...]
        if total:
            acc = buf[0]
            for d in range(1, 8):
                acc = acc + buf[d]
            out_ref[...] = acc

    sems = [pltpu.SemaphoreType.DMA((7,)), pltpu.SemaphoreType.DMA((7,))]
    return pl.pallas_call(
        body, name=name, in_specs=[_VMEM], out_specs=_VMEM,
        out_shape=jax.ShapeDtypeStruct(shape if total else (8,) + shape, F32),
        scratch_shapes=([pltpu.VMEM((8,) + shape, F32)] if total else []) + sems,
        compiler_params=pltpu.CompilerParams(vmem_limit_bytes=VMEM_LIMIT_V7X))(block)


def _mod_rows_exchange(mv):
    def body(mv_ref, out_ref, ssem, rsem):
        x, y, c = _position()
        me = 2 * x + y
        out_ref[me] = mv_ref[4 * x + 2 * y + c]
        sent = []
        for k, (px, py) in enumerate(_other_chips(x, y)):
            cp = pltpu.make_async_remote_copy(mv_ref.at[4 * px + 2 * py + c], out_ref.at[me], ssem.at[k], rsem.at[k],
                                              device_id=(px, py, c), device_id_type=_MESH)
            cp.start()
            sent.append(cp)
        for k, (px, py) in enumerate(_other_chips(x, y)):
            pltpu.make_async_remote_copy(mv_ref.at[0], out_ref.at[2 * px + py], ssem.at[k], rsem.at[k],
                                         device_id=(px, py, c), device_id_type=_MESH).wait_recv()
        for cp in sent:
            cp.wait_send()

    return pl.pallas_call(
        body, name="mod_rows_exchange", in_specs=[_VMEM], out_specs=_VMEM,
        out_shape=jax.ShapeDtypeStruct((N_SHARD,) + mv.shape[1:], F32),
        scratch_shapes=[pltpu.SemaphoreType.DMA((3,)), pltpu.SemaphoreType.DMA((3,))],
        compiler_params=pltpu.CompilerParams(vmem_limit_bytes=VMEM_LIMIT_V7X))(mv)


def _mod_grad_exchange(dmj, dm_rows):
    def body(dmj_ref, rows_ref, out_ref, bias_ref, ssem, rsem):
        me, chip = _to_all(lambda j: dmj_ref.at[j], out_ref, ssem, rsem)
        out_ref[me] = dmj_ref[chip]
        for l in range(2):
            bias_ref[l] = _colsum(rows_ref[l])

    return pl.pallas_call(
        body, name="mod_grad_exchange", in_specs=[_VMEM, _VMEM], out_specs=[_VMEM, _VMEM],
        out_shape=[jax.ShapeDtypeStruct((8,) + dmj.shape[1:], F32), jax.ShapeDtypeStruct((2, 1, dm_rows.shape[-1]), F32)],
        scratch_shapes=[pltpu.SemaphoreType.DMA((7,)), pltpu.SemaphoreType.DMA((7,))],
        compiler_params=pltpu.CompilerParams(vmem_limit_bytes=VMEM_LIMIT_V7X))(dmj, dm_rows)


def _pack(arrays):
    flat = jnp.concatenate([a.reshape(-1).astype(F32) for a in arrays])
    total = flat.shape[0]
    rows = -(-total // (8 * PACK_LANES)) * 8
    return jnp.pad(flat, (0, rows * PACK_LANES - total)).reshape(rows, PACK_LANES)


def _unpack(pack, shapes):
    flat, out, pos = pack.reshape(-1), [], 0
    for sh in shapes:
        size = int(np.prod(sh)) if len(sh) else 1
        out.append(flat[pos:pos + size].reshape(sh))
        pos += size
    return out


def _block_diag(pw):
    out = jnp.zeros((POOL_W, POOL_W), pw.dtype)
    for g in range(4):
        out = out.at[g * 64:(g + 1) * 64, g * 64:(g + 1) * 64].set(pw[g])
    return out


def _local_step(x, ctx, small, mod3s, loss_target, comm):
    cfg = _Cfg(x.shape[0], x.shape[1], ctx.shape[1])
    assert cfg.seq % TM == 0 and cfg.nc % TM == 0 and cfg.seq % cfg.ctx == 0 and cfg.ctx % CHUNK == 0
    nb_all, nb_lat = cfg.r // TM, cfg.nl // TM
    last = 1
    wf, big = comm.wf, comm.grads
    cos_t, sin_t = _rope_tables(cfg.seq)
    xs = (x.reshape(cfg.nl, D), ctx.reshape(cfg.nc, D))
    row = lambda w: pl.BlockSpec((TM, w), lambda i, j, k: (i, 0))
    mod3_spec = pl.BlockSpec((None, 6, D), lambda i, j, k: (cfg.mod_row(i), 0, 0))
    whole = lambda rows: pl.BlockSpec((None, rows, D), lambda i, j, k: (0, 0, 0))

    def conv_params(l):
        dw = jnp.pad(wf[l]["conv_dw"], ((0, 1), (0, 0)))
        return (dw, small["conv_dw_b"][l][None], small["conv_ln_g"][l][None], small["conv_ln_b"][l][None],
                _block_diag(small["pool_w"][l]).astype(BF16), small["pool_scale"][l][None])

    def residual_epi(ig):
        def epi(acc, ex, outs):
            x_ref, m_ref = ex
            outs[0][...] = x_ref[...] + m_ref[ig:ig + 1, :] * acc
            outs[1][...] = acc.astype(BF16)
        return epi

    def mixer_epi(acc, ex, outs):
        x_ref, m_ref, g_ref = ex
        x1 = x_ref[...] + m_ref[2:3, :] * acc
        outs[0][...] = x1
        outs[1][...] = acc.astype(BF16)
        outs[2][...] = _norm_mod(x1, g_ref[...], m_ref, 3, 4)

    def const(shape):
        return pl.BlockSpec(shape, lambda i, j, k: (0,) * len(shape))

    def loss_epi(acc, ex, outs):
        x_ref, m_ref, t_ref, g_ref = ex
        dx_ref, loss_ref, dfg_ref, dy_ref, dgate_ref = outs
        i = pl.program_id(0)
        gate = m_ref[5:6, :]
        xv = x_ref[...] + gate * acc
        g = g_ref[...]
        r = lax.rsqrt(jnp.mean(xv * xv, axis=-1, keepdims=True) + EPS)
        xh = xv * r
        err = xh * g - t_ref[...]
        _accumulate_rows(i == 0, loss_ref, (0.5 / D) * _colsum(jnp.sum(err * err, axis=-1, keepdims=True)))
        dy = err * (1.0 / D)
        _accumulate_rows(i == 0, dfg_ref, _colsum(dy * xh))
        dxh = dy * g
        dxv = r * (dxh - xh * jnp.mean(dxh * xh, axis=-1, keepdims=True))
        dx_ref[...] = dxv
        dy_ref[...] = (dxv * gate).astype(BF16)
        _accumulate_rows(cfg.first_of_row(i), dgate_ref, _colsum(dxv * acc))

    def hosted(name, call):
        outs, got = call(comm.ride(name))
        comm.landed(name, got)
        return outs

    saved = []
    for l in range(2):
        nb = nb_lat if l == last else nb_all
        wl = wf[l]
        mod3 = mod3s[l]
        res = _in_proj(f"in_proj{l}", xs, small["norm1_g"][l][None], mod3, wl["w_in"], 0, cos_t, sin_t, cfg)
        if isinstance(xs, tuple):
            xs = res[0]
        h1, qkv, cpin = res[-3:]
        mix, lse = hosted(f"att_fwd{l}", lambda ride: _att_fwd(f"att_fwd{l}", qkv, small["attn_sink"][l], l != last, cfg, ride))
        prm = conv_params(l)
        mix, yconv = _convpool_fwd(f"convpool_fwd_lat{l}", cpin, mix, None, prm, cfg.seq, 0, cfg)
        if l != last:
            mix, yconv = _convpool_fwd(f"convpool_fwd_ctx{l}", cpin, mix, yconv, prm, cfg.ctx, cfg.nl // cfg.ctx, cfg)
        x1, y1, h2 = _mm(f"out_proj{l}", "nn", (nb, 1, 1), mix, wl["w_out"].reshape(1, D, D), row(D), whole(D),
                         [jax.ShapeDtypeStruct((cfg.r, D), F32)] + [jax.ShapeDtypeStruct((cfg.r, D), BF16)] * 2, [row(D)] * 3,
                         extras=[xs, mod3, small["norm2_g"][l][None]],
                         extra_specs=[row(D), mod3_spec, pl.BlockSpec((1, D), lambda i, j, k: (0, 0))], epi=mixer_epi)
        gu, act = hosted(f"ffn_in{l}", lambda ride: _ffn_in(f"ffn_in{l}", h2, wl["w_ffn_in"], 0, nb, cfg, ride))
        if l != last:
            xs_next, y2 = _mm(f"ffn_out{l}", "nn", (nb, 1, 1), act, wl["w_ffn_out"].reshape(1, D_FF, D), row(D_FF),
                              whole(D_FF), [jax.ShapeDtypeStruct((cfg.r, D), F32), jax.ShapeDtypeStruct((cfg.r, D), BF16)],
                              [row(D), row(D)], extras=[x1, mod3], extra_specs=[row(D), mod3_spec], epi=residual_epi(5))
        else:
            xs_next, y2 = None, None
            dx, loss, d_final_g, dy2, dg2 = _mm(
                f"ffn_out{l}", "nn", (nb, 1, 1), act, wl["w_ffn_out"].reshape(1, D_FF, D), row(D_FF), whole(D_FF),
                [jax.ShapeDtypeStruct((cfg.r, D), F32), jax.ShapeDtypeStruct((1, 1), F32), jax.ShapeDtypeStruct((1, D), F32),
                 jax.ShapeDtypeStruct((cfg.r, D), BF16), jax.ShapeDtypeStruct((MOD_ROWS, 1, D), F32)],
                [row(D), const((1, 1)), const((1, D)), row(D),
                 pl.BlockSpec((None, 1, D), lambda i, j, k: (cfg.mod_row(i), 0, 0))],
                extras=[x1, mod3, loss_target.reshape(cfg.nl, D), small["final_g"][None]],
                extra_specs=[row(D), mod3_spec, row(D), const((1, D))], epi=loss_epi)
        saved.append(dict(mod3=mod3, x0=xs, h1=h1, qkv=qkv, cpin=cpin, mix=mix, yconv=yconv, lse=lse, y1=y1, x1=x1,
                          h2=h2, gu=gu, act=act, y2=y2, prm=prm))
        xs = xs_next

    sg = {k: [None, None] for k in ("norm1_g", "norm2_g", "conv_dw", "conv_dw_b", "conv_ln_g", "conv_ln_b",
                                    "attn_sink", "pool_w", "pool_scale")}
    dms = [None, None]

    def swiglu_bwd_epi(acc, ex, outs):
        outs[0][0] = (acc * ex[0][0].astype(F32)).astype(BF16)
        outs[0][1] = (acc * ex[0][1].astype(F32)).astype(BF16)

    def halves_epi(acc, ex, outs):
        h = acc.shape[0] // 2
        outs[0][0] = acc[:h].astype(BF16)
        outs[0][1] = acc[h:].astype(BF16)

    def row_shards_epi(n):
        def epi(acc, ex, outs):
            s0 = acc.shape[0] // n
            h = s0 // 2
            for t in range(n):
                for half in range(2):
                    outs[0][half, t] = acc[t * s0 + half * h:t * s0 + (half + 1) * h].astype(BF16)
        return epi

    def col_shards_epi(acc, ex, outs):
        h = acc.shape[0] // 2
        for j in range(N_SHARD):
            for half in range(2):
                outs[0][half, j] = acc[half * h:(half + 1) * h, j * IN_SHARD:(j + 1) * IN_SHARD].astype(BF16)

    for l in (1, 0):
        sv = saved[l]
        mod3 = sv["mod3"]
        nb = nb_lat if l == last else nb_all
        tr = _dw_rows(nb * TM)
        steps = nb * TM // tr
        wl = wf[l]
        gu_spec = pl.BlockSpec((2, TM, HALF_FF), lambda j, i, k: (0, i, j))
        df = _mm(f"ffn_out_bwd{l}", "nt", (2, nb, 1), dy2, wl["w_ffn_out"].reshape(1, D_FF, D),
                 pl.BlockSpec((TM, D), lambda j, i, k: (i, 0)), pl.BlockSpec((None, HALF_FF, D), lambda j, i, k: (0, j, 0)),
                 [jax.ShapeDtypeStruct((2, cfg.r, D_FF), BF16)], [gu_spec], extras=[sv["gu"]], extra_specs=[gu_spec],
                 epi=swiglu_bwd_epi)[0]
        big[l]["w_ffn_out"] = _mm_dw(
            f"dw_ffn_out{l}", sv["act"], dy2, pl.BlockSpec((tr, HALF_FF), lambda i, j, k: (k, j)),
            pl.BlockSpec((tr, D), lambda i, j, k: (k, 0)), (2, N_SHARD, D_FF // 8, D),
            pl.BlockSpec((2, 2, D_FF // 8, D), lambda i, j, k: (0, j, 0, 0)), 2, (HALF_FF, D), steps, row_shards_epi(2))[0]
        ride = comm.ride(f"dw_ffn_in{l}")
        res = _mm_dw(f"dw_ffn_in{l}", sv["h2"], df, pl.BlockSpec((tr, D), lambda i, j, k: (k, 0)),
                     pl.BlockSpec((None, tr, HALF_FF), lambda i, j, k: (j // 2, k, j % 2)), (2, N_SHARD, D // 2, HALF_FF),
                     pl.BlockSpec((2, None, D // 2, HALF_FF), lambda i, j, k: (0, j, 0, 0)), N_SHARD, (D, HALF_FF), steps,
                     halves_epi, ride=ride)
        res, got = res if ride is not None else (res, [])
        big[l]["w_ffn_in"] = res[0]
        comm.landed(f"dw_ffn_in{l}", got)
        dx1, dsh2, dsc2, dn2, dy1, dg1 = _proj_norm_bwd(
            f"ffn_in_bwd{l}", df, pl.BlockSpec((2, TM, D_FF), lambda i: (0, i, 0)),
            lambda a_ref, j: a_ref[j // 2, :, (j % 2) * HALF_FF:(j % 2 + 1) * HALF_FF], wl["w_ffn_in"],
            sv["x1"], dx, small["norm2_g"][l][None], mod3, 4, nb, False, (sv["y1"], mod3, 2), cfg)
        dmix = _mm(f"out_proj_bwd{l}", "nt", (nb, 1, 1), dy1, wl["w_out"].reshape(1, D, D), row(D), whole(D),
                   [jax.ShapeDtypeStruct((cfg.r, D), BF16)], [row(D)])[0]
        big[l]["w_out"] = _mm_dw(
            f"dw_out{l}", sv["mix"], dy1, pl.BlockSpec((tr, D), lambda i, j, k: (k, 0)),
            pl.BlockSpec((tr, D), lambda i, j, k: (k, 0)), (2, N_SHARD, D // 8, D),
            pl.BlockSpec((2, N_SHARD, D // 8, D), lambda i, j, k: (0, 0, 0, 0)), 1, (D, D), steps, row_shards_epi(N_SHARD))[0]
        du, dkvl, dkvc, dsink = hosted(f"att_bwd{l}", lambda ride: _att_bwd(
            f"att_bwd{l}", sv["qkv"], sv["mix"], dmix, sv["lse"], small["attn_sink"][l], cos_t, sin_t, l != last, cfg, ride))
        acc = [du] + [jnp.zeros(sh, F32) for sh in _SMALL_SHAPES]
        acc = _convpool_bwd(f"convpool_bwd_lat{l}", sv["cpin"], sv["yconv"], dmix, sv["prm"], acc, cfg.seq, 0, cfg)
        if l != last:
            acc = _convpool_bwd(f"convpool_bwd_ctx{l}", sv["cpin"], sv["yconv"], dmix, sv["prm"], acc, cfg.ctx,
                                cfg.nl // cfg.ctx, cfg)
        du, g_dw, g_dwb, g_lng, g_lnb, g_wbd, g_ps = acc
        du = _place_kv(f"place_kv{l}", du, dkvl, dkvc, l != last, cfg)
        if l == last:
            du = _place_ctx_kv_only(f"place_ctx_kv{l}", du, dkvc, cfg)
        sg["attn_sink"][l] = dsink[:, 0]
        sg["conv_dw"][l], sg["conv_dw_b"][l], sg["conv_ln_g"][l], sg["conv_ln_b"][l] = g_dw[:CONV_K], g_dwb[0], g_lng[0], g_lnb[0]
        sg["pool_w"][l] = jnp.stack([g_wbd[g * 64:(g + 1) * 64, g * 64:(g + 1) * 64] for g in range(4)])
        sg["pool_scale"][l] = g_ps[0]
        tr_all = _dw_rows(cfg.r)
        big[l]["w_in"] = _mm_dw(
            f"dw_in{l}", sv["h1"], du, pl.BlockSpec((tr_all, D), lambda i, j, k: (k, 0)),
            pl.BlockSpec((tr_all, IN_W), lambda i, j, k: (k, 0)), (2, N_SHARD, D // 2, IN_SHARD),
            pl.BlockSpec((2, N_SHARD, D // 2, IN_SHARD), lambda i, j, k: (0, 0, 0, 0)), 1, (D, IN_W), cfg.r // tr_all,
            col_shards_epi)[0]
        below = (saved[l - 1]["y2"], saved[l - 1]["mod3"], 5) if l > 0 else None
        res = _proj_norm_bwd(
            f"in_proj_bwd{l}", du, pl.BlockSpec((TM, IN_W), lambda i: (i, 0)),
            lambda a_ref, j: a_ref[:, j * IN_SHARD:(j + 1) * IN_SHARD], wl["w_in"],
            sv["x0"], dx1, small["norm1_g"][l][None], mod3, 1, nb_all, l == last, below, cfg, dx_latent_only=l == 0)
        dx, dsh1, dsc1, dn1 = res[:4]
        sg["norm1_g"][l], sg["norm2_g"][l] = dn1[0], dn2[0]
        parts = [dsh1, dsc1, dg1, dsh2, dsc2, dg2]
        if below is not None:
            dy2, dg2 = res[4:]
        dm = jnp.concatenate([t[:cfg.b, 0, :] for t in parts], axis=1)
        live = (0, 1) if l == last else range(6)
        dm_ctx = jnp.concatenate([t[cfg.b, 0, :] if i in live else jnp.zeros((D,), F32) for i, t in enumerate(parts)])
        dms[l] = jnp.concatenate([dm, dm_ctx[None, :], jnp.zeros((MOD_ROWS - cfg.b - 1, 6 * D), F32)], axis=0)

    grad_x = dx.reshape(x.shape)
    small_grads = {k: jnp.stack(v) for k, v in sg.items()}
    small_grads["final_g"] = d_final_g[0]
    return loss, grad_x, small_grads, dms


_BIG = ("w_in", "w_out", "w_ffn_in", "w_ffn_out")
_TAPS = "conv_dw"
_SMALL = ("c_ctx", "b_mod", "norm1_g", "norm2_g", "conv_dw", "conv_dw_b", "conv_ln_g", "conv_ln_b", "attn_sink",
          "pool_w", "pool_scale", "final_g")
_ORDER = ("c_ctx", "w_mod", "b_mod", "norm1_g", "norm2_g", "w_in", "conv_dw", "conv_dw_b", "conv_ln_g", "conv_ln_b",
          "attn_sink", "pool_w", "pool_scale", "w_out", "w_ffn_in", "w_ffn_out", "final_g")
_GATHER_HOSTS = {
    "first": ((0, "w_in"),),
    "att_fwd0": ((0, "w_out"), (0, "w_ffn_in"), (0, "w_ffn_out"), (0, _TAPS)),
    "ffn_in0": tuple((1, k) for k in _BIG + (_TAPS,)),
}
_REDUCE_HOSTS = {
    "dw_ffn_in0": tuple((1, k) for k in _BIG),
    "att_bwd0": ((0, "w_ffn_in"), (0, "w_ffn_out"), (0, "w_out")),
    "last": ((0, "w_in"),),
}


class _Comm:
    def __init__(self, w, c_idx, jc_idx):
        self.shapes = {k: w[k].shape[1:] for k in _BIG}
        self.c_idx, self.jc_idx = c_idx, jc_idx
        halves = lambda a: a.reshape(2, a.shape[0] // 2, a.shape[1])
        taps = jnp.pad(w[_TAPS], ((0, 0), (0, 1), (0, 64)))
        cast = {k: w[k].astype(BF16) for k in _BIG}
        self.shards = [{**{k: halves(cast[k][l]) for k in _BIG}, _TAPS: halves(taps[l])} for l in range(2)]
        self.wf = [dict(), dict()]
        self.grads = [dict(), dict()]
        self.reduced = [dict(), dict()]
        self._open = {}
        self.landed("first", _run_ride("gather_first", self.ride("first")))

    def ride(self, host):
        if host in _GATHER_HOSTS:
            return _gather_ride([self.shards[layer][k] for layer, k in _GATHER_HOSTS[host]])
        if host in _REDUCE_HOSTS:
            what = _REDUCE_HOSTS[host]
            mine = [self.grads[layer][k] for layer, k in what]
            other = _send_other_half(f"send_other_half_{host}", mine)
            pair = [_pair_add(f"pair_add{layer}_{k}", a, b, self.c_idx) for (layer, k), a, b in zip(what, mine, other)]
            self._open[host] = pair
            return _exchange_ride(pair)
        return None

    def landed(self, host, got):
        if host in _GATHER_HOSTS:
            for (layer, k), f in zip(_GATHER_HOSTS[host], got):
                if k == _TAPS:
                    taps = f.reshape(N_SHARD, 32, 128)[:, :CONV_K, :64]
                    self.wf[layer][k] = jnp.transpose(taps, (1, 0, 2)).reshape(CONV_K, CONV_W)
                else:
                    self.wf[layer][k] = f.reshape((1, N_SHARD) + self.shapes[k])
        if host in _REDUCE_HOSTS:
            what = _REDUCE_HOSTS[host]
            mine = [_shard_sum(f"shard_sum{layer}_{k}", a, b, self.jc_idx)
                    for (layer, k), a, b in zip(what, self._open.pop(host), got)]
            for (layer, k), g in zip(what, _swap_reduced(f"swap_reduced_{host}", mine)):
                self.reduced[layer][k] = g.reshape(self.shapes[k])


def _conditioning(c, c_ctx, w_mod, b_mod, chip):
    b = c.shape[0]
    block = jnp.concatenate([c, c_ctx[None, :], jnp.zeros((8 - b - 1, D), F32)], axis=0)
    c_all = _share_small("share_c", block, False).reshape(64, D)
    bias = lax.dynamic_slice_in_dim(b_mod, chip * MOD_W, MOD_W, axis=1)
    full = lambda r, q: pl.BlockSpec((r, q), lambda i, j, k: (0, 0))

    def bias_epi(acc, ex, outs):
        outs[0][...] = acc + ex[0][...]

    mv = [_mm(f"mod_fwd{l}", "nn", (1, 1, 1), c_all, w_mod[l], full(64, D), full(D, MOD_W),
              [jax.ShapeDtypeStruct((64, MOD_W), F32)], [full(64, MOD_W)], extras=[bias[l][None]],
              extra_specs=[full(1, MOD_W)], a_fn=_silu, epi=bias_epi)[0] for l in range(2)]
    by_dev = jnp.transpose(jnp.stack(mv).reshape(2, 8, 8, MOD_W), (1, 0, 2, 3))
    rows = jnp.transpose(_mod_rows_exchange(by_dev), (1, 2, 0, 3)).reshape(2, 8, 6 * D)
    rows = jnp.pad(rows, ((0, 0), (0, MOD_ROWS - 8), (0, 0)))
    return [rows[l].reshape(MOD_ROWS, 6, D) for l in range(2)], c_all


def _conditioning_bwd(dms, c_all, w_mod, b):
    dm = jnp.stack([d[:8] for d in dms])
    by_chip = jnp.transpose(dm.reshape(2, 8, N_SHARD, MOD_W), (2, 0, 1, 3))
    gathered, d_bias = _mod_grad_exchange(by_chip, dm)
    dm_all = jnp.transpose(gathered, (1, 0, 2, 3)).reshape(2, 64, MOD_W)
    full = lambda r, q: pl.BlockSpec((r, q), lambda i, j, k: (0, 0))

    def ctx_rows_epi(acc, ex, outs):
        row = lax.broadcasted_iota(jnp.int32, acc.shape, 0) % 8
        outs[0][...] = _colsum(jnp.where(row == b, acc * _dsilu(ex[0][...]), 0.0))

    g_mod, d_ctx = [], jnp.zeros((D,), F32)
    for l in range(2):
        g_mod.append(_mm(f"dw_mod{l}", "tn", (1, 1, 1), c_all, dm_all[l], full(64, D), full(64, MOD_W),
                         [jax.ShapeDtypeStruct((D, MOD_W), F32)], [full(D, MOD_W)], a_fn=_silu)[0])
        part = _mm(f"mod_bwd{l}", "nt", (1, 1, 1), dm_all[l], w_mod[l], full(64, MOD_W), full(D, MOD_W),
                   [jax.ShapeDtypeStruct((1, D), F32)], [full(1, D)], extras=[c_all], extra_specs=[full(64, D)],
                   epi=ctx_rows_epi)[0]
        d_ctx = d_ctx + part[0]
    return g_mod, d_bias[:, 0, :], d_ctx


def kernel(x, c, ctx, c_ctx, w_mod, b_mod, norm1_g, norm2_g, w_in, conv_dw, conv_dw_b, conv_ln_g, conv_ln_b, attn_sink, pool_w, pool_scale, w_out, w_ffn_in, w_ffn_out, final_g, loss_target, m_c_ctx, m_w_mod, m_b_mod, m_norm1_g, m_norm2_g, m_w_in, m_conv_dw, m_conv_dw_b, m_conv_ln_g, m_conv_ln_b, m_attn_sink, m_pool_w, m_pool_scale, m_w_out, m_w_ffn_in, m_w_ffn_out, m_final_g, v_c_ctx, v_w_mod, v_b_mod, v_norm1_g, v_norm2_g, v_w_in, v_conv_dw, v_conv_dw_b, v_conv_ln_g, v_conv_ln_b, v_attn_sink, v_pool_w, v_pool_scale, v_w_out, v_w_ffn_in, v_w_ffn_out, v_final_g):
    w = dict(c_ctx=c_ctx, w_mod=w_mod, b_mod=b_mod, norm1_g=norm1_g, norm2_g=norm2_g, w_in=w_in, conv_dw=conv_dw,
             conv_dw_b=conv_dw_b, conv_ln_g=conv_ln_g, conv_ln_b=conv_ln_b, attn_sink=attn_sink, pool_w=pool_w,
             pool_scale=pool_scale, w_out=w_out, w_ffn_in=w_ffn_in, w_ffn_out=w_ffn_out, final_g=final_g)
    m = dict(c_ctx=m_c_ctx, w_mod=m_w_mod, b_mod=m_b_mod, norm1_g=m_norm1_g, norm2_g=m_norm2_g, w_in=m_w_in,
             conv_dw=m_conv_dw, conv_dw_b=m_conv_dw_b, conv_ln_g=m_conv_ln_g, conv_ln_b=m_conv_ln_b,
             attn_sink=m_attn_sink, pool_w=m_pool_w, pool_scale=m_pool_scale, w_out=m_w_out, w_ffn_in=m_w_ffn_in,
             w_ffn_out=m_w_ffn_out, final_g=m_final_g)
    v = dict(c_ctx=v_c_ctx, w_mod=v_w_mod, b_mod=v_b_mod, norm1_g=v_norm1_g, norm2_g=v_norm2_g, w_in=v_w_in,
             conv_dw=v_conv_dw, conv_dw_b=v_conv_dw_b, conv_ln_g=v_conv_ln_g, conv_ln_b=v_conv_ln_b,
             attn_sink=v_attn_sink, pool_w=v_pool_w, pool_scale=v_pool_scale, w_out=v_w_out, w_ffn_in=v_w_ffn_in,
             w_ffn_out=v_w_ffn_out, final_g=v_final_g)
    xi, yi, ci = _position()
    chip = 2 * xi + yi
    mod3s, c_all = _conditioning(c, c_ctx, w_mod, b_mod, chip)
    comm = _Comm(w, jnp.reshape(ci, (1,)).astype(jnp.int32), jnp.stack([chip, ci]).astype(jnp.int32))
    small = {k: w[k] for k in _SMALL if k not in ("conv_dw", "c_ctx", "b_mod")}
    loss, grad_x, sgrads, dms = _local_step(x, ctx, small, mod3s, loss_target, comm)
    comm.landed("last", _run_ride("exchange_last", comm.ride("last")))
    g_mod, sgrads["b_mod"], d_ctx = _conditioning_bwd(dms, c_all, w_mod, c.shape[0])
    sgrads["c_ctx"] = 0.5 * d_ctx

    names = list(_SMALL)
    total = _share_small("sum_small", _pack([loss] + [sgrads[k] for k in names]), True)
    parts = _unpack(total, [()] + [sgrads[k].shape for k in names])
    loss_out = parts[0]
    gsmall = dict(zip(names, parts[1:]))
    gsmall["conv_dw"] = lax.dynamic_slice_in_dim(gsmall["conv_dw"], chip * 64, 64, axis=2)

    grads, delta, new_m, new_v = dict(gsmall), {}, {}, {}
    for k in ("w_mod",) + _BIG:
        s0, s1 = w[k].shape[1:]
        flat = lambda a: a.reshape(2 * s0, s1)
        g_layers = g_mod if k == "w_mod" else [comm.reduced[l][k] for l in range(2)]
        outs = _adamw_layers(f"adamw_{k}", flat(w[k]), g_layers, flat(m[k]), flat(v[k]))
        grads[k], delta[k], new_m[k], new_v[k] = [a.reshape(w[k].shape) for a in outs]
    d_, m_, v_ = _adamw("adamw_small", _pack([w[k] for k in names]), _pack([gsmall[k] for k in names]),
                        _pack([m[k] for k in names]), _pack([v[k] for k in names]))
    sshapes = [w[k].shape for k in names]
    for k, a, b, e in zip(names, _unpack(d_, sshapes), _unpack(m_, sshapes), _unpack(v_, sshapes)):
        delta[k], new_m[k], new_v[k] = a, b, e
    return (loss_out, grad_x, *[grads[k] for k in _ORDER], *[delta[k] for k in _ORDER],
            *[new_m[k] for k in _ORDER], *[new_v[k] for k in _ORDER])
```
